```python
import math
import jax, jax.numpy as jnp
from jax import lax
import numpy as np

D_MODEL = 2048
BATCH = 8
SEQ = 2048
DEPTH = 4

CHUNK = 64
N_MIXERS = 2
EPS = 1e-6

GLA_HEADS = 4
GLA_KEY_WIDTH = D_MODEL // 2
GLA_VAL_WIDTH = D_MODEL
GLA_DK = GLA_KEY_WIDTH // GLA_HEADS
GLA_DV = GLA_VAL_WIDTH // GLA_HEADS
GLA_GATE_RANK = 16
GLA_GATE_TEMP = 16.0
GLA_IN_WIDTH = 2 * GLA_KEY_WIDTH + 2 * GLA_VAL_WIDTH + GLA_GATE_RANK

S5_WIDTH = D_MODEL // 2
S5_GROUP = 16
S5_GROUPS = S5_WIDTH // S5_GROUP
S5_STATE = 64
S5_DT_MIN = 1e-3
S5_DT_MAX = 1e-1
S5_EIG_CLIP = -1e-4

MLP_HIDDEN = 4 * D_MODEL

kernel_name = "hybrid_gla_s5_stream_block"


def _rmsnorm(x, g):
    xf = x.astype(jnp.float32)
    y = xf * lax.rsqrt(jnp.mean(xf * xf, axis=-1, keepdims=True) + EPS)
    return (y * g.astype(jnp.float32)).astype(x.dtype)


def _gla_mixer(h, w_in, w_gate_up, b_gate, o_norm, w_out):
    bsz, seq, _ = h.shape
    nc = seq // CHUNK
    proj = h @ w_in
    q, k, v, r, g_low = jnp.split(
        proj,
        [GLA_KEY_WIDTH, 2 * GLA_KEY_WIDTH, 2 * GLA_KEY_WIDTH + GLA_VAL_WIDTH,
         2 * GLA_KEY_WIDTH + 2 * GLA_VAL_WIDTH], axis=-1)
    log_a = jax.nn.log_sigmoid((g_low @ w_gate_up + b_gate).astype(jnp.float32)) / GLA_GATE_TEMP

    def to_chunks(t, dh):
        return t.reshape(bsz, nc, CHUNK, GLA_HEADS, dh).transpose(1, 0, 3, 2, 4).astype(jnp.float32)

    qc = to_chunks(q, GLA_DK) * (GLA_DK ** -0.5)
    kc = to_chunks(k, GLA_DK)
    vc = to_chunks(v, GLA_DV)
    ac = to_chunks(log_a, GLA_DK)
    cum = jnp.cumsum(ac, axis=3)
    total = cum[:, :, :, -1:, :]
    k_dec = kc * jnp.exp(total - cum)
    chunk_decay = jnp.exp(total[:, :, :, 0, :])

    def step(state, xs):
        q_c, k_c, v_c, d_c = xs
        state = d_c[..., None] * state + jnp.einsum('bhck,bhcv->bhkv', k_c, v_c)
        return state, jnp.einsum('bhck,bhkv->bhcv', q_c, state)

    s0 = jnp.zeros((bsz, GLA_HEADS, GLA_DK, GLA_DV), jnp.float32)
    _, o = lax.scan(step, s0, (qc, k_dec, vc, chunk_decay))
    o = o * lax.rsqrt(jnp.mean(o * o, axis=-1, keepdims=True) + EPS) * o_norm.astype(jnp.float32)
    o = o.transpose(1, 0, 3, 2, 4).reshape(bsz, seq, GLA_VAL_WIDTH).astype(h.dtype)
    return (o * jax.nn.silu(r)) @ w_out


def _s5_mixer(h, w_in, lam_re, lam_im, log_dt, b_re, b_im, c_re, c_im, d_skip, w_out):
    bsz, seq, _ = h.shape
    u = (h @ w_in).astype(jnp.float32)
    ug = u.reshape(bsz, seq, S5_GROUPS, S5_GROUP)
    lr = jnp.minimum(lam_re.astype(jnp.float32), S5_EIG_CLIP)
    li = lam_im.astype(jnp.float32)
    dt = jnp.exp(log_dt.astype(jnp.float32))[:, None]
    mag = jnp.exp(lr * dt)
    ang = li * dt
    ab_re = mag * jnp.cos(ang)
    ab_im = mag * jnp.sin(ang)
    den = lr * lr + li * li
    nr = ab_re - 1.0
    f_re = (nr * lr + ab_im * li) / den
    f_im = (ab_im * lr - nr * li) / den
    br = b_re.astype(jnp.float32)
    bi = b_im.astype(jnp.float32)
    bb_re = f_re[..., None] * br - f_im[..., None] * bi
    bb_im = f_re[..., None] * bi + f_im[..., None] * br
    bu_re = jnp.einsum('gnc,blgc->blgn', bb_re, ug)
    bu_im = jnp.einsum('gnc,blgc->blgn', bb_im, ug)
    a_re = jnp.broadcast_to(ab_re, bu_re.shape)
    a_im = jnp.broadcast_to(ab_im, bu_im.shape)

    def combine(e1, e2):
        a1r, a1i, b1r, b1i = e1
        a2r, a2i, b2r, b2i = e2
        return (a2r * a1r - a2i * a1i,
                a2r * a1i + a2i * a1r,
                a2r * b1r - a2i * b1i + b2r,
                a2r * b1i + a2i * b1r + b2i)

    _, _, x_re, x_im = lax.associative_scan(combine, (a_re, a_im, bu_re, bu_im), axis=1)
    y = (jnp.einsum('gcn,blgn->blgc', c_re.astype(jnp.float32), x_re)
         - jnp.einsum('gcn,blgn->blgc', c_im.astype(jnp.float32), x_im))
    y = y.reshape(bsz, seq, S5_WIDTH) + d_skip.astype(jnp.float32) * u
    y = jax.nn.gelu(y).astype(h.dtype)
    val, gate = jnp.split(y @ w_out, 2, axis=-1)
    return val * jax.nn.sigmoid(gate)


def _sq_relu_mlp(h, w_up, w_down):
    a = jax.nn.relu(h @ w_up)
    return (a * a) @ w_down


def _fwd_setup_inputs(seed: int = 0) -> dict:
    key = jax.random.key(seed)
    ks = jax.random.split(key, 24)
    n_gla = len(range(0, DEPTH, N_MIXERS))
    n_s5 = len(range(1, DEPTH, N_MIXERS))
    res_scale = (2 * DEPTH) ** -0.5

    def nrm(k, shape, scale):
        return jax.random.normal(k, shape, jnp.float32) * scale

    def gain(k, shape):
        return 1.0 + 0.02 * jax.random.normal(k, shape, jnp.float32)

    lam_im0 = math.pi * jnp.arange(S5_STATE, dtype=jnp.float32)
    return {
        'x': jax.random.normal(ks[0], (BATCH, SEQ, D_MODEL), jnp.float32),
        'gla_norm': gain(ks[1], (n_gla, D_MODEL)),
        'gla_w_in': nrm(ks[2], (n_gla, D_MODEL, GLA_IN_WIDTH), D_MODEL ** -0.5),
        'gla_w_gate_up': nrm(ks[3], (n_gla, GLA_GATE_RANK, GLA_KEY_WIDTH), GLA_GATE_RANK ** -0.5),
        'gla_b_gate': 1.0 + 0.1 * jax.random.normal(ks[4], (n_gla, GLA_KEY_WIDTH), jnp.float32),
        'gla_o_norm': gain(ks[5], (n_gla, GLA_DV)),
        'gla_w_out': nrm(ks[6], (n_gla, GLA_VAL_WIDTH, D_MODEL), GLA_VAL_WIDTH ** -0.5 * res_scale),
        's5_norm': gain(ks[7], (n_s5, D_MODEL)),
        's5_w_in': nrm(ks[8], (n_s5, D_MODEL, S5_WIDTH), D_MODEL ** -0.5),
        's5_lam_re': -0.5 + 0.01 * jax.random.normal(ks[9], (n_s5, S5_GROUPS, S5_STATE), jnp.float32),
        's5_lam_im': lam_im0 + 0.01 * jax.random.normal(ks[10], (n_s5, S5_GROUPS, S5_STATE), jnp.float32),
        's5_log_dt': jax.random.uniform(ks[11], (n_s5, S5_GROUPS), jnp.float32,
                                        minval=math.log(S5_DT_MIN), maxval=math.log(S5_DT_MAX)),
        's5_b_re': nrm(ks[12], (n_s5, S5_GROUPS, S5_STATE, S5_GROUP), (2 * S5_GROUP) ** -0.5),
        's5_b_im': nrm(ks[13], (n_s5, S5_GROUPS, S5_STATE, S5_GROUP), (2 * S5_GROUP) ** -0.5),
        's5_c_re': nrm(ks[14], (n_s5, S5_GROUPS, S5_GROUP, S5_STATE), (2 * S5_STATE) ** -0.5),
        's5_c_im': nrm(ks[15], (n_s5, S5_GROUPS, S5_GROUP, S5_STATE), (2 * S5_STATE) ** -0.5),
        's5_d': jax.random.normal(ks[16], (n_s5, S5_WIDTH), jnp.float32),
        's5_w_out': nrm(ks[17], (n_s5, S5_WIDTH, 2 * D_MODEL), S5_WIDTH ** -0.5 * res_scale),
        'mlp_norm': gain(ks[18], (DEPTH, D_MODEL)),
        'mlp_w_up': nrm(ks[19], (DEPTH, D_MODEL, MLP_HIDDEN), D_MODEL ** -0.5),
        'mlp_w_down': nrm(ks[20], (DEPTH, MLP_HIDDEN, D_MODEL), MLP_HIDDEN ** -0.5 * res_scale),
        'final_norm': gain(ks[21], (D_MODEL,)),
    }


def _fwd_reference(x, gla_norm, gla_w_in, gla_w_gate_up, gla_b_gate, gla_o_norm, gla_w_out,
              s5_norm, s5_w_in, s5_lam_re, s5_lam_im, s5_log_dt, s5_b_re, s5_b_im,
              s5_c_re, s5_c_im, s5_d, s5_w_out, mlp_norm, mlp_w_up, mlp_w_down, final_norm):
    h = x
    for i in range(DEPTH):
        j = i // N_MIXERS
        if i % N_MIXERS == 0:
            h = h + _gla_mixer(_rmsnorm(h, gla_norm[j]), gla_w_in[j], gla_w_gate_up[j],
                               gla_b_gate[j], gla_o_norm[j], gla_w_out[j])
        else:
            h = h + _s5_mixer(_rmsnorm(h, s5_norm[j]), s5_w_in[j], s5_lam_re[j], s5_lam_im[j],
                              s5_log_dt[j], s5_b_re[j], s5_b_im[j], s5_c_re[j], s5_c_im[j],
                              s5_d[j], s5_w_out[j])
        h = h + _sq_relu_mlp(_rmsnorm(h, mlp_norm[i]), mlp_w_up[i], mlp_w_down[i])
    return _rmsnorm(h, final_norm)


import jax as _jax
import jax.numpy as _jnp

TWIN_FORMAT = 'train_step'
FWD_PARAMS = ['x', 'gla_norm', 'gla_w_in', 'gla_w_gate_up', 'gla_b_gate', 'gla_o_norm', 'gla_w_out', 's5_norm', 's5_w_in', 's5_lam_re', 's5_lam_im', 's5_log_dt', 's5_b_re', 's5_b_im', 's5_c_re', 's5_c_im', 's5_d', 's5_w_out', 'mlp_norm', 'mlp_w_up', 'mlp_w_down', 'final_norm']
TWIN_WEIGHTS = ['gla_norm', 'gla_w_in', 'gla_w_gate_up', 'gla_b_gate', 'gla_o_norm', 'gla_w_out', 's5_norm', 's5_w_in', 's5_lam_re', 's5_lam_im', 's5_log_dt', 's5_b_re', 's5_b_im', 's5_c_re', 's5_c_im', 's5_d', 's5_w_out', 'mlp_norm', 'mlp_w_up', 'mlp_w_down', 'final_norm']
TWIN_DIFF_INPUT = 'x'
TWIN_INPUTS = ['x', 'gla_norm', 'gla_w_in', 'gla_w_gate_up', 'gla_b_gate', 'gla_o_norm', 'gla_w_out', 's5_norm', 's5_w_in', 's5_lam_re', 's5_lam_im', 's5_log_dt', 's5_b_re', 's5_b_im', 's5_c_re', 's5_c_im', 's5_d', 's5_w_out', 'mlp_norm', 'mlp_w_up', 'mlp_w_down', 'final_norm', 'loss_target', 'm_gla_norm', 'm_gla_w_in', 'm_gla_w_gate_up', 'm_gla_b_gate', 'm_gla_o_norm', 'm_gla_w_out', 'm_s5_norm', 'm_s5_w_in', 'm_s5_lam_re', 'm_s5_lam_im', 'm_s5_log_dt', 'm_s5_b_re', 'm_s5_b_im', 'm_s5_c_re', 'm_s5_c_im', 'm_s5_d', 'm_s5_w_out', 'm_mlp_norm', 'm_mlp_w_up', 'm_mlp_w_down', 'm_final_norm', 'v_gla_norm', 'v_gla_w_in', 'v_gla_w_gate_up', 'v_gla_b_gate', 'v_gla_o_norm', 'v_gla_w_out', 'v_s5_norm', 'v_s5_w_in', 'v_s5_lam_re', 'v_s5_lam_im', 'v_s5_log_dt', 'v_s5_b_re', 'v_s5_b_im', 'v_s5_c_re', 'v_s5_c_im', 'v_s5_d', 'v_s5_w_out', 'v_mlp_norm', 'v_mlp_w_up', 'v_mlp_w_down', 'v_final_norm']
TWIN_OUTPUTS = ['loss', 'grad_x', 'grad_gla_norm', 'grad_gla_w_in', 'grad_gla_w_gate_up', 'grad_gla_b_gate', 'grad_gla_o_norm', 'grad_gla_w_out', 'grad_s5_norm', 'grad_s5_w_in', 'grad_s5_lam_re', 'grad_s5_lam_im', 'grad_s5_log_dt', 'grad_s5_b_re', 'grad_s5_b_im', 'grad_s5_c_re', 'grad_s5_c_im', 'grad_s5_d', 'grad_s5_w_out', 'grad_mlp_norm', 'grad_mlp_w_up', 'grad_mlp_w_down', 'grad_final_norm', 'delta_gla_norm', 'delta_gla_w_in', 'delta_gla_w_gate_up', 'delta_gla_b_gate', 'delta_gla_o_norm', 'delta_gla_w_out', 'delta_s5_norm', 'delta_s5_w_in', 'delta_s5_lam_re', 'delta_s5_lam_im', 'delta_s5_log_dt', 'delta_s5_b_re', 'delta_s5_b_im', 'delta_s5_c_re', 'delta_s5_c_im', 'delta_s5_d', 'delta_s5_w_out', 'delta_mlp_norm', 'delta_mlp_w_up', 'delta_mlp_w_down', 'delta_final_norm', 'new_m_gla_norm', 'new_m_gla_w_in', 'new_m_gla_w_gate_up', 'new_m_gla_b_gate', 'new_m_gla_o_norm', 'new_m_gla_w_out', 'new_m_s5_norm', 'new_m_s5_w_in', 'new_m_s5_lam_re', 'new_m_s5_lam_im', 'new_m_s5_log_dt', 'new_m_s5_b_re', 'new_m_s5_b_im', 'new_m_s5_c_re', 'new_m_s5_c_im', 'new_m_s5_d', 'new_m_s5_w_out', 'new_m_mlp_norm', 'new_m_mlp_w_up', 'new_m_mlp_w_down', 'new_m_final_norm', 'new_v_gla_norm', 'new_v_gla_w_in', 'new_v_gla_w_gate_up', 'new_v_gla_b_gate', 'new_v_gla_o_norm', 'new_v_gla_w_out', 'new_v_s5_norm', 'new_v_s5_w_in', 'new_v_s5_lam_re', 'new_v_s5_lam_im', 'new_v_s5_log_dt', 'new_v_s5_b_re', 'new_v_s5_b_im', 'new_v_s5_c_re', 'new_v_s5_c_im', 'new_v_s5_d', 'new_v_s5_w_out', 'new_v_mlp_norm', 'new_v_mlp_w_up', 'new_v_mlp_w_down', 'new_v_final_norm']
TWIN_LEAF_KINDS = {'loss': 'loss', 'grad_x': 'grad_x', 'grad_gla_norm': 'grad_w', 'grad_gla_w_in': 'grad_w', 'grad_gla_w_gate_up': 'grad_w', 'grad_gla_b_gate': 'grad_w', 'grad_gla_o_norm': 'grad_w', 'grad_gla_w_out': 'grad_w', 'grad_s5_norm': 'grad_w', 'grad_s5_w_in': 'grad_w', 'grad_s5_lam_re': 'grad_w', 'grad_s5_lam_im': 'grad_w', 'grad_s5_log_dt': 'grad_w', 'grad_s5_b_re': 'grad_w', 'grad_s5_b_im': 'grad_w', 'grad_s5_c_re': 'grad_w', 'grad_s5_c_im': 'grad_w', 'grad_s5_d': 'grad_w', 'grad_s5_w_out': 'grad_w', 'grad_mlp_norm': 'grad_w', 'grad_mlp_w_up': 'grad_w', 'grad_mlp_w_down': 'grad_w', 'grad_final_norm': 'grad_w', 'delta_gla_norm': 'delta_w', 'delta_gla_w_in': 'delta_w', 'delta_gla_w_gate_up': 'delta_w', 'delta_gla_b_gate': 'delta_w', 'delta_gla_o_norm': 'delta_w', 'delta_gla_w_out': 'delta_w', 'delta_s5_norm': 'delta_w', 'delta_s5_w_in': 'delta_w', 'delta_s5_lam_re': 'delta_w', 'delta_s5_lam_im': 'delta_w', 'delta_s5_log_dt': 'delta_w', 'delta_s5_b_re': 'delta_w', 'delta_s5_b_im': 'delta_w', 'delta_s5_c_re': 'delta_w', 'delta_s5_c_im': 'delta_w', 'delta_s5_d': 'delta_w', 'delta_s5_w_out': 'delta_w', 'delta_mlp_norm': 'delta_w', 'delta_mlp_w_up': 'delta_w', 'delta_mlp_w_down': 'delta_w', 'delta_final_norm': 'delta_w', 'new_m_gla_norm': 'new_m', 'new_m_gla_w_in': 'new_m', 'new_m_gla_w_gate_up': 'new_m', 'new_m_gla_b_gate': 'new_m', 'new_m_gla_o_norm': 'new_m', 'new_m_gla_w_out': 'new_m', 'new_m_s5_norm': 'new_m', 'new_m_s5_w_in': 'new_m', 'new_m_s5_lam_re': 'new_m', 'new_m_s5_lam_im': 'new_m', 'new_m_s5_log_dt': 'new_m', 'new_m_s5_b_re': 'new_m', 'new_m_s5_b_im': 'new_m', 'new_m_s5_c_re': 'new_m', 'new_m_s5_c_im': 'new_m', 'new_m_s5_d': 'new_m', 'new_m_s5_w_out': 'new_m', 'new_m_mlp_norm': 'new_m', 'new_m_mlp_w_up': 'new_m', 'new_m_mlp_w_down': 'new_m', 'new_m_final_norm': 'new_m', 'new_v_gla_norm': 'new_v', 'new_v_gla_w_in': 'new_v', 'new_v_gla_w_gate_up': 'new_v', 'new_v_gla_b_gate': 'new_v', 'new_v_gla_o_norm': 'new_v', 'new_v_gla_w_out': 'new_v', 'new_v_s5_norm': 'new_v', 'new_v_s5_w_in': 'new_v', 'new_v_s5_lam_re': 'new_v', 'new_v_s5_lam_im': 'new_v', 'new_v_s5_log_dt': 'new_v', 'new_v_s5_b_re': 'new_v', 'new_v_s5_b_im': 'new_v', 'new_v_s5_c_re': 'new_v', 'new_v_s5_c_im': 'new_v', 'new_v_s5_d': 'new_v', 'new_v_s5_w_out': 'new_v', 'new_v_mlp_norm': 'new_v', 'new_v_mlp_w_up': 'new_v', 'new_v_mlp_w_down': 'new_v', 'new_v_final_norm': 'new_v'}


def _forward(args):
    return _fwd_reference(*[args[k] for k in FWD_PARAMS])


def _output_shape():
    out = _jax.eval_shape(lambda: _forward(_fwd_setup_inputs(0)))
    return out.shape, out.dtype

N_MICROBATCH = 1
ADAM_LR = 0.001
ADAM_B1 = 0.9
ADAM_B2 = 0.999
ADAM_EPS = 1e-08
ADAM_WD = 0.01
ADAM_STEP = 10
PER_EXAMPLE_BATCH_AXIS = {'x': 0, 'loss_target': 0}
SHARED_INPUTS = []
_WEIGHT_DTYPES = {'gla_norm': _jnp.float32, 'gla_w_in': _jnp.float32, 'gla_w_gate_up': _jnp.float32, 'gla_b_gate': _jnp.float32, 'gla_o_norm': _jnp.float32, 'gla_w_out': _jnp.float32, 's5_norm': _jnp.float32, 's5_w_in': _jnp.float32, 's5_lam_re': _jnp.float32, 's5_lam_im': _jnp.float32, 's5_log_dt': _jnp.float32, 's5_b_re': _jnp.float32, 's5_b_im': _jnp.float32, 's5_c_re': _jnp.float32, 's5_c_im': _jnp.float32, 's5_d': _jnp.float32, 's5_w_out': _jnp.float32, 'mlp_norm': _jnp.float32, 'mlp_w_up': _jnp.float32, 'mlp_w_down': _jnp.float32, 'final_norm': _jnp.float32}
MOMENT_SCALE = {'gla_norm': 2.682904e-02, 'gla_w_in': 1.540689e-02, 'gla_w_gate_up': 2.942353e-03, 'gla_b_gate': 1.105763e-02, 'gla_o_norm': 2.673375e-02, 'gla_w_out': 3.697166e-02, 's5_norm': 7.187976e-03, 's5_w_in': 9.777785e-03, 's5_lam_re': 5.627060e-04, 's5_lam_im': 6.298593e-04, 's5_log_dt': 2.746518e-01, 's5_b_re': 3.734184e-04, 's5_b_im': 3.530395e-04, 's5_c_re': 7.245008e-04, 's5_c_im': 7.024528e-04, 's5_d': 1.068114e-02, 's5_w_out': 1.452395e-02, 'mlp_norm': 2.729588e-02, 'mlp_w_up': 1.353214e-02, 'mlp_w_down': 7.082419e-02, 'final_norm': 8.049166e+00}


def _to_microbatches(a, axis):
    t = _jnp.moveaxis(a, axis, 0)
    t = t.reshape((N_MICROBATCH, t.shape[0] // N_MICROBATCH) + t.shape[1:])
    return _jnp.moveaxis(t, 1, axis + 1)


def setup_inputs(seed: int = 0) -> dict:
    inp = _fwd_setup_inputs(seed)
    key = _jax.random.fold_in(_jax.random.key(seed), 7919)
    shape, _ = _output_shape()
    out = dict(inp)
    out["loss_target"] = _jax.random.normal(_jax.random.fold_in(key, 0), shape, _jnp.float32)
    for i, name in enumerate(TWIN_WEIGHTS):
        w = inp[name].astype(_jnp.float32)
        if MOMENT_SCALE is None:
            s = _jnp.sqrt(_jnp.mean(_jnp.square(w)) + 1e-30)
        else:
            s = MOMENT_SCALE[name]
        km, kv = _jax.random.split(_jax.random.fold_in(key, i + 1))
        out[name] = w
        out["m_" + name] = s * _jax.random.normal(km, w.shape, _jnp.float32)
        out["v_" + name] = (s * s) * _jax.random.uniform(kv, w.shape, _jnp.float32, 0.5, 1.5)
    if N_MICROBATCH > 1:
        for name, axis in PER_EXAMPLE_BATCH_AXIS.items():
            out[name] = _to_microbatches(out[name], axis)
    return {'x': out['x'], 'gla_norm': out['gla_norm'], 'gla_w_in': out['gla_w_in'], 'gla_w_gate_up': out['gla_w_gate_up'], 'gla_b_gate': out['gla_b_gate'], 'gla_o_norm': out['gla_o_norm'], 'gla_w_out': out['gla_w_out'], 's5_norm': out['s5_norm'], 's5_w_in': out['s5_w_in'], 's5_lam_re': out['s5_lam_re'], 's5_lam_im': out['s5_lam_im'], 's5_log_dt': out['s5_log_dt'], 's5_b_re': out['s5_b_re'], 's5_b_im': out['s5_b_im'], 's5_c_re': out['s5_c_re'], 's5_c_im': out['s5_c_im'], 's5_d': out['s5_d'], 's5_w_out': out['s5_w_out'], 'mlp_norm': out['mlp_norm'], 'mlp_w_up': out['mlp_w_up'], 'mlp_w_down': out['mlp_w_down'], 'final_norm': out['final_norm'], 'loss_target': out['loss_target'], 'm_gla_norm': out['m_gla_norm'], 'm_gla_w_in': out['m_gla_w_in'], 'm_gla_w_gate_up': out['m_gla_w_gate_up'], 'm_gla_b_gate': out['m_gla_b_gate'], 'm_gla_o_norm': out['m_gla_o_norm'], 'm_gla_w_out': out['m_gla_w_out'], 'm_s5_norm': out['m_s5_norm'], 'm_s5_w_in': out['m_s5_w_in'], 'm_s5_lam_re': out['m_s5_lam_re'], 'm_s5_lam_im': out['m_s5_lam_im'], 'm_s5_log_dt': out['m_s5_log_dt'], 'm_s5_b_re': out['m_s5_b_re'], 'm_s5_b_im': out['m_s5_b_im'], 'm_s5_c_re': out['m_s5_c_re'], 'm_s5_c_im': out['m_s5_c_im'], 'm_s5_d': out['m_s5_d'], 'm_s5_w_out': out['m_s5_w_out'], 'm_mlp_norm': out['m_mlp_norm'], 'm_mlp_w_up': out['m_mlp_w_up'], 'm_mlp_w_down': out['m_mlp_w_down'], 'm_final_norm': out['m_final_norm'], 'v_gla_norm': out['v_gla_norm'], 'v_gla_w_in': out['v_gla_w_in'], 'v_gla_w_gate_up': out['v_gla_w_gate_up'], 'v_gla_b_gate': out['v_gla_b_gate'], 'v_gla_o_norm': out['v_gla_o_norm'], 'v_gla_w_out': out['v_gla_w_out'], 'v_s5_norm': out['v_s5_norm'], 'v_s5_w_in': out['v_s5_w_in'], 'v_s5_lam_re': out['v_s5_lam_re'], 'v_s5_lam_im': out['v_s5_lam_im'], 'v_s5_log_dt': out['v_s5_log_dt'], 'v_s5_b_re': out['v_s5_b_re'], 'v_s5_b_im': out['v_s5_b_im'], 'v_s5_c_re': out['v_s5_c_re'], 'v_s5_c_im': out['v_s5_c_im'], 'v_s5_d': out['v_s5_d'], 'v_s5_w_out': out['v_s5_w_out'], 'v_mlp_norm': out['v_mlp_norm'], 'v_mlp_w_up': out['v_mlp_w_up'], 'v_mlp_w_down': out['v_mlp_w_down'], 'v_final_norm': out['v_final_norm']}


def _loss(weights, diff, rest, loss_target):
    with _jax.named_scope("forward"):
        args = {**rest, TWIN_DIFF_INPUT: diff, **{k: w.astype(_WEIGHT_DTYPES[k]) for k, w in weights.items()}}
        y = _forward(args)
    with _jax.named_scope("loss_head"):
        err = _jnp.square(y.astype(_jnp.float32) - loss_target)
        return 0.5 * _jnp.sum(_jnp.mean(err, axis=-1)) if err.ndim else 0.5 * err


def _adamw(w, g, m, v):
    m = ADAM_B1 * m + (1.0 - ADAM_B1) * g
    v = ADAM_B2 * v + (1.0 - ADAM_B2) * _jnp.square(g)
    m_hat = m / (1.0 - ADAM_B1 ** ADAM_STEP)
    v_hat = v / (1.0 - ADAM_B2 ** ADAM_STEP)
    delta = -ADAM_LR * (m_hat / (_jnp.sqrt(v_hat) + ADAM_EPS) + ADAM_WD * w)
    return delta, m, v


def reference(x, gla_norm, gla_w_in, gla_w_gate_up, gla_b_gate, gla_o_norm, gla_w_out, s5_norm, s5_w_in, s5_lam_re, s5_lam_im, s5_log_dt, s5_b_re, s5_b_im, s5_c_re, s5_c_im, s5_d, s5_w_out, mlp_norm, mlp_w_up, mlp_w_down, final_norm, loss_target, m_gla_norm, m_gla_w_in, m_gla_w_gate_up, m_gla_b_gate, m_gla_o_norm, m_gla_w_out, m_s5_norm, m_s5_w_in, m_s5_lam_re, m_s5_lam_im, m_s5_log_dt, m_s5_b_re, m_s5_b_im, m_s5_c_re, m_s5_c_im, m_s5_d, m_s5_w_out, m_mlp_norm, m_mlp_w_up, m_mlp_w_down, m_final_norm, v_gla_norm, v_gla_w_in, v_gla_w_gate_up, v_gla_b_gate, v_gla_o_norm, v_gla_w_out, v_s5_norm, v_s5_w_in, v_s5_lam_re, v_s5_lam_im, v_s5_log_dt, v_s5_b_re, v_s5_b_im, v_s5_c_re, v_s5_c_im, v_s5_d, v_s5_w_out, v_mlp_norm, v_mlp_w_up, v_mlp_w_down, v_final_norm):
    given = dict(x=x, gla_norm=gla_norm, gla_w_in=gla_w_in, gla_w_gate_up=gla_w_gate_up, gla_b_gate=gla_b_gate, gla_o_norm=gla_o_norm, gla_w_out=gla_w_out, s5_norm=s5_norm, s5_w_in=s5_w_in, s5_lam_re=s5_lam_re, s5_lam_im=s5_lam_im, s5_log_dt=s5_log_dt, s5_b_re=s5_b_re, s5_b_im=s5_b_im, s5_c_re=s5_c_re, s5_c_im=s5_c_im, s5_d=s5_d, s5_w_out=s5_w_out, mlp_norm=mlp_norm, mlp_w_up=mlp_w_up, mlp_w_down=mlp_w_down, final_norm=final_norm, loss_target=loss_target, m_gla_norm=m_gla_norm, m_gla_w_in=m_gla_w_in, m_gla_w_gate_up=m_gla_w_gate_up, m_gla_b_gate=m_gla_b_gate, m_gla_o_norm=m_gla_o_norm, m_gla_w_out=m_gla_w_out, m_s5_norm=m_s5_norm, m_s5_w_in=m_s5_w_in, m_s5_lam_re=m_s5_lam_re, m_s5_lam_im=m_s5_lam_im, m_s5_log_dt=m_s5_log_dt, m_s5_b_re=m_s5_b_re, m_s5_b_im=m_s5_b_im, m_s5_c_re=m_s5_c_re, m_s5_c_im=m_s5_c_im, m_s5_d=m_s5_d, m_s5_w_out=m_s5_w_out, m_mlp_norm=m_mlp_norm, m_mlp_w_up=m_mlp_w_up, m_mlp_w_down=m_mlp_w_down, m_final_norm=m_final_norm, v_gla_norm=v_gla_norm, v_gla_w_in=v_gla_w_in, v_gla_w_gate_up=v_gla_w_gate_up, v_gla_b_gate=v_gla_b_gate, v_gla_o_norm=v_gla_o_norm, v_gla_w_out=v_gla_w_out, v_s5_norm=v_s5_norm, v_s5_w_in=v_s5_w_in, v_s5_lam_re=v_s5_lam_re, v_s5_lam_im=v_s5_lam_im, v_s5_log_dt=v_s5_log_dt, v_s5_b_re=v_s5_b_re, v_s5_b_im=v_s5_b_im, v_s5_c_re=v_s5_c_re, v_s5_c_im=v_s5_c_im, v_s5_d=v_s5_d, v_s5_w_out=v_s5_w_out, v_mlp_norm=v_mlp_norm, v_mlp_w_up=v_mlp_w_up, v_mlp_w_down=v_mlp_w_down, v_final_norm=v_final_norm)
    weights = {n: given[n] for n in TWIN_WEIGHTS}
    shared = {n: given[n] for n in SHARED_INPUTS}
    per_example = {n: given[n] for n in ['x']}
    grad_fn = _jax.value_and_grad(_loss, argnums=(0, 1))

    def one_microbatch(ex, loss_target):
        ex = dict(ex)
        diff = ex.pop(TWIN_DIFF_INPUT)
        return grad_fn(weights, diff, {**shared, **ex}, loss_target)

    if N_MICROBATCH == 1:
        loss, (grad_w, grad_x) = one_microbatch(per_example, given["loss_target"])
    else:
        def body(carry, xs):
            loss_sum, grad_sum = carry
            l_k, (gw_k, gx_k) = one_microbatch(xs[0], xs[1])
            with _jax.named_scope("update"):
                return (loss_sum + l_k, _jax.tree.map(_jnp.add, grad_sum, gw_k)), gx_k

        init = (_jnp.zeros((), _jnp.float32), _jax.tree.map(_jnp.zeros_like, weights))
        (loss, grad_w), grad_x = _jax.lax.scan(body, init, (per_example, given["loss_target"]))
    with _jax.named_scope("update"):
        delta_w, new_m, new_v = {}, {}, {}
        for n in TWIN_WEIGHTS:
            delta_w[n], new_m[n], new_v[n] = _adamw(weights[n], grad_w[n], given["m_" + n], given["v_" + n])
    return (loss, grad_x, *[grad_w[n] for n in TWIN_WEIGHTS], *[delta_w[n] for n in TWIN_WEIGHTS],
            *[new_m[n] for n in TWIN_WEIGHTS], *[new_v[n] for n in TWIN_WEIGHTS])
```

```python
import functools
import math

import jax
import jax.numpy as jnp
from jax import lax
from jax.experimental import pallas as pl
from jax.experimental.pallas import tpu as pltpu

F32, BF16 = jnp.float32, jnp.bfloat16
MESH = pl.DeviceIdType.MESH
N_DEV = 8
LANE = 128
SUBLANE = 8
VMEM_LIMIT = 48 * 1024 * 1024

EPS = 1e-6
CHUNK = 64
GLA_HEADS = 4
GLA_RANK = 16
GLA_TEMP = 16.0
S5_GROUP = 16
S5_STATE = 64
S5_EIG_CLIP = -1e-4
S5_GB = 16
ADAM_LR, ADAM_B1, ADAM_B2, ADAM_EPS, ADAM_WD, ADAM_STEP = 0.001, 0.9, 0.999, 1e-08, 0.01, 10

_CONTRACT = {"nn": ((1,), (0,)), "tn": ((0,), (0,)), "nt": ((1,), (1,))}


def _tile(n, pref, unit=LANE):
    if n <= pref:
        return n
    t = (pref // unit) * unit
    while t > unit and n % t:
        t -= unit
    assert n % t == 0, (n, pref, unit)
    return t


def _params(sem, vmem=None):
    return pltpu.CompilerParams(dimension_semantics=sem, vmem_limit_bytes=vmem)


def _dot(a, b, dims, precision=None):
    return lax.dot_general(a, b, (_CONTRACT[dims], ((), ())), preferred_element_type=F32, precision=precision)


def _bdot(a, b, dims):
    return _dot(a.astype(BF16), b.astype(BF16), dims)


def _mm_call(a, b, *, dims, grid, a_spec, b_spec, o_spec, out_shape, out_dtype, name, res=None):
    nk = grid[2]
    acc_shape = tuple(d for d in o_spec.block_shape if d is not None)

    def body(*refs):
        if res is None:
            a_ref, b_ref, o_ref = refs[:3]
            r_ref = None
        else:
            a_ref, b_ref, r_ref, o_ref = refs[:4]

        def finish(r):
            if r_ref is not None:
                r = r + r_ref[...].astype(F32)
            o_ref[...] = r.astype(o_ref.dtype)

        if nk == 1:
            finish(_bdot(a_ref[...], b_ref[...], dims))
            return
        acc = refs[-1]
        k = pl.program_id(2)

        @pl.when(k == 0)
        def _():
            acc[...] = jnp.zeros_like(acc)

        acc[...] += _bdot(a_ref[...], b_ref[...], dims)

        @pl.when(k == nk - 1)
        def _():
            finish(acc[...])

    ins = [a, b] + ([] if res is None else [res])
    in_specs = [a_spec, b_spec] + ([] if res is None else [o_spec])
    return pl.pallas_call(
        body, grid=grid, in_specs=in_specs, out_specs=o_spec,
        out_shape=jax.ShapeDtypeStruct(out_shape, out_dtype),
        scratch_shapes=[] if nk == 1 else [pltpu.VMEM(acc_shape, F32)],
        compiler_params=_params(("parallel", "parallel", "arbitrary"), VMEM_LIMIT), name=name,
    )(*ins)


def _mm(a, b, dims, *, name, out_dtype=F32, res=None, tm=512, tn=1024, tk=512):
    if dims == "tn":
        (K, M), (_, N) = a.shape, b.shape
    elif dims == "nn":
        (M, K), (_, N) = a.shape, b.shape
    else:
        (M, K), (N, _) = a.shape, b.shape
    tm, tn, tk = _tile(M, tm), _tile(N, tn), _tile(K, tk)
    a_spec = pl.BlockSpec((tk, tm), lambda i, j, k: (k, i)) if dims == "tn" else pl.BlockSpec((tm, tk), lambda i, j, k: (i, k))
    b_spec = pl.BlockSpec((tn, tk), lambda i, j, k: (j, k)) if dims == "nt" else pl.BlockSpec((tk, tn), lambda i, j, k: (k, j))
    o_spec = pl.BlockSpec((tm, tn), lambda i, j, k: (i, j))
    return _mm_call(a, b, dims=dims, grid=(M // tm, N // tn, K // tk), a_spec=a_spec, b_spec=b_spec, o_spec=o_spec,
                    out_shape=(M, N), out_dtype=out_dtype, name=name, res=res)


def _mm_nn_cb(a, b3, *, name, out_dtype=F32, tm=512, tn=1024, tk=512):
    (M, K), (P, _, Ns) = a.shape, b3.shape
    tm, tn, tk = _tile(M, tm), _tile(Ns, tn), _tile(K, tk)
    npb = Ns // tn
    return _mm_call(a, b3, dims="nn", grid=(M // tm, P * npb, K // tk),
                    a_spec=pl.BlockSpec((tm, tk), lambda i, j, k: (i, k)),
                    b_spec=pl.BlockSpec((None, tk, tn), lambda i, j, k: (j // npb, k, j % npb)),
                    o_spec=pl.BlockSpec((tm, tn), lambda i, j, k: (i, j)),
                    out_shape=(M, P * Ns), out_dtype=out_dtype, name=name)


def _mm_nt_cb(a, b3, *, name, out_dtype=F32, tm=512, tn=1024, tk=512):
    (M, _), (P, N, Ns) = a.shape, b3.shape
    tm, tn, tk = _tile(M, tm), _tile(N, tn), _tile(Ns, tk)
    kpb = Ns // tk
    return _mm_call(a, b3, dims="nt", grid=(M // tm, N // tn, P * kpb),
                    a_spec=pl.BlockSpec((tm, tk), lambda i, j, k: (i, k)),
                    b_spec=pl.BlockSpec((None, tn, tk), lambda i, j, k: (k // kpb, j, k % kpb)),
                    o_spec=pl.BlockSpec((tm, tn), lambda i, j, k: (i, j)),
                    out_shape=(M, N), out_dtype=out_dtype, name=name)


def _mm_tn_cbout(a, b, parts, *, name, out_dtype=BF16, tm=512, tn=1024, tk=512):
    (K, M), (_, N) = a.shape, b.shape
    Ns = N // parts
    tm, tn, tk = _tile(M, tm), _tile(Ns, tn), _tile(K, tk)
    npb = Ns // tn
    return _mm_call(a, b, dims="tn", grid=(M // tm, parts * npb, K // tk),
                    a_spec=pl.BlockSpec((tk, tm), lambda i, j, k: (k, i)),
                    b_spec=pl.BlockSpec((tk, tn), lambda i, j, k: (k, j)),
                    o_spec=pl.BlockSpec((None, tm, tn), lambda i, j, k: (j // npb, i, j % npb)),
                    out_shape=(parts, M, Ns), out_dtype=out_dtype, name=name)


def _mm_bd(a, w3, *, name, out_dtype=F32, res=None, tm=512, tn=512):
    (M, _), (nb, Kb, Nb) = a.shape, w3.shape
    tm, tn = _tile(M, tm), _tile(Nb, tn)
    npb = Nb // tn
    return _mm_call(a, w3, dims="nn", grid=(M // tm, nb * npb, 1),
                    a_spec=pl.BlockSpec((tm, Kb), lambda i, j, k: (i, j // npb)),
                    b_spec=pl.BlockSpec((None, Kb, tn), lambda i, j, k: (j // npb, 0, j % npb)),
                    o_spec=pl.BlockSpec((tm, tn), lambda i, j, k: (i, j)),
                    out_shape=(M, nb * Nb), out_dtype=out_dtype, name=name, res=res)


def _mm_tn_bd(a, b, nb, *, name, tk=512):
    (K, MA), (_, NB) = a.shape, b.shape
    Ma, Nb = MA // nb, NB // nb
    tk = _tile(K, tk)
    return _mm_call(a, b, dims="tn", grid=(nb, 1, K // tk),
                    a_spec=pl.BlockSpec((tk, Ma), lambda i, j, k: (k, i)),
                    b_spec=pl.BlockSpec((tk, Nb), lambda i, j, k: (k, i)),
                    o_spec=pl.BlockSpec((None, Ma, Nb), lambda i, j, k: (i, 0, 0)),
                    out_shape=(nb, Ma, Nb), out_dtype=F32, name=name)


def _rowmap(fn, rows, consts, out_defs, red_defs=(), *, name, tr=256):
    L = rows[0].shape[0]
    widest = max([r.shape[1] for r in rows] + [n for n, _ in out_defs])
    tr = _tile(L, max(SUBLANE * 2, min(tr, 512 * 1024 // widest)), SUBLANE * 2)
    n_in, n_o, n_d = len(rows) + len(consts), len(out_defs), len(red_defs)

    def body(*refs):
        res = fn(*[r[...] for r in refs[:n_in]])
        res = res if isinstance(res, (tuple, list)) else (res,)
        outs = refs[n_in:]
        for o_ref, val in zip(outs[:n_o], res[:n_o]):
            o_ref[...] = val.astype(o_ref.dtype)
        if n_d:
            @pl.when(pl.program_id(0) == 0)
            def _():
                for o_ref in outs[n_o:]:
                    o_ref[...] = jnp.zeros_like(o_ref)
            for o_ref, val in zip(outs[n_o:], res[n_o:]):
                o_ref[...] += val

    in_specs = [pl.BlockSpec((tr, r.shape[1]), lambda i: (i, 0)) for r in rows]
    in_specs += [pl.BlockSpec(c.shape, lambda i, nd=c.ndim: (0,) * nd) for c in consts]
    out_specs = [pl.BlockSpec((tr, n), lambda i: (i, 0)) for n, _ in out_defs]
    out_specs += [pl.BlockSpec((1, n), lambda i: (0, 0)) for n in red_defs]
    out_shape = [jax.ShapeDtypeStruct((L, n), dt) for n, dt in out_defs]
    out_shape += [jax.ShapeDtypeStruct((1, n), F32) for n in red_defs]
    return pl.pallas_call(body, grid=(L // tr,), in_specs=in_specs, out_specs=out_specs, out_shape=out_shape,
                          compiler_params=_params(("arbitrary",), VMEM_LIMIT), name=name)(*rows, *consts)


def _rms_parts(x):
    r = lax.rsqrt(jnp.mean(x * x, axis=-1, keepdims=True) + EPS)
    return r, x * r


def _rms_fwd(h, g, name):
    def fn(x, gg):
        _, xh = _rms_parts(x)
        return (xh * gg,)
    return _rowmap(fn, [h], [g], [(h.shape[1], BF16)], name=name)[0]


def _rms_bwd(h, dhn, dh, g, name):
    def fn(x, dy, dres, gg):
        r, xh = _rms_parts(x)
        dxh = dy * gg
        dx = r * (dxh - xh * jnp.mean(dxh * xh, axis=-1, keepdims=True))
        return dres + dx, jnp.sum(dy * xh, axis=0, keepdims=True)
    D = h.shape[1]
    return _rowmap(fn, [h, dhn, dh], [g], [(D, F32)], [D], name=name)


def _loss_head(h, tgt, g, name):
    D = h.shape[1]

    def fn(x, t, gg):
        r, xh = _rms_parts(x)
        diff = xh * gg - t
        dy = diff * (1.0 / D)
        dxh = dy * gg
        dx = r * (dxh - xh * jnp.mean(dxh * xh, axis=-1, keepdims=True))
        return dx, jnp.sum(diff * diff, axis=0, keepdims=True), jnp.sum(dy * xh, axis=0, keepdims=True)
    return _rowmap(fn, [h, tgt], [g], [(D, F32)], [D, D], name=name)


def _tri(n, strict):
    r = lax.broadcasted_iota(jnp.int32, (n, n), 0)
    c = lax.broadcasted_iota(jnp.int32, (n, n), 1)
    return jnp.where((c < r) if strict else (c <= r), 1.0, 0.0).astype(F32)


def _gla_gate(g_ref, wgu_ref, bg_ref):
    pre = _bdot(g_ref[...], wgu_ref[...], "nn") + bg_ref[...]
    la = (jnp.minimum(pre, 0.0) - jnp.log(1.0 + jnp.exp(-jnp.abs(pre)))) * (1.0 / GLA_TEMP)
    cum = _dot(_tri(CHUNK, False), la, "nn", lax.Precision.HIGHEST)
    return pre, cum, cum[CHUNK - 1:CHUNK, :]


def _gla_specs(H, DK, DV, cmap):
    kb, vb, gb = H, (2 * H * DK) // DV, (2 * H * DK + 2 * H * DV) // LANE
    return [
        pl.BlockSpec((CHUNK, DK), lambda h, c: (cmap(h, c), h)),
        pl.BlockSpec((CHUNK, DK), lambda h, c: (cmap(h, c), kb + h)),
        pl.BlockSpec((CHUNK, DV), lambda h, c: (cmap(h, c), vb + h)),
        pl.BlockSpec((CHUNK, DV), lambda h, c: (cmap(h, c), vb + H + h)),
        pl.BlockSpec((CHUNK, LANE), lambda h, c: (cmap(h, c), gb)),
        pl.BlockSpec((LANE, DK), lambda h, c: (0, h)),
        pl.BlockSpec((1, DK), lambda h, c: (0, h)),
        pl.BlockSpec((1, DV), lambda h, c: (0, 0)),
    ]


def _gla_fwd(proj, wgu, bg, on, *, H, DK, DV, name):
    L = proj.shape[0]
    nc = L // CHUNK
    scale = DK ** -0.5

    def body(q_ref, k_ref, v_ref, r_ref, g_ref, wgu_ref, bg_ref, on_ref, y_ref, st_ref, S):
        @pl.when(pl.program_id(1) == 0)
        def _():
            S[...] = jnp.zeros_like(S)
        _, cum, total = _gla_gate(g_ref, wgu_ref, bg_ref)
        kd = k_ref[...] * jnp.exp(total - cum)
        St = S[...] * jnp.exp(total) + _bdot(v_ref[...], kd, "tn")
        S[...] = St
        st_ref[...] = St
        o = _bdot(q_ref[...] * scale, St, "nt")
        _, oh = _rms_parts(o)
        y_ref[...] = (oh * on_ref[...] * jax.nn.silu(r_ref[...])).astype(y_ref.dtype)

    return pl.pallas_call(
        body, grid=(H, nc), in_specs=_gla_specs(H, DK, DV, lambda h, c: c),
        out_specs=[pl.BlockSpec((CHUNK, DV), lambda h, c: (c, h)),
                   pl.BlockSpec((None, None, DV, DK), lambda h, c: (h, c, 0, 0))],
        out_shape=[jax.ShapeDtypeStruct((L, H * DV), BF16), jax.ShapeDtypeStruct((H, nc, DV, DK), F32)],
        scratch_shapes=[pltpu.VMEM((DV, DK), F32)],
        compiler_params=_params(("arbitrary", "arbitrary"), VMEM_LIMIT), name=name,
    )(proj, proj, proj, proj, proj, wgu, bg, on)


def _gla_bwd(proj, wgu, bg, on, st, dy, *, H, DK, DV, name):
    L = proj.shape[0]
    nc = L // CHUNK
    scale = DK ** -0.5
    rev = lambda h, c: nc - 1 - c

    def body(q_ref, k_ref, v_ref, r_ref, g_ref, wgu_ref, bg_ref, on_ref, sc_ref, sp_ref, dy_ref,
             dq_ref, dk_ref, dv_ref, dr_ref, dpre_ref, don_ref, G, decn):
        h, c = pl.program_id(0), pl.program_id(1)

        @pl.when(c == 0)
        def _():
            G[...] = jnp.zeros_like(G)
            decn[...] = jnp.zeros_like(decn)

        @pl.when((c == 0) & (h == 0))
        def _():
            don_ref[...] = jnp.zeros_like(don_ref)

        pre, cum, total = _gla_gate(g_ref, wgu_ref, bg_ref)
        ex = jnp.exp(total - cum)
        k, v, r = k_ref[...], v_ref[...], r_ref[...]
        kd = k * ex
        dec = jnp.exp(total)
        qs = q_ref[...] * scale
        Sc = sc_ref[...]
        o = _bdot(qs, Sc, "nt")
        rinv, oh = _rms_parts(o)
        gn = on_ref[...]
        sg = jax.nn.sigmoid(r)
        dyv = dy_ref[...]
        d_on = dyv * (r * sg)
        dr_ref[...] = (dyv * (oh * gn) * (sg * (1.0 + r * (1.0 - sg)))).astype(dr_ref.dtype)
        don_ref[...] += jnp.sum(d_on * oh, axis=0, keepdims=True)
        dxh = d_on * gn
        do = rinv * (dxh - oh * jnp.mean(dxh * oh, axis=-1, keepdims=True))
        dq_ref[...] = (_bdot(do, Sc, "nn") * scale).astype(dq_ref.dtype)
        Gt = G[...] * decn[...] + _bdot(do, qs, "tn")
        G[...] = Gt
        decn[...] = dec
        dkd = _bdot(v, Gt, "nn")
        dv_ref[...] = _bdot(kd, Gt, "nt").astype(dv_ref.dtype)
        Sp = sp_ref[...] * jnp.where(c == nc - 1, 0.0, 1.0)
        ddec = jnp.sum(Gt * Sp, axis=0, keepdims=True)
        dk_ref[...] = (dkd * ex).astype(dk_ref.dtype)
        dla = ddec * dec + _dot(_tri(CHUNK, True), dkd * kd, "nn", lax.Precision.HIGHEST)
        dpre_ref[...] = dla * (1.0 / GLA_TEMP) * jax.nn.sigmoid(-pre)

    in_specs = _gla_specs(H, DK, DV, rev) + [
        pl.BlockSpec((None, None, DV, DK), lambda h, c: (h, rev(h, c), 0, 0)),
        pl.BlockSpec((None, None, DV, DK), lambda h, c: (h, jnp.maximum(rev(h, c) - 1, 0), 0, 0)),
        pl.BlockSpec((CHUNK, DV), lambda h, c: (rev(h, c), h)),
    ]
    blk_k = pl.BlockSpec((CHUNK, DK), lambda h, c: (rev(h, c), h))
    blk_v = pl.BlockSpec((CHUNK, DV), lambda h, c: (rev(h, c), h))
    out_specs = [blk_k, blk_k, blk_v, blk_v, blk_k, pl.BlockSpec((1, DV), lambda h, c: (0, 0))]
    sds = jax.ShapeDtypeStruct
    out_shape = [sds((L, H * DK), BF16), sds((L, H * DK), BF16), sds((L, H * DV), BF16), sds((L, H * DV), BF16),
                 sds((L, H * DK), F32), sds((1, DV), F32)]
    dq, dk, dv, dr, dpre, don = pl.pallas_call(
        body, grid=(H, nc), in_specs=in_specs, out_specs=out_specs, out_shape=out_shape,
        scratch_shapes=[pltpu.VMEM((DV, DK), F32), pltpu.VMEM((1, DK), F32)],
        compiler_params=_params(("arbitrary", "arbitrary"), VMEM_LIMIT), name=name,
    )(proj, proj, proj, proj, proj, wgu, bg, on, st, st, dy)
    dqkvr = jnp.concatenate([dq, dk, dv, dr], axis=1)
    return dqkvr, dpre, don


def _s5_param_fn(lam_re, lam_im, log_dt, brt, bit):
    lr = jnp.minimum(lam_re, S5_EIG_CLIP)
    li = lam_im
    dt = jnp.exp(log_dt)
    mag = jnp.exp(lr * dt)
    ang = li * dt
    ab_re = mag * jnp.cos(ang)
    ab_im = mag * jnp.sin(ang)
    den = lr * lr + li * li
    nr = ab_re - 1.0
    f_re = (nr * lr + ab_im * li) / den
    f_im = (ab_im * lr - nr * li) / den
    return ab_re, ab_im, f_re * brt - f_im * bit, f_re * bit + f_im * brt


def _s5_params(lam_re, lam_im, log_dt, brt, bit, name):
    G, _, N = lam_re.shape

    def body(lr_ref, li_ref, dt_ref, br_ref, bi_ref, ar_ref, ai_ref, bbr_ref, bbi_ref, pr_ref, pi_ref, a124_ref):
        lre, lim, ldt = lr_ref[...], li_ref[...], dt_ref[...]
        ar, ai, bbr, bbi = _s5_param_fn(lre, lim, ldt, br_ref[...], bi_ref[...])
        ar_ref[...], ai_ref[...], bbr_ref[...], bbi_ref[...] = ar, ai, bbr, bbi
        kk = (lax.broadcasted_iota(jnp.int32, (1, SUBLANE, 1), 1) + 1).astype(F32)
        dt = jnp.exp(ldt)
        mag = jnp.exp(kk * (jnp.minimum(lre, S5_EIG_CLIP) * dt))
        ang = kk * (lim * dt)
        pr_ref[...] = mag * jnp.cos(ang)
        pi_ref[...] = mag * jnp.sin(ang)
        r = lax.broadcasted_iota(jnp.int32, (1, SUBLANE, 1), 1)
        k2 = jnp.where(r < 2, 1.0, jnp.where(r < 4, 2.0, jnp.where(r < 6, 4.0, 0.0)))
        mag2 = jnp.exp(k2 * (jnp.minimum(lre, S5_EIG_CLIP) * dt))
        ang2 = k2 * (lim * dt)
        a124_ref[...] = mag2 * jnp.where(r % 2 == 0, jnp.cos(ang2), jnp.sin(ang2))

    sds = jax.ShapeDtypeStruct
    return pl.pallas_call(
        body, out_shape=[sds((G, 1, N), F32), sds((G, 1, N), F32), sds(brt.shape, F32), sds(brt.shape, F32),
                         sds((G, SUBLANE, N), F32), sds((G, SUBLANE, N), F32), sds((G, SUBLANE, N), F32)], name=name,
    )(lam_re, lam_im, log_dt, brt, bit)


def _s5_params_bwd(lam_re, lam_im, log_dt, brt, bit, dar, dai, dbbr, dbbi, name):
    def body(lr_ref, li_ref, dt_ref, br_ref, bi_ref, dar_ref, dai_ref, dbbr_ref, dbbi_ref, *outs):
        _, vjp = jax.vjp(_s5_param_fn, lr_ref[...], li_ref[...], dt_ref[...], br_ref[...], bi_ref[...])
        for o_ref, val in zip(outs, vjp((dar_ref[...], dai_ref[...], dbbr_ref[...], dbbi_ref[...]))):
            o_ref[...] = val

    ins = (lam_re, lam_im, log_dt, brt, bit)
    return pl.pallas_call(body, out_shape=[jax.ShapeDtypeStruct(a.shape, F32) for a in ins], name=name)(
        *ins, dar, dai, dbbr, dbbi)


def _s5_scan(br, bi, a124, pr, pi, *, reverse, name, W=256):
    L, n = br.shape
    W = _tile(n, W)
    nblk = L // SUBLANE

    def body(br_ref, bi_ref, a_ref, pr_ref, pi_ref, xr_ref, xi_ref):
        A = a_ref[...]
        PR, PI = pr_ref[...], pi_ref[...]
        row = lax.broadcasted_iota(jnp.int32, (SUBLANE, W), 0)
        last = 0 if reverse else SUBLANE - 1

        def step(i, carry):
            cr, ci = carry
            off = pl.multiple_of(((nblk - 1 - i) if reverse else i) * SUBLANE, SUBLANE)
            xr, xi = br_ref[pl.ds(off, SUBLANE), :], bi_ref[pl.ds(off, SUBLANE), :]
            for j, k in enumerate((1, 2, 4)):
                ar, ai = A[2 * j:2 * j + 1, :], A[2 * j + 1:2 * j + 2, :]
                keep = (row < SUBLANE - k) if reverse else (row >= k)
                shift = (SUBLANE - k) if reverse else k
                sr = jnp.where(keep, pltpu.roll(xr, shift, 0), 0.0)
                si = jnp.where(keep, pltpu.roll(xi, shift, 0), 0.0)
                xr, xi = xr + ar * sr - ai * si, xi + ar * si + ai * sr
            xr, xi = xr + PR * cr - PI * ci, xi + PR * ci + PI * cr
            xr_ref[pl.ds(off, SUBLANE), :] = xr
            xi_ref[pl.ds(off, SUBLANE), :] = xi
            return xr[last:last + 1, :], xi[last:last + 1, :]

        z = jnp.zeros((1, W), F32)
        lax.fori_loop(0, nblk, step, (z, z))

    col = pl.BlockSpec((L, W), lambda j: (0, j))
    par = pl.BlockSpec((SUBLANE, W), lambda j: (0, j))
    return pl.pallas_call(
        body, grid=(n // W,), in_specs=[col, col, par, par, par], out_specs=[col, col],
        out_shape=[jax.ShapeDtypeStruct((L, n), F32)] * 2,
        compiler_params=_params(("parallel",), VMEM_LIMIT), name=name,
    )(br, bi, a124, pr, pi)


def _s5_dabar(lr, li, xr, xi, name, W=256):
    L, n = lr.shape
    W = _tile(n, W)

    def body(lr_ref, li_ref, xr_ref, xi_ref, dar_ref, dai_ref):
        first = lax.broadcasted_iota(jnp.int32, (L, W), 0) == 0
        pr = jnp.where(first, 0.0, pltpu.roll(xr_ref[...], 1, 0))
        pi = jnp.where(first, 0.0, pltpu.roll(xi_ref[...], 1, 0))
        a, b = lr_ref[...], li_ref[...]
        dar_ref[...] = jnp.sum(a * pr + b * pi, axis=0, keepdims=True)
        dai_ref[...] = jnp.sum(b * pr - a * pi, axis=0, keepdims=True)

    col = pl.BlockSpec((L, W), lambda j: (0, j))
    one = pl.BlockSpec((1, W), lambda j: (0, j))
    return pl.pallas_call(
        body, grid=(n // W,), in_specs=[col] * 4, out_specs=[one, one],
        out_shape=[jax.ShapeDtypeStruct((1, n), F32)] * 2,
        compiler_params=_params(("parallel",), VMEM_LIMIT), name=name,
    )(lr, li, xr, xi)


def _bd(w):
    G, A, B = w.shape
    w4 = w.reshape(G // S5_GB, S5_GB, A, B)
    eye = jnp.eye(S5_GB, dtype=w.dtype)
    return jnp.einsum("kgab,gh->kgahb", w4, eye).reshape(G // S5_GB, S5_GB * A, S5_GB * B).astype(BF16)


def _bd_extract(m, A, B):
    nb = m.shape[0]
    d = jnp.diagonal(m.reshape(nb, S5_GB, A, S5_GB, B), axis1=1, axis2=3)
    return jnp.moveaxis(d, 3, 1).reshape(nb * S5_GB, A, B)


def _place():
    return lax.axis_index("x"), lax.axis_index("y"), lax.axis_index("c")


def _slot(p):
    return 4 * p[0] + 2 * p[1] + p[2]


def _all_gather(xs, name):
    n = len(xs)

    def body(*refs):
        x_refs, o_refs = refs[:n], refs[n:2 * n]
        send_sems, recv_sems, local_sems = refs[2 * n:]
        x, y, c = _place()
        me, sib = (x, y, c), (x, y, 1 - c)
        chips = [(1 - x, y), (x, 1 - y), (1 - x, 1 - y)]

        def copy(a, k, block, to, src=None):
            dst = o_refs[a].at[_slot(block)]
            return pltpu.make_async_remote_copy(
                src_ref=dst if src is None else src, dst_ref=dst, send_sem=send_sems.at[a, k], recv_sem=recv_sems.at[a, k],
                device_id=to, device_id_type=MESH)

        mine = [pltpu.make_async_copy(x_refs[a], o_refs[a].at[_slot(me)], local_sems.at[a]) for a in range(n)]
        for cp in mine:
            cp.start()
        sent = []
        for a in range(n):
            sent.append(copy(a, 0, me, sib, src=x_refs[a]))
            sent += [copy(a, 1 + j, me, (*chip, c), src=x_refs[a]) for j, chip in enumerate(chips)]
        for cp in sent:
            cp.start()
        for j, chip in enumerate(chips):
            for a in range(n):
                copy(a, 1 + j, (*chip, c), me).wait_recv()
                fwd = copy(a, 4 + j, (*chip, c), sib)
                fwd.start()
                sent.append(fwd)
        for a in range(n):
            copy(a, 0, sib, me).wait_recv()
            for j, chip in enumerate(chips):
                copy(a, 4 + j, (*chip, 1 - c), me).wait_recv()
        for cp in sent:
            cp.wait_send()
        for cp in mine:
            cp.wait()

    any_spec = pl.BlockSpec(memory_space=pl.ANY)
    return pl.pallas_call(
        body, in_specs=[any_spec] * n, out_specs=[any_spec] * n,
        out_shape=[jax.ShapeDtypeStruct((N_DEV, *a.shape), a.dtype) for a in xs],
        scratch_shapes=[pltpu.SemaphoreType.DMA((n, 7)), pltpu.SemaphoreType.DMA((n, 7)), pltpu.SemaphoreType.DMA((n,))],
        name=name,
    )(*xs)


def _all_to_all(xs, name):
    n = len(xs)
    flips = [(fx, fy, fc) for fx in (0, 1) for fy in (0, 1) for fc in (0, 1)][1:]

    def body(*refs):
        x_refs, o_refs = refs[:n], refs[n:2 * n]
        send_sems, recv_sems, local_sems = refs[2 * n:]
        x, y, c = _place()
        me = (x, y, c)
        peers = [(x ^ fx, y ^ fy, c ^ fc) for fx, fy, fc in flips]

        def copy(a, k):
            return pltpu.make_async_remote_copy(
                src_ref=x_refs[a].at[_slot(peers[k])], dst_ref=o_refs[a].at[_slot(me)],
                send_sem=send_sems.at[a, k], recv_sem=recv_sems.at[a, k], device_id=peers[k], device_id_type=MESH)

        def arrival(a, k):
            return pltpu.make_async_remote_copy(
                src_ref=x_refs[a].at[_slot(peers[k])], dst_ref=o_refs[a].at[_slot(peers[k])],
                send_sem=send_sems.at[a, k], recv_sem=recv_sems.at[a, k], device_id=peers[k], device_id_type=MESH)

        mine = [pltpu.make_async_copy(x_refs[a].at[_slot(me)], o_refs[a].at[_slot(me)], local_sems.at[a]) for a in range(n)]
        for cp in mine:
            cp.start()
        sent = [copy(a, k) for a in range(n) for k in range(7)]
        for cp in sent:
            cp.start()
        for a in range(n):
            for k in range(7):
                arrival(a, k).wait_recv()
        for cp in sent:
            cp.wait_send()
        for cp in mine:
            cp.wait()

    any_spec = pl.BlockSpec(memory_space=pl.ANY)
    return pl.pallas_call(
        body, in_specs=[any_spec] * n, out_specs=[any_spec] * n,
        out_shape=[jax.ShapeDtypeStruct(a.shape, a.dtype) for a in xs],
        scratch_shapes=[pltpu.SemaphoreType.DMA((n, 7)), pltpu.SemaphoreType.DMA((n, 7)), pltpu.SemaphoreType.DMA((n,))],
        name=name,
    )(*xs)


def _adamw_math(w, g, m, v):
    m = ADAM_B1 * m + (1.0 - ADAM_B1) * g
    v = ADAM_B2 * v + (1.0 - ADAM_B2) * (g * g)
    m_hat = m / (1.0 - ADAM_B1 ** ADAM_STEP)
    v_hat = v / (1.0 - ADAM_B2 ** ADAM_STEP)
    return -ADAM_LR * (m_hat / (jnp.sqrt(v_hat) + ADAM_EPS) + ADAM_WD * w), m, v


def _adamw(w, m, v, parts, layer, prev, name):
    nl, R, C = w.shape
    P = parts.shape[0]
    unit = SUBLANE * (4 // parts.dtype.itemsize)
    tr = _tile(R, max(unit, (128 * 1024 // C) // unit * unit), unit)

    def body(w_ref, m_ref, v_ref, p_ref, *rest):
        g_ref, d_ref, nm_ref, nv_ref = rest[-4:]
        g = p_ref[0].astype(F32)
        for p in range(1, P):
            g = g + p_ref[p].astype(F32)
        d, nm, nv = _adamw_math(w_ref[...], g, m_ref[...], v_ref[...])
        g_ref[...], d_ref[...], nm_ref[...], nv_ref[...] = g, d, nm, nv

    lay = pl.BlockSpec((None, tr, C), lambda i: (layer, i, 0))
    in_specs = [lay, lay, lay, pl.BlockSpec((P, tr, C), lambda i: (0, i, 0))]
    ins = [w, m, v, parts]
    aliases = {}
    if prev is not None:
        in_specs += [pl.BlockSpec(memory_space=pl.ANY)] * 4
        ins += list(prev)
        aliases = {4 + k: k for k in range(4)}
    return pl.pallas_call(
        body, grid=(R // tr,), in_specs=in_specs, out_specs=[lay] * 4,
        out_shape=[jax.ShapeDtypeStruct(w.shape, F32)] * 4, input_output_aliases=aliases,
        compiler_params=_params(("parallel",), VMEM_LIMIT), name=name,
    )(*ins)


def _adamw_layers(w, m, v, parts_per_layer, name):
    nl = w.shape[0]
    shp = w.shape
    C = shp[-1]
    R = math.prod(shp[1:-1])
    w3, m3, v3 = (a.reshape(nl, R, C) for a in (w, m, v))
    out = None
    for j in range(nl):
        p = parts_per_layer[j]
        out = _adamw(w3, m3, v3, p.reshape(p.shape[0], R, C), j, out, f"{name}_{j}")
    return tuple(o.reshape(shp) for o in out)


def _gelu_tanh(x):
    return jax.nn.gelu(x, approximate=True)


def _pack(arrs):
    tile = SUBLANE * LANE
    out = []
    for a in arrs:
        f = a.reshape(-1).astype(F32)
        out.append(jnp.pad(f, (0, (-f.shape[0]) % tile)))
    return jnp.concatenate(out)


def _unpack(flat, shapes):
    tile = SUBLANE * LANE
    out, off = [], 0
    for s in shapes:
        n = math.prod(s)
        out.append(flat[off:off + n].reshape(s))
        off += n + (-n) % tile
    return out


def kernel(x, gla_norm, gla_w_in, gla_w_gate_up, gla_b_gate, gla_o_norm, gla_w_out, s5_norm, s5_w_in, s5_lam_re, s5_lam_im, s5_log_dt, s5_b_re, s5_b_im, s5_c_re, s5_c_im, s5_d, s5_w_out, mlp_norm, mlp_w_up, mlp_w_down, final_norm, loss_target, m_gla_norm, m_gla_w_in, m_gla_w_gate_up, m_gla_b_gate, m_gla_o_norm, m_gla_w_out, m_s5_norm, m_s5_w_in, m_s5_lam_re, m_s5_lam_im, m_s5_log_dt, m_s5_b_re, m_s5_b_im, m_s5_c_re, m_s5_c_im, m_s5_d, m_s5_w_out, m_mlp_norm, m_mlp_w_up, m_mlp_w_down, m_final_norm, v_gla_norm, v_gla_w_in, v_gla_w_gate_up, v_gla_b_gate, v_gla_o_norm, v_gla_w_out, v_s5_norm, v_s5_w_in, v_s5_lam_re, v_s5_lam_im, v_s5_log_dt, v_s5_b_re, v_s5_b_im, v_s5_c_re, v_s5_c_im, v_s5_d, v_s5_w_out, v_mlp_norm, v_mlp_w_up, v_mlp_w_down, v_final_norm):
    W = dict(gla_norm=gla_norm, gla_w_in=gla_w_in, gla_w_gate_up=gla_w_gate_up, gla_b_gate=gla_b_gate, gla_o_norm=gla_o_norm, gla_w_out=gla_w_out, s5_norm=s5_norm, s5_w_in=s5_w_in, s5_lam_re=s5_lam_re, s5_lam_im=s5_lam_im, s5_log_dt=s5_log_dt, s5_b_re=s5_b_re, s5_b_im=s5_b_im, s5_c_re=s5_c_re, s5_c_im=s5_c_im, s5_d=s5_d, s5_w_out=s5_w_out, mlp_norm=mlp_norm, mlp_w_up=mlp_w_up, mlp_w_down=mlp_w_down, final_norm=final_norm)
    M = dict(gla_norm=m_gla_norm, gla_w_in=m_gla_w_in, gla_w_gate_up=m_gla_w_gate_up, gla_b_gate=m_gla_b_gate, gla_o_norm=m_gla_o_norm, gla_w_out=m_gla_w_out, s5_norm=m_s5_norm, s5_w_in=m_s5_w_in, s5_lam_re=m_s5_lam_re, s5_lam_im=m_s5_lam_im, s5_log_dt=m_s5_log_dt, s5_b_re=m_s5_b_re, s5_b_im=m_s5_b_im, s5_c_re=m_s5_c_re, s5_c_im=m_s5_c_im, s5_d=m_s5_d, s5_w_out=m_s5_w_out, mlp_norm=m_mlp_norm, mlp_w_up=m_mlp_w_up, mlp_w_down=m_mlp_w_down, final_norm=m_final_norm)
    V = dict(gla_norm=v_gla_norm, gla_w_in=v_gla_w_in, gla_w_gate_up=v_gla_w_gate_up, gla_b_gate=v_gla_b_gate, gla_o_norm=v_gla_o_norm, gla_w_out=v_gla_w_out, s5_norm=v_s5_norm, s5_w_in=v_s5_w_in, s5_lam_re=v_s5_lam_re, s5_lam_im=v_s5_lam_im, s5_log_dt=v_s5_log_dt, s5_b_re=v_s5_b_re, s5_b_im=v_s5_b_im, s5_c_re=v_s5_c_re, s5_c_im=v_s5_c_im, s5_d=v_s5_d, s5_w_out=v_s5_w_out, mlp_norm=v_mlp_norm, mlp_w_up=v_mlp_w_up, mlp_w_down=v_mlp_w_down, final_norm=v_final_norm)
    names = list(W)
    big = ["gla_w_in", "gla_w_out", "s5_w_in", "s5_w_out", "mlp_w_up", "mlp_w_down"]
    small_sharded = {"gla_w_gate_up": 2, "s5_norm": 1, "s5_d": 1}
    small = [n for n in names if n not in big]

    _, L, D = x.shape
    n_gla, n_s5, depth = gla_norm.shape[0], s5_lam_re.shape[0], mlp_norm.shape[0]
    H = GLA_HEADS
    KW, VW = D // 2, D
    DK, DV = KW // H, VW // H
    IN = 2 * KW + 2 * VW + GLA_RANK
    INP = 2 * KW + 2 * VW + LANE
    SW = s5_lam_re.shape[1] * S5_GROUP
    G, N = s5_lam_re.shape[1], s5_lam_re.shape[2]
    nb = G // S5_GB
    dev = _slot(_place())

    wb = {n: W[n].astype(BF16) for n in big}
    sm_sh = _pack([W[n] for n in small_sharded])
    order = []
    for i in range(depth):
        j = i // 2
        order += [("gla_w_in", j), ("gla_w_out", j)] if i % 2 == 0 else [("s5_w_in", j), ("s5_w_out", j)]
        order += [("mlp_w_up", i), ("mlp_w_down", i)]
    gathered = _all_gather([sm_sh] + [wb[n][j] for n, j in order], name="gather_weights")
    sm_all = gathered[0]
    full = {key: g for key, g in zip(order, gathered[1:])}
    sm_parts = [_unpack(sm_all[d], [W[n].shape for n in small_sharded]) for d in range(N_DEV)]
    wgu_full = jnp.concatenate([p[0] for p in sm_parts], axis=2)
    s5n_full = jnp.concatenate([p[1] for p in sm_parts], axis=1)
    s5d_full = jnp.concatenate([p[2] for p in sm_parts], axis=1)

    def gla_weights(j):
        w_in = full["gla_w_in", j]
        w_in = jnp.transpose(w_in, (1, 0, 2)).reshape(D, IN)
        w_in = jnp.pad(w_in, ((0, 0), (0, INP - IN)))
        w_out = full["gla_w_out", j].reshape(VW, D)
        wgu = jnp.pad(wgu_full[j], ((0, LANE - GLA_RANK), (0, 0))).astype(BF16)
        return w_in, w_out, wgu

    grads = {}
    h = x[0]
    saved = []

    for i in range(depth):
        j = i // 2
        if i % 2 == 0:
            w_in, w_out, wgu = gla_weights(j)
            gn = gla_norm[j][None]
            hn = _rms_fwd(h, gn, f"gla{j}_norm")
            proj = _mm(hn, w_in, "nn", name=f"gla{j}_proj", tn=896)
            bg, on = gla_b_gate[j][None], gla_o_norm[j][None]
            y, st = _gla_fwd(proj, wgu, bg, on, H=H, DK=DK, DV=DV, name=f"gla{j}_mix")
            h_new = _mm(y, w_out, "nn", name=f"gla{j}_out", res=h)
            saved.append(("gla", dict(h=h, hn=hn, proj=proj, y=y, st=st, w_in=w_in, w_out=w_out, wgu=wgu, gn=gn, bg=bg, on=on)))
        else:
            w_in = full["s5_w_in", j].reshape(D, SW)
            w_out3 = full["s5_w_out", j]
            sn = s5n_full[j][None]
            hn = _rms_fwd(h, sn, f"s5{j}_norm")
            u = _mm(hn, w_in, "nn", name=f"s5{j}_in")
            lre, lim = s5_lam_re[j][:, None, :], s5_lam_im[j][:, None, :]
            ldt = s5_log_dt[j][:, None, None]
            brt, bit = jnp.swapaxes(s5_b_re[j], 1, 2), jnp.swapaxes(s5_b_im[j], 1, 2)
            ar, ai, bbr, bbi, pr, pi, a124 = _s5_params(lre, lim, ldt, brt, bit, f"s5{j}_params")
            flat = lambda t: jnp.swapaxes(t, 0, 1).reshape(t.shape[1], G * N)
            pr, pi, a124 = flat(pr), flat(pi), flat(a124)
            bur = _mm_bd(u, _bd(bbr), name=f"s5{j}_bu_re")
            bui = _mm_bd(u, _bd(bbi), name=f"s5{j}_bu_im")
            xr, xi = _s5_scan(bur, bui, a124, pr, pi, reverse=False, name=f"s5{j}_scan")
            crt, cit = jnp.swapaxes(s5_c_re[j], 1, 2), jnp.swapaxes(s5_c_im[j], 1, 2)
            cx = _mm_bd(xr, _bd(crt), name=f"s5{j}_cx_re")
            cx = _mm_bd(xi, _bd(-cit), name=f"s5{j}_cx_im", res=cx)
            dsk = s5d_full[j][None]

            def act(cxv, uv, dv):
                ypre = cxv + dv * uv
                return ypre, _gelu_tanh(ypre)
            ypre, yg = _rowmap(act, [cx, u], [dsk], [(SW, F32), (SW, BF16)], name=f"s5{j}_act")
            z = _mm_nn_cb(yg, w_out3, name=f"s5{j}_out")

            def glu(zv, hv):
                return (hv + zv[:, :D] * jax.nn.sigmoid(zv[:, D:]),)
            h_new = _rowmap(glu, [z, h], [], [(D, F32)], name=f"s5{j}_glu")[0]
            saved.append(("s5", dict(h=h, hn=hn, u=u, xr=xr, xi=xi, ypre=ypre, yg=yg, z=z, w_in=w_in, w_out3=w_out3, sn=sn,
                                     dsk=dsk, prm=(lre, lim, ldt, brt, bit), a124=a124, pr=pr, pi=pi, bbr=bbr, bbi=bbi)))
        h = h_new
        w_up3 = full["mlp_w_up", i]
        w_down = full["mlp_w_down", i].reshape(4 * D, D)
        mn = mlp_norm[i][None]
        hn = _rms_fwd(h, mn, f"mlp{i}_norm")
        z = _mm_nn_cb(hn, w_up3, name=f"mlp{i}_up")

        def sq_relu(zv):
            a = jnp.maximum(zv, 0.0)
            return (a * a,)
        s = _rowmap(sq_relu, [z], [], [(4 * D, BF16)], name=f"mlp{i}_act")[0]
        h_new = _mm(s, w_down, "nn", name=f"mlp{i}_down", res=h)
        saved.append(("mlp", dict(h=h, hn=hn, z=z, s=s, w_up3=w_up3, w_down=w_down, mn=mn)))
        h = h_new

    dh, sq, dfin = _loss_head(h, loss_target[0], final_norm[None], "loss_head")
    loss = lax.psum(0.5 * jnp.sum(sq) / D, ("x", "y", "c"))
    small_grads = {"final_norm": dfin[0]}
    big_parts = {n: {} for n in big}
    stacks = {n: {} for n in small if n != "final_norm"}

    for idx in range(len(saved) - 1, -1, -1):
        kind, s = saved[idx]
        li = sum(1 for k, _ in saved[:idx] if k == kind)
        if kind == "mlp":
            ds = _mm(dh, s["w_down"], "nt", name=f"mlp{li}_dact")

            def sq_relu_bwd(zv, dsv):
                return (dsv * 2.0 * jnp.maximum(zv, 0.0),)
            dz = _rowmap(sq_relu_bwd, [s["z"], ds], [], [(4 * D, BF16)], name=f"mlp{li}_dz")[0]
            big_parts["mlp_w_down"][li] = _mm(s["s"], dh, "tn", name=f"mlp{li}_dw_down", out_dtype=BF16).reshape(N_DEV, 4 * D // N_DEV, D)
            big_parts["mlp_w_up"][li] = _mm_tn_cbout(s["hn"], dz, N_DEV, name=f"mlp{li}_dw_up")
            dhn = _mm_nt_cb(dz, s["w_up3"], name=f"mlp{li}_dhn")
            dh, dg = _rms_bwd(s["h"], dhn, dh, s["mn"], f"mlp{li}_dnorm")
            stacks["mlp_norm"][li] = dg[0]
        elif kind == "gla":
            dyv = _mm(dh, s["w_out"], "nt", name=f"gla{li}_dy")
            big_parts["gla_w_out"][li] = _mm(s["y"], dh, "tn", name=f"gla{li}_dw_out", out_dtype=BF16).reshape(N_DEV, VW // N_DEV, D)
            dqkvr, dpre, don = _gla_bwd(s["proj"], s["wgu"], s["bg"], s["on"], s["st"], dyv, H=H, DK=DK, DV=DV, name=f"gla{li}_dmix")
            dglow = _mm(dpre, s["wgu"], "nt", name=f"gla{li}_dglow", out_dtype=BF16)
            glow = s["proj"][:, 2 * KW + 2 * VW:]
            dwgu = _mm(glow, dpre, "tn", name=f"gla{li}_dwgu")[:GLA_RANK]
            dbg = _rowmap(lambda t: (jnp.sum(t, axis=0, keepdims=True),), [dpre], [], [], [KW], name=f"gla{li}_dbg")[0]
            dproj = jnp.concatenate([dqkvr, dglow], axis=1)
            dw_in = _mm(s["hn"], dproj, "tn", name=f"gla{li}_dw_in", out_dtype=BF16, tn=896)[:, :IN]
            big_parts["gla_w_in"][li] = jnp.transpose(dw_in.reshape(D, N_DEV, IN // N_DEV), (1, 0, 2))
            dhn = _mm(dproj, s["w_in"], "nt", name=f"gla{li}_dhn", tk=896)
            dh, dg = _rms_bwd(s["h"], dhn, dh, s["gn"], f"gla{li}_dnorm")
            stacks["gla_norm"][li], stacks["gla_b_gate"][li], stacks["gla_o_norm"][li] = dg[0], dbg[0], don[0]
            stacks["gla_w_gate_up"][li] = dwgu
        else:
            def glu_bwd(zv, dhv):
                a, sg = zv[:, :D], jax.nn.sigmoid(zv[:, D:])
                return (jnp.concatenate([dhv * sg, dhv * a * sg * (1.0 - sg)], axis=1),)
            dz = _rowmap(glu_bwd, [s["z"], dh], [], [(2 * D, BF16)], name=f"s5{li}_dglu")[0]
            big_parts["s5_w_out"][li] = _mm_tn_cbout(s["yg"], dz, N_DEV, name=f"s5{li}_dw_out")
            dyg = _mm_nt_cb(dz, s["w_out3"], name=f"s5{li}_dyg")

            def act_bwd(dygv, ypv, uv, dv):
                _, vjp = jax.vjp(_gelu_tanh, ypv)
                dyp = vjp(dygv)[0]
                return dyp, dyp * dv, jnp.sum(dyp * uv, axis=0, keepdims=True)
            dyp, du, dd = _rowmap(act_bwd, [dyg, s["ypre"], s["u"]], [s["dsk"]], [(SW, F32), (SW, F32)], [SW], name=f"s5{li}_dact")
            cre, cim = s5_c_re[li], s5_c_im[li]
            wr = _mm_bd(dyp, _bd(cre), name=f"s5{li}_w_re")
            wi = _mm_bd(dyp, _bd(-cim), name=f"s5{li}_w_im")
            a124c = s["a124"] * jnp.array([1, -1, 1, -1, 1, -1, 1, 1], F32)[:, None]
            lr_, li_ = _s5_scan(wr, wi, a124c, s["pr"][::-1], -s["pi"][::-1], reverse=True, name=f"s5{li}_dscan")
            du = _mm_bd(lr_, _bd(jnp.swapaxes(s["bbr"], 1, 2)), name=f"s5{li}_du_re", res=du)
            du = _mm_bd(li_, _bd(jnp.swapaxes(s["bbi"], 1, 2)), name=f"s5{li}_du_im", res=du)
            dbbr = _bd_extract(_mm_tn_bd(s["u"], lr_, nb, name=f"s5{li}_dbb_re"), S5_GROUP, N)
            dbbi = _bd_extract(_mm_tn_bd(s["u"], li_, nb, name=f"s5{li}_dbb_im"), S5_GROUP, N)
            dcr = _bd_extract(_mm_tn_bd(dyp, s["xr"], nb, name=f"s5{li}_dc_re"), S5_GROUP, N)
            dci = -_bd_extract(_mm_tn_bd(dyp, s["xi"], nb, name=f"s5{li}_dc_im"), S5_GROUP, N)
            dar, dai = _s5_dabar(lr_, li_, s["xr"], s["xi"], f"s5{li}_dabar")
            dlre, dlim, dldt, dbrt, dbit = _s5_params_bwd(*s["prm"], dar.reshape(G, 1, N), dai.reshape(G, 1, N), dbbr, dbbi,
                                                         f"s5{li}_dparams")
            big_parts["s5_w_in"][li] = _mm(s["hn"], du, "tn", name=f"s5{li}_dw_in", out_dtype=BF16).reshape(N_DEV, D // N_DEV, SW)
            dhn = _mm(du, s["w_in"], "nt", name=f"s5{li}_dhn")
            dh, dg = _rms_bwd(s["h"], dhn, dh, s["sn"], f"s5{li}_dnorm")
            stacks["s5_norm"][li], stacks["s5_d"][li] = dg[0], dd[0]
            stacks["s5_lam_re"][li], stacks["s5_lam_im"][li], stacks["s5_log_dt"][li] = dlre[:, 0], dlim[:, 0], dldt[:, 0, 0]
            stacks["s5_b_re"][li], stacks["s5_b_im"][li] = jnp.swapaxes(dbrt, 1, 2), jnp.swapaxes(dbit, 1, 2)
            stacks["s5_c_re"][li], stacks["s5_c_im"][li] = dcr, dci
    grad_x = dh[None]

    for n, st_ in stacks.items():
        small_grads[n] = jnp.stack([st_[k] for k in range(len(st_))])
    full_shapes = [small_grads[n].shape for n in small]
    part = _pack([small_grads[n] for n in small])
    parts_all = _all_gather([part], name="gather_small_grads")[0]

    def mine(n, a):
        if n not in small_sharded:
            return a
        ax = small_sharded[n]
        size = W[n].shape[ax]
        return lax.dynamic_slice_in_dim(a, dev * size, size, axis=ax)

    rows = parts_all.shape[1] // LANE
    def full_layout(T):
        arrs = []
        for n, shp in zip(small, full_shapes):
            if n in small_sharded:
                ax = small_sharded[n]
                arrs.append(lax.dynamic_update_slice_in_dim(jnp.zeros(shp, F32), T[n], dev * T[n].shape[ax], axis=ax))
            else:
                arrs.append(T[n])
        return _pack(arrs)
    ws, ms, vs = full_layout(W), full_layout(M), full_layout(V)
    sg, sd, snm, snv = _adamw(ws.reshape(1, rows, LANE), ms.reshape(1, rows, LANE), vs.reshape(1, rows, LANE),
                              parts_all.reshape(N_DEV, rows, LANE), 0, None, "adamw_small")
    outs = {}
    for key, flat in zip(("grad", "delta", "new_m", "new_v"), (sg, sd, snm, snv)):
        for n, a in zip(small, _unpack(flat.reshape(-1), full_shapes)):
            outs[key, n] = mine(n, a)

    keys = [(n, j) for n in big for j in range(W[n].shape[0])]
    recv = _all_to_all([big_parts[n][j] for n, j in keys], name="exchange_grads")
    got = {k: r for k, r in zip(keys, recv)}
    for n in big:
        res = _adamw_layers(W[n], M[n], V[n], [got[n, j] for j in range(W[n].shape[0])], f"adamw_{n}")
        for key, a in zip(("grad", "delta", "new_m", "new_v"), res):
            outs[key, n] = a

    return (loss, grad_x, *[outs["grad", n] for n in names], *[outs["delta", n] for n in names],
            *[outs["new_m", n] for n in names], *[outs["new_v", n] for n in names])
```

```python
import functools
import math

import jax
import jax.numpy as jnp
from jax import lax
from jax.experimental import pallas as pl
from jax.experimental.pallas import tpu as pltpu

F32, BF16 = jnp.float32, jnp.bfloat16
MESH = pl.DeviceIdType.MESH
N_DEV = 8
LANE = 128
SUBLANE = 8
VMEM_LIMIT = 48 * 1024 * 1024

EPS = 1e-6
CHUNK = 64
GLA_HEADS = 4
GLA_RANK = 16
GLA_TEMP = 16.0
S5_GROUP = 16
S5_STATE = 64
S5_EIG_CLIP = -1e-4
S5_GB = 16
ADAM_LR, ADAM_B1, ADAM_B2, ADAM_EPS, ADAM_WD, ADAM_STEP = 0.001, 0.9, 0.999, 1e-08, 0.01, 10

_CONTRACT = {"nn": ((1,), (0,)), "tn": ((0,), (0,)), "nt": ((1,), (1,))}


def _tile(n, pref, unit=LANE):
    if n <= pref:
        return n
    t = (pref // unit) * unit
    while t > unit and n % t:
        t -= unit
    assert n % t == 0, (n, pref, unit)
    return t


def _params(sem, vmem=None):
    return pltpu.CompilerParams(dimension_semantics=sem, vmem_limit_bytes=vmem)


def _dot(a, b, dims, precision=None):
    return lax.dot_general(a, b, (_CONTRACT[dims], ((), ())), preferred_element_type=F32, precision=precision)


def _bdot(a, b, dims):
    return _dot(a.astype(BF16), b.astype(BF16), dims)


def _mm_call(a, b, *, dims, grid, a_spec, b_spec, o_spec, out_shape, out_dtype, name, res=None, epi=None, epi_ins=()):
    nk = grid[2]
    acc_shape = tuple(d for d in o_spec.block_shape if d is not None)
    if res is not None:
        epi, epi_ins = (lambda r, x: (r + x,)), (res,)
    single = not isinstance(out_dtype, (list, tuple))
    out_dtypes = [out_dtype] if single else list(out_dtype)
    n_e, n_o = len(epi_ins), len(out_dtypes)

    def body(*refs):
        a_ref, b_ref = refs[:2]
        e_refs, o_refs = refs[2:2 + n_e], refs[2 + n_e:2 + n_e + n_o]

        def finish(r):
            vals = (r,) if epi is None else epi(r, *[e[...].astype(F32) for e in e_refs])
            for o_ref, v in zip(o_refs, vals):
                o_ref[...] = v.astype(o_ref.dtype)

        d = _bdot(a_ref[...], b_ref[...], dims)
        if nk == 1:
            finish(d)
            return
        acc = refs[-1]
        k = pl.program_id(2)

        @pl.when(k == 0)
        def _():
            acc[...] = d

        @pl.when((k > 0) & (k < nk - 1))
        def _():
            acc[...] += d

        @pl.when(k == nk - 1)
        def _():
            finish(acc[...] + d)

    outs = pl.pallas_call(
        body, grid=grid, in_specs=[a_spec, b_spec] + [o_spec] * n_e, out_specs=[o_spec] * n_o,
        out_shape=[jax.ShapeDtypeStruct(out_shape, dt) for dt in out_dtypes],
        scratch_shapes=[] if nk == 1 else [pltpu.VMEM(acc_shape, F32)],
        compiler_params=_params(("parallel", "parallel", "arbitrary"), VMEM_LIMIT), name=name,
    )(a, b, *epi_ins)
    return outs[0] if single else outs


TM, TN, TK = 1024, 1024, 2048


def _mm(a, b, dims, *, name, out_dtype=F32, res=None, epi=None, epi_ins=(), tm=TM, tn=TN, tk=TK):
    if dims == "tn":
        (K, M), (_, N) = a.shape, b.shape
    elif dims == "nn":
        (M, K), (_, N) = a.shape, b.shape
    else:
        (M, K), (N, _) = a.shape, b.shape
    tm, tn, tk = _tile(M, tm), _tile(N, tn), _tile(K, tk)
    a_spec = pl.BlockSpec((tk, tm), lambda i, j, k: (k, i)) if dims == "tn" else pl.BlockSpec((tm, tk), lambda i, j, k: (i, k))
    b_spec = pl.BlockSpec((tn, tk), lambda i, j, k: (j, k)) if dims == "nt" else pl.BlockSpec((tk, tn), lambda i, j, k: (k, j))
    o_spec = pl.BlockSpec((tm, tn), lambda i, j, k: (i, j))
    return _mm_call(a, b, dims=dims, grid=(M // tm, N // tn, K // tk), a_spec=a_spec, b_spec=b_spec, o_spec=o_spec,
                    out_shape=(M, N), out_dtype=out_dtype, name=name, res=res, epi=epi, epi_ins=epi_ins)


def _mm_nn_cb(a, b3, *, name, out_dtype=F32, epi=None, tm=TM, tn=TN, tk=TK):
    (M, K), (P, _, Ns) = a.shape, b3.shape
    tm, tn, tk = _tile(M, tm), _tile(Ns, tn), _tile(K, tk)
    npb = Ns // tn
    return _mm_call(a, b3, dims="nn", grid=(M // tm, P * npb, K // tk),
                    a_spec=pl.BlockSpec((tm, tk), lambda i, j, k: (i, k)),
                    b_spec=pl.BlockSpec((None, tk, tn), lambda i, j, k: (j // npb, k, j % npb)),
                    o_spec=pl.BlockSpec((tm, tn), lambda i, j, k: (i, j)),
                    out_shape=(M, P * Ns), out_dtype=out_dtype, name=name, epi=epi)


def _mm_nt_cb(a, b3, *, name, out_dtype=F32, tm=TM, tn=TN, tk=TK):
    (M, _), (P, N, Ns) = a.shape, b3.shape
    tm, tn, tk = _tile(M, tm), _tile(N, tn), _tile(Ns, tk)
    kpb = Ns // tk
    return _mm_call(a, b3, dims="nt", grid=(M // tm, N // tn, P * kpb),
                    a_spec=pl.BlockSpec((tm, tk), lambda i, j, k: (i, k)),
                    b_spec=pl.BlockSpec((None, tn, tk), lambda i, j, k: (k // kpb, j, k % kpb)),
                    o_spec=pl.BlockSpec((tm, tn), lambda i, j, k: (i, j)),
                    out_shape=(M, N), out_dtype=out_dtype, name=name)


def _mm_tn_cbout(a, b, parts, *, name, out_dtype=BF16, tm=TM, tn=TN, tk=TK):
    (K, M), (_, N) = a.shape, b.shape
    Ns = N // parts
    tm, tn, tk = _tile(M, tm), _tile(Ns, tn), _tile(K, tk)
    npb = Ns // tn
    return _mm_call(a, b, dims="tn", grid=(M // tm, parts * npb, K // tk),
                    a_spec=pl.BlockSpec((tk, tm), lambda i, j, k: (k, i)),
                    b_spec=pl.BlockSpec((tk, tn), lambda i, j, k: (k, j)),
                    o_spec=pl.BlockSpec((None, tm, tn), lambda i, j, k: (j // npb, i, j % npb)),
                    out_shape=(parts, M, Ns), out_dtype=out_dtype, name=name)


def _mm_bd(a, w3, *, name, out_dtype=F32, res=None, tm=2048, tn=1024):
    (M, _), (nb, Kb, Nb) = a.shape, w3.shape
    tm, tn = _tile(M, tm), _tile(Nb, tn)
    npb = Nb // tn
    return _mm_call(a, w3, dims="nn", grid=(M // tm, nb * npb, 1),
                    a_spec=pl.BlockSpec((tm, Kb), lambda i, j, k: (i, j // npb)),
                    b_spec=pl.BlockSpec((None, Kb, tn), lambda i, j, k: (j // npb, 0, j % npb)),
                    o_spec=pl.BlockSpec((tm, tn), lambda i, j, k: (i, j)),
                    out_shape=(M, nb * Nb), out_dtype=out_dtype, name=name, res=res)


def _mm_tn_bd(a, b, nb, *, name, tk=1024):
    (K, MA), (_, NB) = a.shape, b.shape
    Ma, Nb = MA // nb, NB // nb
    tk = _tile(K, tk)
    return _mm_call(a, b, dims="tn", grid=(nb, 1, K // tk),
                    a_spec=pl.BlockSpec((tk, Ma), lambda i, j, k: (k, i)),
                    b_spec=pl.BlockSpec((tk, Nb), lambda i, j, k: (k, i)),
                    o_spec=pl.BlockSpec((None, Ma, Nb), lambda i, j, k: (i, 0, 0)),
                    out_shape=(nb, Ma, Nb), out_dtype=F32, name=name)


def _rowmap(fn, rows, consts, out_defs, red_defs=(), *, name, tr=256):
    L = rows[0].shape[0]
    widest = max([r.shape[1] for r in rows] + [n for n, _ in out_defs])
    tr = _tile(L, max(SUBLANE * 2, min(tr, 512 * 1024 // widest)), SUBLANE * 2)
    n_in, n_o, n_d = len(rows) + len(consts), len(out_defs), len(red_defs)

    def body(*refs):
        res = fn(*[r[...] for r in refs[:n_in]])
        res = res if isinstance(res, (tuple, list)) else (res,)
        outs = refs[n_in:]
        for o_ref, val in zip(outs[:n_o], res[:n_o]):
            o_ref[...] = val.astype(o_ref.dtype)
        if n_d:
            @pl.when(pl.program_id(0) == 0)
            def _():
                for o_ref in outs[n_o:]:
                    o_ref[...] = jnp.zeros_like(o_ref)
            for o_ref, val in zip(outs[n_o:], res[n_o:]):
                o_ref[...] += val

    in_specs = [pl.BlockSpec((tr, r.shape[1]), lambda i: (i, 0)) for r in rows]
    in_specs += [pl.BlockSpec(c.shape, lambda i, nd=c.ndim: (0,) * nd) for c in consts]
    out_specs = [pl.BlockSpec((tr, n), lambda i: (i, 0)) for n, _ in out_defs]
    out_specs += [pl.BlockSpec((1, n), lambda i: (0, 0)) for n in red_defs]
    out_shape = [jax.ShapeDtypeStruct((L, n), dt) for n, dt in out_defs]
    out_shape += [jax.ShapeDtypeStruct((1, n), F32) for n in red_defs]
    return pl.pallas_call(body, grid=(L // tr,), in_specs=in_specs, out_specs=out_specs, out_shape=out_shape,
                          compiler_params=_params(("arbitrary",), VMEM_LIMIT), name=name)(*rows, *consts)


def _rms_parts(x):
    r = lax.rsqrt(jnp.mean(x * x, axis=-1, keepdims=True) + EPS)
    return r, x * r


def _rms_fwd(h, g, name):
    def fn(x, gg):
        _, xh = _rms_parts(x)
        return (xh * gg,)
    return _rowmap(fn, [h], [g], [(h.shape[1], BF16)], name=name)[0]


def _rms_bwd(h, dhn, dh, g, name):
    def fn(x, dy, dres, gg):
        r, xh = _rms_parts(x)
        dxh = dy * gg
        dx = r * (dxh - xh * jnp.mean(dxh * xh, axis=-1, keepdims=True))
        return dres + dx, jnp.sum(dy * xh, axis=0, keepdims=True)
    D = h.shape[1]
    return _rowmap(fn, [h, dhn, dh], [g], [(D, F32)], [D], name=name)


def _loss_head(h, tgt, g, name):
    D = h.shape[1]

    def fn(x, t, gg):
        r, xh = _rms_parts(x)
        diff = xh * gg - t
        dy = diff * (1.0 / D)
        dxh = dy * gg
        dx = r * (dxh - xh * jnp.mean(dxh * xh, axis=-1, keepdims=True))
        return dx, jnp.sum(diff * diff, axis=0, keepdims=True), jnp.sum(dy * xh, axis=0, keepdims=True)
    return _rowmap(fn, [h, tgt], [g], [(D, F32)], [D, D], name=name)


def _tri(n, strict):
    r = lax.broadcasted_iota(jnp.int32, (n, n), 0)
    c = lax.broadcasted_iota(jnp.int32, (n, n), 1)
    return jnp.where((c < r) if strict else (c <= r), 1.0, 0.0).astype(F32)


def _gla_gate(g_ref, wgu_ref, bg_ref):
    pre = _bdot(g_ref[...], wgu_ref[...], "nn") + bg_ref[...]
    la = (jnp.minimum(pre, 0.0) - jnp.log(1.0 + jnp.exp(-jnp.abs(pre)))) * (1.0 / GLA_TEMP)
    cum = _dot(_tri(CHUNK, False), la, "nn", lax.Precision.HIGHEST)
    return pre, cum, cum[CHUNK - 1:CHUNK, :]


def _gla_specs(H, DK, DV, cmap):
    kb, vb, gb = H, (2 * H * DK) // DV, (2 * H * DK + 2 * H * DV) // LANE
    return [
        pl.BlockSpec((CHUNK, DK), lambda h, c: (cmap(h, c), h)),
        pl.BlockSpec((CHUNK, DK), lambda h, c: (cmap(h, c), kb + h)),
        pl.BlockSpec((CHUNK, DV), lambda h, c: (cmap(h, c), vb + h)),
        pl.BlockSpec((CHUNK, DV), lambda h, c: (cmap(h, c), vb + H + h)),
        pl.BlockSpec((CHUNK, LANE), lambda h, c: (cmap(h, c), gb)),
        pl.BlockSpec((LANE, DK), lambda h, c: (0, h)),
        pl.BlockSpec((1, DK), lambda h, c: (0, h)),
        pl.BlockSpec((1, DV), lambda h, c: (0, 0)),
    ]


def _gla_fwd(proj, wgu, bg, on, *, H, DK, DV, name):
    L = proj.shape[0]
    nc = L // CHUNK
    scale = DK ** -0.5

    def body(q_ref, k_ref, v_ref, r_ref, g_ref, wgu_ref, bg_ref, on_ref, y_ref, st_ref, S):
        @pl.when(pl.program_id(1) == 0)
        def _():
            S[...] = jnp.zeros_like(S)
        _, cum, total = _gla_gate(g_ref, wgu_ref, bg_ref)
        kd = k_ref[...] * jnp.exp(total - cum)
        St = S[...] * jnp.exp(total) + _bdot(v_ref[...], kd, "tn")
        S[...] = St
        st_ref[...] = St
        o = _bdot(q_ref[...] * scale, St, "nt")
        _, oh = _rms_parts(o)
        y_ref[...] = (oh * on_ref[...] * jax.nn.silu(r_ref[...])).astype(y_ref.dtype)

    return pl.pallas_call(
        body, grid=(H, nc), in_specs=_gla_specs(H, DK, DV, lambda h, c: c),
        out_specs=[pl.BlockSpec((CHUNK, DV), lambda h, c: (c, h)),
                   pl.BlockSpec((None, None, DV, DK), lambda h, c: (h, c, 0, 0))],
        out_shape=[jax.ShapeDtypeStruct((L, H * DV), BF16), jax.ShapeDtypeStruct((H, nc, DV, DK), F32)],
        scratch_shapes=[pltpu.VMEM((DV, DK), F32)],
        compiler_params=_params(("arbitrary", "arbitrary"), VMEM_LIMIT), name=name,
    )(proj, proj, proj, proj, proj, wgu, bg, on)


def _gla_bwd(proj, wgu, bg, on, st, dy, *, H, DK, DV, name):
    L = proj.shape[0]
    nc = L // CHUNK
    scale = DK ** -0.5
    rev = lambda h, c: nc - 1 - c

    def body(q_ref, k_ref, v_ref, r_ref, g_ref, wgu_ref, bg_ref, on_ref, sc_ref, sp_ref, dy_ref,
             dq_ref, dk_ref, dv_ref, dr_ref, dpre_ref, don_ref, G, decn):
        h, c = pl.program_id(0), pl.program_id(1)

        @pl.when(c == 0)
        def _():
            G[...] = jnp.zeros_like(G)
            decn[...] = jnp.zeros_like(decn)

        @pl.when((c == 0) & (h == 0))
        def _():
            don_ref[...] = jnp.zeros_like(don_ref)

        pre, cum, total = _gla_gate(g_ref, wgu_ref, bg_ref)
        ex = jnp.exp(total - cum)
        k, v, r = k_ref[...], v_ref[...], r_ref[...]
        kd = k * ex
        dec = jnp.exp(total)
        qs = q_ref[...] * scale
        Sc = sc_ref[...]
        o = _bdot(qs, Sc, "nt")
        rinv, oh = _rms_parts(o)
        gn = on_ref[...]
        sg = jax.nn.sigmoid(r)
        dyv = dy_ref[...]
        d_on = dyv * (r * sg)
        dr_ref[...] = (dyv * (oh * gn) * (sg * (1.0 + r * (1.0 - sg)))).astype(dr_ref.dtype)
        don_ref[...] += jnp.sum(d_on * oh, axis=0, keepdims=True)
        dxh = d_on * gn
        do = rinv * (dxh - oh * jnp.mean(dxh * oh, axis=-1, keepdims=True))
        dq_ref[...] = (_bdot(do, Sc, "nn") * scale).astype(dq_ref.dtype)
        Gt = G[...] * decn[...] + _bdot(do, qs, "tn")
        G[...] = Gt
        decn[...] = dec
        dkd = _bdot(v, Gt, "nn")
        dv_ref[...] = _bdot(kd, Gt, "nt").astype(dv_ref.dtype)
        Sp = sp_ref[...] * jnp.where(c == nc - 1, 0.0, 1.0)
        ddec = jnp.sum(Gt * Sp, axis=0, keepdims=True)
        dk_ref[...] = (dkd * ex).astype(dk_ref.dtype)
        dla = ddec * dec + _dot(_tri(CHUNK, True), dkd * kd, "nn", lax.Precision.HIGHEST)
        dpre_ref[...] = dla * (1.0 / GLA_TEMP) * jax.nn.sigmoid(-pre)

    in_specs = _gla_specs(H, DK, DV, rev) + [
        pl.BlockSpec((None, None, DV, DK), lambda h, c: (h, rev(h, c), 0, 0)),
        pl.BlockSpec((None, None, DV, DK), lambda h, c: (h, jnp.maximum(rev(h, c) - 1, 0), 0, 0)),
        pl.BlockSpec((CHUNK, DV), lambda h, c: (rev(h, c), h)),
    ]
    blk_k = pl.BlockSpec((CHUNK, DK), lambda h, c: (rev(h, c), h))
    blk_v = pl.BlockSpec((CHUNK, DV), lambda h, c: (rev(h, c), h))
    out_specs = [blk_k, blk_k, blk_v, blk_v, blk_k, pl.BlockSpec((1, DV), lambda h, c: (0, 0))]
    sds = jax.ShapeDtypeStruct
    out_shape = [sds((L, H * DK), BF16), sds((L, H * DK), BF16), sds((L, H * DV), BF16), sds((L, H * DV), BF16),
                 sds((L, H * DK), F32), sds((1, DV), F32)]
    dq, dk, dv, dr, dpre, don = pl.pallas_call(
        body, grid=(H, nc), in_specs=in_specs, out_specs=out_specs, out_shape=out_shape,
        scratch_shapes=[pltpu.VMEM((DV, DK), F32), pltpu.VMEM((1, DK), F32)],
        compiler_params=_params(("arbitrary", "arbitrary"), VMEM_LIMIT), name=name,
    )(proj, proj, proj, proj, proj, wgu, bg, on, st, st, dy)
    dqkvr = jnp.concatenate([dq, dk, dv, dr], axis=1)
    return dqkvr, dpre, don


def _s5_param_fn(lam_re, lam_im, log_dt, brt, bit):
    lr = jnp.minimum(lam_re, S5_EIG_CLIP)
    li = lam_im
    dt = jnp.exp(log_dt)
    mag = jnp.exp(lr * dt)
    ang = li * dt
    ab_re = mag * jnp.cos(ang)
    ab_im = mag * jnp.sin(ang)
    den = lr * lr + li * li
    nr = ab_re - 1.0
    f_re = (nr * lr + ab_im * li) / den
    f_im = (ab_im * lr - nr * li) / den
    return ab_re, ab_im, f_re * brt - f_im * bit, f_re * bit + f_im * brt


def _s5_params(lam_re, lam_im, log_dt, brt, bit, name):
    G, _, N = lam_re.shape

    def body(lr_ref, li_ref, dt_ref, br_ref, bi_ref, ar_ref, ai_ref, bbr_ref, bbi_ref, pr_ref, pi_ref, a124_ref):
        lre, lim, ldt = lr_ref[...], li_ref[...], dt_ref[...]
        ar, ai, bbr, bbi = _s5_param_fn(lre, lim, ldt, br_ref[...], bi_ref[...])
        ar_ref[...], ai_ref[...], bbr_ref[...], bbi_ref[...] = ar, ai, bbr, bbi
        kk = (lax.broadcasted_iota(jnp.int32, (1, SUBLANE, 1), 1) + 1).astype(F32)
        dt = jnp.exp(ldt)
        mag = jnp.exp(kk * (jnp.minimum(lre, S5_EIG_CLIP) * dt))
        ang = kk * (lim * dt)
        pr_ref[...] = mag * jnp.cos(ang)
        pi_ref[...] = mag * jnp.sin(ang)
        r = lax.broadcasted_iota(jnp.int32, (1, SUBLANE, 1), 1)
        k2 = jnp.where(r < 2, 1.0, jnp.where(r < 4, 2.0, jnp.where(r < 6, 4.0, 0.0)))
        mag2 = jnp.exp(k2 * (jnp.minimum(lre, S5_EIG_CLIP) * dt))
        ang2 = k2 * (lim * dt)
        a124_ref[...] = mag2 * jnp.where(r % 2 == 0, jnp.cos(ang2), jnp.sin(ang2))

    sds = jax.ShapeDtypeStruct
    return pl.pallas_call(
        body, out_shape=[sds((G, 1, N), F32), sds((G, 1, N), F32), sds(brt.shape, F32), sds(brt.shape, F32),
                         sds((G, SUBLANE, N), F32), sds((G, SUBLANE, N), F32), sds((G, SUBLANE, N), F32)], name=name,
    )(lam_re, lam_im, log_dt, brt, bit)


def _s5_params_bwd(lam_re, lam_im, log_dt, brt, bit, dar, dai, dbbr, dbbi, name):
    def body(lr_ref, li_ref, dt_ref, br_ref, bi_ref, dar_ref, dai_ref, dbbr_ref, dbbi_ref, *outs):
        _, vjp = jax.vjp(_s5_param_fn, lr_ref[...], li_ref[...], dt_ref[...], br_ref[...], bi_ref[...])
        for o_ref, val in zip(outs, vjp((dar_ref[...], dai_ref[...], dbbr_ref[...], dbbi_ref[...]))):
            o_ref[...] = val

    ins = (lam_re, lam_im, log_dt, brt, bit)
    return pl.pallas_call(body, out_shape=[jax.ShapeDtypeStruct(a.shape, F32) for a in ins], name=name)(
        *ins, dar, dai, dbbr, dbbi)


def _s5_scan(br, bi, a124, pr, pi, *, reverse, name, W=256):
    L, n = br.shape
    W = _tile(n, W)
    nblk = L // SUBLANE

    def body(br_ref, bi_ref, a_ref, pr_ref, pi_ref, xr_ref, xi_ref):
        A = a_ref[...]
        PR, PI = pr_ref[...], pi_ref[...]
        row = lax.broadcasted_iota(jnp.int32, (SUBLANE, W), 0)
        last = 0 if reverse else SUBLANE - 1

        def step(i, carry):
            cr, ci = carry
            off = pl.multiple_of(((nblk - 1 - i) if reverse else i) * SUBLANE, SUBLANE)
            xr, xi = br_ref[pl.ds(off, SUBLANE), :], bi_ref[pl.ds(off, SUBLANE), :]
            for j, k in enumerate((1, 2, 4)):
                ar, ai = A[2 * j:2 * j + 1, :], A[2 * j + 1:2 * j + 2, :]
                keep = (row < SUBLANE - k) if reverse else (row >= k)
                shift = (SUBLANE - k) if reverse else k
                sr = jnp.where(keep, pltpu.roll(xr, shift, 0), 0.0)
                si = jnp.where(keep, pltpu.roll(xi, shift, 0), 0.0)
                xr, xi = xr + ar * sr - ai * si, xi + ar * si + ai * sr
            xr, xi = xr + PR * cr - PI * ci, xi + PR * ci + PI * cr
            xr_ref[pl.ds(off, SUBLANE), :] = xr
            xi_ref[pl.ds(off, SUBLANE), :] = xi
            return xr[last:last + 1, :], xi[last:last + 1, :]

        z = jnp.zeros((1, W), F32)
        lax.fori_loop(0, nblk, step, (z, z))

    col = pl.BlockSpec((L, W), lambda j: (0, j))
    par = pl.BlockSpec((SUBLANE, W), lambda j: (0, j))
    return pl.pallas_call(
        body, grid=(n // W,), in_specs=[col, col, par, par, par], out_specs=[col, col],
        out_shape=[jax.ShapeDtypeStruct((L, n), F32)] * 2,
        compiler_params=_params(("parallel",), VMEM_LIMIT), name=name,
    )(br, bi, a124, pr, pi)


def _s5_dabar(lr, li, xr, xi, name, W=256):
    L, n = lr.shape
    W = _tile(n, W)

    def body(lr_ref, li_ref, xr_ref, xi_ref, dar_ref, dai_ref):
        first = lax.broadcasted_iota(jnp.int32, (L, W), 0) == 0
        pr = jnp.where(first, 0.0, pltpu.roll(xr_ref[...], 1, 0))
        pi = jnp.where(first, 0.0, pltpu.roll(xi_ref[...], 1, 0))
        a, b = lr_ref[...], li_ref[...]
        dar_ref[...] = jnp.sum(a * pr + b * pi, axis=0, keepdims=True)
        dai_ref[...] = jnp.sum(b * pr - a * pi, axis=0, keepdims=True)

    col = pl.BlockSpec((L, W), lambda j: (0, j))
    one = pl.BlockSpec((1, W), lambda j: (0, j))
    return pl.pallas_call(
        body, grid=(n // W,), in_specs=[col] * 4, out_specs=[one, one],
        out_shape=[jax.ShapeDtypeStruct((1, n), F32)] * 2,
        compiler_params=_params(("parallel",), VMEM_LIMIT), name=name,
    )(lr, li, xr, xi)


def _bd(w):
    G, A, B = w.shape
    w4 = w.reshape(G // S5_GB, S5_GB, A, B)
    eye = jnp.eye(S5_GB, dtype=w.dtype)
    return jnp.einsum("kgab,gh->kgahb", w4, eye).reshape(G // S5_GB, S5_GB * A, S5_GB * B).astype(BF16)


def _bd_extract(m, A, B):
    nb = m.shape[0]
    d = jnp.diagonal(m.reshape(nb, S5_GB, A, S5_GB, B), axis1=1, axis2=3)
    return jnp.moveaxis(d, 3, 1).reshape(nb * S5_GB, A, B)


def _place():
    return lax.axis_index("x"), lax.axis_index("y"), lax.axis_index("c")


def _slot(p):
    return 4 * p[0] + 2 * p[1] + p[2]


def _all_gather(xs, name):
    n = len(xs)

    def body(*refs):
        x_refs, o_refs = refs[:n], refs[n:2 * n]
        send_sems, recv_sems, local_sems = refs[2 * n:]
        x, y, c = _place()
        me, sib = (x, y, c), (x, y, 1 - c)
        chips = [(1 - x, y), (x, 1 - y), (1 - x, 1 - y)]

        def copy(a, k, block, to, src=None):
            dst = o_refs[a].at[_slot(block)]
            return pltpu.make_async_remote_copy(
                src_ref=dst if src is None else src, dst_ref=dst, send_sem=send_sems.at[a, k], recv_sem=recv_sems.at[a, k],
                device_id=to, device_id_type=MESH)

        mine = [pltpu.make_async_copy(x_refs[a], o_refs[a].at[_slot(me)], local_sems.at[a]) for a in range(n)]
        for cp in mine:
            cp.start()
        sent = []
        for a in range(n):
            sent.append(copy(a, 0, me, sib, src=x_refs[a]))
            sent += [copy(a, 1 + j, me, (*chip, c), src=x_refs[a]) for j, chip in enumerate(chips)]
        for cp in sent:
            cp.start()
        for j, chip in enumerate(chips):
            for a in range(n):
                copy(a, 1 + j, (*chip, c), me).wait_recv()
                fwd = copy(a, 4 + j, (*chip, c), sib)
                fwd.start()
                sent.append(fwd)
        for a in range(n):
            copy(a, 0, sib, me).wait_recv()
            for j, chip in enumerate(chips):
                copy(a, 4 + j, (*chip, 1 - c), me).wait_recv()
        for cp in sent:
            cp.wait_send()
        for cp in mine:
            cp.wait()

    any_spec = pl.BlockSpec(memory_space=pl.ANY)
    return pl.pallas_call(
        body, in_specs=[any_spec] * n, out_specs=[any_spec] * n,
        out_shape=[jax.ShapeDtypeStruct((N_DEV, *a.shape), a.dtype) for a in xs],
        scratch_shapes=[pltpu.SemaphoreType.DMA((n, 7)), pltpu.SemaphoreType.DMA((n, 7)), pltpu.SemaphoreType.DMA((n,))],
        name=name,
    )(*xs)


N_CHIP = N_DEV // 2


def _pair_exchange(xs, name):
    n = len(xs)

    def body(*refs):
        x_refs, o_refs = refs[:n], refs[n:2 * n]
        send_sems, recv_sems = refs[2 * n:]
        x, y, c = _place()
        sib = (x, y, 1 - c)

        def copy(a, q):
            return pltpu.make_async_remote_copy(
                src_ref=x_refs[a].at[2 * q + (1 - c)], dst_ref=o_refs[a].at[q],
                send_sem=send_sems.at[a, q], recv_sem=recv_sems.at[a, q], device_id=sib, device_id_type=MESH)

        cps = [copy(a, q) for a in range(n) for q in range(N_CHIP)]
        for cp in cps:
            cp.start()
        for cp in cps:
            cp.wait_recv()
        for cp in cps:
            cp.wait_send()

    any_spec = pl.BlockSpec(memory_space=pl.ANY)
    return pl.pallas_call(
        body, in_specs=[any_spec] * n, out_specs=[any_spec] * n,
        out_shape=[jax.ShapeDtypeStruct((N_CHIP, *a.shape[1:]), a.dtype) for a in xs],
        scratch_shapes=[pltpu.SemaphoreType.DMA((n, N_CHIP)), pltpu.SemaphoreType.DMA((n, N_CHIP))],
        name=name,
    )(*xs)


def _pair_sum(x, got, core, name):
    _, R, C = x.shape
    unit = SUBLANE * (4 // x.dtype.itemsize)
    tr = _tile(R, max(unit, (512 * 1024 // C) // unit * unit), unit)

    def body(c_ref, x_ref, g_ref, o_ref):
        o_ref[...] = (x_ref[...].astype(F32) + g_ref[...].astype(F32)).astype(o_ref.dtype)

    return pl.pallas_call(
        body,
        grid_spec=pltpu.PrefetchScalarGridSpec(
            num_scalar_prefetch=1, grid=(N_CHIP, R // tr),
            in_specs=[pl.BlockSpec((None, tr, C), lambda q, i, c_ref: (2 * q + c_ref[0], i, 0)),
                      pl.BlockSpec((None, tr, C), lambda q, i, c_ref: (q, i, 0))],
            out_specs=pl.BlockSpec((None, tr, C), lambda q, i, c_ref: (q, i, 0))),
        out_shape=jax.ShapeDtypeStruct(got.shape, x.dtype),
        compiler_params=_params(("parallel", "parallel"), VMEM_LIMIT), name=name,
    )(core, x, got)


def _chip_exchange(xs, name):
    n = len(xs)

    def body(*refs):
        x_refs, o_refs = refs[:n], refs[n:2 * n]
        send_sems, recv_sems, local_sems = refs[2 * n:]
        x, y, c = _place()
        myq = 2 * x + y
        chips = [(1 - x, y), (x, 1 - y), (1 - x, 1 - y)]

        def copy(a, k):
            px, py = chips[k]
            return pltpu.make_async_remote_copy(
                src_ref=x_refs[a].at[2 * px + py], dst_ref=o_refs[a].at[myq],
                send_sem=send_sems.at[a, k], recv_sem=recv_sems.at[a, k], device_id=(px, py, c), device_id_type=MESH)

        def arrival(a, k):
            px, py = chips[k]
            return pltpu.make_async_remote_copy(
                src_ref=x_refs[a].at[2 * px + py], dst_ref=o_refs[a].at[2 * px + py],
                send_sem=send_sems.at[a, k], recv_sem=recv_sems.at[a, k], device_id=(px, py, c), device_id_type=MESH)

        mine = [pltpu.make_async_copy(x_refs[a].at[myq], o_refs[a].at[myq], local_sems.at[a]) for a in range(n)]
        for cp in mine:
            cp.start()
        sent = [copy(a, k) for a in range(n) for k in range(3)]
        for cp in sent:
            cp.start()
        for a in range(n):
            for k in range(3):
                arrival(a, k).wait_recv()
        for cp in sent:
            cp.wait_send()
        for cp in mine:
            cp.wait()

    any_spec = pl.BlockSpec(memory_space=pl.ANY)
    return pl.pallas_call(
        body, in_specs=[any_spec] * n, out_specs=[any_spec] * n,
        out_shape=[jax.ShapeDtypeStruct(a.shape, a.dtype) for a in xs],
        scratch_shapes=[pltpu.SemaphoreType.DMA((n, 3)), pltpu.SemaphoreType.DMA((n, 3)), pltpu.SemaphoreType.DMA((n,))],
        name=name,
    )(*xs)


def _adamw_math(w, g, m, v):
    m = ADAM_B1 * m + (1.0 - ADAM_B1) * g
    v = ADAM_B2 * v + (1.0 - ADAM_B2) * (g * g)
    m_hat = m / (1.0 - ADAM_B1 ** ADAM_STEP)
    v_hat = v / (1.0 - ADAM_B2 ** ADAM_STEP)
    return -ADAM_LR * (m_hat / (jnp.sqrt(v_hat) + ADAM_EPS) + ADAM_WD * w), m, v


def _adamw(w, m, v, parts, layer, prev, name):
    nl, R, C = w.shape
    P = parts.shape[0]
    unit = SUBLANE * (4 // parts.dtype.itemsize)
    tr = _tile(R, max(unit, (128 * 1024 // C) // unit * unit), unit)

    def body(w_ref, m_ref, v_ref, p_ref, *rest):
        g_ref, d_ref, nm_ref, nv_ref = rest[-4:]
        g = p_ref[0].astype(F32)
        for p in range(1, P):
            g = g + p_ref[p].astype(F32)
        d, nm, nv = _adamw_math(w_ref[...], g, m_ref[...], v_ref[...])
        g_ref[...], d_ref[...], nm_ref[...], nv_ref[...] = g, d, nm, nv

    lay = pl.BlockSpec((None, tr, C), lambda i: (layer, i, 0))
    in_specs = [lay, lay, lay, pl.BlockSpec((P, tr, C), lambda i: (0, i, 0))]
    ins = [w, m, v, parts]
    aliases = {}
    if prev is not None:
        in_specs += [pl.BlockSpec(memory_space=pl.ANY)] * 4
        ins += list(prev)
        aliases = {4 + k: k for k in range(4)}
    return pl.pallas_call(
        body, grid=(R // tr,), in_specs=in_specs, out_specs=[lay] * 4,
        out_shape=[jax.ShapeDtypeStruct(w.shape, F32)] * 4, input_output_aliases=aliases,
        compiler_params=_params(("parallel",), VMEM_LIMIT), name=name,
    )(*ins)


def _adamw_layers(w, m, v, parts_per_layer, name):
    nl = w.shape[0]
    shp = w.shape
    C = shp[-1]
    R = math.prod(shp[1:-1])
    w3, m3, v3 = (a.reshape(nl, R, C) for a in (w, m, v))
    out = None
    for j in range(nl):
        p = parts_per_layer[j]
        out = _adamw(w3, m3, v3, p.reshape(p.shape[0], R, C), j, out, f"{name}_{j}")
    return tuple(o.reshape(shp) for o in out)


def _gelu_tanh(x):
    return jax.nn.gelu(x, approximate=True)


def _pack(arrs):
    tile = SUBLANE * LANE
    out = []
    for a in arrs:
        f = a.reshape(-1).astype(F32)
        out.append(jnp.pad(f, (0, (-f.shape[0]) % tile)))
    return jnp.concatenate(out)


def _unpack(flat, shapes):
    tile = SUBLANE * LANE
    out, off = [], 0
    for s in shapes:
        n = math.prod(s)
        out.append(flat[off:off + n].reshape(s))
        off += n + (-n) % tile
    return out


def kernel(x, gla_norm, gla_w_in, gla_w_gate_up, gla_b_gate, gla_o_norm, gla_w_out, s5_norm, s5_w_in, s5_lam_re, s5_lam_im, s5_log_dt, s5_b_re, s5_b_im, s5_c_re, s5_c_im, s5_d, s5_w_out, mlp_norm, mlp_w_up, mlp_w_down, final_norm, loss_target, m_gla_norm, m_gla_w_in, m_gla_w_gate_up, m_gla_b_gate, m_gla_o_norm, m_gla_w_out, m_s5_norm, m_s5_w_in, m_s5_lam_re, m_s5_lam_im, m_s5_log_dt, m_s5_b_re, m_s5_b_im, m_s5_c_re, m_s5_c_im, m_s5_d, m_s5_w_out, m_mlp_norm, m_mlp_w_up, m_mlp_w_down, m_final_norm, v_gla_norm, v_gla_w_in, v_gla_w_gate_up, v_gla_b_gate, v_gla_o_norm, v_gla_w_out, v_s5_norm, v_s5_w_in, v_s5_lam_re, v_s5_lam_im, v_s5_log_dt, v_s5_b_re, v_s5_b_im, v_s5_c_re, v_s5_c_im, v_s5_d, v_s5_w_out, v_mlp_norm, v_mlp_w_up, v_mlp_w_down, v_final_norm):
    W = dict(gla_norm=gla_norm, gla_w_in=gla_w_in, gla_w_gate_up=gla_w_gate_up, gla_b_gate=gla_b_gate, gla_o_norm=gla_o_norm, gla_w_out=gla_w_out, s5_norm=s5_norm, s5_w_in=s5_w_in, s5_lam_re=s5_lam_re, s5_lam_im=s5_lam_im, s5_log_dt=s5_log_dt, s5_b_re=s5_b_re, s5_b_im=s5_b_im, s5_c_re=s5_c_re, s5_c_im=s5_c_im, s5_d=s5_d, s5_w_out=s5_w_out, mlp_norm=mlp_norm, mlp_w_up=mlp_w_up, mlp_w_down=mlp_w_down, final_norm=final_norm)
    M = dict(gla_norm=m_gla_norm, gla_w_in=m_gla_w_in, gla_w_gate_up=m_gla_w_gate_up, gla_b_gate=m_gla_b_gate, gla_o_norm=m_gla_o_norm, gla_w_out=m_gla_w_out, s5_norm=m_s5_norm, s5_w_in=m_s5_w_in, s5_lam_re=m_s5_lam_re, s5_lam_im=m_s5_lam_im, s5_log_dt=m_s5_log_dt, s5_b_re=m_s5_b_re, s5_b_im=m_s5_b_im, s5_c_re=m_s5_c_re, s5_c_im=m_s5_c_im, s5_d=m_s5_d, s5_w_out=m_s5_w_out, mlp_norm=m_mlp_norm, mlp_w_up=m_mlp_w_up, mlp_w_down=m_mlp_w_down, final_norm=m_final_norm)
    V = dict(gla_norm=v_gla_norm, gla_w_in=v_gla_w_in, gla_w_gate_up=v_gla_w_gate_up, gla_b_gate=v_gla_b_gate, gla_o_norm=v_gla_o_norm, gla_w_out=v_gla_w_out, s5_norm=v_s5_norm, s5_w_in=v_s5_w_in, s5_lam_re=v_s5_lam_re, s5_lam_im=v_s5_lam_im, s5_log_dt=v_s5_log_dt, s5_b_re=v_s5_b_re, s5_b_im=v_s5_b_im, s5_c_re=v_s5_c_re, s5_c_im=v_s5_c_im, s5_d=v_s5_d, s5_w_out=v_s5_w_out, mlp_norm=v_mlp_norm, mlp_w_up=v_mlp_w_up, mlp_w_down=v_mlp_w_down, final_norm=v_final_norm)
    names = list(W)
    big = ["gla_w_in", "gla_w_out", "s5_w_in", "s5_w_out", "mlp_w_up", "mlp_w_down"]
    small_sharded = {"gla_w_gate_up": 2, "s5_norm": 1, "s5_d": 1}
    small = [n for n in names if n not in big]

    _, L, D = x.shape
    n_gla, n_s5, depth = gla_norm.shape[0], s5_lam_re.shape[0], mlp_norm.shape[0]
    H = GLA_HEADS
    KW, VW = D // 2, D
    DK, DV = KW // H, VW // H
    IN = 2 * KW + 2 * VW + GLA_RANK
    INP = 2 * KW + 2 * VW + LANE
    SW = s5_lam_re.shape[1] * S5_GROUP
    G, N = s5_lam_re.shape[1], s5_lam_re.shape[2]
    nb = G // S5_GB
    dev = _slot(_place())

    wb = {n: W[n].astype(BF16) for n in big}
    sm_sh = _pack([W[n] for n in small_sharded])
    order = []
    for i in range(depth):
        j = i // 2
        order += [("gla_w_in", j), ("gla_w_out", j)] if i % 2 == 0 else [("s5_w_in", j), ("s5_w_out", j)]
        order += [("mlp_w_up", i), ("mlp_w_down", i)]
    gathered = _all_gather([sm_sh] + [wb[n][j] for n, j in order], name="gather_weights")
    sm_all = gathered[0]
    full = {key: g for key, g in zip(order, gathered[1:])}
    sm_parts = [_unpack(sm_all[d], [W[n].shape for n in small_sharded]) for d in range(N_DEV)]
    wgu_full = jnp.concatenate([p[0] for p in sm_parts], axis=2)
    s5n_full = jnp.concatenate([p[1] for p in sm_parts], axis=1)
    s5d_full = jnp.concatenate([p[2] for p in sm_parts], axis=1)

    def gla_weights(j):
        w_in = full["gla_w_in", j]
        w_in = jnp.transpose(w_in, (1, 0, 2)).reshape(D, IN)
        w_in = jnp.pad(w_in, ((0, 0), (0, INP - IN)))
        w_out = full["gla_w_out", j].reshape(VW, D)
        wgu = jnp.pad(wgu_full[j], ((0, LANE - GLA_RANK), (0, 0))).astype(BF16)
        return w_in, w_out, wgu

    grads = {}
    h = x[0]
    saved = []

    for i in range(depth):
        j = i // 2
        if i % 2 == 0:
            w_in, w_out, wgu = gla_weights(j)
            gn = gla_norm[j][None]
            hn = _rms_fwd(h, gn, f"gla{j}_norm")
            proj = _mm(hn, w_in, "nn", name=f"gla{j}_proj", tn=896)
            bg, on = gla_b_gate[j][None], gla_o_norm[j][None]
            y, st = _gla_fwd(proj, wgu, bg, on, H=H, DK=DK, DV=DV, name=f"gla{j}_mix")
            h_new = _mm(y, w_out, "nn", name=f"gla{j}_out", res=h)
            saved.append(("gla", dict(h=h, hn=hn, proj=proj, y=y, st=st, w_in=w_in, w_out=w_out, wgu=wgu, gn=gn, bg=bg, on=on)))
        else:
            w_in = full["s5_w_in", j].reshape(D, SW)
            w_out3 = full["s5_w_out", j]
            sn = s5n_full[j][None]
            hn = _rms_fwd(h, sn, f"s5{j}_norm")
            u = _mm(hn, w_in, "nn", name=f"s5{j}_in")
            lre, lim = s5_lam_re[j][:, None, :], s5_lam_im[j][:, None, :]
            ldt = s5_log_dt[j][:, None, None]
            brt, bit = jnp.swapaxes(s5_b_re[j], 1, 2), jnp.swapaxes(s5_b_im[j], 1, 2)
            ar, ai, bbr, bbi, pr, pi, a124 = _s5_params(lre, lim, ldt, brt, bit, f"s5{j}_params")
            flat = lambda t: jnp.swapaxes(t, 0, 1).reshape(t.shape[1], G * N)
            pr, pi, a124 = flat(pr), flat(pi), flat(a124)
            bur = _mm_bd(u, _bd(bbr), name=f"s5{j}_bu_re")
            bui = _mm_bd(u, _bd(bbi), name=f"s5{j}_bu_im")
            xr, xi = _s5_scan(bur, bui, a124, pr, pi, reverse=False, name=f"s5{j}_scan")
            crt, cit = jnp.swapaxes(s5_c_re[j], 1, 2), jnp.swapaxes(s5_c_im[j], 1, 2)
            cx = _mm_bd(xr, _bd(crt), name=f"s5{j}_cx_re")
            cx = _mm_bd(xi, _bd(-cit), name=f"s5{j}_cx_im", res=cx)
            dsk = s5d_full[j][None]

            def act(cxv, uv, dv):
                ypre = cxv + dv * uv
                return ypre, _gelu_tanh(ypre)
            ypre, yg = _rowmap(act, [cx, u], [dsk], [(SW, F32), (SW, BF16)], name=f"s5{j}_act")
            z = _mm_nn_cb(yg, w_out3, name=f"s5{j}_out")

            def glu(zv, hv):
                return (hv + zv[:, :D] * jax.nn.sigmoid(zv[:, D:]),)
            h_new = _rowmap(glu, [z, h], [], [(D, F32)], name=f"s5{j}_glu")[0]
            saved.append(("s5", dict(h=h, hn=hn, u=u, xr=xr, xi=xi, ypre=ypre, yg=yg, z=z, w_in=w_in, w_out3=w_out3, sn=sn,
                                     dsk=dsk, prm=(lre, lim, ldt, brt, bit), a124=a124, pr=pr, pi=pi, bbr=bbr, bbi=bbi)))
        h = h_new
        w_up3 = full["mlp_w_up", i]
        w_down = full["mlp_w_down", i].reshape(4 * D, D)
        mn = mlp_norm[i][None]
        hn = _rms_fwd(h, mn, f"mlp{i}_norm")
        def sq_relu(zv):
            a = jnp.maximum(zv, 0.0)
            return zv, a * a
        z, s = _mm_nn_cb(hn, w_up3, name=f"mlp{i}_up", out_dtype=[F32, BF16], epi=sq_relu)
        h_new = _mm(s, w_down, "nn", name=f"mlp{i}_down", res=h)
        saved.append(("mlp", dict(h=h, hn=hn, z=z, s=s, w_up3=w_up3, w_down=w_down, mn=mn)))
        h = h_new

    dh, sq, dfin = _loss_head(h, loss_target[0], final_norm[None], "loss_head")
    loss = lax.psum(0.5 * jnp.sum(sq) / D, ("x", "y", "c"))
    small_grads = {"final_norm": dfin[0]}
    big_parts = {n: {} for n in big}
    stacks = {n: {} for n in small if n != "final_norm"}

    for idx in range(len(saved) - 1, -1, -1):
        kind, s = saved[idx]
        li = sum(1 for k, _ in saved[:idx] if k == kind)
        if kind == "mlp":
            def sq_relu_bwd(dsv, zv):
                return (dsv * 2.0 * jnp.maximum(zv, 0.0),)
            dz = _mm(dh, s["w_down"], "nt", name=f"mlp{li}_dz", out_dtype=BF16, epi=sq_relu_bwd, epi_ins=(s["z"],))
            big_parts["mlp_w_down"][li] = _mm(s["s"], dh, "tn", name=f"mlp{li}_dw_down", out_dtype=BF16).reshape(N_DEV, 4 * D // N_DEV, D)
            big_parts["mlp_w_up"][li] = _mm_tn_cbout(s["hn"], dz, N_DEV, name=f"mlp{li}_dw_up")
            dhn = _mm_nt_cb(dz, s["w_up3"], name=f"mlp{li}_dhn")
            dh, dg = _rms_bwd(s["h"], dhn, dh, s["mn"], f"mlp{li}_dnorm")
            stacks["mlp_norm"][li] = dg[0]
        elif kind == "gla":
            dyv = _mm(dh, s["w_out"], "nt", name=f"gla{li}_dy")
            big_parts["gla_w_out"][li] = _mm(s["y"], dh, "tn", name=f"gla{li}_dw_out", out_dtype=BF16).reshape(N_DEV, VW // N_DEV, D)
            dqkvr, dpre, don = _gla_bwd(s["proj"], s["wgu"], s["bg"], s["on"], s["st"], dyv, H=H, DK=DK, DV=DV, name=f"gla{li}_dmix")
            dglow = _mm(dpre, s["wgu"], "nt", name=f"gla{li}_dglow", out_dtype=BF16)
            glow = s["proj"][:, 2 * KW + 2 * VW:]
            dwgu = _mm(glow, dpre, "tn", name=f"gla{li}_dwgu")[:GLA_RANK]
            dbg = _rowmap(lambda t: (jnp.sum(t, axis=0, keepdims=True),), [dpre], [], [], [KW], name=f"gla{li}_dbg")[0]
            dproj = jnp.concatenate([dqkvr, dglow], axis=1)
            dw_in = _mm(s["hn"], dproj, "tn", name=f"gla{li}_dw_in", out_dtype=BF16, tn=896)[:, :IN]
            big_parts["gla_w_in"][li] = jnp.transpose(dw_in.reshape(D, N_DEV, IN // N_DEV), (1, 0, 2))
            dhn = _mm(dproj, s["w_in"], "nt", name=f"gla{li}_dhn", tk=896)
            dh, dg = _rms_bwd(s["h"], dhn, dh, s["gn"], f"gla{li}_dnorm")
            stacks["gla_norm"][li], stacks["gla_b_gate"][li], stacks["gla_o_norm"][li] = dg[0], dbg[0], don[0]
            stacks["gla_w_gate_up"][li] = dwgu
        else:
            def glu_bwd(zv, dhv):
                a, sg = zv[:, :D], jax.nn.sigmoid(zv[:, D:])
                return (jnp.concatenate([dhv * sg, dhv * a * sg * (1.0 - sg)], axis=1),)
            dz = _rowmap(glu_bwd, [s["z"], dh], [], [(2 * D, BF16)], name=f"s5{li}_dglu")[0]
            big_parts["s5_w_out"][li] = _mm_tn_cbout(s["yg"], dz, N_DEV, name=f"s5{li}_dw_out")
            dyg = _mm_nt_cb(dz, s["w_out3"], name=f"s5{li}_dyg")

            def act_bwd(dygv, ypv, uv, dv):
                _, vjp = jax.vjp(_gelu_tanh, ypv)
                dyp = vjp(dygv)[0]
                return dyp, dyp * dv, jnp.sum(dyp * uv, axis=0, keepdims=True)
            dyp, du, dd = _rowmap(act_bwd, [dyg, s["ypre"], s["u"]], [s["dsk"]], [(SW, F32), (SW, F32)], [SW], name=f"s5{li}_dact")
            cre, cim = s5_c_re[li], s5_c_im[li]
            wr = _mm_bd(dyp, _bd(cre), name=f"s5{li}_w_re")
            wi = _mm_bd(dyp, _bd(-cim), name=f"s5{li}_w_im")
            a124c = s["a124"] * jnp.array([1, -1, 1, -1, 1, -1, 1, 1], F32)[:, None]
            lr_, li_ = _s5_scan(wr, wi, a124c, s["pr"][::-1], -s["pi"][::-1], reverse=True, name=f"s5{li}_dscan")
            du = _mm_bd(lr_, _bd(jnp.swapaxes(s["bbr"], 1, 2)), name=f"s5{li}_du_re", res=du)
            du = _mm_bd(li_, _bd(jnp.swapaxes(s["bbi"], 1, 2)), name=f"s5{li}_du_im", res=du)
            dbbr = _bd_extract(_mm_tn_bd(s["u"], lr_, nb, name=f"s5{li}_dbb_re"), S5_GROUP, N)
            dbbi = _bd_extract(_mm_tn_bd(s["u"], li_, nb, name=f"s5{li}_dbb_im"), S5_GROUP, N)
            dcr = _bd_extract(_mm_tn_bd(dyp, s["xr"], nb, name=f"s5{li}_dc_re"), S5_GROUP, N)
            dci = -_bd_extract(_mm_tn_bd(dyp, s["xi"], nb, name=f"s5{li}_dc_im"), S5_GROUP, N)
            dar, dai = _s5_dabar(lr_, li_, s["xr"], s["xi"], f"s5{li}_dabar")
            dlre, dlim, dldt, dbrt, dbit = _s5_params_bwd(*s["prm"], dar.reshape(G, 1, N), dai.reshape(G, 1, N), dbbr, dbbi,
                                                         f"s5{li}_dparams")
            big_parts["s5_w_in"][li] = _mm(s["hn"], du, "tn", name=f"s5{li}_dw_in", out_dtype=BF16).reshape(N_DEV, D // N_DEV, SW)
            dhn = _mm(du, s["w_in"], "nt", name=f"s5{li}_dhn")
            dh, dg = _rms_bwd(s["h"], dhn, dh, s["sn"], f"s5{li}_dnorm")
            stacks["s5_norm"][li], stacks["s5_d"][li] = dg[0], dd[0]
            stacks["s5_lam_re"][li], stacks["s5_lam_im"][li], stacks["s5_log_dt"][li] = dlre[:, 0], dlim[:, 0], dldt[:, 0, 0]
            stacks["s5_b_re"][li], stacks["s5_b_im"][li] = jnp.swapaxes(dbrt, 1, 2), jnp.swapaxes(dbit, 1, 2)
            stacks["s5_c_re"][li], stacks["s5_c_im"][li] = dcr, dci
    grad_x = dh[None]

    for n, st_ in stacks.items():
        small_grads[n] = jnp.stack([st_[k] for k in range(len(st_))])
    full_shapes = [small_grads[n].shape for n in small]
    part = _pack([small_grads[n] for n in small])
    parts_all = _all_gather([part], name="gather_small_grads")[0]

    def mine(n, a):
        if n not in small_sharded:
            return a
        ax = small_sharded[n]
        size = W[n].shape[ax]
        return lax.dynamic_slice_in_dim(a, dev * size, size, axis=ax)

    rows = parts_all.shape[1] // LANE
    def full_layout(T):
        arrs = []
        for n, shp in zip(small, full_shapes):
            if n in small_sharded:
                ax = small_sharded[n]
                arrs.append(lax.dynamic_update_slice_in_dim(jnp.zeros(shp, F32), T[n], dev * T[n].shape[ax], axis=ax))
            else:
                arrs.append(T[n])
        return _pack(arrs)
    ws, ms, vs = full_layout(W), full_layout(M), full_layout(V)
    sg, sd, snm, snv = _adamw(ws.reshape(1, rows, LANE), ms.reshape(1, rows, LANE), vs.reshape(1, rows, LANE),
                              parts_all.reshape(N_DEV, rows, LANE), 0, None, "adamw_small")
    outs = {}
    for key, flat in zip(("grad", "delta", "new_m", "new_v"), (sg, sd, snm, snv)):
        for n, a in zip(small, _unpack(flat.reshape(-1), full_shapes)):
            outs[key, n] = mine(n, a)

    keys = [(n, j) for n in big for j in range(W[n].shape[0])]
    core = lax.axis_index("c").astype(jnp.int32).reshape(1)
    flat3 = lambda a: a.reshape(a.shape[0], math.prod(a.shape[1:-1]), a.shape[-1])
    parts8 = [flat3(big_parts[n][j]) for n, j in keys]
    from_sib = _pair_exchange(parts8, name="exchange_grads_pair")
    sums = [_pair_sum(p, g, core, f"pair_sum_{n}_{j}") for (n, j), p, g in zip(keys, parts8, from_sib)]
    recv = _chip_exchange(sums, name="exchange_grads_chips")
    got = {k: r for k, r in zip(keys, recv)}
    for n in big:
        res = _adamw_layers(W[n], M[n], V[n], [got[n, j] for j in range(W[n].shape[0])], f"adamw_{n}")
        for key, a in zip(("grad", "delta", "new_m", "new_v"), res):
            outs[key, n] = a

    return (loss, grad_x, *[outs["grad", n] for n in names], *[outs["delta", n] for n in names],
            *[outs["new_m", n] for n in names], *[outs["new_v", n] for n in names])
```

```python
import functools
import math

import jax
import jax.numpy as jnp
from jax import lax
from jax.experimental import pallas as pl
from jax.experimental.pallas import tpu as pltpu

F32, BF16 = jnp.float32, jnp.bfloat16
MESH = pl.DeviceIdType.MESH
N_DEV = 8
LANE = 128
SUBLANE = 8
VMEM_LIMIT = 48 * 1024 * 1024

EPS = 1e-6
CHUNK = 64
GLA_HEADS = 4
GLA_RANK = 16
GLA_TEMP = 16.0
S5_GROUP = 16
S5_STATE = 64
S5_EIG_CLIP = -1e-4
S5_GB = 16
ADAM_LR, ADAM_B1, ADAM_B2, ADAM_EPS, ADAM_WD, ADAM_STEP = 0.001, 0.9, 0.999, 1e-08, 0.01, 10

_CONTRACT = {"nn": ((1,), (0,)), "tn": ((0,), (0,)), "nt": ((1,), (1,))}


def _tile(n, pref, unit=LANE):
    if n <= pref:
        return n
    t = (pref // unit) * unit
    while t > unit and n % t:
        t -= unit
    assert n % t == 0, (n, pref, unit)
    return t


def _params(sem, vmem=None):
    return pltpu.CompilerParams(dimension_semantics=sem, vmem_limit_bytes=vmem)


def _dot(a, b, dims, precision=None):
    return lax.dot_general(a, b, (_CONTRACT[dims], ((), ())), preferred_element_type=F32, precision=precision)


def _bdot(a, b, dims):
    return _dot(a.astype(BF16), b.astype(BF16), dims)


def _mm_call(a, b, *, dims, grid, a_spec, b_spec, o_spec, out_shape, out_dtype, name, res=None, epi=None, epi_ins=()):
    nk = grid[2]
    acc_shape = tuple(d for d in o_spec.block_shape if d is not None)
    if res is not None:
        epi, epi_ins = (lambda r, x: (r + x,)), (res,)
    single = not isinstance(out_dtype, (list, tuple))
    out_dtypes = [out_dtype] if single else list(out_dtype)
    n_e, n_o = len(epi_ins), len(out_dtypes)

    def body(*refs):
        a_ref, b_ref = refs[:2]
        e_refs, o_refs = refs[2:2 + n_e], refs[2 + n_e:2 + n_e + n_o]

        def finish(r):
            vals = (r,) if epi is None else epi(r, *[e[...].astype(F32) for e in e_refs])
            for o_ref, v in zip(o_refs, vals):
                o_ref[...] = v.astype(o_ref.dtype)

        d = _bdot(a_ref[...], b_ref[...], dims)
        if nk == 1:
            finish(d)
            return
        acc = refs[-1]
        k = pl.program_id(2)

        @pl.when(k == 0)
        def _():
            acc[...] = d

        @pl.when((k > 0) & (k < nk - 1))
        def _():
            acc[...] += d

        @pl.when(k == nk - 1)
        def _():
            finish(acc[...] + d)

    outs = pl.pallas_call(
        body, grid=grid, in_specs=[a_spec, b_spec] + [o_spec] * n_e, out_specs=[o_spec] * n_o,
        out_shape=[jax.ShapeDtypeStruct(out_shape, dt) for dt in out_dtypes],
        scratch_shapes=[] if nk == 1 else [pltpu.VMEM(acc_shape, F32)],
        compiler_params=_params(("parallel", "parallel", "arbitrary"), VMEM_LIMIT), name=name,
    )(a, b, *epi_ins)
    return outs[0] if single else outs


TM, TN, TK = 1024, 1024, 2048


def _mm(a, b, dims, *, name, out_dtype=F32, res=None, epi=None, epi_ins=(), tm=TM, tn=TN, tk=TK):
    if dims == "tn":
        (K, M), (_, N) = a.shape, b.shape
    elif dims == "nn":
        (M, K), (_, N) = a.shape, b.shape
    else:
        (M, K), (N, _) = a.shape, b.shape
    tm, tn, tk = _tile(M, tm), _tile(N, tn), _tile(K, tk)
    a_spec = pl.BlockSpec((tk, tm), lambda i, j, k: (k, i)) if dims == "tn" else pl.BlockSpec((tm, tk), lambda i, j, k: (i, k))
    b_spec = pl.BlockSpec((tn, tk), lambda i, j, k: (j, k)) if dims == "nt" else pl.BlockSpec((tk, tn), lambda i, j, k: (k, j))
    o_spec = pl.BlockSpec((tm, tn), lambda i, j, k: (i, j))
    return _mm_call(a, b, dims=dims, grid=(M // tm, N // tn, K // tk), a_spec=a_spec, b_spec=b_spec, o_spec=o_spec,
                    out_shape=(M, N), out_dtype=out_dtype, name=name, res=res, epi=epi, epi_ins=epi_ins)


def _mm_nn_cb(a, b3, *, name, out_dtype=F32, epi=None, tm=TM, tn=TN, tk=TK):
    (M, K), (P, _, Ns) = a.shape, b3.shape
    tm, tn, tk = _tile(M, tm), _tile(Ns, tn), _tile(K, tk)
    npb = Ns // tn
    return _mm_call(a, b3, dims="nn", grid=(M // tm, P * npb, K // tk),
                    a_spec=pl.BlockSpec((tm, tk), lambda i, j, k: (i, k)),
                    b_spec=pl.BlockSpec((None, tk, tn), lambda i, j, k: (j // npb, k, j % npb)),
                    o_spec=pl.BlockSpec((tm, tn), lambda i, j, k: (i, j)),
                    out_shape=(M, P * Ns), out_dtype=out_dtype, name=name, epi=epi)


def _mm_nt_cb(a, b3, *, name, out_dtype=F32, tm=TM, tn=TN, tk=TK):
    (M, _), (P, N, Ns) = a.shape, b3.shape
    tm, tn, tk = _tile(M, tm), _tile(N, tn), _tile(Ns, tk)
    kpb = Ns // tk
    return _mm_call(a, b3, dims="nt", grid=(M // tm, N // tn, P * kpb),
                    a_spec=pl.BlockSpec((tm, tk), lambda i, j, k: (i, k)),
                    b_spec=pl.BlockSpec((None, tn, tk), lambda i, j, k: (k // kpb, j, k % kpb)),
                    o_spec=pl.BlockSpec((tm, tn), lambda i, j, k: (i, j)),
                    out_shape=(M, N), out_dtype=out_dtype, name=name)


def _mm_tn_cbout(a, b, parts, *, name, out_dtype=BF16, tm=TM, tn=TN, tk=TK):
    (K, M), (_, N) = a.shape, b.shape
    Ns = N // parts
    tm, tn, tk = _tile(M, tm), _tile(Ns, tn), _tile(K, tk)
    npb = Ns // tn
    return _mm_call(a, b, dims="tn", grid=(M // tm, parts * npb, K // tk),
                    a_spec=pl.BlockSpec((tk, tm), lambda i, j, k: (k, i)),
                    b_spec=pl.BlockSpec((tk, tn), lambda i, j, k: (k, j)),
                    o_spec=pl.BlockSpec((None, tm, tn), lambda i, j, k: (j // npb, i, j % npb)),
                    out_shape=(parts, M, Ns), out_dtype=out_dtype, name=name)


def _mm_bd(a, w3, *, name, out_dtype=F32, res=None, tm=2048, tn=1024):
    (M, _), (nb, Kb, Nb) = a.shape, w3.shape
    tm, tn = _tile(M, tm), _tile(Nb, tn)
    npb = Nb // tn
    return _mm_call(a, w3, dims="nn", grid=(M // tm, nb * npb, 1),
                    a_spec=pl.BlockSpec((tm, Kb), lambda i, j, k: (i, j // npb)),
                    b_spec=pl.BlockSpec((None, Kb, tn), lambda i, j, k: (j // npb, 0, j % npb)),
                    o_spec=pl.BlockSpec((tm, tn), lambda i, j, k: (i, j)),
                    out_shape=(M, nb * Nb), out_dtype=out_dtype, name=name, res=res)


def _mm_tn_bd(a, b, nb, *, name, tk=1024):
    (K, MA), (_, NB) = a.shape, b.shape
    Ma, Nb = MA // nb, NB // nb
    tk = _tile(K, tk)
    return _mm_call(a, b, dims="tn", grid=(nb, 1, K // tk),
                    a_spec=pl.BlockSpec((tk, Ma), lambda i, j, k: (k, i)),
                    b_spec=pl.BlockSpec((tk, Nb), lambda i, j, k: (k, i)),
                    o_spec=pl.BlockSpec((None, Ma, Nb), lambda i, j, k: (i, 0, 0)),
                    out_shape=(nb, Ma, Nb), out_dtype=F32, name=name)


def _rowmap(fn, rows, consts, out_defs, red_defs=(), *, name, tr=256):
    L = rows[0].shape[0]
    widest = max([r.shape[1] for r in rows] + [n for n, _ in out_defs])
    tr = _tile(L, max(SUBLANE * 2, min(tr, 512 * 1024 // widest)), SUBLANE * 2)
    n_in, n_o, n_d = len(rows) + len(consts), len(out_defs), len(red_defs)

    def body(*refs):
        res = fn(*[r[...] for r in refs[:n_in]])
        res = res if isinstance(res, (tuple, list)) else (res,)
        outs = refs[n_in:]
        for o_ref, val in zip(outs[:n_o], res[:n_o]):
            o_ref[...] = val.astype(o_ref.dtype)
        if n_d:
            @pl.when(pl.program_id(0) == 0)
            def _():
                for o_ref in outs[n_o:]:
                    o_ref[...] = jnp.zeros_like(o_ref)
            for o_ref, val in zip(outs[n_o:], res[n_o:]):
                o_ref[...] += val

    in_specs = [pl.BlockSpec((tr, r.shape[1]), lambda i: (i, 0)) for r in rows]
    in_specs += [pl.BlockSpec(c.shape, lambda i, nd=c.ndim: (0,) * nd) for c in consts]
    out_specs = [pl.BlockSpec((tr, n), lambda i: (i, 0)) for n, _ in out_defs]
    out_specs += [pl.BlockSpec((1, n), lambda i: (0, 0)) for n in red_defs]
    out_shape = [jax.ShapeDtypeStruct((L, n), dt) for n, dt in out_defs]
    out_shape += [jax.ShapeDtypeStruct((1, n), F32) for n in red_defs]
    return pl.pallas_call(body, grid=(L // tr,), in_specs=in_specs, out_specs=out_specs, out_shape=out_shape,
                          compiler_params=_params(("arbitrary",), VMEM_LIMIT), name=name)(*rows, *consts)


def _rms_parts(x):
    r = lax.rsqrt(jnp.mean(x * x, axis=-1, keepdims=True) + EPS)
    return r, x * r


def _rms_fwd(h, g, name):
    def fn(x, gg):
        _, xh = _rms_parts(x)
        return (xh * gg,)
    return _rowmap(fn, [h], [g], [(h.shape[1], BF16)], name=name)[0]


def _rms_bwd(h, dhn, dh, g, name, deps=()):
    def fn(x, dy, dres, gg, *_):
        r, xh = _rms_parts(x)
        dxh = dy * gg
        dx = r * (dxh - xh * jnp.mean(dxh * xh, axis=-1, keepdims=True))
        return dres + dx, jnp.sum(dy * xh, axis=0, keepdims=True)
    D = h.shape[1]
    return _rowmap(fn, [h, dhn, dh], [g, *deps], [(D, F32)], [D], name=name)


def _loss_head(h, tgt, g, name):
    D = h.shape[1]

    def fn(x, t, gg):
        r, xh = _rms_parts(x)
        diff = xh * gg - t
        dy = diff * (1.0 / D)
        dxh = dy * gg
        dx = r * (dxh - xh * jnp.mean(dxh * xh, axis=-1, keepdims=True))
        return dx, jnp.sum(diff * diff, axis=0, keepdims=True), jnp.sum(dy * xh, axis=0, keepdims=True)
    return _rowmap(fn, [h, tgt], [g], [(D, F32)], [D, D], name=name)


def _tri(n, strict):
    r = lax.broadcasted_iota(jnp.int32, (n, n), 0)
    c = lax.broadcasted_iota(jnp.int32, (n, n), 1)
    return jnp.where((c < r) if strict else (c <= r), 1.0, 0.0).astype(F32)


def _gla_gate(g_ref, wgu_ref, bg_ref):
    pre = _bdot(g_ref[...], wgu_ref[...], "nn") + bg_ref[...]
    la = (jnp.minimum(pre, 0.0) - jnp.log(1.0 + jnp.exp(-jnp.abs(pre)))) * (1.0 / GLA_TEMP)
    cum = _dot(_tri(CHUNK, False), la, "nn", lax.Precision.HIGHEST)
    return pre, cum, cum[CHUNK - 1:CHUNK, :]


def _gla_specs(H, DK, DV, cmap):
    kb, vb, gb = H, (2 * H * DK) // DV, (2 * H * DK + 2 * H * DV) // LANE
    return [
        pl.BlockSpec((CHUNK, DK), lambda h, c: (cmap(h, c), h)),
        pl.BlockSpec((CHUNK, DK), lambda h, c: (cmap(h, c), kb + h)),
        pl.BlockSpec((CHUNK, DV), lambda h, c: (cmap(h, c), vb + h)),
        pl.BlockSpec((CHUNK, DV), lambda h, c: (cmap(h, c), vb + H + h)),
        pl.BlockSpec((CHUNK, LANE), lambda h, c: (cmap(h, c), gb)),
        pl.BlockSpec((LANE, DK), lambda h, c: (0, h)),
        pl.BlockSpec((1, DK), lambda h, c: (0, h)),
        pl.BlockSpec((1, DV), lambda h, c: (0, 0)),
    ]


def _gla_fwd(proj, wgu, bg, on, *, H, DK, DV, name):
    L = proj.shape[0]
    nc = L // CHUNK
    scale = DK ** -0.5

    def body(q_ref, k_ref, v_ref, r_ref, g_ref, wgu_ref, bg_ref, on_ref, y_ref, st_ref, S):
        @pl.when(pl.program_id(1) == 0)
        def _():
            S[...] = jnp.zeros_like(S)
        _, cum, total = _gla_gate(g_ref, wgu_ref, bg_ref)
        kd = k_ref[...] * jnp.exp(total - cum)
        St = S[...] * jnp.exp(total) + _bdot(v_ref[...], kd, "tn")
        S[...] = St
        st_ref[...] = St
        o = _bdot(q_ref[...] * scale, St, "nt")
        _, oh = _rms_parts(o)
        y_ref[...] = (oh * on_ref[...] * jax.nn.silu(r_ref[...])).astype(y_ref.dtype)

    return pl.pallas_call(
        body, grid=(H, nc), in_specs=_gla_specs(H, DK, DV, lambda h, c: c),
        out_specs=[pl.BlockSpec((CHUNK, DV), lambda h, c: (c, h)),
                   pl.BlockSpec((None, None, DV, DK), lambda h, c: (h, c, 0, 0))],
        out_shape=[jax.ShapeDtypeStruct((L, H * DV), BF16), jax.ShapeDtypeStruct((H, nc, DV, DK), F32)],
        scratch_shapes=[pltpu.VMEM((DV, DK), F32)],
        compiler_params=_params(("arbitrary", "arbitrary"), VMEM_LIMIT), name=name,
    )(proj, proj, proj, proj, proj, wgu, bg, on)


def _gla_bwd(proj, wgu, bg, on, st, dy, *, H, DK, DV, name):
    L = proj.shape[0]
    nc = L // CHUNK
    scale = DK ** -0.5
    rev = lambda h, c: nc - 1 - c

    def body(q_ref, k_ref, v_ref, r_ref, g_ref, wgu_ref, bg_ref, on_ref, sc_ref, sp_ref, dy_ref,
             dq_ref, dk_ref, dv_ref, dr_ref, dpre_ref, don_ref, G, decn):
        h, c = pl.program_id(0), pl.program_id(1)

        @pl.when(c == 0)
        def _():
            G[...] = jnp.zeros_like(G)
            decn[...] = jnp.zeros_like(decn)

        @pl.when((c == 0) & (h == 0))
        def _():
            don_ref[...] = jnp.zeros_like(don_ref)

        pre, cum, total = _gla_gate(g_ref, wgu_ref, bg_ref)
        ex = jnp.exp(total - cum)
        k, v, r = k_ref[...], v_ref[...], r_ref[...]
        kd = k * ex
        dec = jnp.exp(total)
        qs = q_ref[...] * scale
        Sc = sc_ref[...]
        o = _bdot(qs, Sc, "nt")
        rinv, oh = _rms_parts(o)
        gn = on_ref[...]
        sg = jax.nn.sigmoid(r)
        dyv = dy_ref[...]
        d_on = dyv * (r * sg)
        dr_ref[...] = (dyv * (oh * gn) * (sg * (1.0 + r * (1.0 - sg)))).astype(dr_ref.dtype)
        don_ref[...] += jnp.sum(d_on * oh, axis=0, keepdims=True)
        dxh = d_on * gn
        do = rinv * (dxh - oh * jnp.mean(dxh * oh, axis=-1, keepdims=True))
        dq_ref[...] = (_bdot(do, Sc, "nn") * scale).astype(dq_ref.dtype)
        Gt = G[...] * decn[...] + _bdot(do, qs, "tn")
        G[...] = Gt
        decn[...] = dec
        dkd = _bdot(v, Gt, "nn")
        dv_ref[...] = _bdot(kd, Gt, "nt").astype(dv_ref.dtype)
        Sp = sp_ref[...] * jnp.where(c == nc - 1, 0.0, 1.0)
        ddec = jnp.sum(Gt * Sp, axis=0, keepdims=True)
        dk_ref[...] = (dkd * ex).astype(dk_ref.dtype)
        dla = ddec * dec + _dot(_tri(CHUNK, True), dkd * kd, "nn", lax.Precision.HIGHEST)
        dpre_ref[...] = dla * (1.0 / GLA_TEMP) * jax.nn.sigmoid(-pre)

    in_specs = _gla_specs(H, DK, DV, rev) + [
        pl.BlockSpec((None, None, DV, DK), lambda h, c: (h, rev(h, c), 0, 0)),
        pl.BlockSpec((None, None, DV, DK), lambda h, c: (h, jnp.maximum(rev(h, c) - 1, 0), 0, 0)),
        pl.BlockSpec((CHUNK, DV), lambda h, c: (rev(h, c), h)),
    ]
    blk_k = pl.BlockSpec((CHUNK, DK), lambda h, c: (rev(h, c), h))
    blk_v = pl.BlockSpec((CHUNK, DV), lambda h, c: (rev(h, c), h))
    out_specs = [blk_k, blk_k, blk_v, blk_v, blk_k, pl.BlockSpec((1, DV), lambda h, c: (0, 0))]
    sds = jax.ShapeDtypeStruct
    out_shape = [sds((L, H * DK), BF16), sds((L, H * DK), BF16), sds((L, H * DV), BF16), sds((L, H * DV), BF16),
                 sds((L, H * DK), F32), sds((1, DV), F32)]
    dq, dk, dv, dr, dpre, don = pl.pallas_call(
        body, grid=(H, nc), in_specs=in_specs, out_specs=out_specs, out_shape=out_shape,
        scratch_shapes=[pltpu.VMEM((DV, DK), F32), pltpu.VMEM((1, DK), F32)],
        compiler_params=_params(("arbitrary", "arbitrary"), VMEM_LIMIT), name=name,
    )(proj, proj, proj, proj, proj, wgu, bg, on, st, st, dy)
    dqkvr = jnp.concatenate([dq, dk, dv, dr], axis=1)
    return dqkvr, dpre, don


def _s5_param_fn(lam_re, lam_im, log_dt, brt, bit):
    lr = jnp.minimum(lam_re, S5_EIG_CLIP)
    li = lam_im
    dt = jnp.exp(log_dt)
    mag = jnp.exp(lr * dt)
    ang = li * dt
    ab_re = mag * jnp.cos(ang)
    ab_im = mag * jnp.sin(ang)
    den = lr * lr + li * li
    nr = ab_re - 1.0
    f_re = (nr * lr + ab_im * li) / den
    f_im = (ab_im * lr - nr * li) / den
    return ab_re, ab_im, f_re * brt - f_im * bit, f_re * bit + f_im * brt


def _s5_params(lam_re, lam_im, log_dt, brt, bit, name):
    G, _, N = lam_re.shape

    def body(lr_ref, li_ref, dt_ref, br_ref, bi_ref, ar_ref, ai_ref, bbr_ref, bbi_ref, pr_ref, pi_ref, a124_ref):
        lre, lim, ldt = lr_ref[...], li_ref[...], dt_ref[...]
        ar, ai, bbr, bbi = _s5_param_fn(lre, lim, ldt, br_ref[...], bi_ref[...])
        ar_ref[...], ai_ref[...], bbr_ref[...], bbi_ref[...] = ar, ai, bbr, bbi
        kk = (lax.broadcasted_iota(jnp.int32, (1, SUBLANE, 1), 1) + 1).astype(F32)
        dt = jnp.exp(ldt)
        mag = jnp.exp(kk * (jnp.minimum(lre, S5_EIG_CLIP) * dt))
        ang = kk * (lim * dt)
        pr_ref[...] = mag * jnp.cos(ang)
        pi_ref[...] = mag * jnp.sin(ang)
        r = lax.broadcasted_iota(jnp.int32, (1, SUBLANE, 1), 1)
        k2 = jnp.where(r < 2, 1.0, jnp.where(r < 4, 2.0, jnp.where(r < 6, 4.0, 0.0)))
        mag2 = jnp.exp(k2 * (jnp.minimum(lre, S5_EIG_CLIP) * dt))
        ang2 = k2 * (lim * dt)
        a124_ref[...] = mag2 * jnp.where(r % 2 == 0, jnp.cos(ang2), jnp.sin(ang2))

    sds = jax.ShapeDtypeStruct
    return pl.pallas_call(
        body, out_shape=[sds((G, 1, N), F32), sds((G, 1, N), F32), sds(brt.shape, F32), sds(brt.shape, F32),
                         sds((G, SUBLANE, N), F32), sds((G, SUBLANE, N), F32), sds((G, SUBLANE, N), F32)], name=name,
    )(lam_re, lam_im, log_dt, brt, bit)


def _s5_params_bwd(lam_re, lam_im, log_dt, brt, bit, dar, dai, dbbr, dbbi, name):
    def body(lr_ref, li_ref, dt_ref, br_ref, bi_ref, dar_ref, dai_ref, dbbr_ref, dbbi_ref, *outs):
        _, vjp = jax.vjp(_s5_param_fn, lr_ref[...], li_ref[...], dt_ref[...], br_ref[...], bi_ref[...])
        for o_ref, val in zip(outs, vjp((dar_ref[...], dai_ref[...], dbbr_ref[...], dbbi_ref[...]))):
            o_ref[...] = val

    ins = (lam_re, lam_im, log_dt, brt, bit)
    return pl.pallas_call(body, out_shape=[jax.ShapeDtypeStruct(a.shape, F32) for a in ins], name=name)(
        *ins, dar, dai, dbbr, dbbi)


def _s5_scan(br, bi, a124, pr, pi, *, reverse, name, W=256):
    L, n = br.shape
    W = _tile(n, W)
    nblk = L // SUBLANE

    def body(br_ref, bi_ref, a_ref, pr_ref, pi_ref, xr_ref, xi_ref):
        A = a_ref[...]
        PR, PI = pr_ref[...], pi_ref[...]
        row = lax.broadcasted_iota(jnp.int32, (SUBLANE, W), 0)
        last = 0 if reverse else SUBLANE - 1

        def step(i, carry):
            cr, ci = carry
            off = pl.multiple_of(((nblk - 1 - i) if reverse else i) * SUBLANE, SUBLANE)
            xr, xi = br_ref[pl.ds(off, SUBLANE), :], bi_ref[pl.ds(off, SUBLANE), :]
            for j, k in enumerate((1, 2, 4)):
                ar, ai = A[2 * j:2 * j + 1, :], A[2 * j + 1:2 * j + 2, :]
                keep = (row < SUBLANE - k) if reverse else (row >= k)
                shift = (SUBLANE - k) if reverse else k
                sr = jnp.where(keep, pltpu.roll(xr, shift, 0), 0.0)
                si = jnp.where(keep, pltpu.roll(xi, shift, 0), 0.0)
                xr, xi = xr + ar * sr - ai * si, xi + ar * si + ai * sr
            xr, xi = xr + PR * cr - PI * ci, xi + PR * ci + PI * cr
            xr_ref[pl.ds(off, SUBLANE), :] = xr
            xi_ref[pl.ds(off, SUBLANE), :] = xi
            return xr[last:last + 1, :], xi[last:last + 1, :]

        z = jnp.zeros((1, W), F32)
        lax.fori_loop(0, nblk, step, (z, z))

    col = pl.BlockSpec((L, W), lambda j: (0, j))
    par = pl.BlockSpec((SUBLANE, W), lambda j: (0, j))
    return pl.pallas_call(
        body, grid=(n // W,), in_specs=[col, col, par, par, par], out_specs=[col, col],
        out_shape=[jax.ShapeDtypeStruct((L, n), F32)] * 2,
        compiler_params=_params(("parallel",), VMEM_LIMIT), name=name,
    )(br, bi, a124, pr, pi)


def _s5_dabar(lr, li, xr, xi, name, W=256):
    L, n = lr.shape
    W = _tile(n, W)

    def body(lr_ref, li_ref, xr_ref, xi_ref, dar_ref, dai_ref):
        first = lax.broadcasted_iota(jnp.int32, (L, W), 0) == 0
        pr = jnp.where(first, 0.0, pltpu.roll(xr_ref[...], 1, 0))
        pi = jnp.where(first, 0.0, pltpu.roll(xi_ref[...], 1, 0))
        a, b = lr_ref[...], li_ref[...]
        dar_ref[...] = jnp.sum(a * pr + b * pi, axis=0, keepdims=True)
        dai_ref[...] = jnp.sum(b * pr - a * pi, axis=0, keepdims=True)

    col = pl.BlockSpec((L, W), lambda j: (0, j))
    one = pl.BlockSpec((1, W), lambda j: (0, j))
    return pl.pallas_call(
        body, grid=(n // W,), in_specs=[col] * 4, out_specs=[one, one],
        out_shape=[jax.ShapeDtypeStruct((1, n), F32)] * 2,
        compiler_params=_params(("parallel",), VMEM_LIMIT), name=name,
    )(lr, li, xr, xi)


def _bd(w):
    G, A, B = w.shape
    w4 = w.reshape(G // S5_GB, S5_GB, A, B)
    eye = jnp.eye(S5_GB, dtype=w.dtype)
    return jnp.einsum("kgab,gh->kgahb", w4, eye).reshape(G // S5_GB, S5_GB * A, S5_GB * B).astype(BF16)


def _bd_extract(m, A, B):
    nb = m.shape[0]
    d = jnp.diagonal(m.reshape(nb, S5_GB, A, S5_GB, B), axis1=1, axis2=3)
    return jnp.moveaxis(d, 3, 1).reshape(nb * S5_GB, A, B)


def _place():
    return lax.axis_index("x"), lax.axis_index("y"), lax.axis_index("c")


def _slot(p):
    return 4 * p[0] + 2 * p[1] + p[2]


def _all_gather(xs, name):
    n = len(xs)

    def body(*refs):
        x_refs, o_refs = refs[:n], refs[n:2 * n]
        send_sems, recv_sems, local_sems = refs[2 * n:]
        x, y, c = _place()
        me, sib = (x, y, c), (x, y, 1 - c)
        chips = [(1 - x, y), (x, 1 - y), (1 - x, 1 - y)]

        def copy(a, k, block, to, src=None):
            dst = o_refs[a].at[_slot(block)]
            return pltpu.make_async_remote_copy(
                src_ref=dst if src is None else src, dst_ref=dst, send_sem=send_sems.at[a, k], recv_sem=recv_sems.at[a, k],
                device_id=to, device_id_type=MESH)

        mine = [pltpu.make_async_copy(x_refs[a], o_refs[a].at[_slot(me)], local_sems.at[a]) for a in range(n)]
        for cp in mine:
            cp.start()
        sent = []
        for a in range(n):
            sent.append(copy(a, 0, me, sib, src=x_refs[a]))
            sent += [copy(a, 1 + j, me, (*chip, c), src=x_refs[a]) for j, chip in enumerate(chips)]
        for cp in sent:
            cp.start()
        for j, chip in enumerate(chips):
            for a in range(n):
                copy(a, 1 + j, (*chip, c), me).wait_recv()
                fwd = copy(a, 4 + j, (*chip, c), sib)
                fwd.start()
                sent.append(fwd)
        for a in range(n):
            copy(a, 0, sib, me).wait_recv()
            for j, chip in enumerate(chips):
                copy(a, 4 + j, (*chip, 1 - c), me).wait_recv()
        for cp in sent:
            cp.wait_send()
        for cp in mine:
            cp.wait()

    any_spec = pl.BlockSpec(memory_space=pl.ANY)
    return pl.pallas_call(
        body, in_specs=[any_spec] * n, out_specs=[any_spec] * n,
        out_shape=[jax.ShapeDtypeStruct((N_DEV, *a.shape), a.dtype) for a in xs],
        scratch_shapes=[pltpu.SemaphoreType.DMA((n, 7)), pltpu.SemaphoreType.DMA((n, 7)), pltpu.SemaphoreType.DMA((n,))],
        name=name,
    )(*xs)


N_CHIP = N_DEV // 2
_HBM = pl.BlockSpec(memory_space=pltpu.HBM)
_SEM = pl.BlockSpec(memory_space=pltpu.SEMAPHORE)
_ANY = pl.BlockSpec(memory_space=pl.ANY)
_EFFECT = pltpu.SideEffectType.DATAFLOW_SIDE_EFFECTING


def _gather_routes(x, y, c):
    me = _slot((x, y, c))
    peers = [(x, y, 1 - c)] + [(px, py, c) for px, py in ((1 - x, y), (x, 1 - y), (1 - x, 1 - y))]
    return [(p, None, me, _slot(p)) for p in peers]


def _chip_routes(x, y, c):
    myq = 2 * x + y
    return [((px, py, c), 2 * px + py, myq, 2 * px + py) for px, py in ((1 - x, y), (x, 1 - y), (1 - x, 1 - y))]


def _split_copies(s_refs, l_refs, send_sems, recv_sems, routes, arrival):
    out = []
    rts = routes(*_place())
    for a in range(len(s_refs)):
        for k, (peer, src_slot, dst_slot, arr_slot) in enumerate(rts):
            src = s_refs[a] if src_slot is None else s_refs[a].at[src_slot]
            sem = a * len(rts) + k
            out.append(pltpu.make_async_remote_copy(
                src_ref=src, dst_ref=l_refs[a].at[arr_slot if arrival else dst_slot], send_sem=send_sems.at[sem],
                recv_sem=recv_sems.at[sem], device_id=peer, device_id_type=MESH))
    return out


def _split_start(srcs, lands, routes, n_routes, dep, name):
    n = len(srcs)

    def body(*refs):
        for cp in _split_copies(refs[:n], refs[n:2 * n], refs[2 * n + 1], refs[2 * n + 2], routes, False):
            cp.start()
        refs[-1][...] = jnp.zeros_like(refs[-1])

    hbm = lambda a: pltpu.with_memory_space_constraint(a, pltpu.HBM)
    sems = pltpu.SemaphoreType.DMA((n * n_routes,))
    res = pl.pallas_call(
        body, name=name,
        out_shape=(sems, sems, *[pltpu.HBM(a.shape, a.dtype) for a in (*srcs, *lands)], jax.ShapeDtypeStruct((SUBLANE, LANE), F32)),
        in_specs=[_HBM] * (2 * n) + [_ANY], out_specs=(_SEM, _SEM, *[_HBM] * (2 * n), pl.BlockSpec(memory_space=pltpu.VMEM)),
        input_output_aliases={i: 2 + i for i in range(2 * n)},
        compiler_params=pltpu.CompilerParams(has_side_effects=_EFFECT),
    )(*[hbm(a) for a in srcs], *[hbm(a) for a in lands], dep)
    return res[0], res[1], list(res[2:2 + n]), list(res[2 + n:2 + 2 * n]), res[-1]


def _split_wait(send_sems, recv_sems, srcs, lands, routes, after, name):
    n = len(srcs)

    def body(*refs):
        for cp in _split_copies(refs[:n], refs[n:2 * n], refs[2 * n], refs[2 * n + 1], routes, True):
            cp.wait_send()
            cp.wait_recv()

    res = pl.pallas_call(
        body, name=name, out_shape=[pltpu.HBM(a.shape, a.dtype) for a in (*srcs, *lands)],
        in_specs=[_HBM] * (2 * n) + [_SEM, _SEM, _ANY], out_specs=[_HBM] * (2 * n),
        input_output_aliases={i: i for i in range(2 * n)},
        compiler_params=pltpu.CompilerParams(has_side_effects=_EFFECT),
    )(*srcs, *lands, send_sems, recv_sems, after)
    return list(res[:n]), list(res[n:])


def _gather_forward(xs, lands, name):
    n = len(xs)

    def body(*refs):
        x_refs, l_in, l_out = refs[:n], refs[n:2 * n], refs[2 * n:3 * n]
        send_sems, recv_sems, local_sems = refs[3 * n:]
        x, y, c = _place()
        me, sib = (x, y, c), (x, y, 1 - c)
        chips = [(1 - x, y), (x, 1 - y), (1 - x, 1 - y)]

        def copy(a, j, core):
            s = _slot((*chips[j], core))
            return pltpu.make_async_remote_copy(
                src_ref=l_in[a].at[s], dst_ref=l_out[a].at[s], send_sem=send_sems.at[a, j], recv_sem=recv_sems.at[a, j],
                device_id=sib, device_id_type=MESH)

        mine = [pltpu.make_async_copy(x_refs[a], l_out[a].at[_slot(me)], local_sems.at[a]) for a in range(n)]
        sent = [copy(a, j, c) for a in range(n) for j in range(3)]
        for cp in mine + sent:
            cp.start()
        for a in range(n):
            for j in range(3):
                copy(a, j, 1 - c).wait_recv()
        for cp in sent:
            cp.wait_send()
        for cp in mine:
            cp.wait()

    return pl.pallas_call(
        body, in_specs=[_ANY] * (2 * n), out_specs=[_ANY] * n,
        out_shape=[jax.ShapeDtypeStruct(a.shape, a.dtype) for a in lands],
        input_output_aliases={n + a: a for a in range(n)},
        scratch_shapes=[pltpu.SemaphoreType.DMA((n, 3)), pltpu.SemaphoreType.DMA((n, 3)), pltpu.SemaphoreType.DMA((n,))],
        name=name,
    )(*xs, *lands)


def _pair_exchange(xs, name):
    n = len(xs)

    def body(*refs):
        x_refs, o_refs = refs[:n], refs[n:2 * n]
        send_sems, recv_sems = refs[2 * n:]
        x, y, c = _place()
        sib = (x, y, 1 - c)

        def copy(a, q):
            return pltpu.make_async_remote_copy(
                src_ref=x_refs[a].at[2 * q + (1 - c)], dst_ref=o_refs[a].at[q],
                send_sem=send_sems.at[a, q], recv_sem=recv_sems.at[a, q], device_id=sib, device_id_type=MESH)

        cps = [copy(a, q) for a in range(n) for q in range(N_CHIP)]
        for cp in cps:
            cp.start()
        for cp in cps:
            cp.wait_recv()
        for cp in cps:
            cp.wait_send()

    any_spec = pl.BlockSpec(memory_space=pl.ANY)
    return pl.pallas_call(
        body, in_specs=[any_spec] * n, out_specs=[any_spec] * n,
        out_shape=[jax.ShapeDtypeStruct((N_CHIP, *a.shape[1:]), a.dtype) for a in xs],
        scratch_shapes=[pltpu.SemaphoreType.DMA((n, N_CHIP)), pltpu.SemaphoreType.DMA((n, N_CHIP))],
        name=name,
    )(*xs)


def _pair_sum(x, got, place, name):
    _, R, C = x.shape
    unit = SUBLANE * (4 // x.dtype.itemsize)
    tr = _tile(R, max(unit, (512 * 1024 // C) // unit * unit), unit)

    def body(p_ref, x_ref, g_ref, o_ref, land_ref):
        s = (x_ref[...].astype(F32) + g_ref[...].astype(F32)).astype(o_ref.dtype)
        o_ref[...] = s

        @pl.when(pl.program_id(1) == p_ref[1])
        def _():
            land_ref[...] = s

    blk = lambda f: pl.BlockSpec((None, tr, C), f)
    return pl.pallas_call(
        body,
        grid_spec=pltpu.PrefetchScalarGridSpec(
            num_scalar_prefetch=1, grid=(R // tr, N_CHIP),
            in_specs=[blk(lambda i, q, p: (2 * q + p[0], i, 0)), blk(lambda i, q, p: (q, i, 0))],
            out_specs=[blk(lambda i, q, p: (q, i, 0)), blk(lambda i, q, p: (p[1], i, 0))]),
        out_shape=[jax.ShapeDtypeStruct(got.shape, x.dtype)] * 2,
        compiler_params=_params(("parallel", "arbitrary"), VMEM_LIMIT), name=name,
    )(place, x, got)


def _adamw_math(w, g, m, v):
    m = ADAM_B1 * m + (1.0 - ADAM_B1) * g
    v = ADAM_B2 * v + (1.0 - ADAM_B2) * (g * g)
    m_hat = m / (1.0 - ADAM_B1 ** ADAM_STEP)
    v_hat = v / (1.0 - ADAM_B2 ** ADAM_STEP)
    return -ADAM_LR * (m_hat / (jnp.sqrt(v_hat) + ADAM_EPS) + ADAM_WD * w), m, v


def _adamw(w, m, v, parts, layer, prev, name):
    nl, R, C = w.shape
    P = parts.shape[0]
    unit = SUBLANE * (4 // parts.dtype.itemsize)
    tr = _tile(R, max(unit, (128 * 1024 // C) // unit * unit), unit)

    def body(w_ref, m_ref, v_ref, p_ref, *rest):
        g_ref, d_ref, nm_ref, nv_ref = rest[-4:]
        g = p_ref[0].astype(F32)
        for p in range(1, P):
            g = g + p_ref[p].astype(F32)
        d, nm, nv = _adamw_math(w_ref[...], g, m_ref[...], v_ref[...])
        g_ref[...], d_ref[...], nm_ref[...], nv_ref[...] = g, d, nm, nv

    lay = pl.BlockSpec((None, tr, C), lambda i: (layer, i, 0))
    in_specs = [lay, lay, lay, pl.BlockSpec((P, tr, C), lambda i: (0, i, 0))]
    ins = [w, m, v, parts]
    aliases = {}
    if prev is not None:
        in_specs += [pl.BlockSpec(memory_space=pl.ANY)] * 4
        ins += list(prev)
        aliases = {4 + k: k for k in range(4)}
    return pl.pallas_call(
        body, grid=(R // tr,), in_specs=in_specs, out_specs=[lay] * 4,
        out_shape=[jax.ShapeDtypeStruct(w.shape, F32)] * 4, input_output_aliases=aliases,
        compiler_params=_params(("parallel",), VMEM_LIMIT), name=name,
    )(*ins)


def _gelu_tanh(x):
    return jax.nn.gelu(x, approximate=True)


def _pack(arrs):
    tile = SUBLANE * LANE
    out = []
    for a in arrs:
        f = a.reshape(-1).astype(F32)
        out.append(jnp.pad(f, (0, (-f.shape[0]) % tile)))
    return jnp.concatenate(out)


def _unpack(flat, shapes):
    tile = SUBLANE * LANE
    out, off = [], 0
    for s in shapes:
        n = math.prod(s)
        out.append(flat[off:off + n].reshape(s))
        off += n + (-n) % tile
    return out


def kernel(x, gla_norm, gla_w_in, gla_w_gate_up, gla_b_gate, gla_o_norm, gla_w_out, s5_norm, s5_w_in, s5_lam_re, s5_lam_im, s5_log_dt, s5_b_re, s5_b_im, s5_c_re, s5_c_im, s5_d, s5_w_out, mlp_norm, mlp_w_up, mlp_w_down, final_norm, loss_target, m_gla_norm, m_gla_w_in, m_gla_w_gate_up, m_gla_b_gate, m_gla_o_norm, m_gla_w_out, m_s5_norm, m_s5_w_in, m_s5_lam_re, m_s5_lam_im, m_s5_log_dt, m_s5_b_re, m_s5_b_im, m_s5_c_re, m_s5_c_im, m_s5_d, m_s5_w_out, m_mlp_norm, m_mlp_w_up, m_mlp_w_down, m_final_norm, v_gla_norm, v_gla_w_in, v_gla_w_gate_up, v_gla_b_gate, v_gla_o_norm, v_gla_w_out, v_s5_norm, v_s5_w_in, v_s5_lam_re, v_s5_lam_im, v_s5_log_dt, v_s5_b_re, v_s5_b_im, v_s5_c_re, v_s5_c_im, v_s5_d, v_s5_w_out, v_mlp_norm, v_mlp_w_up, v_mlp_w_down, v_final_norm):
    W = dict(gla_norm=gla_norm, gla_w_in=gla_w_in, gla_w_gate_up=gla_w_gate_up, gla_b_gate=gla_b_gate, gla_o_norm=gla_o_norm, gla_w_out=gla_w_out, s5_norm=s5_norm, s5_w_in=s5_w_in, s5_lam_re=s5_lam_re, s5_lam_im=s5_lam_im, s5_log_dt=s5_log_dt, s5_b_re=s5_b_re, s5_b_im=s5_b_im, s5_c_re=s5_c_re, s5_c_im=s5_c_im, s5_d=s5_d, s5_w_out=s5_w_out, mlp_norm=mlp_norm, mlp_w_up=mlp_w_up, mlp_w_down=mlp_w_down, final_norm=final_norm)
    M = dict(gla_norm=m_gla_norm, gla_w_in=m_gla_w_in, gla_w_gate_up=m_gla_w_gate_up, gla_b_gate=m_gla_b_gate, gla_o_norm=m_gla_o_norm, gla_w_out=m_gla_w_out, s5_norm=m_s5_norm, s5_w_in=m_s5_w_in, s5_lam_re=m_s5_lam_re, s5_lam_im=m_s5_lam_im, s5_log_dt=m_s5_log_dt, s5_b_re=m_s5_b_re, s5_b_im=m_s5_b_im, s5_c_re=m_s5_c_re, s5_c_im=m_s5_c_im, s5_d=m_s5_d, s5_w_out=m_s5_w_out, mlp_norm=m_mlp_norm, mlp_w_up=m_mlp_w_up, mlp_w_down=m_mlp_w_down, final_norm=m_final_norm)
    V = dict(gla_norm=v_gla_norm, gla_w_in=v_gla_w_in, gla_w_gate_up=v_gla_w_gate_up, gla_b_gate=v_gla_b_gate, gla_o_norm=v_gla_o_norm, gla_w_out=v_gla_w_out, s5_norm=v_s5_norm, s5_w_in=v_s5_w_in, s5_lam_re=v_s5_lam_re, s5_lam_im=v_s5_lam_im, s5_log_dt=v_s5_log_dt, s5_b_re=v_s5_b_re, s5_b_im=v_s5_b_im, s5_c_re=v_s5_c_re, s5_c_im=v_s5_c_im, s5_d=v_s5_d, s5_w_out=v_s5_w_out, mlp_norm=v_mlp_norm, mlp_w_up=v_mlp_w_up, mlp_w_down=v_mlp_w_down, final_norm=v_final_norm)
    names = list(W)
    big = ["gla_w_in", "gla_w_out", "s5_w_in", "s5_w_out", "mlp_w_up", "mlp_w_down"]
    small_sharded = {"gla_w_gate_up": 2, "s5_norm": 1, "s5_d": 1}
    small = [n for n in names if n not in big]

    _, L, D = x.shape
    n_gla, n_s5, depth = gla_norm.shape[0], s5_lam_re.shape[0], mlp_norm.shape[0]
    H = GLA_HEADS
    KW, VW = D // 2, D
    DK, DV = KW // H, VW // H
    IN = 2 * KW + 2 * VW + GLA_RANK
    INP = 2 * KW + 2 * VW + LANE
    SW = s5_lam_re.shape[1] * S5_GROUP
    G, N = s5_lam_re.shape[1], s5_lam_re.shape[2]
    nb = G // S5_GB
    dev = _slot(_place())

    wb = {n: W[n].astype(BF16) for n in big}
    sm_sh = _pack([W[n] for n in small_sharded])
    groups = []
    for i in range(depth):
        j = i // 2
        groups.append([("gla_w_in", j), ("gla_w_out", j)] if i % 2 == 0 else [("s5_w_in", j), ("s5_w_out", j)])
        groups.append([("mlp_w_up", i), ("mlp_w_down", i)])
    groups[0] = [("small", 0)] + groups[0]
    tok = jnp.zeros((SUBLANE, LANE), F32)
    in_flight = []
    for gi, keys in enumerate(groups):
        srcs = [sm_sh if n == "small" else wb[n][j] for n, j in keys]
        lands = [lax.empty((N_DEV, *a.shape), a.dtype) for a in srcs]
        send_sems, recv_sems, srcs, lands, tok = _split_start(srcs, lands, _gather_routes, 4, tok, f"gather_start_{gi}")
        in_flight.append((send_sems, recv_sems, srcs, lands))
    full = {}

    def fetch(gi, after):
        send_sems, recv_sems, srcs, lands = in_flight[gi]
        srcs, lands = _split_wait(send_sems, recv_sems, srcs, lands, _gather_routes, after, f"gather_wait_{gi}")
        for key, g in zip(groups[gi], _gather_forward(srcs, lands, f"gather_pass_{gi}")):
            full[key] = g

    fetch(0, tok)
    sm_all = full["small", 0]
    sm_parts = [_unpack(sm_all[d], [W[n].shape for n in small_sharded]) for d in range(N_DEV)]
    wgu_full = jnp.concatenate([p[0] for p in sm_parts], axis=2)
    s5n_full = jnp.concatenate([p[1] for p in sm_parts], axis=1)
    s5d_full = jnp.concatenate([p[2] for p in sm_parts], axis=1)

    def gla_weights(j):
        w_in = full["gla_w_in", j]
        w_in = jnp.transpose(w_in, (1, 0, 2)).reshape(D, IN)
        w_in = jnp.pad(w_in, ((0, 0), (0, INP - IN)))
        w_out = full["gla_w_out", j].reshape(VW, D)
        wgu = jnp.pad(wgu_full[j], ((0, LANE - GLA_RANK), (0, 0))).astype(BF16)
        return w_in, w_out, wgu

    grads = {}
    h = x[0]
    saved = []

    for i in range(depth):
        j = i // 2
        if i > 0:
            fetch(2 * i, h)
        if i % 2 == 0:
            w_in, w_out, wgu = gla_weights(j)
            gn = gla_norm[j][None]
            hn = _rms_fwd(h, gn, f"gla{j}_norm")
            proj = _mm(hn, w_in, "nn", name=f"gla{j}_proj", tn=896)
            bg, on = gla_b_gate[j][None], gla_o_norm[j][None]
            y, st = _gla_fwd(proj, wgu, bg, on, H=H, DK=DK, DV=DV, name=f"gla{j}_mix")
            h_new = _mm(y, w_out, "nn", name=f"gla{j}_out", res=h)
            saved.append(("gla", dict(h=h, hn=hn, proj=proj, y=y, st=st, w_in=w_in, w_out=w_out, wgu=wgu, gn=gn, bg=bg, on=on)))
        else:
            w_in = full["s5_w_in", j].reshape(D, SW)
            w_out3 = full["s5_w_out", j]
            sn = s5n_full[j][None]
            hn = _rms_fwd(h, sn, f"s5{j}_norm")
            u = _mm(hn, w_in, "nn", name=f"s5{j}_in")
            lre, lim = s5_lam_re[j][:, None, :], s5_lam_im[j][:, None, :]
            ldt = s5_log_dt[j][:, None, None]
            brt, bit = jnp.swapaxes(s5_b_re[j], 1, 2), jnp.swapaxes(s5_b_im[j], 1, 2)
            ar, ai, bbr, bbi, pr, pi, a124 = _s5_params(lre, lim, ldt, brt, bit, f"s5{j}_params")
            flat = lambda t: jnp.swapaxes(t, 0, 1).reshape(t.shape[1], G * N)
            pr, pi, a124 = flat(pr), flat(pi), flat(a124)
            bur = _mm_bd(u, _bd(bbr), name=f"s5{j}_bu_re")
            bui = _mm_bd(u, _bd(bbi), name=f"s5{j}_bu_im")
            xr, xi = _s5_scan(bur, bui, a124, pr, pi, reverse=False, name=f"s5{j}_scan")
            crt, cit = jnp.swapaxes(s5_c_re[j], 1, 2), jnp.swapaxes(s5_c_im[j], 1, 2)
            cx = _mm_bd(xr, _bd(crt), name=f"s5{j}_cx_re")
            cx = _mm_bd(xi, _bd(-cit), name=f"s5{j}_cx_im", res=cx)
            dsk = s5d_full[j][None]

            def act(cxv, uv, dv):
                ypre = cxv + dv * uv
                return ypre, _gelu_tanh(ypre)
            ypre, yg = _rowmap(act, [cx, u], [dsk], [(SW, F32), (SW, BF16)], name=f"s5{j}_act")
            z = _mm_nn_cb(yg, w_out3, name=f"s5{j}_out")

            def glu(zv, hv):
                return (hv + zv[:, :D] * jax.nn.sigmoid(zv[:, D:]),)
            h_new = _rowmap(glu, [z, h], [], [(D, F32)], name=f"s5{j}_glu")[0]
            saved.append(("s5", dict(h=h, hn=hn, u=u, xr=xr, xi=xi, ypre=ypre, yg=yg, z=z, w_in=w_in, w_out3=w_out3, sn=sn,
                                     dsk=dsk, prm=(lre, lim, ldt, brt, bit), a124=a124, pr=pr, pi=pi, bbr=bbr, bbi=bbi)))
        h = h_new
        fetch(2 * i + 1, h)
        w_up3 = full["mlp_w_up", i]
        w_down = full["mlp_w_down", i].reshape(4 * D, D)
        mn = mlp_norm[i][None]
        hn = _rms_fwd(h, mn, f"mlp{i}_norm")
        def sq_relu(zv):
            a = jnp.maximum(zv, 0.0)
            return zv, a * a
        z, s = _mm_nn_cb(hn, w_up3, name=f"mlp{i}_up", out_dtype=[F32, BF16], epi=sq_relu)
        h_new = _mm(s, w_down, "nn", name=f"mlp{i}_down", res=h)
        saved.append(("mlp", dict(h=h, hn=hn, z=z, s=s, w_up3=w_up3, w_down=w_down, mn=mn)))
        h = h_new

    dh, sq, dfin = _loss_head(h, loss_target[0], final_norm[None], "loss_head")
    loss = lax.psum(0.5 * jnp.sum(sq) / D, ("x", "y", "c"))
    small_grads = {"final_norm": dfin[0]}
    big_parts = {n: {} for n in big}
    stacks = {n: {} for n in small if n != "final_norm"}
    place = jnp.stack([lax.axis_index("c"), 2 * lax.axis_index("x") + lax.axis_index("y")]).astype(jnp.int32)
    flat3 = lambda a: a.reshape(a.shape[0], math.prod(a.shape[1:-1]), a.shape[-1])
    exchanges = []

    def send_grads(keys):
        tag = "_".join(f"{n}{j}" for n, j in keys)
        parts8 = [flat3(big_parts[n][j]) for n, j in keys]
        from_sib = _pair_exchange(parts8, name=f"grads_pair_{tag}")
        sums, lands = zip(*[_pair_sum(p, g, place, f"grads_pair_sum_{n}{j}") for (n, j), p, g in zip(keys, parts8, from_sib)])
        send_sems, recv_sems, srcs, lands, token = _split_start(list(sums), list(lands), _chip_routes, 3, sums[0], f"grads_start_{tag}")
        exchanges.append((keys, tag, send_sems, recv_sems, srcs, lands))
        return token

    for idx in range(len(saved) - 1, -1, -1):
        kind, s = saved[idx]
        li = sum(1 for k, _ in saved[:idx] if k == kind)
        if kind == "mlp":
            def sq_relu_bwd(dsv, zv):
                return (dsv * 2.0 * jnp.maximum(zv, 0.0),)
            dz = _mm(dh, s["w_down"], "nt", name=f"mlp{li}_dz", out_dtype=BF16, epi=sq_relu_bwd, epi_ins=(s["z"],))
            big_parts["mlp_w_down"][li] = _mm(s["s"], dh, "tn", name=f"mlp{li}_dw_down", out_dtype=BF16).reshape(N_DEV, 4 * D // N_DEV, D)
            big_parts["mlp_w_up"][li] = _mm_tn_cbout(s["hn"], dz, N_DEV, name=f"mlp{li}_dw_up")
            token = send_grads([("mlp_w_down", li), ("mlp_w_up", li)])
            dhn = _mm_nt_cb(dz, s["w_up3"], name=f"mlp{li}_dhn")
            dh, dg = _rms_bwd(s["h"], dhn, dh, s["mn"], f"mlp{li}_dnorm", deps=(token,))
            stacks["mlp_norm"][li] = dg[0]
        elif kind == "gla":
            dyv = _mm(dh, s["w_out"], "nt", name=f"gla{li}_dy")
            big_parts["gla_w_out"][li] = _mm(s["y"], dh, "tn", name=f"gla{li}_dw_out", out_dtype=BF16).reshape(N_DEV, VW // N_DEV, D)
            dqkvr, dpre, don = _gla_bwd(s["proj"], s["wgu"], s["bg"], s["on"], s["st"], dyv, H=H, DK=DK, DV=DV, name=f"gla{li}_dmix")
            dglow = _mm(dpre, s["wgu"], "nt", name=f"gla{li}_dglow", out_dtype=BF16)
            glow = s["proj"][:, 2 * KW + 2 * VW:]
            dwgu = _mm(glow, dpre, "tn", name=f"gla{li}_dwgu")[:GLA_RANK]
            dbg = _rowmap(lambda t: (jnp.sum(t, axis=0, keepdims=True),), [dpre], [], [], [KW], name=f"gla{li}_dbg")[0]
            dproj = jnp.concatenate([dqkvr, dglow], axis=1)
            dw_in = _mm(s["hn"], dproj, "tn", name=f"gla{li}_dw_in", out_dtype=BF16, tn=896)[:, :IN]
            big_parts["gla_w_in"][li] = jnp.transpose(dw_in.reshape(D, N_DEV, IN // N_DEV), (1, 0, 2))
            token = send_grads([("gla_w_out", li), ("gla_w_in", li)])
            dhn = _mm(dproj, s["w_in"], "nt", name=f"gla{li}_dhn", tk=896)
            dh, dg = _rms_bwd(s["h"], dhn, dh, s["gn"], f"gla{li}_dnorm", deps=(token,))
            stacks["gla_norm"][li], stacks["gla_b_gate"][li], stacks["gla_o_norm"][li] = dg[0], dbg[0], don[0]
            stacks["gla_w_gate_up"][li] = dwgu
        else:
            def glu_bwd(zv, dhv):
                a, sg = zv[:, :D], jax.nn.sigmoid(zv[:, D:])
                return (jnp.concatenate([dhv * sg, dhv * a * sg * (1.0 - sg)], axis=1),)
            dz = _rowmap(glu_bwd, [s["z"], dh], [], [(2 * D, BF16)], name=f"s5{li}_dglu")[0]
            big_parts["s5_w_out"][li] = _mm_tn_cbout(s["yg"], dz, N_DEV, name=f"s5{li}_dw_out")
            dyg = _mm_nt_cb(dz, s["w_out3"], name=f"s5{li}_dyg")

            def act_bwd(dygv, ypv, uv, dv):
                _, vjp = jax.vjp(_gelu_tanh, ypv)
                dyp = vjp(dygv)[0]
                return dyp, dyp * dv, jnp.sum(dyp * uv, axis=0, keepdims=True)
            dyp, du, dd = _rowmap(act_bwd, [dyg, s["ypre"], s["u"]], [s["dsk"]], [(SW, F32), (SW, F32)], [SW], name=f"s5{li}_dact")
            cre, cim = s5_c_re[li], s5_c_im[li]
            wr = _mm_bd(dyp, _bd(cre), name=f"s5{li}_w_re")
            wi = _mm_bd(dyp, _bd(-cim), name=f"s5{li}_w_im")
            a124c = s["a124"] * jnp.array([1, -1, 1, -1, 1, -1, 1, 1], F32)[:, None]
            lr_, li_ = _s5_scan(wr, wi, a124c, s["pr"][::-1], -s["pi"][::-1], reverse=True, name=f"s5{li}_dscan")
            du = _mm_bd(lr_, _bd(jnp.swapaxes(s["bbr"], 1, 2)), name=f"s5{li}_du_re", res=du)
            du = _mm_bd(li_, _bd(jnp.swapaxes(s["bbi"], 1, 2)), name=f"s5{li}_du_im", res=du)
            dbbr = _bd_extract(_mm_tn_bd(s["u"], lr_, nb, name=f"s5{li}_dbb_re"), S5_GROUP, N)
            dbbi = _bd_extract(_mm_tn_bd(s["u"], li_, nb, name=f"s5{li}_dbb_im"), S5_GROUP, N)
            dcr = _bd_extract(_mm_tn_bd(dyp, s["xr"], nb, name=f"s5{li}_dc_re"), S5_GROUP, N)
            dci = -_bd_extract(_mm_tn_bd(dyp, s["xi"], nb, name=f"s5{li}_dc_im"), S5_GROUP, N)
            dar, dai = _s5_dabar(lr_, li_, s["xr"], s["xi"], f"s5{li}_dabar")
            dlre, dlim, dldt, dbrt, dbit = _s5_params_bwd(*s["prm"], dar.reshape(G, 1, N), dai.reshape(G, 1, N), dbbr, dbbi,
                                                         f"s5{li}_dparams")
            big_parts["s5_w_in"][li] = _mm(s["hn"], du, "tn", name=f"s5{li}_dw_in", out_dtype=BF16).reshape(N_DEV, D // N_DEV, SW)
            token = send_grads([("s5_w_out", li), ("s5_w_in", li)])
            dhn = _mm(du, s["w_in"], "nt", name=f"s5{li}_dhn")
            dh, dg = _rms_bwd(s["h"], dhn, dh, s["sn"], f"s5{li}_dnorm", deps=(token,))
            stacks["s5_norm"][li], stacks["s5_d"][li] = dg[0], dd[0]
            stacks["s5_lam_re"][li], stacks["s5_lam_im"][li], stacks["s5_log_dt"][li] = dlre[:, 0], dlim[:, 0], dldt[:, 0, 0]
            stacks["s5_b_re"][li], stacks["s5_b_im"][li] = jnp.swapaxes(dbrt, 1, 2), jnp.swapaxes(dbit, 1, 2)
            stacks["s5_c_re"][li], stacks["s5_c_im"][li] = dcr, dci
    grad_x = dh[None]

    for n, st_ in stacks.items():
        small_grads[n] = jnp.stack([st_[k] for k in range(len(st_))])
    full_shapes = [small_grads[n].shape for n in small]
    part = _pack([small_grads[n] for n in small])
    parts_all = _all_gather([part], name="gather_small_grads")[0]

    def mine(n, a):
        if n not in small_sharded:
            return a
        ax = small_sharded[n]
        size = W[n].shape[ax]
        return lax.dynamic_slice_in_dim(a, dev * size, size, axis=ax)

    rows = parts_all.shape[1] // LANE
    def full_layout(T):
        arrs = []
        for n, shp in zip(small, full_shapes):
            if n in small_sharded:
                ax = small_sharded[n]
                arrs.append(lax.dynamic_update_slice_in_dim(jnp.zeros(shp, F32), T[n], dev * T[n].shape[ax], axis=ax))
            else:
                arrs.append(T[n])
        return _pack(arrs)
    ws, ms, vs = full_layout(W), full_layout(M), full_layout(V)
    sg, sd, snm, snv = _adamw(ws.reshape(1, rows, LANE), ms.reshape(1, rows, LANE), vs.reshape(1, rows, LANE),
                              parts_all.reshape(N_DEV, rows, LANE), 0, None, "adamw_small")
    outs = {}
    for key, flat in zip(("grad", "delta", "new_m", "new_v"), (sg, sd, snm, snv)):
        for n, a in zip(small, _unpack(flat.reshape(-1), full_shapes)):
            outs[key, n] = mine(n, a)

    stacked = {n: None for n in big}
    as3 = lambda a: a.reshape(a.shape[0], math.prod(a.shape[1:-1]), a.shape[-1])
    for keys, tag, send_sems, recv_sems, srcs, lands in exchanges:
        _, lands = _split_wait(send_sems, recv_sems, srcs, lands, _chip_routes, dh, f"grads_wait_{tag}")
        for (n, j), parts in zip(keys, lands):
            stacked[n] = _adamw(as3(W[n]), as3(M[n]), as3(V[n]), parts, j, stacked[n], f"adamw_{n}_{j}")
    for n in big:
        for key, a in zip(("grad", "delta", "new_m", "new_v"), stacked[n]):
            outs[key, n] = a.reshape(W[n].shape)

    return (loss, grad_x, *[outs["grad", n] for n in names], *[outs["delta", n] for n in names],
            *[outs["new_m", n] for n in names], *[outs["new_v", n] for n in names])
```

```python
import functools
import math

import jax
import jax.numpy as jnp
from jax import lax
from jax.experimental import pallas as pl
from jax.experimental.pallas import tpu as pltpu

F32, BF16 = jnp.float32, jnp.bfloat16
MESH = pl.DeviceIdType.MESH
N_DEV = 8
LANE = 128
SUBLANE = 8
VMEM_LIMIT = 48 * 1024 * 1024

EPS = 1e-6
CHUNK = 64
GLA_HEADS = 4
GLA_RANK = 16
GLA_TEMP = 16.0
S5_GROUP = 16
S5_STATE = 64
S5_EIG_CLIP = -1e-4
S5_GB = 16
ADAM_LR, ADAM_B1, ADAM_B2, ADAM_EPS, ADAM_WD, ADAM_STEP = 0.001, 0.9, 0.999, 1e-08, 0.01, 10

_CONTRACT = {"nn": ((1,), (0,)), "tn": ((0,), (0,)), "nt": ((1,), (1,))}


def _tile(n, pref, unit=LANE):
    if n <= pref:
        return n
    t = (pref // unit) * unit
    while t > unit and n % t:
        t -= unit
    assert n % t == 0, (n, pref, unit)
    return t


def _params(sem, vmem=None):
    return pltpu.CompilerParams(dimension_semantics=sem, vmem_limit_bytes=vmem)


def _dot(a, b, dims, precision=None):
    return lax.dot_general(a, b, (_CONTRACT[dims], ((), ())), preferred_element_type=F32, precision=precision)


def _bdot(a, b, dims):
    return _dot(a.astype(BF16), b.astype(BF16), dims)


def _mm_call(a, b, *, dims, grid, a_spec, b_spec, o_spec, out_shape, out_dtype, name, res=None, epi=None, epi_ins=()):
    nk = grid[2]
    acc_shape = tuple(d for d in o_spec.block_shape if d is not None)
    if res is not None:
        epi, epi_ins = (lambda r, x: (r + x,)), (res,)
    single = not isinstance(out_dtype, (list, tuple))
    out_dtypes = [out_dtype] if single else list(out_dtype)
    n_e, n_o = len(epi_ins), len(out_dtypes)

    def body(*refs):
        a_ref, b_ref = refs[:2]
        e_refs, o_refs = refs[2:2 + n_e], refs[2 + n_e:2 + n_e + n_o]

        def finish(r):
            vals = (r,) if epi is None else epi(r, *[e[...].astype(F32) for e in e_refs])
            for o_ref, v in zip(o_refs, vals):
                o_ref[...] = v.astype(o_ref.dtype)

        d = _bdot(a_ref[...], b_ref[...], dims)
        if nk == 1:
            finish(d)
            return
        acc = refs[-1]
        k = pl.program_id(2)

        @pl.when(k == 0)
        def _():
            acc[...] = d

        @pl.when((k > 0) & (k < nk - 1))
        def _():
            acc[...] += d

        @pl.when(k == nk - 1)
        def _():
            finish(acc[...] + d)

    outs = pl.pallas_call(
        body, grid=grid, in_specs=[a_spec, b_spec] + [o_spec] * n_e, out_specs=[o_spec] * n_o,
        out_shape=[jax.ShapeDtypeStruct(out_shape, dt) for dt in out_dtypes],
        scratch_shapes=[] if nk == 1 else [pltpu.VMEM(acc_shape, F32)],
        compiler_params=_params(("parallel", "parallel", "arbitrary"), VMEM_LIMIT), name=name,
    )(a, b, *epi_ins)
    return outs[0] if single else outs


TM, TN, TK = 1024, 1024, 2048


def _mm(a, b, dims, *, name, out_dtype=F32, res=None, epi=None, epi_ins=(), tm=TM, tn=TN, tk=TK):
    if dims == "tn":
        (K, M), (_, N) = a.shape, b.shape
    elif dims == "nn":
        (M, K), (_, N) = a.shape, b.shape
    else:
        (M, K), (N, _) = a.shape, b.shape
    tm, tn, tk = _tile(M, tm), _tile(N, tn), _tile(K, tk)
    a_spec = pl.BlockSpec((tk, tm), lambda i, j, k: (k, i)) if dims == "tn" else pl.BlockSpec((tm, tk), lambda i, j, k: (i, k))
    b_spec = pl.BlockSpec((tn, tk), lambda i, j, k: (j, k)) if dims == "nt" else pl.BlockSpec((tk, tn), lambda i, j, k: (k, j))
    o_spec = pl.BlockSpec((tm, tn), lambda i, j, k: (i, j))
    return _mm_call(a, b, dims=dims, grid=(M // tm, N // tn, K // tk), a_spec=a_spec, b_spec=b_spec, o_spec=o_spec,
                    out_shape=(M, N), out_dtype=out_dtype, name=name, res=res, epi=epi, epi_ins=epi_ins)


def _mm_nn_cb(a, b3, *, name, out_dtype=F32, epi=None, tm=TM, tn=TN, tk=TK):
    (M, K), (P, _, Ns) = a.shape, b3.shape
    tm, tn, tk = _tile(M, tm), _tile(Ns, tn), _tile(K, tk)
    npb = Ns // tn
    return _mm_call(a, b3, dims="nn", grid=(M // tm, P * npb, K // tk),
                    a_spec=pl.BlockSpec((tm, tk), lambda i, j, k: (i, k)),
                    b_spec=pl.BlockSpec((None, tk, tn), lambda i, j, k: (j // npb, k, j % npb)),
                    o_spec=pl.BlockSpec((tm, tn), lambda i, j, k: (i, j)),
                    out_shape=(M, P * Ns), out_dtype=out_dtype, name=name, epi=epi)


def _mm_nt_cb(a, b3, *, name, out_dtype=F32, tm=TM, tn=TN, tk=TK):
    (M, _), (P, N, Ns) = a.shape, b3.shape
    tm, tn, tk = _tile(M, tm), _tile(N, tn), _tile(Ns, tk)
    kpb = Ns // tk
    return _mm_call(a, b3, dims="nt", grid=(M // tm, N // tn, P * kpb),
                    a_spec=pl.BlockSpec((tm, tk), lambda i, j, k: (i, k)),
                    b_spec=pl.BlockSpec((None, tn, tk), lambda i, j, k: (k // kpb, j, k % kpb)),
                    o_spec=pl.BlockSpec((tm, tn), lambda i, j, k: (i, j)),
                    out_shape=(M, N), out_dtype=out_dtype, name=name)


def _mm_tn_cbout(a, b, parts, *, name, out_dtype=BF16, tm=TM, tn=TN, tk=TK):
    (K, M), (_, N) = a.shape, b.shape
    Ns = N // parts
    tm, tn, tk = _tile(M, tm), _tile(Ns, tn), _tile(K, tk)
    npb = Ns // tn
    return _mm_call(a, b, dims="tn", grid=(M // tm, parts * npb, K // tk),
                    a_spec=pl.BlockSpec((tk, tm), lambda i, j, k: (k, i)),
                    b_spec=pl.BlockSpec((tk, tn), lambda i, j, k: (k, j)),
                    o_spec=pl.BlockSpec((None, tm, tn), lambda i, j, k: (j // npb, i, j % npb)),
                    out_shape=(parts, M, Ns), out_dtype=out_dtype, name=name)


def _mm_bd(a, w3, *, name, out_dtype=F32, res=None, tm=2048, tn=1024):
    (M, _), (nb, Kb, Nb) = a.shape, w3.shape
    tm, tn = _tile(M, tm), _tile(Nb, tn)
    npb = Nb // tn
    return _mm_call(a, w3, dims="nn", grid=(M // tm, nb * npb, 1),
                    a_spec=pl.BlockSpec((tm, Kb), lambda i, j, k: (i, j // npb)),
                    b_spec=pl.BlockSpec((None, Kb, tn), lambda i, j, k: (j // npb, 0, j % npb)),
                    o_spec=pl.BlockSpec((tm, tn), lambda i, j, k: (i, j)),
                    out_shape=(M, nb * Nb), out_dtype=out_dtype, name=name, res=res)


def _mm_tn_bd(a, b, nb, *, name, tk=1024):
    (K, MA), (_, NB) = a.shape, b.shape
    Ma, Nb = MA // nb, NB // nb
    tk = _tile(K, tk)
    return _mm_call(a, b, dims="tn", grid=(nb, 1, K // tk),
                    a_spec=pl.BlockSpec((tk, Ma), lambda i, j, k: (k, i)),
                    b_spec=pl.BlockSpec((tk, Nb), lambda i, j, k: (k, i)),
                    o_spec=pl.BlockSpec((None, Ma, Nb), lambda i, j, k: (i, 0, 0)),
                    out_shape=(nb, Ma, Nb), out_dtype=F32, name=name)


def _rowmap(fn, rows, consts, out_defs, red_defs=(), *, name, tr=256):
    L = rows[0].shape[0]
    widest = max([r.shape[1] for r in rows] + [n for n, _ in out_defs])
    tr = _tile(L, max(SUBLANE * 2, min(tr, 512 * 1024 // widest)), SUBLANE * 2)
    n_in, n_o, n_d = len(rows) + len(consts), len(out_defs), len(red_defs)

    def body(*refs):
        res = fn(*[r[...] for r in refs[:n_in]])
        res = res if isinstance(res, (tuple, list)) else (res,)
        outs = refs[n_in:]
        for o_ref, val in zip(outs[:n_o], res[:n_o]):
            o_ref[...] = val.astype(o_ref.dtype)
        if n_d:
            @pl.when(pl.program_id(0) == 0)
            def _():
                for o_ref in outs[n_o:]:
                    o_ref[...] = jnp.zeros_like(o_ref)
            for o_ref, val in zip(outs[n_o:], res[n_o:]):
                o_ref[...] += val

    in_specs = [pl.BlockSpec((tr, r.shape[1]), lambda i: (i, 0)) for r in rows]
    in_specs += [pl.BlockSpec(c.shape, lambda i, nd=c.ndim: (0,) * nd) for c in consts]
    out_specs = [pl.BlockSpec((tr, n), lambda i: (i, 0)) for n, _ in out_defs]
    out_specs += [pl.BlockSpec((1, n), lambda i: (0, 0)) for n in red_defs]
    out_shape = [jax.ShapeDtypeStruct((L, n), dt) for n, dt in out_defs]
    out_shape += [jax.ShapeDtypeStruct((1, n), F32) for n in red_defs]
    return pl.pallas_call(body, grid=(L // tr,), in_specs=in_specs, out_specs=out_specs, out_shape=out_shape,
                          compiler_params=_params(("arbitrary",), VMEM_LIMIT), name=name)(*rows, *consts)


def _rms_parts(x):
    r = lax.rsqrt(jnp.mean(x * x, axis=-1, keepdims=True) + EPS)
    return r, x * r


def _rms_fwd(h, g, name, deps=()):
    def fn(x, gg, *_):
        _, xh = _rms_parts(x)
        return (xh * gg,)
    return _rowmap(fn, [h], [g, *deps], [(h.shape[1], BF16)], name=name)[0]


def _rms_bwd(h, dhn, dh, g, name, deps=()):
    def fn(x, dy, dres, gg, *_):
        r, xh = _rms_parts(x)
        dxh = dy * gg
        dx = r * (dxh - xh * jnp.mean(dxh * xh, axis=-1, keepdims=True))
        return dres + dx, jnp.sum(dy * xh, axis=0, keepdims=True)
    D = h.shape[1]
    return _rowmap(fn, [h, dhn, dh], [g, *deps], [(D, F32)], [D], name=name)


def _loss_head(h, tgt, g, name):
    D = h.shape[1]

    def fn(x, t, gg):
        r, xh = _rms_parts(x)
        diff = xh * gg - t
        dy = diff * (1.0 / D)
        dxh = dy * gg
        dx = r * (dxh - xh * jnp.mean(dxh * xh, axis=-1, keepdims=True))
        return dx, jnp.sum(diff * diff, axis=0, keepdims=True), jnp.sum(dy * xh, axis=0, keepdims=True)
    return _rowmap(fn, [h, tgt], [g], [(D, F32)], [D, D], name=name)


def _tri(n, strict):
    r = lax.broadcasted_iota(jnp.int32, (n, n), 0)
    c = lax.broadcasted_iota(jnp.int32, (n, n), 1)
    return jnp.where((c < r) if strict else (c <= r), 1.0, 0.0).astype(F32)


def _gla_gate(g_ref, wgu_ref, bg_ref):
    pre = _bdot(g_ref[...], wgu_ref[...], "nn") + bg_ref[...]
    la = (jnp.minimum(pre, 0.0) - jnp.log(1.0 + jnp.exp(-jnp.abs(pre)))) * (1.0 / GLA_TEMP)
    cum = _dot(_tri(CHUNK, False), la, "nn", lax.Precision.HIGHEST)
    return pre, cum, cum[CHUNK - 1:CHUNK, :]


def _gla_specs(H, DK, DV, cmap):
    kb, vb, gb = H, (2 * H * DK) // DV, (2 * H * DK + 2 * H * DV) // LANE
    return [
        pl.BlockSpec((CHUNK, DK), lambda h, c: (cmap(h, c), h)),
        pl.BlockSpec((CHUNK, DK), lambda h, c: (cmap(h, c), kb + h)),
        pl.BlockSpec((CHUNK, DV), lambda h, c: (cmap(h, c), vb + h)),
        pl.BlockSpec((CHUNK, DV), lambda h, c: (cmap(h, c), vb + H + h)),
        pl.BlockSpec((CHUNK, LANE), lambda h, c: (cmap(h, c), gb)),
        pl.BlockSpec((LANE, DK), lambda h, c: (0, h)),
        pl.BlockSpec((1, DK), lambda h, c: (0, h)),
        pl.BlockSpec((1, DV), lambda h, c: (0, 0)),
    ]


def _gla_fwd(proj, wgu, bg, on, *, H, DK, DV, name):
    L = proj.shape[0]
    nc = L // CHUNK
    scale = DK ** -0.5

    def body(q_ref, k_ref, v_ref, r_ref, g_ref, wgu_ref, bg_ref, on_ref, y_ref, st_ref, S):
        @pl.when(pl.program_id(1) == 0)
        def _():
            S[...] = jnp.zeros_like(S)
        _, cum, total = _gla_gate(g_ref, wgu_ref, bg_ref)
        kd = k_ref[...] * jnp.exp(total - cum)
        St = S[...] * jnp.exp(total) + _bdot(v_ref[...], kd, "tn")
        S[...] = St
        st_ref[...] = St
        o = _bdot(q_ref[...] * scale, St, "nt")
        _, oh = _rms_parts(o)
        y_ref[...] = (oh * on_ref[...] * jax.nn.silu(r_ref[...])).astype(y_ref.dtype)

    return pl.pallas_call(
        body, grid=(H, nc), in_specs=_gla_specs(H, DK, DV, lambda h, c: c),
        out_specs=[pl.BlockSpec((CHUNK, DV), lambda h, c: (c, h)),
                   pl.BlockSpec((None, None, DV, DK), lambda h, c: (h, c, 0, 0))],
        out_shape=[jax.ShapeDtypeStruct((L, H * DV), BF16), jax.ShapeDtypeStruct((H, nc, DV, DK), F32)],
        scratch_shapes=[pltpu.VMEM((DV, DK), F32)],
        compiler_params=_params(("arbitrary", "arbitrary"), VMEM_LIMIT), name=name,
    )(proj, proj, proj, proj, proj, wgu, bg, on)


def _gla_bwd(proj, wgu, bg, on, st, dy, *, H, DK, DV, name):
    L = proj.shape[0]
    nc = L // CHUNK
    scale = DK ** -0.5
    rev = lambda h, c: nc - 1 - c

    def body(q_ref, k_ref, v_ref, r_ref, g_ref, wgu_ref, bg_ref, on_ref, sc_ref, sp_ref, dy_ref,
             dq_ref, dk_ref, dv_ref, dr_ref, dpre_ref, don_ref, G, decn):
        h, c = pl.program_id(0), pl.program_id(1)

        @pl.when(c == 0)
        def _():
            G[...] = jnp.zeros_like(G)
            decn[...] = jnp.zeros_like(decn)

        @pl.when((c == 0) & (h == 0))
        def _():
            don_ref[...] = jnp.zeros_like(don_ref)

        pre, cum, total = _gla_gate(g_ref, wgu_ref, bg_ref)
        ex = jnp.exp(total - cum)
        k, v, r = k_ref[...], v_ref[...], r_ref[...]
        kd = k * ex
        dec = jnp.exp(total)
        qs = q_ref[...] * scale
        Sc = sc_ref[...]
        o = _bdot(qs, Sc, "nt")
        rinv, oh = _rms_parts(o)
        gn = on_ref[...]
        sg = jax.nn.sigmoid(r)
        dyv = dy_ref[...]
        d_on = dyv * (r * sg)
        dr_ref[...] = (dyv * (oh * gn) * (sg * (1.0 + r * (1.0 - sg)))).astype(dr_ref.dtype)
        don_ref[...] += jnp.sum(d_on * oh, axis=0, keepdims=True)
        dxh = d_on * gn
        do = rinv * (dxh - oh * jnp.mean(dxh * oh, axis=-1, keepdims=True))
        dq_ref[...] = (_bdot(do, Sc, "nn") * scale).astype(dq_ref.dtype)
        Gt = G[...] * decn[...] + _bdot(do, qs, "tn")
        G[...] = Gt
        decn[...] = dec
        dkd = _bdot(v, Gt, "nn")
        dv_ref[...] = _bdot(kd, Gt, "nt").astype(dv_ref.dtype)
        Sp = sp_ref[...] * jnp.where(c == nc - 1, 0.0, 1.0)
        ddec = jnp.sum(Gt * Sp, axis=0, keepdims=True)
        dk_ref[...] = (dkd * ex).astype(dk_ref.dtype)
        dla = ddec * dec + _dot(_tri(CHUNK, True), dkd * kd, "nn", lax.Precision.HIGHEST)
        dpre_ref[...] = dla * (1.0 / GLA_TEMP) * jax.nn.sigmoid(-pre)

    in_specs = _gla_specs(H, DK, DV, rev) + [
        pl.BlockSpec((None, None, DV, DK), lambda h, c: (h, rev(h, c), 0, 0)),
        pl.BlockSpec((None, None, DV, DK), lambda h, c: (h, jnp.maximum(rev(h, c) - 1, 0), 0, 0)),
        pl.BlockSpec((CHUNK, DV), lambda h, c: (rev(h, c), h)),
    ]
    blk_k = pl.BlockSpec((CHUNK, DK), lambda h, c: (rev(h, c), h))
    blk_v = pl.BlockSpec((CHUNK, DV), lambda h, c: (rev(h, c), h))
    out_specs = [blk_k, blk_k, blk_v, blk_v, blk_k, pl.BlockSpec((1, DV), lambda h, c: (0, 0))]
    sds = jax.ShapeDtypeStruct
    out_shape = [sds((L, H * DK), BF16), sds((L, H * DK), BF16), sds((L, H * DV), BF16), sds((L, H * DV), BF16),
                 sds((L, H * DK), F32), sds((1, DV), F32)]
    dq, dk, dv, dr, dpre, don = pl.pallas_call(
        body, grid=(H, nc), in_specs=in_specs, out_specs=out_specs, out_shape=out_shape,
        scratch_shapes=[pltpu.VMEM((DV, DK), F32), pltpu.VMEM((1, DK), F32)],
        compiler_params=_params(("arbitrary", "arbitrary"), VMEM_LIMIT), name=name,
    )(proj, proj, proj, proj, proj, wgu, bg, on, st, st, dy)
    dqkvr = jnp.concatenate([dq, dk, dv, dr], axis=1)
    return dqkvr, dpre, don


def _s5_param_fn(lam_re, lam_im, log_dt, brt, bit):
    lr = jnp.minimum(lam_re, S5_EIG_CLIP)
    li = lam_im
    dt = jnp.exp(log_dt)
    mag = jnp.exp(lr * dt)
    ang = li * dt
    ab_re = mag * jnp.cos(ang)
    ab_im = mag * jnp.sin(ang)
    den = lr * lr + li * li
    nr = ab_re - 1.0
    f_re = (nr * lr + ab_im * li) / den
    f_im = (ab_im * lr - nr * li) / den
    return ab_re, ab_im, f_re * brt - f_im * bit, f_re * bit + f_im * brt


def _s5_params(lam_re, lam_im, log_dt, brt, bit, name):
    G, _, N = lam_re.shape

    def body(lr_ref, li_ref, dt_ref, br_ref, bi_ref, ar_ref, ai_ref, bbr_ref, bbi_ref, pr_ref, pi_ref, a124_ref):
        lre, lim, ldt = lr_ref[...], li_ref[...], dt_ref[...]
        ar, ai, bbr, bbi = _s5_param_fn(lre, lim, ldt, br_ref[...], bi_ref[...])
        ar_ref[...], ai_ref[...], bbr_ref[...], bbi_ref[...] = ar, ai, bbr, bbi
        kk = (lax.broadcasted_iota(jnp.int32, (1, SUBLANE, 1), 1) + 1).astype(F32)
        dt = jnp.exp(ldt)
        mag = jnp.exp(kk * (jnp.minimum(lre, S5_EIG_CLIP) * dt))
        ang = kk * (lim * dt)
        pr_ref[...] = mag * jnp.cos(ang)
        pi_ref[...] = mag * jnp.sin(ang)
        r = lax.broadcasted_iota(jnp.int32, (1, SUBLANE, 1), 1)
        k2 = jnp.where(r < 2, 1.0, jnp.where(r < 4, 2.0, jnp.where(r < 6, 4.0, 0.0)))
        mag2 = jnp.exp(k2 * (jnp.minimum(lre, S5_EIG_CLIP) * dt))
        ang2 = k2 * (lim * dt)
        a124_ref[...] = mag2 * jnp.where(r % 2 == 0, jnp.cos(ang2), jnp.sin(ang2))

    sds = jax.ShapeDtypeStruct
    return pl.pallas_call(
        body, out_shape=[sds((G, 1, N), F32), sds((G, 1, N), F32), sds(brt.shape, F32), sds(brt.shape, F32),
                         sds((G, SUBLANE, N), F32), sds((G, SUBLANE, N), F32), sds((G, SUBLANE, N), F32)], name=name,
    )(lam_re, lam_im, log_dt, brt, bit)


def _s5_params_bwd(lam_re, lam_im, log_dt, brt, bit, dar, dai, dbbr, dbbi, name):
    def body(lr_ref, li_ref, dt_ref, br_ref, bi_ref, dar_ref, dai_ref, dbbr_ref, dbbi_ref, *outs):
        _, vjp = jax.vjp(_s5_param_fn, lr_ref[...], li_ref[...], dt_ref[...], br_ref[...], bi_ref[...])
        for o_ref, val in zip(outs, vjp((dar_ref[...], dai_ref[...], dbbr_ref[...], dbbi_ref[...]))):
            o_ref[...] = val

    ins = (lam_re, lam_im, log_dt, brt, bit)
    return pl.pallas_call(body, out_shape=[jax.ShapeDtypeStruct(a.shape, F32) for a in ins], name=name)(
        *ins, dar, dai, dbbr, dbbi)


def _s5_scan(br, bi, a124, pr, pi, *, reverse, name, W=256):
    L, n = br.shape
    W = _tile(n, W)
    nblk = L // SUBLANE

    def body(br_ref, bi_ref, a_ref, pr_ref, pi_ref, xr_ref, xi_ref):
        A = a_ref[...]
        PR, PI = pr_ref[...], pi_ref[...]
        row = lax.broadcasted_iota(jnp.int32, (SUBLANE, W), 0)
        last = 0 if reverse else SUBLANE - 1

        def step(i, carry):
            cr, ci = carry
            off = pl.multiple_of(((nblk - 1 - i) if reverse else i) * SUBLANE, SUBLANE)
            xr, xi = br_ref[pl.ds(off, SUBLANE), :], bi_ref[pl.ds(off, SUBLANE), :]
            for j, k in enumerate((1, 2, 4)):
                ar, ai = A[2 * j:2 * j + 1, :], A[2 * j + 1:2 * j + 2, :]
                keep = (row < SUBLANE - k) if reverse else (row >= k)
                shift = (SUBLANE - k) if reverse else k
                sr = jnp.where(keep, pltpu.roll(xr, shift, 0), 0.0)
                si = jnp.where(keep, pltpu.roll(xi, shift, 0), 0.0)
                xr, xi = xr + ar * sr - ai * si, xi + ar * si + ai * sr
            xr, xi = xr + PR * cr - PI * ci, xi + PR * ci + PI * cr
            xr_ref[pl.ds(off, SUBLANE), :] = xr
            xi_ref[pl.ds(off, SUBLANE), :] = xi
            return xr[last:last + 1, :], xi[last:last + 1, :]

        z = jnp.zeros((1, W), F32)
        lax.fori_loop(0, nblk, step, (z, z))

    col = pl.BlockSpec((L, W), lambda j: (0, j))
    par = pl.BlockSpec((SUBLANE, W), lambda j: (0, j))
    return pl.pallas_call(
        body, grid=(n // W,), in_specs=[col, col, par, par, par], out_specs=[col, col],
        out_shape=[jax.ShapeDtypeStruct((L, n), F32)] * 2,
        compiler_params=_params(("parallel",), VMEM_LIMIT), name=name,
    )(br, bi, a124, pr, pi)


def _s5_dabar(lr, li, xr, xi, name, W=256):
    L, n = lr.shape
    W = _tile(n, W)

    def body(lr_ref, li_ref, xr_ref, xi_ref, dar_ref, dai_ref):
        first = lax.broadcasted_iota(jnp.int32, (L, W), 0) == 0
        pr = jnp.where(first, 0.0, pltpu.roll(xr_ref[...], 1, 0))
        pi = jnp.where(first, 0.0, pltpu.roll(xi_ref[...], 1, 0))
        a, b = lr_ref[...], li_ref[...]
        dar_ref[...] = jnp.sum(a * pr + b * pi, axis=0, keepdims=True)
        dai_ref[...] = jnp.sum(b * pr - a * pi, axis=0, keepdims=True)

    col = pl.BlockSpec((L, W), lambda j: (0, j))
    one = pl.BlockSpec((1, W), lambda j: (0, j))
    return pl.pallas_call(
        body, grid=(n // W,), in_specs=[col] * 4, out_specs=[one, one],
        out_shape=[jax.ShapeDtypeStruct((1, n), F32)] * 2,
        compiler_params=_params(("parallel",), VMEM_LIMIT), name=name,
    )(lr, li, xr, xi)


def _bd(w):
    G, A, B = w.shape
    w4 = w.reshape(G // S5_GB, S5_GB, A, B)
    eye = jnp.eye(S5_GB, dtype=w.dtype)
    return jnp.einsum("kgab,gh->kgahb", w4, eye).reshape(G // S5_GB, S5_GB * A, S5_GB * B).astype(BF16)


def _bd_extract(m, A, B):
    nb = m.shape[0]
    d = jnp.diagonal(m.reshape(nb, S5_GB, A, S5_GB, B), axis1=1, axis2=3)
    return jnp.moveaxis(d, 3, 1).reshape(nb * S5_GB, A, B)


def _place():
    return lax.axis_index("x"), lax.axis_index("y"), lax.axis_index("c")


def _slot(p):
    return 4 * p[0] + 2 * p[1] + p[2]


def _all_gather(xs, name):
    n = len(xs)

    def body(*refs):
        x_refs, o_refs = refs[:n], refs[n:2 * n]
        send_sems, recv_sems, local_sems = refs[2 * n:]
        x, y, c = _place()
        me, sib = (x, y, c), (x, y, 1 - c)
        chips = [(1 - x, y), (x, 1 - y), (1 - x, 1 - y)]

        def copy(a, k, block, to, src=None):
            dst = o_refs[a].at[_slot(block)]
            return pltpu.make_async_remote_copy(
                src_ref=dst if src is None else src, dst_ref=dst, send_sem=send_sems.at[a, k], recv_sem=recv_sems.at[a, k],
                device_id=to, device_id_type=MESH)

        mine = [pltpu.make_async_copy(x_refs[a], o_refs[a].at[_slot(me)], local_sems.at[a]) for a in range(n)]
        for cp in mine:
            cp.start()
        sent = []
        for a in range(n):
            sent.append(copy(a, 0, me, sib, src=x_refs[a]))
            sent += [copy(a, 1 + j, me, (*chip, c), src=x_refs[a]) for j, chip in enumerate(chips)]
        for cp in sent:
            cp.start()
        for j, chip in enumerate(chips):
            for a in range(n):
                copy(a, 1 + j, (*chip, c), me).wait_recv()
                fwd = copy(a, 4 + j, (*chip, c), sib)
                fwd.start()
                sent.append(fwd)
        for a in range(n):
            copy(a, 0, sib, me).wait_recv()
            for j, chip in enumerate(chips):
                copy(a, 4 + j, (*chip, 1 - c), me).wait_recv()
        for cp in sent:
            cp.wait_send()
        for cp in mine:
            cp.wait()

    any_spec = pl.BlockSpec(memory_space=pl.ANY)
    return pl.pallas_call(
        body, in_specs=[any_spec] * n, out_specs=[any_spec] * n,
        out_shape=[jax.ShapeDtypeStruct((N_DEV, *a.shape), a.dtype) for a in xs],
        scratch_shapes=[pltpu.SemaphoreType.DMA((n, 7)), pltpu.SemaphoreType.DMA((n, 7)), pltpu.SemaphoreType.DMA((n,))],
        name=name,
    )(*xs)


N_CHIP = N_DEV // 2
_HBM = pl.BlockSpec(memory_space=pltpu.HBM)
_SEM = pl.BlockSpec(memory_space=pltpu.SEMAPHORE)
_ANY = pl.BlockSpec(memory_space=pl.ANY)
_EFFECT = pltpu.SideEffectType.DATAFLOW_SIDE_EFFECTING


def _gather_routes(x, y, c):
    me = _slot((x, y, c))
    peers = [(x, y, 1 - c)] + [(px, py, c) for px, py in ((1 - x, y), (x, 1 - y), (1 - x, 1 - y))]
    return [(p, None, me, _slot(p)) for p in peers]


def _chip_routes(x, y, c):
    myq = 2 * x + y
    return [((px, py, c), 2 * px + py, myq, 2 * px + py) for px, py in ((1 - x, y), (x, 1 - y), (1 - x, 1 - y))]


def _pass_routes(x, y, c):
    chips = ((1 - x, y), (x, 1 - y), (1 - x, 1 - y))
    return [((x, y, 1 - c), _slot((px, py, c)), _slot((px, py, c)), _slot((px, py, 1 - c))) for px, py in chips]


def _split_copies(s_refs, l_refs, send_sems, recv_sems, routes, arrival):
    out = []
    rts = routes(*_place())
    for a in range(len(l_refs)):
        for k, (peer, src_slot, dst_slot, arr_slot) in enumerate(rts):
            s_ref = l_refs[a] if s_refs is None else s_refs[a]
            src = s_ref if src_slot is None else s_ref.at[src_slot]
            sem = a * len(rts) + k
            out.append(pltpu.make_async_remote_copy(
                src_ref=src, dst_ref=l_refs[a].at[arr_slot if arrival else dst_slot], send_sem=send_sems.at[sem],
                recv_sem=recv_sems.at[sem], device_id=peer, device_id_type=MESH))
    return out


def _split_start(srcs, lands, routes, n_routes, dep, name):
    n, ns = len(lands), 0 if srcs is None else len(srcs)
    bufs = [*(srcs or ()), *lands]

    def body(*refs):
        s_refs = None if srcs is None else refs[:ns]
        for cp in _split_copies(s_refs, refs[ns:ns + n], refs[ns + n + 1], refs[ns + n + 2], routes, False):
            cp.start()
        refs[-1][...] = jnp.zeros_like(refs[-1])

    sems = pltpu.SemaphoreType.DMA((n * n_routes,))
    res = pl.pallas_call(
        body, name=name,
        out_shape=(sems, sems, *[pltpu.HBM(a.shape, a.dtype) for a in bufs], jax.ShapeDtypeStruct((SUBLANE, LANE), F32)),
        in_specs=[_HBM] * (ns + n) + [_ANY], out_specs=(_SEM, _SEM, *[_HBM] * (ns + n), pl.BlockSpec(memory_space=pltpu.VMEM)),
        input_output_aliases={i: 2 + i for i in range(ns + n)},
        compiler_params=pltpu.CompilerParams(has_side_effects=_EFFECT),
    )(*[pltpu.with_memory_space_constraint(a, pltpu.HBM) for a in bufs], dep)
    return res[0], res[1], (None if srcs is None else list(res[2:2 + ns])), list(res[2 + ns:2 + ns + n]), res[-1]


def _split_wait(send_sems, recv_sems, srcs, lands, routes, after, name):
    n, ns = len(lands), 0 if srcs is None else len(srcs)
    bufs = [*(srcs or ()), *lands]

    def body(*refs):
        s_refs = None if srcs is None else refs[:ns]
        for cp in _split_copies(s_refs, refs[ns:ns + n], refs[ns + n], refs[ns + n + 1], routes, True):
            cp.wait_send()
            cp.wait_recv()

    res = pl.pallas_call(
        body, name=name, out_shape=[pltpu.HBM(a.shape, a.dtype) for a in bufs],
        in_specs=[_HBM] * (ns + n) + [_SEM, _SEM, _ANY], out_specs=[_HBM] * (ns + n),
        input_output_aliases={i: i for i in range(ns + n)},
        compiler_params=pltpu.CompilerParams(has_side_effects=_EFFECT),
    )(*bufs, send_sems, recv_sems, after)
    return (None if srcs is None else list(res[:ns])), list(res[ns:])


def _init_lands(xs, name):
    n = len(xs)

    def body(*refs):
        me = _slot(_place())
        cps = [pltpu.make_async_copy(refs[a], refs[n + a].at[me], refs[2 * n].at[a]) for a in range(n)]
        for cp in cps:
            cp.start()
        for cp in cps:
            cp.wait()

    return pl.pallas_call(
        body, in_specs=[_ANY] * n, out_specs=[_ANY] * n,
        out_shape=[jax.ShapeDtypeStruct((N_DEV, *a.shape), a.dtype) for a in xs],
        scratch_shapes=[pltpu.SemaphoreType.DMA((n,))], name=name,
    )(*xs)


def _pair_exchange(xs, name):
    n = len(xs)

    def body(*refs):
        x_refs, o_refs = refs[:n], refs[n:2 * n]
        send_sems, recv_sems = refs[2 * n:]
        x, y, c = _place()
        sib = (x, y, 1 - c)

        def copy(a, q):
            return pltpu.make_async_remote_copy(
                src_ref=x_refs[a].at[2 * q + (1 - c)], dst_ref=o_refs[a].at[q],
                send_sem=send_sems.at[a, q], recv_sem=recv_sems.at[a, q], device_id=sib, device_id_type=MESH)

        cps = [copy(a, q) for a in range(n) for q in range(N_CHIP)]
        for cp in cps:
            cp.start()
        for cp in cps:
            cp.wait_recv()
        for cp in cps:
            cp.wait_send()

    any_spec = pl.BlockSpec(memory_space=pl.ANY)
    return pl.pallas_call(
        body, in_specs=[any_spec] * n, out_specs=[any_spec] * n,
        out_shape=[jax.ShapeDtypeStruct((N_CHIP, *a.shape[1:]), a.dtype) for a in xs],
        scratch_shapes=[pltpu.SemaphoreType.DMA((n, N_CHIP)), pltpu.SemaphoreType.DMA((n, N_CHIP))],
        name=name,
    )(*xs)


def _pair_sum(x, got, place, name):
    _, R, C = x.shape
    unit = SUBLANE * (4 // x.dtype.itemsize)
    tr = _tile(R, max(unit, (512 * 1024 // C) // unit * unit), unit)

    def body(p_ref, x_ref, g_ref, o_ref, land_ref):
        s = (x_ref[...].astype(F32) + g_ref[...].astype(F32)).astype(o_ref.dtype)
        o_ref[...] = s

        @pl.when(pl.program_id(1) == p_ref[1])
        def _():
            land_ref[...] = s

    blk = lambda f: pl.BlockSpec((None, tr, C), f)
    return pl.pallas_call(
        body,
        grid_spec=pltpu.PrefetchScalarGridSpec(
            num_scalar_prefetch=1, grid=(R // tr, N_CHIP),
            in_specs=[blk(lambda i, q, p: (2 * q + p[0], i, 0)), blk(lambda i, q, p: (q, i, 0))],
            out_specs=[blk(lambda i, q, p: (q, i, 0)), blk(lambda i, q, p: (p[1], i, 0))]),
        out_shape=[jax.ShapeDtypeStruct(got.shape, x.dtype)] * 2,
        compiler_params=_params(("parallel", "arbitrary"), VMEM_LIMIT), name=name,
    )(place, x, got)


def _adamw_math(w, g, m, v):
    m = ADAM_B1 * m + (1.0 - ADAM_B1) * g
    v = ADAM_B2 * v + (1.0 - ADAM_B2) * (g * g)
    m_hat = m / (1.0 - ADAM_B1 ** ADAM_STEP)
    v_hat = v / (1.0 - ADAM_B2 ** ADAM_STEP)
    return -ADAM_LR * (m_hat / (jnp.sqrt(v_hat) + ADAM_EPS) + ADAM_WD * w), m, v


def _adamw(w, m, v, parts, layer, prev, name):
    nl, R, C = w.shape
    P = parts.shape[0]
    unit = SUBLANE * (4 // parts.dtype.itemsize)
    tr = _tile(R, max(unit, (128 * 1024 // C) // unit * unit), unit)

    def body(w_ref, m_ref, v_ref, p_ref, *rest):
        g_ref, d_ref, nm_ref, nv_ref = rest[-4:]
        g = p_ref[0].astype(F32)
        for p in range(1, P):
            g = g + p_ref[p].astype(F32)
        d, nm, nv = _adamw_math(w_ref[...], g, m_ref[...], v_ref[...])
        g_ref[...], d_ref[...], nm_ref[...], nv_ref[...] = g, d, nm, nv

    lay = pl.BlockSpec((None, tr, C), lambda i: (layer, i, 0))
    in_specs = [lay, lay, lay, pl.BlockSpec((P, tr, C), lambda i: (0, i, 0))]
    ins = [w, m, v, parts]
    aliases = {}
    if prev is not None:
        in_specs += [pl.BlockSpec(memory_space=pl.ANY)] * 4
        ins += list(prev)
        aliases = {4 + k: k for k in range(4)}
    return pl.pallas_call(
        body, grid=(R // tr,), in_specs=in_specs, out_specs=[lay] * 4,
        out_shape=[jax.ShapeDtypeStruct(w.shape, F32)] * 4, input_output_aliases=aliases,
        compiler_params=_params(("parallel",), VMEM_LIMIT), name=name,
    )(*ins)


def _gelu_tanh(x):
    return jax.nn.gelu(x, approximate=True)


def _pack(arrs):
    tile = SUBLANE * LANE
    out = []
    for a in arrs:
        f = a.reshape(-1).astype(F32)
        out.append(jnp.pad(f, (0, (-f.shape[0]) % tile)))
    return jnp.concatenate(out)


def _unpack(flat, shapes):
    tile = SUBLANE * LANE
    out, off = [], 0
    for s in shapes:
        n = math.prod(s)
        out.append(flat[off:off + n].reshape(s))
        off += n + (-n) % tile
    return out


def kernel(x, gla_norm, gla_w_in, gla_w_gate_up, gla_b_gate, gla_o_norm, gla_w_out, s5_norm, s5_w_in, s5_lam_re, s5_lam_im, s5_log_dt, s5_b_re, s5_b_im, s5_c_re, s5_c_im, s5_d, s5_w_out, mlp_norm, mlp_w_up, mlp_w_down, final_norm, loss_target, m_gla_norm, m_gla_w_in, m_gla_w_gate_up, m_gla_b_gate, m_gla_o_norm, m_gla_w_out, m_s5_norm, m_s5_w_in, m_s5_lam_re, m_s5_lam_im, m_s5_log_dt, m_s5_b_re, m_s5_b_im, m_s5_c_re, m_s5_c_im, m_s5_d, m_s5_w_out, m_mlp_norm, m_mlp_w_up, m_mlp_w_down, m_final_norm, v_gla_norm, v_gla_w_in, v_gla_w_gate_up, v_gla_b_gate, v_gla_o_norm, v_gla_w_out, v_s5_norm, v_s5_w_in, v_s5_lam_re, v_s5_lam_im, v_s5_log_dt, v_s5_b_re, v_s5_b_im, v_s5_c_re, v_s5_c_im, v_s5_d, v_s5_w_out, v_mlp_norm, v_mlp_w_up, v_mlp_w_down, v_final_norm):
    W = dict(gla_norm=gla_norm, gla_w_in=gla_w_in, gla_w_gate_up=gla_w_gate_up, gla_b_gate=gla_b_gate, gla_o_norm=gla_o_norm, gla_w_out=gla_w_out, s5_norm=s5_norm, s5_w_in=s5_w_in, s5_lam_re=s5_lam_re, s5_lam_im=s5_lam_im, s5_log_dt=s5_log_dt, s5_b_re=s5_b_re, s5_b_im=s5_b_im, s5_c_re=s5_c_re, s5_c_im=s5_c_im, s5_d=s5_d, s5_w_out=s5_w_out, mlp_norm=mlp_norm, mlp_w_up=mlp_w_up, mlp_w_down=mlp_w_down, final_norm=final_norm)
    M = dict(gla_norm=m_gla_norm, gla_w_in=m_gla_w_in, gla_w_gate_up=m_gla_w_gate_up, gla_b_gate=m_gla_b_gate, gla_o_norm=m_gla_o_norm, gla_w_out=m_gla_w_out, s5_norm=m_s5_norm, s5_w_in=m_s5_w_in, s5_lam_re=m_s5_lam_re, s5_lam_im=m_s5_lam_im, s5_log_dt=m_s5_log_dt, s5_b_re=m_s5_b_re, s5_b_im=m_s5_b_im, s5_c_re=m_s5_c_re, s5_c_im=m_s5_c_im, s5_d=m_s5_d, s5_w_out=m_s5_w_out, mlp_norm=m_mlp_norm, mlp_w_up=m_mlp_w_up, mlp_w_down=m_mlp_w_down, final_norm=m_final_norm)
    V = dict(gla_norm=v_gla_norm, gla_w_in=v_gla_w_in, gla_w_gate_up=v_gla_w_gate_up, gla_b_gate=v_gla_b_gate, gla_o_norm=v_gla_o_norm, gla_w_out=v_gla_w_out, s5_norm=v_s5_norm, s5_w_in=v_s5_w_in, s5_lam_re=v_s5_lam_re, s5_lam_im=v_s5_lam_im, s5_log_dt=v_s5_log_dt, s5_b_re=v_s5_b_re, s5_b_im=v_s5_b_im, s5_c_re=v_s5_c_re, s5_c_im=v_s5_c_im, s5_d=v_s5_d, s5_w_out=v_s5_w_out, mlp_norm=v_mlp_norm, mlp_w_up=v_mlp_w_up, mlp_w_down=v_mlp_w_down, final_norm=v_final_norm)
    names = list(W)
    big = ["gla_w_in", "gla_w_out", "s5_w_in", "s5_w_out", "mlp_w_up", "mlp_w_down"]
    small_sharded = {"gla_w_gate_up": 2, "s5_norm": 1, "s5_d": 1}
    small = [n for n in names if n not in big]

    _, L, D = x.shape
    n_gla, n_s5, depth = gla_norm.shape[0], s5_lam_re.shape[0], mlp_norm.shape[0]
    H = GLA_HEADS
    KW, VW = D // 2, D
    DK, DV = KW // H, VW // H
    IN = 2 * KW + 2 * VW + GLA_RANK
    INP = 2 * KW + 2 * VW + LANE
    SW = s5_lam_re.shape[1] * S5_GROUP
    G, N = s5_lam_re.shape[1], s5_lam_re.shape[2]
    nb = G // S5_GB
    dev = _slot(_place())

    wb = {n: W[n].astype(BF16) for n in big}
    sm_sh = _pack([W[n] for n in small_sharded])
    groups = []
    for i in range(depth):
        j = i // 2
        groups.append([("gla_w_in", j), ("gla_w_out", j)] if i % 2 == 0 else [("s5_w_in", j), ("s5_w_out", j)])
        groups.append([("mlp_w_up", i), ("mlp_w_down", i)])
    groups[0] = [("small", 0)] + groups[0]
    tok = jnp.zeros((SUBLANE, LANE), F32)
    shards = [[sm_sh if n == "small" else wb[n][j] for n, j in keys] for keys in groups]
    all_lands = iter(_init_lands([a for g in shards for a in g], "gather_init"))
    level1, level2, full = [], {}, {}
    for gi, srcs in enumerate(shards):
        lands = [next(all_lands) for _ in srcs]
        send_sems, recv_sems, srcs, lands, tok = _split_start(srcs, lands, _gather_routes, 4, tok, f"gather_start_{gi}")
        level1.append((send_sems, recv_sems, srcs, lands))

    def arrive(gi, after):
        send_sems, recv_sems, srcs, lands = level1[gi]
        _, lands = _split_wait(send_sems, recv_sems, srcs, lands, _gather_routes, after, f"gather_wait_{gi}")
        send_sems, recv_sems, _, lands, token = _split_start(None, lands, _pass_routes, 3, after, f"gather_pass_{gi}")
        level2[gi] = (send_sems, recv_sems, lands)
        return token

    def fetch(gi, after):
        send_sems, recv_sems, lands = level2[gi]
        _, lands = _split_wait(send_sems, recv_sems, None, lands, _pass_routes, after, f"gather_done_{gi}")
        full.update(zip(groups[gi], lands))

    tok = arrive(0, tok)
    fetch(0, tok)
    sm_all = full["small", 0]
    sm_parts = [_unpack(sm_all[d], [W[n].shape for n in small_sharded]) for d in range(N_DEV)]
    wgu_full = jnp.concatenate([p[0] for p in sm_parts], axis=2)
    s5n_full = jnp.concatenate([p[1] for p in sm_parts], axis=1)
    s5d_full = jnp.concatenate([p[2] for p in sm_parts], axis=1)

    def gla_weights(j):
        w_in = full["gla_w_in", j]
        w_in = jnp.transpose(w_in, (1, 0, 2)).reshape(D, IN)
        w_in = jnp.pad(w_in, ((0, 0), (0, INP - IN)))
        w_out = full["gla_w_out", j].reshape(VW, D)
        wgu = jnp.pad(wgu_full[j], ((0, LANE - GLA_RANK), (0, 0))).astype(BF16)
        return w_in, w_out, wgu

    grads = {}
    h = x[0]
    saved = []

    for i in range(depth):
        j = i // 2
        if i > 0:
            fetch(2 * i, h)
        ahead = (arrive(2 * i + 1, h),)
        if i % 2 == 0:
            w_in, w_out, wgu = gla_weights(j)
            gn = gla_norm[j][None]
            hn = _rms_fwd(h, gn, f"gla{j}_norm", ahead)
            proj = _mm(hn, w_in, "nn", name=f"gla{j}_proj", tn=896)
            bg, on = gla_b_gate[j][None], gla_o_norm[j][None]
            y, st = _gla_fwd(proj, wgu, bg, on, H=H, DK=DK, DV=DV, name=f"gla{j}_mix")
            h_new = _mm(y, w_out, "nn", name=f"gla{j}_out", res=h)
            saved.append(("gla", dict(h=h, hn=hn, proj=proj, y=y, st=st, w_in=w_in, w_out=w_out, wgu=wgu, gn=gn, bg=bg, on=on)))
        else:
            w_in = full["s5_w_in", j].reshape(D, SW)
            w_out3 = full["s5_w_out", j]
            sn = s5n_full[j][None]
            hn = _rms_fwd(h, sn, f"s5{j}_norm", ahead)
            u = _mm(hn, w_in, "nn", name=f"s5{j}_in")
            lre, lim = s5_lam_re[j][:, None, :], s5_lam_im[j][:, None, :]
            ldt = s5_log_dt[j][:, None, None]
            brt, bit = jnp.swapaxes(s5_b_re[j], 1, 2), jnp.swapaxes(s5_b_im[j], 1, 2)
            ar, ai, bbr, bbi, pr, pi, a124 = _s5_params(lre, lim, ldt, brt, bit, f"s5{j}_params")
            flat = lambda t: jnp.swapaxes(t, 0, 1).reshape(t.shape[1], G * N)
            pr, pi, a124 = flat(pr), flat(pi), flat(a124)
            bur = _mm_bd(u, _bd(bbr), name=f"s5{j}_bu_re")
            bui = _mm_bd(u, _bd(bbi), name=f"s5{j}_bu_im")
            xr, xi = _s5_scan(bur, bui, a124, pr, pi, reverse=False, name=f"s5{j}_scan")
            crt, cit = jnp.swapaxes(s5_c_re[j], 1, 2), jnp.swapaxes(s5_c_im[j], 1, 2)
            cx = _mm_bd(xr, _bd(crt), name=f"s5{j}_cx_re")
            cx = _mm_bd(xi, _bd(-cit), name=f"s5{j}_cx_im", res=cx)
            dsk = s5d_full[j][None]

            def act(cxv, uv, dv):
                ypre = cxv + dv * uv
                return ypre, _gelu_tanh(ypre)
            ypre, yg = _rowmap(act, [cx, u], [dsk], [(SW, F32), (SW, BF16)], name=f"s5{j}_act")
            z = _mm_nn_cb(yg, w_out3, name=f"s5{j}_out")

            def glu(zv, hv):
                return (hv + zv[:, :D] * jax.nn.sigmoid(zv[:, D:]),)
            h_new = _rowmap(glu, [z, h], [], [(D, F32)], name=f"s5{j}_glu")[0]
            saved.append(("s5", dict(h=h, hn=hn, u=u, xr=xr, xi=xi, ypre=ypre, yg=yg, z=z, w_in=w_in, w_out3=w_out3, sn=sn,
                                     dsk=dsk, prm=(lre, lim, ldt, brt, bit), a124=a124, pr=pr, pi=pi, bbr=bbr, bbi=bbi)))
        h = h_new
        fetch(2 * i + 1, h)
        ahead = (arrive(2 * i + 2, h),) if i + 1 < depth else ()
        w_up3 = full["mlp_w_up", i]
        w_down = full["mlp_w_down", i].reshape(4 * D, D)
        mn = mlp_norm[i][None]
        hn = _rms_fwd(h, mn, f"mlp{i}_norm", ahead)
        def sq_relu(zv):
            a = jnp.maximum(zv, 0.0)
            return zv, a * a
        z, s = _mm_nn_cb(hn, w_up3, name=f"mlp{i}_up", out_dtype=[F32, BF16], epi=sq_relu)
        h_new = _mm(s, w_down, "nn", name=f"mlp{i}_down", res=h)
        saved.append(("mlp", dict(h=h, hn=hn, z=z, s=s, w_up3=w_up3, w_down=w_down, mn=mn)))
        h = h_new

    dh, sq, dfin = _loss_head(h, loss_target[0], final_norm[None], "loss_head")
    loss = lax.psum(0.5 * jnp.sum(sq) / D, ("x", "y", "c"))
    small_grads = {"final_norm": dfin[0]}
    big_parts = {n: {} for n in big}
    stacks = {n: {} for n in small if n != "final_norm"}
    place = jnp.stack([lax.axis_index("c"), 2 * lax.axis_index("x") + lax.axis_index("y")]).astype(jnp.int32)
    flat3 = lambda a: a.reshape(a.shape[0], math.prod(a.shape[1:-1]), a.shape[-1])
    exchanges = []

    def send_grads(keys):
        tag = "_".join(f"{n}{j}" for n, j in keys)
        parts8 = [flat3(big_parts[n][j]) for n, j in keys]
        from_sib = _pair_exchange(parts8, name=f"grads_pair_{tag}")
        sums, lands = zip(*[_pair_sum(p, g, place, f"grads_pair_sum_{n}{j}") for (n, j), p, g in zip(keys, parts8, from_sib)])
        send_sems, recv_sems, srcs, lands, token = _split_start(list(sums), list(lands), _chip_routes, 3, place, f"grads_start_{tag}")
        exchanges.append((keys, tag, send_sems, recv_sems, srcs, lands))
        return token

    for idx in range(len(saved) - 1, -1, -1):
        kind, s = saved[idx]
        li = sum(1 for k, _ in saved[:idx] if k == kind)
        if kind == "mlp":
            def sq_relu_bwd(dsv, zv):
                return (dsv * 2.0 * jnp.maximum(zv, 0.0),)
            dz = _mm(dh, s["w_down"], "nt", name=f"mlp{li}_dz", out_dtype=BF16, epi=sq_relu_bwd, epi_ins=(s["z"],))
            big_parts["mlp_w_down"][li] = _mm(s["s"], dh, "tn", name=f"mlp{li}_dw_down", out_dtype=BF16).reshape(N_DEV, 4 * D // N_DEV, D)
            big_parts["mlp_w_up"][li] = _mm_tn_cbout(s["hn"], dz, N_DEV, name=f"mlp{li}_dw_up")
            token = send_grads([("mlp_w_down", li), ("mlp_w_up", li)])
            dhn = _mm_nt_cb(dz, s["w_up3"], name=f"mlp{li}_dhn")
            dh, dg = _rms_bwd(s["h"], dhn, dh, s["mn"], f"mlp{li}_dnorm", deps=(token,))
            stacks["mlp_norm"][li] = dg[0]
        elif kind == "gla":
            dyv = _mm(dh, s["w_out"], "nt", name=f"gla{li}_dy")
            big_parts["gla_w_out"][li] = _mm(s["y"], dh, "tn", name=f"gla{li}_dw_out", out_dtype=BF16).reshape(N_DEV, VW // N_DEV, D)
            dqkvr, dpre, don = _gla_bwd(s["proj"], s["wgu"], s["bg"], s["on"], s["st"], dyv, H=H, DK=DK, DV=DV, name=f"gla{li}_dmix")
            dglow = _mm(dpre, s["wgu"], "nt", name=f"gla{li}_dglow", out_dtype=BF16)
            glow = s["proj"][:, 2 * KW + 2 * VW:]
            dwgu = _mm(glow, dpre, "tn", name=f"gla{li}_dwgu")[:GLA_RANK]
            dbg = _rowmap(lambda t: (jnp.sum(t, axis=0, keepdims=True),), [dpre], [], [], [KW], name=f"gla{li}_dbg")[0]
            dproj = jnp.concatenate([dqkvr, dglow], axis=1)
            dw_in = _mm(s["hn"], dproj, "tn", name=f"gla{li}_dw_in", out_dtype=BF16, tn=896)[:, :IN]
            big_parts["gla_w_in"][li] = jnp.transpose(dw_in.reshape(D, N_DEV, IN // N_DEV), (1, 0, 2))
            token = send_grads([("gla_w_out", li), ("gla_w_in", li)])
            dhn = _mm(dproj, s["w_in"], "nt", name=f"gla{li}_dhn", tk=896)
            dh, dg = _rms_bwd(s["h"], dhn, dh, s["gn"], f"gla{li}_dnorm", deps=(token,))
            stacks["gla_norm"][li], stacks["gla_b_gate"][li], stacks["gla_o_norm"][li] = dg[0], dbg[0], don[0]
            stacks["gla_w_gate_up"][li] = dwgu
        else:
            def glu_bwd(zv, dhv):
                a, sg = zv[:, :D], jax.nn.sigmoid(zv[:, D:])
                return (jnp.concatenate([dhv * sg, dhv * a * sg * (1.0 - sg)], axis=1),)
            dz = _rowmap(glu_bwd, [s["z"], dh], [], [(2 * D, BF16)], name=f"s5{li}_dglu")[0]
            big_parts["s5_w_out"][li] = _mm_tn_cbout(s["yg"], dz, N_DEV, name=f"s5{li}_dw_out")
            dyg = _mm_nt_cb(dz, s["w_out3"], name=f"s5{li}_dyg")

            def act_bwd(dygv, ypv, uv, dv):
                _, vjp = jax.vjp(_gelu_tanh, ypv)
                dyp = vjp(dygv)[0]
                return dyp, dyp * dv, jnp.sum(dyp * uv, axis=0, keepdims=True)
            dyp, du, dd = _rowmap(act_bwd, [dyg, s["ypre"], s["u"]], [s["dsk"]], [(SW, F32), (SW, F32)], [SW], name=f"s5{li}_dact")
            cre, cim = s5_c_re[li], s5_c_im[li]
            wr = _mm_bd(dyp, _bd(cre), name=f"s5{li}_w_re")
            wi = _mm_bd(dyp, _bd(-cim), name=f"s5{li}_w_im")
            a124c = s["a124"] * jnp.array([1, -1, 1, -1, 1, -1, 1, 1], F32)[:, None]
            lr_, li_ = _s5_scan(wr, wi, a124c, s["pr"][::-1], -s["pi"][::-1], reverse=True, name=f"s5{li}_dscan")
            du = _mm_bd(lr_, _bd(jnp.swapaxes(s["bbr"], 1, 2)), name=f"s5{li}_du_re", res=du)
            du = _mm_bd(li_, _bd(jnp.swapaxes(s["bbi"], 1, 2)), name=f"s5{li}_du_im", res=du)
            dbbr = _bd_extract(_mm_tn_bd(s["u"], lr_, nb, name=f"s5{li}_dbb_re"), S5_GROUP, N)
            dbbi = _bd_extract(_mm_tn_bd(s["u"], li_, nb, name=f"s5{li}_dbb_im"), S5_GROUP, N)
            dcr = _bd_extract(_mm_tn_bd(dyp, s["xr"], nb, name=f"s5{li}_dc_re"), S5_GROUP, N)
            dci = -_bd_extract(_mm_tn_bd(dyp, s["xi"], nb, name=f"s5{li}_dc_im"), S5_GROUP, N)
            dar, dai = _s5_dabar(lr_, li_, s["xr"], s["xi"], f"s5{li}_dabar")
            dlre, dlim, dldt, dbrt, dbit = _s5_params_bwd(*s["prm"], dar.reshape(G, 1, N), dai.reshape(G, 1, N), dbbr, dbbi,
                                                         f"s5{li}_dparams")
            big_parts["s5_w_in"][li] = _mm(s["hn"], du, "tn", name=f"s5{li}_dw_in", out_dtype=BF16).reshape(N_DEV, D // N_DEV, SW)
            token = send_grads([("s5_w_out", li), ("s5_w_in", li)])
            dhn = _mm(du, s["w_in"], "nt", name=f"s5{li}_dhn")
            dh, dg = _rms_bwd(s["h"], dhn, dh, s["sn"], f"s5{li}_dnorm", deps=(token,))
            stacks["s5_norm"][li], stacks["s5_d"][li] = dg[0], dd[0]
            stacks["s5_lam_re"][li], stacks["s5_lam_im"][li], stacks["s5_log_dt"][li] = dlre[:, 0], dlim[:, 0], dldt[:, 0, 0]
            stacks["s5_b_re"][li], stacks["s5_b_im"][li] = jnp.swapaxes(dbrt, 1, 2), jnp.swapaxes(dbit, 1, 2)
            stacks["s5_c_re"][li], stacks["s5_c_im"][li] = dcr, dci
    grad_x = dh[None]

    for n, st_ in stacks.items():
        small_grads[n] = jnp.stack([st_[k] for k in range(len(st_))])
    full_shapes = [small_grads[n].shape for n in small]
    part = _pack([small_grads[n] for n in small])
    parts_all = _all_gather([part], name="gather_small_grads")[0]

    def mine(n, a):
        if n not in small_sharded:
            return a
        ax = small_sharded[n]
        size = W[n].shape[ax]
        return lax.dynamic_slice_in_dim(a, dev * size, size, axis=ax)

    rows = parts_all.shape[1] // LANE
    def full_layout(T):
        arrs = []
        for n, shp in zip(small, full_shapes):
            if n in small_sharded:
                ax = small_sharded[n]
                arrs.append(lax.dynamic_update_slice_in_dim(jnp.zeros(shp, F32), T[n], dev * T[n].shape[ax], axis=ax))
            else:
                arrs.append(T[n])
        return _pack(arrs)
    ws, ms, vs = full_layout(W), full_layout(M), full_layout(V)
    sg, sd, snm, snv = _adamw(ws.reshape(1, rows, LANE), ms.reshape(1, rows, LANE), vs.reshape(1, rows, LANE),
                              parts_all.reshape(N_DEV, rows, LANE), 0, None, "adamw_small")
    outs = {}
    for key, flat in zip(("grad", "delta", "new_m", "new_v"), (sg, sd, snm, snv)):
        for n, a in zip(small, _unpack(flat.reshape(-1), full_shapes)):
            outs[key, n] = mine(n, a)

    stacked = {n: None for n in big}
    as3 = lambda a: a.reshape(a.shape[0], math.prod(a.shape[1:-1]), a.shape[-1])
    for keys, tag, send_sems, recv_sems, srcs, lands in exchanges:
        _, lands = _split_wait(send_sems, recv_sems, srcs, lands, _chip_routes, dh, f"grads_wait_{tag}")
        for (n, j), parts in zip(keys, lands):
            stacked[n] = _adamw(as3(W[n]), as3(M[n]), as3(V[n]), parts, j, stacked[n], f"adamw_{n}_{j}")
    for n in big:
        for key, a in zip(("grad", "delta", "new_m", "new_v"), stacked[n]):
            outs[key, n] = a.reshape(W[n].shape)

    return (loss, grad_x, *[outs["grad", n] for n in names], *[outs["delta", n] for n in names],
            *[outs["new_m", n] for n in names], *[outs["new_v", n] for n in names])
```

```python
import functools
import math

import jax
import jax.numpy as jnp
from jax import lax
from jax.experimental import pallas as pl
from jax.experimental.pallas import tpu as pltpu

F32, BF16 = jnp.float32, jnp.bfloat16
MESH = pl.DeviceIdType.MESH
N_DEV = 8
LANE = 128
SUBLANE = 8
VMEM_LIMIT = 48 * 1024 * 1024

EPS = 1e-6
CHUNK = 64
GLA_HEADS = 4
GLA_RANK = 16
GLA_TEMP = 16.0
S5_GROUP = 16
S5_STATE = 64
S5_EIG_CLIP = -1e-4
S5_GB = 16
ADAM_LR, ADAM_B1, ADAM_B2, ADAM_EPS, ADAM_WD, ADAM_STEP = 0.001, 0.9, 0.999, 1e-08, 0.01, 10

_CONTRACT = {"nn": ((1,), (0,)), "tn": ((0,), (0,)), "nt": ((1,), (1,))}


def _tile(n, pref, unit=LANE):
    if n <= pref:
        return n
    t = (pref // unit) * unit
    while t > unit and n % t:
        t -= unit
    assert n % t == 0, (n, pref, unit)
    return t


def _params(sem, vmem=None):
    return pltpu.CompilerParams(dimension_semantics=sem, vmem_limit_bytes=vmem)


def _dot(a, b, dims, precision=None):
    return lax.dot_general(a, b, (_CONTRACT[dims], ((), ())), preferred_element_type=F32, precision=precision)


def _bdot(a, b, dims):
    return _dot(a.astype(BF16), b.astype(BF16), dims)


def _mm_call(a, b, *, dims, grid, a_spec, b_spec, o_spec, out_shape, out_dtype, name, res=None, epi=None, epi_ins=()):
    nk = grid[2]
    acc_shape = tuple(d for d in o_spec.block_shape if d is not None)
    if res is not None:
        epi, epi_ins = (lambda r, x: (r + x,)), (res,)
    single = not isinstance(out_dtype, (list, tuple))
    out_dtypes = [out_dtype] if single else list(out_dtype)
    n_e, n_o = len(epi_ins), len(out_dtypes)

    def body(*refs):
        a_ref, b_ref = refs[:2]
        e_refs, o_refs = refs[2:2 + n_e], refs[2 + n_e:2 + n_e + n_o]

        def finish(r):
            vals = (r,) if epi is None else epi(r, *[e[...].astype(F32) for e in e_refs])
            for o_ref, v in zip(o_refs, vals):
                o_ref[...] = v.astype(o_ref.dtype)

        d = _bdot(a_ref[...], b_ref[...], dims)
        if nk == 1:
            finish(d)
            return
        acc = refs[-1]
        k = pl.program_id(2)

        @pl.when(k == 0)
        def _():
            acc[...] = d

        @pl.when((k > 0) & (k < nk - 1))
        def _():
            acc[...] += d

        @pl.when(k == nk - 1)
        def _():
            finish(acc[...] + d)

    outs = pl.pallas_call(
        body, grid=grid, in_specs=[a_spec, b_spec] + [o_spec] * n_e, out_specs=[o_spec] * n_o,
        out_shape=[jax.ShapeDtypeStruct(out_shape, dt) for dt in out_dtypes],
        scratch_shapes=[] if nk == 1 else [pltpu.VMEM(acc_shape, F32)],
        compiler_params=_params(("parallel", "parallel", "arbitrary"), VMEM_LIMIT), name=name,
    )(a, b, *epi_ins)
    return outs[0] if single else outs


TM, TN, TK = 1024, 1024, 2048


def _mm(a, b, dims, *, name, out_dtype=F32, res=None, epi=None, epi_ins=(), tm=TM, tn=TN, tk=TK):
    if dims == "tn":
        (K, M), (_, N) = a.shape, b.shape
    elif dims == "nn":
        (M, K), (_, N) = a.shape, b.shape
    else:
        (M, K), (N, _) = a.shape, b.shape
    tm, tn, tk = _tile(M, tm), _tile(N, tn), _tile(K, tk)
    a_spec = pl.BlockSpec((tk, tm), lambda i, j, k: (k, i)) if dims == "tn" else pl.BlockSpec((tm, tk), lambda i, j, k: (i, k))
    b_spec = pl.BlockSpec((tn, tk), lambda i, j, k: (j, k)) if dims == "nt" else pl.BlockSpec((tk, tn), lambda i, j, k: (k, j))
    o_spec = pl.BlockSpec((tm, tn), lambda i, j, k: (i, j))
    return _mm_call(a, b, dims=dims, grid=(M // tm, N // tn, K // tk), a_spec=a_spec, b_spec=b_spec, o_spec=o_spec,
                    out_shape=(M, N), out_dtype=out_dtype, name=name, res=res, epi=epi, epi_ins=epi_ins)


def _mm_nn_cb(a, b3, *, name, out_dtype=F32, epi=None, tm=TM, tn=TN, tk=TK):
    (M, K), (P, _, Ns) = a.shape, b3.shape
    tm, tn, tk = _tile(M, tm), _tile(Ns, tn), _tile(K, tk)
    npb = Ns // tn
    return _mm_call(a, b3, dims="nn", grid=(M // tm, P * npb, K // tk),
                    a_spec=pl.BlockSpec((tm, tk), lambda i, j, k: (i, k)),
                    b_spec=pl.BlockSpec((None, tk, tn), lambda i, j, k: (j // npb, k, j % npb)),
                    o_spec=pl.BlockSpec((tm, tn), lambda i, j, k: (i, j)),
                    out_shape=(M, P * Ns), out_dtype=out_dtype, name=name, epi=epi)


def _mm_nt_cb(a, b3, *, name, out_dtype=F32, tm=TM, tn=TN, tk=TK):
    (M, _), (P, N, Ns) = a.shape, b3.shape
    tm, tn, tk = _tile(M, tm), _tile(N, tn), _tile(Ns, tk)
    kpb = Ns // tk
    return _mm_call(a, b3, dims="nt", grid=(M // tm, N // tn, P * kpb),
                    a_spec=pl.BlockSpec((tm, tk), lambda i, j, k: (i, k)),
                    b_spec=pl.BlockSpec((None, tn, tk), lambda i, j, k: (k // kpb, j, k % kpb)),
                    o_spec=pl.BlockSpec((tm, tn), lambda i, j, k: (i, j)),
                    out_shape=(M, N), out_dtype=out_dtype, name=name)


def _mm_tn_cbout(a, b, parts, *, name, out_dtype=BF16, tm=TM, tn=TN, tk=TK):
    (K, M), (_, N) = a.shape, b.shape
    Ns = N // parts
    tm, tn, tk = _tile(M, tm), _tile(Ns, tn), _tile(K, tk)
    npb = Ns // tn
    return _mm_call(a, b, dims="tn", grid=(M // tm, parts * npb, K // tk),
                    a_spec=pl.BlockSpec((tk, tm), lambda i, j, k: (k, i)),
                    b_spec=pl.BlockSpec((tk, tn), lambda i, j, k: (k, j)),
                    o_spec=pl.BlockSpec((None, tm, tn), lambda i, j, k: (j // npb, i, j % npb)),
                    out_shape=(parts, M, Ns), out_dtype=out_dtype, name=name)


def _mm_bd(a, w3, *, name, out_dtype=F32, res=None, tm=2048, tn=1024):
    (M, _), (nb, Kb, Nb) = a.shape, w3.shape
    tm, tn = _tile(M, tm), _tile(Nb, tn)
    npb = Nb // tn
    return _mm_call(a, w3, dims="nn", grid=(M // tm, nb * npb, 1),
                    a_spec=pl.BlockSpec((tm, Kb), lambda i, j, k: (i, j // npb)),
                    b_spec=pl.BlockSpec((None, Kb, tn), lambda i, j, k: (j // npb, 0, j % npb)),
                    o_spec=pl.BlockSpec((tm, tn), lambda i, j, k: (i, j)),
                    out_shape=(M, nb * Nb), out_dtype=out_dtype, name=name, res=res)


def _mm_tn_bd(a, b, nb, *, name, tk=1024):
    (K, MA), (_, NB) = a.shape, b.shape
    Ma, Nb = MA // nb, NB // nb
    tk = _tile(K, tk)
    return _mm_call(a, b, dims="tn", grid=(nb, 1, K // tk),
                    a_spec=pl.BlockSpec((tk, Ma), lambda i, j, k: (k, i)),
                    b_spec=pl.BlockSpec((tk, Nb), lambda i, j, k: (k, i)),
                    o_spec=pl.BlockSpec((None, Ma, Nb), lambda i, j, k: (i, 0, 0)),
                    out_shape=(nb, Ma, Nb), out_dtype=F32, name=name)


def _rowmap(fn, rows, consts, out_defs, red_defs=(), *, name, tr=256):
    L = rows[0].shape[0]
    widest = max([r.shape[1] for r in rows] + [n for n, _ in out_defs])
    tr = _tile(L, max(SUBLANE * 2, min(tr, 512 * 1024 // widest)), SUBLANE * 2)
    n_in, n_o, n_d = len(rows) + len(consts), len(out_defs), len(red_defs)

    def body(*refs):
        res = fn(*[r[...] for r in refs[:n_in]])
        res = res if isinstance(res, (tuple, list)) else (res,)
        outs = refs[n_in:]
        for o_ref, val in zip(outs[:n_o], res[:n_o]):
            o_ref[...] = val.astype(o_ref.dtype)
        if n_d:
            @pl.when(pl.program_id(0) == 0)
            def _():
                for o_ref in outs[n_o:]:
                    o_ref[...] = jnp.zeros_like(o_ref)
            for o_ref, val in zip(outs[n_o:], res[n_o:]):
                o_ref[...] += val

    in_specs = [pl.BlockSpec((tr, r.shape[1]), lambda i: (i, 0)) for r in rows]
    in_specs += [pl.BlockSpec(c.shape, lambda i, nd=c.ndim: (0,) * nd) for c in consts]
    out_specs = [pl.BlockSpec((tr, n), lambda i: (i, 0)) for n, _ in out_defs]
    out_specs += [pl.BlockSpec((1, n), lambda i: (0, 0)) for n in red_defs]
    out_shape = [jax.ShapeDtypeStruct((L, n), dt) for n, dt in out_defs]
    out_shape += [jax.ShapeDtypeStruct((1, n), F32) for n in red_defs]
    return pl.pallas_call(body, grid=(L // tr,), in_specs=in_specs, out_specs=out_specs, out_shape=out_shape,
                          compiler_params=_params(("arbitrary",), VMEM_LIMIT), name=name)(*rows, *consts)


def _rms_parts(x):
    r = lax.rsqrt(jnp.mean(x * x, axis=-1, keepdims=True) + EPS)
    return r, x * r


def _rms_fwd(h, g, name, deps=()):
    def fn(x, gg, *_):
        _, xh = _rms_parts(x)
        return (xh * gg,)
    return _rowmap(fn, [h], [g, *deps], [(h.shape[1], BF16)], name=name)[0]


def _rms_bwd(h, dhn, dh, g, name, deps=()):
    def fn(x, dy, dres, gg, *_):
        r, xh = _rms_parts(x)
        dxh = dy * gg
        dx = r * (dxh - xh * jnp.mean(dxh * xh, axis=-1, keepdims=True))
        return dres + dx, jnp.sum(dy * xh, axis=0, keepdims=True)
    D = h.shape[1]
    return _rowmap(fn, [h, dhn, dh], [g, *deps], [(D, F32)], [D], name=name)


def _loss_head(h, tgt, g, name):
    D = h.shape[1]

    def fn(x, t, gg):
        r, xh = _rms_parts(x)
        diff = xh * gg - t
        dy = diff * (1.0 / D)
        dxh = dy * gg
        dx = r * (dxh - xh * jnp.mean(dxh * xh, axis=-1, keepdims=True))
        return dx, jnp.sum(diff * diff, axis=0, keepdims=True), jnp.sum(dy * xh, axis=0, keepdims=True)
    return _rowmap(fn, [h, tgt], [g], [(D, F32)], [D, D], name=name)


def _tri(n, strict):
    r = lax.broadcasted_iota(jnp.int32, (n, n), 0)
    c = lax.broadcasted_iota(jnp.int32, (n, n), 1)
    return jnp.where((c < r) if strict else (c <= r), 1.0, 0.0).astype(F32)


def _gla_gate(g_ref, wgu_ref, bg_ref):
    pre = _bdot(g_ref[...], wgu_ref[...], "nn") + bg_ref[...]
    la = (jnp.minimum(pre, 0.0) - jnp.log(1.0 + jnp.exp(-jnp.abs(pre)))) * (1.0 / GLA_TEMP)
    cum = _dot(_tri(CHUNK, False), la, "nn", lax.Precision.HIGHEST)
    return pre, cum, cum[CHUNK - 1:CHUNK, :]


def _gla_specs(H, DK, DV, cmap):
    kb, vb, gb = H, (2 * H * DK) // DV, (2 * H * DK + 2 * H * DV) // LANE
    return [
        pl.BlockSpec((CHUNK, DK), lambda h, c: (cmap(h, c), h)),
        pl.BlockSpec((CHUNK, DK), lambda h, c: (cmap(h, c), kb + h)),
        pl.BlockSpec((CHUNK, DV), lambda h, c: (cmap(h, c), vb + h)),
        pl.BlockSpec((CHUNK, DV), lambda h, c: (cmap(h, c), vb + H + h)),
        pl.BlockSpec((CHUNK, LANE), lambda h, c: (cmap(h, c), gb)),
        pl.BlockSpec((LANE, DK), lambda h, c: (0, h)),
        pl.BlockSpec((1, DK), lambda h, c: (0, h)),
        pl.BlockSpec((1, DV), lambda h, c: (0, 0)),
    ]


def _gla_fwd(proj, wgu, bg, on, *, H, DK, DV, name):
    L = proj.shape[0]
    nc = L // CHUNK
    scale = DK ** -0.5

    def body(q_ref, k_ref, v_ref, r_ref, g_ref, wgu_ref, bg_ref, on_ref, y_ref, st_ref, S):
        @pl.when(pl.program_id(1) == 0)
        def _():
            S[...] = jnp.zeros_like(S)
        _, cum, total = _gla_gate(g_ref, wgu_ref, bg_ref)
        kd = k_ref[...] * jnp.exp(total - cum)
        St = S[...] * jnp.exp(total) + _bdot(v_ref[...], kd, "tn")
        S[...] = St
        st_ref[...] = St
        o = _bdot(q_ref[...] * scale, St, "nt")
        _, oh = _rms_parts(o)
        y_ref[...] = (oh * on_ref[...] * jax.nn.silu(r_ref[...])).astype(y_ref.dtype)

    return pl.pallas_call(
        body, grid=(H, nc), in_specs=_gla_specs(H, DK, DV, lambda h, c: c),
        out_specs=[pl.BlockSpec((CHUNK, DV), lambda h, c: (c, h)),
                   pl.BlockSpec((None, None, DV, DK), lambda h, c: (h, c, 0, 0))],
        out_shape=[jax.ShapeDtypeStruct((L, H * DV), BF16), jax.ShapeDtypeStruct((H, nc, DV, DK), F32)],
        scratch_shapes=[pltpu.VMEM((DV, DK), F32)],
        compiler_params=_params(("arbitrary", "arbitrary"), VMEM_LIMIT), name=name,
    )(proj, proj, proj, proj, proj, wgu, bg, on)


def _gla_bwd(proj, wgu, bg, on, st, dy, *, H, DK, DV, name):
    L = proj.shape[0]
    nc = L // CHUNK
    scale = DK ** -0.5
    rev = lambda h, c: nc - 1 - c

    def body(q_ref, k_ref, v_ref, r_ref, g_ref, wgu_ref, bg_ref, on_ref, sc_ref, sp_ref, dy_ref,
             dq_ref, dk_ref, dv_ref, dr_ref, dpre_ref, don_ref, G, decn):
        h, c = pl.program_id(0), pl.program_id(1)

        @pl.when(c == 0)
        def _():
            G[...] = jnp.zeros_like(G)
            decn[...] = jnp.zeros_like(decn)

        @pl.when((c == 0) & (h == 0))
        def _():
            don_ref[...] = jnp.zeros_like(don_ref)

        pre, cum, total = _gla_gate(g_ref, wgu_ref, bg_ref)
        ex = jnp.exp(total - cum)
        k, v, r = k_ref[...], v_ref[...], r_ref[...]
        kd = k * ex
        dec = jnp.exp(total)
        qs = q_ref[...] * scale
        Sc = sc_ref[...]
        o = _bdot(qs, Sc, "nt")
        rinv, oh = _rms_parts(o)
        gn = on_ref[...]
        sg = jax.nn.sigmoid(r)
        dyv = dy_ref[...]
        d_on = dyv * (r * sg)
        dr_ref[...] = (dyv * (oh * gn) * (sg * (1.0 + r * (1.0 - sg)))).astype(dr_ref.dtype)
        don_ref[...] += jnp.sum(d_on * oh, axis=0, keepdims=True)
        dxh = d_on * gn
        do = rinv * (dxh - oh * jnp.mean(dxh * oh, axis=-1, keepdims=True))
        dq_ref[...] = (_bdot(do, Sc, "nn") * scale).astype(dq_ref.dtype)
        Gt = G[...] * decn[...] + _bdot(do, qs, "tn")
        G[...] = Gt
        decn[...] = dec
        dkd = _bdot(v, Gt, "nn")
        dv_ref[...] = _bdot(kd, Gt, "nt").astype(dv_ref.dtype)
        Sp = sp_ref[...] * jnp.where(c == nc - 1, 0.0, 1.0)
        ddec = jnp.sum(Gt * Sp, axis=0, keepdims=True)
        dk_ref[...] = (dkd * ex).astype(dk_ref.dtype)
        dla = ddec * dec + _dot(_tri(CHUNK, True), dkd * kd, "nn", lax.Precision.HIGHEST)
        dpre_ref[...] = dla * (1.0 / GLA_TEMP) * jax.nn.sigmoid(-pre)

    in_specs = _gla_specs(H, DK, DV, rev) + [
        pl.BlockSpec((None, None, DV, DK), lambda h, c: (h, rev(h, c), 0, 0)),
        pl.BlockSpec((None, None, DV, DK), lambda h, c: (h, jnp.maximum(rev(h, c) - 1, 0), 0, 0)),
        pl.BlockSpec((CHUNK, DV), lambda h, c: (rev(h, c), h)),
    ]
    blk_k = pl.BlockSpec((CHUNK, DK), lambda h, c: (rev(h, c), h))
    blk_v = pl.BlockSpec((CHUNK, DV), lambda h, c: (rev(h, c), h))
    out_specs = [blk_k, blk_k, blk_v, blk_v, blk_k, pl.BlockSpec((1, DV), lambda h, c: (0, 0))]
    sds = jax.ShapeDtypeStruct
    out_shape = [sds((L, H * DK), BF16), sds((L, H * DK), BF16), sds((L, H * DV), BF16), sds((L, H * DV), BF16),
                 sds((L, H * DK), F32), sds((1, DV), F32)]
    dq, dk, dv, dr, dpre, don = pl.pallas_call(
        body, grid=(H, nc), in_specs=in_specs, out_specs=out_specs, out_shape=out_shape,
        scratch_shapes=[pltpu.VMEM((DV, DK), F32), pltpu.VMEM((1, DK), F32)],
        compiler_params=_params(("arbitrary", "arbitrary"), VMEM_LIMIT), name=name,
    )(proj, proj, proj, proj, proj, wgu, bg, on, st, st, dy)
    dqkvr = jnp.concatenate([dq, dk, dv, dr], axis=1)
    return dqkvr, dpre, don


def _s5_param_fn(lam_re, lam_im, log_dt, brt, bit):
    lr = jnp.minimum(lam_re, S5_EIG_CLIP)
    li = lam_im
    dt = jnp.exp(log_dt)
    mag = jnp.exp(lr * dt)
    ang = li * dt
    ab_re = mag * jnp.cos(ang)
    ab_im = mag * jnp.sin(ang)
    den = lr * lr + li * li
    nr = ab_re - 1.0
    f_re = (nr * lr + ab_im * li) / den
    f_im = (ab_im * lr - nr * li) / den
    return ab_re, ab_im, f_re * brt - f_im * bit, f_re * bit + f_im * brt


def _s5_params(lam_re, lam_im, log_dt, brt, bit, name):
    G, _, N = lam_re.shape

    def body(lr_ref, li_ref, dt_ref, br_ref, bi_ref, ar_ref, ai_ref, bbr_ref, bbi_ref, pr_ref, pi_ref, a124_ref):
        lre, lim, ldt = lr_ref[...], li_ref[...], dt_ref[...]
        ar, ai, bbr, bbi = _s5_param_fn(lre, lim, ldt, br_ref[...], bi_ref[...])
        ar_ref[...], ai_ref[...], bbr_ref[...], bbi_ref[...] = ar, ai, bbr, bbi
        kk = (lax.broadcasted_iota(jnp.int32, (1, SUBLANE, 1), 1) + 1).astype(F32)
        dt = jnp.exp(ldt)
        mag = jnp.exp(kk * (jnp.minimum(lre, S5_EIG_CLIP) * dt))
        ang = kk * (lim * dt)
        pr_ref[...] = mag * jnp.cos(ang)
        pi_ref[...] = mag * jnp.sin(ang)
        r = lax.broadcasted_iota(jnp.int32, (1, SUBLANE, 1), 1)
        k2 = jnp.where(r < 2, 1.0, jnp.where(r < 4, 2.0, jnp.where(r < 6, 4.0, 0.0)))
        mag2 = jnp.exp(k2 * (jnp.minimum(lre, S5_EIG_CLIP) * dt))
        ang2 = k2 * (lim * dt)
        a124_ref[...] = mag2 * jnp.where(r % 2 == 0, jnp.cos(ang2), jnp.sin(ang2))

    sds = jax.ShapeDtypeStruct
    return pl.pallas_call(
        body, out_shape=[sds((G, 1, N), F32), sds((G, 1, N), F32), sds(brt.shape, F32), sds(brt.shape, F32),
                         sds((G, SUBLANE, N), F32), sds((G, SUBLANE, N), F32), sds((G, SUBLANE, N), F32)], name=name,
    )(lam_re, lam_im, log_dt, brt, bit)


def _s5_params_bwd(lam_re, lam_im, log_dt, brt, bit, dar, dai, dbbr, dbbi, name):
    def body(lr_ref, li_ref, dt_ref, br_ref, bi_ref, dar_ref, dai_ref, dbbr_ref, dbbi_ref, *outs):
        _, vjp = jax.vjp(_s5_param_fn, lr_ref[...], li_ref[...], dt_ref[...], br_ref[...], bi_ref[...])
        for o_ref, val in zip(outs, vjp((dar_ref[...], dai_ref[...], dbbr_ref[...], dbbi_ref[...]))):
            o_ref[...] = val

    ins = (lam_re, lam_im, log_dt, brt, bit)
    return pl.pallas_call(body, out_shape=[jax.ShapeDtypeStruct(a.shape, F32) for a in ins], name=name)(
        *ins, dar, dai, dbbr, dbbi)


def _s5_scan(br, bi, a124, pr, pi, *, reverse, name, W=256):
    L, n = br.shape
    W = _tile(n, W)
    nblk = L // SUBLANE

    def body(br_ref, bi_ref, a_ref, pr_ref, pi_ref, xr_ref, xi_ref):
        A = a_ref[...]
        PR, PI = pr_ref[...], pi_ref[...]
        row = lax.broadcasted_iota(jnp.int32, (SUBLANE, W), 0)
        last = 0 if reverse else SUBLANE - 1

        def step(i, carry):
            cr, ci = carry
            off = pl.multiple_of(((nblk - 1 - i) if reverse else i) * SUBLANE, SUBLANE)
            xr, xi = br_ref[pl.ds(off, SUBLANE), :], bi_ref[pl.ds(off, SUBLANE), :]
            for j, k in enumerate((1, 2, 4)):
                ar, ai = A[2 * j:2 * j + 1, :], A[2 * j + 1:2 * j + 2, :]
                keep = (row < SUBLANE - k) if reverse else (row >= k)
                shift = (SUBLANE - k) if reverse else k
                sr = jnp.where(keep, pltpu.roll(xr, shift, 0), 0.0)
                si = jnp.where(keep, pltpu.roll(xi, shift, 0), 0.0)
                xr, xi = xr + ar * sr - ai * si, xi + ar * si + ai * sr
            xr, xi = xr + PR * cr - PI * ci, xi + PR * ci + PI * cr
            xr_ref[pl.ds(off, SUBLANE), :] = xr
            xi_ref[pl.ds(off, SUBLANE), :] = xi
            return xr[last:last + 1, :], xi[last:last + 1, :]

        z = jnp.zeros((1, W), F32)
        lax.fori_loop(0, nblk, step, (z, z))

    col = pl.BlockSpec((L, W), lambda j: (0, j))
    par = pl.BlockSpec((SUBLANE, W), lambda j: (0, j))
    return pl.pallas_call(
        body, grid=(n // W,), in_specs=[col, col, par, par, par], out_specs=[col, col],
        out_shape=[jax.ShapeDtypeStruct((L, n), F32)] * 2,
        compiler_params=_params(("parallel",), VMEM_LIMIT), name=name,
    )(br, bi, a124, pr, pi)


def _s5_dabar(lr, li, xr, xi, name, W=256):
    L, n = lr.shape
    W = _tile(n, W)

    def body(lr_ref, li_ref, xr_ref, xi_ref, dar_ref, dai_ref):
        first = lax.broadcasted_iota(jnp.int32, (L, W), 0) == 0
        pr = jnp.where(first, 0.0, pltpu.roll(xr_ref[...], 1, 0))
        pi = jnp.where(first, 0.0, pltpu.roll(xi_ref[...], 1, 0))
        a, b = lr_ref[...], li_ref[...]
        dar_ref[...] = jnp.sum(a * pr + b * pi, axis=0, keepdims=True)
        dai_ref[...] = jnp.sum(b * pr - a * pi, axis=0, keepdims=True)

    col = pl.BlockSpec((L, W), lambda j: (0, j))
    one = pl.BlockSpec((1, W), lambda j: (0, j))
    return pl.pallas_call(
        body, grid=(n // W,), in_specs=[col] * 4, out_specs=[one, one],
        out_shape=[jax.ShapeDtypeStruct((1, n), F32)] * 2,
        compiler_params=_params(("parallel",), VMEM_LIMIT), name=name,
    )(lr, li, xr, xi)


def _bd(w):
    G, A, B = w.shape
    w4 = w.reshape(G // S5_GB, S5_GB, A, B)
    eye = jnp.eye(S5_GB, dtype=w.dtype)
    return jnp.einsum("kgab,gh->kgahb", w4, eye).reshape(G // S5_GB, S5_GB * A, S5_GB * B).astype(BF16)


def _bd_extract(m, A, B):
    nb = m.shape[0]
    d = jnp.diagonal(m.reshape(nb, S5_GB, A, S5_GB, B), axis1=1, axis2=3)
    return jnp.moveaxis(d, 3, 1).reshape(nb * S5_GB, A, B)


def _place():
    return lax.axis_index("x"), lax.axis_index("y"), lax.axis_index("c")


def _slot(p):
    return 4 * p[0] + 2 * p[1] + p[2]


N_CHIP = N_DEV // 2
_HBM = pl.BlockSpec(memory_space=pltpu.HBM)
_SEM = pl.BlockSpec(memory_space=pltpu.SEMAPHORE)
_ANY = pl.BlockSpec(memory_space=pl.ANY)
_EFFECT = pltpu.SideEffectType.DATAFLOW_SIDE_EFFECTING


def _gather_routes(x, y, c):
    me = _slot((x, y, c))
    peers = [(x, y, 1 - c)] + [(px, py, c) for px, py in ((1 - x, y), (x, 1 - y), (1 - x, 1 - y))]
    return [(p, me, me, _slot(p)) for p in peers]


def _chip_routes(x, y, c):
    myq = 2 * x + y
    return [((px, py, c), 2 * px + py, myq, 2 * px + py) for px, py in ((1 - x, y), (x, 1 - y), (1 - x, 1 - y))]


def _pass_routes(x, y, c):
    chips = ((1 - x, y), (x, 1 - y), (1 - x, 1 - y))
    return [((x, y, 1 - c), _slot((px, py, c)), _slot((px, py, c)), _slot((px, py, 1 - c))) for px, py in chips]


def _split_copies(s_refs, l_refs, send_sems, recv_sems, routes, arrival):
    out = []
    rts = routes(*_place())
    for a in range(len(l_refs)):
        for k, (peer, src_slot, dst_slot, arr_slot) in enumerate(rts):
            s_ref = l_refs[a] if s_refs is None else s_refs[a]
            src = s_ref if src_slot is None else s_ref.at[src_slot]
            sem = a * len(rts) + k
            out.append(pltpu.make_async_remote_copy(
                src_ref=src, dst_ref=l_refs[a].at[arr_slot if arrival else dst_slot], send_sem=send_sems.at[sem],
                recv_sem=recv_sems.at[sem], device_id=peer, device_id_type=MESH))
    return out


def _split_start(srcs, lands, routes, n_routes, dep, name):
    n, ns = len(lands), 0 if srcs is None else len(srcs)
    bufs = [*(srcs or ()), *lands]

    def body(*refs):
        s_refs = None if srcs is None else refs[:ns]
        for cp in _split_copies(s_refs, refs[ns:ns + n], refs[ns + n + 1], refs[ns + n + 2], routes, False):
            cp.start()
        refs[-1][...] = jnp.zeros_like(refs[-1])

    sems = pltpu.SemaphoreType.DMA((n * n_routes,))
    res = pl.pallas_call(
        body, name=name,
        out_shape=(sems, sems, *[pltpu.HBM(a.shape, a.dtype) for a in bufs], jax.ShapeDtypeStruct((SUBLANE, LANE), F32)),
        in_specs=[_HBM] * (ns + n) + [_ANY], out_specs=(_SEM, _SEM, *[_HBM] * (ns + n), pl.BlockSpec(memory_space=pltpu.VMEM)),
        input_output_aliases={i: 2 + i for i in range(ns + n)},
        compiler_params=pltpu.CompilerParams(has_side_effects=_EFFECT),
    )(*[pltpu.with_memory_space_constraint(a, pltpu.HBM) for a in bufs], dep)
    return res[0], res[1], (None if srcs is None else list(res[2:2 + ns])), list(res[2 + ns:2 + ns + n]), res[-1]


def _split_wait(send_sems, recv_sems, srcs, lands, routes, after, name):
    n, ns = len(lands), 0 if srcs is None else len(srcs)
    bufs = [*(srcs or ()), *lands]

    def body(*refs):
        s_refs = None if srcs is None else refs[:ns]
        for cp in _split_copies(s_refs, refs[ns:ns + n], refs[ns + n], refs[ns + n + 1], routes, True):
            cp.wait_send()
            cp.wait_recv()

    res = pl.pallas_call(
        body, name=name, out_shape=[pltpu.HBM(a.shape, a.dtype) for a in bufs],
        in_specs=[_HBM] * (ns + n) + [_SEM, _SEM, _ANY], out_specs=[_HBM] * (ns + n),
        input_output_aliases={i: i for i in range(ns + n)},
        compiler_params=pltpu.CompilerParams(has_side_effects=_EFFECT),
    )(*bufs, send_sems, recv_sems, after)
    return (None if srcs is None else list(res[:ns])), list(res[ns:])


def _to_slot(w, layer, place, out_dtype, name):
    _, R, C = w.shape
    unit = SUBLANE * (4 // jnp.dtype(out_dtype).itemsize)
    tr = _tile(R, max(unit, (512 * 1024 // C) // unit * unit), unit)

    def body(p_ref, w_ref, o_ref):
        o_ref[...] = w_ref[...].astype(o_ref.dtype)

    return pl.pallas_call(
        body,
        grid_spec=pltpu.PrefetchScalarGridSpec(
            num_scalar_prefetch=1, grid=(R // tr,),
            in_specs=[pl.BlockSpec((None, tr, C), lambda i, p: (layer, i, 0))],
            out_specs=pl.BlockSpec((None, tr, C), lambda i, p: (2 * p[1] + p[0], i, 0))),
        out_shape=jax.ShapeDtypeStruct((N_DEV, R, C), out_dtype),
        compiler_params=_params(("parallel",), VMEM_LIMIT), name=name,
    )(place, w)


def _pair_exchange(xs, name):
    n = len(xs)

    def body(*refs):
        x_refs, o_refs = refs[:n], refs[n:2 * n]
        send_sems, recv_sems = refs[2 * n:]
        x, y, c = _place()
        sib = (x, y, 1 - c)

        def copy(a, q):
            return pltpu.make_async_remote_copy(
                src_ref=x_refs[a].at[2 * q + (1 - c)], dst_ref=o_refs[a].at[q],
                send_sem=send_sems.at[a, q], recv_sem=recv_sems.at[a, q], device_id=sib, device_id_type=MESH)

        cps = [copy(a, q) for a in range(n) for q in range(N_CHIP)]
        for cp in cps:
            cp.start()
        for cp in cps:
            cp.wait_recv()
        for cp in cps:
            cp.wait_send()

    any_spec = pl.BlockSpec(memory_space=pl.ANY)
    return pl.pallas_call(
        body, in_specs=[any_spec] * n, out_specs=[any_spec] * n,
        out_shape=[jax.ShapeDtypeStruct((N_CHIP, *a.shape[1:]), a.dtype) for a in xs],
        scratch_shapes=[pltpu.SemaphoreType.DMA((n, N_CHIP)), pltpu.SemaphoreType.DMA((n, N_CHIP))],
        name=name,
    )(*xs)


def _pair_sum(x, got, place, name):
    _, R, C = x.shape
    unit = SUBLANE * (4 // x.dtype.itemsize)
    tr = _tile(R, max(unit, (512 * 1024 // C) // unit * unit), unit)

    def body(p_ref, x_ref, g_ref, o_ref, land_ref):
        s = (x_ref[...].astype(F32) + g_ref[...].astype(F32)).astype(o_ref.dtype)
        o_ref[...] = s

        @pl.when(pl.program_id(1) == p_ref[1])
        def _():
            land_ref[...] = s

    blk = lambda f: pl.BlockSpec((None, tr, C), f)
    return pl.pallas_call(
        body,
        grid_spec=pltpu.PrefetchScalarGridSpec(
            num_scalar_prefetch=1, grid=(R // tr, N_CHIP),
            in_specs=[blk(lambda i, q, p: (2 * q + p[0], i, 0)), blk(lambda i, q, p: (q, i, 0))],
            out_specs=[blk(lambda i, q, p: (q, i, 0)), blk(lambda i, q, p: (p[1], i, 0))]),
        out_shape=[jax.ShapeDtypeStruct(got.shape, x.dtype)] * 2,
        compiler_params=_params(("parallel", "arbitrary"), VMEM_LIMIT), name=name,
    )(place, x, got)


def _adamw_math(w, g, m, v):
    m = ADAM_B1 * m + (1.0 - ADAM_B1) * g
    v = ADAM_B2 * v + (1.0 - ADAM_B2) * (g * g)
    m_hat = m / (1.0 - ADAM_B1 ** ADAM_STEP)
    v_hat = v / (1.0 - ADAM_B2 ** ADAM_STEP)
    return -ADAM_LR * (m_hat / (jnp.sqrt(v_hat) + ADAM_EPS) + ADAM_WD * w), m, v


def _adamw(w, m, v, parts, layer, prev, name):
    nl, R, C = w.shape
    P = parts.shape[0]
    unit = SUBLANE * (4 // parts.dtype.itemsize)
    tr = _tile(R, max(unit, (128 * 1024 // C) // unit * unit), unit)

    def body(w_ref, m_ref, v_ref, p_ref, *rest):
        g_ref, d_ref, nm_ref, nv_ref = rest[-4:]
        g = p_ref[0].astype(F32)
        for p in range(1, P):
            g = g + p_ref[p].astype(F32)
        d, nm, nv = _adamw_math(w_ref[...], g, m_ref[...], v_ref[...])
        g_ref[...], d_ref[...], nm_ref[...], nv_ref[...] = g, d, nm, nv

    lay = pl.BlockSpec((None, tr, C), lambda i: (layer, i, 0))
    in_specs = [lay, lay, lay, pl.BlockSpec((P, tr, C), lambda i: (0, i, 0))]
    ins = [w, m, v, parts]
    aliases = {}
    if prev is not None:
        in_specs += [pl.BlockSpec(memory_space=pl.ANY)] * 4
        ins += list(prev)
        aliases = {4 + k: k for k in range(4)}
    return pl.pallas_call(
        body, grid=(R // tr,), in_specs=in_specs, out_specs=[lay] * 4,
        out_shape=[jax.ShapeDtypeStruct(w.shape, F32)] * 4, input_output_aliases=aliases,
        compiler_params=_params(("parallel",), VMEM_LIMIT), name=name,
    )(*ins)


def _gelu_tanh(x):
    return jax.nn.gelu(x, approximate=True)


def _pack(arrs):
    tile = SUBLANE * LANE
    out = []
    for a in arrs:
        f = a.reshape(-1).astype(F32)
        out.append(jnp.pad(f, (0, (-f.shape[0]) % tile)))
    return jnp.concatenate(out)


def _unpack(flat, shapes):
    tile = SUBLANE * LANE
    out, off = [], 0
    for s in shapes:
        n = math.prod(s)
        out.append(flat[off:off + n].reshape(s))
        off += n + (-n) % tile
    return out


def kernel(x, gla_norm, gla_w_in, gla_w_gate_up, gla_b_gate, gla_o_norm, gla_w_out, s5_norm, s5_w_in, s5_lam_re, s5_lam_im, s5_log_dt, s5_b_re, s5_b_im, s5_c_re, s5_c_im, s5_d, s5_w_out, mlp_norm, mlp_w_up, mlp_w_down, final_norm, loss_target, m_gla_norm, m_gla_w_in, m_gla_w_gate_up, m_gla_b_gate, m_gla_o_norm, m_gla_w_out, m_s5_norm, m_s5_w_in, m_s5_lam_re, m_s5_lam_im, m_s5_log_dt, m_s5_b_re, m_s5_b_im, m_s5_c_re, m_s5_c_im, m_s5_d, m_s5_w_out, m_mlp_norm, m_mlp_w_up, m_mlp_w_down, m_final_norm, v_gla_norm, v_gla_w_in, v_gla_w_gate_up, v_gla_b_gate, v_gla_o_norm, v_gla_w_out, v_s5_norm, v_s5_w_in, v_s5_lam_re, v_s5_lam_im, v_s5_log_dt, v_s5_b_re, v_s5_b_im, v_s5_c_re, v_s5_c_im, v_s5_d, v_s5_w_out, v_mlp_norm, v_mlp_w_up, v_mlp_w_down, v_final_norm):
    W = dict(gla_norm=gla_norm, gla_w_in=gla_w_in, gla_w_gate_up=gla_w_gate_up, gla_b_gate=gla_b_gate, gla_o_norm=gla_o_norm, gla_w_out=gla_w_out, s5_norm=s5_norm, s5_w_in=s5_w_in, s5_lam_re=s5_lam_re, s5_lam_im=s5_lam_im, s5_log_dt=s5_log_dt, s5_b_re=s5_b_re, s5_b_im=s5_b_im, s5_c_re=s5_c_re, s5_c_im=s5_c_im, s5_d=s5_d, s5_w_out=s5_w_out, mlp_norm=mlp_norm, mlp_w_up=mlp_w_up, mlp_w_down=mlp_w_down, final_norm=final_norm)
    M = dict(gla_norm=m_gla_norm, gla_w_in=m_gla_w_in, gla_w_gate_up=m_gla_w_gate_up, gla_b_gate=m_gla_b_gate, gla_o_norm=m_gla_o_norm, gla_w_out=m_gla_w_out, s5_norm=m_s5_norm, s5_w_in=m_s5_w_in, s5_lam_re=m_s5_lam_re, s5_lam_im=m_s5_lam_im, s5_log_dt=m_s5_log_dt, s5_b_re=m_s5_b_re, s5_b_im=m_s5_b_im, s5_c_re=m_s5_c_re, s5_c_im=m_s5_c_im, s5_d=m_s5_d, s5_w_out=m_s5_w_out, mlp_norm=m_mlp_norm, mlp_w_up=m_mlp_w_up, mlp_w_down=m_mlp_w_down, final_norm=m_final_norm)
    V = dict(gla_norm=v_gla_norm, gla_w_in=v_gla_w_in, gla_w_gate_up=v_gla_w_gate_up, gla_b_gate=v_gla_b_gate, gla_o_norm=v_gla_o_norm, gla_w_out=v_gla_w_out, s5_norm=v_s5_norm, s5_w_in=v_s5_w_in, s5_lam_re=v_s5_lam_re, s5_lam_im=v_s5_lam_im, s5_log_dt=v_s5_log_dt, s5_b_re=v_s5_b_re, s5_b_im=v_s5_b_im, s5_c_re=v_s5_c_re, s5_c_im=v_s5_c_im, s5_d=v_s5_d, s5_w_out=v_s5_w_out, mlp_norm=v_mlp_norm, mlp_w_up=v_mlp_w_up, mlp_w_down=v_mlp_w_down, final_norm=v_final_norm)
    names = list(W)
    big = ["gla_w_in", "gla_w_out", "s5_w_in", "s5_w_out", "mlp_w_up", "mlp_w_down"]
    small_sharded = {"gla_w_gate_up": 2, "s5_norm": 1, "s5_d": 1}
    small = [n for n in names if n not in big]

    _, L, D = x.shape
    n_gla, n_s5, depth = gla_norm.shape[0], s5_lam_re.shape[0], mlp_norm.shape[0]
    H = GLA_HEADS
    KW, VW = D // 2, D
    DK, DV = KW // H, VW // H
    IN = 2 * KW + 2 * VW + GLA_RANK
    INP = 2 * KW + 2 * VW + LANE
    SW = s5_lam_re.shape[1] * S5_GROUP
    G, N = s5_lam_re.shape[1], s5_lam_re.shape[2]
    nb = G // S5_GB
    dev = _slot(_place())

    place = jnp.stack([lax.axis_index("c"), 2 * lax.axis_index("x") + lax.axis_index("y")]).astype(jnp.int32)
    sm_sh = _pack([W[n] for n in small_sharded]).reshape(1, -1, LANE)
    groups = []
    for i in range(depth):
        j = i // 2
        groups.append([("gla_w_in", j), ("gla_w_out", j)] if i % 2 == 0 else [("s5_w_in", j), ("s5_w_out", j)])
        groups.append([("mlp_w_up", i), ("mlp_w_down", i)])
    groups[0] = [("small", 0)] + groups[0]
    tok = jnp.zeros((SUBLANE, LANE), F32)
    level1, level2, full = [], {}, {}
    for gi, keys in enumerate(groups):
        lands = [_to_slot(sm_sh, 0, place, F32, "own_small") if n == "small" else _to_slot(W[n], j, place, BF16, f"own_{n}_{j}")
                 for n, j in keys]
        send_sems, recv_sems, _, lands, tok = _split_start(None, lands, _gather_routes, 4, tok, f"gather_start_{gi}")
        level1.append((send_sems, recv_sems, lands))

    def arrive(gi, after):
        send_sems, recv_sems, lands = level1[gi]
        _, lands = _split_wait(send_sems, recv_sems, None, lands, _gather_routes, after, f"gather_wait_{gi}")
        send_sems, recv_sems, _, lands, token = _split_start(None, lands, _pass_routes, 3, after, f"gather_pass_{gi}")
        level2[gi] = (send_sems, recv_sems, lands)
        return token

    def fetch(gi, after):
        send_sems, recv_sems, lands = level2[gi]
        _, lands = _split_wait(send_sems, recv_sems, None, lands, _pass_routes, after, f"gather_done_{gi}")
        full.update(zip(groups[gi], lands))

    tok = arrive(0, tok)
    fetch(0, tok)
    sm_all = full["small", 0]
    sm_parts = [_unpack(sm_all[d].reshape(-1), [W[n].shape for n in small_sharded]) for d in range(N_DEV)]
    wgu_full = jnp.concatenate([p[0] for p in sm_parts], axis=2)
    s5n_full = jnp.concatenate([p[1] for p in sm_parts], axis=1)
    s5d_full = jnp.concatenate([p[2] for p in sm_parts], axis=1)

    def gla_weights(j):
        w_in = full["gla_w_in", j]
        w_in = jnp.transpose(w_in, (1, 0, 2)).reshape(D, IN)
        w_in = jnp.pad(w_in, ((0, 0), (0, INP - IN)))
        w_out = full["gla_w_out", j].reshape(VW, D)
        wgu = jnp.pad(wgu_full[j], ((0, LANE - GLA_RANK), (0, 0))).astype(BF16)
        return w_in, w_out, wgu

    grads = {}
    h = x[0]
    saved = []

    for i in range(depth):
        j = i // 2
        if i > 0:
            fetch(2 * i, h)
        if i % 2 == 0:
            w_in, w_out, wgu = gla_weights(j)
            gn = gla_norm[j][None]
            hn = _rms_fwd(h, gn, f"gla{j}_norm")
            proj = _mm(hn, w_in, "nn", name=f"gla{j}_proj", tn=896)
            arrive(2 * i + 1, proj)
            bg, on = gla_b_gate[j][None], gla_o_norm[j][None]
            y, st = _gla_fwd(proj, wgu, bg, on, H=H, DK=DK, DV=DV, name=f"gla{j}_mix")
            h_new = _mm(y, w_out, "nn", name=f"gla{j}_out", res=h)
            saved.append(("gla", dict(h=h, hn=hn, proj=proj, y=y, st=st, w_in=w_in, w_out=w_out, wgu=wgu, gn=gn, bg=bg, on=on)))
        else:
            w_in = full["s5_w_in", j].reshape(D, SW)
            w_out3 = full["s5_w_out", j]
            sn = s5n_full[j][None]
            hn = _rms_fwd(h, sn, f"s5{j}_norm")
            u = _mm(hn, w_in, "nn", name=f"s5{j}_in")
            lre, lim = s5_lam_re[j][:, None, :], s5_lam_im[j][:, None, :]
            ldt = s5_log_dt[j][:, None, None]
            brt, bit = jnp.swapaxes(s5_b_re[j], 1, 2), jnp.swapaxes(s5_b_im[j], 1, 2)
            ar, ai, bbr, bbi, pr, pi, a124 = _s5_params(lre, lim, ldt, brt, bit, f"s5{j}_params")
            flat = lambda t: jnp.swapaxes(t, 0, 1).reshape(t.shape[1], G * N)
            pr, pi, a124 = flat(pr), flat(pi), flat(a124)
            bur = _mm_bd(u, _bd(bbr), name=f"s5{j}_bu_re")
            bui = _mm_bd(u, _bd(bbi), name=f"s5{j}_bu_im")
            xr, xi = _s5_scan(bur, bui, a124, pr, pi, reverse=False, name=f"s5{j}_scan")
            arrive(2 * i + 1, xr)
            crt, cit = jnp.swapaxes(s5_c_re[j], 1, 2), jnp.swapaxes(s5_c_im[j], 1, 2)
            cx = _mm_bd(xr, _bd(crt), name=f"s5{j}_cx_re")
            cx = _mm_bd(xi, _bd(-cit), name=f"s5{j}_cx_im", res=cx)
            dsk = s5d_full[j][None]

            def act(cxv, uv, dv):
                ypre = cxv + dv * uv
                return ypre, _gelu_tanh(ypre)
            ypre, yg = _rowmap(act, [cx, u], [dsk], [(SW, F32), (SW, BF16)], name=f"s5{j}_act")
            z = _mm_nn_cb(yg, w_out3, name=f"s5{j}_out")

            def glu(zv, hv):
                return (hv + zv[:, :D] * jax.nn.sigmoid(zv[:, D:]),)
            h_new = _rowmap(glu, [z, h], [], [(D, F32)], name=f"s5{j}_glu")[0]
            saved.append(("s5", dict(h=h, hn=hn, u=u, xr=xr, xi=xi, ypre=ypre, yg=yg, z=z, w_in=w_in, w_out3=w_out3, sn=sn,
                                     dsk=dsk, prm=(lre, lim, ldt, brt, bit), a124=a124, pr=pr, pi=pi, bbr=bbr, bbi=bbi)))
        h = h_new
        fetch(2 * i + 1, h)
        w_up3 = full["mlp_w_up", i]
        w_down = full["mlp_w_down", i].reshape(4 * D, D)
        mn = mlp_norm[i][None]
        hn = _rms_fwd(h, mn, f"mlp{i}_norm")
        def sq_relu(zv):
            a = jnp.maximum(zv, 0.0)
            return zv, a * a
        z, s = _mm_nn_cb(hn, w_up3, name=f"mlp{i}_up", out_dtype=[F32, BF16], epi=sq_relu)
        if i + 1 < depth:
            arrive(2 * i + 2, s)
        h_new = _mm(s, w_down, "nn", name=f"mlp{i}_down", res=h)
        saved.append(("mlp", dict(h=h, hn=hn, z=z, s=s, w_up3=w_up3, w_down=w_down, mn=mn)))
        h = h_new

    dh, sq, dfin = _loss_head(h, loss_target[0], final_norm[None], "loss_head")
    loss = lax.psum(0.5 * jnp.sum(sq) / D, ("x", "y", "c"))
    small_grads = {"final_norm": dfin[0]}
    big_parts = {n: {} for n in big}
    stacks = {n: {} for n in small if n != "final_norm"}
    flat3 = lambda a: a.reshape(a.shape[0], math.prod(a.shape[1:-1]), a.shape[-1])
    exchanges = []

    def send_grads(keys):
        tag = "_".join(f"{n}{j}" for n, j in keys)
        parts8 = [flat3(big_parts[n][j]) for n, j in keys]
        from_sib = _pair_exchange(parts8, name=f"grads_pair_{tag}")
        sums, lands = zip(*[_pair_sum(p, g, place, f"grads_pair_sum_{n}{j}") for (n, j), p, g in zip(keys, parts8, from_sib)])
        send_sems, recv_sems, srcs, lands, token = _split_start(list(sums), list(lands), _chip_routes, 3, place, f"grads_start_{tag}")
        exchanges.append((keys, tag, send_sems, recv_sems, srcs, lands))
        return token

    for idx in range(len(saved) - 1, -1, -1):
        kind, s = saved[idx]
        li = sum(1 for k, _ in saved[:idx] if k == kind)
        if kind == "mlp":
            def sq_relu_bwd(dsv, zv):
                return (dsv * 2.0 * jnp.maximum(zv, 0.0),)
            dz = _mm(dh, s["w_down"], "nt", name=f"mlp{li}_dz", out_dtype=BF16, epi=sq_relu_bwd, epi_ins=(s["z"],))
            big_parts["mlp_w_down"][li] = _mm(s["s"], dh, "tn", name=f"mlp{li}_dw_down", out_dtype=BF16).reshape(N_DEV, 4 * D // N_DEV, D)
            big_parts["mlp_w_up"][li] = _mm_tn_cbout(s["hn"], dz, N_DEV, name=f"mlp{li}_dw_up")
            token = send_grads([("mlp_w_down", li), ("mlp_w_up", li)])
            dhn = _mm_nt_cb(dz, s["w_up3"], name=f"mlp{li}_dhn")
            dh, dg = _rms_bwd(s["h"], dhn, dh, s["mn"], f"mlp{li}_dnorm", deps=(token,))
            stacks["mlp_norm"][li] = dg[0]
        elif kind == "gla":
            dyv = _mm(dh, s["w_out"], "nt", name=f"gla{li}_dy")
            big_parts["gla_w_out"][li] = _mm(s["y"], dh, "tn", name=f"gla{li}_dw_out", out_dtype=BF16).reshape(N_DEV, VW // N_DEV, D)
            dqkvr, dpre, don = _gla_bwd(s["proj"], s["wgu"], s["bg"], s["on"], s["st"], dyv, H=H, DK=DK, DV=DV, name=f"gla{li}_dmix")
            dglow = _mm(dpre, s["wgu"], "nt", name=f"gla{li}_dglow", out_dtype=BF16)
            glow = s["proj"][:, 2 * KW + 2 * VW:]
            dwgu = _mm(glow, dpre, "tn", name=f"gla{li}_dwgu")[:GLA_RANK]
            dbg = _rowmap(lambda t: (jnp.sum(t, axis=0, keepdims=True),), [dpre], [], [], [KW], name=f"gla{li}_dbg")[0]
            dproj = jnp.concatenate([dqkvr, dglow], axis=1)
            dw_in = _mm(s["hn"], dproj, "tn", name=f"gla{li}_dw_in", out_dtype=BF16, tn=896)[:, :IN]
            big_parts["gla_w_in"][li] = jnp.transpose(dw_in.reshape(D, N_DEV, IN // N_DEV), (1, 0, 2))
            token = send_grads([("gla_w_out", li), ("gla_w_in", li)])
            dhn = _mm(dproj, s["w_in"], "nt", name=f"gla{li}_dhn", tk=896)
            dh, dg = _rms_bwd(s["h"], dhn, dh, s["gn"], f"gla{li}_dnorm", deps=(token,))
            stacks["gla_norm"][li], stacks["gla_b_gate"][li], stacks["gla_o_norm"][li] = dg[0], dbg[0], don[0]
            stacks["gla_w_gate_up"][li] = dwgu
        else:
            def glu_bwd(zv, dhv):
                a, sg = zv[:, :D], jax.nn.sigmoid(zv[:, D:])
                return (jnp.concatenate([dhv * sg, dhv * a * sg * (1.0 - sg)], axis=1),)
            dz = _rowmap(glu_bwd, [s["z"], dh], [], [(2 * D, BF16)], name=f"s5{li}_dglu")[0]
            big_parts["s5_w_out"][li] = _mm_tn_cbout(s["yg"], dz, N_DEV, name=f"s5{li}_dw_out")
            dyg = _mm_nt_cb(dz, s["w_out3"], name=f"s5{li}_dyg")

            def act_bwd(dygv, ypv, uv, dv):
                _, vjp = jax.vjp(_gelu_tanh, ypv)
                dyp = vjp(dygv)[0]
                return dyp, dyp * dv, jnp.sum(dyp * uv, axis=0, keepdims=True)
            dyp, du, dd = _rowmap(act_bwd, [dyg, s["ypre"], s["u"]], [s["dsk"]], [(SW, F32), (SW, F32)], [SW], name=f"s5{li}_dact")
            cre, cim = s5_c_re[li], s5_c_im[li]
            wr = _mm_bd(dyp, _bd(cre), name=f"s5{li}_w_re")
            wi = _mm_bd(dyp, _bd(-cim), name=f"s5{li}_w_im")
            a124c = s["a124"] * jnp.array([1, -1, 1, -1, 1, -1, 1, 1], F32)[:, None]
            lr_, li_ = _s5_scan(wr, wi, a124c, s["pr"][::-1], -s["pi"][::-1], reverse=True, name=f"s5{li}_dscan")
            du = _mm_bd(lr_, _bd(jnp.swapaxes(s["bbr"], 1, 2)), name=f"s5{li}_du_re", res=du)
            du = _mm_bd(li_, _bd(jnp.swapaxes(s["bbi"], 1, 2)), name=f"s5{li}_du_im", res=du)
            dbbr = _bd_extract(_mm_tn_bd(s["u"], lr_, nb, name=f"s5{li}_dbb_re"), S5_GROUP, N)
            dbbi = _bd_extract(_mm_tn_bd(s["u"], li_, nb, name=f"s5{li}_dbb_im"), S5_GROUP, N)
            dcr = _bd_extract(_mm_tn_bd(dyp, s["xr"], nb, name=f"s5{li}_dc_re"), S5_GROUP, N)
            dci = -_bd_extract(_mm_tn_bd(dyp, s["xi"], nb, name=f"s5{li}_dc_im"), S5_GROUP, N)
            dar, dai = _s5_dabar(lr_, li_, s["xr"], s["xi"], f"s5{li}_dabar")
            dlre, dlim, dldt, dbrt, dbit = _s5_params_bwd(*s["prm"], dar.reshape(G, 1, N), dai.reshape(G, 1, N), dbbr, dbbi,
                                                         f"s5{li}_dparams")
            big_parts["s5_w_in"][li] = _mm(s["hn"], du, "tn", name=f"s5{li}_dw_in", out_dtype=BF16).reshape(N_DEV, D // N_DEV, SW)
            token = send_grads([("s5_w_out", li), ("s5_w_in", li)])
            dhn = _mm(du, s["w_in"], "nt", name=f"s5{li}_dhn")
            dh, dg = _rms_bwd(s["h"], dhn, dh, s["sn"], f"s5{li}_dnorm", deps=(token,))
            stacks["s5_norm"][li], stacks["s5_d"][li] = dg[0], dd[0]
            stacks["s5_lam_re"][li], stacks["s5_lam_im"][li], stacks["s5_log_dt"][li] = dlre[:, 0], dlim[:, 0], dldt[:, 0, 0]
            stacks["s5_b_re"][li], stacks["s5_b_im"][li] = jnp.swapaxes(dbrt, 1, 2), jnp.swapaxes(dbit, 1, 2)
            stacks["s5_c_re"][li], stacks["s5_c_im"][li] = dcr, dci
    grad_x = dh[None]

    for n, st_ in stacks.items():
        small_grads[n] = jnp.stack([st_[k] for k in range(len(st_))])
    full_shapes = [small_grads[n].shape for n in small]
    part = _pack([small_grads[n] for n in small]).reshape(1, -1, LANE)
    lands = [_to_slot(part, 0, place, F32, "own_small_grads")]
    send_sems, recv_sems, _, lands, t = _split_start(None, lands, _gather_routes, 4, place, "small_grads_start")
    _, lands = _split_wait(send_sems, recv_sems, None, lands, _gather_routes, t, "small_grads_wait")
    send_sems, recv_sems, _, lands, t = _split_start(None, lands, _pass_routes, 3, t, "small_grads_pass")
    _, lands = _split_wait(send_sems, recv_sems, None, lands, _pass_routes, t, "small_grads_done")
    parts_all = lands[0]

    def mine(n, a):
        if n not in small_sharded:
            return a
        ax = small_sharded[n]
        size = W[n].shape[ax]
        return lax.dynamic_slice_in_dim(a, dev * size, size, axis=ax)

    rows = parts_all.shape[1]
    def full_layout(T):
        arrs = []
        for n, shp in zip(small, full_shapes):
            if n in small_sharded:
                ax = small_sharded[n]
                arrs.append(lax.dynamic_update_slice_in_dim(jnp.zeros(shp, F32), T[n], dev * T[n].shape[ax], axis=ax))
            else:
                arrs.append(T[n])
        return _pack(arrs)
    ws, ms, vs = full_layout(W), full_layout(M), full_layout(V)
    sg, sd, snm, snv = _adamw(ws.reshape(1, rows, LANE), ms.reshape(1, rows, LANE), vs.reshape(1, rows, LANE),
                              parts_all.reshape(N_DEV, rows, LANE), 0, None, "adamw_small")
    outs = {}
    for key, flat in zip(("grad", "delta", "new_m", "new_v"), (sg, sd, snm, snv)):
        for n, a in zip(small, _unpack(flat.reshape(-1), full_shapes)):
            outs[key, n] = mine(n, a)

    stacked = {n: None for n in big}
    as3 = lambda a: a.reshape(a.shape[0], math.prod(a.shape[1:-1]), a.shape[-1])
    for keys, tag, send_sems, recv_sems, srcs, lands in exchanges:
        _, lands = _split_wait(send_sems, recv_sems, srcs, lands, _chip_routes, dh, f"grads_wait_{tag}")
        for (n, j), parts in zip(keys, lands):
            stacked[n] = _adamw(as3(W[n]), as3(M[n]), as3(V[n]), parts, j, stacked[n], f"adamw_{n}_{j}")
    for n in big:
        for key, a in zip(("grad", "delta", "new_m", "new_v"), stacked[n]):
            outs[key, n] = a.reshape(W[n].shape)

    return (loss, grad_x, *[outs["grad", n] for n in names], *[outs["delta", n] for n in names],
            *[outs["new_m", n] for n in names], *[outs["new_v", n] for n in names])
```

```python
import functools
import math

import jax
import jax.numpy as jnp
from jax import lax
from jax.experimental import pallas as pl
from jax.experimental.pallas import tpu as pltpu

F32, BF16 = jnp.float32, jnp.bfloat16
MESH = pl.DeviceIdType.MESH
N_DEV = 8
LANE = 128
SUBLANE = 8
VMEM_LIMIT = 48 * 1024 * 1024

EPS = 1e-6
CHUNK = 64
GLA_HEADS = 4
GLA_RANK = 16
GLA_TEMP = 16.0
S5_GROUP = 16
S5_STATE = 64
S5_EIG_CLIP = -1e-4
S5_GB = 16
ADAM_LR, ADAM_B1, ADAM_B2, ADAM_EPS, ADAM_WD, ADAM_STEP = 0.001, 0.9, 0.999, 1e-08, 0.01, 10

_CONTRACT = {"nn": ((1,), (0,)), "tn": ((0,), (0,)), "nt": ((1,), (1,))}


def _tile(n, pref, unit=LANE):
    if n <= pref:
        return n
    t = (pref // unit) * unit
    while t > unit and n % t:
        t -= unit
    assert n % t == 0, (n, pref, unit)
    return t


def _params(sem, vmem=None):
    return pltpu.CompilerParams(dimension_semantics=sem, vmem_limit_bytes=vmem)


def _dot(a, b, dims, precision=None):
    return lax.dot_general(a, b, (_CONTRACT[dims], ((), ())), preferred_element_type=F32, precision=precision)


def _bdot(a, b, dims):
    return _dot(a.astype(BF16), b.astype(BF16), dims)


def _mm_call(a, b, *, dims, grid, a_spec, b_spec, o_spec, out_shape, out_dtype, name, res=None, epi=None, epi_ins=(), deps=()):
    nk = grid[2]
    acc_shape = tuple(d for d in o_spec.block_shape if d is not None)
    if res is not None:
        epi, epi_ins = (lambda r, x: (r + x,)), (res,)
    single = not isinstance(out_dtype, (list, tuple))
    out_dtypes = [out_dtype] if single else list(out_dtype)
    n_e, n_o, n_d = len(epi_ins), len(out_dtypes), len(deps)

    def body(*refs):
        a_ref, b_ref = refs[:2]
        e_refs, o_refs = refs[2:2 + n_e], refs[2 + n_e + n_d:2 + n_e + n_d + n_o]

        def finish(r):
            vals = (r,) if epi is None else epi(r, *[e[...].astype(F32) for e in e_refs])
            for o_ref, v in zip(o_refs, vals):
                o_ref[...] = v.astype(o_ref.dtype)

        d = _bdot(a_ref[...], b_ref[...], dims)
        if nk == 1:
            finish(d)
            return
        acc = refs[-1]
        k = pl.program_id(2)

        @pl.when(k == 0)
        def _():
            acc[...] = d

        @pl.when((k > 0) & (k < nk - 1))
        def _():
            acc[...] += d

        @pl.when(k == nk - 1)
        def _():
            finish(acc[...] + d)

    outs = pl.pallas_call(
        body, grid=grid, in_specs=[a_spec, b_spec] + [o_spec] * n_e + [pl.BlockSpec(memory_space=pl.ANY)] * n_d,
        out_specs=[o_spec] * n_o, out_shape=[jax.ShapeDtypeStruct(out_shape, dt) for dt in out_dtypes],
        scratch_shapes=[] if nk == 1 else [pltpu.VMEM(acc_shape, F32)],
        compiler_params=_params(("parallel", "parallel", "arbitrary"), VMEM_LIMIT), name=name,
    )(a, b, *epi_ins, *deps)
    return outs[0] if single else outs


TM, TN, TK = 1024, 1024, 2048


def _mm(a, b, dims, *, name, out_dtype=F32, res=None, epi=None, epi_ins=(), deps=(), tm=TM, tn=TN, tk=TK):
    if dims == "tn":
        (K, M), (_, N) = a.shape, b.shape
    elif dims == "nn":
        (M, K), (_, N) = a.shape, b.shape
    else:
        (M, K), (N, _) = a.shape, b.shape
    tm, tn, tk = _tile(M, tm), _tile(N, tn), _tile(K, tk)
    a_spec = pl.BlockSpec((tk, tm), lambda i, j, k: (k, i)) if dims == "tn" else pl.BlockSpec((tm, tk), lambda i, j, k: (i, k))
    b_spec = pl.BlockSpec((tn, tk), lambda i, j, k: (j, k)) if dims == "nt" else pl.BlockSpec((tk, tn), lambda i, j, k: (k, j))
    o_spec = pl.BlockSpec((tm, tn), lambda i, j, k: (i, j))
    return _mm_call(a, b, dims=dims, grid=(M // tm, N // tn, K // tk), a_spec=a_spec, b_spec=b_spec, o_spec=o_spec,
                    out_shape=(M, N), out_dtype=out_dtype, name=name, res=res, epi=epi, epi_ins=epi_ins, deps=deps)


def _mm_nn_cb(a, b3, *, name, out_dtype=F32, epi=None, tm=TM, tn=TN, tk=TK):
    (M, K), (P, _, Ns) = a.shape, b3.shape
    tm, tn, tk = _tile(M, tm), _tile(Ns, tn), _tile(K, tk)
    npb = Ns // tn
    return _mm_call(a, b3, dims="nn", grid=(M // tm, P * npb, K // tk),
                    a_spec=pl.BlockSpec((tm, tk), lambda i, j, k: (i, k)),
                    b_spec=pl.BlockSpec((None, tk, tn), lambda i, j, k: (j // npb, k, j % npb)),
                    o_spec=pl.BlockSpec((tm, tn), lambda i, j, k: (i, j)),
                    out_shape=(M, P * Ns), out_dtype=out_dtype, name=name, epi=epi)


def _mm_nt_cb(a, b3, *, name, out_dtype=F32, tm=TM, tn=TN, tk=TK):
    (M, _), (P, N, Ns) = a.shape, b3.shape
    tm, tn, tk = _tile(M, tm), _tile(N, tn), _tile(Ns, tk)
    kpb = Ns // tk
    return _mm_call(a, b3, dims="nt", grid=(M // tm, N // tn, P * kpb),
                    a_spec=pl.BlockSpec((tm, tk), lambda i, j, k: (i, k)),
                    b_spec=pl.BlockSpec((None, tn, tk), lambda i, j, k: (k // kpb, j, k % kpb)),
                    o_spec=pl.BlockSpec((tm, tn), lambda i, j, k: (i, j)),
                    out_shape=(M, N), out_dtype=out_dtype, name=name)


def _mm_tn_cbout(a, b, parts, *, name, out_dtype=BF16, tm=TM, tn=TN, tk=TK):
    (K, M), (_, N) = a.shape, b.shape
    Ns = N // parts
    tm, tn, tk = _tile(M, tm), _tile(Ns, tn), _tile(K, tk)
    npb = Ns // tn
    return _mm_call(a, b, dims="tn", grid=(M // tm, parts * npb, K // tk),
                    a_spec=pl.BlockSpec((tk, tm), lambda i, j, k: (k, i)),
                    b_spec=pl.BlockSpec((tk, tn), lambda i, j, k: (k, j)),
                    o_spec=pl.BlockSpec((None, tm, tn), lambda i, j, k: (j // npb, i, j % npb)),
                    out_shape=(parts, M, Ns), out_dtype=out_dtype, name=name)


def _mm_bd(a, w3, *, name, out_dtype=F32, res=None, deps=(), tm=2048, tn=1024):
    (M, _), (nb, Kb, Nb) = a.shape, w3.shape
    tm, tn = _tile(M, tm), _tile(Nb, tn)
    npb = Nb // tn
    return _mm_call(a, w3, dims="nn", grid=(M // tm, nb * npb, 1),
                    a_spec=pl.BlockSpec((tm, Kb), lambda i, j, k: (i, j // npb)),
                    b_spec=pl.BlockSpec((None, Kb, tn), lambda i, j, k: (j // npb, 0, j % npb)),
                    o_spec=pl.BlockSpec((tm, tn), lambda i, j, k: (i, j)),
                    out_shape=(M, nb * Nb), out_dtype=out_dtype, name=name, res=res, deps=deps)


def _mm_tn_bd(a, b, nb, *, name, tk=1024):
    (K, MA), (_, NB) = a.shape, b.shape
    Ma, Nb = MA // nb, NB // nb
    tk = _tile(K, tk)
    return _mm_call(a, b, dims="tn", grid=(nb, 1, K // tk),
                    a_spec=pl.BlockSpec((tk, Ma), lambda i, j, k: (k, i)),
                    b_spec=pl.BlockSpec((tk, Nb), lambda i, j, k: (k, i)),
                    o_spec=pl.BlockSpec((None, Ma, Nb), lambda i, j, k: (i, 0, 0)),
                    out_shape=(nb, Ma, Nb), out_dtype=F32, name=name)


def _rowmap(fn, rows, consts, out_defs, red_defs=(), *, name, tr=256):
    L = rows[0].shape[0]
    widest = max([r.shape[1] for r in rows] + [n for n, _ in out_defs])
    tr = _tile(L, max(SUBLANE * 2, min(tr, 512 * 1024 // widest)), SUBLANE * 2)
    n_in, n_o, n_d = len(rows) + len(consts), len(out_defs), len(red_defs)

    def body(*refs):
        res = fn(*[r[...] for r in refs[:n_in]])
        res = res if isinstance(res, (tuple, list)) else (res,)
        outs = refs[n_in:]
        for o_ref, val in zip(outs[:n_o], res[:n_o]):
            o_ref[...] = val.astype(o_ref.dtype)
        if n_d:
            @pl.when(pl.program_id(0) == 0)
            def _():
                for o_ref in outs[n_o:]:
                    o_ref[...] = jnp.zeros_like(o_ref)
            for o_ref, val in zip(outs[n_o:], res[n_o:]):
                o_ref[...] += val

    in_specs = [pl.BlockSpec((tr, r.shape[1]), lambda i: (i, 0)) for r in rows]
    in_specs += [pl.BlockSpec(c.shape, lambda i, nd=c.ndim: (0,) * nd) for c in consts]
    out_specs = [pl.BlockSpec((tr, n), lambda i: (i, 0)) for n, _ in out_defs]
    out_specs += [pl.BlockSpec((1, n), lambda i: (0, 0)) for n in red_defs]
    out_shape = [jax.ShapeDtypeStruct((L, n), dt) for n, dt in out_defs]
    out_shape += [jax.ShapeDtypeStruct((1, n), F32) for n in red_defs]
    return pl.pallas_call(body, grid=(L // tr,), in_specs=in_specs, out_specs=out_specs, out_shape=out_shape,
                          compiler_params=_params(("arbitrary",), VMEM_LIMIT), name=name)(*rows, *consts)


def _rms_parts(x):
    r = lax.rsqrt(jnp.mean(x * x, axis=-1, keepdims=True) + EPS)
    return r, x * r


def _rms_fwd(h, g, name, deps=()):
    def fn(x, gg, *_):
        _, xh = _rms_parts(x)
        return (xh * gg,)
    return _rowmap(fn, [h], [g, *deps], [(h.shape[1], BF16)], name=name)[0]


def _rms_bwd(h, dhn, dh, g, name, deps=()):
    def fn(x, dy, dres, gg, *_):
        r, xh = _rms_parts(x)
        dxh = dy * gg
        dx = r * (dxh - xh * jnp.mean(dxh * xh, axis=-1, keepdims=True))
        return dres + dx, jnp.sum(dy * xh, axis=0, keepdims=True)
    D = h.shape[1]
    return _rowmap(fn, [h, dhn, dh], [g, *deps], [(D, F32)], [D], name=name)


def _loss_head(h, tgt, g, name):
    D = h.shape[1]

    def fn(x, t, gg):
        r, xh = _rms_parts(x)
        diff = xh * gg - t
        dy = diff * (1.0 / D)
        dxh = dy * gg
        dx = r * (dxh - xh * jnp.mean(dxh * xh, axis=-1, keepdims=True))
        return dx, jnp.sum(diff * diff, axis=0, keepdims=True), jnp.sum(dy * xh, axis=0, keepdims=True)
    return _rowmap(fn, [h, tgt], [g], [(D, F32)], [D, D], name=name)


def _tri(n, strict):
    r = lax.broadcasted_iota(jnp.int32, (n, n), 0)
    c = lax.broadcasted_iota(jnp.int32, (n, n), 1)
    return jnp.where((c < r) if strict else (c <= r), 1.0, 0.0).astype(F32)


def _gla_gate(g_ref, wgu_ref, bg_ref):
    pre = _bdot(g_ref[...], wgu_ref[...], "nn") + bg_ref[...]
    la = (jnp.minimum(pre, 0.0) - jnp.log(1.0 + jnp.exp(-jnp.abs(pre)))) * (1.0 / GLA_TEMP)
    cum = _dot(_tri(CHUNK, False), la, "nn", lax.Precision.HIGHEST)
    return pre, cum, cum[CHUNK - 1:CHUNK, :]


def _gla_specs(H, DK, DV, cmap):
    kb, vb, gb = H, (2 * H * DK) // DV, (2 * H * DK + 2 * H * DV) // LANE
    return [
        pl.BlockSpec((CHUNK, DK), lambda h, c: (cmap(h, c), h)),
        pl.BlockSpec((CHUNK, DK), lambda h, c: (cmap(h, c), kb + h)),
        pl.BlockSpec((CHUNK, DV), lambda h, c: (cmap(h, c), vb + h)),
        pl.BlockSpec((CHUNK, DV), lambda h, c: (cmap(h, c), vb + H + h)),
        pl.BlockSpec((CHUNK, LANE), lambda h, c: (cmap(h, c), gb)),
        pl.BlockSpec((LANE, DK), lambda h, c: (0, h)),
        pl.BlockSpec((1, DK), lambda h, c: (0, h)),
        pl.BlockSpec((1, DV), lambda h, c: (0, 0)),
    ]


def _gla_fwd(proj, wgu, bg, on, *, H, DK, DV, name, deps=()):
    L = proj.shape[0]
    nc = L // CHUNK
    scale = DK ** -0.5
    n_d = len(deps)

    def body(q_ref, k_ref, v_ref, r_ref, g_ref, wgu_ref, bg_ref, on_ref, *rest):
        y_ref, st_ref, S = rest[n_d:]
        @pl.when(pl.program_id(1) == 0)
        def _():
            S[...] = jnp.zeros_like(S)
        _, cum, total = _gla_gate(g_ref, wgu_ref, bg_ref)
        kd = k_ref[...] * jnp.exp(total - cum)
        St = S[...] * jnp.exp(total) + _bdot(v_ref[...], kd, "tn")
        S[...] = St
        st_ref[...] = St
        o = _bdot(q_ref[...] * scale, St, "nt")
        _, oh = _rms_parts(o)
        y_ref[...] = (oh * on_ref[...] * jax.nn.silu(r_ref[...])).astype(y_ref.dtype)

    return pl.pallas_call(
        body, grid=(H, nc), in_specs=_gla_specs(H, DK, DV, lambda h, c: c) + [pl.BlockSpec(memory_space=pl.ANY)] * n_d,
        out_specs=[pl.BlockSpec((CHUNK, DV), lambda h, c: (c, h)),
                   pl.BlockSpec((None, None, DV, DK), lambda h, c: (h, c, 0, 0))],
        out_shape=[jax.ShapeDtypeStruct((L, H * DV), BF16), jax.ShapeDtypeStruct((H, nc, DV, DK), F32)],
        scratch_shapes=[pltpu.VMEM((DV, DK), F32)],
        compiler_params=_params(("arbitrary", "arbitrary"), VMEM_LIMIT), name=name,
    )(proj, proj, proj, proj, proj, wgu, bg, on, *deps)


def _gla_bwd(proj, wgu, bg, on, st, dy, *, H, DK, DV, name):
    L = proj.shape[0]
    nc = L // CHUNK
    scale = DK ** -0.5
    rev = lambda h, c: nc - 1 - c

    def body(q_ref, k_ref, v_ref, r_ref, g_ref, wgu_ref, bg_ref, on_ref, sc_ref, sp_ref, dy_ref,
             dq_ref, dk_ref, dv_ref, dr_ref, dpre_ref, don_ref, G, decn):
        h, c = pl.program_id(0), pl.program_id(1)

        @pl.when(c == 0)
        def _():
            G[...] = jnp.zeros_like(G)
            decn[...] = jnp.zeros_like(decn)

        @pl.when((c == 0) & (h == 0))
        def _():
            don_ref[...] = jnp.zeros_like(don_ref)

        pre, cum, total = _gla_gate(g_ref, wgu_ref, bg_ref)
        ex = jnp.exp(total - cum)
        k, v, r = k_ref[...], v_ref[...], r_ref[...]
        kd = k * ex
        dec = jnp.exp(total)
        qs = q_ref[...] * scale
        Sc = sc_ref[...]
        o = _bdot(qs, Sc, "nt")
        rinv, oh = _rms_parts(o)
        gn = on_ref[...]
        sg = jax.nn.sigmoid(r)
        dyv = dy_ref[...]
        d_on = dyv * (r * sg)
        dr_ref[...] = (dyv * (oh * gn) * (sg * (1.0 + r * (1.0 - sg)))).astype(dr_ref.dtype)
        don_ref[...] += jnp.sum(d_on * oh, axis=0, keepdims=True)
        dxh = d_on * gn
        do = rinv * (dxh - oh * jnp.mean(dxh * oh, axis=-1, keepdims=True))
        dq_ref[...] = (_bdot(do, Sc, "nn") * scale).astype(dq_ref.dtype)
        Gt = G[...] * decn[...] + _bdot(do, qs, "tn")
        G[...] = Gt
        decn[...] = dec
        dkd = _bdot(v, Gt, "nn")
        dv_ref[...] = _bdot(kd, Gt, "nt").astype(dv_ref.dtype)
        Sp = sp_ref[...] * jnp.where(c == nc - 1, 0.0, 1.0)
        ddec = jnp.sum(Gt * Sp, axis=0, keepdims=True)
        dk_ref[...] = (dkd * ex).astype(dk_ref.dtype)
        dla = ddec * dec + _dot(_tri(CHUNK, True), dkd * kd, "nn", lax.Precision.HIGHEST)
        dpre_ref[...] = dla * (1.0 / GLA_TEMP) * jax.nn.sigmoid(-pre)

    in_specs = _gla_specs(H, DK, DV, rev) + [
        pl.BlockSpec((None, None, DV, DK), lambda h, c: (h, rev(h, c), 0, 0)),
        pl.BlockSpec((None, None, DV, DK), lambda h, c: (h, jnp.maximum(rev(h, c) - 1, 0), 0, 0)),
        pl.BlockSpec((CHUNK, DV), lambda h, c: (rev(h, c), h)),
    ]
    blk_k = pl.BlockSpec((CHUNK, DK), lambda h, c: (rev(h, c), h))
    blk_v = pl.BlockSpec((CHUNK, DV), lambda h, c: (rev(h, c), h))
    out_specs = [blk_k, blk_k, blk_v, blk_v, blk_k, pl.BlockSpec((1, DV), lambda h, c: (0, 0))]
    sds = jax.ShapeDtypeStruct
    out_shape = [sds((L, H * DK), BF16), sds((L, H * DK), BF16), sds((L, H * DV), BF16), sds((L, H * DV), BF16),
                 sds((L, H * DK), F32), sds((1, DV), F32)]
    dq, dk, dv, dr, dpre, don = pl.pallas_call(
        body, grid=(H, nc), in_specs=in_specs, out_specs=out_specs, out_shape=out_shape,
        scratch_shapes=[pltpu.VMEM((DV, DK), F32), pltpu.VMEM((1, DK), F32)],
        compiler_params=_params(("arbitrary", "arbitrary"), VMEM_LIMIT), name=name,
    )(proj, proj, proj, proj, proj, wgu, bg, on, st, st, dy)
    dqkvr = jnp.concatenate([dq, dk, dv, dr], axis=1)
    return dqkvr, dpre, don


def _s5_param_fn(lam_re, lam_im, log_dt, brt, bit):
    lr = jnp.minimum(lam_re, S5_EIG_CLIP)
    li = lam_im
    dt = jnp.exp(log_dt)
    mag = jnp.exp(lr * dt)
    ang = li * dt
    ab_re = mag * jnp.cos(ang)
    ab_im = mag * jnp.sin(ang)
    den = lr * lr + li * li
    nr = ab_re - 1.0
    f_re = (nr * lr + ab_im * li) / den
    f_im = (ab_im * lr - nr * li) / den
    return ab_re, ab_im, f_re * brt - f_im * bit, f_re * bit + f_im * brt


def _s5_params(lam_re, lam_im, log_dt, brt, bit, name):
    G, _, N = lam_re.shape

    def body(lr_ref, li_ref, dt_ref, br_ref, bi_ref, ar_ref, ai_ref, bbr_ref, bbi_ref, pr_ref, pi_ref, a124_ref):
        lre, lim, ldt = lr_ref[...], li_ref[...], dt_ref[...]
        ar, ai, bbr, bbi = _s5_param_fn(lre, lim, ldt, br_ref[...], bi_ref[...])
        ar_ref[...], ai_ref[...], bbr_ref[...], bbi_ref[...] = ar, ai, bbr, bbi
        kk = (lax.broadcasted_iota(jnp.int32, (1, SUBLANE, 1), 1) + 1).astype(F32)
        dt = jnp.exp(ldt)
        mag = jnp.exp(kk * (jnp.minimum(lre, S5_EIG_CLIP) * dt))
        ang = kk * (lim * dt)
        pr_ref[...] = mag * jnp.cos(ang)
        pi_ref[...] = mag * jnp.sin(ang)
        r = lax.broadcasted_iota(jnp.int32, (1, SUBLANE, 1), 1)
        k2 = jnp.where(r < 2, 1.0, jnp.where(r < 4, 2.0, jnp.where(r < 6, 4.0, 0.0)))
        mag2 = jnp.exp(k2 * (jnp.minimum(lre, S5_EIG_CLIP) * dt))
        ang2 = k2 * (lim * dt)
        a124_ref[...] = mag2 * jnp.where(r % 2 == 0, jnp.cos(ang2), jnp.sin(ang2))

    sds = jax.ShapeDtypeStruct
    return pl.pallas_call(
        body, out_shape=[sds((G, 1, N), F32), sds((G, 1, N), F32), sds(brt.shape, F32), sds(brt.shape, F32),
                         sds((G, SUBLANE, N), F32), sds((G, SUBLANE, N), F32), sds((G, SUBLANE, N), F32)], name=name,
    )(lam_re, lam_im, log_dt, brt, bit)


def _s5_params_bwd(lam_re, lam_im, log_dt, brt, bit, dar, dai, dbbr, dbbi, name):
    def body(lr_ref, li_ref, dt_ref, br_ref, bi_ref, dar_ref, dai_ref, dbbr_ref, dbbi_ref, *outs):
        _, vjp = jax.vjp(_s5_param_fn, lr_ref[...], li_ref[...], dt_ref[...], br_ref[...], bi_ref[...])
        for o_ref, val in zip(outs, vjp((dar_ref[...], dai_ref[...], dbbr_ref[...], dbbi_ref[...]))):
            o_ref[...] = val

    ins = (lam_re, lam_im, log_dt, brt, bit)
    return pl.pallas_call(body, out_shape=[jax.ShapeDtypeStruct(a.shape, F32) for a in ins], name=name)(
        *ins, dar, dai, dbbr, dbbi)


def _s5_scan(br, bi, a124, pr, pi, *, reverse, name, W=256):
    L, n = br.shape
    W = _tile(n, W)
    nblk = L // SUBLANE

    def body(br_ref, bi_ref, a_ref, pr_ref, pi_ref, xr_ref, xi_ref):
        A = a_ref[...]
        PR, PI = pr_ref[...], pi_ref[...]
        row = lax.broadcasted_iota(jnp.int32, (SUBLANE, W), 0)
        last = 0 if reverse else SUBLANE - 1

        def step(i, carry):
            cr, ci = carry
            off = pl.multiple_of(((nblk - 1 - i) if reverse else i) * SUBLANE, SUBLANE)
            xr, xi = br_ref[pl.ds(off, SUBLANE), :], bi_ref[pl.ds(off, SUBLANE), :]
            for j, k in enumerate((1, 2, 4)):
                ar, ai = A[2 * j:2 * j + 1, :], A[2 * j + 1:2 * j + 2, :]
                keep = (row < SUBLANE - k) if reverse else (row >= k)
                shift = (SUBLANE - k) if reverse else k
                sr = jnp.where(keep, pltpu.roll(xr, shift, 0), 0.0)
                si = jnp.where(keep, pltpu.roll(xi, shift, 0), 0.0)
                xr, xi = xr + ar * sr - ai * si, xi + ar * si + ai * sr
            xr, xi = xr + PR * cr - PI * ci, xi + PR * ci + PI * cr
            xr_ref[pl.ds(off, SUBLANE), :] = xr
            xi_ref[pl.ds(off, SUBLANE), :] = xi
            return xr[last:last + 1, :], xi[last:last + 1, :]

        z = jnp.zeros((1, W), F32)
        lax.fori_loop(0, nblk, step, (z, z))

    col = pl.BlockSpec((L, W), lambda j: (0, j))
    par = pl.BlockSpec((SUBLANE, W), lambda j: (0, j))
    return pl.pallas_call(
        body, grid=(n // W,), in_specs=[col, col, par, par, par], out_specs=[col, col],
        out_shape=[jax.ShapeDtypeStruct((L, n), F32)] * 2,
        compiler_params=_params(("parallel",), VMEM_LIMIT), name=name,
    )(br, bi, a124, pr, pi)


def _s5_dabar(lr, li, xr, xi, name, W=256):
    L, n = lr.shape
    W = _tile(n, W)

    def body(lr_ref, li_ref, xr_ref, xi_ref, dar_ref, dai_ref):
        first = lax.broadcasted_iota(jnp.int32, (L, W), 0) == 0
        pr = jnp.where(first, 0.0, pltpu.roll(xr_ref[...], 1, 0))
        pi = jnp.where(first, 0.0, pltpu.roll(xi_ref[...], 1, 0))
        a, b = lr_ref[...], li_ref[...]
        dar_ref[...] = jnp.sum(a * pr + b * pi, axis=0, keepdims=True)
        dai_ref[...] = jnp.sum(b * pr - a * pi, axis=0, keepdims=True)

    col = pl.BlockSpec((L, W), lambda j: (0, j))
    one = pl.BlockSpec((1, W), lambda j: (0, j))
    return pl.pallas_call(
        body, grid=(n // W,), in_specs=[col] * 4, out_specs=[one, one],
        out_shape=[jax.ShapeDtypeStruct((1, n), F32)] * 2,
        compiler_params=_params(("parallel",), VMEM_LIMIT), name=name,
    )(lr, li, xr, xi)


def _bd_all(mats, name, deps=()):
    n = len(mats)

    def body(*refs):
        for w_ref, o_ref in zip(refs[:n], refs[n + len(deps):]):
            G, A, B = w_ref.shape
            rows, cols = S5_GB * A, S5_GB * B
            tile = jnp.where(lax.broadcasted_iota(jnp.int32, (B, cols), 1) % B == lax.broadcasted_iota(jnp.int32, (B, cols), 0),
                             1.0, 0.0).astype(BF16)
            keep = (lax.broadcasted_iota(jnp.int32, (rows, cols), 0) // A) == (lax.broadcasted_iota(jnp.int32, (rows, cols), 1) // B)
            for kb in range(G // S5_GB):
                w2 = w_ref[kb * S5_GB:(kb + 1) * S5_GB].reshape(rows, B)
                o_ref[kb] = jnp.where(keep, _bdot(w2, tile, "nn"), 0.0).astype(o_ref.dtype)

    return pl.pallas_call(
        body, out_shape=[jax.ShapeDtypeStruct((m.shape[0] // S5_GB, S5_GB * m.shape[1], S5_GB * m.shape[2]), BF16) for m in mats],
        compiler_params=pltpu.CompilerParams(vmem_limit_bytes=VMEM_LIMIT), name=name,
    )(*mats, *deps)


def _bd_extract(m, A, B):
    nb = m.shape[0]
    d = jnp.diagonal(m.reshape(nb, S5_GB, A, S5_GB, B), axis1=1, axis2=3)
    return jnp.moveaxis(d, 3, 1).reshape(nb * S5_GB, A, B)


def _place():
    return lax.axis_index("x"), lax.axis_index("y"), lax.axis_index("c")


def _slot(p):
    return 4 * p[0] + 2 * p[1] + p[2]


N_CHIP = N_DEV // 2
_HBM = pl.BlockSpec(memory_space=pltpu.HBM)
_SEM = pl.BlockSpec(memory_space=pltpu.SEMAPHORE)
_ANY = pl.BlockSpec(memory_space=pl.ANY)
_EFFECT = pltpu.SideEffectType.DATAFLOW_SIDE_EFFECTING


def _gather_routes(x, y, c):
    me = _slot((x, y, c))
    peers = [(x, y, 1 - c)] + [(px, py, c) for px, py in ((1 - x, y), (x, 1 - y), (1 - x, 1 - y))]
    return [(p, me, me, _slot(p)) for p in peers]


def _chip_routes(x, y, c):
    myq = 2 * x + y
    return [((px, py, c), 2 * px + py, myq, 2 * px + py) for px, py in ((1 - x, y), (x, 1 - y), (1 - x, 1 - y))]


def _pass_routes(x, y, c):
    chips = ((1 - x, y), (x, 1 - y), (1 - x, 1 - y))
    return [((x, y, 1 - c), _slot((px, py, c)), _slot((px, py, c)), _slot((px, py, 1 - c))) for px, py in chips]


def _split_copies(s_refs, l_refs, send_sems, recv_sems, routes, arrival):
    out = []
    rts = routes(*_place())
    for a in range(len(l_refs)):
        for k, (peer, src_slot, dst_slot, arr_slot) in enumerate(rts):
            s_ref = l_refs[a] if s_refs is None else s_refs[a]
            src = s_ref if src_slot is None else s_ref.at[src_slot]
            sem = a * len(rts) + k
            out.append(pltpu.make_async_remote_copy(
                src_ref=src, dst_ref=l_refs[a].at[arr_slot if arrival else dst_slot], send_sem=send_sems.at[sem],
                recv_sem=recv_sems.at[sem], device_id=peer, device_id_type=MESH))
    return out


def _split_start(srcs, lands, routes, n_routes, dep, name):
    n, ns = len(lands), 0 if srcs is None else len(srcs)
    bufs = [*(srcs or ()), *lands]

    def body(*refs):
        s_refs = None if srcs is None else refs[:ns]
        for cp in _split_copies(s_refs, refs[ns:ns + n], refs[ns + n + 1], refs[ns + n + 2], routes, False):
            cp.start()
        refs[-1][...] = jnp.zeros_like(refs[-1])

    sems = pltpu.SemaphoreType.DMA((n * n_routes,))
    res = pl.pallas_call(
        body, name=name,
        out_shape=(sems, sems, *[pltpu.HBM(a.shape, a.dtype) for a in bufs], jax.ShapeDtypeStruct((SUBLANE, LANE), F32)),
        in_specs=[_HBM] * (ns + n) + [_ANY], out_specs=(_SEM, _SEM, *[_HBM] * (ns + n), pl.BlockSpec(memory_space=pltpu.VMEM)),
        input_output_aliases={i: 2 + i for i in range(ns + n)},
        compiler_params=pltpu.CompilerParams(has_side_effects=_EFFECT),
    )(*[pltpu.with_memory_space_constraint(a, pltpu.HBM) for a in bufs], dep)
    return res[0], res[1], (None if srcs is None else list(res[2:2 + ns])), list(res[2 + ns:2 + ns + n]), res[-1]


def _split_wait(send_sems, recv_sems, srcs, lands, routes, after, name):
    n, ns = len(lands), 0 if srcs is None else len(srcs)
    bufs = [*(srcs or ()), *lands]

    def body(*refs):
        s_refs = None if srcs is None else refs[:ns]
        for cp in _split_copies(s_refs, refs[ns:ns + n], refs[ns + n], refs[ns + n + 1], routes, True):
            cp.wait_send()
            cp.wait_recv()

    res = pl.pallas_call(
        body, name=name, out_shape=[pltpu.HBM(a.shape, a.dtype) for a in bufs],
        in_specs=[_HBM] * (ns + n) + [_SEM, _SEM, _ANY], out_specs=[_HBM] * (ns + n),
        input_output_aliases={i: i for i in range(ns + n)},
        compiler_params=pltpu.CompilerParams(has_side_effects=_EFFECT),
    )(*bufs, send_sems, recv_sems, after)
    return (None if srcs is None else list(res[:ns])), list(res[ns:])


def _to_slot(w, layer, place, out_dtype, name):
    _, R, C = w.shape
    unit = SUBLANE * (4 // jnp.dtype(out_dtype).itemsize)
    tr = _tile(R, max(unit, (512 * 1024 // C) // unit * unit), unit)

    def body(p_ref, w_ref, o_ref):
        o_ref[...] = w_ref[...].astype(o_ref.dtype)

    return pl.pallas_call(
        body,
        grid_spec=pltpu.PrefetchScalarGridSpec(
            num_scalar_prefetch=1, grid=(R // tr,),
            in_specs=[pl.BlockSpec((None, tr, C), lambda i, p: (layer, i, 0))],
            out_specs=pl.BlockSpec((None, tr, C), lambda i, p: (2 * p[1] + p[0], i, 0))),
        out_shape=jax.ShapeDtypeStruct((N_DEV, R, C), out_dtype),
        compiler_params=_params(("parallel",), VMEM_LIMIT), name=name,
    )(place, w)


def _pair_exchange(xs, name):
    n = len(xs)

    def body(*refs):
        x_refs, o_refs = refs[:n], refs[n:2 * n]
        send_sems, recv_sems = refs[2 * n:]
        x, y, c = _place()
        sib = (x, y, 1 - c)

        def copy(a, q):
            return pltpu.make_async_remote_copy(
                src_ref=x_refs[a].at[2 * q + (1 - c)], dst_ref=o_refs[a].at[q],
                send_sem=send_sems.at[a, q], recv_sem=recv_sems.at[a, q], device_id=sib, device_id_type=MESH)

        cps = [copy(a, q) for a in range(n) for q in range(N_CHIP)]
        for cp in cps:
            cp.start()
        for cp in cps:
            cp.wait_recv()
        for cp in cps:
            cp.wait_send()

    any_spec = pl.BlockSpec(memory_space=pl.ANY)
    return pl.pallas_call(
        body, in_specs=[any_spec] * n, out_specs=[any_spec] * n,
        out_shape=[jax.ShapeDtypeStruct((N_CHIP, *a.shape[1:]), a.dtype) for a in xs],
        scratch_shapes=[pltpu.SemaphoreType.DMA((n, N_CHIP)), pltpu.SemaphoreType.DMA((n, N_CHIP))],
        name=name,
    )(*xs)


def _pair_sum(x, got, place, name):
    _, R, C = x.shape
    unit = SUBLANE * (4 // x.dtype.itemsize)
    tr = _tile(R, max(unit, (512 * 1024 // C) // unit * unit), unit)

    def body(p_ref, x_ref, g_ref, o_ref, land_ref):
        s = (x_ref[...].astype(F32) + g_ref[...].astype(F32)).astype(o_ref.dtype)
        o_ref[...] = s

        @pl.when(pl.program_id(1) == p_ref[1])
        def _():
            land_ref[...] = s

    blk = lambda f: pl.BlockSpec((None, tr, C), f)
    return pl.pallas_call(
        body,
        grid_spec=pltpu.PrefetchScalarGridSpec(
            num_scalar_prefetch=1, grid=(R // tr, N_CHIP),
            in_specs=[blk(lambda i, q, p: (2 * q + p[0], i, 0)), blk(lambda i, q, p: (q, i, 0))],
            out_specs=[blk(lambda i, q, p: (q, i, 0)), blk(lambda i, q, p: (p[1], i, 0))]),
        out_shape=[jax.ShapeDtypeStruct(got.shape, x.dtype)] * 2,
        compiler_params=_params(("parallel", "arbitrary"), VMEM_LIMIT), name=name,
    )(place, x, got)


def _adamw_math(w, g, m, v):
    m = ADAM_B1 * m + (1.0 - ADAM_B1) * g
    v = ADAM_B2 * v + (1.0 - ADAM_B2) * (g * g)
    m_hat = m / (1.0 - ADAM_B1 ** ADAM_STEP)
    v_hat = v / (1.0 - ADAM_B2 ** ADAM_STEP)
    return -ADAM_LR * (m_hat / (jnp.sqrt(v_hat) + ADAM_EPS) + ADAM_WD * w), m, v


def _adamw(w, m, v, parts, layer, prev, name):
    nl, R, C = w.shape
    P = parts.shape[0]
    unit = SUBLANE * (4 // parts.dtype.itemsize)
    tr = _tile(R, max(unit, (128 * 1024 // C) // unit * unit), unit)

    def body(w_ref, m_ref, v_ref, p_ref, *rest):
        g_ref, d_ref, nm_ref, nv_ref = rest[-4:]
        g = p_ref[0].astype(F32)
        for p in range(1, P):
            g = g + p_ref[p].astype(F32)
        d, nm, nv = _adamw_math(w_ref[...], g, m_ref[...], v_ref[...])
        g_ref[...], d_ref[...], nm_ref[...], nv_ref[...] = g, d, nm, nv

    lay = pl.BlockSpec((None, tr, C), lambda i: (layer, i, 0))
    in_specs = [lay, lay, lay, pl.BlockSpec((P, tr, C), lambda i: (0, i, 0))]
    ins = [w, m, v, parts]
    aliases = {}
    if prev is not None:
        in_specs += [pl.BlockSpec(memory_space=pl.ANY)] * 4
        ins += list(prev)
        aliases = {4 + k: k for k in range(4)}
    return pl.pallas_call(
        body, grid=(R // tr,), in_specs=in_specs, out_specs=[lay] * 4,
        out_shape=[jax.ShapeDtypeStruct(w.shape, F32)] * 4, input_output_aliases=aliases,
        compiler_params=_params(("parallel",), VMEM_LIMIT), name=name,
    )(*ins)


def _gelu_tanh(x):
    return jax.nn.gelu(x, approximate=True)


def _pack(arrs):
    tile = SUBLANE * LANE
    out = []
    for a in arrs:
        f = a.reshape(-1).astype(F32)
        out.append(jnp.pad(f, (0, (-f.shape[0]) % tile)))
    return jnp.concatenate(out)


def _unpack(flat, shapes):
    tile = SUBLANE * LANE
    out, off = [], 0
    for s in shapes:
        n = math.prod(s)
        out.append(flat[off:off + n].reshape(s))
        off += n + (-n) % tile
    return out


def kernel(x, gla_norm, gla_w_in, gla_w_gate_up, gla_b_gate, gla_o_norm, gla_w_out, s5_norm, s5_w_in, s5_lam_re, s5_lam_im, s5_log_dt, s5_b_re, s5_b_im, s5_c_re, s5_c_im, s5_d, s5_w_out, mlp_norm, mlp_w_up, mlp_w_down, final_norm, loss_target, m_gla_norm, m_gla_w_in, m_gla_w_gate_up, m_gla_b_gate, m_gla_o_norm, m_gla_w_out, m_s5_norm, m_s5_w_in, m_s5_lam_re, m_s5_lam_im, m_s5_log_dt, m_s5_b_re, m_s5_b_im, m_s5_c_re, m_s5_c_im, m_s5_d, m_s5_w_out, m_mlp_norm, m_mlp_w_up, m_mlp_w_down, m_final_norm, v_gla_norm, v_gla_w_in, v_gla_w_gate_up, v_gla_b_gate, v_gla_o_norm, v_gla_w_out, v_s5_norm, v_s5_w_in, v_s5_lam_re, v_s5_lam_im, v_s5_log_dt, v_s5_b_re, v_s5_b_im, v_s5_c_re, v_s5_c_im, v_s5_d, v_s5_w_out, v_mlp_norm, v_mlp_w_up, v_mlp_w_down, v_final_norm):
    W = dict(gla_norm=gla_norm, gla_w_in=gla_w_in, gla_w_gate_up=gla_w_gate_up, gla_b_gate=gla_b_gate, gla_o_norm=gla_o_norm, gla_w_out=gla_w_out, s5_norm=s5_norm, s5_w_in=s5_w_in, s5_lam_re=s5_lam_re, s5_lam_im=s5_lam_im, s5_log_dt=s5_log_dt, s5_b_re=s5_b_re, s5_b_im=s5_b_im, s5_c_re=s5_c_re, s5_c_im=s5_c_im, s5_d=s5_d, s5_w_out=s5_w_out, mlp_norm=mlp_norm, mlp_w_up=mlp_w_up, mlp_w_down=mlp_w_down, final_norm=final_norm)
    M = dict(gla_norm=m_gla_norm, gla_w_in=m_gla_w_in, gla_w_gate_up=m_gla_w_gate_up, gla_b_gate=m_gla_b_gate, gla_o_norm=m_gla_o_norm, gla_w_out=m_gla_w_out, s5_norm=m_s5_norm, s5_w_in=m_s5_w_in, s5_lam_re=m_s5_lam_re, s5_lam_im=m_s5_lam_im, s5_log_dt=m_s5_log_dt, s5_b_re=m_s5_b_re, s5_b_im=m_s5_b_im, s5_c_re=m_s5_c_re, s5_c_im=m_s5_c_im, s5_d=m_s5_d, s5_w_out=m_s5_w_out, mlp_norm=m_mlp_norm, mlp_w_up=m_mlp_w_up, mlp_w_down=m_mlp_w_down, final_norm=m_final_norm)
    V = dict(gla_norm=v_gla_norm, gla_w_in=v_gla_w_in, gla_w_gate_up=v_gla_w_gate_up, gla_b_gate=v_gla_b_gate, gla_o_norm=v_gla_o_norm, gla_w_out=v_gla_w_out, s5_norm=v_s5_norm, s5_w_in=v_s5_w_in, s5_lam_re=v_s5_lam_re, s5_lam_im=v_s5_lam_im, s5_log_dt=v_s5_log_dt, s5_b_re=v_s5_b_re, s5_b_im=v_s5_b_im, s5_c_re=v_s5_c_re, s5_c_im=v_s5_c_im, s5_d=v_s5_d, s5_w_out=v_s5_w_out, mlp_norm=v_mlp_norm, mlp_w_up=v_mlp_w_up, mlp_w_down=v_mlp_w_down, final_norm=v_final_norm)
    names = list(W)
    big = ["gla_w_in", "gla_w_out", "s5_w_in", "s5_w_out", "mlp_w_up", "mlp_w_down"]
    small_sharded = {"gla_w_gate_up": 2, "s5_norm": 1, "s5_d": 1}
    small = [n for n in names if n not in big]

    _, L, D = x.shape
    n_gla, n_s5, depth = gla_norm.shape[0], s5_lam_re.shape[0], mlp_norm.shape[0]
    H = GLA_HEADS
    KW, VW = D // 2, D
    DK, DV = KW // H, VW // H
    IN = 2 * KW + 2 * VW + GLA_RANK
    INP = 2 * KW + 2 * VW + LANE
    SW = s5_lam_re.shape[1] * S5_GROUP
    G, N = s5_lam_re.shape[1], s5_lam_re.shape[2]
    nb = G // S5_GB
    dev = _slot(_place())

    place = jnp.stack([lax.axis_index("c"), 2 * lax.axis_index("x") + lax.axis_index("y")]).astype(jnp.int32)
    sm_sh = _pack([W[n] for n in small_sharded]).reshape(1, -1, LANE)
    groups = []
    for i in range(depth):
        j = i // 2
        groups.append([("gla_w_in", j), ("gla_w_out", j)] if i % 2 == 0 else [("s5_w_in", j), ("s5_w_out", j)])
        groups.append([("mlp_w_up", i), ("mlp_w_down", i)])
    groups[0] = [("small", 0)] + groups[0]
    tok = jnp.zeros((SUBLANE, LANE), F32)
    level1, level2, full = [], {}, {}
    for gi, keys in enumerate(groups):
        lands = [_to_slot(sm_sh, 0, place, F32, "own_small") if n == "small" else _to_slot(W[n], j, place, BF16, f"own_{n}_{j}")
                 for n, j in keys]
        send_sems, recv_sems, _, lands, tok = _split_start(None, lands, _gather_routes, 4, tok, f"gather_start_{gi}")
        level1.append((send_sems, recv_sems, lands))

    def arrive(gi, after):
        send_sems, recv_sems, lands = level1[gi]
        _, lands = _split_wait(send_sems, recv_sems, None, lands, _gather_routes, after, f"gather_wait_{gi}")
        send_sems, recv_sems, _, lands, token = _split_start(None, lands, _pass_routes, 3, after, f"gather_pass_{gi}")
        level2[gi] = (send_sems, recv_sems, lands)
        return token

    def fetch(gi, after):
        send_sems, recv_sems, lands = level2[gi]
        _, lands = _split_wait(send_sems, recv_sems, None, lands, _pass_routes, after, f"gather_done_{gi}")
        full.update(zip(groups[gi], lands))

    s5p = []
    for j in range(n_s5):
        lre, lim = s5_lam_re[j][:, None, :], s5_lam_im[j][:, None, :]
        ldt = s5_log_dt[j][:, None, None]
        brt, bit = jnp.swapaxes(s5_b_re[j], 1, 2), jnp.swapaxes(s5_b_im[j], 1, 2)
        ar, ai, bbr, bbi, pr, pi, a124 = _s5_params(lre, lim, ldt, brt, bit, f"s5{j}_params")
        flat = lambda t: jnp.swapaxes(t, 0, 1).reshape(t.shape[1], G * N)
        pr, pi, a124 = flat(pr), flat(pi), flat(a124)
        sw = lambda t: jnp.swapaxes(t, 1, 2)
        bd = _bd_all([bbr, bbi, sw(s5_c_re[j]), -sw(s5_c_im[j]), s5_c_re[j], -s5_c_im[j], sw(bbr), sw(bbi)], f"s5{j}_blockdiag",
                     (tok,) if j == n_s5 - 1 else ())
        s5p.append(dict(prm=(lre, lim, ldt, brt, bit), pr=pr, pi=pi, a124=a124, bd=dict(zip(
            ("bu_re", "bu_im", "cx_re", "cx_im", "w_re", "w_im", "du_re", "du_im"), bd))))

    tok = arrive(0, s5p[-1]["bd"]["bu_re"] if s5p else tok)
    fetch(0, tok)
    sm_all = full["small", 0]
    sm_parts = [_unpack(sm_all[d].reshape(-1), [W[n].shape for n in small_sharded]) for d in range(N_DEV)]
    wgu_full = jnp.concatenate([p[0] for p in sm_parts], axis=2)
    s5n_full = jnp.concatenate([p[1] for p in sm_parts], axis=1)
    s5d_full = jnp.concatenate([p[2] for p in sm_parts], axis=1)

    def gla_weights(j):
        w_in = full["gla_w_in", j]
        w_in = jnp.transpose(w_in, (1, 0, 2)).reshape(D, IN)
        w_in = jnp.pad(w_in, ((0, 0), (0, INP - IN)))
        w_out = full["gla_w_out", j].reshape(VW, D)
        wgu = jnp.pad(wgu_full[j], ((0, LANE - GLA_RANK), (0, 0))).astype(BF16)
        return w_in, w_out, wgu

    grads = {}
    h = x[0]
    saved = []

    for i in range(depth):
        j = i // 2
        if i > 0:
            fetch(2 * i, h)
        if i % 2 == 0:
            w_in, w_out, wgu = gla_weights(j)
            gn = gla_norm[j][None]
            hn = _rms_fwd(h, gn, f"gla{j}_norm")
            proj = _mm(hn, w_in, "nn", name=f"gla{j}_proj", tn=896)
            ahead = arrive(2 * i + 1, proj)
            bg, on = gla_b_gate[j][None], gla_o_norm[j][None]
            y, st = _gla_fwd(proj, wgu, bg, on, H=H, DK=DK, DV=DV, name=f"gla{j}_mix", deps=(ahead,))
            h_new = _mm(y, w_out, "nn", name=f"gla{j}_out", res=h)
            saved.append(("gla", dict(h=h, hn=hn, proj=proj, y=y, st=st, w_in=w_in, w_out=w_out, wgu=wgu, gn=gn, bg=bg, on=on)))
        else:
            w_in = full["s5_w_in", j].reshape(D, SW)
            w_out3 = full["s5_w_out", j]
            sn = s5n_full[j][None]
            hn = _rms_fwd(h, sn, f"s5{j}_norm")
            u = _mm(hn, w_in, "nn", name=f"s5{j}_in")
            sp = s5p[j]
            bur = _mm_bd(u, sp["bd"]["bu_re"], name=f"s5{j}_bu_re")
            bui = _mm_bd(u, sp["bd"]["bu_im"], name=f"s5{j}_bu_im")
            xr, xi = _s5_scan(bur, bui, sp["a124"], sp["pr"], sp["pi"], reverse=False, name=f"s5{j}_scan")
            ahead = arrive(2 * i + 1, xr)
            cx = _mm_bd(xr, sp["bd"]["cx_re"], name=f"s5{j}_cx_re", deps=(ahead,))
            cx = _mm_bd(xi, sp["bd"]["cx_im"], name=f"s5{j}_cx_im", res=cx)
            dsk = s5d_full[j][None]

            def act(cxv, uv, dv):
                ypre = cxv + dv * uv
                return ypre, _gelu_tanh(ypre)
            ypre, yg = _rowmap(act, [cx, u], [dsk], [(SW, F32), (SW, BF16)], name=f"s5{j}_act")
            z = _mm_nn_cb(yg, w_out3, name=f"s5{j}_out")

            def glu(zv, hv):
                return (hv + zv[:, :D] * jax.nn.sigmoid(zv[:, D:]),)
            h_new = _rowmap(glu, [z, h], [], [(D, F32)], name=f"s5{j}_glu")[0]
            saved.append(("s5", dict(h=h, hn=hn, u=u, xr=xr, xi=xi, ypre=ypre, yg=yg, z=z, w_in=w_in, w_out3=w_out3, sn=sn,
                                     dsk=dsk, **sp)))
        h = h_new
        fetch(2 * i + 1, h)
        w_up3 = full["mlp_w_up", i]
        w_down = full["mlp_w_down", i].reshape(4 * D, D)
        mn = mlp_norm[i][None]
        hn = _rms_fwd(h, mn, f"mlp{i}_norm")
        def sq_relu(zv):
            a = jnp.maximum(zv, 0.0)
            return zv, a * a
        z, s = _mm_nn_cb(hn, w_up3, name=f"mlp{i}_up", out_dtype=[F32, BF16], epi=sq_relu)
        ahead = (arrive(2 * i + 2, s),) if i + 1 < depth else ()
        h_new = _mm(s, w_down, "nn", name=f"mlp{i}_down", res=h, deps=ahead)
        saved.append(("mlp", dict(h=h, hn=hn, z=z, s=s, w_up3=w_up3, w_down=w_down, mn=mn)))
        h = h_new

    dh, sq, dfin = _loss_head(h, loss_target[0], final_norm[None], "loss_head")
    loss = lax.psum(0.5 * jnp.sum(sq) / D, ("x", "y", "c"))
    small_grads = {"final_norm": dfin[0]}
    big_parts = {n: {} for n in big}
    stacks = {n: {} for n in small if n != "final_norm"}
    flat3 = lambda a: a.reshape(a.shape[0], math.prod(a.shape[1:-1]), a.shape[-1])
    exchanges = []

    def send_grads(keys):
        tag = "_".join(f"{n}{j}" for n, j in keys)
        parts8 = [flat3(big_parts[n][j]) for n, j in keys]
        from_sib = _pair_exchange(parts8, name=f"grads_pair_{tag}")
        sums, lands = zip(*[_pair_sum(p, g, place, f"grads_pair_sum_{n}{j}") for (n, j), p, g in zip(keys, parts8, from_sib)])
        send_sems, recv_sems, srcs, lands, token = _split_start(list(sums), list(lands), _chip_routes, 3, place, f"grads_start_{tag}")
        exchanges.append((keys, tag, send_sems, recv_sems, srcs, lands))
        return token

    for idx in range(len(saved) - 1, -1, -1):
        kind, s = saved[idx]
        li = sum(1 for k, _ in saved[:idx] if k == kind)
        if kind == "mlp":
            def sq_relu_bwd(dsv, zv):
                return (dsv * 2.0 * jnp.maximum(zv, 0.0),)
            dz = _mm(dh, s["w_down"], "nt", name=f"mlp{li}_dz", out_dtype=BF16, epi=sq_relu_bwd, epi_ins=(s["z"],))
            big_parts["mlp_w_down"][li] = _mm(s["s"], dh, "tn", name=f"mlp{li}_dw_down", out_dtype=BF16).reshape(N_DEV, 4 * D // N_DEV, D)
            big_parts["mlp_w_up"][li] = _mm_tn_cbout(s["hn"], dz, N_DEV, name=f"mlp{li}_dw_up")
            token = send_grads([("mlp_w_down", li), ("mlp_w_up", li)])
            dhn = _mm_nt_cb(dz, s["w_up3"], name=f"mlp{li}_dhn")
            dh, dg = _rms_bwd(s["h"], dhn, dh, s["mn"], f"mlp{li}_dnorm", deps=(token,))
            stacks["mlp_norm"][li] = dg[0]
        elif kind == "gla":
            dyv = _mm(dh, s["w_out"], "nt", name=f"gla{li}_dy")
            big_parts["gla_w_out"][li] = _mm(s["y"], dh, "tn", name=f"gla{li}_dw_out", out_dtype=BF16).reshape(N_DEV, VW // N_DEV, D)
            dqkvr, dpre, don = _gla_bwd(s["proj"], s["wgu"], s["bg"], s["on"], s["st"], dyv, H=H, DK=DK, DV=DV, name=f"gla{li}_dmix")
            dglow = _mm(dpre, s["wgu"], "nt", name=f"gla{li}_dglow", out_dtype=BF16)
            glow = s["proj"][:, 2 * KW + 2 * VW:]
            dwgu = _mm(glow, dpre, "tn", name=f"gla{li}_dwgu")[:GLA_RANK]
            dbg = _rowmap(lambda t: (jnp.sum(t, axis=0, keepdims=True),), [dpre], [], [], [KW], name=f"gla{li}_dbg")[0]
            dproj = jnp.concatenate([dqkvr, dglow], axis=1)
            dw_in = _mm(s["hn"], dproj, "tn", name=f"gla{li}_dw_in", out_dtype=BF16, tn=896)[:, :IN]
            big_parts["gla_w_in"][li] = jnp.transpose(dw_in.reshape(D, N_DEV, IN // N_DEV), (1, 0, 2))
            token = send_grads([("gla_w_out", li), ("gla_w_in", li)])
            dhn = _mm(dproj, s["w_in"], "nt", name=f"gla{li}_dhn", tk=896)
            dh, dg = _rms_bwd(s["h"], dhn, dh, s["gn"], f"gla{li}_dnorm", deps=(token,))
            stacks["gla_norm"][li], stacks["gla_b_gate"][li], stacks["gla_o_norm"][li] = dg[0], dbg[0], don[0]
            stacks["gla_w_gate_up"][li] = dwgu
        else:
            def glu_bwd(zv, dhv):
                a, sg = zv[:, :D], jax.nn.sigmoid(zv[:, D:])
                return (jnp.concatenate([dhv * sg, dhv * a * sg * (1.0 - sg)], axis=1),)
            dz = _rowmap(glu_bwd, [s["z"], dh], [], [(2 * D, BF16)], name=f"s5{li}_dglu")[0]
            big_parts["s5_w_out"][li] = _mm_tn_cbout(s["yg"], dz, N_DEV, name=f"s5{li}_dw_out")
            dyg = _mm_nt_cb(dz, s["w_out3"], name=f"s5{li}_dyg")

            def act_bwd(dygv, ypv, uv, dv):
                _, vjp = jax.vjp(_gelu_tanh, ypv)
                dyp = vjp(dygv)[0]
                return dyp, dyp * dv, jnp.sum(dyp * uv, axis=0, keepdims=True)
            dyp, du, dd = _rowmap(act_bwd, [dyg, s["ypre"], s["u"]], [s["dsk"]], [(SW, F32), (SW, F32)], [SW], name=f"s5{li}_dact")
            wr = _mm_bd(dyp, s["bd"]["w_re"], name=f"s5{li}_w_re")
            wi = _mm_bd(dyp, s["bd"]["w_im"], name=f"s5{li}_w_im")
            a124c = s["a124"] * jnp.array([1, -1, 1, -1, 1, -1, 1, 1], F32)[:, None]
            lr_, li_ = _s5_scan(wr, wi, a124c, s["pr"][::-1], -s["pi"][::-1], reverse=True, name=f"s5{li}_dscan")
            du = _mm_bd(lr_, s["bd"]["du_re"], name=f"s5{li}_du_re", res=du)
            du = _mm_bd(li_, s["bd"]["du_im"], name=f"s5{li}_du_im", res=du)
            dbbr = _bd_extract(_mm_tn_bd(s["u"], lr_, nb, name=f"s5{li}_dbb_re"), S5_GROUP, N)
            dbbi = _bd_extract(_mm_tn_bd(s["u"], li_, nb, name=f"s5{li}_dbb_im"), S5_GROUP, N)
            dcr = _bd_extract(_mm_tn_bd(dyp, s["xr"], nb, name=f"s5{li}_dc_re"), S5_GROUP, N)
            dci = -_bd_extract(_mm_tn_bd(dyp, s["xi"], nb, name=f"s5{li}_dc_im"), S5_GROUP, N)
            dar, dai = _s5_dabar(lr_, li_, s["xr"], s["xi"], f"s5{li}_dabar")
            dlre, dlim, dldt, dbrt, dbit = _s5_params_bwd(*s["prm"], dar.reshape(G, 1, N), dai.reshape(G, 1, N), dbbr, dbbi,
                                                         f"s5{li}_dparams")
            big_parts["s5_w_in"][li] = _mm(s["hn"], du, "tn", name=f"s5{li}_dw_in", out_dtype=BF16).reshape(N_DEV, D // N_DEV, SW)
            token = send_grads([("s5_w_out", li), ("s5_w_in", li)])
            dhn = _mm(du, s["w_in"], "nt", name=f"s5{li}_dhn")
            dh, dg = _rms_bwd(s["h"], dhn, dh, s["sn"], f"s5{li}_dnorm", deps=(token,))
            stacks["s5_norm"][li], stacks["s5_d"][li] = dg[0], dd[0]
            stacks["s5_lam_re"][li], stacks["s5_lam_im"][li], stacks["s5_log_dt"][li] = dlre[:, 0], dlim[:, 0], dldt[:, 0, 0]
            stacks["s5_b_re"][li], stacks["s5_b_im"][li] = jnp.swapaxes(dbrt, 1, 2), jnp.swapaxes(dbit, 1, 2)
            stacks["s5_c_re"][li], stacks["s5_c_im"][li] = dcr, dci
    grad_x = dh[None]

    for n, st_ in stacks.items():
        small_grads[n] = jnp.stack([st_[k] for k in range(len(st_))])
    full_shapes = [small_grads[n].shape for n in small]
    part = _pack([small_grads[n] for n in small]).reshape(1, -1, LANE)
    lands = [_to_slot(part, 0, place, F32, "own_small_grads")]
    send_sems, recv_sems, _, lands, t = _split_start(None, lands, _gather_routes, 4, place, "small_grads_start")

    stacked = {n: None for n in big}
    as3 = lambda a: a.reshape(a.shape[0], math.prod(a.shape[1:-1]), a.shape[-1])

    def finish(exchange, after):
        keys, tag, e_send, e_recv, e_srcs, e_lands = exchange
        _, e_lands = _split_wait(e_send, e_recv, e_srcs, e_lands, _chip_routes, after, f"grads_wait_{tag}")
        for (n, j), parts in zip(keys, e_lands):
            stacked[n] = _adamw(as3(W[n]), as3(M[n]), as3(V[n]), parts, j, stacked[n], f"adamw_{n}_{j}")
        return stacked[keys[0][0]][0]

    for exchange in exchanges[:-1]:
        t = finish(exchange, dh)

    _, lands = _split_wait(send_sems, recv_sems, None, lands, _gather_routes, t, "small_grads_wait")
    send_sems, recv_sems, _, lands, t = _split_start(None, lands, _pass_routes, 3, t, "small_grads_pass")
    _, lands = _split_wait(send_sems, recv_sems, None, lands, _pass_routes, t, "small_grads_done")
    parts_all = lands[0]

    def mine(n, a):
        if n not in small_sharded:
            return a
        ax = small_sharded[n]
        size = W[n].shape[ax]
        return lax.dynamic_slice_in_dim(a, dev * size, size, axis=ax)

    rows = parts_all.shape[1]
    def full_layout(T):
        arrs = []
        for n, shp in zip(small, full_shapes):
            if n in small_sharded:
                ax = small_sharded[n]
                arrs.append(lax.dynamic_update_slice_in_dim(jnp.zeros(shp, F32), T[n], dev * T[n].shape[ax], axis=ax))
            else:
                arrs.append(T[n])
        return _pack(arrs)
    ws, ms, vs = full_layout(W), full_layout(M), full_layout(V)
    sg, sd, snm, snv = _adamw(ws.reshape(1, rows, LANE), ms.reshape(1, rows, LANE), vs.reshape(1, rows, LANE),
                              parts_all.reshape(N_DEV, rows, LANE), 0, None, "adamw_small")
    outs = {}
    for key, flat in zip(("grad", "delta", "new_m", "new_v"), (sg, sd, snm, snv)):
        for n, a in zip(small, _unpack(flat.reshape(-1), full_shapes)):
            outs[key, n] = mine(n, a)

    finish(exchanges[-1], sg)
    for n in big:
        for key, a in zip(("grad", "delta", "new_m", "new_v"), stacked[n]):
            outs[key, n] = a.reshape(W[n].shape)

    return (loss, grad_x, *[outs["grad", n] for n in names], *[outs["delta", n] for n in names],
            *[outs["new_m", n] for n in names], *[outs["new_v", n] for n in names])
```

```python
import functools
import math

import jax
import jax.numpy as jnp
from jax import lax
from jax.experimental import pallas as pl
from jax.experimental.pallas import tpu as pltpu

F32, BF16 = jnp.float32, jnp.bfloat16
MESH = pl.DeviceIdType.MESH
N_DEV = 8
LANE = 128
SUBLANE = 8
VMEM_LIMIT = 48 * 1024 * 1024

EPS = 1e-6
CHUNK = 64
GLA_HEADS = 4
GLA_RANK = 16
GLA_TEMP = 16.0
S5_GROUP = 16
S5_STATE = 64
S5_EIG_CLIP = -1e-4
S5_GB = 16
ADAM_LR, ADAM_B1, ADAM_B2, ADAM_EPS, ADAM_WD, ADAM_STEP = 0.001, 0.9, 0.999, 1e-08, 0.01, 10

_CONTRACT = {"nn": ((1,), (0,)), "tn": ((0,), (0,)), "nt": ((1,), (1,))}


def _tile(n, pref, unit=LANE):
    if n <= pref:
        return n
    t = (pref // unit) * unit
    while t > unit and n % t:
        t -= unit
    assert n % t == 0, (n, pref, unit)
    return t


def _params(sem, vmem=None):
    return pltpu.CompilerParams(dimension_semantics=sem, vmem_limit_bytes=vmem)


def _dot(a, b, dims, precision=None):
    return lax.dot_general(a, b, (_CONTRACT[dims], ((), ())), preferred_element_type=F32, precision=precision)


def _bdot(a, b, dims):
    return _dot(a.astype(BF16), b.astype(BF16), dims)


def _mm_call(a, b, *, dims, grid, a_spec, b_spec, o_spec, out_shape, out_dtype, name, res=None, epi=None, epi_ins=(), deps=()):
    nk = grid[2]
    acc_shape = tuple(d for d in o_spec.block_shape if d is not None)
    if res is not None:
        epi, epi_ins = (lambda r, x: (r + x,)), (res,)
    single = not isinstance(out_dtype, (list, tuple))
    out_dtypes = [out_dtype] if single else list(out_dtype)
    n_e, n_o, n_d = len(epi_ins), len(out_dtypes), len(deps)

    def body(*refs):
        a_ref, b_ref = refs[:2]
        e_refs, o_refs = refs[2:2 + n_e], refs[2 + n_e + n_d:2 + n_e + n_d + n_o]

        def finish(r):
            vals = (r,) if epi is None else epi(r, *[e[...].astype(F32) for e in e_refs])
            for o_ref, v in zip(o_refs, vals):
                o_ref[...] = v.astype(o_ref.dtype)

        d = _bdot(a_ref[...], b_ref[...], dims)
        if nk == 1:
            finish(d)
            return
        acc = refs[-1]
        k = pl.program_id(2)

        @pl.when(k == 0)
        def _():
            acc[...] = d

        @pl.when((k > 0) & (k < nk - 1))
        def _():
            acc[...] += d

        @pl.when(k == nk - 1)
        def _():
            finish(acc[...] + d)

    outs = pl.pallas_call(
        body, grid=grid, in_specs=[a_spec, b_spec] + [o_spec] * n_e + [pl.BlockSpec(memory_space=pl.ANY)] * n_d,
        out_specs=[o_spec] * n_o, out_shape=[jax.ShapeDtypeStruct(out_shape, dt) for dt in out_dtypes],
        scratch_shapes=[] if nk == 1 else [pltpu.VMEM(acc_shape, F32)],
        compiler_params=_params(("parallel", "parallel", "arbitrary"), VMEM_LIMIT), name=name,
    )(a, b, *epi_ins, *deps)
    return outs[0] if single else outs


TM, TN, TK = 1024, 1024, 2048


def _mm(a, b, dims, *, name, out_dtype=F32, res=None, epi=None, epi_ins=(), deps=(), tm=TM, tn=TN, tk=TK):
    if dims == "tn":
        (K, M), (_, N) = a.shape, b.shape
    elif dims == "nn":
        (M, K), (_, N) = a.shape, b.shape
    else:
        (M, K), (N, _) = a.shape, b.shape
    tm, tn, tk = _tile(M, tm), _tile(N, tn), _tile(K, tk)
    a_spec = pl.BlockSpec((tk, tm), lambda i, j, k: (k, i)) if dims == "tn" else pl.BlockSpec((tm, tk), lambda i, j, k: (i, k))
    b_spec = pl.BlockSpec((tn, tk), lambda i, j, k: (j, k)) if dims == "nt" else pl.BlockSpec((tk, tn), lambda i, j, k: (k, j))
    o_spec = pl.BlockSpec((tm, tn), lambda i, j, k: (i, j))
    return _mm_call(a, b, dims=dims, grid=(M // tm, N // tn, K // tk), a_spec=a_spec, b_spec=b_spec, o_spec=o_spec,
                    out_shape=(M, N), out_dtype=out_dtype, name=name, res=res, epi=epi, epi_ins=epi_ins, deps=deps)


def _mm_nn_cb(a, b3, *, name, out_dtype=F32, epi=None, tm=TM, tn=TN, tk=TK):
    (M, K), (P, _, Ns) = a.shape, b3.shape
    tm, tn, tk = _tile(M, tm), _tile(Ns, tn), _tile(K, tk)
    npb = Ns // tn
    return _mm_call(a, b3, dims="nn", grid=(M // tm, P * npb, K // tk),
                    a_spec=pl.BlockSpec((tm, tk), lambda i, j, k: (i, k)),
                    b_spec=pl.BlockSpec((None, tk, tn), lambda i, j, k: (j // npb, k, j % npb)),
                    o_spec=pl.BlockSpec((tm, tn), lambda i, j, k: (i, j)),
                    out_shape=(M, P * Ns), out_dtype=out_dtype, name=name, epi=epi)


def _mm_nt_cb(a, b3, *, name, out_dtype=F32, tm=TM, tn=TN, tk=TK):
    (M, _), (P, N, Ns) = a.shape, b3.shape
    tm, tn, tk = _tile(M, tm), _tile(N, tn), _tile(Ns, tk)
    kpb = Ns // tk
    return _mm_call(a, b3, dims="nt", grid=(M // tm, N // tn, P * kpb),
                    a_spec=pl.BlockSpec((tm, tk), lambda i, j, k: (i, k)),
                    b_spec=pl.BlockSpec((None, tn, tk), lambda i, j, k: (k // kpb, j, k % kpb)),
                    o_spec=pl.BlockSpec((tm, tn), lambda i, j, k: (i, j)),
                    out_shape=(M, N), out_dtype=out_dtype, name=name)


def _mm_tn_cbout(a, b, parts, *, name, out_dtype=BF16, tm=TM, tn=TN, tk=TK):
    (K, M), (_, N) = a.shape, b.shape
    Ns = N // parts
    tm, tn, tk = _tile(M, tm), _tile(Ns, tn), _tile(K, tk)
    npb = Ns // tn
    return _mm_call(a, b, dims="tn", grid=(M // tm, parts * npb, K // tk),
                    a_spec=pl.BlockSpec((tk, tm), lambda i, j, k: (k, i)),
                    b_spec=pl.BlockSpec((tk, tn), lambda i, j, k: (k, j)),
                    o_spec=pl.BlockSpec((None, tm, tn), lambda i, j, k: (j // npb, i, j % npb)),
                    out_shape=(parts, M, Ns), out_dtype=out_dtype, name=name)


def _mm_bd(a, w3, *, name, out_dtype=F32, res=None, deps=(), tm=2048, tn=1024):
    (M, _), (nb, Kb, Nb) = a.shape, w3.shape
    tm, tn = _tile(M, tm), _tile(Nb, tn)
    npb = Nb // tn
    return _mm_call(a, w3, dims="nn", grid=(M // tm, nb * npb, 1),
                    a_spec=pl.BlockSpec((tm, Kb), lambda i, j, k: (i, j // npb)),
                    b_spec=pl.BlockSpec((None, Kb, tn), lambda i, j, k: (j // npb, 0, j % npb)),
                    o_spec=pl.BlockSpec((tm, tn), lambda i, j, k: (i, j)),
                    out_shape=(M, nb * Nb), out_dtype=out_dtype, name=name, res=res, deps=deps)


def _mm_tn_bd(a, b, nb, *, name, tk=1024):
    (K, MA), (_, NB) = a.shape, b.shape
    Ma, Nb = MA // nb, NB // nb
    tk = _tile(K, tk)
    return _mm_call(a, b, dims="tn", grid=(nb, 1, K // tk),
                    a_spec=pl.BlockSpec((tk, Ma), lambda i, j, k: (k, i)),
                    b_spec=pl.BlockSpec((tk, Nb), lambda i, j, k: (k, i)),
                    o_spec=pl.BlockSpec((None, Ma, Nb), lambda i, j, k: (i, 0, 0)),
                    out_shape=(nb, Ma, Nb), out_dtype=F32, name=name)


def _rowmap(fn, rows, consts, out_defs, red_defs=(), *, name, tr=256):
    L = rows[0].shape[0]
    widest = max([r.shape[1] for r in rows] + [n for n, _ in out_defs])
    tr = _tile(L, max(SUBLANE * 2, min(tr, 512 * 1024 // widest)), SUBLANE * 2)
    n_in, n_o, n_d = len(rows) + len(consts), len(out_defs), len(red_defs)

    def body(*refs):
        res = fn(*[r[...] for r in refs[:n_in]])
        res = res if isinstance(res, (tuple, list)) else (res,)
        outs = refs[n_in:]
        for o_ref, val in zip(outs[:n_o], res[:n_o]):
            o_ref[...] = val.astype(o_ref.dtype)
        if n_d:
            @pl.when(pl.program_id(0) == 0)
            def _():
                for o_ref in outs[n_o:]:
                    o_ref[...] = jnp.zeros_like(o_ref)
            for o_ref, val in zip(outs[n_o:], res[n_o:]):
                o_ref[...] += val

    in_specs = [pl.BlockSpec((tr, r.shape[1]), lambda i: (i, 0)) for r in rows]
    in_specs += [pl.BlockSpec(c.shape, lambda i, nd=c.ndim: (0,) * nd) for c in consts]
    out_specs = [pl.BlockSpec((tr, n), lambda i: (i, 0)) for n, _ in out_defs]
    out_specs += [pl.BlockSpec((1, n), lambda i: (0, 0)) for n in red_defs]
    out_shape = [jax.ShapeDtypeStruct((L, n), dt) for n, dt in out_defs]
    out_shape += [jax.ShapeDtypeStruct((1, n), F32) for n in red_defs]
    return pl.pallas_call(body, grid=(L // tr,), in_specs=in_specs, out_specs=out_specs, out_shape=out_shape,
                          compiler_params=_params(("arbitrary",), VMEM_LIMIT), name=name)(*rows, *consts)


def _rms_parts(x):
    r = lax.rsqrt(jnp.mean(x * x, axis=-1, keepdims=True) + EPS)
    return r, x * r


def _rms_fwd(h, g, name, deps=()):
    def fn(x, gg, *_):
        _, xh = _rms_parts(x)
        return (xh * gg,)
    return _rowmap(fn, [h], [g, *deps], [(h.shape[1], BF16)], name=name)[0]


def _rms_bwd(h, dhn, dh, g, name, deps=()):
    def fn(x, dy, dres, gg, *_):
        r, xh = _rms_parts(x)
        dxh = dy * gg
        dx = r * (dxh - xh * jnp.mean(dxh * xh, axis=-1, keepdims=True))
        return dres + dx, jnp.sum(dy * xh, axis=0, keepdims=True)
    D = h.shape[1]
    return _rowmap(fn, [h, dhn, dh], [g, *deps], [(D, F32)], [D], name=name)


def _loss_head(h, tgt, g, name):
    D = h.shape[1]

    def fn(x, t, gg):
        r, xh = _rms_parts(x)
        diff = xh * gg - t
        dy = diff * (1.0 / D)
        dxh = dy * gg
        dx = r * (dxh - xh * jnp.mean(dxh * xh, axis=-1, keepdims=True))
        return dx, jnp.sum(diff * diff, axis=0, keepdims=True), jnp.sum(dy * xh, axis=0, keepdims=True)
    return _rowmap(fn, [h, tgt], [g], [(D, F32)], [D, D], name=name)


def _tri(n, strict):
    r = lax.broadcasted_iota(jnp.int32, (n, n), 0)
    c = lax.broadcasted_iota(jnp.int32, (n, n), 1)
    return jnp.where((c < r) if strict else (c <= r), 1.0, 0.0).astype(F32)


def _gla_gate(g_ref, wgu_ref, bg_ref):
    pre = _bdot(g_ref[...], wgu_ref[...], "nn") + bg_ref[...]
    la = (jnp.minimum(pre, 0.0) - jnp.log(1.0 + jnp.exp(-jnp.abs(pre)))) * (1.0 / GLA_TEMP)
    cum = _dot(_tri(CHUNK, False), la, "nn", lax.Precision.HIGHEST)
    return pre, cum, cum[CHUNK - 1:CHUNK, :]


def _gla_specs(H, DK, DV, cmap):
    kb, vb, gb = H, (2 * H * DK) // DV, (2 * H * DK + 2 * H * DV) // LANE
    return [
        pl.BlockSpec((CHUNK, DK), lambda h, c: (cmap(h, c), h)),
        pl.BlockSpec((CHUNK, DK), lambda h, c: (cmap(h, c), kb + h)),
        pl.BlockSpec((CHUNK, DV), lambda h, c: (cmap(h, c), vb + h)),
        pl.BlockSpec((CHUNK, DV), lambda h, c: (cmap(h, c), vb + H + h)),
        pl.BlockSpec((CHUNK, LANE), lambda h, c: (cmap(h, c), gb)),
        pl.BlockSpec((LANE, DK), lambda h, c: (0, h)),
        pl.BlockSpec((1, DK), lambda h, c: (0, h)),
        pl.BlockSpec((1, DV), lambda h, c: (0, 0)),
    ]


def _gla_fwd(proj, wgu, bg, on, *, H, DK, DV, name, deps=()):
    L = proj.shape[0]
    nc = L // CHUNK
    scale = DK ** -0.5
    n_d = len(deps)

    def body(q_ref, k_ref, v_ref, r_ref, g_ref, wgu_ref, bg_ref, on_ref, *rest):
        y_ref, st_ref, S = rest[n_d:]
        @pl.when(pl.program_id(1) == 0)
        def _():
            S[...] = jnp.zeros_like(S)
        _, cum, total = _gla_gate(g_ref, wgu_ref, bg_ref)
        kd = k_ref[...] * jnp.exp(total - cum)
        St = S[...] * jnp.exp(total) + _bdot(v_ref[...], kd, "tn")
        S[...] = St
        st_ref[...] = St
        o = _bdot(q_ref[...] * scale, St, "nt")
        _, oh = _rms_parts(o)
        y_ref[...] = (oh * on_ref[...] * jax.nn.silu(r_ref[...])).astype(y_ref.dtype)

    return pl.pallas_call(
        body, grid=(H, nc), in_specs=_gla_specs(H, DK, DV, lambda h, c: c) + [pl.BlockSpec(memory_space=pl.ANY)] * n_d,
        out_specs=[pl.BlockSpec((CHUNK, DV), lambda h, c: (c, h)),
                   pl.BlockSpec((None, None, DV, DK), lambda h, c: (h, c, 0, 0))],
        out_shape=[jax.ShapeDtypeStruct((L, H * DV), BF16), jax.ShapeDtypeStruct((H, nc, DV, DK), F32)],
        scratch_shapes=[pltpu.VMEM((DV, DK), F32)],
        compiler_params=_params(("arbitrary", "arbitrary"), VMEM_LIMIT), name=name,
    )(proj, proj, proj, proj, proj, wgu, bg, on, *deps)


def _gla_bwd(proj, wgu, bg, on, st, dy, *, H, DK, DV, name):
    L = proj.shape[0]
    nc = L // CHUNK
    scale = DK ** -0.5
    rev = lambda h, c: nc - 1 - c

    def body(q_ref, k_ref, v_ref, r_ref, g_ref, wgu_ref, bg_ref, on_ref, sc_ref, sp_ref, dy_ref,
             dq_ref, dk_ref, dv_ref, dr_ref, dpre_ref, don_ref, G, decn):
        h, c = pl.program_id(0), pl.program_id(1)

        @pl.when(c == 0)
        def _():
            G[...] = jnp.zeros_like(G)
            decn[...] = jnp.zeros_like(decn)

        @pl.when((c == 0) & (h == 0))
        def _():
            don_ref[...] = jnp.zeros_like(don_ref)

        pre, cum, total = _gla_gate(g_ref, wgu_ref, bg_ref)
        ex = jnp.exp(total - cum)
        k, v, r = k_ref[...], v_ref[...], r_ref[...]
        kd = k * ex
        dec = jnp.exp(total)
        qs = q_ref[...] * scale
        Sc = sc_ref[...]
        o = _bdot(qs, Sc, "nt")
        rinv, oh = _rms_parts(o)
        gn = on_ref[...]
        sg = jax.nn.sigmoid(r)
        dyv = dy_ref[...]
        d_on = dyv * (r * sg)
        dr_ref[...] = (dyv * (oh * gn) * (sg * (1.0 + r * (1.0 - sg)))).astype(dr_ref.dtype)
        don_ref[...] += jnp.sum(d_on * oh, axis=0, keepdims=True)
        dxh = d_on * gn
        do = rinv * (dxh - oh * jnp.mean(dxh * oh, axis=-1, keepdims=True))
        dq_ref[...] = (_bdot(do, Sc, "nn") * scale).astype(dq_ref.dtype)
        Gt = G[...] * decn[...] + _bdot(do, qs, "tn")
        G[...] = Gt
        decn[...] = dec
        dkd = _bdot(v, Gt, "nn")
        dv_ref[...] = _bdot(kd, Gt, "nt").astype(dv_ref.dtype)
        Sp = sp_ref[...] * jnp.where(c == nc - 1, 0.0, 1.0)
        ddec = jnp.sum(Gt * Sp, axis=0, keepdims=True)
        dk_ref[...] = (dkd * ex).astype(dk_ref.dtype)
        dla = ddec * dec + _dot(_tri(CHUNK, True), dkd * kd, "nn", lax.Precision.HIGHEST)
        dpre_ref[...] = dla * (1.0 / GLA_TEMP) * jax.nn.sigmoid(-pre)

    in_specs = _gla_specs(H, DK, DV, rev) + [
        pl.BlockSpec((None, None, DV, DK), lambda h, c: (h, rev(h, c), 0, 0)),
        pl.BlockSpec((None, None, DV, DK), lambda h, c: (h, jnp.maximum(rev(h, c) - 1, 0), 0, 0)),
        pl.BlockSpec((CHUNK, DV), lambda h, c: (rev(h, c), h)),
    ]
    blk_k = pl.BlockSpec((CHUNK, DK), lambda h, c: (rev(h, c), h))
    blk_v = pl.BlockSpec((CHUNK, DV), lambda h, c: (rev(h, c), h))
    out_specs = [blk_k, blk_k, blk_v, blk_v, blk_k, pl.BlockSpec((1, DV), lambda h, c: (0, 0))]
    sds = jax.ShapeDtypeStruct
    out_shape = [sds((L, H * DK), BF16), sds((L, H * DK), BF16), sds((L, H * DV), BF16), sds((L, H * DV), BF16),
                 sds((L, H * DK), F32), sds((1, DV), F32)]
    dq, dk, dv, dr, dpre, don = pl.pallas_call(
        body, grid=(H, nc), in_specs=in_specs, out_specs=out_specs, out_shape=out_shape,
        scratch_shapes=[pltpu.VMEM((DV, DK), F32), pltpu.VMEM((1, DK), F32)],
        compiler_params=_params(("arbitrary", "arbitrary"), VMEM_LIMIT), name=name,
    )(proj, proj, proj, proj, proj, wgu, bg, on, st, st, dy)
    dqkvr = jnp.concatenate([dq, dk, dv, dr], axis=1)
    return dqkvr, dpre, don


def _s5_param_fn(lam_re, lam_im, log_dt, brt, bit):
    lr = jnp.minimum(lam_re, S5_EIG_CLIP)
    li = lam_im
    dt = jnp.exp(log_dt)
    mag = jnp.exp(lr * dt)
    ang = li * dt
    ab_re = mag * jnp.cos(ang)
    ab_im = mag * jnp.sin(ang)
    den = lr * lr + li * li
    nr = ab_re - 1.0
    f_re = (nr * lr + ab_im * li) / den
    f_im = (ab_im * lr - nr * li) / den
    return ab_re, ab_im, f_re * brt - f_im * bit, f_re * bit + f_im * brt


def _s5_params(lam_re, lam_im, log_dt, brt, bit, name):
    G, _, N = lam_re.shape

    def body(lr_ref, li_ref, dt_ref, br_ref, bi_ref, ar_ref, ai_ref, bbr_ref, bbi_ref, pr_ref, pi_ref, a124_ref):
        lre, lim, ldt = lr_ref[...], li_ref[...], dt_ref[...]
        ar, ai, bbr, bbi = _s5_param_fn(lre, lim, ldt, br_ref[...], bi_ref[...])
        ar_ref[...], ai_ref[...], bbr_ref[...], bbi_ref[...] = ar, ai, bbr, bbi
        kk = (lax.broadcasted_iota(jnp.int32, (1, SUBLANE, 1), 1) + 1).astype(F32)
        dt = jnp.exp(ldt)
        mag = jnp.exp(kk * (jnp.minimum(lre, S5_EIG_CLIP) * dt))
        ang = kk * (lim * dt)
        pr_ref[...] = mag * jnp.cos(ang)
        pi_ref[...] = mag * jnp.sin(ang)
        r = lax.broadcasted_iota(jnp.int32, (1, SUBLANE, 1), 1)
        k2 = jnp.where(r < 2, 1.0, jnp.where(r < 4, 2.0, jnp.where(r < 6, 4.0, 0.0)))
        mag2 = jnp.exp(k2 * (jnp.minimum(lre, S5_EIG_CLIP) * dt))
        ang2 = k2 * (lim * dt)
        a124_ref[...] = mag2 * jnp.where(r % 2 == 0, jnp.cos(ang2), jnp.sin(ang2))

    sds = jax.ShapeDtypeStruct
    return pl.pallas_call(
        body, out_shape=[sds((G, 1, N), F32), sds((G, 1, N), F32), sds(brt.shape, F32), sds(brt.shape, F32),
                         sds((G, SUBLANE, N), F32), sds((G, SUBLANE, N), F32), sds((G, SUBLANE, N), F32)], name=name,
    )(lam_re, lam_im, log_dt, brt, bit)


def _s5_params_bwd(lam_re, lam_im, log_dt, brt, bit, dar, dai, dbbr, dbbi, name):
    def body(lr_ref, li_ref, dt_ref, br_ref, bi_ref, dar_ref, dai_ref, dbbr_ref, dbbi_ref, *outs):
        _, vjp = jax.vjp(_s5_param_fn, lr_ref[...], li_ref[...], dt_ref[...], br_ref[...], bi_ref[...])
        for o_ref, val in zip(outs, vjp((dar_ref[...], dai_ref[...], dbbr_ref[...], dbbi_ref[...]))):
            o_ref[...] = val

    ins = (lam_re, lam_im, log_dt, brt, bit)
    return pl.pallas_call(body, out_shape=[jax.ShapeDtypeStruct(a.shape, F32) for a in ins], name=name)(
        *ins, dar, dai, dbbr, dbbi)


def _s5_scan(br, bi, a124, pr, pi, *, reverse, name, W=512):
    L, n = br.shape
    W = _tile(n, W)
    nblk = L // SUBLANE

    def body(br_ref, bi_ref, a_ref, pr_ref, pi_ref, xr_ref, xi_ref):
        A = a_ref[...]
        PR, PI = pr_ref[...], pi_ref[...]
        row = lax.broadcasted_iota(jnp.int32, (SUBLANE, W), 0)
        last = 0 if reverse else SUBLANE - 1
        mult = []
        for j, k in enumerate((1, 2, 4)):
            keep = (row < SUBLANE - k) if reverse else (row >= k)
            mult.append((jnp.where(keep, A[2 * j:2 * j + 1, :], 0.0), jnp.where(keep, A[2 * j + 1:2 * j + 2, :], 0.0)))

        def step(i, carry):
            cr, ci = carry
            off = pl.multiple_of(((nblk - 1 - i) if reverse else i) * SUBLANE, SUBLANE)
            xr, xi = br_ref[pl.ds(off, SUBLANE), :], bi_ref[pl.ds(off, SUBLANE), :]
            for k, (ar, ai) in zip((1, 2, 4), mult):
                shift = (SUBLANE - k) if reverse else k
                sr, si = pltpu.roll(xr, shift, 0), pltpu.roll(xi, shift, 0)
                xr, xi = xr + ar * sr - ai * si, xi + ar * si + ai * sr
            xr, xi = xr + PR * cr - PI * ci, xi + PR * ci + PI * cr
            xr_ref[pl.ds(off, SUBLANE), :] = xr
            xi_ref[pl.ds(off, SUBLANE), :] = xi
            return xr[last:last + 1, :], xi[last:last + 1, :]

        z = jnp.zeros((1, W), F32)
        lax.fori_loop(0, nblk, step, (z, z))

    col = pl.BlockSpec((L, W), lambda j: (0, j))
    par = pl.BlockSpec((SUBLANE, W), lambda j: (0, j))
    return pl.pallas_call(
        body, grid=(n // W,), in_specs=[col, col, par, par, par], out_specs=[col, col],
        out_shape=[jax.ShapeDtypeStruct((L, n), F32)] * 2,
        compiler_params=_params(("parallel",), VMEM_LIMIT), name=name,
    )(br, bi, a124, pr, pi)


def _s5_dabar(lr, li, xr, xi, name, W=256):
    L, n = lr.shape
    W = _tile(n, W)

    def body(lr_ref, li_ref, xr_ref, xi_ref, dar_ref, dai_ref):
        first = lax.broadcasted_iota(jnp.int32, (L, W), 0) == 0
        pr = jnp.where(first, 0.0, pltpu.roll(xr_ref[...], 1, 0))
        pi = jnp.where(first, 0.0, pltpu.roll(xi_ref[...], 1, 0))
        a, b = lr_ref[...], li_ref[...]
        dar_ref[...] = jnp.sum(a * pr + b * pi, axis=0, keepdims=True)
        dai_ref[...] = jnp.sum(b * pr - a * pi, axis=0, keepdims=True)

    col = pl.BlockSpec((L, W), lambda j: (0, j))
    one = pl.BlockSpec((1, W), lambda j: (0, j))
    return pl.pallas_call(
        body, grid=(n // W,), in_specs=[col] * 4, out_specs=[one, one],
        out_shape=[jax.ShapeDtypeStruct((1, n), F32)] * 2,
        compiler_params=_params(("parallel",), VMEM_LIMIT), name=name,
    )(lr, li, xr, xi)


def _bd_all(mats, name, deps=()):
    n = len(mats)

    def body(*refs):
        for w_ref, o_ref in zip(refs[:n], refs[n + len(deps):]):
            G, A, B = w_ref.shape
            rows, cols = S5_GB * A, S5_GB * B
            tile = jnp.where(lax.broadcasted_iota(jnp.int32, (B, cols), 1) % B == lax.broadcasted_iota(jnp.int32, (B, cols), 0),
                             1.0, 0.0).astype(BF16)
            keep = (lax.broadcasted_iota(jnp.int32, (rows, cols), 0) // A) == (lax.broadcasted_iota(jnp.int32, (rows, cols), 1) // B)
            for kb in range(G // S5_GB):
                w2 = w_ref[kb * S5_GB:(kb + 1) * S5_GB].reshape(rows, B)
                o_ref[kb] = jnp.where(keep, _bdot(w2, tile, "nn"), 0.0).astype(o_ref.dtype)

    return pl.pallas_call(
        body, out_shape=[jax.ShapeDtypeStruct((m.shape[0] // S5_GB, S5_GB * m.shape[1], S5_GB * m.shape[2]), BF16) for m in mats],
        compiler_params=pltpu.CompilerParams(vmem_limit_bytes=VMEM_LIMIT), name=name,
    )(*mats, *deps)


def _bd_extract(m, A, B):
    nb = m.shape[0]
    d = jnp.diagonal(m.reshape(nb, S5_GB, A, S5_GB, B), axis1=1, axis2=3)
    return jnp.moveaxis(d, 3, 1).reshape(nb * S5_GB, A, B)


def _place():
    return lax.axis_index("x"), lax.axis_index("y"), lax.axis_index("c")


def _slot(p):
    return 4 * p[0] + 2 * p[1] + p[2]


N_CHIP = N_DEV // 2
_HBM = pl.BlockSpec(memory_space=pltpu.HBM)
_SEM = pl.BlockSpec(memory_space=pltpu.SEMAPHORE)
_ANY = pl.BlockSpec(memory_space=pl.ANY)
_EFFECT = pltpu.SideEffectType.DATAFLOW_SIDE_EFFECTING


def _gather_routes(x, y, c):
    me = _slot((x, y, c))
    peers = [(x, y, 1 - c)] + [(px, py, c) for px, py in ((1 - x, y), (x, 1 - y), (1 - x, 1 - y))]
    return [(p, me, me, _slot(p)) for p in peers]


def _chip_routes(x, y, c):
    myq = 2 * x + y
    return [((px, py, c), 2 * px + py, myq, 2 * px + py) for px, py in ((1 - x, y), (x, 1 - y), (1 - x, 1 - y))]


def _pass_routes(x, y, c):
    chips = ((1 - x, y), (x, 1 - y), (1 - x, 1 - y))
    return [((x, y, 1 - c), _slot((px, py, c)), _slot((px, py, c)), _slot((px, py, 1 - c))) for px, py in chips]


def _split_copies(s_refs, l_refs, send_sems, recv_sems, routes, arrival):
    out = []
    rts = routes(*_place())
    for a in range(len(l_refs)):
        for k, (peer, src_slot, dst_slot, arr_slot) in enumerate(rts):
            s_ref = l_refs[a] if s_refs is None else s_refs[a]
            src = s_ref if src_slot is None else s_ref.at[src_slot]
            sem = a * len(rts) + k
            out.append(pltpu.make_async_remote_copy(
                src_ref=src, dst_ref=l_refs[a].at[arr_slot if arrival else dst_slot], send_sem=send_sems.at[sem],
                recv_sem=recv_sems.at[sem], device_id=peer, device_id_type=MESH))
    return out


def _split_start(srcs, lands, routes, n_routes, dep, name):
    n, ns = len(lands), 0 if srcs is None else len(srcs)
    bufs = [*(srcs or ()), *lands]

    def body(*refs):
        s_refs = None if srcs is None else refs[:ns]
        for cp in _split_copies(s_refs, refs[ns:ns + n], refs[ns + n + 1], refs[ns + n + 2], routes, False):
            cp.start()
        refs[-1][...] = jnp.zeros_like(refs[-1])

    sems = pltpu.SemaphoreType.DMA((n * n_routes,))
    res = pl.pallas_call(
        body, name=name,
        out_shape=(sems, sems, *[pltpu.HBM(a.shape, a.dtype) for a in bufs], jax.ShapeDtypeStruct((SUBLANE, LANE), F32)),
        in_specs=[_HBM] * (ns + n) + [_ANY], out_specs=(_SEM, _SEM, *[_HBM] * (ns + n), pl.BlockSpec(memory_space=pltpu.VMEM)),
        input_output_aliases={i: 2 + i for i in range(ns + n)},
        compiler_params=pltpu.CompilerParams(has_side_effects=_EFFECT),
    )(*[pltpu.with_memory_space_constraint(a, pltpu.HBM) for a in bufs], dep)
    return res[0], res[1], (None if srcs is None else list(res[2:2 + ns])), list(res[2 + ns:2 + ns + n]), res[-1]


def _split_wait(send_sems, recv_sems, srcs, lands, routes, after, name):
    n, ns = len(lands), 0 if srcs is None else len(srcs)
    bufs = [*(srcs or ()), *lands]

    def body(*refs):
        s_refs = None if srcs is None else refs[:ns]
        for cp in _split_copies(s_refs, refs[ns:ns + n], refs[ns + n], refs[ns + n + 1], routes, True):
            cp.wait_send()
            cp.wait_recv()

    res = pl.pallas_call(
        body, name=name, out_shape=[pltpu.HBM(a.shape, a.dtype) for a in bufs],
        in_specs=[_HBM] * (ns + n) + [_SEM, _SEM, _ANY], out_specs=[_HBM] * (ns + n),
        input_output_aliases={i: i for i in range(ns + n)},
        compiler_params=pltpu.CompilerParams(has_side_effects=_EFFECT),
    )(*bufs, send_sems, recv_sems, after)
    return (None if srcs is None else list(res[:ns])), list(res[ns:])


def _to_slot(w, layer, place, out_dtype, name):
    _, R, C = w.shape
    unit = SUBLANE * (4 // jnp.dtype(out_dtype).itemsize)
    tr = _tile(R, max(unit, (512 * 1024 // C) // unit * unit), unit)

    def body(p_ref, w_ref, o_ref):
        o_ref[...] = w_ref[...].astype(o_ref.dtype)

    return pl.pallas_call(
        body,
        grid_spec=pltpu.PrefetchScalarGridSpec(
            num_scalar_prefetch=1, grid=(R // tr,),
            in_specs=[pl.BlockSpec((None, tr, C), lambda i, p: (layer, i, 0))],
            out_specs=pl.BlockSpec((None, tr, C), lambda i, p: (2 * p[1] + p[0], i, 0))),
        out_shape=jax.ShapeDtypeStruct((N_DEV, R, C), out_dtype),
        compiler_params=_params(("parallel",), VMEM_LIMIT), name=name,
    )(place, w)


def _pair_routes(x, y, c):
    return [((x, y, 1 - c), 2 * q + (1 - c), q, q) for q in range(N_CHIP)]


def _pair_sum(x, got, place, name):
    _, R, C = x.shape
    unit = SUBLANE * (4 // x.dtype.itemsize)
    tr = _tile(R, max(unit, (512 * 1024 // C) // unit * unit), unit)

    def body(p_ref, x_ref, g_ref, o_ref, land_ref):
        s = (x_ref[...].astype(F32) + g_ref[...].astype(F32)).astype(o_ref.dtype)
        o_ref[...] = s

        @pl.when(pl.program_id(1) == p_ref[1])
        def _():
            land_ref[...] = s

    blk = lambda f: pl.BlockSpec((None, tr, C), f)
    return pl.pallas_call(
        body,
        grid_spec=pltpu.PrefetchScalarGridSpec(
            num_scalar_prefetch=1, grid=(R // tr, N_CHIP),
            in_specs=[blk(lambda i, q, p: (2 * q + p[0], i, 0)), blk(lambda i, q, p: (q, i, 0))],
            out_specs=[blk(lambda i, q, p: (q, i, 0)), blk(lambda i, q, p: (p[1], i, 0))]),
        out_shape=[jax.ShapeDtypeStruct(got.shape, x.dtype)] * 2,
        compiler_params=_params(("parallel", "arbitrary"), VMEM_LIMIT), name=name,
    )(place, x, got)


def _adamw_math(w, g, m, v):
    m = ADAM_B1 * m + (1.0 - ADAM_B1) * g
    v = ADAM_B2 * v + (1.0 - ADAM_B2) * (g * g)
    m_hat = m / (1.0 - ADAM_B1 ** ADAM_STEP)
    v_hat = v / (1.0 - ADAM_B2 ** ADAM_STEP)
    return -ADAM_LR * (m_hat / (jnp.sqrt(v_hat) + ADAM_EPS) + ADAM_WD * w), m, v


def _adamw(w, m, v, parts, layer, prev, name):
    nl, R, C = w.shape
    P = parts.shape[0]
    unit = SUBLANE * (4 // parts.dtype.itemsize)
    tr = _tile(R, max(unit, (128 * 1024 // C) // unit * unit), unit)

    def body(w_ref, m_ref, v_ref, p_ref, *rest):
        g_ref, d_ref, nm_ref, nv_ref = rest[-4:]
        g = p_ref[0].astype(F32)
        for p in range(1, P):
            g = g + p_ref[p].astype(F32)
        d, nm, nv = _adamw_math(w_ref[...], g, m_ref[...], v_ref[...])
        g_ref[...], d_ref[...], nm_ref[...], nv_ref[...] = g, d, nm, nv

    lay = pl.BlockSpec((None, tr, C), lambda i: (layer, i, 0))
    in_specs = [lay, lay, lay, pl.BlockSpec((P, tr, C), lambda i: (0, i, 0))]
    ins = [w, m, v, parts]
    aliases = {}
    if prev is not None:
        in_specs += [pl.BlockSpec(memory_space=pl.ANY)] * 4
        ins += list(prev)
        aliases = {4 + k: k for k in range(4)}
    return pl.pallas_call(
        body, grid=(R // tr,), in_specs=in_specs, out_specs=[lay] * 4,
        out_shape=[jax.ShapeDtypeStruct(w.shape, F32)] * 4, input_output_aliases=aliases,
        compiler_params=_params(("parallel",), VMEM_LIMIT), name=name,
    )(*ins)


def _gelu_tanh(x):
    return jax.nn.gelu(x, approximate=True)


def _pack(arrs):
    tile = SUBLANE * LANE
    out = []
    for a in arrs:
        f = a.reshape(-1).astype(F32)
        out.append(jnp.pad(f, (0, (-f.shape[0]) % tile)))
    return jnp.concatenate(out)


def _unpack(flat, shapes):
    tile = SUBLANE * LANE
    out, off = [], 0
    for s in shapes:
        n = math.prod(s)
        out.append(flat[off:off + n].reshape(s))
        off += n + (-n) % tile
    return out


def kernel(x, gla_norm, gla_w_in, gla_w_gate_up, gla_b_gate, gla_o_norm, gla_w_out, s5_norm, s5_w_in, s5_lam_re, s5_lam_im, s5_log_dt, s5_b_re, s5_b_im, s5_c_re, s5_c_im, s5_d, s5_w_out, mlp_norm, mlp_w_up, mlp_w_down, final_norm, loss_target, m_gla_norm, m_gla_w_in, m_gla_w_gate_up, m_gla_b_gate, m_gla_o_norm, m_gla_w_out, m_s5_norm, m_s5_w_in, m_s5_lam_re, m_s5_lam_im, m_s5_log_dt, m_s5_b_re, m_s5_b_im, m_s5_c_re, m_s5_c_im, m_s5_d, m_s5_w_out, m_mlp_norm, m_mlp_w_up, m_mlp_w_down, m_final_norm, v_gla_norm, v_gla_w_in, v_gla_w_gate_up, v_gla_b_gate, v_gla_o_norm, v_gla_w_out, v_s5_norm, v_s5_w_in, v_s5_lam_re, v_s5_lam_im, v_s5_log_dt, v_s5_b_re, v_s5_b_im, v_s5_c_re, v_s5_c_im, v_s5_d, v_s5_w_out, v_mlp_norm, v_mlp_w_up, v_mlp_w_down, v_final_norm):
    W = dict(gla_norm=gla_norm, gla_w_in=gla_w_in, gla_w_gate_up=gla_w_gate_up, gla_b_gate=gla_b_gate, gla_o_norm=gla_o_norm, gla_w_out=gla_w_out, s5_norm=s5_norm, s5_w_in=s5_w_in, s5_lam_re=s5_lam_re, s5_lam_im=s5_lam_im, s5_log_dt=s5_log_dt, s5_b_re=s5_b_re, s5_b_im=s5_b_im, s5_c_re=s5_c_re, s5_c_im=s5_c_im, s5_d=s5_d, s5_w_out=s5_w_out, mlp_norm=mlp_norm, mlp_w_up=mlp_w_up, mlp_w_down=mlp_w_down, final_norm=final_norm)
    M = dict(gla_norm=m_gla_norm, gla_w_in=m_gla_w_in, gla_w_gate_up=m_gla_w_gate_up, gla_b_gate=m_gla_b_gate, gla_o_norm=m_gla_o_norm, gla_w_out=m_gla_w_out, s5_norm=m_s5_norm, s5_w_in=m_s5_w_in, s5_lam_re=m_s5_lam_re, s5_lam_im=m_s5_lam_im, s5_log_dt=m_s5_log_dt, s5_b_re=m_s5_b_re, s5_b_im=m_s5_b_im, s5_c_re=m_s5_c_re, s5_c_im=m_s5_c_im, s5_d=m_s5_d, s5_w_out=m_s5_w_out, mlp_norm=m_mlp_norm, mlp_w_up=m_mlp_w_up, mlp_w_down=m_mlp_w_down, final_norm=m_final_norm)
    V = dict(gla_norm=v_gla_norm, gla_w_in=v_gla_w_in, gla_w_gate_up=v_gla_w_gate_up, gla_b_gate=v_gla_b_gate, gla_o_norm=v_gla_o_norm, gla_w_out=v_gla_w_out, s5_norm=v_s5_norm, s5_w_in=v_s5_w_in, s5_lam_re=v_s5_lam_re, s5_lam_im=v_s5_lam_im, s5_log_dt=v_s5_log_dt, s5_b_re=v_s5_b_re, s5_b_im=v_s5_b_im, s5_c_re=v_s5_c_re, s5_c_im=v_s5_c_im, s5_d=v_s5_d, s5_w_out=v_s5_w_out, mlp_norm=v_mlp_norm, mlp_w_up=v_mlp_w_up, mlp_w_down=v_mlp_w_down, final_norm=v_final_norm)
    names = list(W)
    big = ["gla_w_in", "gla_w_out", "s5_w_in", "s5_w_out", "mlp_w_up", "mlp_w_down"]
    small_sharded = {"gla_w_gate_up": 2, "s5_norm": 1, "s5_d": 1}
    small = [n for n in names if n not in big]

    _, L, D = x.shape
    n_gla, n_s5, depth = gla_norm.shape[0], s5_lam_re.shape[0], mlp_norm.shape[0]
    H = GLA_HEADS
    KW, VW = D // 2, D
    DK, DV = KW // H, VW // H
    IN = 2 * KW + 2 * VW + GLA_RANK
    INP = 2 * KW + 2 * VW + LANE
    SW = s5_lam_re.shape[1] * S5_GROUP
    G, N = s5_lam_re.shape[1], s5_lam_re.shape[2]
    nb = G // S5_GB
    dev = _slot(_place())

    place = jnp.stack([lax.axis_index("c"), 2 * lax.axis_index("x") + lax.axis_index("y")]).astype(jnp.int32)
    sm_sh = _pack([W[n] for n in small_sharded]).reshape(1, -1, LANE)
    groups = []
    for i in range(depth):
        j = i // 2
        groups.append([("gla_w_in", j), ("gla_w_out", j)] if i % 2 == 0 else [("s5_w_in", j), ("s5_w_out", j)])
        groups.append([("mlp_w_up", i), ("mlp_w_down", i)])
    groups[0] = [("small", 0)] + groups[0]
    tok = jnp.zeros((SUBLANE, LANE), F32)
    level1, level2, full = [], {}, {}
    for gi, keys in enumerate(groups):
        lands = [_to_slot(sm_sh, 0, place, F32, "own_small") if n == "small" else _to_slot(W[n], j, place, BF16, f"own_{n}_{j}")
                 for n, j in keys]
        send_sems, recv_sems, _, lands, tok = _split_start(None, lands, _gather_routes, 4, tok, f"gather_start_{gi}")
        level1.append((send_sems, recv_sems, lands))

    def arrive(gi, after):
        send_sems, recv_sems, lands = level1[gi]
        _, lands = _split_wait(send_sems, recv_sems, None, lands, _gather_routes, after, f"gather_wait_{gi}")
        send_sems, recv_sems, _, lands, token = _split_start(None, lands, _pass_routes, 3, after, f"gather_pass_{gi}")
        level2[gi] = (send_sems, recv_sems, lands)
        return token

    def fetch(gi, after):
        send_sems, recv_sems, lands = level2[gi]
        _, lands = _split_wait(send_sems, recv_sems, None, lands, _pass_routes, after, f"gather_done_{gi}")
        full.update(zip(groups[gi], lands))

    s5p = []
    for j in range(n_s5):
        lre, lim = s5_lam_re[j][:, None, :], s5_lam_im[j][:, None, :]
        ldt = s5_log_dt[j][:, None, None]
        brt, bit = jnp.swapaxes(s5_b_re[j], 1, 2), jnp.swapaxes(s5_b_im[j], 1, 2)
        ar, ai, bbr, bbi, pr, pi, a124 = _s5_params(lre, lim, ldt, brt, bit, f"s5{j}_params")
        flat = lambda t: jnp.swapaxes(t, 0, 1).reshape(t.shape[1], G * N)
        pr, pi, a124 = flat(pr), flat(pi), flat(a124)
        sw = lambda t: jnp.swapaxes(t, 1, 2)
        bd = _bd_all([bbr, bbi, sw(s5_c_re[j]), -sw(s5_c_im[j]), s5_c_re[j], -s5_c_im[j], sw(bbr), sw(bbi)], f"s5{j}_blockdiag",
                     (tok,) if j == n_s5 - 1 else ())
        s5p.append(dict(prm=(lre, lim, ldt, brt, bit), pr=pr, pi=pi, a124=a124, bd=dict(zip(
            ("bu_re", "bu_im", "cx_re", "cx_im", "w_re", "w_im", "du_re", "du_im"), bd))))

    tok = arrive(0, s5p[-1]["bd"]["bu_re"] if s5p else tok)
    fetch(0, tok)
    sm_all = full["small", 0]
    sm_parts = [_unpack(sm_all[d].reshape(-1), [W[n].shape for n in small_sharded]) for d in range(N_DEV)]
    wgu_full = jnp.concatenate([p[0] for p in sm_parts], axis=2)
    s5n_full = jnp.concatenate([p[1] for p in sm_parts], axis=1)
    s5d_full = jnp.concatenate([p[2] for p in sm_parts], axis=1)

    def gla_weights(j):
        w_in = full["gla_w_in", j]
        w_in = jnp.transpose(w_in, (1, 0, 2)).reshape(D, IN)
        w_in = jnp.pad(w_in, ((0, 0), (0, INP - IN)))
        w_out = full["gla_w_out", j].reshape(VW, D)
        wgu = jnp.pad(wgu_full[j], ((0, LANE - GLA_RANK), (0, 0))).astype(BF16)
        return w_in, w_out, wgu

    grads = {}
    h = x[0]
    saved = []

    for i in range(depth):
        j = i // 2
        if i > 0:
            fetch(2 * i, h)
        if i % 2 == 0:
            w_in, w_out, wgu = gla_weights(j)
            gn = gla_norm[j][None]
            hn = _rms_fwd(h, gn, f"gla{j}_norm")
            proj = _mm(hn, w_in, "nn", name=f"gla{j}_proj", tn=896)
            ahead = arrive(2 * i + 1, proj)
            bg, on = gla_b_gate[j][None], gla_o_norm[j][None]
            y, st = _gla_fwd(proj, wgu, bg, on, H=H, DK=DK, DV=DV, name=f"gla{j}_mix", deps=(ahead,))
            h_new = _mm(y, w_out, "nn", name=f"gla{j}_out", res=h)
            saved.append(("gla", dict(h=h, hn=hn, proj=proj, y=y, st=st, w_in=w_in, w_out=w_out, wgu=wgu, gn=gn, bg=bg, on=on)))
        else:
            w_in = full["s5_w_in", j].reshape(D, SW)
            w_out3 = full["s5_w_out", j]
            sn = s5n_full[j][None]
            hn = _rms_fwd(h, sn, f"s5{j}_norm")
            u = _mm(hn, w_in, "nn", name=f"s5{j}_in")
            sp = s5p[j]
            bur = _mm_bd(u, sp["bd"]["bu_re"], name=f"s5{j}_bu_re")
            bui = _mm_bd(u, sp["bd"]["bu_im"], name=f"s5{j}_bu_im")
            xr, xi = _s5_scan(bur, bui, sp["a124"], sp["pr"], sp["pi"], reverse=False, name=f"s5{j}_scan")
            ahead = arrive(2 * i + 1, xr)
            cx = _mm_bd(xr, sp["bd"]["cx_re"], name=f"s5{j}_cx_re", deps=(ahead,))
            cx = _mm_bd(xi, sp["bd"]["cx_im"], name=f"s5{j}_cx_im", res=cx)
            dsk = s5d_full[j][None]

            def act(cxv, uv, dv):
                ypre = cxv + dv * uv
                return ypre, _gelu_tanh(ypre)
            ypre, yg = _rowmap(act, [cx, u], [dsk], [(SW, F32), (SW, BF16)], name=f"s5{j}_act")
            z = _mm_nn_cb(yg, w_out3, name=f"s5{j}_out")

            def glu(zv, hv):
                return (hv + zv[:, :D] * jax.nn.sigmoid(zv[:, D:]),)
            h_new = _rowmap(glu, [z, h], [], [(D, F32)], name=f"s5{j}_glu")[0]
            saved.append(("s5", dict(h=h, hn=hn, u=u, xr=xr, xi=xi, ypre=ypre, yg=yg, z=z, w_in=w_in, w_out3=w_out3, sn=sn,
                                     dsk=dsk, **sp)))
        h = h_new
        fetch(2 * i + 1, h)
        w_up3 = full["mlp_w_up", i]
        w_down = full["mlp_w_down", i].reshape(4 * D, D)
        mn = mlp_norm[i][None]
        hn = _rms_fwd(h, mn, f"mlp{i}_norm")
        def sq_relu(zv):
            a = jnp.maximum(zv, 0.0)
            return zv, a * a
        z, s = _mm_nn_cb(hn, w_up3, name=f"mlp{i}_up", out_dtype=[F32, BF16], epi=sq_relu)
        ahead = (arrive(2 * i + 2, s),) if i + 1 < depth else ()
        h_new = _mm(s, w_down, "nn", name=f"mlp{i}_down", res=h, deps=ahead)
        saved.append(("mlp", dict(h=h, hn=hn, z=z, s=s, w_up3=w_up3, w_down=w_down, mn=mn)))
        h = h_new

    dh, sq, dfin = _loss_head(h, loss_target[0], final_norm[None], "loss_head")
    loss = lax.psum(0.5 * jnp.sum(sq) / D, ("x", "y", "c"))
    small_grads = {"final_norm": dfin[0]}
    big_parts = {n: {} for n in big}
    stacks = {n: {} for n in small if n != "final_norm"}
    flat3 = lambda a: a.reshape(a.shape[0], math.prod(a.shape[1:-1]), a.shape[-1])
    exchanges = []

    def pair_grads(keys):
        tag = "_".join(f"{n}{j}" for n, j in keys)
        parts8 = [flat3(big_parts[n][j]) for n, j in keys]
        lands = [lax.empty((N_CHIP, *p.shape[1:]), p.dtype) for p in parts8]
        send_sems, recv_sems, parts8, lands, token = _split_start(parts8, lands, _pair_routes, N_CHIP, place, f"grads_pair_{tag}")
        return (keys, tag, send_sems, recv_sems, parts8, lands), token

    def send_grads(pending, after):
        keys, tag, send_sems, recv_sems, parts8, lands = pending
        parts8, from_sib = _split_wait(send_sems, recv_sems, parts8, lands, _pair_routes, after, f"grads_paired_{tag}")
        sums, lands = zip(*[_pair_sum(p, g, place, f"grads_pair_sum_{n}{j}") for (n, j), p, g in zip(keys, parts8, from_sib)])
        send_sems, recv_sems, srcs, lands, token = _split_start(list(sums), list(lands), _chip_routes, 3, place, f"grads_start_{tag}")
        exchanges.append((keys, tag, send_sems, recv_sems, srcs, lands))
        return token

    carry = ()

    outs = {}
    s5_small = [n for n in small if n.startswith("s5_") and n not in small_sharded]

    def small_start(subset, tag):
        for n in subset:
            if n in stacks:
                small_grads[n] = jnp.stack([stacks[n][k] for k in range(len(stacks[n]))])
        part = _pack([small_grads[n] for n in subset]).reshape(1, -1, LANE)
        lands = [_to_slot(part, 0, place, F32, f"own_small_grads_{tag}")]
        send_sems, recv_sems, _, lands, _ = _split_start(None, lands, _gather_routes, 4, place, f"small_grads_start_{tag}")
        return subset, tag, send_sems, recv_sems, lands

    def small_finish(handle, after):
        subset, tag, send_sems, recv_sems, lands = handle
        _, lands = _split_wait(send_sems, recv_sems, None, lands, _gather_routes, after, f"small_grads_wait_{tag}")
        send_sems, recv_sems, _, lands, t = _split_start(None, lands, _pass_routes, 3, after, f"small_grads_pass_{tag}")
        _, lands = _split_wait(send_sems, recv_sems, None, lands, _pass_routes, t, f"small_grads_done_{tag}")
        parts_all = lands[0]
        rows = parts_all.shape[1]
        shapes = [small_grads[n].shape for n in subset]

        def full_layout(T):
            arrs = []
            for n, shp in zip(subset, shapes):
                if n in small_sharded:
                    ax = small_sharded[n]
                    arrs.append(lax.dynamic_update_slice_in_dim(jnp.zeros(shp, F32), T[n], dev * T[n].shape[ax], axis=ax))
                else:
                    arrs.append(T[n])
            return _pack(arrs).reshape(1, rows, LANE)

        res = _adamw(full_layout(W), full_layout(M), full_layout(V), parts_all, 0, None, f"adamw_small_{tag}")
        for key, flat in zip(("grad", "delta", "new_m", "new_v"), res):
            for n, a in zip(subset, _unpack(flat.reshape(-1), shapes)):
                if n in small_sharded:
                    ax = small_sharded[n]
                    a = lax.dynamic_slice_in_dim(a, dev * W[n].shape[ax], W[n].shape[ax], axis=ax)
                outs[key, n] = a
        return res[0]

    for idx in range(len(saved) - 1, -1, -1):
        kind, s = saved[idx]
        li = sum(1 for k, _ in saved[:idx] if k == kind)
        if kind == "mlp":
            def sq_relu_bwd(dsv, zv):
                return (dsv * 2.0 * jnp.maximum(zv, 0.0),)
            dz = _mm(dh, s["w_down"], "nt", name=f"mlp{li}_dz", out_dtype=BF16, epi=sq_relu_bwd, epi_ins=(s["z"],), deps=carry)
            big_parts["mlp_w_down"][li] = _mm(s["s"], dh, "tn", name=f"mlp{li}_dw_down", out_dtype=BF16).reshape(N_DEV, 4 * D // N_DEV, D)
            big_parts["mlp_w_up"][li] = _mm_tn_cbout(s["hn"], dz, N_DEV, name=f"mlp{li}_dw_up")
            pending, token = pair_grads([("mlp_w_down", li), ("mlp_w_up", li)])
            dhn = _mm_nt_cb(dz, s["w_up3"], name=f"mlp{li}_dhn")
            dh, dg = _rms_bwd(s["h"], dhn, dh, s["mn"], f"mlp{li}_dnorm", deps=(token,))
            stacks["mlp_norm"][li] = dg[0]
        elif kind == "gla":
            dyv = _mm(dh, s["w_out"], "nt", name=f"gla{li}_dy", deps=carry)
            big_parts["gla_w_out"][li] = _mm(s["y"], dh, "tn", name=f"gla{li}_dw_out", out_dtype=BF16).reshape(N_DEV, VW // N_DEV, D)
            dqkvr, dpre, don = _gla_bwd(s["proj"], s["wgu"], s["bg"], s["on"], s["st"], dyv, H=H, DK=DK, DV=DV, name=f"gla{li}_dmix")
            dglow = _mm(dpre, s["wgu"], "nt", name=f"gla{li}_dglow", out_dtype=BF16)
            glow = s["proj"][:, 2 * KW + 2 * VW:]
            dwgu = _mm(glow, dpre, "tn", name=f"gla{li}_dwgu")[:GLA_RANK]
            dbg = _rowmap(lambda t: (jnp.sum(t, axis=0, keepdims=True),), [dpre], [], [], [KW], name=f"gla{li}_dbg")[0]
            dproj = jnp.concatenate([dqkvr, dglow], axis=1)
            dw_in = _mm(s["hn"], dproj, "tn", name=f"gla{li}_dw_in", out_dtype=BF16, tn=896)[:, :IN]
            big_parts["gla_w_in"][li] = jnp.transpose(dw_in.reshape(D, N_DEV, IN // N_DEV), (1, 0, 2))
            pending, token = pair_grads([("gla_w_out", li), ("gla_w_in", li)])
            dhn = _mm(dproj, s["w_in"], "nt", name=f"gla{li}_dhn", tk=896)
            dh, dg = _rms_bwd(s["h"], dhn, dh, s["gn"], f"gla{li}_dnorm", deps=(token,))
            stacks["gla_norm"][li], stacks["gla_b_gate"][li], stacks["gla_o_norm"][li] = dg[0], dbg[0], don[0]
            stacks["gla_w_gate_up"][li] = dwgu
        else:
            def glu_bwd(zv, dhv, *_):
                a, sg = zv[:, :D], jax.nn.sigmoid(zv[:, D:])
                return (jnp.concatenate([dhv * sg, dhv * a * sg * (1.0 - sg)], axis=1),)
            dz = _rowmap(glu_bwd, [s["z"], dh], list(carry), [(2 * D, BF16)], name=f"s5{li}_dglu")[0]
            big_parts["s5_w_out"][li] = _mm_tn_cbout(s["yg"], dz, N_DEV, name=f"s5{li}_dw_out")
            dyg = _mm_nt_cb(dz, s["w_out3"], name=f"s5{li}_dyg")

            def act_bwd(dygv, ypv, uv, dv):
                _, vjp = jax.vjp(_gelu_tanh, ypv)
                dyp = vjp(dygv)[0]
                return dyp, dyp * dv, jnp.sum(dyp * uv, axis=0, keepdims=True)
            dyp, du, dd = _rowmap(act_bwd, [dyg, s["ypre"], s["u"]], [s["dsk"]], [(SW, F32), (SW, F32)], [SW], name=f"s5{li}_dact")
            wr = _mm_bd(dyp, s["bd"]["w_re"], name=f"s5{li}_w_re")
            wi = _mm_bd(dyp, s["bd"]["w_im"], name=f"s5{li}_w_im")
            a124c = s["a124"] * jnp.array([1, -1, 1, -1, 1, -1, 1, 1], F32)[:, None]
            lr_, li_ = _s5_scan(wr, wi, a124c, s["pr"][::-1], -s["pi"][::-1], reverse=True, name=f"s5{li}_dscan")
            du = _mm_bd(lr_, s["bd"]["du_re"], name=f"s5{li}_du_re", res=du)
            du = _mm_bd(li_, s["bd"]["du_im"], name=f"s5{li}_du_im", res=du)
            dbbr = _bd_extract(_mm_tn_bd(s["u"], lr_, nb, name=f"s5{li}_dbb_re"), S5_GROUP, N)
            dbbi = _bd_extract(_mm_tn_bd(s["u"], li_, nb, name=f"s5{li}_dbb_im"), S5_GROUP, N)
            dcr = _bd_extract(_mm_tn_bd(dyp, s["xr"], nb, name=f"s5{li}_dc_re"), S5_GROUP, N)
            dci = -_bd_extract(_mm_tn_bd(dyp, s["xi"], nb, name=f"s5{li}_dc_im"), S5_GROUP, N)
            dar, dai = _s5_dabar(lr_, li_, s["xr"], s["xi"], f"s5{li}_dabar")
            dlre, dlim, dldt, dbrt, dbit = _s5_params_bwd(*s["prm"], dar.reshape(G, 1, N), dai.reshape(G, 1, N), dbbr, dbbi,
                                                         f"s5{li}_dparams")
            big_parts["s5_w_in"][li] = _mm(s["hn"], du, "tn", name=f"s5{li}_dw_in", out_dtype=BF16).reshape(N_DEV, D // N_DEV, SW)
            pending, token = pair_grads([("s5_w_out", li), ("s5_w_in", li)])
            dhn = _mm(du, s["w_in"], "nt", name=f"s5{li}_dhn")
            dh, dg = _rms_bwd(s["h"], dhn, dh, s["sn"], f"s5{li}_dnorm", deps=(token,))
            stacks["s5_norm"][li], stacks["s5_d"][li] = dg[0], dd[0]
            stacks["s5_lam_re"][li], stacks["s5_lam_im"][li], stacks["s5_log_dt"][li] = dlre[:, 0], dlim[:, 0], dldt[:, 0, 0]
            stacks["s5_b_re"][li], stacks["s5_b_im"][li] = jnp.swapaxes(dbrt, 1, 2), jnp.swapaxes(dbit, 1, 2)
            stacks["s5_c_re"][li], stacks["s5_c_im"][li] = dcr, dci
            if li == 0:
                s5_handle = small_start(s5_small, "s5")
        carry = (send_grads(pending, dh),)
    grad_x = dh[None]

    rest = small_start([n for n in small if n not in s5_small], "rest")

    stacked = {n: None for n in big}
    as3 = lambda a: a.reshape(a.shape[0], math.prod(a.shape[1:-1]), a.shape[-1])

    def finish(exchange, after):
        keys, tag, e_send, e_recv, e_srcs, e_lands = exchange
        _, e_lands = _split_wait(e_send, e_recv, e_srcs, e_lands, _chip_routes, after, f"grads_wait_{tag}")
        for (n, j), parts in zip(keys, e_lands):
            stacked[n] = _adamw(as3(W[n]), as3(M[n]), as3(V[n]), parts, j, stacked[n], f"adamw_{n}_{j}")
        return stacked[keys[0][0]][0]

    t = small_finish(s5_handle, dh)
    for exchange in exchanges[:-1]:
        t = finish(exchange, dh)
    t = small_finish(rest, t)
    finish(exchanges[-1], t)
    for n in big:
        for key, a in zip(("grad", "delta", "new_m", "new_v"), stacked[n]):
            outs[key, n] = a.reshape(W[n].shape)

    return (loss, grad_x, *[outs["grad", n] for n in names], *[outs["delta", n] for n in names],
            *[outs["new_m", n] for n in names], *[outs["new_v", n] for n in names])
```

```python
import functools
import math

import jax
import jax.numpy as jnp
from jax import lax
from jax.experimental import pallas as pl
from jax.experimental.pallas import tpu as pltpu

F32, BF16 = jnp.float32, jnp.bfloat16
MESH = pl.DeviceIdType.MESH
N_DEV = 8
LANE = 128
SUBLANE = 8
VMEM_LIMIT = 48 * 1024 * 1024

EPS = 1e-6
CHUNK = 64
GLA_HEADS = 4
GLA_RANK = 16
GLA_TEMP = 16.0
S5_GROUP = 16
S5_STATE = 64
S5_EIG_CLIP = -1e-4
S5_GB = 16
ADAM_LR, ADAM_B1, ADAM_B2, ADAM_EPS, ADAM_WD, ADAM_STEP = 0.001, 0.9, 0.999, 1e-08, 0.01, 10

_CONTRACT = {"nn": ((1,), (0,)), "tn": ((0,), (0,)), "nt": ((1,), (1,))}


def _tile(n, pref, unit=LANE):
    if n <= pref:
        return n
    t = (pref // unit) * unit
    while t > unit and n % t:
        t -= unit
    assert n % t == 0, (n, pref, unit)
    return t


def _params(sem, vmem=None):
    return pltpu.CompilerParams(dimension_semantics=sem, vmem_limit_bytes=vmem)


def _dot(a, b, dims, precision=None):
    return lax.dot_general(a, b, (_CONTRACT[dims], ((), ())), preferred_element_type=F32, precision=precision)


def _bdot(a, b, dims):
    return _dot(a.astype(BF16), b.astype(BF16), dims)


def _mm_call(a, b, *, dims, grid, a_spec, b_spec, o_spec, out_shape, out_dtype, name, res=None, epi=None, epi_ins=(), deps=()):
    nk = grid[2]
    acc_shape = tuple(d for d in o_spec.block_shape if d is not None)
    if res is not None:
        epi, epi_ins = (lambda r, x: (r + x,)), (res,)
    single = not isinstance(out_dtype, (list, tuple))
    out_dtypes = [out_dtype] if single else list(out_dtype)
    n_e, n_o, n_d = len(epi_ins), len(out_dtypes), len(deps)

    def body(*refs):
        a_ref, b_ref = refs[:2]
        e_refs, o_refs = refs[2:2 + n_e], refs[2 + n_e + n_d:2 + n_e + n_d + n_o]

        def finish(r):
            vals = (r,) if epi is None else epi(r, *[e[...].astype(F32) for e in e_refs])
            for o_ref, v in zip(o_refs, vals):
                o_ref[...] = v.astype(o_ref.dtype)

        d = _bdot(a_ref[...], b_ref[...], dims)
        if nk == 1:
            finish(d)
            return
        acc = refs[-1]
        k = pl.program_id(2)

        @pl.when(k == 0)
        def _():
            acc[...] = d

        @pl.when((k > 0) & (k < nk - 1))
        def _():
            acc[...] += d

        @pl.when(k == nk - 1)
        def _():
            finish(acc[...] + d)

    outs = pl.pallas_call(
        body, grid=grid, in_specs=[a_spec, b_spec] + [o_spec] * n_e + [pl.BlockSpec(memory_space=pl.ANY)] * n_d,
        out_specs=[o_spec] * n_o, out_shape=[jax.ShapeDtypeStruct(out_shape, dt) for dt in out_dtypes],
        scratch_shapes=[] if nk == 1 else [pltpu.VMEM(acc_shape, F32)],
        compiler_params=_params(("parallel", "parallel", "arbitrary"), VMEM_LIMIT), name=name,
    )(a, b, *epi_ins, *deps)
    return outs[0] if single else outs


TM, TN, TK = 1024, 1024, 2048


def _mm(a, b, dims, *, name, out_dtype=F32, res=None, epi=None, epi_ins=(), deps=(), tm=TM, tn=TN, tk=TK):
    if dims == "tn":
        (K, M), (_, N) = a.shape, b.shape
    elif dims == "nn":
        (M, K), (_, N) = a.shape, b.shape
    else:
        (M, K), (N, _) = a.shape, b.shape
    tm, tn, tk = _tile(M, tm), _tile(N, tn), _tile(K, tk)
    a_spec = pl.BlockSpec((tk, tm), lambda i, j, k: (k, i)) if dims == "tn" else pl.BlockSpec((tm, tk), lambda i, j, k: (i, k))
    b_spec = pl.BlockSpec((tn, tk), lambda i, j, k: (j, k)) if dims == "nt" else pl.BlockSpec((tk, tn), lambda i, j, k: (k, j))
    o_spec = pl.BlockSpec((tm, tn), lambda i, j, k: (i, j))
    return _mm_call(a, b, dims=dims, grid=(M // tm, N // tn, K // tk), a_spec=a_spec, b_spec=b_spec, o_spec=o_spec,
                    out_shape=(M, N), out_dtype=out_dtype, name=name, res=res, epi=epi, epi_ins=epi_ins, deps=deps)


def _mm_nn_cb(a, b3, *, name, out_dtype=F32, epi=None, tm=TM, tn=TN, tk=TK):
    (M, K), (P, _, Ns) = a.shape, b3.shape
    tm, tn, tk = _tile(M, tm), _tile(Ns, tn), _tile(K, tk)
    npb = Ns // tn
    return _mm_call(a, b3, dims="nn", grid=(M // tm, P * npb, K // tk),
                    a_spec=pl.BlockSpec((tm, tk), lambda i, j, k: (i, k)),
                    b_spec=pl.BlockSpec((None, tk, tn), lambda i, j, k: (j // npb, k, j % npb)),
                    o_spec=pl.BlockSpec((tm, tn), lambda i, j, k: (i, j)),
                    out_shape=(M, P * Ns), out_dtype=out_dtype, name=name, epi=epi)


def _mm_nt_cb(a, b3, *, name, out_dtype=F32, tm=TM, tn=TN, tk=TK):
    (M, _), (P, N, Ns) = a.shape, b3.shape
    tm, tn, tk = _tile(M, tm), _tile(N, tn), _tile(Ns, tk)
    kpb = Ns // tk
    return _mm_call(a, b3, dims="nt", grid=(M // tm, N // tn, P * kpb),
                    a_spec=pl.BlockSpec((tm, tk), lambda i, j, k: (i, k)),
                    b_spec=pl.BlockSpec((None, tn, tk), lambda i, j, k: (k // kpb, j, k % kpb)),
                    o_spec=pl.BlockSpec((tm, tn), lambda i, j, k: (i, j)),
                    out_shape=(M, N), out_dtype=out_dtype, name=name)


def _mm_tn_cbout(a, b, parts, *, name, out_dtype=BF16, tm=TM, tn=TN, tk=TK):
    (K, M), (_, N) = a.shape, b.shape
    Ns = N // parts
    tm, tn, tk = _tile(M, tm), _tile(Ns, tn), _tile(K, tk)
    npb = Ns // tn
    return _mm_call(a, b, dims="tn", grid=(M // tm, parts * npb, K // tk),
                    a_spec=pl.BlockSpec((tk, tm), lambda i, j, k: (k, i)),
                    b_spec=pl.BlockSpec((tk, tn), lambda i, j, k: (k, j)),
                    o_spec=pl.BlockSpec((None, tm, tn), lambda i, j, k: (j // npb, i, j % npb)),
                    out_shape=(parts, M, Ns), out_dtype=out_dtype, name=name)


def _mm_bd(a, w3, *, name, out_dtype=F32, res=None, deps=(), tm=2048, tn=1024):
    (M, _), (nb, Kb, Nb) = a.shape, w3.shape
    tm, tn = _tile(M, tm), _tile(Nb, tn)
    npb = Nb // tn
    return _mm_call(a, w3, dims="nn", grid=(M // tm, nb * npb, 1),
                    a_spec=pl.BlockSpec((tm, Kb), lambda i, j, k: (i, j // npb)),
                    b_spec=pl.BlockSpec((None, Kb, tn), lambda i, j, k: (j // npb, 0, j % npb)),
                    o_spec=pl.BlockSpec((tm, tn), lambda i, j, k: (i, j)),
                    out_shape=(M, nb * Nb), out_dtype=out_dtype, name=name, res=res, deps=deps)


def _mm_tn_bd(a, b, nb, *, name, tk=1024):
    (K, MA), (_, NB) = a.shape, b.shape
    Ma, Nb = MA // nb, NB // nb
    tk = _tile(K, tk)
    return _mm_call(a, b, dims="tn", grid=(nb, 1, K // tk),
                    a_spec=pl.BlockSpec((tk, Ma), lambda i, j, k: (k, i)),
                    b_spec=pl.BlockSpec((tk, Nb), lambda i, j, k: (k, i)),
                    o_spec=pl.BlockSpec((None, Ma, Nb), lambda i, j, k: (i, 0, 0)),
                    out_shape=(nb, Ma, Nb), out_dtype=F32, name=name)


def _rowmap(fn, rows, consts, out_defs, red_defs=(), *, name, tr=256):
    L = rows[0].shape[0]
    widest = max([r.shape[1] for r in rows] + [n for n, _ in out_defs])
    tr = _tile(L, max(SUBLANE * 2, min(tr, 512 * 1024 // widest)), SUBLANE * 2)
    n_in, n_o, n_d = len(rows) + len(consts), len(out_defs), len(red_defs)

    def body(*refs):
        res = fn(*[r[...] for r in refs[:n_in]])
        res = res if isinstance(res, (tuple, list)) else (res,)
        outs = refs[n_in:]
        for o_ref, val in zip(outs[:n_o], res[:n_o]):
            o_ref[...] = val.astype(o_ref.dtype)
        if n_d:
            @pl.when(pl.program_id(0) == 0)
            def _():
                for o_ref in outs[n_o:]:
                    o_ref[...] = jnp.zeros_like(o_ref)
            for o_ref, val in zip(outs[n_o:], res[n_o:]):
                o_ref[...] += val

    in_specs = [pl.BlockSpec((tr, r.shape[1]), lambda i: (i, 0)) for r in rows]
    in_specs += [pl.BlockSpec(c.shape, lambda i, nd=c.ndim: (0,) * nd) for c in consts]
    out_specs = [pl.BlockSpec((tr, n), lambda i: (i, 0)) for n, _ in out_defs]
    out_specs += [pl.BlockSpec((1, n), lambda i: (0, 0)) for n in red_defs]
    out_shape = [jax.ShapeDtypeStruct((L, n), dt) for n, dt in out_defs]
    out_shape += [jax.ShapeDtypeStruct((1, n), F32) for n in red_defs]
    return pl.pallas_call(body, grid=(L // tr,), in_specs=in_specs, out_specs=out_specs, out_shape=out_shape,
                          compiler_params=_params(("arbitrary",), VMEM_LIMIT), name=name)(*rows, *consts)


def _rms_parts(x):
    r = lax.rsqrt(jnp.mean(x * x, axis=-1, keepdims=True) + EPS)
    return r, x * r


def _rms_fwd(h, g, name, deps=()):
    def fn(x, gg, *_):
        _, xh = _rms_parts(x)
        return (xh * gg,)
    return _rowmap(fn, [h], [g, *deps], [(h.shape[1], BF16)], name=name)[0]


def _rms_bwd(h, dhn, dh, g, name, deps=()):
    def fn(x, dy, dres, gg, *_):
        r, xh = _rms_parts(x)
        dxh = dy * gg
        dx = r * (dxh - xh * jnp.mean(dxh * xh, axis=-1, keepdims=True))
        return dres + dx, jnp.sum(dy * xh, axis=0, keepdims=True)
    D = h.shape[1]
    return _rowmap(fn, [h, dhn, dh], [g, *deps], [(D, F32)], [D], name=name)


def _loss_head(h, tgt, g, name):
    D = h.shape[1]

    def fn(x, t, gg):
        r, xh = _rms_parts(x)
        diff = xh * gg - t
        dy = diff * (1.0 / D)
        dxh = dy * gg
        dx = r * (dxh - xh * jnp.mean(dxh * xh, axis=-1, keepdims=True))
        return dx, jnp.sum(diff * diff, axis=0, keepdims=True), jnp.sum(dy * xh, axis=0, keepdims=True)
    return _rowmap(fn, [h, tgt], [g], [(D, F32)], [D, D], name=name)


def _tri(n, strict):
    r = lax.broadcasted_iota(jnp.int32, (n, n), 0)
    c = lax.broadcasted_iota(jnp.int32, (n, n), 1)
    return jnp.where((c < r) if strict else (c <= r), 1.0, 0.0).astype(F32)


def _gla_gate(g_ref, wgu_ref, bg_ref):
    pre = _bdot(g_ref[...], wgu_ref[...], "nn") + bg_ref[...]
    la = (jnp.minimum(pre, 0.0) - jnp.log(1.0 + jnp.exp(-jnp.abs(pre)))) * (1.0 / GLA_TEMP)
    cum = _dot(_tri(CHUNK, False), la, "nn", lax.Precision.HIGHEST)
    return pre, cum, cum[CHUNK - 1:CHUNK, :]


def _gla_specs(H, DK, DV, cmap):
    KW, VW = H * DK, H * DV
    assert (2 * KW) % VW == 0 and (2 * KW + 2 * VW) % LANE == 0
    vb, gb = (2 * KW) // VW, (2 * KW + 2 * VW) // LANE
    return [
        pl.BlockSpec((CHUNK, KW), lambda c: (cmap(c), 0)),
        pl.BlockSpec((CHUNK, KW), lambda c: (cmap(c), 1)),
        pl.BlockSpec((CHUNK, VW), lambda c: (cmap(c), vb)),
        pl.BlockSpec((CHUNK, VW), lambda c: (cmap(c), vb + 1)),
        pl.BlockSpec((CHUNK, LANE), lambda c: (cmap(c), gb)),
        pl.BlockSpec((LANE, KW), lambda c: (0, 0)),
        pl.BlockSpec((1, KW), lambda c: (0, 0)),
        pl.BlockSpec((1, DV), lambda c: (0, 0)),
    ]


def _gla_fwd(proj, wgu, bg, on, *, H, DK, DV, name, deps=()):
    L = proj.shape[0]
    nc = L // CHUNK
    scale = DK ** -0.5
    n_d = len(deps)

    def body(q_ref, k_ref, v_ref, r_ref, g_ref, wgu_ref, bg_ref, on_ref, *rest):
        y_ref, st_ref, S = rest[n_d:]

        @pl.when(pl.program_id(0) == 0)
        def _():
            S[...] = jnp.zeros_like(S)
        _, cum, total = _gla_gate(g_ref, wgu_ref, bg_ref)
        kd = k_ref[...] * jnp.exp(total - cum)
        dec = jnp.exp(total)
        for h in range(H):
            ks, vs = slice(h * DK, (h + 1) * DK), slice(h * DV, (h + 1) * DV)
            St = S[h] * dec[:, ks] + _bdot(v_ref[:, vs], kd[:, ks], "tn")
            S[h] = St
            st_ref[h] = St
            o = _bdot(q_ref[:, ks] * scale, St, "nt")
            _, oh = _rms_parts(o)
            y_ref[:, vs] = (oh * on_ref[...] * jax.nn.silu(r_ref[:, vs])).astype(y_ref.dtype)

    return pl.pallas_call(
        body, grid=(nc,), in_specs=_gla_specs(H, DK, DV, lambda c: c) + [pl.BlockSpec(memory_space=pl.ANY)] * n_d,
        out_specs=[pl.BlockSpec((CHUNK, H * DV), lambda c: (c, 0)),
                   pl.BlockSpec((H, None, DV, DK), lambda c: (0, c, 0, 0))],
        out_shape=[jax.ShapeDtypeStruct((L, H * DV), BF16), jax.ShapeDtypeStruct((H, nc, DV, DK), F32)],
        scratch_shapes=[pltpu.VMEM((H, DV, DK), F32)],
        compiler_params=_params(("arbitrary",), VMEM_LIMIT), name=name,
    )(proj, proj, proj, proj, proj, wgu, bg, on, *deps)


def _gla_bwd(proj, wgu, bg, on, st, dy, *, H, DK, DV, name):
    L = proj.shape[0]
    nc = L // CHUNK
    KW, VW = H * DK, H * DV
    scale = DK ** -0.5
    rev = lambda c: nc - 1 - c

    def body(q_ref, k_ref, v_ref, r_ref, g_ref, wgu_ref, bg_ref, on_ref, sc_ref, sp_ref, dy_ref,
             dp_ref, don_ref, dbg_ref, dwgu_ref, G, decn):
        c = pl.program_id(0)

        @pl.when(c == 0)
        def _():
            for ref in (G, decn, don_ref, dbg_ref, dwgu_ref):
                ref[...] = jnp.zeros_like(ref)

        pre, cum, total = _gla_gate(g_ref, wgu_ref, bg_ref)
        ex = jnp.exp(total - cum)
        kd = k_ref[...] * ex
        dec = jnp.exp(total)
        gn = on_ref[...]
        alive = jnp.where(c == nc - 1, 0.0, 1.0)
        don = jnp.zeros_like(gn)
        ddec, dkd = [], []
        for h in range(H):
            ks, vs = slice(h * DK, (h + 1) * DK), slice(h * DV, (h + 1) * DV)
            qs = q_ref[:, ks] * scale
            Sc = sc_ref[h]
            o = _bdot(qs, Sc, "nt")
            rinv, oh = _rms_parts(o)
            r = r_ref[:, vs]
            sg = jax.nn.sigmoid(r)
            dyv = dy_ref[:, vs]
            d_on = dyv * (r * sg)
            dp_ref[:, 2 * KW + VW + h * DV:2 * KW + VW + (h + 1) * DV] = (
                dyv * (oh * gn) * (sg * (1.0 + r * (1.0 - sg)))).astype(dp_ref.dtype)
            don = don + jnp.sum(d_on * oh, axis=0, keepdims=True)
            dxh = d_on * gn
            do = rinv * (dxh - oh * jnp.mean(dxh * oh, axis=-1, keepdims=True))
            dp_ref[:, ks] = (_bdot(do, Sc, "nn") * scale).astype(dp_ref.dtype)
            Gt = G[h] * decn[:, ks] + _bdot(do, qs, "tn")
            G[h] = Gt
            dkd.append(_bdot(v_ref[:, vs], Gt, "nn"))
            dp_ref[:, 2 * KW + h * DV:2 * KW + (h + 1) * DV] = _bdot(kd[:, ks], Gt, "nt").astype(dp_ref.dtype)
            ddec.append(jnp.sum(Gt * (sp_ref[h] * alive), axis=0, keepdims=True))
        don_ref[...] += don
        decn[...] = dec
        dkd, ddec = jnp.concatenate(dkd, axis=1), jnp.concatenate(ddec, axis=1)
        dp_ref[:, KW:2 * KW] = (dkd * ex).astype(dp_ref.dtype)
        dla = ddec * dec + _dot(_tri(CHUNK, True), dkd * kd, "nn", lax.Precision.HIGHEST)
        dpre = dla * (1.0 / GLA_TEMP) * jax.nn.sigmoid(-pre)
        dbg_ref[...] += jnp.sum(dpre, axis=0, keepdims=True)
        dwgu_ref[...] += _bdot(g_ref[...], dpre, "tn")
        dp_ref[:, 2 * KW + 2 * VW:] = _bdot(dpre, wgu_ref[...], "nt").astype(dp_ref.dtype)

    in_specs = _gla_specs(H, DK, DV, rev) + [
        pl.BlockSpec((H, None, DV, DK), lambda c: (0, rev(c), 0, 0)),
        pl.BlockSpec((H, None, DV, DK), lambda c: (0, jnp.maximum(rev(c) - 1, 0), 0, 0)),
        pl.BlockSpec((CHUNK, VW), lambda c: (rev(c), 0)),
    ]
    W = 2 * KW + 2 * VW + LANE
    whole = lambda shp: pl.BlockSpec(shp, lambda c: (0, 0))
    sds = jax.ShapeDtypeStruct
    return pl.pallas_call(
        body, grid=(nc,), in_specs=in_specs,
        out_specs=[pl.BlockSpec((CHUNK, W), lambda c: (rev(c), 0)), whole((1, DV)), whole((1, KW)), whole((LANE, KW))],
        out_shape=[sds((L, W), BF16), sds((1, DV), F32), sds((1, KW), F32), sds((LANE, KW), F32)],
        scratch_shapes=[pltpu.VMEM((H, DV, DK), F32), pltpu.VMEM((1, KW), F32)],
        compiler_params=_params(("arbitrary",), VMEM_LIMIT), name=name,
    )(proj, proj, proj, proj, proj, wgu, bg, on, st, st, dy)


def _s5_param_fn(lam_re, lam_im, log_dt, brt, bit):
    lr = jnp.minimum(lam_re, S5_EIG_CLIP)
    li = lam_im
    dt = jnp.exp(log_dt)
    mag = jnp.exp(lr * dt)
    ang = li * dt
    ab_re = mag * jnp.cos(ang)
    ab_im = mag * jnp.sin(ang)
    den = lr * lr + li * li
    nr = ab_re - 1.0
    f_re = (nr * lr + ab_im * li) / den
    f_im = (ab_im * lr - nr * li) / den
    return ab_re, ab_im, f_re * brt - f_im * bit, f_re * bit + f_im * brt


def _s5_params(lam_re, lam_im, log_dt, brt, bit, name):
    G, _, N = lam_re.shape

    def body(lr_ref, li_ref, dt_ref, br_ref, bi_ref, ar_ref, ai_ref, bbr_ref, bbi_ref, pr_ref, pi_ref, a124_ref):
        lre, lim, ldt = lr_ref[...], li_ref[...], dt_ref[...]
        ar, ai, bbr, bbi = _s5_param_fn(lre, lim, ldt, br_ref[...], bi_ref[...])
        ar_ref[...], ai_ref[...], bbr_ref[...], bbi_ref[...] = ar, ai, bbr, bbi
        kk = (lax.broadcasted_iota(jnp.int32, (1, SUBLANE, 1), 1) + 1).astype(F32)
        dt = jnp.exp(ldt)
        mag = jnp.exp(kk * (jnp.minimum(lre, S5_EIG_CLIP) * dt))
        ang = kk * (lim * dt)
        pr_ref[...] = mag * jnp.cos(ang)
        pi_ref[...] = mag * jnp.sin(ang)
        r = lax.broadcasted_iota(jnp.int32, (1, SUBLANE, 1), 1)
        k2 = jnp.where(r < 2, 1.0, jnp.where(r < 4, 2.0, jnp.where(r < 6, 4.0, 0.0)))
        mag2 = jnp.exp(k2 * (jnp.minimum(lre, S5_EIG_CLIP) * dt))
        ang2 = k2 * (lim * dt)
        a124_ref[...] = mag2 * jnp.where(r % 2 == 0, jnp.cos(ang2), jnp.sin(ang2))

    sds = jax.ShapeDtypeStruct
    return pl.pallas_call(
        body, out_shape=[sds((G, 1, N), F32), sds((G, 1, N), F32), sds(brt.shape, F32), sds(brt.shape, F32),
                         sds((G, SUBLANE, N), F32), sds((G, SUBLANE, N), F32), sds((G, SUBLANE, N), F32)], name=name,
    )(lam_re, lam_im, log_dt, brt, bit)


def _s5_params_bwd(lam_re, lam_im, log_dt, brt, bit, dar, dai, dbbr, dbbi, name):
    def body(lr_ref, li_ref, dt_ref, br_ref, bi_ref, dar_ref, dai_ref, dbbr_ref, dbbi_ref, *outs):
        _, vjp = jax.vjp(_s5_param_fn, lr_ref[...], li_ref[...], dt_ref[...], br_ref[...], bi_ref[...])
        for o_ref, val in zip(outs, vjp((dar_ref[...], dai_ref[...], dbbr_ref[...], dbbi_ref[...]))):
            o_ref[...] = val

    ins = (lam_re, lam_im, log_dt, brt, bit)
    return pl.pallas_call(body, out_shape=[jax.ShapeDtypeStruct(a.shape, F32) for a in ins], name=name)(
        *ins, dar, dai, dbbr, dbbi)


def _s5_scan(br, bi, a124, pr, pi, *, reverse, name, W=512):
    L, n = br.shape
    W = _tile(n, W)
    nblk = L // SUBLANE

    def body(br_ref, bi_ref, a_ref, pr_ref, pi_ref, xr_ref, xi_ref):
        A = a_ref[...]
        PR, PI = pr_ref[...], pi_ref[...]
        row = lax.broadcasted_iota(jnp.int32, (SUBLANE, W), 0)
        last = 0 if reverse else SUBLANE - 1
        mult = []
        for j, k in enumerate((1, 2, 4)):
            keep = (row < SUBLANE - k) if reverse else (row >= k)
            mult.append((jnp.where(keep, A[2 * j:2 * j + 1, :], 0.0), jnp.where(keep, A[2 * j + 1:2 * j + 2, :], 0.0)))

        def step(i, carry):
            cr, ci = carry
            off = pl.multiple_of(((nblk - 1 - i) if reverse else i) * SUBLANE, SUBLANE)
            xr, xi = br_ref[pl.ds(off, SUBLANE), :], bi_ref[pl.ds(off, SUBLANE), :]
            for k, (ar, ai) in zip((1, 2, 4), mult):
                shift = (SUBLANE - k) if reverse else k
                sr, si = pltpu.roll(xr, shift, 0), pltpu.roll(xi, shift, 0)
                xr, xi = xr + ar * sr - ai * si, xi + ar * si + ai * sr
            xr, xi = xr + PR * cr - PI * ci, xi + PR * ci + PI * cr
            xr_ref[pl.ds(off, SUBLANE), :] = xr
            xi_ref[pl.ds(off, SUBLANE), :] = xi
            return xr[last:last + 1, :], xi[last:last + 1, :]

        z = jnp.zeros((1, W), F32)
        lax.fori_loop(0, nblk, step, (z, z))

    col = pl.BlockSpec((L, W), lambda j: (0, j))
    par = pl.BlockSpec((SUBLANE, W), lambda j: (0, j))
    return pl.pallas_call(
        body, grid=(n // W,), in_specs=[col, col, par, par, par], out_specs=[col, col],
        out_shape=[jax.ShapeDtypeStruct((L, n), F32)] * 2,
        compiler_params=_params(("parallel",), VMEM_LIMIT), name=name,
    )(br, bi, a124, pr, pi)


def _s5_dabar(lr, li, xr, xi, name, W=256):
    L, n = lr.shape
    W = _tile(n, W)

    def body(lr_ref, li_ref, xr_ref, xi_ref, dar_ref, dai_ref):
        first = lax.broadcasted_iota(jnp.int32, (L, W), 0) == 0
        pr = jnp.where(first, 0.0, pltpu.roll(xr_ref[...], 1, 0))
        pi = jnp.where(first, 0.0, pltpu.roll(xi_ref[...], 1, 0))
        a, b = lr_ref[...], li_ref[...]
        dar_ref[...] = jnp.sum(a * pr + b * pi, axis=0, keepdims=True)
        dai_ref[...] = jnp.sum(b * pr - a * pi, axis=0, keepdims=True)

    col = pl.BlockSpec((L, W), lambda j: (0, j))
    one = pl.BlockSpec((1, W), lambda j: (0, j))
    return pl.pallas_call(
        body, grid=(n // W,), in_specs=[col] * 4, out_specs=[one, one],
        out_shape=[jax.ShapeDtypeStruct((1, n), F32)] * 2,
        compiler_params=_params(("parallel",), VMEM_LIMIT), name=name,
    )(lr, li, xr, xi)


def _bd_all(mats, name, deps=()):
    n = len(mats)

    def body(*refs):
        for w_ref, o_ref in zip(refs[:n], refs[n + len(deps):]):
            G, A, B = w_ref.shape
            rows, cols = S5_GB * A, S5_GB * B
            tile = jnp.where(lax.broadcasted_iota(jnp.int32, (B, cols), 1) % B == lax.broadcasted_iota(jnp.int32, (B, cols), 0),
                             1.0, 0.0).astype(BF16)
            keep = (lax.broadcasted_iota(jnp.int32, (rows, cols), 0) // A) == (lax.broadcasted_iota(jnp.int32, (rows, cols), 1) // B)
            for kb in range(G // S5_GB):
                w2 = w_ref[kb * S5_GB:(kb + 1) * S5_GB].reshape(rows, B)
                o_ref[kb] = jnp.where(keep, _bdot(w2, tile, "nn"), 0.0).astype(o_ref.dtype)

    return pl.pallas_call(
        body, out_shape=[jax.ShapeDtypeStruct((m.shape[0] // S5_GB, S5_GB * m.shape[1], S5_GB * m.shape[2]), BF16) for m in mats],
        compiler_params=pltpu.CompilerParams(vmem_limit_bytes=VMEM_LIMIT), name=name,
    )(*mats, *deps)


def _bd_extract(m, A, B):
    nb = m.shape[0]
    d = jnp.diagonal(m.reshape(nb, S5_GB, A, S5_GB, B), axis1=1, axis2=3)
    return jnp.moveaxis(d, 3, 1).reshape(nb * S5_GB, A, B)


def _place():
    return lax.axis_index("x"), lax.axis_index("y"), lax.axis_index("c")


def _slot(p):
    return 4 * p[0] + 2 * p[1] + p[2]


N_CHIP = N_DEV // 2
_HBM = pl.BlockSpec(memory_space=pltpu.HBM)
_SEM = pl.BlockSpec(memory_space=pltpu.SEMAPHORE)
_ANY = pl.BlockSpec(memory_space=pl.ANY)
_EFFECT = pltpu.SideEffectType.DATAFLOW_SIDE_EFFECTING


def _gather_routes(x, y, c):
    me = _slot((x, y, c))
    peers = [(x, y, 1 - c)] + [(px, py, c) for px, py in ((1 - x, y), (x, 1 - y), (1 - x, 1 - y))]
    return [(p, me, me, _slot(p)) for p in peers]


def _chip_routes(x, y, c):
    myq = 2 * x + y
    return [((px, py, c), 2 * px + py, myq, 2 * px + py) for px, py in ((1 - x, y), (x, 1 - y), (1 - x, 1 - y))]


def _pass_routes(x, y, c):
    chips = ((1 - x, y), (x, 1 - y), (1 - x, 1 - y))
    return [((x, y, 1 - c), _slot((px, py, c)), _slot((px, py, c)), _slot((px, py, 1 - c))) for px, py in chips]


def _split_copies(s_refs, l_refs, send_sems, recv_sems, routes, arrival):
    out = []
    rts = routes(*_place())
    for a in range(len(l_refs)):
        for k, (peer, src_slot, dst_slot, arr_slot) in enumerate(rts):
            s_ref = l_refs[a] if s_refs is None else s_refs[a]
            src = s_ref if src_slot is None else s_ref.at[src_slot]
            sem = a * len(rts) + k
            out.append(pltpu.make_async_remote_copy(
                src_ref=src, dst_ref=l_refs[a].at[arr_slot if arrival else dst_slot], send_sem=send_sems.at[sem],
                recv_sem=recv_sems.at[sem], device_id=peer, device_id_type=MESH))
    return out


def _split_start(srcs, lands, routes, n_routes, dep, name):
    n, ns = len(lands), 0 if srcs is None else len(srcs)
    bufs = [*(srcs or ()), *lands]

    def body(*refs):
        s_refs = None if srcs is None else refs[:ns]
        for cp in _split_copies(s_refs, refs[ns:ns + n], refs[ns + n + 1], refs[ns + n + 2], routes, False):
            cp.start()
        refs[-1][...] = jnp.zeros_like(refs[-1])

    sems = pltpu.SemaphoreType.DMA((n * n_routes,))
    res = pl.pallas_call(
        body, name=name,
        out_shape=(sems, sems, *[pltpu.HBM(a.shape, a.dtype) for a in bufs], jax.ShapeDtypeStruct((SUBLANE, LANE), F32)),
        in_specs=[_HBM] * (ns + n) + [_ANY], out_specs=(_SEM, _SEM, *[_HBM] * (ns + n), pl.BlockSpec(memory_space=pltpu.VMEM)),
        input_output_aliases={i: 2 + i for i in range(ns + n)},
        compiler_params=pltpu.CompilerParams(has_side_effects=_EFFECT),
    )(*[pltpu.with_memory_space_constraint(a, pltpu.HBM) for a in bufs], dep)
    return res[0], res[1], (None if srcs is None else list(res[2:2 + ns])), list(res[2 + ns:2 + ns + n]), res[-1]


def _split_wait(send_sems, recv_sems, srcs, lands, routes, after, name):
    n, ns = len(lands), 0 if srcs is None else len(srcs)
    bufs = [*(srcs or ()), *lands]
    after = list(after) if isinstance(after, (list, tuple)) else [after]

    def body(*refs):
        s_refs = None if srcs is None else refs[:ns]
        for cp in _split_copies(s_refs, refs[ns:ns + n], refs[ns + n], refs[ns + n + 1], routes, True):
            cp.wait_send()
            cp.wait_recv()

    res = pl.pallas_call(
        body, name=name, out_shape=[pltpu.HBM(a.shape, a.dtype) for a in bufs],
        in_specs=[_HBM] * (ns + n) + [_SEM, _SEM] + [_ANY] * len(after), out_specs=[_HBM] * (ns + n),
        input_output_aliases={i: i for i in range(ns + n)},
        compiler_params=pltpu.CompilerParams(has_side_effects=_EFFECT),
    )(*bufs, send_sems, recv_sems, *after)
    return (None if srcs is None else list(res[:ns])), list(res[ns:])


def _to_slot(w, layer, place, out_dtype, name):
    _, R, C = w.shape
    unit = SUBLANE * (4 // jnp.dtype(out_dtype).itemsize)
    tr = _tile(R, max(unit, (512 * 1024 // C) // unit * unit), unit)

    def body(p_ref, w_ref, o_ref):
        o_ref[...] = w_ref[...].astype(o_ref.dtype)

    return pl.pallas_call(
        body,
        grid_spec=pltpu.PrefetchScalarGridSpec(
            num_scalar_prefetch=1, grid=(R // tr,),
            in_specs=[pl.BlockSpec((None, tr, C), lambda i, p: (layer, i, 0))],
            out_specs=pl.BlockSpec((None, tr, C), lambda i, p: (2 * p[1] + p[0], i, 0))),
        out_shape=jax.ShapeDtypeStruct((N_DEV, R, C), out_dtype),
        compiler_params=_params(("parallel",), VMEM_LIMIT), name=name,
    )(place, w)


def _pair_routes(x, y, c):
    return [((x, y, 1 - c), 2 * q + (1 - c), q, q) for q in range(N_CHIP)]


def _pair_sum(x, got, place, name):
    _, R, C = x.shape
    unit = SUBLANE * (4 // x.dtype.itemsize)
    tr = _tile(R, max(unit, (512 * 1024 // C) // unit * unit), unit)

    def body(p_ref, x_ref, g_ref, o_ref, land_ref):
        s = (x_ref[...].astype(F32) + g_ref[...].astype(F32)).astype(o_ref.dtype)
        o_ref[...] = s

        @pl.when(pl.program_id(1) == p_ref[1])
        def _():
            land_ref[...] = s

    blk = lambda f: pl.BlockSpec((None, tr, C), f)
    return pl.pallas_call(
        body,
        grid_spec=pltpu.PrefetchScalarGridSpec(
            num_scalar_prefetch=1, grid=(R // tr, N_CHIP),
            in_specs=[blk(lambda i, q, p: (2 * q + p[0], i, 0)), blk(lambda i, q, p: (q, i, 0))],
            out_specs=[blk(lambda i, q, p: (q, i, 0)), blk(lambda i, q, p: (p[1], i, 0))]),
        out_shape=[jax.ShapeDtypeStruct(got.shape, x.dtype)] * 2,
        compiler_params=_params(("parallel", "arbitrary"), VMEM_LIMIT), name=name,
    )(place, x, got)


def _adamw_math(w, g, m, v):
    m = ADAM_B1 * m + (1.0 - ADAM_B1) * g
    v = ADAM_B2 * v + (1.0 - ADAM_B2) * (g * g)
    m_hat = m / (1.0 - ADAM_B1 ** ADAM_STEP)
    v_hat = v / (1.0 - ADAM_B2 ** ADAM_STEP)
    return -ADAM_LR * (m_hat / (jnp.sqrt(v_hat) + ADAM_EPS) + ADAM_WD * w), m, v


def _adamw(w, m, v, parts, layer, prev, name):
    nl, R, C = w.shape
    P = parts.shape[0]
    unit = SUBLANE * (4 // parts.dtype.itemsize)
    tr = _tile(R, max(unit, (128 * 1024 // C) // unit * unit), unit)

    def body(w_ref, m_ref, v_ref, p_ref, *rest):
        g_ref, d_ref, nm_ref, nv_ref = rest[-4:]
        g = p_ref[0].astype(F32)
        for p in range(1, P):
            g = g + p_ref[p].astype(F32)
        d, nm, nv = _adamw_math(w_ref[...], g, m_ref[...], v_ref[...])
        g_ref[...], d_ref[...], nm_ref[...], nv_ref[...] = g, d, nm, nv

    lay = pl.BlockSpec((None, tr, C), lambda i: (layer, i, 0))
    in_specs = [lay, lay, lay, pl.BlockSpec((P, tr, C), lambda i: (0, i, 0))]
    ins = [w, m, v, parts]
    aliases = {}
    if prev is not None:
        in_specs += [pl.BlockSpec(memory_space=pl.ANY)] * 4
        ins += list(prev)
        aliases = {4 + k: k for k in range(4)}
    return pl.pallas_call(
        body, grid=(R // tr,), in_specs=in_specs, out_specs=[lay] * 4,
        out_shape=[jax.ShapeDtypeStruct(w.shape, F32)] * 4, input_output_aliases=aliases,
        compiler_params=_params(("parallel",), VMEM_LIMIT), name=name,
    )(*ins)


def _gelu_tanh(x):
    return jax.nn.gelu(x, approximate=True)


def _pack(arrs):
    tile = SUBLANE * LANE
    out = []
    for a in arrs:
        f = a.reshape(-1).astype(F32)
        out.append(jnp.pad(f, (0, (-f.shape[0]) % tile)))
    return jnp.concatenate(out)


def _unpack(flat, shapes):
    tile = SUBLANE * LANE
    out, off = [], 0
    for s in shapes:
        n = math.prod(s)
        out.append(flat[off:off + n].reshape(s))
        off += n + (-n) % tile
    return out


def kernel(x, gla_norm, gla_w_in, gla_w_gate_up, gla_b_gate, gla_o_norm, gla_w_out, s5_norm, s5_w_in, s5_lam_re, s5_lam_im, s5_log_dt, s5_b_re, s5_b_im, s5_c_re, s5_c_im, s5_d, s5_w_out, mlp_norm, mlp_w_up, mlp_w_down, final_norm, loss_target, m_gla_norm, m_gla_w_in, m_gla_w_gate_up, m_gla_b_gate, m_gla_o_norm, m_gla_w_out, m_s5_norm, m_s5_w_in, m_s5_lam_re, m_s5_lam_im, m_s5_log_dt, m_s5_b_re, m_s5_b_im, m_s5_c_re, m_s5_c_im, m_s5_d, m_s5_w_out, m_mlp_norm, m_mlp_w_up, m_mlp_w_down, m_final_norm, v_gla_norm, v_gla_w_in, v_gla_w_gate_up, v_gla_b_gate, v_gla_o_norm, v_gla_w_out, v_s5_norm, v_s5_w_in, v_s5_lam_re, v_s5_lam_im, v_s5_log_dt, v_s5_b_re, v_s5_b_im, v_s5_c_re, v_s5_c_im, v_s5_d, v_s5_w_out, v_mlp_norm, v_mlp_w_up, v_mlp_w_down, v_final_norm):
    W = dict(gla_norm=gla_norm, gla_w_in=gla_w_in, gla_w_gate_up=gla_w_gate_up, gla_b_gate=gla_b_gate, gla_o_norm=gla_o_norm, gla_w_out=gla_w_out, s5_norm=s5_norm, s5_w_in=s5_w_in, s5_lam_re=s5_lam_re, s5_lam_im=s5_lam_im, s5_log_dt=s5_log_dt, s5_b_re=s5_b_re, s5_b_im=s5_b_im, s5_c_re=s5_c_re, s5_c_im=s5_c_im, s5_d=s5_d, s5_w_out=s5_w_out, mlp_norm=mlp_norm, mlp_w_up=mlp_w_up, mlp_w_down=mlp_w_down, final_norm=final_norm)
    M = dict(gla_norm=m_gla_norm, gla_w_in=m_gla_w_in, gla_w_gate_up=m_gla_w_gate_up, gla_b_gate=m_gla_b_gate, gla_o_norm=m_gla_o_norm, gla_w_out=m_gla_w_out, s5_norm=m_s5_norm, s5_w_in=m_s5_w_in, s5_lam_re=m_s5_lam_re, s5_lam_im=m_s5_lam_im, s5_log_dt=m_s5_log_dt, s5_b_re=m_s5_b_re, s5_b_im=m_s5_b_im, s5_c_re=m_s5_c_re, s5_c_im=m_s5_c_im, s5_d=m_s5_d, s5_w_out=m_s5_w_out, mlp_norm=m_mlp_norm, mlp_w_up=m_mlp_w_up, mlp_w_down=m_mlp_w_down, final_norm=m_final_norm)
    V = dict(gla_norm=v_gla_norm, gla_w_in=v_gla_w_in, gla_w_gate_up=v_gla_w_gate_up, gla_b_gate=v_gla_b_gate, gla_o_norm=v_gla_o_norm, gla_w_out=v_gla_w_out, s5_norm=v_s5_norm, s5_w_in=v_s5_w_in, s5_lam_re=v_s5_lam_re, s5_lam_im=v_s5_lam_im, s5_log_dt=v_s5_log_dt, s5_b_re=v_s5_b_re, s5_b_im=v_s5_b_im, s5_c_re=v_s5_c_re, s5_c_im=v_s5_c_im, s5_d=v_s5_d, s5_w_out=v_s5_w_out, mlp_norm=v_mlp_norm, mlp_w_up=v_mlp_w_up, mlp_w_down=v_mlp_w_down, final_norm=v_final_norm)
    names = list(W)
    big = ["gla_w_in", "gla_w_out", "s5_w_in", "s5_w_out", "mlp_w_up", "mlp_w_down"]
    small_sharded = {"gla_w_gate_up": 2, "s5_norm": 1, "s5_d": 1}
    small = [n for n in names if n not in big]

    _, L, D = x.shape
    n_gla, n_s5, depth = gla_norm.shape[0], s5_lam_re.shape[0], mlp_norm.shape[0]
    H = GLA_HEADS
    KW, VW = D // 2, D
    DK, DV = KW // H, VW // H
    IN = 2 * KW + 2 * VW + GLA_RANK
    INP = 2 * KW + 2 * VW + LANE
    SW = s5_lam_re.shape[1] * S5_GROUP
    G, N = s5_lam_re.shape[1], s5_lam_re.shape[2]
    nb = G // S5_GB
    dev = _slot(_place())

    place = jnp.stack([lax.axis_index("c"), 2 * lax.axis_index("x") + lax.axis_index("y")]).astype(jnp.int32)
    sm_sh = _pack([W[n] for n in small_sharded]).reshape(1, -1, LANE)
    groups = []
    for i in range(depth):
        j = i // 2
        groups.append([("gla_w_in", j), ("gla_w_out", j)] if i % 2 == 0 else [("s5_w_in", j), ("s5_w_out", j)])
        groups.append([("mlp_w_up", i), ("mlp_w_down", i)])
    groups[0] = [("small", 0)] + groups[0]
    tok = jnp.zeros((SUBLANE, LANE), F32)
    level1, level2, full = [], {}, {}
    for gi, keys in enumerate(groups):
        lands = [_to_slot(sm_sh, 0, place, F32, "own_small") if n == "small" else _to_slot(W[n], j, place, BF16, f"own_{n}_{j}")
                 for n, j in keys]
        send_sems, recv_sems, _, lands, tok = _split_start(None, lands, _gather_routes, 4, tok, f"gather_start_{gi}")
        level1.append((send_sems, recv_sems, lands))

    def arrive(gi, after):
        send_sems, recv_sems, lands = level1[gi]
        _, lands = _split_wait(send_sems, recv_sems, None, lands, _gather_routes, after, f"gather_wait_{gi}")
        send_sems, recv_sems, _, lands, token = _split_start(None, lands, _pass_routes, 3, after, f"gather_pass_{gi}")
        level2[gi] = (send_sems, recv_sems, lands)
        return token

    def fetch(gi, after):
        send_sems, recv_sems, lands = level2[gi]
        _, lands = _split_wait(send_sems, recv_sems, None, lands, _pass_routes, after, f"gather_done_{gi}")
        full.update(zip(groups[gi], lands))

    s5p = []
    for j in range(n_s5):
        lre, lim = s5_lam_re[j][:, None, :], s5_lam_im[j][:, None, :]
        ldt = s5_log_dt[j][:, None, None]
        brt, bit = jnp.swapaxes(s5_b_re[j], 1, 2), jnp.swapaxes(s5_b_im[j], 1, 2)
        ar, ai, bbr, bbi, pr, pi, a124 = _s5_params(lre, lim, ldt, brt, bit, f"s5{j}_params")
        flat = lambda t: jnp.swapaxes(t, 0, 1).reshape(t.shape[1], G * N)
        pr, pi, a124 = flat(pr), flat(pi), flat(a124)
        sw = lambda t: jnp.swapaxes(t, 1, 2)
        bd = _bd_all([bbr, bbi, sw(s5_c_re[j]), -sw(s5_c_im[j]), s5_c_re[j], -s5_c_im[j], sw(bbr), sw(bbi)], f"s5{j}_blockdiag",
                     (tok,) if j == n_s5 - 1 else ())
        s5p.append(dict(prm=(lre, lim, ldt, brt, bit), pr=pr, pi=pi, a124=a124, bd=dict(zip(
            ("bu_re", "bu_im", "cx_re", "cx_im", "w_re", "w_im", "du_re", "du_im"), bd))))

    tok = arrive(0, s5p[-1]["bd"]["bu_re"] if s5p else tok)
    fetch(0, tok)
    sm_all = full["small", 0]
    sm_parts = [_unpack(sm_all[d].reshape(-1), [W[n].shape for n in small_sharded]) for d in range(N_DEV)]
    wgu_full = jnp.concatenate([p[0] for p in sm_parts], axis=2)
    s5n_full = jnp.concatenate([p[1] for p in sm_parts], axis=1)
    s5d_full = jnp.concatenate([p[2] for p in sm_parts], axis=1)

    def gla_weights(j):
        w_in = full["gla_w_in", j]
        w_in = jnp.transpose(w_in, (1, 0, 2)).reshape(D, IN)
        w_in = jnp.pad(w_in, ((0, 0), (0, INP - IN)))
        w_out = full["gla_w_out", j].reshape(VW, D)
        wgu = jnp.pad(wgu_full[j], ((0, LANE - GLA_RANK), (0, 0))).astype(BF16)
        return w_in, w_out, wgu

    grads = {}
    h = x[0]
    saved = []

    for i in range(depth):
        j = i // 2
        if i > 0:
            fetch(2 * i, h)
        if i % 2 == 0:
            w_in, w_out, wgu = gla_weights(j)
            gn = gla_norm[j][None]
            hn = _rms_fwd(h, gn, f"gla{j}_norm")
            proj = _mm(hn, w_in, "nn", name=f"gla{j}_proj", tn=896)
            ahead = arrive(2 * i + 1, proj)
            bg, on = gla_b_gate[j][None], gla_o_norm[j][None]
            y, st = _gla_fwd(proj, wgu, bg, on, H=H, DK=DK, DV=DV, name=f"gla{j}_mix", deps=(ahead,))
            h_new = _mm(y, w_out, "nn", name=f"gla{j}_out", res=h)
            saved.append(("gla", dict(h=h, hn=hn, proj=proj, y=y, st=st, w_in=w_in, w_out=w_out, wgu=wgu, gn=gn, bg=bg, on=on)))
        else:
            w_in = full["s5_w_in", j].reshape(D, SW)
            w_out3 = full["s5_w_out", j]
            sn = s5n_full[j][None]
            hn = _rms_fwd(h, sn, f"s5{j}_norm")
            u = _mm(hn, w_in, "nn", name=f"s5{j}_in")
            sp = s5p[j]
            bur = _mm_bd(u, sp["bd"]["bu_re"], name=f"s5{j}_bu_re")
            bui = _mm_bd(u, sp["bd"]["bu_im"], name=f"s5{j}_bu_im")
            xr, xi = _s5_scan(bur, bui, sp["a124"], sp["pr"], sp["pi"], reverse=False, name=f"s5{j}_scan")
            ahead = arrive(2 * i + 1, xr)
            cx = _mm_bd(xr, sp["bd"]["cx_re"], name=f"s5{j}_cx_re", deps=(ahead,))
            cx = _mm_bd(xi, sp["bd"]["cx_im"], name=f"s5{j}_cx_im", res=cx)
            dsk = s5d_full[j][None]

            def act(cxv, uv, dv):
                ypre = cxv + dv * uv
                return ypre, _gelu_tanh(ypre)
            ypre, yg = _rowmap(act, [cx, u], [dsk], [(SW, F32), (SW, BF16)], name=f"s5{j}_act")
            z = _mm_nn_cb(yg, w_out3, name=f"s5{j}_out")

            def glu(zv, hv):
                return (hv + zv[:, :D] * jax.nn.sigmoid(zv[:, D:]),)
            h_new = _rowmap(glu, [z, h], [], [(D, F32)], name=f"s5{j}_glu")[0]
            saved.append(("s5", dict(h=h, hn=hn, u=u, xr=xr, xi=xi, ypre=ypre, yg=yg, z=z, w_in=w_in, w_out3=w_out3, sn=sn,
                                     dsk=dsk, **sp)))
        h = h_new
        fetch(2 * i + 1, h)
        w_up3 = full["mlp_w_up", i]
        w_down = full["mlp_w_down", i].reshape(4 * D, D)
        mn = mlp_norm[i][None]
        hn = _rms_fwd(h, mn, f"mlp{i}_norm")
        def sq_relu(zv):
            a = jnp.maximum(zv, 0.0)
            return zv, a * a
        z, s = _mm_nn_cb(hn, w_up3, name=f"mlp{i}_up", out_dtype=[F32, BF16], epi=sq_relu)
        ahead = (arrive(2 * i + 2, s),) if i + 1 < depth else ()
        h_new = _mm(s, w_down, "nn", name=f"mlp{i}_down", res=h, deps=ahead)
        saved.append(("mlp", dict(h=h, hn=hn, z=z, s=s, w_up3=w_up3, w_down=w_down, mn=mn)))
        h = h_new

    dh, sq, dfin = _loss_head(h, loss_target[0], final_norm[None], "loss_head")
    loss = lax.psum(0.5 * jnp.sum(sq) / D, ("x", "y", "c"))
    small_grads = {"final_norm": dfin[0]}
    big_parts = {n: {} for n in big}
    stacks = {n: {} for n in small if n != "final_norm"}
    flat3 = lambda a: a.reshape(a.shape[0], math.prod(a.shape[1:-1]), a.shape[-1])
    exchanges = []

    def pair_grads(keys):
        tag = "_".join(f"{n}{j}" for n, j in keys)
        parts8 = [flat3(big_parts[n][j]) for n, j in keys]
        lands = [lax.empty((N_CHIP, *p.shape[1:]), p.dtype) for p in parts8]
        send_sems, recv_sems, parts8, lands, token = _split_start(parts8, lands, _pair_routes, N_CHIP, place, f"grads_pair_{tag}")
        return (keys, tag, send_sems, recv_sems, parts8, lands), token

    def send_grads(pending, after):
        keys, tag, send_sems, recv_sems, parts8, lands = pending
        parts8, from_sib = _split_wait(send_sems, recv_sems, parts8, lands, _pair_routes, after, f"grads_paired_{tag}")
        sums, lands = zip(*[_pair_sum(p, g, place, f"grads_pair_sum_{n}{j}") for (n, j), p, g in zip(keys, parts8, from_sib)])
        send_sems, recv_sems, srcs, lands, token = _split_start(list(sums), list(lands), _chip_routes, 3, place, f"grads_start_{tag}")
        exchanges.append((keys, tag, send_sems, recv_sems, srcs, lands))
        return token

    carry = ()

    outs = {}
    s5_small = [n for n in small if n.startswith("s5_") and n not in small_sharded]

    def small_start(subset, tag):
        for n in subset:
            if n in stacks:
                small_grads[n] = jnp.stack([stacks[n][k] for k in range(len(stacks[n]))])
        part = _pack([small_grads[n] for n in subset]).reshape(1, -1, LANE)
        lands = [_to_slot(part, 0, place, F32, f"own_small_grads_{tag}")]
        send_sems, recv_sems, _, lands, _ = _split_start(None, lands, _gather_routes, 4, place, f"small_grads_start_{tag}")
        return subset, tag, send_sems, recv_sems, lands

    def small_finish(handle, after):
        subset, tag, send_sems, recv_sems, lands = handle
        _, lands = _split_wait(send_sems, recv_sems, None, lands, _gather_routes, after, f"small_grads_wait_{tag}")
        send_sems, recv_sems, _, lands, t = _split_start(None, lands, _pass_routes, 3, after, f"small_grads_pass_{tag}")
        _, lands = _split_wait(send_sems, recv_sems, None, lands, _pass_routes, t, f"small_grads_done_{tag}")
        parts_all = lands[0]
        rows = parts_all.shape[1]
        shapes = [small_grads[n].shape for n in subset]

        def full_layout(T):
            arrs = []
            for n, shp in zip(subset, shapes):
                if n in small_sharded:
                    ax = small_sharded[n]
                    arrs.append(lax.dynamic_update_slice_in_dim(jnp.zeros(shp, F32), T[n], dev * T[n].shape[ax], axis=ax))
                else:
                    arrs.append(T[n])
            return _pack(arrs).reshape(1, rows, LANE)

        res = _adamw(full_layout(W), full_layout(M), full_layout(V), parts_all, 0, None, f"adamw_small_{tag}")
        for key, flat in zip(("grad", "delta", "new_m", "new_v"), res):
            for n, a in zip(subset, _unpack(flat.reshape(-1), shapes)):
                if n in small_sharded:
                    ax = small_sharded[n]
                    a = lax.dynamic_slice_in_dim(a, dev * W[n].shape[ax], W[n].shape[ax], axis=ax)
                outs[key, n] = a
        return res[0]

    for idx in range(len(saved) - 1, -1, -1):
        kind, s = saved[idx]
        li = sum(1 for k, _ in saved[:idx] if k == kind)
        if kind == "mlp":
            def sq_relu_bwd(dsv, zv):
                return (dsv * 2.0 * jnp.maximum(zv, 0.0),)
            dz = _mm(dh, s["w_down"], "nt", name=f"mlp{li}_dz", out_dtype=BF16, epi=sq_relu_bwd, epi_ins=(s["z"],), deps=carry)
            big_parts["mlp_w_down"][li] = _mm(s["s"], dh, "tn", name=f"mlp{li}_dw_down", out_dtype=BF16).reshape(N_DEV, 4 * D // N_DEV, D)
            big_parts["mlp_w_up"][li] = _mm_tn_cbout(s["hn"], dz, N_DEV, name=f"mlp{li}_dw_up")
            pending, token = pair_grads([("mlp_w_down", li), ("mlp_w_up", li)])
            dhn = _mm_nt_cb(dz, s["w_up3"], name=f"mlp{li}_dhn")
            dh, dg = _rms_bwd(s["h"], dhn, dh, s["mn"], f"mlp{li}_dnorm", deps=(token,))
            stacks["mlp_norm"][li] = dg[0]
        elif kind == "gla":
            dyv = _mm(dh, s["w_out"], "nt", name=f"gla{li}_dy", deps=carry)
            big_parts["gla_w_out"][li] = _mm(s["y"], dh, "tn", name=f"gla{li}_dw_out", out_dtype=BF16).reshape(N_DEV, VW // N_DEV, D)
            dproj, don, dbg, dwgu = _gla_bwd(s["proj"], s["wgu"], s["bg"], s["on"], s["st"], dyv, H=H, DK=DK, DV=DV, name=f"gla{li}_dmix")
            dwgu = dwgu[:GLA_RANK]
            dw_in = _mm(s["hn"], dproj, "tn", name=f"gla{li}_dw_in", out_dtype=BF16, tn=896)[:, :IN]
            big_parts["gla_w_in"][li] = jnp.transpose(dw_in.reshape(D, N_DEV, IN // N_DEV), (1, 0, 2))
            pending, token = pair_grads([("gla_w_out", li), ("gla_w_in", li)])
            dhn = _mm(dproj, s["w_in"], "nt", name=f"gla{li}_dhn", tk=896)
            dh, dg = _rms_bwd(s["h"], dhn, dh, s["gn"], f"gla{li}_dnorm", deps=(token,))
            stacks["gla_norm"][li], stacks["gla_b_gate"][li], stacks["gla_o_norm"][li] = dg[0], dbg[0], don[0]
            stacks["gla_w_gate_up"][li] = dwgu
        else:
            def glu_bwd(zv, dhv, *_):
                a, sg = zv[:, :D], jax.nn.sigmoid(zv[:, D:])
                return (jnp.concatenate([dhv * sg, dhv * a * sg * (1.0 - sg)], axis=1),)
            dz = _rowmap(glu_bwd, [s["z"], dh], list(carry), [(2 * D, BF16)], name=f"s5{li}_dglu")[0]
            big_parts["s5_w_out"][li] = _mm_tn_cbout(s["yg"], dz, N_DEV, name=f"s5{li}_dw_out")
            dyg = _mm_nt_cb(dz, s["w_out3"], name=f"s5{li}_dyg")

            def act_bwd(dygv, ypv, uv, dv):
                _, vjp = jax.vjp(_gelu_tanh, ypv)
                dyp = vjp(dygv)[0]
                return dyp, dyp * dv, jnp.sum(dyp * uv, axis=0, keepdims=True)
            dyp, du, dd = _rowmap(act_bwd, [dyg, s["ypre"], s["u"]], [s["dsk"]], [(SW, F32), (SW, F32)], [SW], name=f"s5{li}_dact")
            wr = _mm_bd(dyp, s["bd"]["w_re"], name=f"s5{li}_w_re")
            wi = _mm_bd(dyp, s["bd"]["w_im"], name=f"s5{li}_w_im")
            a124c = s["a124"] * jnp.array([1, -1, 1, -1, 1, -1, 1, 1], F32)[:, None]
            lr_, li_ = _s5_scan(wr, wi, a124c, s["pr"][::-1], -s["pi"][::-1], reverse=True, name=f"s5{li}_dscan")
            du = _mm_bd(lr_, s["bd"]["du_re"], name=f"s5{li}_du_re", res=du)
            du = _mm_bd(li_, s["bd"]["du_im"], name=f"s5{li}_du_im", res=du)
            dbbr = _bd_extract(_mm_tn_bd(s["u"], lr_, nb, name=f"s5{li}_dbb_re"), S5_GROUP, N)
            dbbi = _bd_extract(_mm_tn_bd(s["u"], li_, nb, name=f"s5{li}_dbb_im"), S5_GROUP, N)
            dcr = _bd_extract(_mm_tn_bd(dyp, s["xr"], nb, name=f"s5{li}_dc_re"), S5_GROUP, N)
            dci = -_bd_extract(_mm_tn_bd(dyp, s["xi"], nb, name=f"s5{li}_dc_im"), S5_GROUP, N)
            dar, dai = _s5_dabar(lr_, li_, s["xr"], s["xi"], f"s5{li}_dabar")
            dlre, dlim, dldt, dbrt, dbit = _s5_params_bwd(*s["prm"], dar.reshape(G, 1, N), dai.reshape(G, 1, N), dbbr, dbbi,
                                                         f"s5{li}_dparams")
            big_parts["s5_w_in"][li] = _mm(s["hn"], du, "tn", name=f"s5{li}_dw_in", out_dtype=BF16).reshape(N_DEV, D // N_DEV, SW)
            pending, token = pair_grads([("s5_w_out", li), ("s5_w_in", li)])
            dhn = _mm(du, s["w_in"], "nt", name=f"s5{li}_dhn")
            dh, dg = _rms_bwd(s["h"], dhn, dh, s["sn"], f"s5{li}_dnorm", deps=(token,))
            stacks["s5_norm"][li], stacks["s5_d"][li] = dg[0], dd[0]
            stacks["s5_lam_re"][li], stacks["s5_lam_im"][li], stacks["s5_log_dt"][li] = dlre[:, 0], dlim[:, 0], dldt[:, 0, 0]
            stacks["s5_b_re"][li], stacks["s5_b_im"][li] = jnp.swapaxes(dbrt, 1, 2), jnp.swapaxes(dbit, 1, 2)
            stacks["s5_c_re"][li], stacks["s5_c_im"][li] = dcr, dci
            if li == 0:
                s5_handle = small_start(s5_small, "s5")
        carry = (send_grads(pending, dh),)
    grad_x = dh[None]

    rest = small_start([n for n in small if n not in s5_small], "rest")

    stacked = {n: None for n in big}
    as3 = lambda a: a.reshape(a.shape[0], math.prod(a.shape[1:-1]), a.shape[-1])

    def finish(exchange, after):
        keys, tag, e_send, e_recv, e_srcs, e_lands = exchange
        _, e_lands = _split_wait(e_send, e_recv, e_srcs, e_lands, _chip_routes, after, f"grads_wait_{tag}")
        for (n, j), parts in zip(keys, e_lands):
            stacked[n] = _adamw(as3(W[n]), as3(M[n]), as3(V[n]), parts, j, stacked[n], f"adamw_{n}_{j}")
        return stacked[keys[0][0]][0]

    done = [small_finish(s5_handle, carry[0])]
    for exchange in exchanges[:-1]:
        finish(exchange, carry[0])
    done.append(small_finish(rest, carry[0]))
    last_keys = {n for n, _ in exchanges[-1][0]}
    finish(exchanges[-1], done + [stacked[n][0] for n in big if n not in last_keys])
    for n in big:
        for key, a in zip(("grad", "delta", "new_m", "new_v"), stacked[n]):
            outs[key, n] = a.reshape(W[n].shape)

    return (loss, grad_x, *[outs["grad", n] for n in names], *[outs["delta", n] for n in names],
            *[outs["new_m", n] for n in names], *[outs["new_v", n] for n in names])
```

```python
import functools
import math

import jax
import jax.numpy as jnp
from jax import lax
from jax.experimental import pallas as pl
from jax.experimental.pallas import tpu as pltpu

F32, BF16 = jnp.float32, jnp.bfloat16
MESH = pl.DeviceIdType.MESH
N_DEV = 8
LANE = 128
SUBLANE = 8
VMEM_LIMIT = 48 * 1024 * 1024

EPS = 1e-6
CHUNK = 64
GLA_HEADS = 4
GLA_RANK = 16
GLA_TEMP = 16.0
S5_GROUP = 16
S5_STATE = 64
S5_EIG_CLIP = -1e-4
S5_GB = 16
ADAM_LR, ADAM_B1, ADAM_B2, ADAM_EPS, ADAM_WD, ADAM_STEP = 0.001, 0.9, 0.999, 1e-08, 0.01, 10

_CONTRACT = {"nn": ((1,), (0,)), "tn": ((0,), (0,)), "nt": ((1,), (1,))}


def _tile(n, pref, unit=LANE):
    if n <= pref:
        return n
    t = (pref // unit) * unit
    while t > unit and n % t:
        t -= unit
    assert n % t == 0, (n, pref, unit)
    return t


def _params(sem, vmem=None):
    return pltpu.CompilerParams(dimension_semantics=sem, vmem_limit_bytes=vmem)


def _dot(a, b, dims, precision=None):
    return lax.dot_general(a, b, (_CONTRACT[dims], ((), ())), preferred_element_type=F32, precision=precision)


def _bdot(a, b, dims):
    return _dot(a.astype(BF16), b.astype(BF16), dims)


def _mm_call(a, b, *, dims, grid, a_spec, b_spec, o_spec, out_shape, out_dtype, name, res=None, epi=None, epi_ins=(), deps=(),
             acc_shape=None):
    nk = grid[2]
    acc_shape = acc_shape or tuple(d for d in o_spec.block_shape if d is not None)
    if res is not None:
        epi, epi_ins = (lambda r, x: (r + x,)), (res,)
    single = not isinstance(out_dtype, (list, tuple))
    out_dtypes = [out_dtype] if single else list(out_dtype)
    n_e, n_o, n_d = len(epi_ins), len(out_dtypes), len(deps)

    def body(*refs):
        a_ref, b_ref = refs[:2]
        e_refs, o_refs = refs[2:2 + n_e], refs[2 + n_e + n_d:2 + n_e + n_d + n_o]

        def finish(r):
            vals = (r,) if epi is None else epi(r, *[e[...].astype(F32) for e in e_refs])
            for o_ref, v in zip(o_refs, vals):
                o_ref[...] = v.astype(o_ref.dtype)

        d = _bdot(a_ref[...], b_ref[...], dims)
        if nk == 1:
            finish(d)
            return
        acc = refs[-1]
        k = pl.program_id(2)

        @pl.when(k == 0)
        def _():
            acc[...] = d

        @pl.when((k > 0) & (k < nk - 1))
        def _():
            acc[...] += d

        @pl.when(k == nk - 1)
        def _():
            finish(acc[...] + d)

    outs = pl.pallas_call(
        body, grid=grid, in_specs=[a_spec, b_spec] + [o_spec] * n_e + [pl.BlockSpec(memory_space=pl.ANY)] * n_d,
        out_specs=[o_spec] * n_o, out_shape=[jax.ShapeDtypeStruct(out_shape, dt) for dt in out_dtypes],
        scratch_shapes=[] if nk == 1 else [pltpu.VMEM(acc_shape, F32)],
        compiler_params=_params(("parallel", "parallel", "arbitrary"), VMEM_LIMIT), name=name,
    )(a, b, *epi_ins, *deps)
    return outs[0] if single else outs


TM, TN, TK = 1024, 1024, 2048


def _mm(a, b, dims, *, name, out_dtype=F32, res=None, epi=None, epi_ins=(), deps=(), tm=TM, tn=TN, tk=TK):
    if dims == "tn":
        (K, M), (_, N) = a.shape, b.shape
    elif dims == "nn":
        (M, K), (_, N) = a.shape, b.shape
    else:
        (M, K), (N, _) = a.shape, b.shape
    tm, tn, tk = _tile(M, tm), _tile(N, tn), _tile(K, tk)
    a_spec = pl.BlockSpec((tk, tm), lambda i, j, k: (k, i)) if dims == "tn" else pl.BlockSpec((tm, tk), lambda i, j, k: (i, k))
    b_spec = pl.BlockSpec((tn, tk), lambda i, j, k: (j, k)) if dims == "nt" else pl.BlockSpec((tk, tn), lambda i, j, k: (k, j))
    o_spec = pl.BlockSpec((tm, tn), lambda i, j, k: (i, j))
    return _mm_call(a, b, dims=dims, grid=(M // tm, N // tn, K // tk), a_spec=a_spec, b_spec=b_spec, o_spec=o_spec,
                    out_shape=(M, N), out_dtype=out_dtype, name=name, res=res, epi=epi, epi_ins=epi_ins, deps=deps)


def _mm_nn_cb(a, b3, *, name, out_dtype=F32, epi=None, tm=TM, tn=TN, tk=TK):
    (M, K), (P, _, Ns) = a.shape, b3.shape
    tm, tn, tk = _tile(M, tm), _tile(Ns, tn), _tile(K, tk)
    npb = Ns // tn
    return _mm_call(a, b3, dims="nn", grid=(M // tm, P * npb, K // tk),
                    a_spec=pl.BlockSpec((tm, tk), lambda i, j, k: (i, k)),
                    b_spec=pl.BlockSpec((None, tk, tn), lambda i, j, k: (j // npb, k, j % npb)),
                    o_spec=pl.BlockSpec((tm, tn), lambda i, j, k: (i, j)),
                    out_shape=(M, P * Ns), out_dtype=out_dtype, name=name, epi=epi)


def _mm_nt_cb(a, b3, *, name, out_dtype=F32, tm=TM, tn=TN, tk=TK):
    (M, _), (P, N, Ns) = a.shape, b3.shape
    tm, tn, tk = _tile(M, tm), _tile(N, tn), _tile(Ns, tk)
    kpb = Ns // tk
    return _mm_call(a, b3, dims="nt", grid=(M // tm, N // tn, P * kpb),
                    a_spec=pl.BlockSpec((tm, tk), lambda i, j, k: (i, k)),
                    b_spec=pl.BlockSpec((None, tn, tk), lambda i, j, k: (k // kpb, j, k % kpb)),
                    o_spec=pl.BlockSpec((tm, tn), lambda i, j, k: (i, j)),
                    out_shape=(M, N), out_dtype=out_dtype, name=name)


def _mm_tn_cbout(a, b, parts, *, name, out_dtype=BF16, tm=TM, tn=TN, tk=TK):
    (K, M), (_, N) = a.shape, b.shape
    Ns = N // parts
    tm, tn, tk = _tile(M, tm), _tile(Ns, tn), _tile(K, tk)
    npb = Ns // tn
    return _mm_call(a, b, dims="tn", grid=(M // tm, parts * npb, K // tk),
                    a_spec=pl.BlockSpec((tk, tm), lambda i, j, k: (k, i)),
                    b_spec=pl.BlockSpec((tk, tn), lambda i, j, k: (k, j)),
                    o_spec=pl.BlockSpec((None, tm, tn), lambda i, j, k: (j // npb, i, j % npb)),
                    out_shape=(parts, M, Ns), out_dtype=out_dtype, name=name)


def _mm_bd(a, w3, *, name, out_dtype=F32, res=None, deps=(), tm=2048, tn=1024):
    (M, _), (nb, Kb, Nb) = a.shape, w3.shape
    tm, tn = _tile(M, tm), _tile(Nb, tn)
    npb = Nb // tn
    return _mm_call(a, w3, dims="nn", grid=(M // tm, nb * npb, 1),
                    a_spec=pl.BlockSpec((tm, Kb), lambda i, j, k: (i, j // npb)),
                    b_spec=pl.BlockSpec((None, Kb, tn), lambda i, j, k: (j // npb, 0, j % npb)),
                    o_spec=pl.BlockSpec((tm, tn), lambda i, j, k: (i, j)),
                    out_shape=(M, nb * Nb), out_dtype=out_dtype, name=name, res=res, deps=deps)


def _mm_tn_bd(a, b, nb, A, B, *, name, tk=1024):
    (K, MA), (_, NB) = a.shape, b.shape
    Ma, Nb = MA // nb, NB // nb
    tk = _tile(K, tk)

    def diagonal(r):
        keep = (lax.broadcasted_iota(jnp.int32, (Ma, Nb), 0) // A) == (lax.broadcasted_iota(jnp.int32, (Ma, Nb), 1) // B)
        fold = jnp.where(lax.broadcasted_iota(jnp.int32, (Nb, B), 0) % B == lax.broadcasted_iota(jnp.int32, (Nb, B), 1), 1.0, 0.0)
        return (_dot(jnp.where(keep, r, 0.0), fold, "nn", lax.Precision.HIGHEST),)

    out = _mm_call(a, b, dims="tn", grid=(nb, 1, K // tk),
                   a_spec=pl.BlockSpec((tk, Ma), lambda i, j, k: (k, i)),
                   b_spec=pl.BlockSpec((tk, Nb), lambda i, j, k: (k, i)),
                   o_spec=pl.BlockSpec((None, Ma, B), lambda i, j, k: (i, 0, 0)),
                   out_shape=(nb, Ma, B), out_dtype=F32, name=name, epi=diagonal, acc_shape=(Ma, Nb))
    return out.reshape(nb * (Ma // A), A, B)


def _rowmap(fn, rows, consts, out_defs, red_defs=(), *, name, tr=256):
    L = rows[0].shape[0]
    widest = max([r.shape[1] for r in rows] + [n for n, _ in out_defs])
    tr = _tile(L, max(SUBLANE * 2, min(tr, 512 * 1024 // widest)), SUBLANE * 2)
    n_in, n_o, n_d = len(rows) + len(consts), len(out_defs), len(red_defs)

    def body(*refs):
        res = fn(*[r[...] for r in refs[:n_in]])
        res = res if isinstance(res, (tuple, list)) else (res,)
        outs = refs[n_in:]
        for o_ref, val in zip(outs[:n_o], res[:n_o]):
            o_ref[...] = val.astype(o_ref.dtype)
        if n_d:
            @pl.when(pl.program_id(0) == 0)
            def _():
                for o_ref in outs[n_o:]:
                    o_ref[...] = jnp.zeros_like(o_ref)
            for o_ref, val in zip(outs[n_o:], res[n_o:]):
                o_ref[...] += val

    in_specs = [pl.BlockSpec((tr, r.shape[1]), lambda i: (i, 0)) for r in rows]
    in_specs += [pl.BlockSpec(c.shape, lambda i, nd=c.ndim: (0,) * nd) for c in consts]
    out_specs = [pl.BlockSpec((tr, n), lambda i: (i, 0)) for n, _ in out_defs]
    out_specs += [pl.BlockSpec((1, n), lambda i: (0, 0)) for n in red_defs]
    out_shape = [jax.ShapeDtypeStruct((L, n), dt) for n, dt in out_defs]
    out_shape += [jax.ShapeDtypeStruct((1, n), F32) for n in red_defs]
    return pl.pallas_call(body, grid=(L // tr,), in_specs=in_specs, out_specs=out_specs, out_shape=out_shape,
                          compiler_params=_params(("arbitrary",), VMEM_LIMIT), name=name)(*rows, *consts)


def _rms_parts(x):
    r = lax.rsqrt(jnp.mean(x * x, axis=-1, keepdims=True) + EPS)
    return r, x * r


def _rms_fwd(h, g, name, deps=()):
    def fn(x, gg, *_):
        _, xh = _rms_parts(x)
        return (xh * gg,)
    return _rowmap(fn, [h], [g, *deps], [(h.shape[1], BF16)], name=name)[0]


def _rms_bwd(h, dhn, dh, g, name, deps=()):
    def fn(x, dy, dres, gg, *_):
        r, xh = _rms_parts(x)
        dxh = dy * gg
        dx = r * (dxh - xh * jnp.mean(dxh * xh, axis=-1, keepdims=True))
        return dres + dx, jnp.sum(dy * xh, axis=0, keepdims=True)
    D = h.shape[1]
    return _rowmap(fn, [h, dhn, dh], [g, *deps], [(D, F32)], [D], name=name)


def _loss_head(h, tgt, g, name):
    D = h.shape[1]

    def fn(x, t, gg):
        r, xh = _rms_parts(x)
        diff = xh * gg - t
        dy = diff * (1.0 / D)
        dxh = dy * gg
        dx = r * (dxh - xh * jnp.mean(dxh * xh, axis=-1, keepdims=True))
        return dx, jnp.sum(diff * diff, axis=0, keepdims=True), jnp.sum(dy * xh, axis=0, keepdims=True)
    return _rowmap(fn, [h, tgt], [g], [(D, F32)], [D, D], name=name)


def _tri(n, strict):
    r = lax.broadcasted_iota(jnp.int32, (n, n), 0)
    c = lax.broadcasted_iota(jnp.int32, (n, n), 1)
    return jnp.where((c < r) if strict else (c <= r), 1.0, 0.0).astype(F32)


def _gla_gate(g_ref, wgu_ref, bg_ref):
    pre = _bdot(g_ref[...], wgu_ref[...], "nn") + bg_ref[...]
    la = (jnp.minimum(pre, 0.0) - jnp.log(1.0 + jnp.exp(-jnp.abs(pre)))) * (1.0 / GLA_TEMP)
    cum = _dot(_tri(CHUNK, False), la, "nn", lax.Precision.HIGHEST)
    return pre, cum, cum[CHUNK - 1:CHUNK, :]


def _gla_specs(H, DK, DV, cmap):
    KW, VW = H * DK, H * DV
    assert (2 * KW) % VW == 0 and (2 * KW + 2 * VW) % LANE == 0
    vb, gb = (2 * KW) // VW, (2 * KW + 2 * VW) // LANE
    return [
        pl.BlockSpec((CHUNK, KW), lambda c: (cmap(c), 0)),
        pl.BlockSpec((CHUNK, KW), lambda c: (cmap(c), 1)),
        pl.BlockSpec((CHUNK, VW), lambda c: (cmap(c), vb)),
        pl.BlockSpec((CHUNK, VW), lambda c: (cmap(c), vb + 1)),
        pl.BlockSpec((CHUNK, LANE), lambda c: (cmap(c), gb)),
        pl.BlockSpec((LANE, KW), lambda c: (0, 0)),
        pl.BlockSpec((1, KW), lambda c: (0, 0)),
        pl.BlockSpec((1, DV), lambda c: (0, 0)),
    ]


def _gla_fwd(proj, wgu, bg, on, *, H, DK, DV, name, deps=()):
    L = proj.shape[0]
    nc = L // CHUNK
    scale = DK ** -0.5
    n_d = len(deps)

    def body(q_ref, k_ref, v_ref, r_ref, g_ref, wgu_ref, bg_ref, on_ref, *rest):
        y_ref, st_ref, S = rest[n_d:]

        @pl.when(pl.program_id(0) == 0)
        def _():
            S[...] = jnp.zeros_like(S)
        _, cum, total = _gla_gate(g_ref, wgu_ref, bg_ref)
        kd = k_ref[...] * jnp.exp(total - cum)
        dec = jnp.exp(total)
        for h in range(H):
            ks, vs = slice(h * DK, (h + 1) * DK), slice(h * DV, (h + 1) * DV)
            St = S[h] * dec[:, ks] + _bdot(v_ref[:, vs], kd[:, ks], "tn")
            S[h] = St
            st_ref[h] = St
            o = _bdot(q_ref[:, ks] * scale, St, "nt")
            _, oh = _rms_parts(o)
            y_ref[:, vs] = (oh * on_ref[...] * jax.nn.silu(r_ref[:, vs])).astype(y_ref.dtype)

    return pl.pallas_call(
        body, grid=(nc,), in_specs=_gla_specs(H, DK, DV, lambda c: c) + [pl.BlockSpec(memory_space=pl.ANY)] * n_d,
        out_specs=[pl.BlockSpec((CHUNK, H * DV), lambda c: (c, 0)),
                   pl.BlockSpec((H, None, DV, DK), lambda c: (0, c, 0, 0))],
        out_shape=[jax.ShapeDtypeStruct((L, H * DV), BF16), jax.ShapeDtypeStruct((H, nc, DV, DK), F32)],
        scratch_shapes=[pltpu.VMEM((H, DV, DK), F32)],
        compiler_params=_params(("arbitrary",), VMEM_LIMIT), name=name,
    )(proj, proj, proj, proj, proj, wgu, bg, on, *deps)


def _gla_bwd(proj, wgu, bg, on, st, dy, *, H, DK, DV, name):
    L = proj.shape[0]
    nc = L // CHUNK
    KW, VW = H * DK, H * DV
    scale = DK ** -0.5
    rev = lambda c: nc - 1 - c

    def body(q_ref, k_ref, v_ref, r_ref, g_ref, wgu_ref, bg_ref, on_ref, sc_ref, sp_ref, dy_ref,
             dp_ref, don_ref, dbg_ref, dwgu_ref, G, decn):
        c = pl.program_id(0)

        @pl.when(c == 0)
        def _():
            for ref in (G, decn, don_ref, dbg_ref, dwgu_ref):
                ref[...] = jnp.zeros_like(ref)

        pre, cum, total = _gla_gate(g_ref, wgu_ref, bg_ref)
        ex = jnp.exp(total - cum)
        kd = k_ref[...] * ex
        dec = jnp.exp(total)
        gn = on_ref[...]
        alive = jnp.where(c == nc - 1, 0.0, 1.0)
        don = jnp.zeros_like(gn)
        ddec, dkd = [], []
        for h in range(H):
            ks, vs = slice(h * DK, (h + 1) * DK), slice(h * DV, (h + 1) * DV)
            qs = q_ref[:, ks] * scale
            Sc = sc_ref[h]
            o = _bdot(qs, Sc, "nt")
            rinv, oh = _rms_parts(o)
            r = r_ref[:, vs]
            sg = jax.nn.sigmoid(r)
            dyv = dy_ref[:, vs]
            d_on = dyv * (r * sg)
            dp_ref[:, 2 * KW + VW + h * DV:2 * KW + VW + (h + 1) * DV] = (
                dyv * (oh * gn) * (sg * (1.0 + r * (1.0 - sg)))).astype(dp_ref.dtype)
            don = don + jnp.sum(d_on * oh, axis=0, keepdims=True)
            dxh = d_on * gn
            do = rinv * (dxh - oh * jnp.mean(dxh * oh, axis=-1, keepdims=True))
            dp_ref[:, ks] = (_bdot(do, Sc, "nn") * scale).astype(dp_ref.dtype)
            Gt = G[h] * decn[:, ks] + _bdot(do, qs, "tn")
            G[h] = Gt
            dkd.append(_bdot(v_ref[:, vs], Gt, "nn"))
            dp_ref[:, 2 * KW + h * DV:2 * KW + (h + 1) * DV] = _bdot(kd[:, ks], Gt, "nt").astype(dp_ref.dtype)
            ddec.append(jnp.sum(Gt * (sp_ref[h] * alive), axis=0, keepdims=True))
        don_ref[...] += don
        decn[...] = dec
        dkd, ddec = jnp.concatenate(dkd, axis=1), jnp.concatenate(ddec, axis=1)
        dp_ref[:, KW:2 * KW] = (dkd * ex).astype(dp_ref.dtype)
        dla = ddec * dec + _dot(_tri(CHUNK, True), dkd * kd, "nn", lax.Precision.HIGHEST)
        dpre = dla * (1.0 / GLA_TEMP) * jax.nn.sigmoid(-pre)
        dbg_ref[...] += jnp.sum(dpre, axis=0, keepdims=True)
        dwgu_ref[...] += _bdot(g_ref[...], dpre, "tn")
        dp_ref[:, 2 * KW + 2 * VW:] = _bdot(dpre, wgu_ref[...], "nt").astype(dp_ref.dtype)

    in_specs = _gla_specs(H, DK, DV, rev) + [
        pl.BlockSpec((H, None, DV, DK), lambda c: (0, rev(c), 0, 0)),
        pl.BlockSpec((H, None, DV, DK), lambda c: (0, jnp.maximum(rev(c) - 1, 0), 0, 0)),
        pl.BlockSpec((CHUNK, VW), lambda c: (rev(c), 0)),
    ]
    W = 2 * KW + 2 * VW + LANE
    whole = lambda shp: pl.BlockSpec(shp, lambda c: (0, 0))
    sds = jax.ShapeDtypeStruct
    return pl.pallas_call(
        body, grid=(nc,), in_specs=in_specs,
        out_specs=[pl.BlockSpec((CHUNK, W), lambda c: (rev(c), 0)), whole((1, DV)), whole((1, KW)), whole((LANE, KW))],
        out_shape=[sds((L, W), BF16), sds((1, DV), F32), sds((1, KW), F32), sds((LANE, KW), F32)],
        scratch_shapes=[pltpu.VMEM((H, DV, DK), F32), pltpu.VMEM((1, KW), F32)],
        compiler_params=_params(("arbitrary",), VMEM_LIMIT), name=name,
    )(proj, proj, proj, proj, proj, wgu, bg, on, st, st, dy)


def _s5_param_fn(lam_re, lam_im, log_dt, brt, bit):
    lr = jnp.minimum(lam_re, S5_EIG_CLIP)
    li = lam_im
    dt = jnp.exp(log_dt)
    mag = jnp.exp(lr * dt)
    ang = li * dt
    ab_re = mag * jnp.cos(ang)
    ab_im = mag * jnp.sin(ang)
    den = lr * lr + li * li
    nr = ab_re - 1.0
    f_re = (nr * lr + ab_im * li) / den
    f_im = (ab_im * lr - nr * li) / den
    return ab_re, ab_im, f_re * brt - f_im * bit, f_re * bit + f_im * brt


def _s5_params(lam_re, lam_im, log_dt, brt, bit, name):
    G, _, N = lam_re.shape

    def body(lr_ref, li_ref, dt_ref, br_ref, bi_ref, ar_ref, ai_ref, bbr_ref, bbi_ref, pr_ref, pi_ref, a124_ref,
             qr_ref, qi_ref, c124_ref):
        lre, lim, ldt = lr_ref[...], li_ref[...], dt_ref[...]
        ar, ai, bbr, bbi = _s5_param_fn(lre, lim, ldt, br_ref[...], bi_ref[...])
        ar_ref[...], ai_ref[...], bbr_ref[...], bbi_ref[...] = ar, ai, bbr, bbi
        r = lax.broadcasted_iota(jnp.int32, (1, SUBLANE, 1), 1)
        dt = jnp.exp(ldt)
        rate, turn = jnp.minimum(lre, S5_EIG_CLIP) * dt, lim * dt

        def power(k):
            mag, ang = jnp.exp(k * rate), k * turn
            return mag * jnp.cos(ang), mag * jnp.sin(ang)
        pr_ref[...], pi_ref[...] = power((r + 1).astype(F32))
        qr, qi = power((SUBLANE - r).astype(F32))
        qr_ref[...], qi_ref[...] = qr, -qi
        re, im = power(jnp.where(r < 2, 1.0, jnp.where(r < 4, 2.0, jnp.where(r < 6, 4.0, 0.0))))
        a124_ref[...] = jnp.where(r % 2 == 0, re, im)
        c124_ref[...] = jnp.where(r % 2 == 0, re, -im)

    sds = jax.ShapeDtypeStruct
    return pl.pallas_call(
        body, out_shape=[sds((G, 1, N), F32), sds((G, 1, N), F32), sds(brt.shape, F32), sds(brt.shape, F32)]
        + [sds((G, SUBLANE, N), F32)] * 6, name=name,
    )(lam_re, lam_im, log_dt, brt, bit)


def _s5_params_bwd(lam_re, lam_im, log_dt, brt, bit, dar, dai, dbbr, dbbi, name):
    def body(lr_ref, li_ref, dt_ref, br_ref, bi_ref, dar_ref, dai_ref, dbbr_ref, dbbi_ref, *outs):
        _, vjp = jax.vjp(_s5_param_fn, lr_ref[...], li_ref[...], dt_ref[...], br_ref[...], bi_ref[...])
        for o_ref, val in zip(outs, vjp((dar_ref[...], dai_ref[...], dbbr_ref[...], dbbi_ref[...]))):
            o_ref[...] = val

    ins = (lam_re, lam_im, log_dt, brt, bit)
    return pl.pallas_call(body, out_shape=[jax.ShapeDtypeStruct(a.shape, F32) for a in ins], name=name)(
        *ins, dar, dai, dbbr, dbbi)


def _s5_scan(br, bi, a124, pr, pi, *, reverse, name, W=512):
    L, n = br.shape
    W = _tile(n, W)
    nblk = L // SUBLANE

    def body(br_ref, bi_ref, a_ref, pr_ref, pi_ref, xr_ref, xi_ref):
        A = a_ref[...]
        PR, PI = pr_ref[...], pi_ref[...]
        row = lax.broadcasted_iota(jnp.int32, (SUBLANE, W), 0)
        last = 0 if reverse else SUBLANE - 1
        mult = []
        for j, k in enumerate((1, 2, 4)):
            keep = (row < SUBLANE - k) if reverse else (row >= k)
            mult.append((jnp.where(keep, A[2 * j:2 * j + 1, :], 0.0), jnp.where(keep, A[2 * j + 1:2 * j + 2, :], 0.0)))

        def step(i, carry):
            cr, ci = carry
            off = pl.multiple_of(((nblk - 1 - i) if reverse else i) * SUBLANE, SUBLANE)
            xr, xi = br_ref[pl.ds(off, SUBLANE), :], bi_ref[pl.ds(off, SUBLANE), :]
            for k, (ar, ai) in zip((1, 2, 4), mult):
                shift = (SUBLANE - k) if reverse else k
                sr, si = pltpu.roll(xr, shift, 0), pltpu.roll(xi, shift, 0)
                xr, xi = xr + ar * sr - ai * si, xi + ar * si + ai * sr
            xr, xi = xr + PR * cr - PI * ci, xi + PR * ci + PI * cr
            xr_ref[pl.ds(off, SUBLANE), :] = xr
            xi_ref[pl.ds(off, SUBLANE), :] = xi
            return xr[last:last + 1, :], xi[last:last + 1, :]

        z = jnp.zeros((1, W), F32)
        lax.fori_loop(0, nblk, step, (z, z))

    col = pl.BlockSpec((L, W), lambda j: (0, j))
    par = pl.BlockSpec((SUBLANE, W), lambda j: (0, j))
    return pl.pallas_call(
        body, grid=(n // W,), in_specs=[col, col, par, par, par], out_specs=[col, col],
        out_shape=[jax.ShapeDtypeStruct((L, n), F32)] * 2,
        compiler_params=_params(("parallel",), VMEM_LIMIT), name=name,
    )(br, bi, a124, pr, pi)


def _s5_dabar(lr, li, xr, xi, name, W=256):
    L, n = lr.shape
    W = _tile(n, W)

    def body(lr_ref, li_ref, xr_ref, xi_ref, dar_ref, dai_ref):
        first = lax.broadcasted_iota(jnp.int32, (L, W), 0) == 0
        pr = jnp.where(first, 0.0, pltpu.roll(xr_ref[...], 1, 0))
        pi = jnp.where(first, 0.0, pltpu.roll(xi_ref[...], 1, 0))
        a, b = lr_ref[...], li_ref[...]
        dar_ref[...] = jnp.sum(a * pr + b * pi, axis=0, keepdims=True)
        dai_ref[...] = jnp.sum(b * pr - a * pi, axis=0, keepdims=True)

    col = pl.BlockSpec((L, W), lambda j: (0, j))
    one = pl.BlockSpec((1, W), lambda j: (0, j))
    return pl.pallas_call(
        body, grid=(n // W,), in_specs=[col] * 4, out_specs=[one, one],
        out_shape=[jax.ShapeDtypeStruct((1, n), F32)] * 2,
        compiler_params=_params(("parallel",), VMEM_LIMIT), name=name,
    )(lr, li, xr, xi)


def _bd_all(mats, name, deps=()):
    n = len(mats)

    def body(*refs):
        for w_ref, o_ref in zip(refs[:n], refs[n + len(deps):]):
            G, A, B = w_ref.shape
            rows, cols = S5_GB * A, S5_GB * B
            tile = jnp.where(lax.broadcasted_iota(jnp.int32, (B, cols), 1) % B == lax.broadcasted_iota(jnp.int32, (B, cols), 0),
                             1.0, 0.0).astype(BF16)
            keep = (lax.broadcasted_iota(jnp.int32, (rows, cols), 0) // A) == (lax.broadcasted_iota(jnp.int32, (rows, cols), 1) // B)
            for kb in range(G // S5_GB):
                w2 = w_ref[kb * S5_GB:(kb + 1) * S5_GB].reshape(rows, B)
                o_ref[kb] = jnp.where(keep, _bdot(w2, tile, "nn"), 0.0).astype(o_ref.dtype)

    return pl.pallas_call(
        body, out_shape=[jax.ShapeDtypeStruct((m.shape[0] // S5_GB, S5_GB * m.shape[1], S5_GB * m.shape[2]), BF16) for m in mats],
        compiler_params=pltpu.CompilerParams(vmem_limit_bytes=VMEM_LIMIT), name=name,
    )(*mats, *deps)


def _place():
    return lax.axis_index("x"), lax.axis_index("y"), lax.axis_index("c")


def _slot(p):
    return 4 * p[0] + 2 * p[1] + p[2]


N_CHIP = N_DEV // 2
_HBM = pl.BlockSpec(memory_space=pltpu.HBM)
_SEM = pl.BlockSpec(memory_space=pltpu.SEMAPHORE)
_ANY = pl.BlockSpec(memory_space=pl.ANY)
_EFFECT = pltpu.SideEffectType.DATAFLOW_SIDE_EFFECTING


def _gather_routes(x, y, c):
    me = _slot((x, y, c))
    peers = [(x, y, 1 - c)] + [(px, py, c) for px, py in ((1 - x, y), (x, 1 - y), (1 - x, 1 - y))]
    return [(p, me, me, _slot(p)) for p in peers]


def _chip_routes(x, y, c):
    myq = 2 * x + y
    return [((px, py, c), 2 * px + py, myq, 2 * px + py) for px, py in ((1 - x, y), (x, 1 - y), (1 - x, 1 - y))]


def _pass_routes(x, y, c):
    chips = ((1 - x, y), (x, 1 - y), (1 - x, 1 - y))
    return [((x, y, 1 - c), _slot((px, py, c)), _slot((px, py, c)), _slot((px, py, 1 - c))) for px, py in chips]


def _split_copies(s_refs, l_refs, send_sems, recv_sems, routes, arrival):
    out = []
    rts = routes(*_place())
    for a in range(len(l_refs)):
        for k, (peer, src_slot, dst_slot, arr_slot) in enumerate(rts):
            s_ref = l_refs[a] if s_refs is None else s_refs[a]
            src = s_ref if src_slot is None else s_ref.at[src_slot]
            sem = a * len(rts) + k
            out.append(pltpu.make_async_remote_copy(
                src_ref=src, dst_ref=l_refs[a].at[arr_slot if arrival else dst_slot], send_sem=send_sems.at[sem],
                recv_sem=recv_sems.at[sem], device_id=peer, device_id_type=MESH))
    return out


def _split_start(srcs, lands, routes, n_routes, dep, name):
    n, ns = len(lands), 0 if srcs is None else len(srcs)
    bufs = [*(srcs or ()), *lands]

    def body(*refs):
        s_refs = None if srcs is None else refs[:ns]
        for cp in _split_copies(s_refs, refs[ns:ns + n], refs[ns + n + 1], refs[ns + n + 2], routes, False):
            cp.start()
        refs[-1][...] = jnp.zeros_like(refs[-1])

    sems = pltpu.SemaphoreType.DMA((n * n_routes,))
    res = pl.pallas_call(
        body, name=name,
        out_shape=(sems, sems, *[pltpu.HBM(a.shape, a.dtype) for a in bufs], jax.ShapeDtypeStruct((SUBLANE, LANE), F32)),
        in_specs=[_HBM] * (ns + n) + [_ANY], out_specs=(_SEM, _SEM, *[_HBM] * (ns + n), pl.BlockSpec(memory_space=pltpu.VMEM)),
        input_output_aliases={i: 2 + i for i in range(ns + n)},
        compiler_params=pltpu.CompilerParams(has_side_effects=_EFFECT),
    )(*[pltpu.with_memory_space_constraint(a, pltpu.HBM) for a in bufs], dep)
    return res[0], res[1], (None if srcs is None else list(res[2:2 + ns])), list(res[2 + ns:2 + ns + n]), res[-1]


def _split_wait(send_sems, recv_sems, srcs, lands, routes, after, name):
    n, ns = len(lands), 0 if srcs is None else len(srcs)
    bufs = [*(srcs or ()), *lands]
    after = list(after) if isinstance(after, (list, tuple)) else [after]

    def body(*refs):
        s_refs = None if srcs is None else refs[:ns]
        for cp in _split_copies(s_refs, refs[ns:ns + n], refs[ns + n], refs[ns + n + 1], routes, True):
            cp.wait_send()
            cp.wait_recv()

    res = pl.pallas_call(
        body, name=name, out_shape=[pltpu.HBM(a.shape, a.dtype) for a in bufs],
        in_specs=[_HBM] * (ns + n) + [_SEM, _SEM] + [_ANY] * len(after), out_specs=[_HBM] * (ns + n),
        input_output_aliases={i: i for i in range(ns + n)},
        compiler_params=pltpu.CompilerParams(has_side_effects=_EFFECT),
    )(*bufs, send_sems, recv_sems, *after)
    return (None if srcs is None else list(res[:ns])), list(res[ns:])


def _to_slot(w, layer, place, out_dtype, name):
    _, R, C = w.shape
    unit = SUBLANE * (4 // jnp.dtype(out_dtype).itemsize)
    tr = _tile(R, max(unit, (512 * 1024 // C) // unit * unit), unit)

    def body(p_ref, w_ref, o_ref):
        o_ref[...] = w_ref[...].astype(o_ref.dtype)

    return pl.pallas_call(
        body,
        grid_spec=pltpu.PrefetchScalarGridSpec(
            num_scalar_prefetch=1, grid=(R // tr,),
            in_specs=[pl.BlockSpec((None, tr, C), lambda i, p: (layer, i, 0))],
            out_specs=pl.BlockSpec((None, tr, C), lambda i, p: (2 * p[1] + p[0], i, 0))),
        out_shape=jax.ShapeDtypeStruct((N_DEV, R, C), out_dtype),
        compiler_params=_params(("parallel",), VMEM_LIMIT), name=name,
    )(place, w)


def _pair_routes(x, y, c):
    return [((x, y, 1 - c), 2 * q + (1 - c), q, q) for q in range(N_CHIP)]


def _pair_sum(x, got, place, name):
    _, R, C = x.shape
    unit = SUBLANE * (4 // x.dtype.itemsize)
    tr = _tile(R, max(unit, (512 * 1024 // C) // unit * unit), unit)

    def body(p_ref, x_ref, g_ref, o_ref, land_ref):
        s = (x_ref[...].astype(F32) + g_ref[...].astype(F32)).astype(o_ref.dtype)
        o_ref[...] = s

        @pl.when(pl.program_id(1) == p_ref[1])
        def _():
            land_ref[...] = s

    blk = lambda f: pl.BlockSpec((None, tr, C), f)
    return pl.pallas_call(
        body,
        grid_spec=pltpu.PrefetchScalarGridSpec(
            num_scalar_prefetch=1, grid=(R // tr, N_CHIP),
            in_specs=[blk(lambda i, q, p: (2 * q + p[0], i, 0)), blk(lambda i, q, p: (q, i, 0))],
            out_specs=[blk(lambda i, q, p: (q, i, 0)), blk(lambda i, q, p: (p[1], i, 0))]),
        out_shape=[jax.ShapeDtypeStruct(got.shape, x.dtype)] * 2,
        compiler_params=_params(("parallel", "arbitrary"), VMEM_LIMIT), name=name,
    )(place, x, got)


def _adamw_math(w, g, m, v):
    m = ADAM_B1 * m + (1.0 - ADAM_B1) * g
    v = ADAM_B2 * v + (1.0 - ADAM_B2) * (g * g)
    m_hat = m / (1.0 - ADAM_B1 ** ADAM_STEP)
    v_hat = v / (1.0 - ADAM_B2 ** ADAM_STEP)
    return -ADAM_LR * (m_hat / (jnp.sqrt(v_hat) + ADAM_EPS) + ADAM_WD * w), m, v


def _adamw(w, m, v, parts, layer, prev, name):
    nl, R, C = w.shape
    P = parts.shape[0]
    unit = SUBLANE * (4 // parts.dtype.itemsize)
    tr = _tile(R, max(unit, (128 * 1024 // C) // unit * unit), unit)

    def body(w_ref, m_ref, v_ref, p_ref, *rest):
        g_ref, d_ref, nm_ref, nv_ref = rest[-4:]
        g = p_ref[0].astype(F32)
        for p in range(1, P):
            g = g + p_ref[p].astype(F32)
        d, nm, nv = _adamw_math(w_ref[...], g, m_ref[...], v_ref[...])
        g_ref[...], d_ref[...], nm_ref[...], nv_ref[...] = g, d, nm, nv

    lay = pl.BlockSpec((None, tr, C), lambda i: (layer, i, 0))
    in_specs = [lay, lay, lay, pl.BlockSpec((P, tr, C), lambda i: (0, i, 0))]
    ins = [w, m, v, parts]
    aliases = {}
    if prev is not None:
        in_specs += [pl.BlockSpec(memory_space=pl.ANY)] * 4
        ins += list(prev)
        aliases = {4 + k: k for k in range(4)}
    return pl.pallas_call(
        body, grid=(R // tr,), in_specs=in_specs, out_specs=[lay] * 4,
        out_shape=[jax.ShapeDtypeStruct(w.shape, F32)] * 4, input_output_aliases=aliases,
        compiler_params=_params(("parallel",), VMEM_LIMIT), name=name,
    )(*ins)


def _gelu_tanh(x):
    return jax.nn.gelu(x, approximate=True)


def _pack(arrs):
    tile = SUBLANE * LANE
    out = []
    for a in arrs:
        f = a.reshape(-1).astype(F32)
        out.append(jnp.pad(f, (0, (-f.shape[0]) % tile)))
    return jnp.concatenate(out)


def _unpack(flat, shapes):
    tile = SUBLANE * LANE
    out, off = [], 0
    for s in shapes:
        n = math.prod(s)
        out.append(flat[off:off + n].reshape(s))
        off += n + (-n) % tile
    return out


def kernel(x, gla_norm, gla_w_in, gla_w_gate_up, gla_b_gate, gla_o_norm, gla_w_out, s5_norm, s5_w_in, s5_lam_re, s5_lam_im, s5_log_dt, s5_b_re, s5_b_im, s5_c_re, s5_c_im, s5_d, s5_w_out, mlp_norm, mlp_w_up, mlp_w_down, final_norm, loss_target, m_gla_norm, m_gla_w_in, m_gla_w_gate_up, m_gla_b_gate, m_gla_o_norm, m_gla_w_out, m_s5_norm, m_s5_w_in, m_s5_lam_re, m_s5_lam_im, m_s5_log_dt, m_s5_b_re, m_s5_b_im, m_s5_c_re, m_s5_c_im, m_s5_d, m_s5_w_out, m_mlp_norm, m_mlp_w_up, m_mlp_w_down, m_final_norm, v_gla_norm, v_gla_w_in, v_gla_w_gate_up, v_gla_b_gate, v_gla_o_norm, v_gla_w_out, v_s5_norm, v_s5_w_in, v_s5_lam_re, v_s5_lam_im, v_s5_log_dt, v_s5_b_re, v_s5_b_im, v_s5_c_re, v_s5_c_im, v_s5_d, v_s5_w_out, v_mlp_norm, v_mlp_w_up, v_mlp_w_down, v_final_norm):
    W = dict(gla_norm=gla_norm, gla_w_in=gla_w_in, gla_w_gate_up=gla_w_gate_up, gla_b_gate=gla_b_gate, gla_o_norm=gla_o_norm, gla_w_out=gla_w_out, s5_norm=s5_norm, s5_w_in=s5_w_in, s5_lam_re=s5_lam_re, s5_lam_im=s5_lam_im, s5_log_dt=s5_log_dt, s5_b_re=s5_b_re, s5_b_im=s5_b_im, s5_c_re=s5_c_re, s5_c_im=s5_c_im, s5_d=s5_d, s5_w_out=s5_w_out, mlp_norm=mlp_norm, mlp_w_up=mlp_w_up, mlp_w_down=mlp_w_down, final_norm=final_norm)
    M = dict(gla_norm=m_gla_norm, gla_w_in=m_gla_w_in, gla_w_gate_up=m_gla_w_gate_up, gla_b_gate=m_gla_b_gate, gla_o_norm=m_gla_o_norm, gla_w_out=m_gla_w_out, s5_norm=m_s5_norm, s5_w_in=m_s5_w_in, s5_lam_re=m_s5_lam_re, s5_lam_im=m_s5_lam_im, s5_log_dt=m_s5_log_dt, s5_b_re=m_s5_b_re, s5_b_im=m_s5_b_im, s5_c_re=m_s5_c_re, s5_c_im=m_s5_c_im, s5_d=m_s5_d, s5_w_out=m_s5_w_out, mlp_norm=m_mlp_norm, mlp_w_up=m_mlp_w_up, mlp_w_down=m_mlp_w_down, final_norm=m_final_norm)
    V = dict(gla_norm=v_gla_norm, gla_w_in=v_gla_w_in, gla_w_gate_up=v_gla_w_gate_up, gla_b_gate=v_gla_b_gate, gla_o_norm=v_gla_o_norm, gla_w_out=v_gla_w_out, s5_norm=v_s5_norm, s5_w_in=v_s5_w_in, s5_lam_re=v_s5_lam_re, s5_lam_im=v_s5_lam_im, s5_log_dt=v_s5_log_dt, s5_b_re=v_s5_b_re, s5_b_im=v_s5_b_im, s5_c_re=v_s5_c_re, s5_c_im=v_s5_c_im, s5_d=v_s5_d, s5_w_out=v_s5_w_out, mlp_norm=v_mlp_norm, mlp_w_up=v_mlp_w_up, mlp_w_down=v_mlp_w_down, final_norm=v_final_norm)
    names = list(W)
    big = ["gla_w_in", "gla_w_out", "s5_w_in", "s5_w_out", "mlp_w_up", "mlp_w_down"]
    small_sharded = {"gla_w_gate_up": 2, "s5_norm": 1, "s5_d": 1}
    small = [n for n in names if n not in big]

    _, L, D = x.shape
    n_gla, n_s5, depth = gla_norm.shape[0], s5_lam_re.shape[0], mlp_norm.shape[0]
    H = GLA_HEADS
    KW, VW = D // 2, D
    DK, DV = KW // H, VW // H
    IN = 2 * KW + 2 * VW + GLA_RANK
    INP = 2 * KW + 2 * VW + LANE
    SW = s5_lam_re.shape[1] * S5_GROUP
    G, N = s5_lam_re.shape[1], s5_lam_re.shape[2]
    nb = G // S5_GB
    dev = _slot(_place())

    place = jnp.stack([lax.axis_index("c"), 2 * lax.axis_index("x") + lax.axis_index("y")]).astype(jnp.int32)
    sm_sh = _pack([W[n] for n in small_sharded]).reshape(1, -1, LANE)
    groups = []
    for i in range(depth):
        j = i // 2
        groups.append([("gla_w_in", j), ("gla_w_out", j)] if i % 2 == 0 else [("s5_w_in", j), ("s5_w_out", j)])
        groups.append([("mlp_w_up", i), ("mlp_w_down", i)])
    groups[0] = [("small", 0)] + groups[0]
    tok = jnp.zeros((SUBLANE, LANE), F32)
    level1, level2, full = [], {}, {}
    for gi, keys in enumerate(groups):
        lands = [_to_slot(sm_sh, 0, place, F32, "own_small") if n == "small" else _to_slot(W[n], j, place, BF16, f"own_{n}_{j}")
                 for n, j in keys]
        send_sems, recv_sems, _, lands, tok = _split_start(None, lands, _gather_routes, 4, tok, f"gather_start_{gi}")
        level1.append((send_sems, recv_sems, lands))

    def arrive(gi, after):
        send_sems, recv_sems, lands = level1[gi]
        _, lands = _split_wait(send_sems, recv_sems, None, lands, _gather_routes, after, f"gather_wait_{gi}")
        send_sems, recv_sems, _, lands, token = _split_start(None, lands, _pass_routes, 3, after, f"gather_pass_{gi}")
        level2[gi] = (send_sems, recv_sems, lands)
        return token

    def fetch(gi, after):
        send_sems, recv_sems, lands = level2[gi]
        _, lands = _split_wait(send_sems, recv_sems, None, lands, _pass_routes, after, f"gather_done_{gi}")
        full.update(zip(groups[gi], lands))

    s5p = []
    for j in range(n_s5):
        lre, lim = s5_lam_re[j][:, None, :], s5_lam_im[j][:, None, :]
        ldt = s5_log_dt[j][:, None, None]
        brt, bit = jnp.swapaxes(s5_b_re[j], 1, 2), jnp.swapaxes(s5_b_im[j], 1, 2)
        ar, ai, bbr, bbi, *mult = _s5_params(lre, lim, ldt, brt, bit, f"s5{j}_params")
        pr, pi, a124, qr, qi, c124 = [jnp.swapaxes(t, 0, 1).reshape(SUBLANE, G * N) for t in mult]
        sw = lambda t: jnp.swapaxes(t, 1, 2)
        bd = _bd_all([bbr, bbi, sw(s5_c_re[j]), -sw(s5_c_im[j]), s5_c_re[j], -s5_c_im[j], sw(bbr), sw(bbi)], f"s5{j}_blockdiag",
                     (tok,) if j == n_s5 - 1 else ())
        s5p.append(dict(prm=(lre, lim, ldt, brt, bit), pr=pr, pi=pi, a124=a124, qr=qr, qi=qi, c124=c124, bd=dict(zip(
            ("bu_re", "bu_im", "cx_re", "cx_im", "w_re", "w_im", "du_re", "du_im"), bd))))

    tok = arrive(0, s5p[-1]["bd"]["bu_re"] if s5p else tok)
    fetch(0, tok)
    sm_all = full["small", 0]
    sm_parts = [_unpack(sm_all[d].reshape(-1), [W[n].shape for n in small_sharded]) for d in range(N_DEV)]
    wgu_full = jnp.concatenate([p[0] for p in sm_parts], axis=2)
    s5n_full = jnp.concatenate([p[1] for p in sm_parts], axis=1)
    s5d_full = jnp.concatenate([p[2] for p in sm_parts], axis=1)

    def gla_weights(j):
        w_in = full["gla_w_in", j]
        w_in = jnp.transpose(w_in, (1, 0, 2)).reshape(D, IN)
        w_in = jnp.pad(w_in, ((0, 0), (0, INP - IN)))
        w_out = full["gla_w_out", j].reshape(VW, D)
        wgu = jnp.pad(wgu_full[j], ((0, LANE - GLA_RANK), (0, 0))).astype(BF16)
        return w_in, w_out, wgu

    grads = {}
    h = x[0]
    saved = []

    for i in range(depth):
        j = i // 2
        if i > 0:
            fetch(2 * i, h)
        if i % 2 == 0:
            w_in, w_out, wgu = gla_weights(j)
            gn = gla_norm[j][None]
            hn = _rms_fwd(h, gn, f"gla{j}_norm")
            proj = _mm(hn, w_in, "nn", name=f"gla{j}_proj", tn=896)
            ahead = arrive(2 * i + 1, proj)
            bg, on = gla_b_gate[j][None], gla_o_norm[j][None]
            y, st = _gla_fwd(proj, wgu, bg, on, H=H, DK=DK, DV=DV, name=f"gla{j}_mix", deps=(ahead,))
            h_new = _mm(y, w_out, "nn", name=f"gla{j}_out", res=h)
            saved.append(("gla", dict(h=h, hn=hn, proj=proj, y=y, st=st, w_in=w_in, w_out=w_out, wgu=wgu, gn=gn, bg=bg, on=on)))
        else:
            w_in = full["s5_w_in", j].reshape(D, SW)
            w_out3 = full["s5_w_out", j]
            sn = s5n_full[j][None]
            hn = _rms_fwd(h, sn, f"s5{j}_norm")
            u = _mm(hn, w_in, "nn", name=f"s5{j}_in")
            sp = s5p[j]
            bur = _mm_bd(u, sp["bd"]["bu_re"], name=f"s5{j}_bu_re")
            bui = _mm_bd(u, sp["bd"]["bu_im"], name=f"s5{j}_bu_im")
            xr, xi = _s5_scan(bur, bui, sp["a124"], sp["pr"], sp["pi"], reverse=False, name=f"s5{j}_scan")
            ahead = arrive(2 * i + 1, xr)
            cx = _mm_bd(xr, sp["bd"]["cx_re"], name=f"s5{j}_cx_re", deps=(ahead,))
            cx = _mm_bd(xi, sp["bd"]["cx_im"], name=f"s5{j}_cx_im", res=cx)
            dsk = s5d_full[j][None]

            def act(cxv, uv, dv):
                ypre = cxv + dv * uv
                return ypre, _gelu_tanh(ypre)
            ypre, yg = _rowmap(act, [cx, u], [dsk], [(SW, F32), (SW, BF16)], name=f"s5{j}_act")
            z = _mm_nn_cb(yg, w_out3, name=f"s5{j}_out")

            def glu(zv, hv):
                return (hv + zv[:, :D] * jax.nn.sigmoid(zv[:, D:]),)
            h_new = _rowmap(glu, [z, h], [], [(D, F32)], name=f"s5{j}_glu")[0]
            saved.append(("s5", dict(h=h, hn=hn, u=u, xr=xr, xi=xi, ypre=ypre, yg=yg, z=z, w_in=w_in, w_out3=w_out3, sn=sn,
                                     dsk=dsk, **sp)))
        h = h_new
        fetch(2 * i + 1, h)
        w_up3 = full["mlp_w_up", i]
        w_down = full["mlp_w_down", i].reshape(4 * D, D)
        mn = mlp_norm[i][None]
        hn = _rms_fwd(h, mn, f"mlp{i}_norm")
        def sq_relu(zv):
            a = jnp.maximum(zv, 0.0)
            return zv, a * a
        z, s = _mm_nn_cb(hn, w_up3, name=f"mlp{i}_up", out_dtype=[F32, BF16], epi=sq_relu)
        ahead = (arrive(2 * i + 2, s),) if i + 1 < depth else ()
        h_new = _mm(s, w_down, "nn", name=f"mlp{i}_down", res=h, deps=ahead)
        saved.append(("mlp", dict(h=h, hn=hn, z=z, s=s, w_up3=w_up3, w_down=w_down, mn=mn)))
        h = h_new

    dh, sq, dfin = _loss_head(h, loss_target[0], final_norm[None], "loss_head")
    loss = lax.psum(0.5 * jnp.sum(sq) / D, ("x", "y", "c"))
    small_grads = {"final_norm": dfin[0]}
    big_parts = {n: {} for n in big}
    stacks = {n: {} for n in small if n != "final_norm"}
    flat3 = lambda a: a.reshape(a.shape[0], math.prod(a.shape[1:-1]), a.shape[-1])
    exchanges = []

    def pair_grads(keys):
        tag = "_".join(f"{n}{j}" for n, j in keys)
        parts8 = [flat3(big_parts[n][j]) for n, j in keys]
        lands = [lax.empty((N_CHIP, *p.shape[1:]), p.dtype) for p in parts8]
        send_sems, recv_sems, parts8, lands, token = _split_start(parts8, lands, _pair_routes, N_CHIP, place, f"grads_pair_{tag}")
        return (keys, tag, send_sems, recv_sems, parts8, lands), token

    def send_grads(pending, after):
        keys, tag, send_sems, recv_sems, parts8, lands = pending
        parts8, from_sib = _split_wait(send_sems, recv_sems, parts8, lands, _pair_routes, after, f"grads_paired_{tag}")
        sums, lands = zip(*[_pair_sum(p, g, place, f"grads_pair_sum_{n}{j}") for (n, j), p, g in zip(keys, parts8, from_sib)])
        send_sems, recv_sems, srcs, lands, token = _split_start(list(sums), list(lands), _chip_routes, 3, place, f"grads_start_{tag}")
        exchanges.append((keys, tag, send_sems, recv_sems, srcs, lands))
        return token

    carry = ()

    outs = {}
    s5_small = [n for n in small if n.startswith("s5_") and n not in small_sharded]

    def small_start(subset, tag):
        for n in subset:
            if n in stacks:
                small_grads[n] = jnp.stack([stacks[n][k] for k in range(len(stacks[n]))])
        part = _pack([small_grads[n] for n in subset]).reshape(1, -1, LANE)
        lands = [_to_slot(part, 0, place, F32, f"own_small_grads_{tag}")]
        send_sems, recv_sems, _, lands, _ = _split_start(None, lands, _gather_routes, 4, place, f"small_grads_start_{tag}")
        return subset, tag, send_sems, recv_sems, lands

    def small_finish(handle, after):
        subset, tag, send_sems, recv_sems, lands = handle
        _, lands = _split_wait(send_sems, recv_sems, None, lands, _gather_routes, after, f"small_grads_wait_{tag}")
        send_sems, recv_sems, _, lands, t = _split_start(None, lands, _pass_routes, 3, after, f"small_grads_pass_{tag}")
        _, lands = _split_wait(send_sems, recv_sems, None, lands, _pass_routes, t, f"small_grads_done_{tag}")
        parts_all = lands[0]
        rows = parts_all.shape[1]
        shapes = [small_grads[n].shape for n in subset]

        def full_layout(T):
            arrs = []
            for n, shp in zip(subset, shapes):
                if n in small_sharded:
                    ax = small_sharded[n]
                    arrs.append(lax.dynamic_update_slice_in_dim(jnp.zeros(shp, F32), T[n], dev * T[n].shape[ax], axis=ax))
                else:
                    arrs.append(T[n])
            return _pack(arrs).reshape(1, rows, LANE)

        res = _adamw(full_layout(W), full_layout(M), full_layout(V), parts_all, 0, None, f"adamw_small_{tag}")
        for key, flat in zip(("grad", "delta", "new_m", "new_v"), res):
            for n, a in zip(subset, _unpack(flat.reshape(-1), shapes)):
                if n in small_sharded:
                    ax = small_sharded[n]
                    a = lax.dynamic_slice_in_dim(a, dev * W[n].shape[ax], W[n].shape[ax], axis=ax)
                outs[key, n] = a
        return res[0]

    for idx in range(len(saved) - 1, -1, -1):
        kind, s = saved[idx]
        li = sum(1 for k, _ in saved[:idx] if k == kind)
        if kind == "mlp":
            def sq_relu_bwd(dsv, zv):
                return (dsv * 2.0 * jnp.maximum(zv, 0.0),)
            dz = _mm(dh, s["w_down"], "nt", name=f"mlp{li}_dz", out_dtype=BF16, epi=sq_relu_bwd, epi_ins=(s["z"],), deps=carry)
            big_parts["mlp_w_down"][li] = _mm(s["s"], dh, "tn", name=f"mlp{li}_dw_down", out_dtype=BF16).reshape(N_DEV, 4 * D // N_DEV, D)
            big_parts["mlp_w_up"][li] = _mm_tn_cbout(s["hn"], dz, N_DEV, name=f"mlp{li}_dw_up")
            pending, token = pair_grads([("mlp_w_down", li), ("mlp_w_up", li)])
            dhn = _mm_nt_cb(dz, s["w_up3"], name=f"mlp{li}_dhn")
            dh, dg = _rms_bwd(s["h"], dhn, dh, s["mn"], f"mlp{li}_dnorm", deps=(token,))
            stacks["mlp_norm"][li] = dg[0]
        elif kind == "gla":
            dyv = _mm(dh, s["w_out"], "nt", name=f"gla{li}_dy", deps=carry)
            big_parts["gla_w_out"][li] = _mm(s["y"], dh, "tn", name=f"gla{li}_dw_out", out_dtype=BF16).reshape(N_DEV, VW // N_DEV, D)
            dproj, don, dbg, dwgu = _gla_bwd(s["proj"], s["wgu"], s["bg"], s["on"], s["st"], dyv, H=H, DK=DK, DV=DV, name=f"gla{li}_dmix")
            dwgu = dwgu[:GLA_RANK]
            dw_in = _mm(s["hn"], dproj, "tn", name=f"gla{li}_dw_in", out_dtype=BF16, tn=896)[:, :IN]
            big_parts["gla_w_in"][li] = jnp.transpose(dw_in.reshape(D, N_DEV, IN // N_DEV), (1, 0, 2))
            pending, token = pair_grads([("gla_w_out", li), ("gla_w_in", li)])
            dhn = _mm(dproj, s["w_in"], "nt", name=f"gla{li}_dhn", tk=896)
            dh, dg = _rms_bwd(s["h"], dhn, dh, s["gn"], f"gla{li}_dnorm", deps=(token,))
            stacks["gla_norm"][li], stacks["gla_b_gate"][li], stacks["gla_o_norm"][li] = dg[0], dbg[0], don[0]
            stacks["gla_w_gate_up"][li] = dwgu
        else:
            def glu_bwd(zv, dhv, *_):
                a, sg = zv[:, :D], jax.nn.sigmoid(zv[:, D:])
                return (jnp.concatenate([dhv * sg, dhv * a * sg * (1.0 - sg)], axis=1),)
            dz = _rowmap(glu_bwd, [s["z"], dh], list(carry), [(2 * D, BF16)], name=f"s5{li}_dglu")[0]
            big_parts["s5_w_out"][li] = _mm_tn_cbout(s["yg"], dz, N_DEV, name=f"s5{li}_dw_out")
            dyg = _mm_nt_cb(dz, s["w_out3"], name=f"s5{li}_dyg")

            def act_bwd(dygv, ypv, uv, dv):
                _, vjp = jax.vjp(_gelu_tanh, ypv)
                dyp = vjp(dygv)[0]
                return dyp, dyp * dv, jnp.sum(dyp * uv, axis=0, keepdims=True)
            dyp, du, dd = _rowmap(act_bwd, [dyg, s["ypre"], s["u"]], [s["dsk"]], [(SW, F32), (SW, F32)], [SW], name=f"s5{li}_dact")
            wr = _mm_bd(dyp, s["bd"]["w_re"], name=f"s5{li}_w_re")
            wi = _mm_bd(dyp, s["bd"]["w_im"], name=f"s5{li}_w_im")
            lr_, li_ = _s5_scan(wr, wi, s["c124"], s["qr"], s["qi"], reverse=True, name=f"s5{li}_dscan")
            du = _mm_bd(lr_, s["bd"]["du_re"], name=f"s5{li}_du_re", res=du)
            du = _mm_bd(li_, s["bd"]["du_im"], name=f"s5{li}_du_im", res=du)
            dbbr = _mm_tn_bd(s["u"], lr_, nb, S5_GROUP, N, name=f"s5{li}_dbb_re")
            dbbi = _mm_tn_bd(s["u"], li_, nb, S5_GROUP, N, name=f"s5{li}_dbb_im")
            dcr = _mm_tn_bd(dyp, s["xr"], nb, S5_GROUP, N, name=f"s5{li}_dc_re")
            dci = -_mm_tn_bd(dyp, s["xi"], nb, S5_GROUP, N, name=f"s5{li}_dc_im")
            dar, dai = _s5_dabar(lr_, li_, s["xr"], s["xi"], f"s5{li}_dabar")
            dlre, dlim, dldt, dbrt, dbit = _s5_params_bwd(*s["prm"], dar.reshape(G, 1, N), dai.reshape(G, 1, N), dbbr, dbbi,
                                                         f"s5{li}_dparams")
            big_parts["s5_w_in"][li] = _mm(s["hn"], du, "tn", name=f"s5{li}_dw_in", out_dtype=BF16).reshape(N_DEV, D // N_DEV, SW)
            pending, token = pair_grads([("s5_w_out", li), ("s5_w_in", li)])
            dhn = _mm(du, s["w_in"], "nt", name=f"s5{li}_dhn")
            dh, dg = _rms_bwd(s["h"], dhn, dh, s["sn"], f"s5{li}_dnorm", deps=(token,))
            stacks["s5_norm"][li], stacks["s5_d"][li] = dg[0], dd[0]
            stacks["s5_lam_re"][li], stacks["s5_lam_im"][li], stacks["s5_log_dt"][li] = dlre[:, 0], dlim[:, 0], dldt[:, 0, 0]
            stacks["s5_b_re"][li], stacks["s5_b_im"][li] = jnp.swapaxes(dbrt, 1, 2), jnp.swapaxes(dbit, 1, 2)
            stacks["s5_c_re"][li], stacks["s5_c_im"][li] = dcr, dci
            if li == 0:
                s5_handle = small_start(s5_small, "s5")
        carry = (send_grads(pending, dh),)
    grad_x = dh[None]

    rest = small_start([n for n in small if n not in s5_small], "rest")

    stacked = {n: None for n in big}
    as3 = lambda a: a.reshape(a.shape[0], math.prod(a.shape[1:-1]), a.shape[-1])

    def finish(exchange, after):
        keys, tag, e_send, e_recv, e_srcs, e_lands = exchange
        _, e_lands = _split_wait(e_send, e_recv, e_srcs, e_lands, _chip_routes, after, f"grads_wait_{tag}")
        for (n, j), parts in zip(keys, e_lands):
            stacked[n] = _adamw(as3(W[n]), as3(M[n]), as3(V[n]), parts, j, stacked[n], f"adamw_{n}_{j}")
        return stacked[keys[0][0]][0]

    done = [small_finish(s5_handle, carry[0])]
    for exchange in exchanges[:-1]:
        finish(exchange, carry[0])
    done.append(small_finish(rest, carry[0]))
    last_keys = {n for n, _ in exchanges[-1][0]}
    finish(exchanges[-1], done + [stacked[n][0] for n in big if n not in last_keys])
    for n in big:
        for key, a in zip(("grad", "delta", "new_m", "new_v"), stacked[n]):
            outs[key, n] = a.reshape(W[n].shape)

    return (loss, grad_x, *[outs["grad", n] for n in names], *[outs["delta", n] for n in names],
            *[outs["new_m", n] for n in names], *[outs["new_v", n] for n in names])
```

```python
import functools
import math

import jax
import jax.numpy as jnp
from jax import lax
from jax.experimental import pallas as pl
from jax.experimental.pallas import tpu as pltpu

F32, BF16 = jnp.float32, jnp.bfloat16
MESH = pl.DeviceIdType.MESH
N_DEV = 8
LANE = 128
SUBLANE = 8
VMEM_LIMIT = 48 * 1024 * 1024

EPS = 1e-6
CHUNK = 64
GLA_HEADS = 4
GLA_RANK = 16
GLA_TEMP = 16.0
S5_GROUP = 16
S5_STATE = 64
S5_EIG_CLIP = -1e-4
S5_GB = 16
ADAM_LR, ADAM_B1, ADAM_B2, ADAM_EPS, ADAM_WD, ADAM_STEP = 0.001, 0.9, 0.999, 1e-08, 0.01, 10

_CONTRACT = {"nn": ((1,), (0,)), "tn": ((0,), (0,)), "nt": ((1,), (1,))}


def _tile(n, pref, unit=LANE):
    if n <= pref:
        return n
    t = (pref // unit) * unit
    while t > unit and n % t:
        t -= unit
    assert n % t == 0, (n, pref, unit)
    return t


def _params(sem, vmem=None):
    return pltpu.CompilerParams(dimension_semantics=sem, vmem_limit_bytes=vmem)


def _dot(a, b, dims, precision=None):
    return lax.dot_general(a, b, (_CONTRACT[dims], ((), ())), preferred_element_type=F32, precision=precision)


def _bdot(a, b, dims):
    return _dot(a.astype(BF16), b.astype(BF16), dims)


def _mm_call(a, b, *, dims, grid, a_spec, b_spec, o_spec, out_shape, out_dtype, name, res=None, epi=None, epi_ins=(), deps=(),
             acc_shape=None):
    nk = grid[2]
    acc_shape = acc_shape or tuple(d for d in o_spec.block_shape if d is not None)
    if res is not None:
        epi, epi_ins = (lambda r, x: (r + x,)), (res,)
    single = not isinstance(out_dtype, (list, tuple))
    out_dtypes = [out_dtype] if single else list(out_dtype)
    n_e, n_o, n_d = len(epi_ins), len(out_dtypes), len(deps)

    def body(*refs):
        a_ref, b_ref = refs[:2]
        e_refs, o_refs = refs[2:2 + n_e], refs[2 + n_e + n_d:2 + n_e + n_d + n_o]

        def finish(r):
            vals = (r,) if epi is None else epi(r, *[e[...].astype(F32) for e in e_refs])
            for o_ref, v in zip(o_refs, vals):
                o_ref[...] = v.astype(o_ref.dtype)

        d = _bdot(a_ref[...], b_ref[...], dims)
        if nk == 1:
            finish(d)
            return
        acc = refs[-1]
        k = pl.program_id(2)

        @pl.when(k == 0)
        def _():
            acc[...] = d

        @pl.when((k > 0) & (k < nk - 1))
        def _():
            acc[...] += d

        @pl.when(k == nk - 1)
        def _():
            finish(acc[...] + d)

    outs = pl.pallas_call(
        body, grid=grid, in_specs=[a_spec, b_spec] + [o_spec] * n_e + [pl.BlockSpec(memory_space=pl.ANY)] * n_d,
        out_specs=[o_spec] * n_o, out_shape=[jax.ShapeDtypeStruct(out_shape, dt) for dt in out_dtypes],
        scratch_shapes=[] if nk == 1 else [pltpu.VMEM(acc_shape, F32)],
        compiler_params=_params(("parallel", "parallel", "arbitrary"), VMEM_LIMIT), name=name,
    )(a, b, *epi_ins, *deps)
    return outs[0] if single else outs


TM, TN, TK = 1024, 1024, 2048


def _mm(a, b, dims, *, name, out_dtype=F32, res=None, epi=None, epi_ins=(), deps=(), tm=TM, tn=TN, tk=TK):
    if dims == "tn":
        (K, M), (_, N) = a.shape, b.shape
    elif dims == "nn":
        (M, K), (_, N) = a.shape, b.shape
    else:
        (M, K), (N, _) = a.shape, b.shape
    tm, tn, tk = _tile(M, tm), _tile(N, tn), _tile(K, tk)
    a_spec = pl.BlockSpec((tk, tm), lambda i, j, k: (k, i)) if dims == "tn" else pl.BlockSpec((tm, tk), lambda i, j, k: (i, k))
    b_spec = pl.BlockSpec((tn, tk), lambda i, j, k: (j, k)) if dims == "nt" else pl.BlockSpec((tk, tn), lambda i, j, k: (k, j))
    o_spec = pl.BlockSpec((tm, tn), lambda i, j, k: (i, j))
    return _mm_call(a, b, dims=dims, grid=(M // tm, N // tn, K // tk), a_spec=a_spec, b_spec=b_spec, o_spec=o_spec,
                    out_shape=(M, N), out_dtype=out_dtype, name=name, res=res, epi=epi, epi_ins=epi_ins, deps=deps)


def _mm_nn_cb(a, b3, *, name, out_dtype=F32, epi=None, tm=TM, tn=TN, tk=TK):
    (M, K), (P, _, Ns) = a.shape, b3.shape
    tm, tn, tk = _tile(M, tm), _tile(Ns, tn), _tile(K, tk)
    npb = Ns // tn
    return _mm_call(a, b3, dims="nn", grid=(M // tm, P * npb, K // tk),
                    a_spec=pl.BlockSpec((tm, tk), lambda i, j, k: (i, k)),
                    b_spec=pl.BlockSpec((None, tk, tn), lambda i, j, k: (j // npb, k, j % npb)),
                    o_spec=pl.BlockSpec((tm, tn), lambda i, j, k: (i, j)),
                    out_shape=(M, P * Ns), out_dtype=out_dtype, name=name, epi=epi)


def _mm_nt_cb(a, b3, *, name, out_dtype=F32, tm=TM, tn=TN, tk=TK):
    (M, _), (P, N, Ns) = a.shape, b3.shape
    tm, tn, tk = _tile(M, tm), _tile(N, tn), _tile(Ns, tk)
    kpb = Ns // tk
    return _mm_call(a, b3, dims="nt", grid=(M // tm, N // tn, P * kpb),
                    a_spec=pl.BlockSpec((tm, tk), lambda i, j, k: (i, k)),
                    b_spec=pl.BlockSpec((None, tn, tk), lambda i, j, k: (k // kpb, j, k % kpb)),
                    o_spec=pl.BlockSpec((tm, tn), lambda i, j, k: (i, j)),
                    out_shape=(M, N), out_dtype=out_dtype, name=name)


def _mm_tn_cbout(a, b, parts, *, name, out_dtype=BF16, tm=TM, tn=TN, tk=TK):
    (K, M), (_, N) = a.shape, b.shape
    Ns = N // parts
    tm, tn, tk = _tile(M, tm), _tile(Ns, tn), _tile(K, tk)
    npb = Ns // tn
    return _mm_call(a, b, dims="tn", grid=(M // tm, parts * npb, K // tk),
                    a_spec=pl.BlockSpec((tk, tm), lambda i, j, k: (k, i)),
                    b_spec=pl.BlockSpec((tk, tn), lambda i, j, k: (k, j)),
                    o_spec=pl.BlockSpec((None, tm, tn), lambda i, j, k: (j // npb, i, j % npb)),
                    out_shape=(parts, M, Ns), out_dtype=out_dtype, name=name)


def _mm_tn_bd(a, b, nb, A, B, *, name, tk=1024):
    (K, MA), (_, NB) = a.shape, b.shape
    Ma, Nb = MA // nb, NB // nb
    tk = _tile(K, tk)

    def diagonal(r):
        keep = (lax.broadcasted_iota(jnp.int32, (Ma, Nb), 0) // A) == (lax.broadcasted_iota(jnp.int32, (Ma, Nb), 1) // B)
        fold = jnp.where(lax.broadcasted_iota(jnp.int32, (Nb, B), 0) % B == lax.broadcasted_iota(jnp.int32, (Nb, B), 1), 1.0, 0.0)
        return (_dot(jnp.where(keep, r, 0.0), fold, "nn", lax.Precision.HIGHEST),)

    out = _mm_call(a, b, dims="tn", grid=(nb, 1, K // tk),
                   a_spec=pl.BlockSpec((tk, Ma), lambda i, j, k: (k, i)),
                   b_spec=pl.BlockSpec((tk, Nb), lambda i, j, k: (k, i)),
                   o_spec=pl.BlockSpec((None, Ma, B), lambda i, j, k: (i, 0, 0)),
                   out_shape=(nb, Ma, B), out_dtype=F32, name=name, epi=diagonal, acc_shape=(Ma, Nb))
    return out.reshape(nb * (Ma // A), A, B)


def _rowmap(fn, rows, consts, out_defs, red_defs=(), *, name, tr=256):
    L = rows[0].shape[0]
    widest = max([r.shape[1] for r in rows] + [n for n, _ in out_defs])
    tr = _tile(L, max(SUBLANE * 2, min(tr, 512 * 1024 // widest)), SUBLANE * 2)
    n_in, n_o, n_d = len(rows) + len(consts), len(out_defs), len(red_defs)

    def body(*refs):
        res = fn(*[r[...] for r in refs[:n_in]])
        res = res if isinstance(res, (tuple, list)) else (res,)
        outs = refs[n_in:]
        for o_ref, val in zip(outs[:n_o], res[:n_o]):
            o_ref[...] = val.astype(o_ref.dtype)
        if n_d:
            @pl.when(pl.program_id(0) == 0)
            def _():
                for o_ref in outs[n_o:]:
                    o_ref[...] = jnp.zeros_like(o_ref)
            for o_ref, val in zip(outs[n_o:], res[n_o:]):
                o_ref[...] += val

    in_specs = [pl.BlockSpec((tr, r.shape[1]), lambda i: (i, 0)) for r in rows]
    in_specs += [pl.BlockSpec(c.shape, lambda i, nd=c.ndim: (0,) * nd) for c in consts]
    out_specs = [pl.BlockSpec((tr, n), lambda i: (i, 0)) for n, _ in out_defs]
    out_specs += [pl.BlockSpec((1, n), lambda i: (0, 0)) for n in red_defs]
    out_shape = [jax.ShapeDtypeStruct((L, n), dt) for n, dt in out_defs]
    out_shape += [jax.ShapeDtypeStruct((1, n), F32) for n in red_defs]
    return pl.pallas_call(body, grid=(L // tr,), in_specs=in_specs, out_specs=out_specs, out_shape=out_shape,
                          compiler_params=_params(("arbitrary",), VMEM_LIMIT), name=name)(*rows, *consts)


def _rms_parts(x):
    r = lax.rsqrt(jnp.mean(x * x, axis=-1, keepdims=True) + EPS)
    return r, x * r


def _rms_fwd(h, g, name, deps=()):
    def fn(x, gg, *_):
        _, xh = _rms_parts(x)
        return (xh * gg,)
    return _rowmap(fn, [h], [g, *deps], [(h.shape[1], BF16)], name=name)[0]


def _rms_bwd(h, dhn, dh, g, name, deps=()):
    def fn(x, dy, dres, gg, *_):
        r, xh = _rms_parts(x)
        dxh = dy * gg
        dx = r * (dxh - xh * jnp.mean(dxh * xh, axis=-1, keepdims=True))
        return dres + dx, jnp.sum(dy * xh, axis=0, keepdims=True)
    D = h.shape[1]
    return _rowmap(fn, [h, dhn, dh], [g, *deps], [(D, F32)], [D], name=name)


def _loss_head(h, tgt, g, name):
    D = h.shape[1]

    def fn(x, t, gg):
        r, xh = _rms_parts(x)
        diff = xh * gg - t
        dy = diff * (1.0 / D)
        dxh = dy * gg
        dx = r * (dxh - xh * jnp.mean(dxh * xh, axis=-1, keepdims=True))
        return dx, jnp.sum(diff * diff, axis=0, keepdims=True), jnp.sum(dy * xh, axis=0, keepdims=True)
    return _rowmap(fn, [h, tgt], [g], [(D, F32)], [D, D], name=name)


def _tri(n, strict):
    r = lax.broadcasted_iota(jnp.int32, (n, n), 0)
    c = lax.broadcasted_iota(jnp.int32, (n, n), 1)
    return jnp.where((c < r) if strict else (c <= r), 1.0, 0.0).astype(F32)


def _gla_gate(g_ref, wgu_ref, bg_ref):
    pre = _bdot(g_ref[...], wgu_ref[...], "nn") + bg_ref[...]
    la = (jnp.minimum(pre, 0.0) - jnp.log(1.0 + jnp.exp(-jnp.abs(pre)))) * (1.0 / GLA_TEMP)
    cum = _dot(_tri(CHUNK, False), la, "nn", lax.Precision.HIGHEST)
    return pre, cum, cum[CHUNK - 1:CHUNK, :]


def _gla_specs(H, DK, DV, cmap):
    KW, VW = H * DK, H * DV
    assert (2 * KW) % VW == 0 and (2 * KW + 2 * VW) % LANE == 0
    vb, gb = (2 * KW) // VW, (2 * KW + 2 * VW) // LANE
    return [
        pl.BlockSpec((CHUNK, KW), lambda c: (cmap(c), 0)),
        pl.BlockSpec((CHUNK, KW), lambda c: (cmap(c), 1)),
        pl.BlockSpec((CHUNK, VW), lambda c: (cmap(c), vb)),
        pl.BlockSpec((CHUNK, VW), lambda c: (cmap(c), vb + 1)),
        pl.BlockSpec((CHUNK, LANE), lambda c: (cmap(c), gb)),
        pl.BlockSpec((LANE, KW), lambda c: (0, 0)),
        pl.BlockSpec((1, KW), lambda c: (0, 0)),
        pl.BlockSpec((1, DV), lambda c: (0, 0)),
    ]


def _gla_fwd(proj, wgu, bg, on, *, H, DK, DV, name, deps=()):
    L = proj.shape[0]
    nc = L // CHUNK
    scale = DK ** -0.5
    n_d = len(deps)

    def body(q_ref, k_ref, v_ref, r_ref, g_ref, wgu_ref, bg_ref, on_ref, *rest):
        y_ref, st_ref, S = rest[n_d:]

        @pl.when(pl.program_id(0) == 0)
        def _():
            S[...] = jnp.zeros_like(S)
        _, cum, total = _gla_gate(g_ref, wgu_ref, bg_ref)
        kd = k_ref[...] * jnp.exp(total - cum)
        dec = jnp.exp(total)
        for h in range(H):
            ks, vs = slice(h * DK, (h + 1) * DK), slice(h * DV, (h + 1) * DV)
            St = S[h] * dec[:, ks] + _bdot(v_ref[:, vs], kd[:, ks], "tn")
            S[h] = St
            st_ref[h] = St
            o = _bdot(q_ref[:, ks] * scale, St, "nt")
            _, oh = _rms_parts(o)
            y_ref[:, vs] = (oh * on_ref[...] * jax.nn.silu(r_ref[:, vs])).astype(y_ref.dtype)

    return pl.pallas_call(
        body, grid=(nc,), in_specs=_gla_specs(H, DK, DV, lambda c: c) + [pl.BlockSpec(memory_space=pl.ANY)] * n_d,
        out_specs=[pl.BlockSpec((CHUNK, H * DV), lambda c: (c, 0)),
                   pl.BlockSpec((H, None, DV, DK), lambda c: (0, c, 0, 0))],
        out_shape=[jax.ShapeDtypeStruct((L, H * DV), BF16), jax.ShapeDtypeStruct((H, nc, DV, DK), F32)],
        scratch_shapes=[pltpu.VMEM((H, DV, DK), F32)],
        compiler_params=_params(("arbitrary",), VMEM_LIMIT), name=name,
    )(proj, proj, proj, proj, proj, wgu, bg, on, *deps)


def _gla_bwd(proj, wgu, bg, on, st, dy, *, H, DK, DV, name):
    L = proj.shape[0]
    nc = L // CHUNK
    KW, VW = H * DK, H * DV
    scale = DK ** -0.5
    rev = lambda c: nc - 1 - c

    def body(q_ref, k_ref, v_ref, r_ref, g_ref, wgu_ref, bg_ref, on_ref, sc_ref, sp_ref, dy_ref,
             dp_ref, don_ref, dbg_ref, dwgu_ref, G, decn):
        c = pl.program_id(0)

        @pl.when(c == 0)
        def _():
            for ref in (G, decn, don_ref, dbg_ref, dwgu_ref):
                ref[...] = jnp.zeros_like(ref)

        pre, cum, total = _gla_gate(g_ref, wgu_ref, bg_ref)
        ex = jnp.exp(total - cum)
        kd = k_ref[...] * ex
        dec = jnp.exp(total)
        gn = on_ref[...]
        alive = jnp.where(c == nc - 1, 0.0, 1.0)
        don = jnp.zeros_like(gn)
        ddec, dkd = [], []
        for h in range(H):
            ks, vs = slice(h * DK, (h + 1) * DK), slice(h * DV, (h + 1) * DV)
            qs = q_ref[:, ks] * scale
            Sc = sc_ref[h]
            o = _bdot(qs, Sc, "nt")
            rinv, oh = _rms_parts(o)
            r = r_ref[:, vs]
            sg = jax.nn.sigmoid(r)
            dyv = dy_ref[:, vs]
            d_on = dyv * (r * sg)
            dp_ref[:, 2 * KW + VW + h * DV:2 * KW + VW + (h + 1) * DV] = (
                dyv * (oh * gn) * (sg * (1.0 + r * (1.0 - sg)))).astype(dp_ref.dtype)
            don = don + jnp.sum(d_on * oh, axis=0, keepdims=True)
            dxh = d_on * gn
            do = rinv * (dxh - oh * jnp.mean(dxh * oh, axis=-1, keepdims=True))
            dp_ref[:, ks] = (_bdot(do, Sc, "nn") * scale).astype(dp_ref.dtype)
            Gt = G[h] * decn[:, ks] + _bdot(do, qs, "tn")
            G[h] = Gt
            dkd.append(_bdot(v_ref[:, vs], Gt, "nn"))
            dp_ref[:, 2 * KW + h * DV:2 * KW + (h + 1) * DV] = _bdot(kd[:, ks], Gt, "nt").astype(dp_ref.dtype)
            ddec.append(jnp.sum(Gt * (sp_ref[h] * alive), axis=0, keepdims=True))
        don_ref[...] += don
        decn[...] = dec
        dkd, ddec = jnp.concatenate(dkd, axis=1), jnp.concatenate(ddec, axis=1)
        dp_ref[:, KW:2 * KW] = (dkd * ex).astype(dp_ref.dtype)
        dla = ddec * dec + _dot(_tri(CHUNK, True), dkd * kd, "nn", lax.Precision.HIGHEST)
        dpre = dla * (1.0 / GLA_TEMP) * jax.nn.sigmoid(-pre)
        dbg_ref[...] += jnp.sum(dpre, axis=0, keepdims=True)
        dwgu_ref[...] += _bdot(g_ref[...], dpre, "tn")
        dp_ref[:, 2 * KW + 2 * VW:] = _bdot(dpre, wgu_ref[...], "nt").astype(dp_ref.dtype)

    in_specs = _gla_specs(H, DK, DV, rev) + [
        pl.BlockSpec((H, None, DV, DK), lambda c: (0, rev(c), 0, 0)),
        pl.BlockSpec((H, None, DV, DK), lambda c: (0, jnp.maximum(rev(c) - 1, 0), 0, 0)),
        pl.BlockSpec((CHUNK, VW), lambda c: (rev(c), 0)),
    ]
    W = 2 * KW + 2 * VW + LANE
    whole = lambda shp: pl.BlockSpec(shp, lambda c: (0, 0))
    sds = jax.ShapeDtypeStruct
    return pl.pallas_call(
        body, grid=(nc,), in_specs=in_specs,
        out_specs=[pl.BlockSpec((CHUNK, W), lambda c: (rev(c), 0)), whole((1, DV)), whole((1, KW)), whole((LANE, KW))],
        out_shape=[sds((L, W), BF16), sds((1, DV), F32), sds((1, KW), F32), sds((LANE, KW), F32)],
        scratch_shapes=[pltpu.VMEM((H, DV, DK), F32), pltpu.VMEM((1, KW), F32)],
        compiler_params=_params(("arbitrary",), VMEM_LIMIT), name=name,
    )(proj, proj, proj, proj, proj, wgu, bg, on, st, st, dy)


def _s5_param_fn(lam_re, lam_im, log_dt, brt, bit):
    lr = jnp.minimum(lam_re, S5_EIG_CLIP)
    li = lam_im
    dt = jnp.exp(log_dt)
    mag = jnp.exp(lr * dt)
    ang = li * dt
    ab_re = mag * jnp.cos(ang)
    ab_im = mag * jnp.sin(ang)
    den = lr * lr + li * li
    nr = ab_re - 1.0
    f_re = (nr * lr + ab_im * li) / den
    f_im = (ab_im * lr - nr * li) / den
    return ab_re, ab_im, f_re * brt - f_im * bit, f_re * bit + f_im * brt


def _s5_params(lam_re, lam_im, log_dt, brt, bit, name):
    G, _, N = lam_re.shape

    def body(lr_ref, li_ref, dt_ref, br_ref, bi_ref, ar_ref, ai_ref, bbr_ref, bbi_ref, pr_ref, pi_ref, a124_ref,
             qr_ref, qi_ref, c124_ref):
        lre, lim, ldt = lr_ref[...], li_ref[...], dt_ref[...]
        ar, ai, bbr, bbi = _s5_param_fn(lre, lim, ldt, br_ref[...], bi_ref[...])
        ar_ref[...], ai_ref[...], bbr_ref[...], bbi_ref[...] = ar, ai, bbr, bbi
        r = lax.broadcasted_iota(jnp.int32, (1, SUBLANE, 1), 1)
        dt = jnp.exp(ldt)
        rate, turn = jnp.minimum(lre, S5_EIG_CLIP) * dt, lim * dt

        def power(k):
            mag, ang = jnp.exp(k * rate), k * turn
            return mag * jnp.cos(ang), mag * jnp.sin(ang)
        pr_ref[...], pi_ref[...] = power((r + 1).astype(F32))
        qr, qi = power((SUBLANE - r).astype(F32))
        qr_ref[...], qi_ref[...] = qr, -qi
        re, im = power(jnp.where(r < 2, 1.0, jnp.where(r < 4, 2.0, jnp.where(r < 6, 4.0, 0.0))))
        a124_ref[...] = jnp.where(r % 2 == 0, re, im)
        c124_ref[...] = jnp.where(r % 2 == 0, re, -im)

    sds = jax.ShapeDtypeStruct
    return pl.pallas_call(
        body, out_shape=[sds((G, 1, N), F32), sds((G, 1, N), F32), sds(brt.shape, F32), sds(brt.shape, F32)]
        + [sds((G, SUBLANE, N), F32)] * 6, name=name,
    )(lam_re, lam_im, log_dt, brt, bit)


def _s5_params_bwd(lam_re, lam_im, log_dt, brt, bit, dar, dai, dbbr, dbbi, name):
    def body(lr_ref, li_ref, dt_ref, br_ref, bi_ref, dar_ref, dai_ref, dbbr_ref, dbbi_ref, *outs):
        _, vjp = jax.vjp(_s5_param_fn, lr_ref[...], li_ref[...], dt_ref[...], br_ref[...], bi_ref[...])
        for o_ref, val in zip(outs, vjp((dar_ref[...], dai_ref[...], dbbr_ref[...], dbbi_ref[...]))):
            o_ref[...] = val

    ins = (lam_re, lam_im, log_dt, brt, bit)
    return pl.pallas_call(body, out_shape=[jax.ShapeDtypeStruct(a.shape, F32) for a in ins], name=name)(
        *ins, dar, dai, dbbr, dbbi)


def _s5_scan(src, w_re, w_im, a124, pr, pi, c_re, c_im, res, *, reverse, name, W=512):
    L, S = src.shape
    n = pr.shape[1]
    W = _tile(n, W)
    nblk, nj = L // SUBLANE, n // W
    SB = S // nj
    hp = w_re.shape[1] // SB
    assert SB % LANE == 0 and w_re.shape[1] % SB == 0 and w_re.shape[2] == hp * W

    def body(src_ref, wr_ref, wi_ref, a_ref, pr_ref, pi_ref, cr_ref, ci_ref, *rest):
        xr_ref, xi_ref, y_ref = rest[-3:]
        br_ref, bi_ref = xr_ref, xi_ref
        xr_ref[...] = _bdot(src_ref[...], wr_ref[...], "nn")
        xi_ref[...] = _bdot(src_ref[...], wi_ref[...], "nn")
        A = a_ref[...]
        PR, PI = pr_ref[...], pi_ref[...]
        row = lax.broadcasted_iota(jnp.int32, (SUBLANE, W), 0)
        last = 0 if reverse else SUBLANE - 1
        mult = []
        for j, k in enumerate((1, 2, 4)):
            keep = (row < SUBLANE - k) if reverse else (row >= k)
            mult.append((jnp.where(keep, A[2 * j:2 * j + 1, :], 0.0), jnp.where(keep, A[2 * j + 1:2 * j + 2, :], 0.0)))

        def step(i, carry):
            cr, ci = carry
            off = pl.multiple_of(((nblk - 1 - i) if reverse else i) * SUBLANE, SUBLANE)
            xr, xi = br_ref[pl.ds(off, SUBLANE), :], bi_ref[pl.ds(off, SUBLANE), :]
            for k, (ar, ai) in zip((1, 2, 4), mult):
                shift = (SUBLANE - k) if reverse else k
                sr, si = pltpu.roll(xr, shift, 0), pltpu.roll(xi, shift, 0)
                xr, xi = xr + ar * sr - ai * si, xi + ar * si + ai * sr
            xr, xi = xr + PR * cr - PI * ci, xi + PR * ci + PI * cr
            xr_ref[pl.ds(off, SUBLANE), :] = xr
            xi_ref[pl.ds(off, SUBLANE), :] = xi
            return xr[last:last + 1, :], xi[last:last + 1, :]

        z = jnp.zeros((1, W), F32)
        lax.fori_loop(0, nblk, step, (z, z))
        y = _bdot(xr_ref[...], cr_ref[...], "nn") + _bdot(xi_ref[...], ci_ref[...], "nn")
        y_ref[...] = y if res is None else y + rest[0][...]

    col = pl.BlockSpec((L, W), lambda j: (0, j))
    chan = pl.BlockSpec((L, SB), lambda j: (0, j))
    par = pl.BlockSpec((SUBLANE, W), lambda j: (0, j))
    w_in = pl.BlockSpec((None, SB, W), lambda j: (j // hp, j % hp, j % hp))
    w_out = pl.BlockSpec((None, W, SB), lambda j: (j // hp, j % hp, j % hp))
    return pl.pallas_call(
        body, grid=(nj,), in_specs=[chan, w_in, w_in, par, par, par, w_out, w_out] + ([] if res is None else [chan]),
        out_specs=[col, col, chan],
        out_shape=[jax.ShapeDtypeStruct((L, n), F32)] * 2 + [jax.ShapeDtypeStruct((L, S), F32)],
        compiler_params=_params(("parallel",), VMEM_LIMIT), name=name,
    )(src, w_re, w_im, a124, pr, pi, c_re, c_im, *([] if res is None else [res]))


def _s5_dabar(lr, li, xr, xi, name, W=256):
    L, n = lr.shape
    W = _tile(n, W)

    def body(lr_ref, li_ref, xr_ref, xi_ref, dar_ref, dai_ref):
        first = lax.broadcasted_iota(jnp.int32, (L, W), 0) == 0
        pr = jnp.where(first, 0.0, pltpu.roll(xr_ref[...], 1, 0))
        pi = jnp.where(first, 0.0, pltpu.roll(xi_ref[...], 1, 0))
        a, b = lr_ref[...], li_ref[...]
        dar_ref[...] = jnp.sum(a * pr + b * pi, axis=0, keepdims=True)
        dai_ref[...] = jnp.sum(b * pr - a * pi, axis=0, keepdims=True)

    col = pl.BlockSpec((L, W), lambda j: (0, j))
    one = pl.BlockSpec((1, W), lambda j: (0, j))
    return pl.pallas_call(
        body, grid=(n // W,), in_specs=[col] * 4, out_specs=[one, one],
        out_shape=[jax.ShapeDtypeStruct((1, n), F32)] * 2,
        compiler_params=_params(("parallel",), VMEM_LIMIT), name=name,
    )(lr, li, xr, xi)


def _bd_all(mats, name, deps=()):
    n = len(mats)

    def body(*refs):
        for w_ref, o_ref in zip(refs[:n], refs[n + len(deps):]):
            G, A, B = w_ref.shape
            rows, cols = S5_GB * A, S5_GB * B
            tile = jnp.where(lax.broadcasted_iota(jnp.int32, (B, cols), 1) % B == lax.broadcasted_iota(jnp.int32, (B, cols), 0),
                             1.0, 0.0).astype(BF16)
            keep = (lax.broadcasted_iota(jnp.int32, (rows, cols), 0) // A) == (lax.broadcasted_iota(jnp.int32, (rows, cols), 1) // B)
            for kb in range(G // S5_GB):
                w2 = w_ref[kb * S5_GB:(kb + 1) * S5_GB].reshape(rows, B)
                o_ref[kb] = jnp.where(keep, _bdot(w2, tile, "nn"), 0.0).astype(o_ref.dtype)

    return pl.pallas_call(
        body, out_shape=[jax.ShapeDtypeStruct((m.shape[0] // S5_GB, S5_GB * m.shape[1], S5_GB * m.shape[2]), BF16) for m in mats],
        compiler_params=pltpu.CompilerParams(vmem_limit_bytes=VMEM_LIMIT), name=name,
    )(*mats, *deps)


def _place():
    return lax.axis_index("x"), lax.axis_index("y"), lax.axis_index("c")


def _slot(p):
    return 4 * p[0] + 2 * p[1] + p[2]


N_CHIP = N_DEV // 2
_HBM = pl.BlockSpec(memory_space=pltpu.HBM)
_SEM = pl.BlockSpec(memory_space=pltpu.SEMAPHORE)
_ANY = pl.BlockSpec(memory_space=pl.ANY)
_EFFECT = pltpu.SideEffectType.DATAFLOW_SIDE_EFFECTING


def _gather_routes(x, y, c):
    me = _slot((x, y, c))
    peers = [(x, y, 1 - c)] + [(px, py, c) for px, py in ((1 - x, y), (x, 1 - y), (1 - x, 1 - y))]
    return [(p, me, me, _slot(p)) for p in peers]


def _chip_routes(x, y, c):
    myq = 2 * x + y
    return [((px, py, c), 2 * px + py, myq, 2 * px + py) for px, py in ((1 - x, y), (x, 1 - y), (1 - x, 1 - y))]


def _pass_routes(x, y, c):
    chips = ((1 - x, y), (x, 1 - y), (1 - x, 1 - y))
    return [((x, y, 1 - c), _slot((px, py, c)), _slot((px, py, c)), _slot((px, py, 1 - c))) for px, py in chips]


def _split_copies(s_refs, l_refs, send_sems, recv_sems, routes, arrival):
    out = []
    rts = routes(*_place())
    for a in range(len(l_refs)):
        for k, (peer, src_slot, dst_slot, arr_slot) in enumerate(rts):
            s_ref = l_refs[a] if s_refs is None else s_refs[a]
            src = s_ref if src_slot is None else s_ref.at[src_slot]
            sem = a * len(rts) + k
            out.append(pltpu.make_async_remote_copy(
                src_ref=src, dst_ref=l_refs[a].at[arr_slot if arrival else dst_slot], send_sem=send_sems.at[sem],
                recv_sem=recv_sems.at[sem], device_id=peer, device_id_type=MESH))
    return out


def _split_start(srcs, lands, routes, n_routes, dep, name):
    n, ns = len(lands), 0 if srcs is None else len(srcs)
    bufs = [*(srcs or ()), *lands]

    def body(*refs):
        s_refs = None if srcs is None else refs[:ns]
        for cp in _split_copies(s_refs, refs[ns:ns + n], refs[ns + n + 1], refs[ns + n + 2], routes, False):
            cp.start()
        refs[-1][...] = jnp.zeros_like(refs[-1])

    sems = pltpu.SemaphoreType.DMA((n * n_routes,))
    res = pl.pallas_call(
        body, name=name,
        out_shape=(sems, sems, *[pltpu.HBM(a.shape, a.dtype) for a in bufs], jax.ShapeDtypeStruct((SUBLANE, LANE), F32)),
        in_specs=[_HBM] * (ns + n) + [_ANY], out_specs=(_SEM, _SEM, *[_HBM] * (ns + n), pl.BlockSpec(memory_space=pltpu.VMEM)),
        input_output_aliases={i: 2 + i for i in range(ns + n)},
        compiler_params=pltpu.CompilerParams(has_side_effects=_EFFECT),
    )(*[pltpu.with_memory_space_constraint(a, pltpu.HBM) for a in bufs], dep)
    return res[0], res[1], (None if srcs is None else list(res[2:2 + ns])), list(res[2 + ns:2 + ns + n]), res[-1]


def _split_wait(send_sems, recv_sems, srcs, lands, routes, after, name):
    n, ns = len(lands), 0 if srcs is None else len(srcs)
    bufs = [*(srcs or ()), *lands]
    after = list(after) if isinstance(after, (list, tuple)) else [after]

    def body(*refs):
        s_refs = None if srcs is None else refs[:ns]
        for cp in _split_copies(s_refs, refs[ns:ns + n], refs[ns + n], refs[ns + n + 1], routes, True):
            cp.wait_send()
            cp.wait_recv()

    res = pl.pallas_call(
        body, name=name, out_shape=[pltpu.HBM(a.shape, a.dtype) for a in bufs],
        in_specs=[_HBM] * (ns + n) + [_SEM, _SEM] + [_ANY] * len(after), out_specs=[_HBM] * (ns + n),
        input_output_aliases={i: i for i in range(ns + n)},
        compiler_params=pltpu.CompilerParams(has_side_effects=_EFFECT),
    )(*bufs, send_sems, recv_sems, *after)
    return (None if srcs is None else list(res[:ns])), list(res[ns:])


def _to_slot(w, layer, place, out_dtype, name):
    _, R, C = w.shape
    unit = SUBLANE * (4 // jnp.dtype(out_dtype).itemsize)
    tr = _tile(R, max(unit, (512 * 1024 // C) // unit * unit), unit)

    def body(p_ref, w_ref, o_ref):
        o_ref[...] = w_ref[...].astype(o_ref.dtype)

    return pl.pallas_call(
        body,
        grid_spec=pltpu.PrefetchScalarGridSpec(
            num_scalar_prefetch=1, grid=(R // tr,),
            in_specs=[pl.BlockSpec((None, tr, C), lambda i, p: (layer, i, 0))],
            out_specs=pl.BlockSpec((None, tr, C), lambda i, p: (2 * p[1] + p[0], i, 0))),
        out_shape=jax.ShapeDtypeStruct((N_DEV, R, C), out_dtype),
        compiler_params=_params(("parallel",), VMEM_LIMIT), name=name,
    )(place, w)


def _pair_routes(x, y, c):
    return [((x, y, 1 - c), 2 * q + (1 - c), q, q) for q in range(N_CHIP)]


def _pair_sum(x, got, place, name):
    _, R, C = x.shape
    unit = SUBLANE * (4 // x.dtype.itemsize)
    tr = _tile(R, max(unit, (512 * 1024 // C) // unit * unit), unit)

    def body(p_ref, x_ref, g_ref, o_ref, land_ref):
        s = (x_ref[...].astype(F32) + g_ref[...].astype(F32)).astype(o_ref.dtype)
        o_ref[...] = s

        @pl.when(pl.program_id(1) == p_ref[1])
        def _():
            land_ref[...] = s

    blk = lambda f: pl.BlockSpec((None, tr, C), f)
    return pl.pallas_call(
        body,
        grid_spec=pltpu.PrefetchScalarGridSpec(
            num_scalar_prefetch=1, grid=(R // tr, N_CHIP),
            in_specs=[blk(lambda i, q, p: (2 * q + p[0], i, 0)), blk(lambda i, q, p: (q, i, 0))],
            out_specs=[blk(lambda i, q, p: (q, i, 0)), blk(lambda i, q, p: (p[1], i, 0))]),
        out_shape=[jax.ShapeDtypeStruct(got.shape, x.dtype)] * 2,
        compiler_params=_params(("parallel", "arbitrary"), VMEM_LIMIT), name=name,
    )(place, x, got)


def _adamw_math(w, g, m, v):
    m = ADAM_B1 * m + (1.0 - ADAM_B1) * g
    v = ADAM_B2 * v + (1.0 - ADAM_B2) * (g * g)
    m_hat = m / (1.0 - ADAM_B1 ** ADAM_STEP)
    v_hat = v / (1.0 - ADAM_B2 ** ADAM_STEP)
    return -ADAM_LR * (m_hat / (jnp.sqrt(v_hat) + ADAM_EPS) + ADAM_WD * w), m, v


def _adamw(w, m, v, parts, layer, prev, name):
    nl, R, C = w.shape
    P = parts.shape[0]
    unit = SUBLANE * (4 // parts.dtype.itemsize)
    tr = _tile(R, max(unit, (128 * 1024 // C) // unit * unit), unit)

    def body(w_ref, m_ref, v_ref, p_ref, *rest):
        g_ref, d_ref, nm_ref, nv_ref = rest[-4:]
        g = p_ref[0].astype(F32)
        for p in range(1, P):
            g = g + p_ref[p].astype(F32)
        d, nm, nv = _adamw_math(w_ref[...], g, m_ref[...], v_ref[...])
        g_ref[...], d_ref[...], nm_ref[...], nv_ref[...] = g, d, nm, nv

    lay = pl.BlockSpec((None, tr, C), lambda i: (layer, i, 0))
    in_specs = [lay, lay, lay, pl.BlockSpec((P, tr, C), lambda i: (0, i, 0))]
    ins = [w, m, v, parts]
    aliases = {}
    if prev is not None:
        in_specs += [pl.BlockSpec(memory_space=pl.ANY)] * 4
        ins += list(prev)
        aliases = {4 + k: k for k in range(4)}
    return pl.pallas_call(
        body, grid=(R // tr,), in_specs=in_specs, out_specs=[lay] * 4,
        out_shape=[jax.ShapeDtypeStruct(w.shape, F32)] * 4, input_output_aliases=aliases,
        compiler_params=_params(("parallel",), VMEM_LIMIT), name=name,
    )(*ins)


def _gelu_tanh(x):
    return jax.nn.gelu(x, approximate=True)


def _pack(arrs):
    tile = SUBLANE * LANE
    out = []
    for a in arrs:
        f = a.reshape(-1).astype(F32)
        out.append(jnp.pad(f, (0, (-f.shape[0]) % tile)))
    return jnp.concatenate(out)


def _unpack(flat, shapes):
    tile = SUBLANE * LANE
    out, off = [], 0
    for s in shapes:
        n = math.prod(s)
        out.append(flat[off:off + n].reshape(s))
        off += n + (-n) % tile
    return out


def kernel(x, gla_norm, gla_w_in, gla_w_gate_up, gla_b_gate, gla_o_norm, gla_w_out, s5_norm, s5_w_in, s5_lam_re, s5_lam_im, s5_log_dt, s5_b_re, s5_b_im, s5_c_re, s5_c_im, s5_d, s5_w_out, mlp_norm, mlp_w_up, mlp_w_down, final_norm, loss_target, m_gla_norm, m_gla_w_in, m_gla_w_gate_up, m_gla_b_gate, m_gla_o_norm, m_gla_w_out, m_s5_norm, m_s5_w_in, m_s5_lam_re, m_s5_lam_im, m_s5_log_dt, m_s5_b_re, m_s5_b_im, m_s5_c_re, m_s5_c_im, m_s5_d, m_s5_w_out, m_mlp_norm, m_mlp_w_up, m_mlp_w_down, m_final_norm, v_gla_norm, v_gla_w_in, v_gla_w_gate_up, v_gla_b_gate, v_gla_o_norm, v_gla_w_out, v_s5_norm, v_s5_w_in, v_s5_lam_re, v_s5_lam_im, v_s5_log_dt, v_s5_b_re, v_s5_b_im, v_s5_c_re, v_s5_c_im, v_s5_d, v_s5_w_out, v_mlp_norm, v_mlp_w_up, v_mlp_w_down, v_final_norm):
    W = dict(gla_norm=gla_norm, gla_w_in=gla_w_in, gla_w_gate_up=gla_w_gate_up, gla_b_gate=gla_b_gate, gla_o_norm=gla_o_norm, gla_w_out=gla_w_out, s5_norm=s5_norm, s5_w_in=s5_w_in, s5_lam_re=s5_lam_re, s5_lam_im=s5_lam_im, s5_log_dt=s5_log_dt, s5_b_re=s5_b_re, s5_b_im=s5_b_im, s5_c_re=s5_c_re, s5_c_im=s5_c_im, s5_d=s5_d, s5_w_out=s5_w_out, mlp_norm=mlp_norm, mlp_w_up=mlp_w_up, mlp_w_down=mlp_w_down, final_norm=final_norm)
    M = dict(gla_norm=m_gla_norm, gla_w_in=m_gla_w_in, gla_w_gate_up=m_gla_w_gate_up, gla_b_gate=m_gla_b_gate, gla_o_norm=m_gla_o_norm, gla_w_out=m_gla_w_out, s5_norm=m_s5_norm, s5_w_in=m_s5_w_in, s5_lam_re=m_s5_lam_re, s5_lam_im=m_s5_lam_im, s5_log_dt=m_s5_log_dt, s5_b_re=m_s5_b_re, s5_b_im=m_s5_b_im, s5_c_re=m_s5_c_re, s5_c_im=m_s5_c_im, s5_d=m_s5_d, s5_w_out=m_s5_w_out, mlp_norm=m_mlp_norm, mlp_w_up=m_mlp_w_up, mlp_w_down=m_mlp_w_down, final_norm=m_final_norm)
    V = dict(gla_norm=v_gla_norm, gla_w_in=v_gla_w_in, gla_w_gate_up=v_gla_w_gate_up, gla_b_gate=v_gla_b_gate, gla_o_norm=v_gla_o_norm, gla_w_out=v_gla_w_out, s5_norm=v_s5_norm, s5_w_in=v_s5_w_in, s5_lam_re=v_s5_lam_re, s5_lam_im=v_s5_lam_im, s5_log_dt=v_s5_log_dt, s5_b_re=v_s5_b_re, s5_b_im=v_s5_b_im, s5_c_re=v_s5_c_re, s5_c_im=v_s5_c_im, s5_d=v_s5_d, s5_w_out=v_s5_w_out, mlp_norm=v_mlp_norm, mlp_w_up=v_mlp_w_up, mlp_w_down=v_mlp_w_down, final_norm=v_final_norm)
    names = list(W)
    big = ["gla_w_in", "gla_w_out", "s5_w_in", "s5_w_out", "mlp_w_up", "mlp_w_down"]
    small_sharded = {"gla_w_gate_up": 2, "s5_norm": 1, "s5_d": 1}
    small = [n for n in names if n not in big]

    _, L, D = x.shape
    n_gla, n_s5, depth = gla_norm.shape[0], s5_lam_re.shape[0], mlp_norm.shape[0]
    H = GLA_HEADS
    KW, VW = D // 2, D
    DK, DV = KW // H, VW // H
    IN = 2 * KW + 2 * VW + GLA_RANK
    INP = 2 * KW + 2 * VW + LANE
    SW = s5_lam_re.shape[1] * S5_GROUP
    G, N = s5_lam_re.shape[1], s5_lam_re.shape[2]
    nb = G // S5_GB
    dev = _slot(_place())

    place = jnp.stack([lax.axis_index("c"), 2 * lax.axis_index("x") + lax.axis_index("y")]).astype(jnp.int32)
    sm_sh = _pack([W[n] for n in small_sharded]).reshape(1, -1, LANE)
    groups = []
    for i in range(depth):
        j = i // 2
        groups.append([("gla_w_in", j), ("gla_w_out", j)] if i % 2 == 0 else [("s5_w_in", j), ("s5_w_out", j)])
        groups.append([("mlp_w_up", i), ("mlp_w_down", i)])
    groups[0] = [("small", 0)] + groups[0]
    tok = jnp.zeros((SUBLANE, LANE), F32)
    level1, level2, full = [], {}, {}
    for gi, keys in enumerate(groups):
        lands = [_to_slot(sm_sh, 0, place, F32, "own_small") if n == "small" else _to_slot(W[n], j, place, BF16, f"own_{n}_{j}")
                 for n, j in keys]
        send_sems, recv_sems, _, lands, tok = _split_start(None, lands, _gather_routes, 4, tok, f"gather_start_{gi}")
        level1.append((send_sems, recv_sems, lands))

    def arrive(gi, after):
        send_sems, recv_sems, lands = level1[gi]
        _, lands = _split_wait(send_sems, recv_sems, None, lands, _gather_routes, after, f"gather_wait_{gi}")
        send_sems, recv_sems, _, lands, token = _split_start(None, lands, _pass_routes, 3, after, f"gather_pass_{gi}")
        level2[gi] = (send_sems, recv_sems, lands)
        return token

    def fetch(gi, after):
        send_sems, recv_sems, lands = level2[gi]
        _, lands = _split_wait(send_sems, recv_sems, None, lands, _pass_routes, after, f"gather_done_{gi}")
        full.update(zip(groups[gi], lands))

    s5p = []
    for j in range(n_s5):
        lre, lim = s5_lam_re[j][:, None, :], s5_lam_im[j][:, None, :]
        ldt = s5_log_dt[j][:, None, None]
        brt, bit = jnp.swapaxes(s5_b_re[j], 1, 2), jnp.swapaxes(s5_b_im[j], 1, 2)
        ar, ai, bbr, bbi, *mult = _s5_params(lre, lim, ldt, brt, bit, f"s5{j}_params")
        pr, pi, a124, qr, qi, c124 = [jnp.swapaxes(t, 0, 1).reshape(SUBLANE, G * N) for t in mult]
        sw = lambda t: jnp.swapaxes(t, 1, 2)
        bd = _bd_all([bbr, bbi, sw(s5_c_re[j]), -sw(s5_c_im[j]), s5_c_re[j], -s5_c_im[j], sw(bbr), sw(bbi)], f"s5{j}_blockdiag",
                     (tok,) if j == n_s5 - 1 else ())
        s5p.append(dict(prm=(lre, lim, ldt, brt, bit), pr=pr, pi=pi, a124=a124, qr=qr, qi=qi, c124=c124, bd=dict(zip(
            ("bu_re", "bu_im", "cx_re", "cx_im", "w_re", "w_im", "du_re", "du_im"), bd))))

    tok = arrive(0, s5p[-1]["bd"]["bu_re"] if s5p else tok)
    fetch(0, tok)
    sm_all = full["small", 0]
    sm_parts = [_unpack(sm_all[d].reshape(-1), [W[n].shape for n in small_sharded]) for d in range(N_DEV)]
    wgu_full = jnp.concatenate([p[0] for p in sm_parts], axis=2)
    s5n_full = jnp.concatenate([p[1] for p in sm_parts], axis=1)
    s5d_full = jnp.concatenate([p[2] for p in sm_parts], axis=1)

    def gla_weights(j):
        w_in = full["gla_w_in", j]
        w_in = jnp.transpose(w_in, (1, 0, 2)).reshape(D, IN)
        w_in = jnp.pad(w_in, ((0, 0), (0, INP - IN)))
        w_out = full["gla_w_out", j].reshape(VW, D)
        wgu = jnp.pad(wgu_full[j], ((0, LANE - GLA_RANK), (0, 0))).astype(BF16)
        return w_in, w_out, wgu

    grads = {}
    h = x[0]
    saved = []

    for i in range(depth):
        j = i // 2
        if i > 0:
            fetch(2 * i, h)
        if i % 2 == 0:
            w_in, w_out, wgu = gla_weights(j)
            gn = gla_norm[j][None]
            hn = _rms_fwd(h, gn, f"gla{j}_norm")
            proj = _mm(hn, w_in, "nn", name=f"gla{j}_proj", tn=896)
            ahead = arrive(2 * i + 1, proj)
            bg, on = gla_b_gate[j][None], gla_o_norm[j][None]
            y, st = _gla_fwd(proj, wgu, bg, on, H=H, DK=DK, DV=DV, name=f"gla{j}_mix", deps=(ahead,))
            h_new = _mm(y, w_out, "nn", name=f"gla{j}_out", res=h)
            saved.append(("gla", dict(h=h, hn=hn, proj=proj, y=y, st=st, w_in=w_in, w_out=w_out, wgu=wgu, gn=gn, bg=bg, on=on)))
        else:
            w_in = full["s5_w_in", j].reshape(D, SW)
            w_out3 = full["s5_w_out", j]
            sn = s5n_full[j][None]
            hn = _rms_fwd(h, sn, f"s5{j}_norm")
            u = _mm(hn, w_in, "nn", name=f"s5{j}_in")
            sp = s5p[j]
            bd = sp["bd"]
            xr, xi, cx = _s5_scan(u, bd["bu_re"], bd["bu_im"], sp["a124"], sp["pr"], sp["pi"], bd["cx_re"], bd["cx_im"], None,
                                  reverse=False, name=f"s5{j}_scan")
            ahead = arrive(2 * i + 1, xr)
            dsk = s5d_full[j][None]

            def act(cxv, uv, dv, *_):
                ypre = cxv + dv * uv
                return ypre, _gelu_tanh(ypre)
            ypre, yg = _rowmap(act, [cx, u], [dsk, ahead], [(SW, F32), (SW, BF16)], name=f"s5{j}_act")
            z = _mm_nn_cb(yg, w_out3, name=f"s5{j}_out")

            def glu(zv, hv):
                return (hv + zv[:, :D] * jax.nn.sigmoid(zv[:, D:]),)
            h_new = _rowmap(glu, [z, h], [], [(D, F32)], name=f"s5{j}_glu")[0]
            saved.append(("s5", dict(h=h, hn=hn, u=u, xr=xr, xi=xi, ypre=ypre, yg=yg, z=z, w_in=w_in, w_out3=w_out3, sn=sn,
                                     dsk=dsk, **sp)))
        h = h_new
        fetch(2 * i + 1, h)
        w_up3 = full["mlp_w_up", i]
        w_down = full["mlp_w_down", i].reshape(4 * D, D)
        mn = mlp_norm[i][None]
        hn = _rms_fwd(h, mn, f"mlp{i}_norm")
        def sq_relu(zv):
            a = jnp.maximum(zv, 0.0)
            return zv, a * a
        z, s = _mm_nn_cb(hn, w_up3, name=f"mlp{i}_up", out_dtype=[F32, BF16], epi=sq_relu)
        ahead = (arrive(2 * i + 2, s),) if i + 1 < depth else ()
        h_new = _mm(s, w_down, "nn", name=f"mlp{i}_down", res=h, deps=ahead)
        saved.append(("mlp", dict(h=h, hn=hn, z=z, s=s, w_up3=w_up3, w_down=w_down, mn=mn)))
        h = h_new

    dh, sq, dfin = _loss_head(h, loss_target[0], final_norm[None], "loss_head")
    loss = lax.psum(0.5 * jnp.sum(sq) / D, ("x", "y", "c"))
    small_grads = {"final_norm": dfin[0]}
    big_parts = {n: {} for n in big}
    stacks = {n: {} for n in small if n != "final_norm"}
    flat3 = lambda a: a.reshape(a.shape[0], math.prod(a.shape[1:-1]), a.shape[-1])
    exchanges = []

    def pair_grads(keys):
        tag = "_".join(f"{n}{j}" for n, j in keys)
        parts8 = [flat3(big_parts[n][j]) for n, j in keys]
        lands = [lax.empty((N_CHIP, *p.shape[1:]), p.dtype) for p in parts8]
        send_sems, recv_sems, parts8, lands, token = _split_start(parts8, lands, _pair_routes, N_CHIP, place, f"grads_pair_{tag}")
        return (keys, tag, send_sems, recv_sems, parts8, lands), token

    def send_grads(pending, after):
        keys, tag, send_sems, recv_sems, parts8, lands = pending
        parts8, from_sib = _split_wait(send_sems, recv_sems, parts8, lands, _pair_routes, after, f"grads_paired_{tag}")
        sums, lands = zip(*[_pair_sum(p, g, place, f"grads_pair_sum_{n}{j}") for (n, j), p, g in zip(keys, parts8, from_sib)])
        send_sems, recv_sems, srcs, lands, token = _split_start(list(sums), list(lands), _chip_routes, 3, place, f"grads_start_{tag}")
        exchanges.append((keys, tag, send_sems, recv_sems, srcs, lands))
        return token

    carry = ()

    outs = {}
    s5_small = [n for n in small if n.startswith("s5_") and n not in small_sharded]

    def small_start(subset, tag):
        for n in subset:
            if n in stacks:
                small_grads[n] = jnp.stack([stacks[n][k] for k in range(len(stacks[n]))])
        part = _pack([small_grads[n] for n in subset]).reshape(1, -1, LANE)
        lands = [_to_slot(part, 0, place, F32, f"own_small_grads_{tag}")]
        send_sems, recv_sems, _, lands, _ = _split_start(None, lands, _gather_routes, 4, place, f"small_grads_start_{tag}")
        return subset, tag, send_sems, recv_sems, lands

    def small_finish(handle, after):
        subset, tag, send_sems, recv_sems, lands = handle
        _, lands = _split_wait(send_sems, recv_sems, None, lands, _gather_routes, after, f"small_grads_wait_{tag}")
        send_sems, recv_sems, _, lands, t = _split_start(None, lands, _pass_routes, 3, after, f"small_grads_pass_{tag}")
        _, lands = _split_wait(send_sems, recv_sems, None, lands, _pass_routes, t, f"small_grads_done_{tag}")
        parts_all = lands[0]
        rows = parts_all.shape[1]
        shapes = [small_grads[n].shape for n in subset]

        def full_layout(T):
            arrs = []
            for n, shp in zip(subset, shapes):
                if n in small_sharded:
                    ax = small_sharded[n]
                    arrs.append(lax.dynamic_update_slice_in_dim(jnp.zeros(shp, F32), T[n], dev * T[n].shape[ax], axis=ax))
                else:
                    arrs.append(T[n])
            return _pack(arrs).reshape(1, rows, LANE)

        res = _adamw(full_layout(W), full_layout(M), full_layout(V), parts_all, 0, None, f"adamw_small_{tag}")
        for key, flat in zip(("grad", "delta", "new_m", "new_v"), res):
            for n, a in zip(subset, _unpack(flat.reshape(-1), shapes)):
                if n in small_sharded:
                    ax = small_sharded[n]
                    a = lax.dynamic_slice_in_dim(a, dev * W[n].shape[ax], W[n].shape[ax], axis=ax)
                outs[key, n] = a
        return res[0]

    for idx in range(len(saved) - 1, -1, -1):
        kind, s = saved[idx]
        li = sum(1 for k, _ in saved[:idx] if k == kind)
        if kind == "mlp":
            def sq_relu_bwd(dsv, zv):
                return (dsv * 2.0 * jnp.maximum(zv, 0.0),)
            dz = _mm(dh, s["w_down"], "nt", name=f"mlp{li}_dz", out_dtype=BF16, epi=sq_relu_bwd, epi_ins=(s["z"],), deps=carry)
            big_parts["mlp_w_down"][li] = _mm(s["s"], dh, "tn", name=f"mlp{li}_dw_down", out_dtype=BF16).reshape(N_DEV, 4 * D // N_DEV, D)
            big_parts["mlp_w_up"][li] = _mm_tn_cbout(s["hn"], dz, N_DEV, name=f"mlp{li}_dw_up")
            pending, token = pair_grads([("mlp_w_down", li), ("mlp_w_up", li)])
            dhn = _mm_nt_cb(dz, s["w_up3"], name=f"mlp{li}_dhn")
            dh, dg = _rms_bwd(s["h"], dhn, dh, s["mn"], f"mlp{li}_dnorm", deps=(token,))
            stacks["mlp_norm"][li] = dg[0]
        elif kind == "gla":
            dyv = _mm(dh, s["w_out"], "nt", name=f"gla{li}_dy", deps=carry)
            big_parts["gla_w_out"][li] = _mm(s["y"], dh, "tn", name=f"gla{li}_dw_out", out_dtype=BF16).reshape(N_DEV, VW // N_DEV, D)
            dproj, don, dbg, dwgu = _gla_bwd(s["proj"], s["wgu"], s["bg"], s["on"], s["st"], dyv, H=H, DK=DK, DV=DV, name=f"gla{li}_dmix")
            dwgu = dwgu[:GLA_RANK]
            dw_in = _mm(s["hn"], dproj, "tn", name=f"gla{li}_dw_in", out_dtype=BF16, tn=896)[:, :IN]
            big_parts["gla_w_in"][li] = jnp.transpose(dw_in.reshape(D, N_DEV, IN // N_DEV), (1, 0, 2))
            pending, token = pair_grads([("gla_w_out", li), ("gla_w_in", li)])
            dhn = _mm(dproj, s["w_in"], "nt", name=f"gla{li}_dhn", tk=896)
            dh, dg = _rms_bwd(s["h"], dhn, dh, s["gn"], f"gla{li}_dnorm", deps=(token,))
            stacks["gla_norm"][li], stacks["gla_b_gate"][li], stacks["gla_o_norm"][li] = dg[0], dbg[0], don[0]
            stacks["gla_w_gate_up"][li] = dwgu
        else:
            def glu_bwd(zv, dhv, *_):
                a, sg = zv[:, :D], jax.nn.sigmoid(zv[:, D:])
                return (jnp.concatenate([dhv * sg, dhv * a * sg * (1.0 - sg)], axis=1),)
            dz = _rowmap(glu_bwd, [s["z"], dh], list(carry), [(2 * D, BF16)], name=f"s5{li}_dglu")[0]
            big_parts["s5_w_out"][li] = _mm_tn_cbout(s["yg"], dz, N_DEV, name=f"s5{li}_dw_out")
            dyg = _mm_nt_cb(dz, s["w_out3"], name=f"s5{li}_dyg")

            def act_bwd(dygv, ypv, uv, dv):
                _, vjp = jax.vjp(_gelu_tanh, ypv)
                dyp = vjp(dygv)[0]
                return dyp, dyp * dv, jnp.sum(dyp * uv, axis=0, keepdims=True)
            dyp, du, dd = _rowmap(act_bwd, [dyg, s["ypre"], s["u"]], [s["dsk"]], [(SW, F32), (SW, F32)], [SW], name=f"s5{li}_dact")
            bd = s["bd"]
            lr_, li_, du = _s5_scan(dyp, bd["w_re"], bd["w_im"], s["c124"], s["qr"], s["qi"], bd["du_re"], bd["du_im"], du,
                                    reverse=True, name=f"s5{li}_dscan")
            dbbr = _mm_tn_bd(s["u"], lr_, nb, S5_GROUP, N, name=f"s5{li}_dbb_re")
            dbbi = _mm_tn_bd(s["u"], li_, nb, S5_GROUP, N, name=f"s5{li}_dbb_im")
            dcr = _mm_tn_bd(dyp, s["xr"], nb, S5_GROUP, N, name=f"s5{li}_dc_re")
            dci = -_mm_tn_bd(dyp, s["xi"], nb, S5_GROUP, N, name=f"s5{li}_dc_im")
            dar, dai = _s5_dabar(lr_, li_, s["xr"], s["xi"], f"s5{li}_dabar")
            dlre, dlim, dldt, dbrt, dbit = _s5_params_bwd(*s["prm"], dar.reshape(G, 1, N), dai.reshape(G, 1, N), dbbr, dbbi,
                                                         f"s5{li}_dparams")
            big_parts["s5_w_in"][li] = _mm(s["hn"], du, "tn", name=f"s5{li}_dw_in", out_dtype=BF16).reshape(N_DEV, D // N_DEV, SW)
            pending, token = pair_grads([("s5_w_out", li), ("s5_w_in", li)])
            dhn = _mm(du, s["w_in"], "nt", name=f"s5{li}_dhn")
            dh, dg = _rms_bwd(s["h"], dhn, dh, s["sn"], f"s5{li}_dnorm", deps=(token,))
            stacks["s5_norm"][li], stacks["s5_d"][li] = dg[0], dd[0]
            stacks["s5_lam_re"][li], stacks["s5_lam_im"][li], stacks["s5_log_dt"][li] = dlre[:, 0], dlim[:, 0], dldt[:, 0, 0]
            stacks["s5_b_re"][li], stacks["s5_b_im"][li] = jnp.swapaxes(dbrt, 1, 2), jnp.swapaxes(dbit, 1, 2)
            stacks["s5_c_re"][li], stacks["s5_c_im"][li] = dcr, dci
            if li == 0:
                s5_handle = small_start(s5_small, "s5")
        carry = (send_grads(pending, dh),)
    grad_x = dh[None]

    rest = small_start([n for n in small if n not in s5_small], "rest")

    stacked = {n: None for n in big}
    as3 = lambda a: a.reshape(a.shape[0], math.prod(a.shape[1:-1]), a.shape[-1])

    def finish(exchange, after):
        keys, tag, e_send, e_recv, e_srcs, e_lands = exchange
        _, e_lands = _split_wait(e_send, e_recv, e_srcs, e_lands, _chip_routes, after, f"grads_wait_{tag}")
        for (n, j), parts in zip(keys, e_lands):
            stacked[n] = _adamw(as3(W[n]), as3(M[n]), as3(V[n]), parts, j, stacked[n], f"adamw_{n}_{j}")
        return stacked[keys[0][0]][0]

    done = [small_finish(s5_handle, carry[0])]
    for exchange in exchanges[:-1]:
        finish(exchange, carry[0])
    done.append(small_finish(rest, carry[0]))
    last_keys = {n for n, _ in exchanges[-1][0]}
    finish(exchanges[-1], done + [stacked[n][0] for n in big if n not in last_keys])
    for n in big:
        for key, a in zip(("grad", "delta", "new_m", "new_v"), stacked[n]):
            outs[key, n] = a.reshape(W[n].shape)

    return (loss, grad_x, *[outs["grad", n] for n in names], *[outs["delta", n] for n in names],
            *[outs["new_m", n] for n in names], *[outs["new_v", n] for n in names])
```

```python
import functools
import math

import jax
import jax.numpy as jnp
from jax import lax
from jax.experimental import pallas as pl
from jax.experimental.pallas import tpu as pltpu

F32, BF16 = jnp.float32, jnp.bfloat16
MESH = pl.DeviceIdType.MESH
N_DEV = 8
LANE = 128
SUBLANE = 8
VMEM_LIMIT = 48 * 1024 * 1024

EPS = 1e-6
CHUNK = 64
GLA_HEADS = 4
GLA_RANK = 16
GLA_TEMP = 16.0
S5_GROUP = 16
S5_STATE = 64
S5_EIG_CLIP = -1e-4
S5_GB = 16
ADAM_LR, ADAM_B1, ADAM_B2, ADAM_EPS, ADAM_WD, ADAM_STEP = 0.001, 0.9, 0.999, 1e-08, 0.01, 10

_CONTRACT = {"nn": ((1,), (0,)), "tn": ((0,), (0,)), "nt": ((1,), (1,))}


def _tile(n, pref, unit=LANE):
    if n <= pref:
        return n
    t = (pref // unit) * unit
    while t > unit and n % t:
        t -= unit
    assert n % t == 0, (n, pref, unit)
    return t


def _params(sem, vmem=None):
    return pltpu.CompilerParams(dimension_semantics=sem, vmem_limit_bytes=vmem)


def _dot(a, b, dims, precision=None):
    return lax.dot_general(a, b, (_CONTRACT[dims], ((), ())), preferred_element_type=F32, precision=precision)


def _bdot(a, b, dims):
    return _dot(a.astype(BF16), b.astype(BF16), dims)


def _mm_call(a, b, *, dims, grid, a_spec, b_spec, o_spec, out_shape, out_dtype, name, res=None, epi=None, epi_ins=(), deps=(),
             acc_shape=None):
    nk = grid[2]
    acc_shape = acc_shape or tuple(d for d in o_spec.block_shape if d is not None)
    if res is not None:
        epi, epi_ins = (lambda r, x: (r + x,)), (res,)
    single = not isinstance(out_dtype, (list, tuple))
    out_dtypes = [out_dtype] if single else list(out_dtype)
    n_e, n_o, n_d = len(epi_ins), len(out_dtypes), len(deps)

    def body(*refs):
        a_ref, b_ref = refs[:2]
        e_refs, o_refs = refs[2:2 + n_e], refs[2 + n_e + n_d:2 + n_e + n_d + n_o]

        def finish(r):
            vals = (r,) if epi is None else epi(r, *[e[...].astype(F32) for e in e_refs])
            for o_ref, v in zip(o_refs, vals):
                o_ref[...] = v.astype(o_ref.dtype)

        d = _bdot(a_ref[...], b_ref[...], dims)
        if nk == 1:
            finish(d)
            return
        acc = refs[-1]
        k = pl.program_id(2)

        @pl.when(k == 0)
        def _():
            acc[...] = d

        @pl.when((k > 0) & (k < nk - 1))
        def _():
            acc[...] += d

        @pl.when(k == nk - 1)
        def _():
            finish(acc[...] + d)

    outs = pl.pallas_call(
        body, grid=grid, in_specs=[a_spec, b_spec] + [o_spec] * n_e + [pl.BlockSpec(memory_space=pl.ANY)] * n_d,
        out_specs=[o_spec] * n_o, out_shape=[jax.ShapeDtypeStruct(out_shape, dt) for dt in out_dtypes],
        scratch_shapes=[] if nk == 1 else [pltpu.VMEM(acc_shape, F32)],
        compiler_params=_params(("parallel", "parallel", "arbitrary"), VMEM_LIMIT), name=name,
    )(a, b, *epi_ins, *deps)
    return outs[0] if single else outs


TM, TN, TK = 1024, 1024, 2048


def _mm(a, b, dims, *, name, out_dtype=F32, res=None, epi=None, epi_ins=(), deps=(), tm=TM, tn=TN, tk=TK):
    if dims == "tn":
        (K, M), (_, N) = a.shape, b.shape
    elif dims == "nn":
        (M, K), (_, N) = a.shape, b.shape
    else:
        (M, K), (N, _) = a.shape, b.shape
    tm, tn, tk = _tile(M, tm), _tile(N, tn), _tile(K, tk)
    a_spec = pl.BlockSpec((tk, tm), lambda i, j, k: (k, i)) if dims == "tn" else pl.BlockSpec((tm, tk), lambda i, j, k: (i, k))
    b_spec = pl.BlockSpec((tn, tk), lambda i, j, k: (j, k)) if dims == "nt" else pl.BlockSpec((tk, tn), lambda i, j, k: (k, j))
    o_spec = pl.BlockSpec((tm, tn), lambda i, j, k: (i, j))
    return _mm_call(a, b, dims=dims, grid=(M // tm, N // tn, K // tk), a_spec=a_spec, b_spec=b_spec, o_spec=o_spec,
                    out_shape=(M, N), out_dtype=out_dtype, name=name, res=res, epi=epi, epi_ins=epi_ins, deps=deps)


def _mm_nn_cb(a, b3, *, name, out_dtype=F32, epi=None, tm=TM, tn=TN, tk=TK):
    (M, K), (P, _, Ns) = a.shape, b3.shape
    tm, tn, tk = _tile(M, tm), _tile(Ns, tn), _tile(K, tk)
    npb = Ns // tn
    return _mm_call(a, b3, dims="nn", grid=(M // tm, P * npb, K // tk),
                    a_spec=pl.BlockSpec((tm, tk), lambda i, j, k: (i, k)),
                    b_spec=pl.BlockSpec((None, tk, tn), lambda i, j, k: (j // npb, k, j % npb)),
                    o_spec=pl.BlockSpec((tm, tn), lambda i, j, k: (i, j)),
                    out_shape=(M, P * Ns), out_dtype=out_dtype, name=name, epi=epi)


def _mm_nt_cb(a, b3, *, name, out_dtype=F32, tm=TM, tn=TN, tk=TK):
    (M, _), (P, N, Ns) = a.shape, b3.shape
    tm, tn, tk = _tile(M, tm), _tile(N, tn), _tile(Ns, tk)
    kpb = Ns // tk
    return _mm_call(a, b3, dims="nt", grid=(M // tm, N // tn, P * kpb),
                    a_spec=pl.BlockSpec((tm, tk), lambda i, j, k: (i, k)),
                    b_spec=pl.BlockSpec((None, tn, tk), lambda i, j, k: (k // kpb, j, k % kpb)),
                    o_spec=pl.BlockSpec((tm, tn), lambda i, j, k: (i, j)),
                    out_shape=(M, N), out_dtype=out_dtype, name=name)


def _mm_tn_cbout(a, b, parts, *, name, out_dtype=BF16, tm=TM, tn=TN, tk=TK):
    (K, M), (_, N) = a.shape, b.shape
    Ns = N // parts
    tm, tn, tk = _tile(M, tm), _tile(Ns, tn), _tile(K, tk)
    npb = Ns // tn
    return _mm_call(a, b, dims="tn", grid=(M // tm, parts * npb, K // tk),
                    a_spec=pl.BlockSpec((tk, tm), lambda i, j, k: (k, i)),
                    b_spec=pl.BlockSpec((tk, tn), lambda i, j, k: (k, j)),
                    o_spec=pl.BlockSpec((None, tm, tn), lambda i, j, k: (j // npb, i, j % npb)),
                    out_shape=(parts, M, Ns), out_dtype=out_dtype, name=name)


def _mm_tn_bd(a, b, nb, A, B, *, name, tk=1024):
    (K, MA), (_, NB) = a.shape, b.shape
    Ma, Nb = MA // nb, NB // nb
    tk = _tile(K, tk)

    def diagonal(r):
        keep = (lax.broadcasted_iota(jnp.int32, (Ma, Nb), 0) // A) == (lax.broadcasted_iota(jnp.int32, (Ma, Nb), 1) // B)
        fold = jnp.where(lax.broadcasted_iota(jnp.int32, (Nb, B), 0) % B == lax.broadcasted_iota(jnp.int32, (Nb, B), 1), 1.0, 0.0)
        return (_dot(jnp.where(keep, r, 0.0), fold, "nn", lax.Precision.HIGHEST),)

    out = _mm_call(a, b, dims="tn", grid=(nb, 1, K // tk),
                   a_spec=pl.BlockSpec((tk, Ma), lambda i, j, k: (k, i)),
                   b_spec=pl.BlockSpec((tk, Nb), lambda i, j, k: (k, i)),
                   o_spec=pl.BlockSpec((None, Ma, B), lambda i, j, k: (i, 0, 0)),
                   out_shape=(nb, Ma, B), out_dtype=F32, name=name, epi=diagonal, acc_shape=(Ma, Nb))
    return out.reshape(nb * (Ma // A), A, B)


def _rowmap(fn, rows, consts, out_defs, red_defs=(), *, name, tr=256):
    L = rows[0].shape[0]
    widest = max([r.shape[1] for r in rows] + [n for n, _ in out_defs])
    tr = _tile(L, max(SUBLANE * 2, min(tr, 512 * 1024 // widest)), SUBLANE * 2)
    n_in, n_o, n_d = len(rows) + len(consts), len(out_defs), len(red_defs)

    def body(*refs):
        res = fn(*[r[...] for r in refs[:n_in]])
        res = res if isinstance(res, (tuple, list)) else (res,)
        outs = refs[n_in:]
        for o_ref, val in zip(outs[:n_o], res[:n_o]):
            o_ref[...] = val.astype(o_ref.dtype)
        if n_d:
            @pl.when(pl.program_id(0) == 0)
            def _():
                for o_ref in outs[n_o:]:
                    o_ref[...] = jnp.zeros_like(o_ref)
            for o_ref, val in zip(outs[n_o:], res[n_o:]):
                o_ref[...] += val

    in_specs = [pl.BlockSpec((tr, r.shape[1]), lambda i: (i, 0)) for r in rows]
    in_specs += [pl.BlockSpec(c.shape, lambda i, nd=c.ndim: (0,) * nd) for c in consts]
    out_specs = [pl.BlockSpec((tr, n), lambda i: (i, 0)) for n, _ in out_defs]
    out_specs += [pl.BlockSpec((1, n), lambda i: (0, 0)) for n in red_defs]
    out_shape = [jax.ShapeDtypeStruct((L, n), dt) for n, dt in out_defs]
    out_shape += [jax.ShapeDtypeStruct((1, n), F32) for n in red_defs]
    return pl.pallas_call(body, grid=(L // tr,), in_specs=in_specs, out_specs=out_specs, out_shape=out_shape,
                          compiler_params=_params(("arbitrary",), VMEM_LIMIT), name=name)(*rows, *consts)


def _rms_parts(x):
    r = lax.rsqrt(jnp.mean(x * x, axis=-1, keepdims=True) + EPS)
    return r, x * r


def _rms_fwd(h, g, name, deps=()):
    def fn(x, gg, *_):
        _, xh = _rms_parts(x)
        return (xh * gg,)
    return _rowmap(fn, [h], [g, *deps], [(h.shape[1], BF16)], name=name)[0]


def _rms_bwd(h, dhn, dh, g, name, deps=()):
    def fn(x, dy, dres, gg, *_):
        r, xh = _rms_parts(x)
        dxh = dy * gg
        dx = r * (dxh - xh * jnp.mean(dxh * xh, axis=-1, keepdims=True))
        return dres + dx, jnp.sum(dy * xh, axis=0, keepdims=True)
    D = h.shape[1]
    return _rowmap(fn, [h, dhn, dh], [g, *deps], [(D, F32)], [D], name=name)


def _loss_head(h, tgt, g, name):
    D = h.shape[1]

    def fn(x, t, gg):
        r, xh = _rms_parts(x)
        diff = xh * gg - t
        dy = diff * (1.0 / D)
        dxh = dy * gg
        dx = r * (dxh - xh * jnp.mean(dxh * xh, axis=-1, keepdims=True))
        return dx, jnp.sum(diff * diff, axis=0, keepdims=True), jnp.sum(dy * xh, axis=0, keepdims=True)
    return _rowmap(fn, [h, tgt], [g], [(D, F32)], [D, D], name=name)


def _tri(n, strict):
    r = lax.broadcasted_iota(jnp.int32, (n, n), 0)
    c = lax.broadcasted_iota(jnp.int32, (n, n), 1)
    return jnp.where((c < r) if strict else (c <= r), 1.0, 0.0).astype(F32)


def _gla_gate(g_ref, wgu_ref, bg_ref):
    pre = _bdot(g_ref[...], wgu_ref[...], "nn") + bg_ref[...]
    la = (jnp.minimum(pre, 0.0) - jnp.log(1.0 + jnp.exp(-jnp.abs(pre)))) * (1.0 / GLA_TEMP)
    cum = _dot(_tri(CHUNK, False), la, "nn", lax.Precision.HIGHEST)
    return pre, cum, cum[CHUNK - 1:CHUNK, :]


def _gla_specs(H, DK, DV, cmap):
    KW, VW = H * DK, H * DV
    assert (2 * KW) % VW == 0 and (2 * KW + 2 * VW) % LANE == 0
    vb, gb = (2 * KW) // VW, (2 * KW + 2 * VW) // LANE
    return [
        pl.BlockSpec((CHUNK, KW), lambda c: (cmap(c), 0)),
        pl.BlockSpec((CHUNK, KW), lambda c: (cmap(c), 1)),
        pl.BlockSpec((CHUNK, VW), lambda c: (cmap(c), vb)),
        pl.BlockSpec((CHUNK, VW), lambda c: (cmap(c), vb + 1)),
        pl.BlockSpec((CHUNK, LANE), lambda c: (cmap(c), gb)),
        pl.BlockSpec((LANE, KW), lambda c: (0, 0)),
        pl.BlockSpec((1, KW), lambda c: (0, 0)),
        pl.BlockSpec((1, DV), lambda c: (0, 0)),
    ]


def _gla_fwd(proj, wgu, bg, on, *, H, DK, DV, name, deps=()):
    L = proj.shape[0]
    nc = L // CHUNK
    scale = DK ** -0.5
    n_d = len(deps)

    def body(q_ref, k_ref, v_ref, r_ref, g_ref, wgu_ref, bg_ref, on_ref, *rest):
        y_ref, st_ref, S = rest[n_d:]

        @pl.when(pl.program_id(0) == 0)
        def _():
            S[...] = jnp.zeros_like(S)
        _, cum, total = _gla_gate(g_ref, wgu_ref, bg_ref)
        kd = k_ref[...] * jnp.exp(total - cum)
        dec = jnp.exp(total)
        for h in range(H):
            ks, vs = slice(h * DK, (h + 1) * DK), slice(h * DV, (h + 1) * DV)
            St = S[h] * dec[:, ks] + _bdot(v_ref[:, vs], kd[:, ks], "tn")
            S[h] = St
            st_ref[h] = St
            o = _bdot(q_ref[:, ks] * scale, St, "nt")
            _, oh = _rms_parts(o)
            y_ref[:, vs] = (oh * on_ref[...] * jax.nn.silu(r_ref[:, vs])).astype(y_ref.dtype)

    return pl.pallas_call(
        body, grid=(nc,), in_specs=_gla_specs(H, DK, DV, lambda c: c) + [pl.BlockSpec(memory_space=pl.ANY)] * n_d,
        out_specs=[pl.BlockSpec((CHUNK, H * DV), lambda c: (c, 0)),
                   pl.BlockSpec((H, None, DV, DK), lambda c: (0, c, 0, 0))],
        out_shape=[jax.ShapeDtypeStruct((L, H * DV), BF16), jax.ShapeDtypeStruct((H, nc, DV, DK), F32)],
        scratch_shapes=[pltpu.VMEM((H, DV, DK), F32)],
        compiler_params=_params(("arbitrary",), VMEM_LIMIT), name=name,
    )(proj, proj, proj, proj, proj, wgu, bg, on, *deps)


def _gla_bwd(proj, wgu, bg, on, st, dy, *, H, DK, DV, name):
    L = proj.shape[0]
    nc = L // CHUNK
    KW, VW = H * DK, H * DV
    scale = DK ** -0.5
    rev = lambda c: nc - 1 - c

    def body(q_ref, k_ref, v_ref, r_ref, g_ref, wgu_ref, bg_ref, on_ref, sc_ref, sp_ref, dy_ref,
             dp_ref, don_ref, dbg_ref, dwgu_ref, G, decn):
        c = pl.program_id(0)

        @pl.when(c == 0)
        def _():
            for ref in (G, decn, don_ref, dbg_ref, dwgu_ref):
                ref[...] = jnp.zeros_like(ref)

        pre, cum, total = _gla_gate(g_ref, wgu_ref, bg_ref)
        ex = jnp.exp(total - cum)
        kd = k_ref[...] * ex
        dec = jnp.exp(total)
        gn = on_ref[...]
        alive = jnp.where(c == nc - 1, 0.0, 1.0)
        don = jnp.zeros_like(gn)
        ddec, dkd = [], []
        for h in range(H):
            ks, vs = slice(h * DK, (h + 1) * DK), slice(h * DV, (h + 1) * DV)
            qs = q_ref[:, ks] * scale
            Sc = sc_ref[h]
            o = _bdot(qs, Sc, "nt")
            rinv, oh = _rms_parts(o)
            r = r_ref[:, vs]
            sg = jax.nn.sigmoid(r)
            dyv = dy_ref[:, vs]
            d_on = dyv * (r * sg)
            dp_ref[:, 2 * KW + VW + h * DV:2 * KW + VW + (h + 1) * DV] = (
                dyv * (oh * gn) * (sg * (1.0 + r * (1.0 - sg)))).astype(dp_ref.dtype)
            don = don + jnp.sum(d_on * oh, axis=0, keepdims=True)
            dxh = d_on * gn
            do = rinv * (dxh - oh * jnp.mean(dxh * oh, axis=-1, keepdims=True))
            dp_ref[:, ks] = (_bdot(do, Sc, "nn") * scale).astype(dp_ref.dtype)
            Gt = G[h] * decn[:, ks] + _bdot(do, qs, "tn")
            G[h] = Gt
            dkd.append(_bdot(v_ref[:, vs], Gt, "nn"))
            dp_ref[:, 2 * KW + h * DV:2 * KW + (h + 1) * DV] = _bdot(kd[:, ks], Gt, "nt").astype(dp_ref.dtype)
            ddec.append(jnp.sum(Gt * (sp_ref[h] * alive), axis=0, keepdims=True))
        don_ref[...] += don
        decn[...] = dec
        dkd, ddec = jnp.concatenate(dkd, axis=1), jnp.concatenate(ddec, axis=1)
        dp_ref[:, KW:2 * KW] = (dkd * ex).astype(dp_ref.dtype)
        dla = ddec * dec + _dot(_tri(CHUNK, True), dkd * kd, "nn", lax.Precision.HIGHEST)
        dpre = dla * (1.0 / GLA_TEMP) * jax.nn.sigmoid(-pre)
        dbg_ref[...] += jnp.sum(dpre, axis=0, keepdims=True)
        dwgu_ref[...] += _bdot(g_ref[...], dpre, "tn")
        dp_ref[:, 2 * KW + 2 * VW:] = _bdot(dpre, wgu_ref[...], "nt").astype(dp_ref.dtype)

    in_specs = _gla_specs(H, DK, DV, rev) + [
        pl.BlockSpec((H, None, DV, DK), lambda c: (0, rev(c), 0, 0)),
        pl.BlockSpec((H, None, DV, DK), lambda c: (0, jnp.maximum(rev(c) - 1, 0), 0, 0)),
        pl.BlockSpec((CHUNK, VW), lambda c: (rev(c), 0)),
    ]
    W = 2 * KW + 2 * VW + LANE
    whole = lambda shp: pl.BlockSpec(shp, lambda c: (0, 0))
    sds = jax.ShapeDtypeStruct
    return pl.pallas_call(
        body, grid=(nc,), in_specs=in_specs,
        out_specs=[pl.BlockSpec((CHUNK, W), lambda c: (rev(c), 0)), whole((1, DV)), whole((1, KW)), whole((LANE, KW))],
        out_shape=[sds((L, W), BF16), sds((1, DV), F32), sds((1, KW), F32), sds((LANE, KW), F32)],
        scratch_shapes=[pltpu.VMEM((H, DV, DK), F32), pltpu.VMEM((1, KW), F32)],
        compiler_params=_params(("arbitrary",), VMEM_LIMIT), name=name,
    )(proj, proj, proj, proj, proj, wgu, bg, on, st, st, dy)


def _s5_param_fn(lam_re, lam_im, log_dt, brt, bit):
    lr = jnp.minimum(lam_re, S5_EIG_CLIP)
    li = lam_im
    dt = jnp.exp(log_dt)
    mag = jnp.exp(lr * dt)
    ang = li * dt
    ab_re = mag * jnp.cos(ang)
    ab_im = mag * jnp.sin(ang)
    den = lr * lr + li * li
    nr = ab_re - 1.0
    f_re = (nr * lr + ab_im * li) / den
    f_im = (ab_im * lr - nr * li) / den
    return ab_re, ab_im, f_re * brt - f_im * bit, f_re * bit + f_im * brt


def _s5_params(lam_re, lam_im, log_dt, brt, bit, name):
    G, _, N = lam_re.shape

    def body(lr_ref, li_ref, dt_ref, br_ref, bi_ref, ar_ref, ai_ref, bbr_ref, bbi_ref, pr_ref, pi_ref, a124_ref,
             qr_ref, qi_ref, c124_ref):
        lre, lim, ldt = lr_ref[...], li_ref[...], dt_ref[...]
        ar, ai, bbr, bbi = _s5_param_fn(lre, lim, ldt, br_ref[...], bi_ref[...])
        ar_ref[...], ai_ref[...], bbr_ref[...], bbi_ref[...] = ar, ai, bbr, bbi
        r = lax.broadcasted_iota(jnp.int32, (1, SUBLANE, 1), 1)
        dt = jnp.exp(ldt)
        rate, turn = jnp.minimum(lre, S5_EIG_CLIP) * dt, lim * dt

        def power(k):
            mag, ang = jnp.exp(k * rate), k * turn
            return mag * jnp.cos(ang), mag * jnp.sin(ang)
        pr_ref[...], pi_ref[...] = power((r + 1).astype(F32))
        qr, qi = power((SUBLANE - r).astype(F32))
        qr_ref[...], qi_ref[...] = qr, -qi
        re, im = power(jnp.where(r < 2, 1.0, jnp.where(r < 4, 2.0, jnp.where(r < 6, 4.0, 0.0))))
        a124_ref[...] = jnp.where(r % 2 == 0, re, im)
        c124_ref[...] = jnp.where(r % 2 == 0, re, -im)

    sds = jax.ShapeDtypeStruct
    return pl.pallas_call(
        body, out_shape=[sds((G, 1, N), F32), sds((G, 1, N), F32), sds(brt.shape, F32), sds(brt.shape, F32)]
        + [sds((G, SUBLANE, N), F32)] * 6, name=name,
    )(lam_re, lam_im, log_dt, brt, bit)


def _s5_params_bwd(lam_re, lam_im, log_dt, brt, bit, dar, dai, dbbr, dbbi, name):
    def body(lr_ref, li_ref, dt_ref, br_ref, bi_ref, dar_ref, dai_ref, dbbr_ref, dbbi_ref, *outs):
        _, vjp = jax.vjp(_s5_param_fn, lr_ref[...], li_ref[...], dt_ref[...], br_ref[...], bi_ref[...])
        for o_ref, val in zip(outs, vjp((dar_ref[...], dai_ref[...], dbbr_ref[...], dbbi_ref[...]))):
            o_ref[...] = val

    ins = (lam_re, lam_im, log_dt, brt, bit)
    return pl.pallas_call(body, out_shape=[jax.ShapeDtypeStruct(a.shape, F32) for a in ins], name=name)(
        *ins, dar, dai, dbbr, dbbi)


def _s5_scan(src, w_re, w_im, a124, pr, pi, c_re, c_im, res, *, reverse, name, fwd_state=None, W=512):
    L, S = src.shape
    n = pr.shape[1]
    W = _tile(n, W)
    nblk, nj = L // SUBLANE, n // W
    SB = S // nj
    hp = w_re.shape[1] // SB
    assert SB % LANE == 0 and w_re.shape[1] % SB == 0 and w_re.shape[2] == hp * W
    assert fwd_state is None or reverse
    n_x = (0 if res is None else 1) + (0 if fwd_state is None else 2)

    def body(src_ref, wr_ref, wi_ref, a_ref, pr_ref, pi_ref, cr_ref, ci_ref, *rest):
        xr_ref, xi_ref, y_ref = rest[n_x:n_x + 3]
        br_ref, bi_ref = xr_ref, xi_ref
        xr_ref[...] = _bdot(src_ref[...], wr_ref[...], "nn")
        xi_ref[...] = _bdot(src_ref[...], wi_ref[...], "nn")
        A = a_ref[...]
        PR, PI = pr_ref[...], pi_ref[...]
        row = lax.broadcasted_iota(jnp.int32, (SUBLANE, W), 0)
        last = 0 if reverse else SUBLANE - 1
        mult = []
        for j, k in enumerate((1, 2, 4)):
            keep = (row < SUBLANE - k) if reverse else (row >= k)
            mult.append((jnp.where(keep, A[2 * j:2 * j + 1, :], 0.0), jnp.where(keep, A[2 * j + 1:2 * j + 2, :], 0.0)))

        def step(i, carry):
            cr, ci = carry[:2]
            off = pl.multiple_of(((nblk - 1 - i) if reverse else i) * SUBLANE, SUBLANE)
            xr, xi = br_ref[pl.ds(off, SUBLANE), :], bi_ref[pl.ds(off, SUBLANE), :]
            for k, (ar, ai) in zip((1, 2, 4), mult):
                shift = (SUBLANE - k) if reverse else k
                sr, si = pltpu.roll(xr, shift, 0), pltpu.roll(xi, shift, 0)
                xr, xi = xr + ar * sr - ai * si, xi + ar * si + ai * sr
            xr, xi = xr + PR * cr - PI * ci, xi + PR * ci + PI * cr
            xr_ref[pl.ds(off, SUBLANE), :] = xr
            xi_ref[pl.ds(off, SUBLANE), :] = xi
            out = (xr[last:last + 1, :], xi[last:last + 1, :])
            if fwd_state is not None:
                sr_ref, si_ref = rest[n_x - 2:n_x]
                before = pl.multiple_of(jnp.maximum(off - SUBLANE, 0), SUBLANE)
                shifted = []
                for s_ref in (sr_ref, si_ref):
                    head = jnp.where(off > 0, s_ref[pl.ds(before, SUBLANE), :][SUBLANE - 1:, :], 0.0)
                    shifted.append(jnp.where(row == 0, head, pltpu.roll(s_ref[pl.ds(off, SUBLANE), :], 1, 0)))
                pr_, pi_ = shifted
                out += (carry[2] + xr * pr_ + xi * pi_, carry[3] + xi * pr_ - xr * pi_)
            return out

        z = jnp.zeros((1, W), F32)
        init = (z, z) if fwd_state is None else (z, z, jnp.zeros((SUBLANE, W), F32), jnp.zeros((SUBLANE, W), F32))
        final = lax.fori_loop(0, nblk, step, init)
        y = _bdot(xr_ref[...], cr_ref[...], "nn") + _bdot(xi_ref[...], ci_ref[...], "nn")
        y_ref[...] = y if res is None else y + rest[0][...]
        if fwd_state is not None:
            rest[n_x + 3][...] = jnp.sum(final[2], axis=0, keepdims=True)
            rest[n_x + 4][...] = jnp.sum(final[3], axis=0, keepdims=True)

    col = pl.BlockSpec((L, W), lambda j: (0, j))
    chan = pl.BlockSpec((L, SB), lambda j: (0, j))
    par = pl.BlockSpec((SUBLANE, W), lambda j: (0, j))
    one = pl.BlockSpec((1, W), lambda j: (0, j))
    w_in = pl.BlockSpec((None, SB, W), lambda j: (j // hp, j % hp, j % hp))
    w_out = pl.BlockSpec((None, W, SB), lambda j: (j // hp, j % hp, j % hp))
    extra = ([] if res is None else [res]) + list(fwd_state or ())
    sds = jax.ShapeDtypeStruct
    return pl.pallas_call(
        body, grid=(nj,),
        in_specs=[chan, w_in, w_in, par, par, par, w_out, w_out] + ([] if res is None else [chan]) + [col] * (len(extra) - (res is not None)),
        out_specs=[col, col, chan] + ([] if fwd_state is None else [one, one]),
        out_shape=[sds((L, n), F32)] * 2 + [sds((L, S), F32)] + ([] if fwd_state is None else [sds((1, n), F32)] * 2),
        compiler_params=_params(("parallel",), VMEM_LIMIT), name=name,
    )(src, w_re, w_im, a124, pr, pi, c_re, c_im, *extra)


def _bd_all(mats, name, deps=()):
    n = len(mats)

    def body(*refs):
        for w_ref, o_ref in zip(refs[:n], refs[n + len(deps):]):
            G, A, B = w_ref.shape
            rows, cols = S5_GB * A, S5_GB * B
            tile = jnp.where(lax.broadcasted_iota(jnp.int32, (B, cols), 1) % B == lax.broadcasted_iota(jnp.int32, (B, cols), 0),
                             1.0, 0.0).astype(BF16)
            keep = (lax.broadcasted_iota(jnp.int32, (rows, cols), 0) // A) == (lax.broadcasted_iota(jnp.int32, (rows, cols), 1) // B)
            for kb in range(G // S5_GB):
                w2 = w_ref[kb * S5_GB:(kb + 1) * S5_GB].reshape(rows, B)
                o_ref[kb] = jnp.where(keep, _bdot(w2, tile, "nn"), 0.0).astype(o_ref.dtype)

    return pl.pallas_call(
        body, out_shape=[jax.ShapeDtypeStruct((m.shape[0] // S5_GB, S5_GB * m.shape[1], S5_GB * m.shape[2]), BF16) for m in mats],
        compiler_params=pltpu.CompilerParams(vmem_limit_bytes=VMEM_LIMIT), name=name,
    )(*mats, *deps)


def _place():
    return lax.axis_index("x"), lax.axis_index("y"), lax.axis_index("c")


def _slot(p):
    return 4 * p[0] + 2 * p[1] + p[2]


N_CHIP = N_DEV // 2
_HBM = pl.BlockSpec(memory_space=pltpu.HBM)
_SEM = pl.BlockSpec(memory_space=pltpu.SEMAPHORE)
_ANY = pl.BlockSpec(memory_space=pl.ANY)
_EFFECT = pltpu.SideEffectType.DATAFLOW_SIDE_EFFECTING


def _gather_routes(x, y, c):
    me = _slot((x, y, c))
    peers = [(x, y, 1 - c)] + [(px, py, c) for px, py in ((1 - x, y), (x, 1 - y), (1 - x, 1 - y))]
    return [(p, me, me, _slot(p)) for p in peers]


def _chip_routes(x, y, c):
    myq = 2 * x + y
    return [((px, py, c), 2 * px + py, myq, 2 * px + py) for px, py in ((1 - x, y), (x, 1 - y), (1 - x, 1 - y))]


def _pass_routes(x, y, c):
    chips = ((1 - x, y), (x, 1 - y), (1 - x, 1 - y))
    return [((x, y, 1 - c), _slot((px, py, c)), _slot((px, py, c)), _slot((px, py, 1 - c))) for px, py in chips]


def _split_copies(s_refs, l_refs, send_sems, recv_sems, routes, arrival):
    out = []
    rts = routes(*_place())
    for a in range(len(l_refs)):
        for k, (peer, src_slot, dst_slot, arr_slot) in enumerate(rts):
            s_ref = l_refs[a] if s_refs is None else s_refs[a]
            src = s_ref if src_slot is None else s_ref.at[src_slot]
            sem = a * len(rts) + k
            out.append(pltpu.make_async_remote_copy(
                src_ref=src, dst_ref=l_refs[a].at[arr_slot if arrival else dst_slot], send_sem=send_sems.at[sem],
                recv_sem=recv_sems.at[sem], device_id=peer, device_id_type=MESH))
    return out


def _split_start(srcs, lands, routes, n_routes, dep, name):
    n, ns = len(lands), 0 if srcs is None else len(srcs)
    bufs = [*(srcs or ()), *lands]

    def body(*refs):
        s_refs = None if srcs is None else refs[:ns]
        for cp in _split_copies(s_refs, refs[ns:ns + n], refs[ns + n + 1], refs[ns + n + 2], routes, False):
            cp.start()
        refs[-1][...] = jnp.zeros_like(refs[-1])

    sems = pltpu.SemaphoreType.DMA((n * n_routes,))
    res = pl.pallas_call(
        body, name=name,
        out_shape=(sems, sems, *[pltpu.HBM(a.shape, a.dtype) for a in bufs], jax.ShapeDtypeStruct((SUBLANE, LANE), F32)),
        in_specs=[_HBM] * (ns + n) + [_ANY], out_specs=(_SEM, _SEM, *[_HBM] * (ns + n), pl.BlockSpec(memory_space=pltpu.VMEM)),
        input_output_aliases={i: 2 + i for i in range(ns + n)},
        compiler_params=pltpu.CompilerParams(has_side_effects=_EFFECT),
    )(*[pltpu.with_memory_space_constraint(a, pltpu.HBM) for a in bufs], dep)
    return res[0], res[1], (None if srcs is None else list(res[2:2 + ns])), list(res[2 + ns:2 + ns + n]), res[-1]


def _split_wait(send_sems, recv_sems, srcs, lands, routes, after, name):
    n, ns = len(lands), 0 if srcs is None else len(srcs)
    bufs = [*(srcs or ()), *lands]
    after = list(after) if isinstance(after, (list, tuple)) else [after]

    def body(*refs):
        s_refs = None if srcs is None else refs[:ns]
        for cp in _split_copies(s_refs, refs[ns:ns + n], refs[ns + n], refs[ns + n + 1], routes, True):
            cp.wait_send()
            cp.wait_recv()

    res = pl.pallas_call(
        body, name=name, out_shape=[pltpu.HBM(a.shape, a.dtype) for a in bufs],
        in_specs=[_HBM] * (ns + n) + [_SEM, _SEM] + [_ANY] * len(after), out_specs=[_HBM] * (ns + n),
        input_output_aliases={i: i for i in range(ns + n)},
        compiler_params=pltpu.CompilerParams(has_side_effects=_EFFECT),
    )(*bufs, send_sems, recv_sems, *after)
    return (None if srcs is None else list(res[:ns])), list(res[ns:])


def _to_slot(w, layer, place, out_dtype, name):
    _, R, C = w.shape
    unit = SUBLANE * (4 // jnp.dtype(out_dtype).itemsize)
    tr = _tile(R, max(unit, (512 * 1024 // C) // unit * unit), unit)

    def body(p_ref, w_ref, o_ref):
        o_ref[...] = w_ref[...].astype(o_ref.dtype)

    return pl.pallas_call(
        body,
        grid_spec=pltpu.PrefetchScalarGridSpec(
            num_scalar_prefetch=1, grid=(R // tr,),
            in_specs=[pl.BlockSpec((None, tr, C), lambda i, p: (layer, i, 0))],
            out_specs=pl.BlockSpec((None, tr, C), lambda i, p: (2 * p[1] + p[0], i, 0))),
        out_shape=jax.ShapeDtypeStruct((N_DEV, R, C), out_dtype),
        compiler_params=_params(("parallel",), VMEM_LIMIT), name=name,
    )(place, w)


def _pair_routes(x, y, c):
    return [((x, y, 1 - c), 2 * q + (1 - c), q, q) for q in range(N_CHIP)]


def _pair_sum(x, got, place, name):
    _, R, C = x.shape
    unit = SUBLANE * (4 // x.dtype.itemsize)
    tr = _tile(R, max(unit, (512 * 1024 // C) // unit * unit), unit)

    def body(p_ref, x_ref, g_ref, o_ref, land_ref):
        s = (x_ref[...].astype(F32) + g_ref[...].astype(F32)).astype(o_ref.dtype)
        o_ref[...] = s

        @pl.when(pl.program_id(1) == p_ref[1])
        def _():
            land_ref[...] = s

    blk = lambda f: pl.BlockSpec((None, tr, C), f)
    return pl.pallas_call(
        body,
        grid_spec=pltpu.PrefetchScalarGridSpec(
            num_scalar_prefetch=1, grid=(R // tr, N_CHIP),
            in_specs=[blk(lambda i, q, p: (2 * q + p[0], i, 0)), blk(lambda i, q, p: (q, i, 0))],
            out_specs=[blk(lambda i, q, p: (q, i, 0)), blk(lambda i, q, p: (p[1], i, 0))]),
        out_shape=[jax.ShapeDtypeStruct(got.shape, x.dtype)] * 2,
        compiler_params=_params(("parallel", "arbitrary"), VMEM_LIMIT), name=name,
    )(place, x, got)


def _adamw_math(w, g, m, v):
    m = ADAM_B1 * m + (1.0 - ADAM_B1) * g
    v = ADAM_B2 * v + (1.0 - ADAM_B2) * (g * g)
    m_hat = m / (1.0 - ADAM_B1 ** ADAM_STEP)
    v_hat = v / (1.0 - ADAM_B2 ** ADAM_STEP)
    return -ADAM_LR * (m_hat / (jnp.sqrt(v_hat) + ADAM_EPS) + ADAM_WD * w), m, v


def _adamw(w, m, v, parts, layer, prev, name):
    nl, R, C = w.shape
    P = parts.shape[0]
    unit = SUBLANE * (4 // parts.dtype.itemsize)
    tr = _tile(R, max(unit, (128 * 1024 // C) // unit * unit), unit)

    def body(w_ref, m_ref, v_ref, p_ref, *rest):
        g_ref, d_ref, nm_ref, nv_ref = rest[-4:]
        g = p_ref[0].astype(F32)
        for p in range(1, P):
            g = g + p_ref[p].astype(F32)
        d, nm, nv = _adamw_math(w_ref[...], g, m_ref[...], v_ref[...])
        g_ref[...], d_ref[...], nm_ref[...], nv_ref[...] = g, d, nm, nv

    lay = pl.BlockSpec((None, tr, C), lambda i: (layer, i, 0))
    in_specs = [lay, lay, lay, pl.BlockSpec((P, tr, C), lambda i: (0, i, 0))]
    ins = [w, m, v, parts]
    aliases = {}
    if prev is not None:
        in_specs += [pl.BlockSpec(memory_space=pl.ANY)] * 4
        ins += list(prev)
        aliases = {4 + k: k for k in range(4)}
    return pl.pallas_call(
        body, grid=(R // tr,), in_specs=in_specs, out_specs=[lay] * 4,
        out_shape=[jax.ShapeDtypeStruct(w.shape, F32)] * 4, input_output_aliases=aliases,
        compiler_params=_params(("parallel",), VMEM_LIMIT), name=name,
    )(*ins)


def _gelu_tanh(x):
    return jax.nn.gelu(x, approximate=True)


def _pack(arrs):
    tile = SUBLANE * LANE
    out = []
    for a in arrs:
        f = a.reshape(-1).astype(F32)
        out.append(jnp.pad(f, (0, (-f.shape[0]) % tile)))
    return jnp.concatenate(out)


def _unpack(flat, shapes):
    tile = SUBLANE * LANE
    out, off = [], 0
    for s in shapes:
        n = math.prod(s)
        out.append(flat[off:off + n].reshape(s))
        off += n + (-n) % tile
    return out


def kernel(x, gla_norm, gla_w_in, gla_w_gate_up, gla_b_gate, gla_o_norm, gla_w_out, s5_norm, s5_w_in, s5_lam_re, s5_lam_im, s5_log_dt, s5_b_re, s5_b_im, s5_c_re, s5_c_im, s5_d, s5_w_out, mlp_norm, mlp_w_up, mlp_w_down, final_norm, loss_target, m_gla_norm, m_gla_w_in, m_gla_w_gate_up, m_gla_b_gate, m_gla_o_norm, m_gla_w_out, m_s5_norm, m_s5_w_in, m_s5_lam_re, m_s5_lam_im, m_s5_log_dt, m_s5_b_re, m_s5_b_im, m_s5_c_re, m_s5_c_im, m_s5_d, m_s5_w_out, m_mlp_norm, m_mlp_w_up, m_mlp_w_down, m_final_norm, v_gla_norm, v_gla_w_in, v_gla_w_gate_up, v_gla_b_gate, v_gla_o_norm, v_gla_w_out, v_s5_norm, v_s5_w_in, v_s5_lam_re, v_s5_lam_im, v_s5_log_dt, v_s5_b_re, v_s5_b_im, v_s5_c_re, v_s5_c_im, v_s5_d, v_s5_w_out, v_mlp_norm, v_mlp_w_up, v_mlp_w_down, v_final_norm):
    W = dict(gla_norm=gla_norm, gla_w_in=gla_w_in, gla_w_gate_up=gla_w_gate_up, gla_b_gate=gla_b_gate, gla_o_norm=gla_o_norm, gla_w_out=gla_w_out, s5_norm=s5_norm, s5_w_in=s5_w_in, s5_lam_re=s5_lam_re, s5_lam_im=s5_lam_im, s5_log_dt=s5_log_dt, s5_b_re=s5_b_re, s5_b_im=s5_b_im, s5_c_re=s5_c_re, s5_c_im=s5_c_im, s5_d=s5_d, s5_w_out=s5_w_out, mlp_norm=mlp_norm, mlp_w_up=mlp_w_up, mlp_w_down=mlp_w_down, final_norm=final_norm)
    M = dict(gla_norm=m_gla_norm, gla_w_in=m_gla_w_in, gla_w_gate_up=m_gla_w_gate_up, gla_b_gate=m_gla_b_gate, gla_o_norm=m_gla_o_norm, gla_w_out=m_gla_w_out, s5_norm=m_s5_norm, s5_w_in=m_s5_w_in, s5_lam_re=m_s5_lam_re, s5_lam_im=m_s5_lam_im, s5_log_dt=m_s5_log_dt, s5_b_re=m_s5_b_re, s5_b_im=m_s5_b_im, s5_c_re=m_s5_c_re, s5_c_im=m_s5_c_im, s5_d=m_s5_d, s5_w_out=m_s5_w_out, mlp_norm=m_mlp_norm, mlp_w_up=m_mlp_w_up, mlp_w_down=m_mlp_w_down, final_norm=m_final_norm)
    V = dict(gla_norm=v_gla_norm, gla_w_in=v_gla_w_in, gla_w_gate_up=v_gla_w_gate_up, gla_b_gate=v_gla_b_gate, gla_o_norm=v_gla_o_norm, gla_w_out=v_gla_w_out, s5_norm=v_s5_norm, s5_w_in=v_s5_w_in, s5_lam_re=v_s5_lam_re, s5_lam_im=v_s5_lam_im, s5_log_dt=v_s5_log_dt, s5_b_re=v_s5_b_re, s5_b_im=v_s5_b_im, s5_c_re=v_s5_c_re, s5_c_im=v_s5_c_im, s5_d=v_s5_d, s5_w_out=v_s5_w_out, mlp_norm=v_mlp_norm, mlp_w_up=v_mlp_w_up, mlp_w_down=v_mlp_w_down, final_norm=v_final_norm)
    names = list(W)
    big = ["gla_w_in", "gla_w_out", "s5_w_in", "s5_w_out", "mlp_w_up", "mlp_w_down"]
    small_sharded = {"gla_w_gate_up": 2, "s5_norm": 1, "s5_d": 1}
    small = [n for n in names if n not in big]

    _, L, D = x.shape
    n_gla, n_s5, depth = gla_norm.shape[0], s5_lam_re.shape[0], mlp_norm.shape[0]
    H = GLA_HEADS
    KW, VW = D // 2, D
    DK, DV = KW // H, VW // H
    IN = 2 * KW + 2 * VW + GLA_RANK
    INP = 2 * KW + 2 * VW + LANE
    SW = s5_lam_re.shape[1] * S5_GROUP
    G, N = s5_lam_re.shape[1], s5_lam_re.shape[2]
    nb = G // S5_GB
    dev = _slot(_place())

    place = jnp.stack([lax.axis_index("c"), 2 * lax.axis_index("x") + lax.axis_index("y")]).astype(jnp.int32)
    sm_sh = _pack([W[n] for n in small_sharded]).reshape(1, -1, LANE)
    groups = []
    for i in range(depth):
        j = i // 2
        groups.append([("gla_w_in", j), ("gla_w_out", j)] if i % 2 == 0 else [("s5_w_in", j), ("s5_w_out", j)])
        groups.append([("mlp_w_up", i), ("mlp_w_down", i)])
    groups[0] = [("small", 0)] + groups[0]
    tok = jnp.zeros((SUBLANE, LANE), F32)
    level1, level2, full = [], {}, {}
    for gi, keys in enumerate(groups):
        lands = [_to_slot(sm_sh, 0, place, F32, "own_small") if n == "small" else _to_slot(W[n], j, place, BF16, f"own_{n}_{j}")
                 for n, j in keys]
        send_sems, recv_sems, _, lands, tok = _split_start(None, lands, _gather_routes, 4, tok, f"gather_start_{gi}")
        level1.append((send_sems, recv_sems, lands))

    def arrive(gi, after):
        send_sems, recv_sems, lands = level1[gi]
        _, lands = _split_wait(send_sems, recv_sems, None, lands, _gather_routes, after, f"gather_wait_{gi}")
        send_sems, recv_sems, _, lands, token = _split_start(None, lands, _pass_routes, 3, after, f"gather_pass_{gi}")
        level2[gi] = (send_sems, recv_sems, lands)
        return token

    def fetch(gi, after):
        send_sems, recv_sems, lands = level2[gi]
        _, lands = _split_wait(send_sems, recv_sems, None, lands, _pass_routes, after, f"gather_done_{gi}")
        full.update(zip(groups[gi], lands))

    s5p = []
    for j in range(n_s5):
        lre, lim = s5_lam_re[j][:, None, :], s5_lam_im[j][:, None, :]
        ldt = s5_log_dt[j][:, None, None]
        brt, bit = jnp.swapaxes(s5_b_re[j], 1, 2), jnp.swapaxes(s5_b_im[j], 1, 2)
        ar, ai, bbr, bbi, *mult = _s5_params(lre, lim, ldt, brt, bit, f"s5{j}_params")
        pr, pi, a124, qr, qi, c124 = [jnp.swapaxes(t, 0, 1).reshape(SUBLANE, G * N) for t in mult]
        sw = lambda t: jnp.swapaxes(t, 1, 2)
        bd = _bd_all([bbr, bbi, sw(s5_c_re[j]), -sw(s5_c_im[j]), s5_c_re[j], -s5_c_im[j], sw(bbr), sw(bbi)], f"s5{j}_blockdiag",
                     (tok,) if j == n_s5 - 1 else ())
        s5p.append(dict(prm=(lre, lim, ldt, brt, bit), pr=pr, pi=pi, a124=a124, qr=qr, qi=qi, c124=c124, bd=dict(zip(
            ("bu_re", "bu_im", "cx_re", "cx_im", "w_re", "w_im", "du_re", "du_im"), bd))))

    tok = arrive(0, s5p[-1]["bd"]["bu_re"] if s5p else tok)
    fetch(0, tok)
    sm_all = full["small", 0]
    sm_parts = [_unpack(sm_all[d].reshape(-1), [W[n].shape for n in small_sharded]) for d in range(N_DEV)]
    wgu_full = jnp.concatenate([p[0] for p in sm_parts], axis=2)
    s5n_full = jnp.concatenate([p[1] for p in sm_parts], axis=1)
    s5d_full = jnp.concatenate([p[2] for p in sm_parts], axis=1)

    def gla_weights(j):
        w_in = full["gla_w_in", j]
        w_in = jnp.transpose(w_in, (1, 0, 2)).reshape(D, IN)
        w_in = jnp.pad(w_in, ((0, 0), (0, INP - IN)))
        w_out = full["gla_w_out", j].reshape(VW, D)
        wgu = jnp.pad(wgu_full[j], ((0, LANE - GLA_RANK), (0, 0))).astype(BF16)
        return w_in, w_out, wgu

    grads = {}
    h = x[0]
    saved = []

    for i in range(depth):
        j = i // 2
        if i > 0:
            fetch(2 * i, h)
        if i % 2 == 0:
            w_in, w_out, wgu = gla_weights(j)
            gn = gla_norm[j][None]
            hn = _rms_fwd(h, gn, f"gla{j}_norm")
            proj = _mm(hn, w_in, "nn", name=f"gla{j}_proj", tn=896)
            ahead = arrive(2 * i + 1, proj)
            bg, on = gla_b_gate[j][None], gla_o_norm[j][None]
            y, st = _gla_fwd(proj, wgu, bg, on, H=H, DK=DK, DV=DV, name=f"gla{j}_mix", deps=(ahead,))
            h_new = _mm(y, w_out, "nn", name=f"gla{j}_out", res=h)
            saved.append(("gla", dict(h=h, hn=hn, proj=proj, y=y, st=st, w_in=w_in, w_out=w_out, wgu=wgu, gn=gn, bg=bg, on=on)))
        else:
            w_in = full["s5_w_in", j].reshape(D, SW)
            w_out3 = full["s5_w_out", j]
            sn = s5n_full[j][None]
            hn = _rms_fwd(h, sn, f"s5{j}_norm")
            u = _mm(hn, w_in, "nn", name=f"s5{j}_in")
            sp = s5p[j]
            bd = sp["bd"]
            xr, xi, cx = _s5_scan(u, bd["bu_re"], bd["bu_im"], sp["a124"], sp["pr"], sp["pi"], bd["cx_re"], bd["cx_im"], None,
                                  reverse=False, name=f"s5{j}_scan")
            ahead = arrive(2 * i + 1, xr)
            dsk = s5d_full[j][None]

            def act(cxv, uv, dv, *_):
                ypre = cxv + dv * uv
                return ypre, _gelu_tanh(ypre)
            ypre, yg = _rowmap(act, [cx, u], [dsk, ahead], [(SW, F32), (SW, BF16)], name=f"s5{j}_act")
            z = _mm_nn_cb(yg, w_out3, name=f"s5{j}_out")

            def glu(zv, hv):
                return (hv + zv[:, :D] * jax.nn.sigmoid(zv[:, D:]),)
            h_new = _rowmap(glu, [z, h], [], [(D, F32)], name=f"s5{j}_glu")[0]
            saved.append(("s5", dict(h=h, hn=hn, u=u, xr=xr, xi=xi, ypre=ypre, yg=yg, z=z, w_in=w_in, w_out3=w_out3, sn=sn,
                                     dsk=dsk, **sp)))
        h = h_new
        fetch(2 * i + 1, h)
        w_up3 = full["mlp_w_up", i]
        w_down = full["mlp_w_down", i].reshape(4 * D, D)
        mn = mlp_norm[i][None]
        hn = _rms_fwd(h, mn, f"mlp{i}_norm")
        def sq_relu(zv):
            a = jnp.maximum(zv, 0.0)
            return zv, a * a
        z, s = _mm_nn_cb(hn, w_up3, name=f"mlp{i}_up", out_dtype=[F32, BF16], epi=sq_relu)
        ahead = (arrive(2 * i + 2, s),) if i + 1 < depth else ()
        h_new = _mm(s, w_down, "nn", name=f"mlp{i}_down", res=h, deps=ahead)
        saved.append(("mlp", dict(h=h, hn=hn, z=z, s=s, w_up3=w_up3, w_down=w_down, mn=mn)))
        h = h_new

    dh, sq, dfin = _loss_head(h, loss_target[0], final_norm[None], "loss_head")
    loss = lax.psum(0.5 * jnp.sum(sq) / D, ("x", "y", "c"))
    small_grads = {"final_norm": dfin[0]}
    big_parts = {n: {} for n in big}
    stacks = {n: {} for n in small if n != "final_norm"}
    flat3 = lambda a: a.reshape(a.shape[0], math.prod(a.shape[1:-1]), a.shape[-1])
    exchanges = []

    def pair_grads(keys):
        tag = "_".join(f"{n}{j}" for n, j in keys)
        parts8 = [flat3(big_parts[n][j]) for n, j in keys]
        lands = [lax.empty((N_CHIP, *p.shape[1:]), p.dtype) for p in parts8]
        send_sems, recv_sems, parts8, lands, token = _split_start(parts8, lands, _pair_routes, N_CHIP, place, f"grads_pair_{tag}")
        return (keys, tag, send_sems, recv_sems, parts8, lands), token

    def send_grads(pending, after):
        keys, tag, send_sems, recv_sems, parts8, lands = pending
        parts8, from_sib = _split_wait(send_sems, recv_sems, parts8, lands, _pair_routes, after, f"grads_paired_{tag}")
        sums, lands = zip(*[_pair_sum(p, g, place, f"grads_pair_sum_{n}{j}") for (n, j), p, g in zip(keys, parts8, from_sib)])
        send_sems, recv_sems, srcs, lands, token = _split_start(list(sums), list(lands), _chip_routes, 3, place, f"grads_start_{tag}")
        exchanges.append((keys, tag, send_sems, recv_sems, srcs, lands))
        return token

    carry = ()

    outs = {}
    s5_small = [n for n in small if n.startswith("s5_") and n not in small_sharded]

    def small_start(subset, tag):
        for n in subset:
            if n in stacks:
                small_grads[n] = jnp.stack([stacks[n][k] for k in range(len(stacks[n]))])
        part = _pack([small_grads[n] for n in subset]).reshape(1, -1, LANE)
        lands = [_to_slot(part, 0, place, F32, f"own_small_grads_{tag}")]
        send_sems, recv_sems, _, lands, _ = _split_start(None, lands, _gather_routes, 4, place, f"small_grads_start_{tag}")
        return subset, tag, send_sems, recv_sems, lands

    def small_finish(handle, after):
        subset, tag, send_sems, recv_sems, lands = handle
        _, lands = _split_wait(send_sems, recv_sems, None, lands, _gather_routes, after, f"small_grads_wait_{tag}")
        send_sems, recv_sems, _, lands, t = _split_start(None, lands, _pass_routes, 3, after, f"small_grads_pass_{tag}")
        _, lands = _split_wait(send_sems, recv_sems, None, lands, _pass_routes, t, f"small_grads_done_{tag}")
        parts_all = lands[0]
        rows = parts_all.shape[1]
        shapes = [small_grads[n].shape for n in subset]

        def full_layout(T):
            arrs = []
            for n, shp in zip(subset, shapes):
                if n in small_sharded:
                    ax = small_sharded[n]
                    arrs.append(lax.dynamic_update_slice_in_dim(jnp.zeros(shp, F32), T[n], dev * T[n].shape[ax], axis=ax))
                else:
                    arrs.append(T[n])
            return _pack(arrs).reshape(1, rows, LANE)

        res = _adamw(full_layout(W), full_layout(M), full_layout(V), parts_all, 0, None, f"adamw_small_{tag}")
        for key, flat in zip(("grad", "delta", "new_m", "new_v"), res):
            for n, a in zip(subset, _unpack(flat.reshape(-1), shapes)):
                if n in small_sharded:
                    ax = small_sharded[n]
                    a = lax.dynamic_slice_in_dim(a, dev * W[n].shape[ax], W[n].shape[ax], axis=ax)
                outs[key, n] = a
        return res[0]

    for idx in range(len(saved) - 1, -1, -1):
        kind, s = saved[idx]
        li = sum(1 for k, _ in saved[:idx] if k == kind)
        if kind == "mlp":
            def sq_relu_bwd(dsv, zv):
                return (dsv * 2.0 * jnp.maximum(zv, 0.0),)
            dz = _mm(dh, s["w_down"], "nt", name=f"mlp{li}_dz", out_dtype=BF16, epi=sq_relu_bwd, epi_ins=(s["z"],), deps=carry)
            big_parts["mlp_w_down"][li] = _mm(s["s"], dh, "tn", name=f"mlp{li}_dw_down", out_dtype=BF16).reshape(N_DEV, 4 * D // N_DEV, D)
            big_parts["mlp_w_up"][li] = _mm_tn_cbout(s["hn"], dz, N_DEV, name=f"mlp{li}_dw_up")
            pending, token = pair_grads([("mlp_w_down", li), ("mlp_w_up", li)])
            dhn = _mm_nt_cb(dz, s["w_up3"], name=f"mlp{li}_dhn")
            dh, dg = _rms_bwd(s["h"], dhn, dh, s["mn"], f"mlp{li}_dnorm", deps=(token,))
            stacks["mlp_norm"][li] = dg[0]
        elif kind == "gla":
            dyv = _mm(dh, s["w_out"], "nt", name=f"gla{li}_dy", deps=carry)
            big_parts["gla_w_out"][li] = _mm(s["y"], dh, "tn", name=f"gla{li}_dw_out", out_dtype=BF16).reshape(N_DEV, VW // N_DEV, D)
            dproj, don, dbg, dwgu = _gla_bwd(s["proj"], s["wgu"], s["bg"], s["on"], s["st"], dyv, H=H, DK=DK, DV=DV, name=f"gla{li}_dmix")
            dwgu = dwgu[:GLA_RANK]
            dw_in = _mm(s["hn"], dproj, "tn", name=f"gla{li}_dw_in", out_dtype=BF16, tn=896)[:, :IN]
            big_parts["gla_w_in"][li] = jnp.transpose(dw_in.reshape(D, N_DEV, IN // N_DEV), (1, 0, 2))
            pending, token = pair_grads([("gla_w_out", li), ("gla_w_in", li)])
            dhn = _mm(dproj, s["w_in"], "nt", name=f"gla{li}_dhn", tk=896)
            dh, dg = _rms_bwd(s["h"], dhn, dh, s["gn"], f"gla{li}_dnorm", deps=(token,))
            stacks["gla_norm"][li], stacks["gla_b_gate"][li], stacks["gla_o_norm"][li] = dg[0], dbg[0], don[0]
            stacks["gla_w_gate_up"][li] = dwgu
        else:
            def glu_bwd(zv, dhv, *_):
                a, sg = zv[:, :D], jax.nn.sigmoid(zv[:, D:])
                return (jnp.concatenate([dhv * sg, dhv * a * sg * (1.0 - sg)], axis=1),)
            dz = _rowmap(glu_bwd, [s["z"], dh], list(carry), [(2 * D, BF16)], name=f"s5{li}_dglu")[0]
            big_parts["s5_w_out"][li] = _mm_tn_cbout(s["yg"], dz, N_DEV, name=f"s5{li}_dw_out")
            dyg = _mm_nt_cb(dz, s["w_out3"], name=f"s5{li}_dyg")

            def act_bwd(dygv, ypv, uv, dv):
                _, vjp = jax.vjp(_gelu_tanh, ypv)
                dyp = vjp(dygv)[0]
                return dyp, dyp * dv, jnp.sum(dyp * uv, axis=0, keepdims=True)
            dyp, du, dd = _rowmap(act_bwd, [dyg, s["ypre"], s["u"]], [s["dsk"]], [(SW, F32), (SW, F32)], [SW], name=f"s5{li}_dact")
            bd = s["bd"]
            lr_, li_, du, dar, dai = _s5_scan(dyp, bd["w_re"], bd["w_im"], s["c124"], s["qr"], s["qi"], bd["du_re"], bd["du_im"], du,
                                              reverse=True, fwd_state=(s["xr"], s["xi"]), name=f"s5{li}_dscan")
            dbbr = _mm_tn_bd(s["u"], lr_, nb, S5_GROUP, N, name=f"s5{li}_dbb_re")
            dbbi = _mm_tn_bd(s["u"], li_, nb, S5_GROUP, N, name=f"s5{li}_dbb_im")
            dcr = _mm_tn_bd(dyp, s["xr"], nb, S5_GROUP, N, name=f"s5{li}_dc_re")
            dci = -_mm_tn_bd(dyp, s["xi"], nb, S5_GROUP, N, name=f"s5{li}_dc_im")
            dlre, dlim, dldt, dbrt, dbit = _s5_params_bwd(*s["prm"], dar.reshape(G, 1, N), dai.reshape(G, 1, N), dbbr, dbbi,
                                                         f"s5{li}_dparams")
            big_parts["s5_w_in"][li] = _mm(s["hn"], du, "tn", name=f"s5{li}_dw_in", out_dtype=BF16).reshape(N_DEV, D // N_DEV, SW)
            pending, token = pair_grads([("s5_w_out", li), ("s5_w_in", li)])
            dhn = _mm(du, s["w_in"], "nt", name=f"s5{li}_dhn")
            dh, dg = _rms_bwd(s["h"], dhn, dh, s["sn"], f"s5{li}_dnorm", deps=(token,))
            stacks["s5_norm"][li], stacks["s5_d"][li] = dg[0], dd[0]
            stacks["s5_lam_re"][li], stacks["s5_lam_im"][li], stacks["s5_log_dt"][li] = dlre[:, 0], dlim[:, 0], dldt[:, 0, 0]
            stacks["s5_b_re"][li], stacks["s5_b_im"][li] = jnp.swapaxes(dbrt, 1, 2), jnp.swapaxes(dbit, 1, 2)
            stacks["s5_c_re"][li], stacks["s5_c_im"][li] = dcr, dci
            if li == 0:
                s5_handle = small_start(s5_small, "s5")
        carry = (send_grads(pending, dh),)
    grad_x = dh[None]

    rest = small_start([n for n in small if n not in s5_small], "rest")

    stacked = {n: None for n in big}
    as3 = lambda a: a.reshape(a.shape[0], math.prod(a.shape[1:-1]), a.shape[-1])

    def finish(exchange, after):
        keys, tag, e_send, e_recv, e_srcs, e_lands = exchange
        _, e_lands = _split_wait(e_send, e_recv, e_srcs, e_lands, _chip_routes, after, f"grads_wait_{tag}")
        for (n, j), parts in zip(keys, e_lands):
            stacked[n] = _adamw(as3(W[n]), as3(M[n]), as3(V[n]), parts, j, stacked[n], f"adamw_{n}_{j}")
        return stacked[keys[0][0]][0]

    done = [small_finish(s5_handle, carry[0])]
    for exchange in exchanges[:-1]:
        finish(exchange, carry[0])
    done.append(small_finish(rest, carry[0]))
    last_keys = {n for n, _ in exchanges[-1][0]}
    finish(exchanges[-1], done + [stacked[n][0] for n in big if n not in last_keys])
    for n in big:
        for key, a in zip(("grad", "delta", "new_m", "new_v"), stacked[n]):
            outs[key, n] = a.reshape(W[n].shape)

    return (loss, grad_x, *[outs["grad", n] for n in names], *[outs["delta", n] for n in names],
            *[outs["new_m", n] for n in names], *[outs["new_v", n] for n in names])
```

```python
import functools
import math

import jax
import jax.numpy as jnp
from jax import lax
from jax.experimental import pallas as pl
from jax.experimental.pallas import tpu as pltpu

F32, BF16 = jnp.float32, jnp.bfloat16
MESH = pl.DeviceIdType.MESH
N_DEV = 8
LANE = 128
SUBLANE = 8
VMEM_LIMIT = 48 * 1024 * 1024

EPS = 1e-6
CHUNK = 64
GLA_HEADS = 4
GLA_RANK = 16
GLA_TEMP = 16.0
S5_GROUP = 16
S5_STATE = 64
S5_EIG_CLIP = -1e-4
S5_GB = 16
ADAM_LR, ADAM_B1, ADAM_B2, ADAM_EPS, ADAM_WD, ADAM_STEP = 0.001, 0.9, 0.999, 1e-08, 0.01, 10

_CONTRACT = {"nn": ((1,), (0,)), "tn": ((0,), (0,)), "nt": ((1,), (1,))}


def _tile(n, pref, unit=LANE):
    if n <= pref:
        return n
    t = (pref // unit) * unit
    while t > unit and n % t:
        t -= unit
    assert n % t == 0, (n, pref, unit)
    return t


def _params(sem, vmem=None):
    return pltpu.CompilerParams(dimension_semantics=sem, vmem_limit_bytes=vmem)


def _dot(a, b, dims, precision=None):
    return lax.dot_general(a, b, (_CONTRACT[dims], ((), ())), preferred_element_type=F32, precision=precision)


def _bdot(a, b, dims):
    return _dot(a.astype(BF16), b.astype(BF16), dims)


def _mm_call(a, b, *, dims, grid, a_spec, b_spec, o_spec, out_shape, out_dtype, name, res=None, epi=None, epi_ins=(), deps=(),
             acc_shape=None):
    nk = grid[2]
    acc_shape = acc_shape or tuple(d for d in o_spec.block_shape if d is not None)
    if res is not None:
        epi, epi_ins = (lambda r, x: (r + x,)), (res,)
    single = not isinstance(out_dtype, (list, tuple))
    out_dtypes = [out_dtype] if single else list(out_dtype)
    n_e, n_o, n_d = len(epi_ins), len(out_dtypes), len(deps)

    def body(*refs):
        a_ref, b_ref = refs[:2]
        e_refs, o_refs = refs[2:2 + n_e], refs[2 + n_e + n_d:2 + n_e + n_d + n_o]

        def finish(r):
            vals = (r,) if epi is None else epi(r, *[e[...].astype(F32) for e in e_refs])
            for o_ref, v in zip(o_refs, vals):
                o_ref[...] = v.astype(o_ref.dtype)

        d = _bdot(a_ref[...], b_ref[...], dims)
        if nk == 1:
            finish(d)
            return
        acc = refs[-1]
        k = pl.program_id(2)

        @pl.when(k == 0)
        def _():
            acc[...] = d

        @pl.when((k > 0) & (k < nk - 1))
        def _():
            acc[...] += d

        @pl.when(k == nk - 1)
        def _():
            finish(acc[...] + d)

    outs = pl.pallas_call(
        body, grid=grid, in_specs=[a_spec, b_spec] + [o_spec] * n_e + [pl.BlockSpec(memory_space=pl.ANY)] * n_d,
        out_specs=[o_spec] * n_o, out_shape=[jax.ShapeDtypeStruct(out_shape, dt) for dt in out_dtypes],
        scratch_shapes=[] if nk == 1 else [pltpu.VMEM(acc_shape, F32)],
        compiler_params=_params(("parallel", "parallel", "arbitrary"), VMEM_LIMIT), name=name,
    )(a, b, *epi_ins, *deps)
    return outs[0] if single else outs


TM, TN, TK = 1024, 1024, 2048


def _mm(a, b, dims, *, name, out_dtype=F32, res=None, epi=None, epi_ins=(), deps=(), tm=TM, tn=TN, tk=TK):
    if dims == "tn":
        (K, M), (_, N) = a.shape, b.shape
    elif dims == "nn":
        (M, K), (_, N) = a.shape, b.shape
    else:
        (M, K), (N, _) = a.shape, b.shape
    tm, tn, tk = _tile(M, tm), _tile(N, tn), _tile(K, tk)
    a_spec = pl.BlockSpec((tk, tm), lambda i, j, k: (k, i)) if dims == "tn" else pl.BlockSpec((tm, tk), lambda i, j, k: (i, k))
    b_spec = pl.BlockSpec((tn, tk), lambda i, j, k: (j, k)) if dims == "nt" else pl.BlockSpec((tk, tn), lambda i, j, k: (k, j))
    o_spec = pl.BlockSpec((tm, tn), lambda i, j, k: (i, j))
    return _mm_call(a, b, dims=dims, grid=(M // tm, N // tn, K // tk), a_spec=a_spec, b_spec=b_spec, o_spec=o_spec,
                    out_shape=(M, N), out_dtype=out_dtype, name=name, res=res, epi=epi, epi_ins=epi_ins, deps=deps)


def _mm_nn_cb(a, b3, *, name, out_dtype=F32, epi=None, tm=TM, tn=TN, tk=TK):
    (M, K), (P, _, Ns) = a.shape, b3.shape
    tm, tn, tk = _tile(M, tm), _tile(Ns, tn), _tile(K, tk)
    npb = Ns // tn
    return _mm_call(a, b3, dims="nn", grid=(M // tm, P * npb, K // tk),
                    a_spec=pl.BlockSpec((tm, tk), lambda i, j, k: (i, k)),
                    b_spec=pl.BlockSpec((None, tk, tn), lambda i, j, k: (j // npb, k, j % npb)),
                    o_spec=pl.BlockSpec((tm, tn), lambda i, j, k: (i, j)),
                    out_shape=(M, P * Ns), out_dtype=out_dtype, name=name, epi=epi)


def _mm_nt_cb(a, b3, *, name, out_dtype=F32, tm=TM, tn=TN, tk=TK):
    (M, _), (P, N, Ns) = a.shape, b3.shape
    tm, tn, tk = _tile(M, tm), _tile(N, tn), _tile(Ns, tk)
    kpb = Ns // tk
    return _mm_call(a, b3, dims="nt", grid=(M // tm, N // tn, P * kpb),
                    a_spec=pl.BlockSpec((tm, tk), lambda i, j, k: (i, k)),
                    b_spec=pl.BlockSpec((None, tn, tk), lambda i, j, k: (k // kpb, j, k % kpb)),
                    o_spec=pl.BlockSpec((tm, tn), lambda i, j, k: (i, j)),
                    out_shape=(M, N), out_dtype=out_dtype, name=name)


def _mm_tn_cbout(a, b, parts, *, name, out_dtype=BF16, tm=TM, tn=TN, tk=TK):
    (K, M), (_, N) = a.shape, b.shape
    Ns = N // parts
    tm, tn, tk = _tile(M, tm), _tile(Ns, tn), _tile(K, tk)
    npb = Ns // tn
    return _mm_call(a, b, dims="tn", grid=(M // tm, parts * npb, K // tk),
                    a_spec=pl.BlockSpec((tk, tm), lambda i, j, k: (k, i)),
                    b_spec=pl.BlockSpec((tk, tn), lambda i, j, k: (k, j)),
                    o_spec=pl.BlockSpec((None, tm, tn), lambda i, j, k: (j // npb, i, j % npb)),
                    out_shape=(parts, M, Ns), out_dtype=out_dtype, name=name)


def _mm_tn_bd(a, b, nb, A, B, *, name, tk=1024):
    (K, MA), (_, NB) = a.shape, b.shape
    Ma, Nb = MA // nb, NB // nb
    tk = _tile(K, tk)

    def diagonal(r):
        keep = (lax.broadcasted_iota(jnp.int32, (Ma, Nb), 0) // A) == (lax.broadcasted_iota(jnp.int32, (Ma, Nb), 1) // B)
        fold = jnp.where(lax.broadcasted_iota(jnp.int32, (Nb, B), 0) % B == lax.broadcasted_iota(jnp.int32, (Nb, B), 1), 1.0, 0.0)
        return (_dot(jnp.where(keep, r, 0.0), fold, "nn", lax.Precision.HIGHEST),)

    out = _mm_call(a, b, dims="tn", grid=(nb, 1, K // tk),
                   a_spec=pl.BlockSpec((tk, Ma), lambda i, j, k: (k, i)),
                   b_spec=pl.BlockSpec((tk, Nb), lambda i, j, k: (k, i)),
                   o_spec=pl.BlockSpec((None, Ma, B), lambda i, j, k: (i, 0, 0)),
                   out_shape=(nb, Ma, B), out_dtype=F32, name=name, epi=diagonal, acc_shape=(Ma, Nb))
    return out.reshape(nb * (Ma // A), A, B)


def _rowmap(fn, rows, consts, out_defs, red_defs=(), *, name, tr=256):
    L = rows[0].shape[0]
    widest = max([r.shape[1] for r in rows] + [n for n, _ in out_defs])
    tr = _tile(L, max(SUBLANE * 2, min(tr, 512 * 1024 // widest)), SUBLANE * 2)
    n_in, n_o, n_d = len(rows) + len(consts), len(out_defs), len(red_defs)

    def body(*refs):
        res = fn(*[r[...] for r in refs[:n_in]])
        res = res if isinstance(res, (tuple, list)) else (res,)
        outs = refs[n_in:]
        for o_ref, val in zip(outs[:n_o], res[:n_o]):
            o_ref[...] = val.astype(o_ref.dtype)
        if n_d:
            @pl.when(pl.program_id(0) == 0)
            def _():
                for o_ref in outs[n_o:]:
                    o_ref[...] = jnp.zeros_like(o_ref)
            for o_ref, val in zip(outs[n_o:], res[n_o:]):
                o_ref[...] += val

    in_specs = [pl.BlockSpec((tr, r.shape[1]), lambda i: (i, 0)) for r in rows]
    in_specs += [pl.BlockSpec(c.shape, lambda i, nd=c.ndim: (0,) * nd) for c in consts]
    out_specs = [pl.BlockSpec((tr, n), lambda i: (i, 0)) for n, _ in out_defs]
    out_specs += [pl.BlockSpec((1, n), lambda i: (0, 0)) for n in red_defs]
    out_shape = [jax.ShapeDtypeStruct((L, n), dt) for n, dt in out_defs]
    out_shape += [jax.ShapeDtypeStruct((1, n), F32) for n in red_defs]
    return pl.pallas_call(body, grid=(L // tr,), in_specs=in_specs, out_specs=out_specs, out_shape=out_shape,
                          compiler_params=_params(("arbitrary",), VMEM_LIMIT), name=name)(*rows, *consts)


def _rms_parts(x):
    r = lax.rsqrt(jnp.mean(x * x, axis=-1, keepdims=True) + EPS)
    return r, x * r


def _rms_fwd(h, g, name, deps=()):
    def fn(x, gg, *_):
        _, xh = _rms_parts(x)
        return (xh * gg,)
    return _rowmap(fn, [h], [g, *deps], [(h.shape[1], BF16)], name=name)[0]


def _rms_bwd(h, dhn, dh, g, name, deps=()):
    def fn(x, dy, dres, gg, *_):
        r, xh = _rms_parts(x)
        dxh = dy * gg
        dx = dres + r * (dxh - xh * jnp.mean(dxh * xh, axis=-1, keepdims=True))
        return dx, dx, jnp.sum(dy * xh, axis=0, keepdims=True)
    D = h.shape[1]
    return _rowmap(fn, [h, dhn, dh], [g, *deps], [(D, F32), (D, BF16)], [D], name=name)


def _loss_head(h, tgt, g, name):
    D = h.shape[1]

    def fn(x, t, gg):
        r, xh = _rms_parts(x)
        diff = xh * gg - t
        dy = diff * (1.0 / D)
        dxh = dy * gg
        dx = r * (dxh - xh * jnp.mean(dxh * xh, axis=-1, keepdims=True))
        return dx, dx, jnp.sum(diff * diff, axis=0, keepdims=True), jnp.sum(dy * xh, axis=0, keepdims=True)
    return _rowmap(fn, [h, tgt], [g], [(D, F32), (D, BF16)], [D, D], name=name)


def _tri(n, strict):
    r = lax.broadcasted_iota(jnp.int32, (n, n), 0)
    c = lax.broadcasted_iota(jnp.int32, (n, n), 1)
    return jnp.where((c < r) if strict else (c <= r), 1.0, 0.0).astype(F32)


def _gla_gate(g_ref, wgu_ref, bg_ref):
    pre = _bdot(g_ref[...], wgu_ref[...], "nn") + bg_ref[...]
    la = (jnp.minimum(pre, 0.0) - jnp.log(1.0 + jnp.exp(-jnp.abs(pre)))) * (1.0 / GLA_TEMP)
    cum = _dot(_tri(CHUNK, False), la, "nn", lax.Precision.HIGHEST)
    return pre, cum, cum[CHUNK - 1:CHUNK, :]


def _gla_specs(H, DK, DV, cmap):
    KW, VW = H * DK, H * DV
    assert (2 * KW) % VW == 0 and (2 * KW + 2 * VW) % LANE == 0
    vb, gb = (2 * KW) // VW, (2 * KW + 2 * VW) // LANE
    return [
        pl.BlockSpec((CHUNK, KW), lambda c: (cmap(c), 0)),
        pl.BlockSpec((CHUNK, KW), lambda c: (cmap(c), 1)),
        pl.BlockSpec((CHUNK, VW), lambda c: (cmap(c), vb)),
        pl.BlockSpec((CHUNK, VW), lambda c: (cmap(c), vb + 1)),
        pl.BlockSpec((CHUNK, LANE), lambda c: (cmap(c), gb)),
        pl.BlockSpec((LANE, KW), lambda c: (0, 0)),
        pl.BlockSpec((1, KW), lambda c: (0, 0)),
        pl.BlockSpec((1, DV), lambda c: (0, 0)),
    ]


def _gla_fwd(proj, wgu, bg, on, *, H, DK, DV, name, deps=()):
    L = proj.shape[0]
    nc = L // CHUNK
    scale = DK ** -0.5
    n_d = len(deps)

    def body(q_ref, k_ref, v_ref, r_ref, g_ref, wgu_ref, bg_ref, on_ref, *rest):
        y_ref, st_ref, S = rest[n_d:]

        @pl.when(pl.program_id(0) == 0)
        def _():
            S[...] = jnp.zeros_like(S)
        _, cum, total = _gla_gate(g_ref, wgu_ref, bg_ref)
        kd = k_ref[...] * jnp.exp(total - cum)
        dec = jnp.exp(total)
        for h in range(H):
            ks, vs = slice(h * DK, (h + 1) * DK), slice(h * DV, (h + 1) * DV)
            St = S[h] * dec[:, ks] + _bdot(v_ref[:, vs], kd[:, ks], "tn")
            S[h] = St
            st_ref[h] = St
            o = _bdot(q_ref[:, ks] * scale, St, "nt")
            _, oh = _rms_parts(o)
            y_ref[:, vs] = (oh * on_ref[...] * jax.nn.silu(r_ref[:, vs])).astype(y_ref.dtype)

    return pl.pallas_call(
        body, grid=(nc,), in_specs=_gla_specs(H, DK, DV, lambda c: c) + [pl.BlockSpec(memory_space=pl.ANY)] * n_d,
        out_specs=[pl.BlockSpec((CHUNK, H * DV), lambda c: (c, 0)),
                   pl.BlockSpec((H, None, DV, DK), lambda c: (0, c, 0, 0))],
        out_shape=[jax.ShapeDtypeStruct((L, H * DV), BF16), jax.ShapeDtypeStruct((H, nc, DV, DK), F32)],
        scratch_shapes=[pltpu.VMEM((H, DV, DK), F32)],
        compiler_params=_params(("arbitrary",), VMEM_LIMIT), name=name,
    )(proj, proj, proj, proj, proj, wgu, bg, on, *deps)


def _gla_bwd(proj, wgu, bg, on, st, dy, *, H, DK, DV, name):
    L = proj.shape[0]
    nc = L // CHUNK
    KW, VW = H * DK, H * DV
    scale = DK ** -0.5
    rev = lambda c: nc - 1 - c

    def body(q_ref, k_ref, v_ref, r_ref, g_ref, wgu_ref, bg_ref, on_ref, sc_ref, sp_ref, dy_ref,
             dp_ref, don_ref, dbg_ref, dwgu_ref, G, decn):
        c = pl.program_id(0)

        @pl.when(c == 0)
        def _():
            for ref in (G, decn, don_ref, dbg_ref, dwgu_ref):
                ref[...] = jnp.zeros_like(ref)

        pre, cum, total = _gla_gate(g_ref, wgu_ref, bg_ref)
        ex = jnp.exp(total - cum)
        kd = k_ref[...] * ex
        dec = jnp.exp(total)
        gn = on_ref[...]
        alive = jnp.where(c == nc - 1, 0.0, 1.0)
        don = jnp.zeros_like(gn)
        ddec, dkd = [], []
        for h in range(H):
            ks, vs = slice(h * DK, (h + 1) * DK), slice(h * DV, (h + 1) * DV)
            qs = q_ref[:, ks] * scale
            Sc = sc_ref[h]
            o = _bdot(qs, Sc, "nt")
            rinv, oh = _rms_parts(o)
            r = r_ref[:, vs]
            sg = jax.nn.sigmoid(r)
            dyv = dy_ref[:, vs]
            d_on = dyv * (r * sg)
            dp_ref[:, 2 * KW + VW + h * DV:2 * KW + VW + (h + 1) * DV] = (
                dyv * (oh * gn) * (sg * (1.0 + r * (1.0 - sg)))).astype(dp_ref.dtype)
            don = don + jnp.sum(d_on * oh, axis=0, keepdims=True)
            dxh = d_on * gn
            do = rinv * (dxh - oh * jnp.mean(dxh * oh, axis=-1, keepdims=True))
            dp_ref[:, ks] = (_bdot(do, Sc, "nn") * scale).astype(dp_ref.dtype)
            Gt = G[h] * decn[:, ks] + _bdot(do, qs, "tn")
            G[h] = Gt
            dkd.append(_bdot(v_ref[:, vs], Gt, "nn"))
            dp_ref[:, 2 * KW + h * DV:2 * KW + (h + 1) * DV] = _bdot(kd[:, ks], Gt, "nt").astype(dp_ref.dtype)
            ddec.append(jnp.sum(Gt * (sp_ref[h] * alive), axis=0, keepdims=True))
        don_ref[...] += don
        decn[...] = dec
        dkd, ddec = jnp.concatenate(dkd, axis=1), jnp.concatenate(ddec, axis=1)
        dp_ref[:, KW:2 * KW] = (dkd * ex).astype(dp_ref.dtype)
        dla = ddec * dec + _dot(_tri(CHUNK, True), dkd * kd, "nn", lax.Precision.HIGHEST)
        dpre = dla * (1.0 / GLA_TEMP) * jax.nn.sigmoid(-pre)
        dbg_ref[...] += jnp.sum(dpre, axis=0, keepdims=True)
        dwgu_ref[...] += _bdot(g_ref[...], dpre, "tn")
        dp_ref[:, 2 * KW + 2 * VW:] = _bdot(dpre, wgu_ref[...], "nt").astype(dp_ref.dtype)

    in_specs = _gla_specs(H, DK, DV, rev) + [
        pl.BlockSpec((H, None, DV, DK), lambda c: (0, rev(c), 0, 0)),
        pl.BlockSpec((H, None, DV, DK), lambda c: (0, jnp.maximum(rev(c) - 1, 0), 0, 0)),
        pl.BlockSpec((CHUNK, VW), lambda c: (rev(c), 0)),
    ]
    W = 2 * KW + 2 * VW + LANE
    whole = lambda shp: pl.BlockSpec(shp, lambda c: (0, 0))
    sds = jax.ShapeDtypeStruct
    return pl.pallas_call(
        body, grid=(nc,), in_specs=in_specs,
        out_specs=[pl.BlockSpec((CHUNK, W), lambda c: (rev(c), 0)), whole((1, DV)), whole((1, KW)), whole((LANE, KW))],
        out_shape=[sds((L, W), BF16), sds((1, DV), F32), sds((1, KW), F32), sds((LANE, KW), F32)],
        scratch_shapes=[pltpu.VMEM((H, DV, DK), F32), pltpu.VMEM((1, KW), F32)],
        compiler_params=_params(("arbitrary",), VMEM_LIMIT), name=name,
    )(proj, proj, proj, proj, proj, wgu, bg, on, st, st, dy)


def _s5_param_fn(lam_re, lam_im, log_dt, brt, bit):
    lr = jnp.minimum(lam_re, S5_EIG_CLIP)
    li = lam_im
    dt = jnp.exp(log_dt)
    mag = jnp.exp(lr * dt)
    ang = li * dt
    ab_re = mag * jnp.cos(ang)
    ab_im = mag * jnp.sin(ang)
    den = lr * lr + li * li
    nr = ab_re - 1.0
    f_re = (nr * lr + ab_im * li) / den
    f_im = (ab_im * lr - nr * li) / den
    return ab_re, ab_im, f_re * brt - f_im * bit, f_re * bit + f_im * brt


def _s5_params(lam_re, lam_im, log_dt, brt, bit, name):
    G, _, N = lam_re.shape

    def body(lr_ref, li_ref, dt_ref, br_ref, bi_ref, ar_ref, ai_ref, bbr_ref, bbi_ref, pr_ref, pi_ref, a124_ref,
             qr_ref, qi_ref, c124_ref):
        lre, lim, ldt = lr_ref[...], li_ref[...], dt_ref[...]
        ar, ai, bbr, bbi = _s5_param_fn(lre, lim, ldt, br_ref[...], bi_ref[...])
        ar_ref[...], ai_ref[...], bbr_ref[...], bbi_ref[...] = ar, ai, bbr, bbi
        r = lax.broadcasted_iota(jnp.int32, (1, SUBLANE, 1), 1)
        dt = jnp.exp(ldt)
        rate, turn = jnp.minimum(lre, S5_EIG_CLIP) * dt, lim * dt

        def power(k):
            mag, ang = jnp.exp(k * rate), k * turn
            return mag * jnp.cos(ang), mag * jnp.sin(ang)
        pr_ref[...], pi_ref[...] = power((r + 1).astype(F32))
        qr, qi = power((SUBLANE - r).astype(F32))
        qr_ref[...], qi_ref[...] = qr, -qi
        re, im = power(jnp.where(r < 2, 1.0, jnp.where(r < 4, 2.0, jnp.where(r < 6, 4.0, 0.0))))
        a124_ref[...] = jnp.where(r % 2 == 0, re, im)
        c124_ref[...] = jnp.where(r % 2 == 0, re, -im)

    sds = jax.ShapeDtypeStruct
    return pl.pallas_call(
        body, out_shape=[sds((G, 1, N), F32), sds((G, 1, N), F32), sds(brt.shape, F32), sds(brt.shape, F32)]
        + [sds((G, SUBLANE, N), F32)] * 6, name=name,
    )(lam_re, lam_im, log_dt, brt, bit)


def _s5_params_bwd(lam_re, lam_im, log_dt, brt, bit, dar, dai, dbbr, dbbi, name):
    def body(lr_ref, li_ref, dt_ref, br_ref, bi_ref, dar_ref, dai_ref, dbbr_ref, dbbi_ref, *outs):
        _, vjp = jax.vjp(_s5_param_fn, lr_ref[...], li_ref[...], dt_ref[...], br_ref[...], bi_ref[...])
        for o_ref, val in zip(outs, vjp((dar_ref[...], dai_ref[...], dbbr_ref[...], dbbi_ref[...]))):
            o_ref[...] = val

    ins = (lam_re, lam_im, log_dt, brt, bit)
    return pl.pallas_call(body, out_shape=[jax.ShapeDtypeStruct(a.shape, F32) for a in ins], name=name)(
        *ins, dar, dai, dbbr, dbbi)


def _s5_scan(src, w_re, w_im, a124, pr, pi, c_re, c_im, res, *, reverse, name, fwd_state=None, W=512):
    L, S = src.shape
    n = pr.shape[1]
    W = _tile(n, W)
    nblk, nj = L // SUBLANE, n // W
    SB = S // nj
    hp = w_re.shape[1] // SB
    assert SB % LANE == 0 and w_re.shape[1] % SB == 0 and w_re.shape[2] == hp * W
    assert fwd_state is None or reverse
    n_x = (0 if res is None else 1) + (0 if fwd_state is None else 2)

    def body(src_ref, wr_ref, wi_ref, a_ref, pr_ref, pi_ref, cr_ref, ci_ref, *rest):
        xr_ref, xi_ref, y_ref = rest[n_x:n_x + 3]
        br_ref, bi_ref = xr_ref, xi_ref
        xr_ref[...] = _bdot(src_ref[...], wr_ref[...], "nn")
        xi_ref[...] = _bdot(src_ref[...], wi_ref[...], "nn")
        A = a_ref[...]
        PR, PI = pr_ref[...], pi_ref[...]
        row = lax.broadcasted_iota(jnp.int32, (SUBLANE, W), 0)
        last = 0 if reverse else SUBLANE - 1
        mult = []
        for j, k in enumerate((1, 2, 4)):
            keep = (row < SUBLANE - k) if reverse else (row >= k)
            mult.append((jnp.where(keep, A[2 * j:2 * j + 1, :], 0.0), jnp.where(keep, A[2 * j + 1:2 * j + 2, :], 0.0)))

        def step(i, carry):
            cr, ci = carry[:2]
            off = pl.multiple_of(((nblk - 1 - i) if reverse else i) * SUBLANE, SUBLANE)
            xr, xi = br_ref[pl.ds(off, SUBLANE), :], bi_ref[pl.ds(off, SUBLANE), :]
            for k, (ar, ai) in zip((1, 2, 4), mult):
                shift = (SUBLANE - k) if reverse else k
                sr, si = pltpu.roll(xr, shift, 0), pltpu.roll(xi, shift, 0)
                xr, xi = xr + ar * sr - ai * si, xi + ar * si + ai * sr
            xr, xi = xr + PR * cr - PI * ci, xi + PR * ci + PI * cr
            xr_ref[pl.ds(off, SUBLANE), :] = xr
            xi_ref[pl.ds(off, SUBLANE), :] = xi
            out = (xr[last:last + 1, :], xi[last:last + 1, :])
            if fwd_state is not None:
                sr_ref, si_ref = rest[n_x - 2:n_x]
                before = pl.multiple_of(jnp.maximum(off - SUBLANE, 0), SUBLANE)
                shifted = []
                for s_ref in (sr_ref, si_ref):
                    head = jnp.where(off > 0, s_ref[pl.ds(before, SUBLANE), :][SUBLANE - 1:, :], 0.0)
                    shifted.append(jnp.where(row == 0, head, pltpu.roll(s_ref[pl.ds(off, SUBLANE), :], 1, 0)))
                pr_, pi_ = shifted
                out += (carry[2] + xr * pr_ + xi * pi_, carry[3] + xi * pr_ - xr * pi_)
            return out

        z = jnp.zeros((1, W), F32)
        init = (z, z) if fwd_state is None else (z, z, jnp.zeros((SUBLANE, W), F32), jnp.zeros((SUBLANE, W), F32))
        final = lax.fori_loop(0, nblk, step, init)
        y = _bdot(xr_ref[...], cr_ref[...], "nn") + _bdot(xi_ref[...], ci_ref[...], "nn")
        y_ref[...] = y if res is None else y + rest[0][...]
        if fwd_state is not None:
            rest[n_x + 3][...] = jnp.sum(final[2], axis=0, keepdims=True)
            rest[n_x + 4][...] = jnp.sum(final[3], axis=0, keepdims=True)

    col = pl.BlockSpec((L, W), lambda j: (0, j))
    chan = pl.BlockSpec((L, SB), lambda j: (0, j))
    par = pl.BlockSpec((SUBLANE, W), lambda j: (0, j))
    one = pl.BlockSpec((1, W), lambda j: (0, j))
    w_in = pl.BlockSpec((None, SB, W), lambda j: (j // hp, j % hp, j % hp))
    w_out = pl.BlockSpec((None, W, SB), lambda j: (j // hp, j % hp, j % hp))
    extra = ([] if res is None else [res]) + list(fwd_state or ())
    sds = jax.ShapeDtypeStruct
    return pl.pallas_call(
        body, grid=(nj,),
        in_specs=[chan, w_in, w_in, par, par, par, w_out, w_out] + ([] if res is None else [chan]) + [col] * (len(extra) - (res is not None)),
        out_specs=[col, col, chan] + ([] if fwd_state is None else [one, one]),
        out_shape=[sds((L, n), F32)] * 2 + [sds((L, S), F32)] + ([] if fwd_state is None else [sds((1, n), F32)] * 2),
        compiler_params=_params(("parallel",), VMEM_LIMIT), name=name,
    )(src, w_re, w_im, a124, pr, pi, c_re, c_im, *extra)


def _bd_all(mats, name, deps=()):
    n = len(mats)

    def body(*refs):
        for w_ref, o_ref in zip(refs[:n], refs[n + len(deps):]):
            G, A, B = w_ref.shape
            rows, cols = S5_GB * A, S5_GB * B
            tile = jnp.where(lax.broadcasted_iota(jnp.int32, (B, cols), 1) % B == lax.broadcasted_iota(jnp.int32, (B, cols), 0),
                             1.0, 0.0).astype(BF16)
            keep = (lax.broadcasted_iota(jnp.int32, (rows, cols), 0) // A) == (lax.broadcasted_iota(jnp.int32, (rows, cols), 1) // B)
            for kb in range(G // S5_GB):
                w2 = w_ref[kb * S5_GB:(kb + 1) * S5_GB].reshape(rows, B)
                o_ref[kb] = jnp.where(keep, _bdot(w2, tile, "nn"), 0.0).astype(o_ref.dtype)

    return pl.pallas_call(
        body, out_shape=[jax.ShapeDtypeStruct((m.shape[0] // S5_GB, S5_GB * m.shape[1], S5_GB * m.shape[2]), BF16) for m in mats],
        compiler_params=pltpu.CompilerParams(vmem_limit_bytes=VMEM_LIMIT), name=name,
    )(*mats, *deps)


def _place():
    return lax.axis_index("x"), lax.axis_index("y"), lax.axis_index("c")


def _slot(p):
    return 4 * p[0] + 2 * p[1] + p[2]


N_CHIP = N_DEV // 2
_HBM = pl.BlockSpec(memory_space=pltpu.HBM)
_SEM = pl.BlockSpec(memory_space=pltpu.SEMAPHORE)
_ANY = pl.BlockSpec(memory_space=pl.ANY)
_EFFECT = pltpu.SideEffectType.DATAFLOW_SIDE_EFFECTING


def _gather_routes(x, y, c):
    me = _slot((x, y, c))
    peers = [(x, y, 1 - c)] + [(px, py, c) for px, py in ((1 - x, y), (x, 1 - y), (1 - x, 1 - y))]
    return [(p, me, me, _slot(p)) for p in peers]


def _chip_routes(x, y, c):
    myq = 2 * x + y
    return [((px, py, c), 2 * px + py, myq, 2 * px + py) for px, py in ((1 - x, y), (x, 1 - y), (1 - x, 1 - y))]


def _pass_routes(x, y, c):
    chips = ((1 - x, y), (x, 1 - y), (1 - x, 1 - y))
    return [((x, y, 1 - c), _slot((px, py, c)), _slot((px, py, c)), _slot((px, py, 1 - c))) for px, py in chips]


def _split_copies(s_refs, l_refs, send_sems, recv_sems, routes, arrival):
    out = []
    rts = routes(*_place())
    for a in range(len(l_refs)):
        for k, (peer, src_slot, dst_slot, arr_slot) in enumerate(rts):
            s_ref = l_refs[a] if s_refs is None else s_refs[a]
            src = s_ref if src_slot is None else s_ref.at[src_slot]
            sem = a * len(rts) + k
            out.append(pltpu.make_async_remote_copy(
                src_ref=src, dst_ref=l_refs[a].at[arr_slot if arrival else dst_slot], send_sem=send_sems.at[sem],
                recv_sem=recv_sems.at[sem], device_id=peer, device_id_type=MESH))
    return out


def _split_start(srcs, lands, routes, n_routes, dep, name):
    n, ns = len(lands), 0 if srcs is None else len(srcs)
    bufs = [*(srcs or ()), *lands]

    def body(*refs):
        s_refs = None if srcs is None else refs[:ns]
        for cp in _split_copies(s_refs, refs[ns:ns + n], refs[ns + n + 1], refs[ns + n + 2], routes, False):
            cp.start()
        refs[-1][...] = jnp.zeros_like(refs[-1])

    sems = pltpu.SemaphoreType.DMA((n * n_routes,))
    res = pl.pallas_call(
        body, name=name,
        out_shape=(sems, sems, *[pltpu.HBM(a.shape, a.dtype) for a in bufs], jax.ShapeDtypeStruct((SUBLANE, LANE), F32)),
        in_specs=[_HBM] * (ns + n) + [_ANY], out_specs=(_SEM, _SEM, *[_HBM] * (ns + n), pl.BlockSpec(memory_space=pltpu.VMEM)),
        input_output_aliases={i: 2 + i for i in range(ns + n)},
        compiler_params=pltpu.CompilerParams(has_side_effects=_EFFECT),
    )(*[pltpu.with_memory_space_constraint(a, pltpu.HBM) for a in bufs], dep)
    return res[0], res[1], (None if srcs is None else list(res[2:2 + ns])), list(res[2 + ns:2 + ns + n]), res[-1]


def _split_wait(send_sems, recv_sems, srcs, lands, routes, after, name):
    n, ns = len(lands), 0 if srcs is None else len(srcs)
    bufs = [*(srcs or ()), *lands]
    after = list(after) if isinstance(after, (list, tuple)) else [after]

    def body(*refs):
        s_refs = None if srcs is None else refs[:ns]
        for cp in _split_copies(s_refs, refs[ns:ns + n], refs[ns + n], refs[ns + n + 1], routes, True):
            cp.wait_send()
            cp.wait_recv()

    res = pl.pallas_call(
        body, name=name, out_shape=[pltpu.HBM(a.shape, a.dtype) for a in bufs],
        in_specs=[_HBM] * (ns + n) + [_SEM, _SEM] + [_ANY] * len(after), out_specs=[_HBM] * (ns + n),
        input_output_aliases={i: i for i in range(ns + n)},
        compiler_params=pltpu.CompilerParams(has_side_effects=_EFFECT),
    )(*bufs, send_sems, recv_sems, *after)
    return (None if srcs is None else list(res[:ns])), list(res[ns:])


def _to_slot(w, layer, place, out_dtype, name):
    _, R, C = w.shape
    unit = SUBLANE * (4 // jnp.dtype(out_dtype).itemsize)
    tr = _tile(R, max(unit, (512 * 1024 // C) // unit * unit), unit)

    def body(p_ref, w_ref, o_ref):
        o_ref[...] = w_ref[...].astype(o_ref.dtype)

    return pl.pallas_call(
        body,
        grid_spec=pltpu.PrefetchScalarGridSpec(
            num_scalar_prefetch=1, grid=(R // tr,),
            in_specs=[pl.BlockSpec((None, tr, C), lambda i, p: (layer, i, 0))],
            out_specs=pl.BlockSpec((None, tr, C), lambda i, p: (2 * p[1] + p[0], i, 0))),
        out_shape=jax.ShapeDtypeStruct((N_DEV, R, C), out_dtype),
        compiler_params=_params(("parallel",), VMEM_LIMIT), name=name,
    )(place, w)


def _pair_routes(x, y, c):
    return [((x, y, 1 - c), 2 * q + (1 - c), q, q) for q in range(N_CHIP)]


def _pair_sum(x, got, place, name):
    _, R, C = x.shape
    unit = SUBLANE * (4 // x.dtype.itemsize)
    tr = _tile(R, max(unit, (512 * 1024 // C) // unit * unit), unit)

    def body(p_ref, x_ref, g_ref, o_ref, land_ref):
        s = (x_ref[...].astype(F32) + g_ref[...].astype(F32)).astype(o_ref.dtype)
        o_ref[...] = s

        @pl.when(pl.program_id(1) == p_ref[1])
        def _():
            land_ref[...] = s

    blk = lambda f: pl.BlockSpec((None, tr, C), f)
    return pl.pallas_call(
        body,
        grid_spec=pltpu.PrefetchScalarGridSpec(
            num_scalar_prefetch=1, grid=(R // tr, N_CHIP),
            in_specs=[blk(lambda i, q, p: (2 * q + p[0], i, 0)), blk(lambda i, q, p: (q, i, 0))],
            out_specs=[blk(lambda i, q, p: (q, i, 0)), blk(lambda i, q, p: (p[1], i, 0))]),
        out_shape=[jax.ShapeDtypeStruct(got.shape, x.dtype)] * 2,
        compiler_params=_params(("parallel", "arbitrary"), VMEM_LIMIT), name=name,
    )(place, x, got)


def _adamw_math(w, g, m, v):
    m = ADAM_B1 * m + (1.0 - ADAM_B1) * g
    v = ADAM_B2 * v + (1.0 - ADAM_B2) * (g * g)
    m_hat = m / (1.0 - ADAM_B1 ** ADAM_STEP)
    v_hat = v / (1.0 - ADAM_B2 ** ADAM_STEP)
    return -ADAM_LR * (m_hat / (jnp.sqrt(v_hat) + ADAM_EPS) + ADAM_WD * w), m, v


def _adamw(w, m, v, parts, layer, prev, name):
    nl, R, C = w.shape
    P = parts.shape[0]
    unit = SUBLANE * (4 // parts.dtype.itemsize)
    tr = _tile(R, max(unit, (128 * 1024 // C) // unit * unit), unit)

    def body(w_ref, m_ref, v_ref, p_ref, *rest):
        g_ref, d_ref, nm_ref, nv_ref = rest[-4:]
        g = p_ref[0].astype(F32)
        for p in range(1, P):
            g = g + p_ref[p].astype(F32)
        d, nm, nv = _adamw_math(w_ref[...], g, m_ref[...], v_ref[...])
        g_ref[...], d_ref[...], nm_ref[...], nv_ref[...] = g, d, nm, nv

    lay = pl.BlockSpec((None, tr, C), lambda i: (layer, i, 0))
    in_specs = [lay, lay, lay, pl.BlockSpec((P, tr, C), lambda i: (0, i, 0))]
    ins = [w, m, v, parts]
    aliases = {}
    if prev is not None:
        in_specs += [pl.BlockSpec(memory_space=pl.ANY)] * 4
        ins += list(prev)
        aliases = {4 + k: k for k in range(4)}
    return pl.pallas_call(
        body, grid=(R // tr,), in_specs=in_specs, out_specs=[lay] * 4,
        out_shape=[jax.ShapeDtypeStruct(w.shape, F32)] * 4, input_output_aliases=aliases,
        compiler_params=_params(("parallel",), VMEM_LIMIT), name=name,
    )(*ins)


def _gelu_tanh(x):
    return jax.nn.gelu(x, approximate=True)


def _pack(arrs):
    tile = SUBLANE * LANE
    out = []
    for a in arrs:
        f = a.reshape(-1).astype(F32)
        out.append(jnp.pad(f, (0, (-f.shape[0]) % tile)))
    return jnp.concatenate(out)


def _unpack(flat, shapes):
    tile = SUBLANE * LANE
    out, off = [], 0
    for s in shapes:
        n = math.prod(s)
        out.append(flat[off:off + n].reshape(s))
        off += n + (-n) % tile
    return out


def kernel(x, gla_norm, gla_w_in, gla_w_gate_up, gla_b_gate, gla_o_norm, gla_w_out, s5_norm, s5_w_in, s5_lam_re, s5_lam_im, s5_log_dt, s5_b_re, s5_b_im, s5_c_re, s5_c_im, s5_d, s5_w_out, mlp_norm, mlp_w_up, mlp_w_down, final_norm, loss_target, m_gla_norm, m_gla_w_in, m_gla_w_gate_up, m_gla_b_gate, m_gla_o_norm, m_gla_w_out, m_s5_norm, m_s5_w_in, m_s5_lam_re, m_s5_lam_im, m_s5_log_dt, m_s5_b_re, m_s5_b_im, m_s5_c_re, m_s5_c_im, m_s5_d, m_s5_w_out, m_mlp_norm, m_mlp_w_up, m_mlp_w_down, m_final_norm, v_gla_norm, v_gla_w_in, v_gla_w_gate_up, v_gla_b_gate, v_gla_o_norm, v_gla_w_out, v_s5_norm, v_s5_w_in, v_s5_lam_re, v_s5_lam_im, v_s5_log_dt, v_s5_b_re, v_s5_b_im, v_s5_c_re, v_s5_c_im, v_s5_d, v_s5_w_out, v_mlp_norm, v_mlp_w_up, v_mlp_w_down, v_final_norm):
    W = dict(gla_norm=gla_norm, gla_w_in=gla_w_in, gla_w_gate_up=gla_w_gate_up, gla_b_gate=gla_b_gate, gla_o_norm=gla_o_norm, gla_w_out=gla_w_out, s5_norm=s5_norm, s5_w_in=s5_w_in, s5_lam_re=s5_lam_re, s5_lam_im=s5_lam_im, s5_log_dt=s5_log_dt, s5_b_re=s5_b_re, s5_b_im=s5_b_im, s5_c_re=s5_c_re, s5_c_im=s5_c_im, s5_d=s5_d, s5_w_out=s5_w_out, mlp_norm=mlp_norm, mlp_w_up=mlp_w_up, mlp_w_down=mlp_w_down, final_norm=final_norm)
    M = dict(gla_norm=m_gla_norm, gla_w_in=m_gla_w_in, gla_w_gate_up=m_gla_w_gate_up, gla_b_gate=m_gla_b_gate, gla_o_norm=m_gla_o_norm, gla_w_out=m_gla_w_out, s5_norm=m_s5_norm, s5_w_in=m_s5_w_in, s5_lam_re=m_s5_lam_re, s5_lam_im=m_s5_lam_im, s5_log_dt=m_s5_log_dt, s5_b_re=m_s5_b_re, s5_b_im=m_s5_b_im, s5_c_re=m_s5_c_re, s5_c_im=m_s5_c_im, s5_d=m_s5_d, s5_w_out=m_s5_w_out, mlp_norm=m_mlp_norm, mlp_w_up=m_mlp_w_up, mlp_w_down=m_mlp_w_down, final_norm=m_final_norm)
    V = dict(gla_norm=v_gla_norm, gla_w_in=v_gla_w_in, gla_w_gate_up=v_gla_w_gate_up, gla_b_gate=v_gla_b_gate, gla_o_norm=v_gla_o_norm, gla_w_out=v_gla_w_out, s5_norm=v_s5_norm, s5_w_in=v_s5_w_in, s5_lam_re=v_s5_lam_re, s5_lam_im=v_s5_lam_im, s5_log_dt=v_s5_log_dt, s5_b_re=v_s5_b_re, s5_b_im=v_s5_b_im, s5_c_re=v_s5_c_re, s5_c_im=v_s5_c_im, s5_d=v_s5_d, s5_w_out=v_s5_w_out, mlp_norm=v_mlp_norm, mlp_w_up=v_mlp_w_up, mlp_w_down=v_mlp_w_down, final_norm=v_final_norm)
    names = list(W)
    big = ["gla_w_in", "gla_w_out", "s5_w_in", "s5_w_out", "mlp_w_up", "mlp_w_down"]
    small_sharded = {"gla_w_gate_up": 2, "s5_norm": 1, "s5_d": 1}
    small = [n for n in names if n not in big]

    _, L, D = x.shape
    n_gla, n_s5, depth = gla_norm.shape[0], s5_lam_re.shape[0], mlp_norm.shape[0]
    H = GLA_HEADS
    KW, VW = D // 2, D
    DK, DV = KW // H, VW // H
    IN = 2 * KW + 2 * VW + GLA_RANK
    INP = 2 * KW + 2 * VW + LANE
    SW = s5_lam_re.shape[1] * S5_GROUP
    G, N = s5_lam_re.shape[1], s5_lam_re.shape[2]
    nb = G // S5_GB
    dev = _slot(_place())

    place = jnp.stack([lax.axis_index("c"), 2 * lax.axis_index("x") + lax.axis_index("y")]).astype(jnp.int32)
    sm_sh = _pack([W[n] for n in small_sharded]).reshape(1, -1, LANE)
    groups = []
    for i in range(depth):
        j = i // 2
        groups.append([("gla_w_in", j), ("gla_w_out", j)] if i % 2 == 0 else [("s5_w_in", j), ("s5_w_out", j)])
        groups.append([("mlp_w_up", i), ("mlp_w_down", i)])
    groups[0] = [("small", 0)] + groups[0]
    tok = jnp.zeros((SUBLANE, LANE), F32)
    level1, level2, full = [], {}, {}
    for gi, keys in enumerate(groups):
        lands = [_to_slot(sm_sh, 0, place, F32, "own_small") if n == "small" else _to_slot(W[n], j, place, BF16, f"own_{n}_{j}")
                 for n, j in keys]
        send_sems, recv_sems, _, lands, tok = _split_start(None, lands, _gather_routes, 4, tok, f"gather_start_{gi}")
        level1.append((send_sems, recv_sems, lands))

    def arrive(gi, after):
        send_sems, recv_sems, lands = level1[gi]
        _, lands = _split_wait(send_sems, recv_sems, None, lands, _gather_routes, after, f"gather_wait_{gi}")
        send_sems, recv_sems, _, lands, token = _split_start(None, lands, _pass_routes, 3, after, f"gather_pass_{gi}")
        level2[gi] = (send_sems, recv_sems, lands)
        return token

    def fetch(gi, after):
        send_sems, recv_sems, lands = level2[gi]
        _, lands = _split_wait(send_sems, recv_sems, None, lands, _pass_routes, after, f"gather_done_{gi}")
        full.update(zip(groups[gi], lands))

    s5p = []
    for j in range(n_s5):
        lre, lim = s5_lam_re[j][:, None, :], s5_lam_im[j][:, None, :]
        ldt = s5_log_dt[j][:, None, None]
        brt, bit = jnp.swapaxes(s5_b_re[j], 1, 2), jnp.swapaxes(s5_b_im[j], 1, 2)
        ar, ai, bbr, bbi, *mult = _s5_params(lre, lim, ldt, brt, bit, f"s5{j}_params")
        pr, pi, a124, qr, qi, c124 = [jnp.swapaxes(t, 0, 1).reshape(SUBLANE, G * N) for t in mult]
        sw = lambda t: jnp.swapaxes(t, 1, 2)
        bd = _bd_all([bbr, bbi, sw(s5_c_re[j]), -sw(s5_c_im[j]), s5_c_re[j], -s5_c_im[j], sw(bbr), sw(bbi)], f"s5{j}_blockdiag",
                     (tok,) if j == n_s5 - 1 else ())
        s5p.append(dict(prm=(lre, lim, ldt, brt, bit), pr=pr, pi=pi, a124=a124, qr=qr, qi=qi, c124=c124, bd=dict(zip(
            ("bu_re", "bu_im", "cx_re", "cx_im", "w_re", "w_im", "du_re", "du_im"), bd))))

    tok = arrive(0, s5p[-1]["bd"]["bu_re"] if s5p else tok)
    fetch(0, tok)
    sm_all = full["small", 0]
    sm_parts = [_unpack(sm_all[d].reshape(-1), [W[n].shape for n in small_sharded]) for d in range(N_DEV)]
    wgu_full = jnp.concatenate([p[0] for p in sm_parts], axis=2)
    s5n_full = jnp.concatenate([p[1] for p in sm_parts], axis=1)
    s5d_full = jnp.concatenate([p[2] for p in sm_parts], axis=1)

    def gla_weights(j):
        w_in = full["gla_w_in", j]
        w_in = jnp.transpose(w_in, (1, 0, 2)).reshape(D, IN)
        w_in = jnp.pad(w_in, ((0, 0), (0, INP - IN)))
        w_out = full["gla_w_out", j].reshape(VW, D)
        wgu = jnp.pad(wgu_full[j], ((0, LANE - GLA_RANK), (0, 0))).astype(BF16)
        return w_in, w_out, wgu

    grads = {}
    h = x[0]
    saved = []

    for i in range(depth):
        j = i // 2
        if i > 0:
            fetch(2 * i, h)
        if i % 2 == 0:
            w_in, w_out, wgu = gla_weights(j)
            gn = gla_norm[j][None]
            hn = _rms_fwd(h, gn, f"gla{j}_norm")
            proj = _mm(hn, w_in, "nn", name=f"gla{j}_proj", tn=896)
            ahead = arrive(2 * i + 1, proj)
            bg, on = gla_b_gate[j][None], gla_o_norm[j][None]
            y, st = _gla_fwd(proj, wgu, bg, on, H=H, DK=DK, DV=DV, name=f"gla{j}_mix", deps=(ahead,))
            h_new = _mm(y, w_out, "nn", name=f"gla{j}_out", res=h)
            saved.append(("gla", dict(h=h, hn=hn, proj=proj, y=y, st=st, w_in=w_in, w_out=w_out, wgu=wgu, gn=gn, bg=bg, on=on)))
        else:
            w_in = full["s5_w_in", j].reshape(D, SW)
            w_out3 = full["s5_w_out", j]
            sn = s5n_full[j][None]
            hn = _rms_fwd(h, sn, f"s5{j}_norm")
            u = _mm(hn, w_in, "nn", name=f"s5{j}_in")
            sp = s5p[j]
            bd = sp["bd"]
            xr, xi, cx = _s5_scan(u, bd["bu_re"], bd["bu_im"], sp["a124"], sp["pr"], sp["pi"], bd["cx_re"], bd["cx_im"], None,
                                  reverse=False, name=f"s5{j}_scan")
            ahead = arrive(2 * i + 1, xr)
            dsk = s5d_full[j][None]

            def act(cxv, uv, dv, *_):
                ypre = cxv + dv * uv
                return ypre, _gelu_tanh(ypre)
            ypre, yg = _rowmap(act, [cx, u], [dsk, ahead], [(SW, F32), (SW, BF16)], name=f"s5{j}_act")
            z = _mm_nn_cb(yg, w_out3, name=f"s5{j}_out")

            def glu(zv, hv):
                return (hv + zv[:, :D] * jax.nn.sigmoid(zv[:, D:]),)
            h_new = _rowmap(glu, [z, h], [], [(D, F32)], name=f"s5{j}_glu")[0]
            saved.append(("s5", dict(h=h, hn=hn, u=u, xr=xr, xi=xi, ypre=ypre, yg=yg, z=z, w_in=w_in, w_out3=w_out3, sn=sn,
                                     dsk=dsk, **sp)))
        h = h_new
        fetch(2 * i + 1, h)
        w_up3 = full["mlp_w_up", i]
        w_down = full["mlp_w_down", i].reshape(4 * D, D)
        mn = mlp_norm[i][None]
        hn = _rms_fwd(h, mn, f"mlp{i}_norm")
        def sq_relu(zv):
            a = jnp.maximum(zv, 0.0)
            return zv, a * a
        z, s = _mm_nn_cb(hn, w_up3, name=f"mlp{i}_up", out_dtype=[F32, BF16], epi=sq_relu)
        ahead = (arrive(2 * i + 2, s),) if i + 1 < depth else ()
        h_new = _mm(s, w_down, "nn", name=f"mlp{i}_down", res=h, deps=ahead)
        saved.append(("mlp", dict(h=h, hn=hn, z=z, s=s, w_up3=w_up3, w_down=w_down, mn=mn)))
        h = h_new

    dh, dhb, sq, dfin = _loss_head(h, loss_target[0], final_norm[None], "loss_head")
    loss = lax.psum(0.5 * jnp.sum(sq) / D, ("x", "y", "c"))
    small_grads = {"final_norm": dfin[0]}
    big_parts = {n: {} for n in big}
    stacks = {n: {} for n in small if n != "final_norm"}
    flat3 = lambda a: a.reshape(a.shape[0], math.prod(a.shape[1:-1]), a.shape[-1])
    exchanges = []

    def pair_grads(keys):
        tag = "_".join(f"{n}{j}" for n, j in keys)
        parts8 = [flat3(big_parts[n][j]) for n, j in keys]
        lands = [lax.empty((N_CHIP, *p.shape[1:]), p.dtype) for p in parts8]
        send_sems, recv_sems, parts8, lands, token = _split_start(parts8, lands, _pair_routes, N_CHIP, place, f"grads_pair_{tag}")
        return (keys, tag, send_sems, recv_sems, parts8, lands), token

    def send_grads(pending, after):
        keys, tag, send_sems, recv_sems, parts8, lands = pending
        parts8, from_sib = _split_wait(send_sems, recv_sems, parts8, lands, _pair_routes, after, f"grads_paired_{tag}")
        sums, lands = zip(*[_pair_sum(p, g, place, f"grads_pair_sum_{n}{j}") for (n, j), p, g in zip(keys, parts8, from_sib)])
        send_sems, recv_sems, srcs, lands, token = _split_start(list(sums), list(lands), _chip_routes, 3, place, f"grads_start_{tag}")
        exchanges.append((keys, tag, send_sems, recv_sems, srcs, lands))
        return token

    carry = ()

    outs = {}
    s5_small = [n for n in small if n.startswith("s5_") and n not in small_sharded]

    def small_start(subset, tag):
        for n in subset:
            if n in stacks:
                small_grads[n] = jnp.stack([stacks[n][k] for k in range(len(stacks[n]))])
        part = _pack([small_grads[n] for n in subset]).reshape(1, -1, LANE)
        lands = [_to_slot(part, 0, place, F32, f"own_small_grads_{tag}")]
        send_sems, recv_sems, _, lands, _ = _split_start(None, lands, _gather_routes, 4, place, f"small_grads_start_{tag}")
        return subset, tag, send_sems, recv_sems, lands

    def small_finish(handle, after):
        subset, tag, send_sems, recv_sems, lands = handle
        _, lands = _split_wait(send_sems, recv_sems, None, lands, _gather_routes, after, f"small_grads_wait_{tag}")
        send_sems, recv_sems, _, lands, t = _split_start(None, lands, _pass_routes, 3, after, f"small_grads_pass_{tag}")
        _, lands = _split_wait(send_sems, recv_sems, None, lands, _pass_routes, t, f"small_grads_done_{tag}")
        parts_all = lands[0]
        rows = parts_all.shape[1]
        shapes = [small_grads[n].shape for n in subset]

        def full_layout(T):
            arrs = []
            for n, shp in zip(subset, shapes):
                if n in small_sharded:
                    ax = small_sharded[n]
                    arrs.append(lax.dynamic_update_slice_in_dim(jnp.zeros(shp, F32), T[n], dev * T[n].shape[ax], axis=ax))
                else:
                    arrs.append(T[n])
            return _pack(arrs).reshape(1, rows, LANE)

        res = _adamw(full_layout(W), full_layout(M), full_layout(V), parts_all, 0, None, f"adamw_small_{tag}")
        for key, flat in zip(("grad", "delta", "new_m", "new_v"), res):
            for n, a in zip(subset, _unpack(flat.reshape(-1), shapes)):
                if n in small_sharded:
                    ax = small_sharded[n]
                    a = lax.dynamic_slice_in_dim(a, dev * W[n].shape[ax], W[n].shape[ax], axis=ax)
                outs[key, n] = a
        return res[0]

    for idx in range(len(saved) - 1, -1, -1):
        kind, s = saved[idx]
        li = sum(1 for k, _ in saved[:idx] if k == kind)
        if kind == "mlp":
            def sq_relu_bwd(dsv, zv):
                return (dsv * 2.0 * jnp.maximum(zv, 0.0),)
            dz = _mm(dhb, s["w_down"], "nt", name=f"mlp{li}_dz", out_dtype=BF16, epi=sq_relu_bwd, epi_ins=(s["z"],), deps=carry)
            big_parts["mlp_w_down"][li] = _mm(s["s"], dhb, "tn", name=f"mlp{li}_dw_down", out_dtype=BF16).reshape(N_DEV, 4 * D // N_DEV, D)
            big_parts["mlp_w_up"][li] = _mm_tn_cbout(s["hn"], dz, N_DEV, name=f"mlp{li}_dw_up")
            pending, token = pair_grads([("mlp_w_down", li), ("mlp_w_up", li)])
            dhn = _mm_nt_cb(dz, s["w_up3"], name=f"mlp{li}_dhn")
            dh, dhb, dg = _rms_bwd(s["h"], dhn, dh, s["mn"], f"mlp{li}_dnorm", deps=(token,))
            stacks["mlp_norm"][li] = dg[0]
        elif kind == "gla":
            dyv = _mm(dhb, s["w_out"], "nt", name=f"gla{li}_dy", deps=carry)
            big_parts["gla_w_out"][li] = _mm(s["y"], dhb, "tn", name=f"gla{li}_dw_out", out_dtype=BF16).reshape(N_DEV, VW // N_DEV, D)
            dproj, don, dbg, dwgu = _gla_bwd(s["proj"], s["wgu"], s["bg"], s["on"], s["st"], dyv, H=H, DK=DK, DV=DV, name=f"gla{li}_dmix")
            dwgu = dwgu[:GLA_RANK]
            dw_in = _mm(s["hn"], dproj, "tn", name=f"gla{li}_dw_in", out_dtype=BF16, tn=896)[:, :IN]
            big_parts["gla_w_in"][li] = jnp.transpose(dw_in.reshape(D, N_DEV, IN // N_DEV), (1, 0, 2))
            pending, token = pair_grads([("gla_w_out", li), ("gla_w_in", li)])
            dhn = _mm(dproj, s["w_in"], "nt", name=f"gla{li}_dhn", tk=896)
            dh, dhb, dg = _rms_bwd(s["h"], dhn, dh, s["gn"], f"gla{li}_dnorm", deps=(token,))
            stacks["gla_norm"][li], stacks["gla_b_gate"][li], stacks["gla_o_norm"][li] = dg[0], dbg[0], don[0]
            stacks["gla_w_gate_up"][li] = dwgu
        else:
            def glu_bwd(zv, dhv, *_):
                a, sg = zv[:, :D], jax.nn.sigmoid(zv[:, D:])
                return (jnp.concatenate([dhv * sg, dhv * a * sg * (1.0 - sg)], axis=1),)
            dz = _rowmap(glu_bwd, [s["z"], dh], list(carry), [(2 * D, BF16)], name=f"s5{li}_dglu")[0]
            big_parts["s5_w_out"][li] = _mm_tn_cbout(s["yg"], dz, N_DEV, name=f"s5{li}_dw_out")
            dyg = _mm_nt_cb(dz, s["w_out3"], name=f"s5{li}_dyg")

            def act_bwd(dygv, ypv, uv, dv):
                _, vjp = jax.vjp(_gelu_tanh, ypv)
                dyp = vjp(dygv)[0]
                return dyp, dyp * dv, jnp.sum(dyp * uv, axis=0, keepdims=True)
            dyp, du, dd = _rowmap(act_bwd, [dyg, s["ypre"], s["u"]], [s["dsk"]], [(SW, F32), (SW, F32)], [SW], name=f"s5{li}_dact")
            bd = s["bd"]
            lr_, li_, du, dar, dai = _s5_scan(dyp, bd["w_re"], bd["w_im"], s["c124"], s["qr"], s["qi"], bd["du_re"], bd["du_im"], du,
                                              reverse=True, fwd_state=(s["xr"], s["xi"]), name=f"s5{li}_dscan")
            dbbr = _mm_tn_bd(s["u"], lr_, nb, S5_GROUP, N, name=f"s5{li}_dbb_re")
            dbbi = _mm_tn_bd(s["u"], li_, nb, S5_GROUP, N, name=f"s5{li}_dbb_im")
            dcr = _mm_tn_bd(dyp, s["xr"], nb, S5_GROUP, N, name=f"s5{li}_dc_re")
            dci = -_mm_tn_bd(dyp, s["xi"], nb, S5_GROUP, N, name=f"s5{li}_dc_im")
            dlre, dlim, dldt, dbrt, dbit = _s5_params_bwd(*s["prm"], dar.reshape(G, 1, N), dai.reshape(G, 1, N), dbbr, dbbi,
                                                         f"s5{li}_dparams")
            big_parts["s5_w_in"][li] = _mm(s["hn"], du, "tn", name=f"s5{li}_dw_in", out_dtype=BF16).reshape(N_DEV, D // N_DEV, SW)
            pending, token = pair_grads([("s5_w_out", li), ("s5_w_in", li)])
            dhn = _mm(du, s["w_in"], "nt", name=f"s5{li}_dhn")
            dh, dhb, dg = _rms_bwd(s["h"], dhn, dh, s["sn"], f"s5{li}_dnorm", deps=(token,))
            stacks["s5_norm"][li], stacks["s5_d"][li] = dg[0], dd[0]
            stacks["s5_lam_re"][li], stacks["s5_lam_im"][li], stacks["s5_log_dt"][li] = dlre[:, 0], dlim[:, 0], dldt[:, 0, 0]
            stacks["s5_b_re"][li], stacks["s5_b_im"][li] = jnp.swapaxes(dbrt, 1, 2), jnp.swapaxes(dbit, 1, 2)
            stacks["s5_c_re"][li], stacks["s5_c_im"][li] = dcr, dci
            if li == 0:
                s5_handle = small_start(s5_small, "s5")
        carry = (send_grads(pending, dh),)
    grad_x = dh[None]

    rest = small_start([n for n in small if n not in s5_small], "rest")

    stacked = {n: None for n in big}
    as3 = lambda a: a.reshape(a.shape[0], math.prod(a.shape[1:-1]), a.shape[-1])

    def finish(exchange, after):
        keys, tag, e_send, e_recv, e_srcs, e_lands = exchange
        _, e_lands = _split_wait(e_send, e_recv, e_srcs, e_lands, _chip_routes, after, f"grads_wait_{tag}")
        for (n, j), parts in zip(keys, e_lands):
            stacked[n] = _adamw(as3(W[n]), as3(M[n]), as3(V[n]), parts, j, stacked[n], f"adamw_{n}_{j}")
        return stacked[keys[0][0]][0]

    done = [small_finish(s5_handle, carry[0])]
    for exchange in exchanges[:-1]:
        finish(exchange, carry[0])
    done.append(small_finish(rest, carry[0]))
    last_keys = {n for n, _ in exchanges[-1][0]}
    finish(exchanges[-1], done + [stacked[n][0] for n in big if n not in last_keys])
    for n in big:
        for key, a in zip(("grad", "delta", "new_m", "new_v"), stacked[n]):
            outs[key, n] = a.reshape(W[n].shape)

    return (loss, grad_x, *[outs["grad", n] for n in names], *[outs["delta", n] for n in names],
            *[outs["new_m", n] for n in names], *[outs["new_v", n] for n in names])
```

```python
import functools
import math

import jax
import jax.numpy as jnp
from jax import lax
from jax.experimental import pallas as pl
from jax.experimental.pallas import tpu as pltpu

F32, BF16 = jnp.float32, jnp.bfloat16
MESH = pl.DeviceIdType.MESH
N_DEV = 8
LANE = 128
SUBLANE = 8
VMEM_LIMIT = 48 * 1024 * 1024

EPS = 1e-6
CHUNK = 64
GLA_HEADS = 4
GLA_RANK = 16
GLA_TEMP = 16.0
S5_GROUP = 16
S5_STATE = 64
S5_EIG_CLIP = -1e-4
S5_GB = 16
ADAM_LR, ADAM_B1, ADAM_B2, ADAM_EPS, ADAM_WD, ADAM_STEP = 0.001, 0.9, 0.999, 1e-08, 0.01, 10

_CONTRACT = {"nn": ((1,), (0,)), "tn": ((0,), (0,)), "nt": ((1,), (1,))}


def _tile(n, pref, unit=LANE):
    if n <= pref:
        return n
    t = (pref // unit) * unit
    while t > unit and n % t:
        t -= unit
    assert n % t == 0, (n, pref, unit)
    return t


def _params(sem, vmem=None):
    return pltpu.CompilerParams(dimension_semantics=sem, vmem_limit_bytes=vmem)


def _dot(a, b, dims, precision=None):
    return lax.dot_general(a, b, (_CONTRACT[dims], ((), ())), preferred_element_type=F32, precision=precision)


def _bdot(a, b, dims):
    return _dot(a.astype(BF16), b.astype(BF16), dims)


def _mm_call(a, b, *, dims, grid, a_spec, b_spec, o_spec, out_shape, out_dtype, name, res=None, epi=None, epi_ins=(), deps=(),
             acc_shape=None):
    nk = grid[2]
    acc_shape = acc_shape or tuple(d for d in o_spec.block_shape if d is not None)
    if res is not None:
        epi, epi_ins = (lambda r, x: (r + x,)), (res,)
    single = not isinstance(out_dtype, (list, tuple))
    out_dtypes = [out_dtype] if single else list(out_dtype)
    n_e, n_o, n_d = len(epi_ins), len(out_dtypes), len(deps)

    def body(*refs):
        a_ref, b_ref = refs[:2]
        e_refs, o_refs = refs[2:2 + n_e], refs[2 + n_e + n_d:2 + n_e + n_d + n_o]

        def finish(r):
            vals = (r,) if epi is None else epi(r, *[e[...].astype(F32) for e in e_refs])
            for o_ref, v in zip(o_refs, vals):
                o_ref[...] = v.astype(o_ref.dtype)

        if len(b_ref.shape) == 3:
            w = a_ref.shape[1] // b_ref.shape[0]
            d = sum(_bdot(a_ref[:, p * w:(p + 1) * w], b_ref[p], dims) for p in range(b_ref.shape[0]))
        else:
            d = _bdot(a_ref[...], b_ref[...], dims)
        if nk == 1:
            finish(d)
            return
        acc = refs[-1]
        k = pl.program_id(2)

        @pl.when(k == 0)
        def _():
            acc[...] = d

        @pl.when((k > 0) & (k < nk - 1))
        def _():
            acc[...] += d

        @pl.when(k == nk - 1)
        def _():
            finish(acc[...] + d)

    outs = pl.pallas_call(
        body, grid=grid, in_specs=[a_spec, b_spec] + [o_spec] * n_e + [pl.BlockSpec(memory_space=pl.ANY)] * n_d,
        out_specs=[o_spec] * n_o, out_shape=[jax.ShapeDtypeStruct(out_shape, dt) for dt in out_dtypes],
        scratch_shapes=[] if nk == 1 else [pltpu.VMEM(acc_shape, F32)],
        compiler_params=_params(("parallel", "parallel", "arbitrary"), VMEM_LIMIT), name=name,
    )(a, b, *epi_ins, *deps)
    return outs[0] if single else outs


TM, TN, TK = 1024, 1024, 2048


def _mm(a, b, dims, *, name, out_dtype=F32, res=None, epi=None, epi_ins=(), deps=(), tm=TM, tn=TN, tk=TK):
    if dims == "tn":
        (K, M), (_, N) = a.shape, b.shape
    elif dims == "nn":
        (M, K), (_, N) = a.shape, b.shape
    else:
        (M, K), (N, _) = a.shape, b.shape
    tm, tn, tk = _tile(M, tm), _tile(N, tn), _tile(K, tk)
    a_spec = pl.BlockSpec((tk, tm), lambda i, j, k: (k, i)) if dims == "tn" else pl.BlockSpec((tm, tk), lambda i, j, k: (i, k))
    b_spec = pl.BlockSpec((tn, tk), lambda i, j, k: (j, k)) if dims == "nt" else pl.BlockSpec((tk, tn), lambda i, j, k: (k, j))
    o_spec = pl.BlockSpec((tm, tn), lambda i, j, k: (i, j))
    return _mm_call(a, b, dims=dims, grid=(M // tm, N // tn, K // tk), a_spec=a_spec, b_spec=b_spec, o_spec=o_spec,
                    out_shape=(M, N), out_dtype=out_dtype, name=name, res=res, epi=epi, epi_ins=epi_ins, deps=deps)


def _mm_nn_cb(a, b3, *, name, out_dtype=F32, epi=None, tm=TM, tn=TN, tk=TK):
    (M, K), (P, _, Ns) = a.shape, b3.shape
    tm, tn, tk = _tile(M, tm), _tile(Ns, tn), _tile(K, tk)
    npb = Ns // tn
    return _mm_call(a, b3, dims="nn", grid=(M // tm, P * npb, K // tk),
                    a_spec=pl.BlockSpec((tm, tk), lambda i, j, k: (i, k)),
                    b_spec=pl.BlockSpec((None, tk, tn), lambda i, j, k: (j // npb, k, j % npb)),
                    o_spec=pl.BlockSpec((tm, tn), lambda i, j, k: (i, j)),
                    out_shape=(M, P * Ns), out_dtype=out_dtype, name=name, epi=epi)


def _mm_nt_cb(a, b3, *, name, out_dtype=F32, tm=TM, tn=TN, tk=TK):
    (M, _), (P, N, Ns) = a.shape, b3.shape
    tm, tn = _tile(M, tm), _tile(N, tn)
    if tk >= 2 * Ns and tk % Ns == 0 and P % (tk // Ns) == 0:
        sp = tk // Ns
        return _mm_call(a, b3, dims="nt", grid=(M // tm, N // tn, P // sp),
                        a_spec=pl.BlockSpec((tm, sp * Ns), lambda i, j, k: (i, k)),
                        b_spec=pl.BlockSpec((sp, tn, Ns), lambda i, j, k: (k, j, 0)),
                        o_spec=pl.BlockSpec((tm, tn), lambda i, j, k: (i, j)),
                        out_shape=(M, N), out_dtype=out_dtype, name=name)
    tk = _tile(Ns, tk)
    kpb = Ns // tk
    return _mm_call(a, b3, dims="nt", grid=(M // tm, N // tn, P * kpb),
                    a_spec=pl.BlockSpec((tm, tk), lambda i, j, k: (i, k)),
                    b_spec=pl.BlockSpec((None, tn, tk), lambda i, j, k: (k // kpb, j, k % kpb)),
                    o_spec=pl.BlockSpec((tm, tn), lambda i, j, k: (i, j)),
                    out_shape=(M, N), out_dtype=out_dtype, name=name)


def _mm_tn_cbout(a, b, parts, *, name, out_dtype=BF16, tm=TM, tn=TN, tk=TK):
    (K, M), (_, N) = a.shape, b.shape
    Ns = N // parts
    tm, tn, tk = _tile(M, tm), _tile(Ns, tn), _tile(K, tk)
    npb = Ns // tn
    return _mm_call(a, b, dims="tn", grid=(M // tm, parts * npb, K // tk),
                    a_spec=pl.BlockSpec((tk, tm), lambda i, j, k: (k, i)),
                    b_spec=pl.BlockSpec((tk, tn), lambda i, j, k: (k, j)),
                    o_spec=pl.BlockSpec((None, tm, tn), lambda i, j, k: (j // npb, i, j % npb)),
                    out_shape=(parts, M, Ns), out_dtype=out_dtype, name=name)


def _mm_tn_bd(a, b, nb, A, B, *, name, tk=1024):
    (K, MA), (_, NB) = a.shape, b.shape
    Ma, Nb = MA // nb, NB // nb
    tk = _tile(K, tk)

    def diagonal(r):
        keep = (lax.broadcasted_iota(jnp.int32, (Ma, Nb), 0) // A) == (lax.broadcasted_iota(jnp.int32, (Ma, Nb), 1) // B)
        fold = jnp.where(lax.broadcasted_iota(jnp.int32, (Nb, B), 0) % B == lax.broadcasted_iota(jnp.int32, (Nb, B), 1), 1.0, 0.0)
        return (_dot(jnp.where(keep, r, 0.0), fold, "nn", lax.Precision.HIGHEST),)

    out = _mm_call(a, b, dims="tn", grid=(nb, 1, K // tk),
                   a_spec=pl.BlockSpec((tk, Ma), lambda i, j, k: (k, i)),
                   b_spec=pl.BlockSpec((tk, Nb), lambda i, j, k: (k, i)),
                   o_spec=pl.BlockSpec((None, Ma, B), lambda i, j, k: (i, 0, 0)),
                   out_shape=(nb, Ma, B), out_dtype=F32, name=name, epi=diagonal, acc_shape=(Ma, Nb))
    return out.reshape(nb * (Ma // A), A, B)


def _rowmap(fn, rows, consts, out_defs, red_defs=(), *, name, tr=256):
    L = rows[0].shape[0]
    widest = max([r.shape[1] for r in rows] + [n for n, _ in out_defs])
    tr = _tile(L, max(SUBLANE * 2, min(tr, 512 * 1024 // widest)), SUBLANE * 2)
    n_in, n_o, n_d = len(rows) + len(consts), len(out_defs), len(red_defs)

    def body(*refs):
        res = fn(*[r[...] for r in refs[:n_in]])
        res = res if isinstance(res, (tuple, list)) else (res,)
        outs = refs[n_in:]
        for o_ref, val in zip(outs[:n_o], res[:n_o]):
            o_ref[...] = val.astype(o_ref.dtype)
        if n_d:
            @pl.when(pl.program_id(0) == 0)
            def _():
                for o_ref in outs[n_o:]:
                    o_ref[...] = jnp.zeros_like(o_ref)
            for o_ref, val in zip(outs[n_o:], res[n_o:]):
                o_ref[...] += val

    in_specs = [pl.BlockSpec((tr, r.shape[1]), lambda i: (i, 0)) for r in rows]
    in_specs += [pl.BlockSpec(c.shape, lambda i, nd=c.ndim: (0,) * nd) for c in consts]
    out_specs = [pl.BlockSpec((tr, n), lambda i: (i, 0)) for n, _ in out_defs]
    out_specs += [pl.BlockSpec((1, n), lambda i: (0, 0)) for n in red_defs]
    out_shape = [jax.ShapeDtypeStruct((L, n), dt) for n, dt in out_defs]
    out_shape += [jax.ShapeDtypeStruct((1, n), F32) for n in red_defs]
    return pl.pallas_call(body, grid=(L // tr,), in_specs=in_specs, out_specs=out_specs, out_shape=out_shape,
                          compiler_params=_params(("arbitrary",), VMEM_LIMIT), name=name)(*rows, *consts)


def _rms_parts(x):
    r = lax.rsqrt(jnp.mean(x * x, axis=-1, keepdims=True) + EPS)
    return r, x * r


def _rms_fwd(h, g, name, deps=()):
    def fn(x, gg, *_):
        _, xh = _rms_parts(x)
        return (xh * gg,)
    return _rowmap(fn, [h], [g, *deps], [(h.shape[1], BF16)], name=name)[0]


def _rms_bwd(h, dhn, dh, g, name, deps=()):
    def fn(x, dy, dres, gg, *_):
        r, xh = _rms_parts(x)
        dxh = dy * gg
        dx = dres + r * (dxh - xh * jnp.mean(dxh * xh, axis=-1, keepdims=True))
        return dx, dx, jnp.sum(dy * xh, axis=0, keepdims=True)
    D = h.shape[1]
    return _rowmap(fn, [h, dhn, dh], [g, *deps], [(D, F32), (D, BF16)], [D], name=name)


def _loss_head(h, tgt, g, name):
    D = h.shape[1]

    def fn(x, t, gg):
        r, xh = _rms_parts(x)
        diff = xh * gg - t
        dy = diff * (1.0 / D)
        dxh = dy * gg
        dx = r * (dxh - xh * jnp.mean(dxh * xh, axis=-1, keepdims=True))
        return dx, dx, jnp.sum(diff * diff, axis=0, keepdims=True), jnp.sum(dy * xh, axis=0, keepdims=True)
    return _rowmap(fn, [h, tgt], [g], [(D, F32), (D, BF16)], [D, D], name=name)


def _tri(n, strict):
    r = lax.broadcasted_iota(jnp.int32, (n, n), 0)
    c = lax.broadcasted_iota(jnp.int32, (n, n), 1)
    return jnp.where((c < r) if strict else (c <= r), 1.0, 0.0).astype(F32)


def _gla_gate(g_ref, wgu_ref, bg_ref):
    pre = _bdot(g_ref[...], wgu_ref[...], "nn") + bg_ref[...]
    la = (jnp.minimum(pre, 0.0) - jnp.log(1.0 + jnp.exp(-jnp.abs(pre)))) * (1.0 / GLA_TEMP)
    cum = _dot(_tri(CHUNK, False), la, "nn", lax.Precision.HIGHEST)
    return pre, cum, cum[CHUNK - 1:CHUNK, :]


def _gla_specs(H, DK, DV, cmap):
    KW, VW = H * DK, H * DV
    assert (2 * KW) % VW == 0 and (2 * KW + 2 * VW) % LANE == 0
    vb, gb = (2 * KW) // VW, (2 * KW + 2 * VW) // LANE
    return [
        pl.BlockSpec((CHUNK, KW), lambda c: (cmap(c), 0)),
        pl.BlockSpec((CHUNK, KW), lambda c: (cmap(c), 1)),
        pl.BlockSpec((CHUNK, VW), lambda c: (cmap(c), vb)),
        pl.BlockSpec((CHUNK, VW), lambda c: (cmap(c), vb + 1)),
        pl.BlockSpec((CHUNK, LANE), lambda c: (cmap(c), gb)),
        pl.BlockSpec((LANE, KW), lambda c: (0, 0)),
        pl.BlockSpec((1, KW), lambda c: (0, 0)),
        pl.BlockSpec((1, DV), lambda c: (0, 0)),
    ]


def _gla_fwd(proj, wgu, bg, on, *, H, DK, DV, name, deps=()):
    L = proj.shape[0]
    nc = L // CHUNK
    scale = DK ** -0.5
    n_d = len(deps)

    def body(q_ref, k_ref, v_ref, r_ref, g_ref, wgu_ref, bg_ref, on_ref, *rest):
        y_ref, st_ref, S = rest[n_d:]

        @pl.when(pl.program_id(0) == 0)
        def _():
            S[...] = jnp.zeros_like(S)
        _, cum, total = _gla_gate(g_ref, wgu_ref, bg_ref)
        kd = k_ref[...] * jnp.exp(total - cum)
        dec = jnp.exp(total)
        for h in range(H):
            ks, vs = slice(h * DK, (h + 1) * DK), slice(h * DV, (h + 1) * DV)
            St = S[h] * dec[:, ks] + _bdot(v_ref[:, vs], kd[:, ks], "tn")
            S[h] = St
            st_ref[h] = St
            o = _bdot(q_ref[:, ks] * scale, St, "nt")
            _, oh = _rms_parts(o)
            y_ref[:, vs] = (oh * on_ref[...] * jax.nn.silu(r_ref[:, vs])).astype(y_ref.dtype)

    return pl.pallas_call(
        body, grid=(nc,), in_specs=_gla_specs(H, DK, DV, lambda c: c) + [pl.BlockSpec(memory_space=pl.ANY)] * n_d,
        out_specs=[pl.BlockSpec((CHUNK, H * DV), lambda c: (c, 0)),
                   pl.BlockSpec((H, None, DV, DK), lambda c: (0, c, 0, 0))],
        out_shape=[jax.ShapeDtypeStruct((L, H * DV), BF16), jax.ShapeDtypeStruct((H, nc, DV, DK), F32)],
        scratch_shapes=[pltpu.VMEM((H, DV, DK), F32)],
        compiler_params=_params(("arbitrary",), VMEM_LIMIT), name=name,
    )(proj, proj, proj, proj, proj, wgu, bg, on, *deps)


def _gla_bwd(proj, wgu, bg, on, st, dy, *, H, DK, DV, name):
    L = proj.shape[0]
    nc = L // CHUNK
    KW, VW = H * DK, H * DV
    scale = DK ** -0.5
    rev = lambda c: nc - 1 - c

    def body(q_ref, k_ref, v_ref, r_ref, g_ref, wgu_ref, bg_ref, on_ref, sc_ref, sp_ref, dy_ref,
             dp_ref, don_ref, dbg_ref, dwgu_ref, G, decn):
        c = pl.program_id(0)

        @pl.when(c == 0)
        def _():
            for ref in (G, decn, don_ref, dbg_ref, dwgu_ref):
                ref[...] = jnp.zeros_like(ref)

        pre, cum, total = _gla_gate(g_ref, wgu_ref, bg_ref)
        ex = jnp.exp(total - cum)
        kd = k_ref[...] * ex
        dec = jnp.exp(total)
        gn = on_ref[...]
        alive = jnp.where(c == nc - 1, 0.0, 1.0)
        don = jnp.zeros_like(gn)
        ddec, dkd = [], []
        for h in range(H):
            ks, vs = slice(h * DK, (h + 1) * DK), slice(h * DV, (h + 1) * DV)
            qs = q_ref[:, ks] * scale
            Sc = sc_ref[h]
            o = _bdot(qs, Sc, "nt")
            rinv, oh = _rms_parts(o)
            r = r_ref[:, vs]
            sg = jax.nn.sigmoid(r)
            dyv = dy_ref[:, vs]
            d_on = dyv * (r * sg)
            dp_ref[:, 2 * KW + VW + h * DV:2 * KW + VW + (h + 1) * DV] = (
                dyv * (oh * gn) * (sg * (1.0 + r * (1.0 - sg)))).astype(dp_ref.dtype)
            don = don + jnp.sum(d_on * oh, axis=0, keepdims=True)
            dxh = d_on * gn
            do = rinv * (dxh - oh * jnp.mean(dxh * oh, axis=-1, keepdims=True))
            dp_ref[:, ks] = (_bdot(do, Sc, "nn") * scale).astype(dp_ref.dtype)
            Gt = G[h] * decn[:, ks] + _bdot(do, qs, "tn")
            G[h] = Gt
            dkd.append(_bdot(v_ref[:, vs], Gt, "nn"))
            dp_ref[:, 2 * KW + h * DV:2 * KW + (h + 1) * DV] = _bdot(kd[:, ks], Gt, "nt").astype(dp_ref.dtype)
            ddec.append(jnp.sum(Gt * (sp_ref[h] * alive), axis=0, keepdims=True))
        don_ref[...] += don
        decn[...] = dec
        dkd, ddec = jnp.concatenate(dkd, axis=1), jnp.concatenate(ddec, axis=1)
        dp_ref[:, KW:2 * KW] = (dkd * ex).astype(dp_ref.dtype)
        dla = ddec * dec + _dot(_tri(CHUNK, True), dkd * kd, "nn", lax.Precision.HIGHEST)
        dpre = dla * (1.0 / GLA_TEMP) * jax.nn.sigmoid(-pre)
        dbg_ref[...] += jnp.sum(dpre, axis=0, keepdims=True)
        dwgu_ref[...] += _bdot(g_ref[...], dpre, "tn")
        dp_ref[:, 2 * KW + 2 * VW:] = _bdot(dpre, wgu_ref[...], "nt").astype(dp_ref.dtype)

    in_specs = _gla_specs(H, DK, DV, rev) + [
        pl.BlockSpec((H, None, DV, DK), lambda c: (0, rev(c), 0, 0)),
        pl.BlockSpec((H, None, DV, DK), lambda c: (0, jnp.maximum(rev(c) - 1, 0), 0, 0)),
        pl.BlockSpec((CHUNK, VW), lambda c: (rev(c), 0)),
    ]
    W = 2 * KW + 2 * VW + LANE
    whole = lambda shp: pl.BlockSpec(shp, lambda c: (0, 0))
    sds = jax.ShapeDtypeStruct
    return pl.pallas_call(
        body, grid=(nc,), in_specs=in_specs,
        out_specs=[pl.BlockSpec((CHUNK, W), lambda c: (rev(c), 0)), whole((1, DV)), whole((1, KW)), whole((LANE, KW))],
        out_shape=[sds((L, W), BF16), sds((1, DV), F32), sds((1, KW), F32), sds((LANE, KW), F32)],
        scratch_shapes=[pltpu.VMEM((H, DV, DK), F32), pltpu.VMEM((1, KW), F32)],
        compiler_params=_params(("arbitrary",), VMEM_LIMIT), name=name,
    )(proj, proj, proj, proj, proj, wgu, bg, on, st, st, dy)


def _s5_param_fn(lam_re, lam_im, log_dt, brt, bit):
    lr = jnp.minimum(lam_re, S5_EIG_CLIP)
    li = lam_im
    dt = jnp.exp(log_dt)
    mag = jnp.exp(lr * dt)
    ang = li * dt
    ab_re = mag * jnp.cos(ang)
    ab_im = mag * jnp.sin(ang)
    den = lr * lr + li * li
    nr = ab_re - 1.0
    f_re = (nr * lr + ab_im * li) / den
    f_im = (ab_im * lr - nr * li) / den
    return ab_re, ab_im, f_re * brt - f_im * bit, f_re * bit + f_im * brt


def _s5_params(lam_re, lam_im, log_dt, brt, bit, name):
    G, _, N = lam_re.shape

    def body(lr_ref, li_ref, dt_ref, br_ref, bi_ref, ar_ref, ai_ref, bbr_ref, bbi_ref, pr_ref, pi_ref, a124_ref,
             qr_ref, qi_ref, c124_ref):
        lre, lim, ldt = lr_ref[...], li_ref[...], dt_ref[...]
        ar, ai, bbr, bbi = _s5_param_fn(lre, lim, ldt, br_ref[...], bi_ref[...])
        ar_ref[...], ai_ref[...], bbr_ref[...], bbi_ref[...] = ar, ai, bbr, bbi
        r = lax.broadcasted_iota(jnp.int32, (1, SUBLANE, 1), 1)
        dt = jnp.exp(ldt)
        rate, turn = jnp.minimum(lre, S5_EIG_CLIP) * dt, lim * dt

        def power(k):
            mag, ang = jnp.exp(k * rate), k * turn
            return mag * jnp.cos(ang), mag * jnp.sin(ang)
        pr_ref[...], pi_ref[...] = power((r + 1).astype(F32))
        qr, qi = power((SUBLANE - r).astype(F32))
        qr_ref[...], qi_ref[...] = qr, -qi
        re, im = power(jnp.where(r < 2, 1.0, jnp.where(r < 4, 2.0, jnp.where(r < 6, 4.0, 0.0))))
        a124_ref[...] = jnp.where(r % 2 == 0, re, im)
        c124_ref[...] = jnp.where(r % 2 == 0, re, -im)

    sds = jax.ShapeDtypeStruct
    return pl.pallas_call(
        body, out_shape=[sds((G, 1, N), F32), sds((G, 1, N), F32), sds(brt.shape, F32), sds(brt.shape, F32)]
        + [sds((G, SUBLANE, N), F32)] * 6, name=name,
    )(lam_re, lam_im, log_dt, brt, bit)


def _s5_params_bwd(lam_re, lam_im, log_dt, brt, bit, dar, dai, dbbr, dbbi, name):
    def body(lr_ref, li_ref, dt_ref, br_ref, bi_ref, dar_ref, dai_ref, dbbr_ref, dbbi_ref, *outs):
        _, vjp = jax.vjp(_s5_param_fn, lr_ref[...], li_ref[...], dt_ref[...], br_ref[...], bi_ref[...])
        for o_ref, val in zip(outs, vjp((dar_ref[...], dai_ref[...], dbbr_ref[...], dbbi_ref[...]))):
            o_ref[...] = val

    ins = (lam_re, lam_im, log_dt, brt, bit)
    return pl.pallas_call(body, out_shape=[jax.ShapeDtypeStruct(a.shape, F32) for a in ins], name=name)(
        *ins, dar, dai, dbbr, dbbi)


def _s5_scan(src, w_re, w_im, a124, pr, pi, c_re, c_im, res, *, reverse, name, fwd_state=None, W=512):
    L, S = src.shape
    n = pr.shape[1]
    W = _tile(n, W)
    nblk, nj = L // SUBLANE, n // W
    SB = S // nj
    hp = w_re.shape[1] // SB
    assert SB % LANE == 0 and w_re.shape[1] % SB == 0 and w_re.shape[2] == hp * W
    assert fwd_state is None or reverse
    n_x = (0 if res is None else 1) + (0 if fwd_state is None else 2)

    def body(src_ref, wr_ref, wi_ref, a_ref, pr_ref, pi_ref, cr_ref, ci_ref, *rest):
        xr_ref, xi_ref, y_ref = rest[n_x:n_x + 3]
        br_ref, bi_ref = xr_ref, xi_ref
        xr_ref[...] = _bdot(src_ref[...], wr_ref[...], "nn")
        xi_ref[...] = _bdot(src_ref[...], wi_ref[...], "nn")
        A = a_ref[...]
        PR, PI = pr_ref[...], pi_ref[...]
        row = lax.broadcasted_iota(jnp.int32, (SUBLANE, W), 0)
        last = 0 if reverse else SUBLANE - 1
        mult = []
        for j, k in enumerate((1, 2, 4)):
            keep = (row < SUBLANE - k) if reverse else (row >= k)
            mult.append((jnp.where(keep, A[2 * j:2 * j + 1, :], 0.0), jnp.where(keep, A[2 * j + 1:2 * j + 2, :], 0.0)))

        def step(i, carry):
            cr, ci = carry[:2]
            off = pl.multiple_of(((nblk - 1 - i) if reverse else i) * SUBLANE, SUBLANE)
            xr, xi = br_ref[pl.ds(off, SUBLANE), :], bi_ref[pl.ds(off, SUBLANE), :]
            for k, (ar, ai) in zip((1, 2, 4), mult):
                shift = (SUBLANE - k) if reverse else k
                sr, si = pltpu.roll(xr, shift, 0), pltpu.roll(xi, shift, 0)
                xr, xi = xr + ar * sr - ai * si, xi + ar * si + ai * sr
            xr, xi = xr + PR * cr - PI * ci, xi + PR * ci + PI * cr
            xr_ref[pl.ds(off, SUBLANE), :] = xr
            xi_ref[pl.ds(off, SUBLANE), :] = xi
            out = (xr[last:last + 1, :], xi[last:last + 1, :])
            if fwd_state is not None:
                sr_ref, si_ref = rest[n_x - 2:n_x]
                before = pl.multiple_of(jnp.maximum(off - SUBLANE, 0), SUBLANE)
                shifted = []
                for s_ref in (sr_ref, si_ref):
                    head = jnp.where(off > 0, s_ref[pl.ds(before, SUBLANE), :][SUBLANE - 1:, :], 0.0)
                    shifted.append(jnp.where(row == 0, head, pltpu.roll(s_ref[pl.ds(off, SUBLANE), :], 1, 0)))
                pr_, pi_ = shifted
                out += (carry[2] + xr * pr_ + xi * pi_, carry[3] + xi * pr_ - xr * pi_)
            return out

        z = jnp.zeros((1, W), F32)
        init = (z, z) if fwd_state is None else (z, z, jnp.zeros((SUBLANE, W), F32), jnp.zeros((SUBLANE, W), F32))
        final = lax.fori_loop(0, nblk, step, init)
        y = _bdot(xr_ref[...], cr_ref[...], "nn") + _bdot(xi_ref[...], ci_ref[...], "nn")
        y_ref[...] = y if res is None else y + rest[0][...]
        if fwd_state is not None:
            rest[n_x + 3][...] = jnp.sum(final[2], axis=0, keepdims=True)
            rest[n_x + 4][...] = jnp.sum(final[3], axis=0, keepdims=True)

    col = pl.BlockSpec((L, W), lambda j: (0, j))
    chan = pl.BlockSpec((L, SB), lambda j: (0, j))
    par = pl.BlockSpec((SUBLANE, W), lambda j: (0, j))
    one = pl.BlockSpec((1, W), lambda j: (0, j))
    w_in = pl.BlockSpec((None, SB, W), lambda j: (j // hp, j % hp, j % hp))
    w_out = pl.BlockSpec((None, W, SB), lambda j: (j // hp, j % hp, j % hp))
    extra = ([] if res is None else [res]) + list(fwd_state or ())
    sds = jax.ShapeDtypeStruct
    return pl.pallas_call(
        body, grid=(nj,),
        in_specs=[chan, w_in, w_in, par, par, par, w_out, w_out] + ([] if res is None else [chan]) + [col] * (len(extra) - (res is not None)),
        out_specs=[col, col, chan] + ([] if fwd_state is None else [one, one]),
        out_shape=[sds((L, n), F32)] * 2 + [sds((L, S), F32)] + ([] if fwd_state is None else [sds((1, n), F32)] * 2),
        compiler_params=_params(("parallel",), VMEM_LIMIT), name=name,
    )(src, w_re, w_im, a124, pr, pi, c_re, c_im, *extra)


def _bd_all(mats, name, deps=()):
    n = len(mats)

    def body(*refs):
        for w_ref, o_ref in zip(refs[:n], refs[n + len(deps):]):
            G, A, B = w_ref.shape
            rows, cols = S5_GB * A, S5_GB * B
            tile = jnp.where(lax.broadcasted_iota(jnp.int32, (B, cols), 1) % B == lax.broadcasted_iota(jnp.int32, (B, cols), 0),
                             1.0, 0.0).astype(BF16)
            keep = (lax.broadcasted_iota(jnp.int32, (rows, cols), 0) // A) == (lax.broadcasted_iota(jnp.int32, (rows, cols), 1) // B)
            for kb in range(G // S5_GB):
                w2 = w_ref[kb * S5_GB:(kb + 1) * S5_GB].reshape(rows, B)
                o_ref[kb] = jnp.where(keep, _bdot(w2, tile, "nn"), 0.0).astype(o_ref.dtype)

    return pl.pallas_call(
        body, out_shape=[jax.ShapeDtypeStruct((m.shape[0] // S5_GB, S5_GB * m.shape[1], S5_GB * m.shape[2]), BF16) for m in mats],
        compiler_params=pltpu.CompilerParams(vmem_limit_bytes=VMEM_LIMIT), name=name,
    )(*mats, *deps)


def _place():
    return lax.axis_index("x"), lax.axis_index("y"), lax.axis_index("c")


def _slot(p):
    return 4 * p[0] + 2 * p[1] + p[2]


N_CHIP = N_DEV // 2
_HBM = pl.BlockSpec(memory_space=pltpu.HBM)
_SEM = pl.BlockSpec(memory_space=pltpu.SEMAPHORE)
_ANY = pl.BlockSpec(memory_space=pl.ANY)
_EFFECT = pltpu.SideEffectType.DATAFLOW_SIDE_EFFECTING


def _gather_routes(x, y, c):
    me = _slot((x, y, c))
    peers = [(x, y, 1 - c)] + [(px, py, c) for px, py in ((1 - x, y), (x, 1 - y), (1 - x, 1 - y))]
    return [(p, me, me, _slot(p)) for p in peers]


def _chip_routes(x, y, c):
    myq = 2 * x + y
    return [((px, py, c), 2 * px + py, myq, 2 * px + py) for px, py in ((1 - x, y), (x, 1 - y), (1 - x, 1 - y))]


def _pass_routes(x, y, c):
    chips = ((1 - x, y), (x, 1 - y), (1 - x, 1 - y))
    return [((x, y, 1 - c), _slot((px, py, c)), _slot((px, py, c)), _slot((px, py, 1 - c))) for px, py in chips]


def _split_copies(s_refs, l_refs, send_sems, recv_sems, routes, arrival):
    out = []
    rts = routes(*_place())
    for a in range(len(l_refs)):
        for k, (peer, src_slot, dst_slot, arr_slot) in enumerate(rts):
            s_ref = l_refs[a] if s_refs is None else s_refs[a]
            src = s_ref if src_slot is None else s_ref.at[src_slot]
            sem = a * len(rts) + k
            out.append(pltpu.make_async_remote_copy(
                src_ref=src, dst_ref=l_refs[a].at[arr_slot if arrival else dst_slot], send_sem=send_sems.at[sem],
                recv_sem=recv_sems.at[sem], device_id=peer, device_id_type=MESH))
    return out


def _split_start(srcs, lands, routes, n_routes, dep, name):
    n, ns = len(lands), 0 if srcs is None else len(srcs)
    bufs = [*(srcs or ()), *lands]

    def body(*refs):
        s_refs = None if srcs is None else refs[:ns]
        for cp in _split_copies(s_refs, refs[ns:ns + n], refs[ns + n + 1], refs[ns + n + 2], routes, False):
            cp.start()
        refs[-1][...] = jnp.zeros_like(refs[-1])

    sems = pltpu.SemaphoreType.DMA((n * n_routes,))
    res = pl.pallas_call(
        body, name=name,
        out_shape=(sems, sems, *[pltpu.HBM(a.shape, a.dtype) for a in bufs], jax.ShapeDtypeStruct((SUBLANE, LANE), F32)),
        in_specs=[_HBM] * (ns + n) + [_ANY], out_specs=(_SEM, _SEM, *[_HBM] * (ns + n), pl.BlockSpec(memory_space=pltpu.VMEM)),
        input_output_aliases={i: 2 + i for i in range(ns + n)},
        compiler_params=pltpu.CompilerParams(has_side_effects=_EFFECT),
    )(*[pltpu.with_memory_space_constraint(a, pltpu.HBM) for a in bufs], dep)
    return res[0], res[1], (None if srcs is None else list(res[2:2 + ns])), list(res[2 + ns:2 + ns + n]), res[-1]


def _split_wait(send_sems, recv_sems, srcs, lands, routes, after, name):
    n, ns = len(lands), 0 if srcs is None else len(srcs)
    bufs = [*(srcs or ()), *lands]
    after = list(after) if isinstance(after, (list, tuple)) else [after]

    def body(*refs):
        s_refs = None if srcs is None else refs[:ns]
        for cp in _split_copies(s_refs, refs[ns:ns + n], refs[ns + n], refs[ns + n + 1], routes, True):
            cp.wait_send()
            cp.wait_recv()

    res = pl.pallas_call(
        body, name=name, out_shape=[pltpu.HBM(a.shape, a.dtype) for a in bufs],
        in_specs=[_HBM] * (ns + n) + [_SEM, _SEM] + [_ANY] * len(after), out_specs=[_HBM] * (ns + n),
        input_output_aliases={i: i for i in range(ns + n)},
        compiler_params=pltpu.CompilerParams(has_side_effects=_EFFECT),
    )(*bufs, send_sems, recv_sems, *after)
    return (None if srcs is None else list(res[:ns])), list(res[ns:])


def _to_slot(w, layer, place, out_dtype, name):
    _, R, C = w.shape
    unit = SUBLANE * (4 // jnp.dtype(out_dtype).itemsize)
    tr = _tile(R, max(unit, (512 * 1024 // C) // unit * unit), unit)

    def body(p_ref, w_ref, o_ref):
        o_ref[...] = w_ref[...].astype(o_ref.dtype)

    return pl.pallas_call(
        body,
        grid_spec=pltpu.PrefetchScalarGridSpec(
            num_scalar_prefetch=1, grid=(R // tr,),
            in_specs=[pl.BlockSpec((None, tr, C), lambda i, p: (layer, i, 0))],
            out_specs=pl.BlockSpec((None, tr, C), lambda i, p: (2 * p[1] + p[0], i, 0))),
        out_shape=jax.ShapeDtypeStruct((N_DEV, R, C), out_dtype),
        compiler_params=_params(("parallel",), VMEM_LIMIT), name=name,
    )(place, w)


def _pair_routes(x, y, c):
    return [((x, y, 1 - c), 2 * q + (1 - c), q, q) for q in range(N_CHIP)]


def _pair_sum(x, got, place, name):
    _, R, C = x.shape
    unit = SUBLANE * (4 // x.dtype.itemsize)
    tr = _tile(R, max(unit, (512 * 1024 // C) // unit * unit), unit)

    def body(p_ref, x_ref, g_ref, o_ref, land_ref):
        s = (x_ref[...].astype(F32) + g_ref[...].astype(F32)).astype(o_ref.dtype)
        o_ref[...] = s

        @pl.when(pl.program_id(1) == p_ref[1])
        def _():
            land_ref[...] = s

    blk = lambda f: pl.BlockSpec((None, tr, C), f)
    return pl.pallas_call(
        body,
        grid_spec=pltpu.PrefetchScalarGridSpec(
            num_scalar_prefetch=1, grid=(R // tr, N_CHIP),
            in_specs=[blk(lambda i, q, p: (2 * q + p[0], i, 0)), blk(lambda i, q, p: (q, i, 0))],
            out_specs=[blk(lambda i, q, p: (q, i, 0)), blk(lambda i, q, p: (p[1], i, 0))]),
        out_shape=[jax.ShapeDtypeStruct(got.shape, x.dtype)] * 2,
        compiler_params=_params(("parallel", "arbitrary"), VMEM_LIMIT), name=name,
    )(place, x, got)


def _adamw_math(w, g, m, v):
    m = ADAM_B1 * m + (1.0 - ADAM_B1) * g
    v = ADAM_B2 * v + (1.0 - ADAM_B2) * (g * g)
    m_hat = m / (1.0 - ADAM_B1 ** ADAM_STEP)
    v_hat = v / (1.0 - ADAM_B2 ** ADAM_STEP)
    return -ADAM_LR * (m_hat / (jnp.sqrt(v_hat) + ADAM_EPS) + ADAM_WD * w), m, v


def _adamw(w, m, v, parts, layer, prev, name):
    nl, R, C = w.shape
    P = parts.shape[0]
    unit = SUBLANE * (4 // parts.dtype.itemsize)
    tr = _tile(R, max(unit, (128 * 1024 // C) // unit * unit), unit)

    def body(w_ref, m_ref, v_ref, p_ref, *rest):
        g_ref, d_ref, nm_ref, nv_ref = rest[-4:]
        g = p_ref[0].astype(F32)
        for p in range(1, P):
            g = g + p_ref[p].astype(F32)
        d, nm, nv = _adamw_math(w_ref[...], g, m_ref[...], v_ref[...])
        g_ref[...], d_ref[...], nm_ref[...], nv_ref[...] = g, d, nm, nv

    lay = pl.BlockSpec((None, tr, C), lambda i: (layer, i, 0))
    in_specs = [lay, lay, lay, pl.BlockSpec((P, tr, C), lambda i: (0, i, 0))]
    ins = [w, m, v, parts]
    aliases = {}
    if prev is not None:
        in_specs += [pl.BlockSpec(memory_space=pl.ANY)] * 4
        ins += list(prev)
        aliases = {4 + k: k for k in range(4)}
    return pl.pallas_call(
        body, grid=(R // tr,), in_specs=in_specs, out_specs=[lay] * 4,
        out_shape=[jax.ShapeDtypeStruct(w.shape, F32)] * 4, input_output_aliases=aliases,
        compiler_params=_params(("parallel",), VMEM_LIMIT), name=name,
    )(*ins)


def _gelu_tanh(x):
    return jax.nn.gelu(x, approximate=True)


def _pack(arrs):
    tile = SUBLANE * LANE
    out = []
    for a in arrs:
        f = a.reshape(-1).astype(F32)
        out.append(jnp.pad(f, (0, (-f.shape[0]) % tile)))
    return jnp.concatenate(out)


def _unpack(flat, shapes):
    tile = SUBLANE * LANE
    out, off = [], 0
    for s in shapes:
        n = math.prod(s)
        out.append(flat[off:off + n].reshape(s))
        off += n + (-n) % tile
    return out


def kernel(x, gla_norm, gla_w_in, gla_w_gate_up, gla_b_gate, gla_o_norm, gla_w_out, s5_norm, s5_w_in, s5_lam_re, s5_lam_im, s5_log_dt, s5_b_re, s5_b_im, s5_c_re, s5_c_im, s5_d, s5_w_out, mlp_norm, mlp_w_up, mlp_w_down, final_norm, loss_target, m_gla_norm, m_gla_w_in, m_gla_w_gate_up, m_gla_b_gate, m_gla_o_norm, m_gla_w_out, m_s5_norm, m_s5_w_in, m_s5_lam_re, m_s5_lam_im, m_s5_log_dt, m_s5_b_re, m_s5_b_im, m_s5_c_re, m_s5_c_im, m_s5_d, m_s5_w_out, m_mlp_norm, m_mlp_w_up, m_mlp_w_down, m_final_norm, v_gla_norm, v_gla_w_in, v_gla_w_gate_up, v_gla_b_gate, v_gla_o_norm, v_gla_w_out, v_s5_norm, v_s5_w_in, v_s5_lam_re, v_s5_lam_im, v_s5_log_dt, v_s5_b_re, v_s5_b_im, v_s5_c_re, v_s5_c_im, v_s5_d, v_s5_w_out, v_mlp_norm, v_mlp_w_up, v_mlp_w_down, v_final_norm):
    W = dict(gla_norm=gla_norm, gla_w_in=gla_w_in, gla_w_gate_up=gla_w_gate_up, gla_b_gate=gla_b_gate, gla_o_norm=gla_o_norm, gla_w_out=gla_w_out, s5_norm=s5_norm, s5_w_in=s5_w_in, s5_lam_re=s5_lam_re, s5_lam_im=s5_lam_im, s5_log_dt=s5_log_dt, s5_b_re=s5_b_re, s5_b_im=s5_b_im, s5_c_re=s5_c_re, s5_c_im=s5_c_im, s5_d=s5_d, s5_w_out=s5_w_out, mlp_norm=mlp_norm, mlp_w_up=mlp_w_up, mlp_w_down=mlp_w_down, final_norm=final_norm)
    M = dict(gla_norm=m_gla_norm, gla_w_in=m_gla_w_in, gla_w_gate_up=m_gla_w_gate_up, gla_b_gate=m_gla_b_gate, gla_o_norm=m_gla_o_norm, gla_w_out=m_gla_w_out, s5_norm=m_s5_norm, s5_w_in=m_s5_w_in, s5_lam_re=m_s5_lam_re, s5_lam_im=m_s5_lam_im, s5_log_dt=m_s5_log_dt, s5_b_re=m_s5_b_re, s5_b_im=m_s5_b_im, s5_c_re=m_s5_c_re, s5_c_im=m_s5_c_im, s5_d=m_s5_d, s5_w_out=m_s5_w_out, mlp_norm=m_mlp_norm, mlp_w_up=m_mlp_w_up, mlp_w_down=m_mlp_w_down, final_norm=m_final_norm)
    V = dict(gla_norm=v_gla_norm, gla_w_in=v_gla_w_in, gla_w_gate_up=v_gla_w_gate_up, gla_b_gate=v_gla_b_gate, gla_o_norm=v_gla_o_norm, gla_w_out=v_gla_w_out, s5_norm=v_s5_norm, s5_w_in=v_s5_w_in, s5_lam_re=v_s5_lam_re, s5_lam_im=v_s5_lam_im, s5_log_dt=v_s5_log_dt, s5_b_re=v_s5_b_re, s5_b_im=v_s5_b_im, s5_c_re=v_s5_c_re, s5_c_im=v_s5_c_im, s5_d=v_s5_d, s5_w_out=v_s5_w_out, mlp_norm=v_mlp_norm, mlp_w_up=v_mlp_w_up, mlp_w_down=v_mlp_w_down, final_norm=v_final_norm)
    names = list(W)
    big = ["gla_w_in", "gla_w_out", "s5_w_in", "s5_w_out", "mlp_w_up", "mlp_w_down"]
    small_sharded = {"gla_w_gate_up": 2, "s5_norm": 1, "s5_d": 1}
    small = [n for n in names if n not in big]

    _, L, D = x.shape
    n_gla, n_s5, depth = gla_norm.shape[0], s5_lam_re.shape[0], mlp_norm.shape[0]
    H = GLA_HEADS
    KW, VW = D // 2, D
    DK, DV = KW // H, VW // H
    IN = 2 * KW + 2 * VW + GLA_RANK
    INP = 2 * KW + 2 * VW + LANE
    SW = s5_lam_re.shape[1] * S5_GROUP
    G, N = s5_lam_re.shape[1], s5_lam_re.shape[2]
    nb = G // S5_GB
    dev = _slot(_place())

    place = jnp.stack([lax.axis_index("c"), 2 * lax.axis_index("x") + lax.axis_index("y")]).astype(jnp.int32)
    sm_sh = _pack([W[n] for n in small_sharded]).reshape(1, -1, LANE)
    groups = []
    for i in range(depth):
        j = i // 2
        groups.append([("gla_w_in", j), ("gla_w_out", j)] if i % 2 == 0 else [("s5_w_in", j), ("s5_w_out", j)])
        groups.append([("mlp_w_up", i), ("mlp_w_down", i)])
    groups[0] = [("small", 0)] + groups[0]
    tok = jnp.zeros((SUBLANE, LANE), F32)
    level1, level2, full = [], {}, {}
    for gi, keys in enumerate(groups):
        lands = [_to_slot(sm_sh, 0, place, F32, "own_small") if n == "small" else _to_slot(W[n], j, place, BF16, f"own_{n}_{j}")
                 for n, j in keys]
        send_sems, recv_sems, _, lands, tok = _split_start(None, lands, _gather_routes, 4, tok, f"gather_start_{gi}")
        level1.append((send_sems, recv_sems, lands))

    def arrive(gi, after):
        send_sems, recv_sems, lands = level1[gi]
        _, lands = _split_wait(send_sems, recv_sems, None, lands, _gather_routes, after, f"gather_wait_{gi}")
        send_sems, recv_sems, _, lands, token = _split_start(None, lands, _pass_routes, 3, after, f"gather_pass_{gi}")
        level2[gi] = (send_sems, recv_sems, lands)
        return token

    def fetch(gi, after):
        send_sems, recv_sems, lands = level2[gi]
        _, lands = _split_wait(send_sems, recv_sems, None, lands, _pass_routes, after, f"gather_done_{gi}")
        full.update(zip(groups[gi], lands))

    s5p = []
    for j in range(n_s5):
        lre, lim = s5_lam_re[j][:, None, :], s5_lam_im[j][:, None, :]
        ldt = s5_log_dt[j][:, None, None]
        brt, bit = jnp.swapaxes(s5_b_re[j], 1, 2), jnp.swapaxes(s5_b_im[j], 1, 2)
        ar, ai, bbr, bbi, *mult = _s5_params(lre, lim, ldt, brt, bit, f"s5{j}_params")
        pr, pi, a124, qr, qi, c124 = [jnp.swapaxes(t, 0, 1).reshape(SUBLANE, G * N) for t in mult]
        sw = lambda t: jnp.swapaxes(t, 1, 2)
        bd = _bd_all([bbr, bbi, sw(s5_c_re[j]), -sw(s5_c_im[j]), s5_c_re[j], -s5_c_im[j], sw(bbr), sw(bbi)], f"s5{j}_blockdiag",
                     (tok,) if j == n_s5 - 1 else ())
        s5p.append(dict(prm=(lre, lim, ldt, brt, bit), pr=pr, pi=pi, a124=a124, qr=qr, qi=qi, c124=c124, bd=dict(zip(
            ("bu_re", "bu_im", "cx_re", "cx_im", "w_re", "w_im", "du_re", "du_im"), bd))))

    tok = arrive(0, s5p[-1]["bd"]["bu_re"] if s5p else tok)
    fetch(0, tok)
    sm_all = full["small", 0]
    sm_parts = [_unpack(sm_all[d].reshape(-1), [W[n].shape for n in small_sharded]) for d in range(N_DEV)]
    wgu_full = jnp.concatenate([p[0] for p in sm_parts], axis=2)
    s5n_full = jnp.concatenate([p[1] for p in sm_parts], axis=1)
    s5d_full = jnp.concatenate([p[2] for p in sm_parts], axis=1)

    def gla_weights(j):
        w_in = full["gla_w_in", j]
        w_in = jnp.transpose(w_in, (1, 0, 2)).reshape(D, IN)
        w_in = jnp.pad(w_in, ((0, 0), (0, INP - IN)))
        w_out = full["gla_w_out", j].reshape(VW, D)
        wgu = jnp.pad(wgu_full[j], ((0, LANE - GLA_RANK), (0, 0))).astype(BF16)
        return w_in, w_out, wgu

    grads = {}
    h = x[0]
    saved = []

    for i in range(depth):
        j = i // 2
        if i > 0:
            fetch(2 * i, h)
        if i % 2 == 0:
            w_in, w_out, wgu = gla_weights(j)
            gn = gla_norm[j][None]
            hn = _rms_fwd(h, gn, f"gla{j}_norm")
            proj = _mm(hn, w_in, "nn", name=f"gla{j}_proj", tn=896)
            ahead = arrive(2 * i + 1, proj)
            bg, on = gla_b_gate[j][None], gla_o_norm[j][None]
            y, st = _gla_fwd(proj, wgu, bg, on, H=H, DK=DK, DV=DV, name=f"gla{j}_mix", deps=(ahead,))
            h_new = _mm(y, w_out, "nn", name=f"gla{j}_out", res=h)
            saved.append(("gla", dict(h=h, hn=hn, proj=proj, y=y, st=st, w_in=w_in, w_out=w_out, wgu=wgu, gn=gn, bg=bg, on=on)))
        else:
            w_in = full["s5_w_in", j].reshape(D, SW)
            w_out3 = full["s5_w_out", j]
            sn = s5n_full[j][None]
            hn = _rms_fwd(h, sn, f"s5{j}_norm")
            u = _mm(hn, w_in, "nn", name=f"s5{j}_in")
            sp = s5p[j]
            bd = sp["bd"]
            xr, xi, cx = _s5_scan(u, bd["bu_re"], bd["bu_im"], sp["a124"], sp["pr"], sp["pi"], bd["cx_re"], bd["cx_im"], None,
                                  reverse=False, name=f"s5{j}_scan")
            ahead = arrive(2 * i + 1, xr)
            dsk = s5d_full[j][None]

            def act(cxv, uv, dv, *_):
                ypre = cxv + dv * uv
                return ypre, _gelu_tanh(ypre)
            ypre, yg = _rowmap(act, [cx, u], [dsk, ahead], [(SW, F32), (SW, BF16)], name=f"s5{j}_act")
            z = _mm_nn_cb(yg, w_out3, name=f"s5{j}_out")

            def glu(zv, hv):
                return (hv + zv[:, :D] * jax.nn.sigmoid(zv[:, D:]),)
            h_new = _rowmap(glu, [z, h], [], [(D, F32)], name=f"s5{j}_glu")[0]
            saved.append(("s5", dict(h=h, hn=hn, u=u, xr=xr, xi=xi, ypre=ypre, yg=yg, z=z, w_in=w_in, w_out3=w_out3, sn=sn,
                                     dsk=dsk, **sp)))
        h = h_new
        fetch(2 * i + 1, h)
        w_up3 = full["mlp_w_up", i]
        w_down = full["mlp_w_down", i].reshape(4 * D, D)
        mn = mlp_norm[i][None]
        hn = _rms_fwd(h, mn, f"mlp{i}_norm")
        def sq_relu(zv):
            a = jnp.maximum(zv, 0.0)
            return zv, a * a
        z, s = _mm_nn_cb(hn, w_up3, name=f"mlp{i}_up", out_dtype=[F32, BF16], epi=sq_relu)
        ahead = (arrive(2 * i + 2, s),) if i + 1 < depth else ()
        h_new = _mm(s, w_down, "nn", name=f"mlp{i}_down", res=h, deps=ahead)
        saved.append(("mlp", dict(h=h, hn=hn, z=z, s=s, w_up3=w_up3, w_down=w_down, mn=mn)))
        h = h_new

    dh, dhb, sq, dfin = _loss_head(h, loss_target[0], final_norm[None], "loss_head")
    loss = lax.psum(0.5 * jnp.sum(sq) / D, ("x", "y", "c"))
    small_grads = {"final_norm": dfin[0]}
    big_parts = {n: {} for n in big}
    stacks = {n: {} for n in small if n != "final_norm"}
    flat3 = lambda a: a.reshape(a.shape[0], math.prod(a.shape[1:-1]), a.shape[-1])
    exchanges = []

    def pair_grads(keys):
        tag = "_".join(f"{n}{j}" for n, j in keys)
        parts8 = [flat3(big_parts[n][j]) for n, j in keys]
        lands = [lax.empty((N_CHIP, *p.shape[1:]), p.dtype) for p in parts8]
        send_sems, recv_sems, parts8, lands, token = _split_start(parts8, lands, _pair_routes, N_CHIP, place, f"grads_pair_{tag}")
        return (keys, tag, send_sems, recv_sems, parts8, lands), token

    def send_grads(pending, after):
        keys, tag, send_sems, recv_sems, parts8, lands = pending
        parts8, from_sib = _split_wait(send_sems, recv_sems, parts8, lands, _pair_routes, after, f"grads_paired_{tag}")
        sums, lands = zip(*[_pair_sum(p, g, place, f"grads_pair_sum_{n}{j}") for (n, j), p, g in zip(keys, parts8, from_sib)])
        send_sems, recv_sems, srcs, lands, token = _split_start(list(sums), list(lands), _chip_routes, 3, place, f"grads_start_{tag}")
        exchanges.append((keys, tag, send_sems, recv_sems, srcs, lands))
        return token

    carry = ()

    outs = {}
    s5_small = [n for n in small if n.startswith("s5_") and n not in small_sharded]

    def small_start(subset, tag):
        for n in subset:
            if n in stacks:
                small_grads[n] = jnp.stack([stacks[n][k] for k in range(len(stacks[n]))])
        part = _pack([small_grads[n] for n in subset]).reshape(1, -1, LANE)
        lands = [_to_slot(part, 0, place, F32, f"own_small_grads_{tag}")]
        send_sems, recv_sems, _, lands, _ = _split_start(None, lands, _gather_routes, 4, place, f"small_grads_start_{tag}")
        return subset, tag, send_sems, recv_sems, lands

    def small_finish(handle, after):
        subset, tag, send_sems, recv_sems, lands = handle
        _, lands = _split_wait(send_sems, recv_sems, None, lands, _gather_routes, after, f"small_grads_wait_{tag}")
        send_sems, recv_sems, _, lands, t = _split_start(None, lands, _pass_routes, 3, after, f"small_grads_pass_{tag}")
        _, lands = _split_wait(send_sems, recv_sems, None, lands, _pass_routes, t, f"small_grads_done_{tag}")
        parts_all = lands[0]
        rows = parts_all.shape[1]
        shapes = [small_grads[n].shape for n in subset]

        def full_layout(T):
            arrs = []
            for n, shp in zip(subset, shapes):
                if n in small_sharded:
                    ax = small_sharded[n]
                    arrs.append(lax.dynamic_update_slice_in_dim(jnp.zeros(shp, F32), T[n], dev * T[n].shape[ax], axis=ax))
                else:
                    arrs.append(T[n])
            return _pack(arrs).reshape(1, rows, LANE)

        res = _adamw(full_layout(W), full_layout(M), full_layout(V), parts_all, 0, None, f"adamw_small_{tag}")
        for key, flat in zip(("grad", "delta", "new_m", "new_v"), res):
            for n, a in zip(subset, _unpack(flat.reshape(-1), shapes)):
                if n in small_sharded:
                    ax = small_sharded[n]
                    a = lax.dynamic_slice_in_dim(a, dev * W[n].shape[ax], W[n].shape[ax], axis=ax)
                outs[key, n] = a
        return res[0]

    for idx in range(len(saved) - 1, -1, -1):
        kind, s = saved[idx]
        li = sum(1 for k, _ in saved[:idx] if k == kind)
        if kind == "mlp":
            def sq_relu_bwd(dsv, zv):
                return (dsv * 2.0 * jnp.maximum(zv, 0.0),)
            dz = _mm(dhb, s["w_down"], "nt", name=f"mlp{li}_dz", out_dtype=BF16, epi=sq_relu_bwd, epi_ins=(s["z"],), deps=carry)
            big_parts["mlp_w_down"][li] = _mm(s["s"], dhb, "tn", name=f"mlp{li}_dw_down", out_dtype=BF16).reshape(N_DEV, 4 * D // N_DEV, D)
            big_parts["mlp_w_up"][li] = _mm_tn_cbout(s["hn"], dz, N_DEV, name=f"mlp{li}_dw_up")
            pending, token = pair_grads([("mlp_w_down", li), ("mlp_w_up", li)])
            dhn = _mm_nt_cb(dz, s["w_up3"], name=f"mlp{li}_dhn")
            dh, dhb, dg = _rms_bwd(s["h"], dhn, dh, s["mn"], f"mlp{li}_dnorm", deps=(token,))
            stacks["mlp_norm"][li] = dg[0]
        elif kind == "gla":
            dyv = _mm(dhb, s["w_out"], "nt", name=f"gla{li}_dy", deps=carry)
            big_parts["gla_w_out"][li] = _mm(s["y"], dhb, "tn", name=f"gla{li}_dw_out", out_dtype=BF16).reshape(N_DEV, VW // N_DEV, D)
            dproj, don, dbg, dwgu = _gla_bwd(s["proj"], s["wgu"], s["bg"], s["on"], s["st"], dyv, H=H, DK=DK, DV=DV, name=f"gla{li}_dmix")
            dwgu = dwgu[:GLA_RANK]
            dw_in = _mm(s["hn"], dproj, "tn", name=f"gla{li}_dw_in", out_dtype=BF16, tn=896)[:, :IN]
            big_parts["gla_w_in"][li] = jnp.transpose(dw_in.reshape(D, N_DEV, IN // N_DEV), (1, 0, 2))
            pending, token = pair_grads([("gla_w_out", li), ("gla_w_in", li)])
            dhn = _mm(dproj, s["w_in"], "nt", name=f"gla{li}_dhn", tk=896)
            dh, dhb, dg = _rms_bwd(s["h"], dhn, dh, s["gn"], f"gla{li}_dnorm", deps=(token,))
            stacks["gla_norm"][li], stacks["gla_b_gate"][li], stacks["gla_o_norm"][li] = dg[0], dbg[0], don[0]
            stacks["gla_w_gate_up"][li] = dwgu
        else:
            def glu_bwd(zv, dhv, *_):
                a, sg = zv[:, :D], jax.nn.sigmoid(zv[:, D:])
                return (jnp.concatenate([dhv * sg, dhv * a * sg * (1.0 - sg)], axis=1),)
            dz = _rowmap(glu_bwd, [s["z"], dh], list(carry), [(2 * D, BF16)], name=f"s5{li}_dglu")[0]
            big_parts["s5_w_out"][li] = _mm_tn_cbout(s["yg"], dz, N_DEV, name=f"s5{li}_dw_out")
            dyg = _mm_nt_cb(dz, s["w_out3"], name=f"s5{li}_dyg")

            def act_bwd(dygv, ypv, uv, dv):
                _, vjp = jax.vjp(_gelu_tanh, ypv)
                dyp = vjp(dygv)[0]
                return dyp, dyp * dv, jnp.sum(dyp * uv, axis=0, keepdims=True)
            dyp, du, dd = _rowmap(act_bwd, [dyg, s["ypre"], s["u"]], [s["dsk"]], [(SW, F32), (SW, F32)], [SW], name=f"s5{li}_dact")
            bd = s["bd"]
            lr_, li_, du, dar, dai = _s5_scan(dyp, bd["w_re"], bd["w_im"], s["c124"], s["qr"], s["qi"], bd["du_re"], bd["du_im"], du,
                                              reverse=True, fwd_state=(s["xr"], s["xi"]), name=f"s5{li}_dscan")
            dbbr = _mm_tn_bd(s["u"], lr_, nb, S5_GROUP, N, name=f"s5{li}_dbb_re")
            dbbi = _mm_tn_bd(s["u"], li_, nb, S5_GROUP, N, name=f"s5{li}_dbb_im")
            dcr = _mm_tn_bd(dyp, s["xr"], nb, S5_GROUP, N, name=f"s5{li}_dc_re")
            dci = -_mm_tn_bd(dyp, s["xi"], nb, S5_GROUP, N, name=f"s5{li}_dc_im")
            dlre, dlim, dldt, dbrt, dbit = _s5_params_bwd(*s["prm"], dar.reshape(G, 1, N), dai.reshape(G, 1, N), dbbr, dbbi,
                                                         f"s5{li}_dparams")
            big_parts["s5_w_in"][li] = _mm(s["hn"], du, "tn", name=f"s5{li}_dw_in", out_dtype=BF16).reshape(N_DEV, D // N_DEV, SW)
            pending, token = pair_grads([("s5_w_out", li), ("s5_w_in", li)])
            dhn = _mm(du, s["w_in"], "nt", name=f"s5{li}_dhn")
            dh, dhb, dg = _rms_bwd(s["h"], dhn, dh, s["sn"], f"s5{li}_dnorm", deps=(token,))
            stacks["s5_norm"][li], stacks["s5_d"][li] = dg[0], dd[0]
            stacks["s5_lam_re"][li], stacks["s5_lam_im"][li], stacks["s5_log_dt"][li] = dlre[:, 0], dlim[:, 0], dldt[:, 0, 0]
            stacks["s5_b_re"][li], stacks["s5_b_im"][li] = jnp.swapaxes(dbrt, 1, 2), jnp.swapaxes(dbit, 1, 2)
            stacks["s5_c_re"][li], stacks["s5_c_im"][li] = dcr, dci
            if li == 0:
                s5_handle = small_start(s5_small, "s5")
        carry = (send_grads(pending, dh),)
    grad_x = dh[None]

    rest = small_start([n for n in small if n not in s5_small], "rest")

    stacked = {n: None for n in big}
    as3 = lambda a: a.reshape(a.shape[0], math.prod(a.shape[1:-1]), a.shape[-1])

    def finish(exchange, after):
        keys, tag, e_send, e_recv, e_srcs, e_lands = exchange
        _, e_lands = _split_wait(e_send, e_recv, e_srcs, e_lands, _chip_routes, after, f"grads_wait_{tag}")
        for (n, j), parts in zip(keys, e_lands):
            stacked[n] = _adamw(as3(W[n]), as3(M[n]), as3(V[n]), parts, j, stacked[n], f"adamw_{n}_{j}")
        return stacked[keys[0][0]][0]

    done = [small_finish(s5_handle, carry[0])]
    for exchange in exchanges[:-1]:
        finish(exchange, carry[0])
    done.append(small_finish(rest, carry[0]))
    last_keys = {n for n, _ in exchanges[-1][0]}
    finish(exchanges[-1], done + [stacked[n][0] for n in big if n not in last_keys])
    for n in big:
        for key, a in zip(("grad", "delta", "new_m", "new_v"), stacked[n]):
            outs[key, n] = a.reshape(W[n].shape)

    return (loss, grad_x, *[outs["grad", n] for n in names], *[outs["delta", n] for n in names],
            *[outs["new_m", n] for n in names], *[outs["new_v", n] for n in names])
```

```python
import math

import jax
import jax.numpy as jnp
from jax import lax
from jax.experimental import pallas as pl
from jax.experimental.pallas import tpu as pltpu

F32, BF16 = jnp.float32, jnp.bfloat16
MESH = pl.DeviceIdType.MESH
N_DEV = 8
LANE = 128
SUBLANE = 8
VMEM_LIMIT = 48 * 1024 * 1024

EPS = 1e-6
CHUNK = 64
GLA_HEADS = 4
GLA_RANK = 16
GLA_TEMP = 16.0
S5_GROUP = 16
S5_EIG_CLIP = -1e-4
S5_GB = 16
ADAM_LR, ADAM_B1, ADAM_B2, ADAM_EPS, ADAM_WD, ADAM_STEP = 0.001, 0.9, 0.999, 1e-08, 0.01, 10

_CONTRACT = {"nn": ((1,), (0,)), "tn": ((0,), (0,)), "nt": ((1,), (1,))}


def _tile(n, pref, unit=LANE):
    if n <= pref:
        return n
    t = (pref // unit) * unit
    while t > unit and n % t:
        t -= unit
    assert n % t == 0, (n, pref, unit)
    return t


def _params(sem, vmem=None):
    return pltpu.CompilerParams(dimension_semantics=sem, vmem_limit_bytes=vmem)


def _dot(a, b, dims, precision=None):
    return lax.dot_general(a, b, (_CONTRACT[dims], ((), ())), preferred_element_type=F32, precision=precision)


def _bdot(a, b, dims):
    return _dot(a.astype(BF16), b.astype(BF16), dims)


def _mm_call(a, b, *, dims, grid, a_spec, b_spec, o_spec, out_shape, out_dtype, name, res=None, epi=None, epi_ins=(), deps=(),
             acc_shape=None):
    nk = grid[2]
    acc_shape = acc_shape or tuple(d for d in o_spec.block_shape if d is not None)
    if res is not None:
        epi, epi_ins = (lambda r, x: (r + x,)), (res,)
    single = not isinstance(out_dtype, (list, tuple))
    out_dtypes = [out_dtype] if single else list(out_dtype)
    n_e, n_o, n_d = len(epi_ins), len(out_dtypes), len(deps)

    def body(*refs):
        a_ref, b_ref = refs[:2]
        e_refs, o_refs = refs[2:2 + n_e], refs[2 + n_e + n_d:2 + n_e + n_d + n_o]

        def finish(r):
            vals = (r,) if epi is None else epi(r, *[e[...].astype(F32) for e in e_refs])
            for o_ref, v in zip(o_refs, vals):
                o_ref[...] = v.astype(o_ref.dtype)

        if len(b_ref.shape) == 3:
            w = a_ref.shape[1] // b_ref.shape[0]
            d = sum(_bdot(a_ref[:, p * w:(p + 1) * w], b_ref[p], dims) for p in range(b_ref.shape[0]))
        else:
            d = _bdot(a_ref[...], b_ref[...], dims)
        if nk == 1:
            finish(d)
            return
        acc = refs[-1]
        k = pl.program_id(2)

        @pl.when(k == 0)
        def _():
            acc[...] = d

        @pl.when((k > 0) & (k < nk - 1))
        def _():
            acc[...] += d

        @pl.when(k == nk - 1)
        def _():
            finish(acc[...] + d)

    outs = pl.pallas_call(
        body, grid=grid, in_specs=[a_spec, b_spec] + [o_spec] * n_e + [pl.BlockSpec(memory_space=pl.ANY)] * n_d,
        out_specs=[o_spec] * n_o, out_shape=[jax.ShapeDtypeStruct(out_shape, dt) for dt in out_dtypes],
        scratch_shapes=[] if nk == 1 else [pltpu.VMEM(acc_shape, F32)],
        compiler_params=_params(("parallel", "parallel", "arbitrary"), VMEM_LIMIT), name=name,
    )(a, b, *epi_ins, *deps)
    return outs[0] if single else outs


TM, TN, TK = 1024, 1024, 2048


def _mm(a, b, dims, *, name, out_dtype=F32, res=None, epi=None, epi_ins=(), deps=(), tm=TM, tn=TN, tk=TK):
    if dims == "tn":
        (K, M), (_, N) = a.shape, b.shape
    elif dims == "nn":
        (M, K), (_, N) = a.shape, b.shape
    else:
        (M, K), (N, _) = a.shape, b.shape
    tm, tn, tk = _tile(M, tm), _tile(N, tn), _tile(K, tk)
    a_spec = pl.BlockSpec((tk, tm), lambda i, j, k: (k, i)) if dims == "tn" else pl.BlockSpec((tm, tk), lambda i, j, k: (i, k))
    b_spec = pl.BlockSpec((tn, tk), lambda i, j, k: (j, k)) if dims == "nt" else pl.BlockSpec((tk, tn), lambda i, j, k: (k, j))
    o_spec = pl.BlockSpec((tm, tn), lambda i, j, k: (i, j))
    return _mm_call(a, b, dims=dims, grid=(M // tm, N // tn, K // tk), a_spec=a_spec, b_spec=b_spec, o_spec=o_spec,
                    out_shape=(M, N), out_dtype=out_dtype, name=name, res=res, epi=epi, epi_ins=epi_ins, deps=deps)


def _mm_nn_cb(a, b3, *, name, out_dtype=F32, epi=None, tm=TM, tn=TN, tk=TK):
    (M, K), (P, _, Ns) = a.shape, b3.shape
    tm, tn, tk = _tile(M, tm), _tile(Ns, tn), _tile(K, tk)
    npb = Ns // tn
    return _mm_call(a, b3, dims="nn", grid=(M // tm, P * npb, K // tk),
                    a_spec=pl.BlockSpec((tm, tk), lambda i, j, k: (i, k)),
                    b_spec=pl.BlockSpec((None, tk, tn), lambda i, j, k: (j // npb, k, j % npb)),
                    o_spec=pl.BlockSpec((tm, tn), lambda i, j, k: (i, j)),
                    out_shape=(M, P * Ns), out_dtype=out_dtype, name=name, epi=epi)


def _mm_nt_cb(a, b3, *, name, out_dtype=F32, tm=TM, tn=TN, tk=TK):
    (M, _), (P, N, Ns) = a.shape, b3.shape
    tm, tn = _tile(M, tm), _tile(N, tn)
    if tk >= 2 * Ns and tk % Ns == 0 and P % (tk // Ns) == 0:
        sp = tk // Ns
        return _mm_call(a, b3, dims="nt", grid=(M // tm, N // tn, P // sp),
                        a_spec=pl.BlockSpec((tm, sp * Ns), lambda i, j, k: (i, k)),
                        b_spec=pl.BlockSpec((sp, tn, Ns), lambda i, j, k: (k, j, 0)),
                        o_spec=pl.BlockSpec((tm, tn), lambda i, j, k: (i, j)),
                        out_shape=(M, N), out_dtype=out_dtype, name=name)
    tk = _tile(Ns, tk)
    kpb = Ns // tk
    return _mm_call(a, b3, dims="nt", grid=(M // tm, N // tn, P * kpb),
                    a_spec=pl.BlockSpec((tm, tk), lambda i, j, k: (i, k)),
                    b_spec=pl.BlockSpec((None, tn, tk), lambda i, j, k: (k // kpb, j, k % kpb)),
                    o_spec=pl.BlockSpec((tm, tn), lambda i, j, k: (i, j)),
                    out_shape=(M, N), out_dtype=out_dtype, name=name)


def _mm_tn_cbout(a, b, parts, *, name, out_dtype=BF16, tm=TM, tn=TN, tk=TK):
    (K, M), (_, N) = a.shape, b.shape
    Ns = N // parts
    tm, tn, tk = _tile(M, tm), _tile(Ns, tn), _tile(K, tk)
    npb = Ns // tn
    return _mm_call(a, b, dims="tn", grid=(M // tm, parts * npb, K // tk),
                    a_spec=pl.BlockSpec((tk, tm), lambda i, j, k: (k, i)),
                    b_spec=pl.BlockSpec((tk, tn), lambda i, j, k: (k, j)),
                    o_spec=pl.BlockSpec((None, tm, tn), lambda i, j, k: (j // npb, i, j % npb)),
                    out_shape=(parts, M, Ns), out_dtype=out_dtype, name=name)


def _mm_tn_bd(a, b, nb, A, B, *, name, tk=1024):
    (K, MA), (_, NB) = a.shape, b.shape
    Ma, Nb = MA // nb, NB // nb
    tk = _tile(K, tk)

    def diagonal(r):
        keep = (lax.broadcasted_iota(jnp.int32, (Ma, Nb), 0) // A) == (lax.broadcasted_iota(jnp.int32, (Ma, Nb), 1) // B)
        fold = jnp.where(lax.broadcasted_iota(jnp.int32, (Nb, B), 0) % B == lax.broadcasted_iota(jnp.int32, (Nb, B), 1), 1.0, 0.0)
        return (_dot(jnp.where(keep, r, 0.0), fold, "nn", lax.Precision.HIGHEST),)

    out = _mm_call(a, b, dims="tn", grid=(nb, 1, K // tk),
                   a_spec=pl.BlockSpec((tk, Ma), lambda i, j, k: (k, i)),
                   b_spec=pl.BlockSpec((tk, Nb), lambda i, j, k: (k, i)),
                   o_spec=pl.BlockSpec((None, Ma, B), lambda i, j, k: (i, 0, 0)),
                   out_shape=(nb, Ma, B), out_dtype=F32, name=name, epi=diagonal, acc_shape=(Ma, Nb))
    return out.reshape(nb * (Ma // A), A, B)


def _rowmap(fn, rows, consts, out_defs, red_defs=(), *, name, tr=256):
    L = rows[0].shape[0]
    widest = max([r.shape[1] for r in rows] + [n for n, _ in out_defs])
    tr = _tile(L, max(SUBLANE * 2, min(tr, 512 * 1024 // widest)), SUBLANE * 2)
    n_in, n_o, n_d = len(rows) + len(consts), len(out_defs), len(red_defs)

    def body(*refs):
        res = fn(*[r[...] for r in refs[:n_in]])
        res = res if isinstance(res, (tuple, list)) else (res,)
        outs = refs[n_in:]
        for o_ref, val in zip(outs[:n_o], res[:n_o]):
            o_ref[...] = val.astype(o_ref.dtype)
        if n_d:
            @pl.when(pl.program_id(0) == 0)
            def _():
                for o_ref in outs[n_o:]:
                    o_ref[...] = jnp.zeros_like(o_ref)
            for o_ref, val in zip(outs[n_o:], res[n_o:]):
                o_ref[...] += val

    in_specs = [pl.BlockSpec((tr, r.shape[1]), lambda i: (i, 0)) for r in rows]
    in_specs += [pl.BlockSpec(c.shape, lambda i, nd=c.ndim: (0,) * nd) for c in consts]
    out_specs = [pl.BlockSpec((tr, n), lambda i: (i, 0)) for n, _ in out_defs]
    out_specs += [pl.BlockSpec((1, n), lambda i: (0, 0)) for n in red_defs]
    out_shape = [jax.ShapeDtypeStruct((L, n), dt) for n, dt in out_defs]
    out_shape += [jax.ShapeDtypeStruct((1, n), F32) for n in red_defs]
    return pl.pallas_call(body, grid=(L // tr,), in_specs=in_specs, out_specs=out_specs, out_shape=out_shape,
                          compiler_params=_params(("arbitrary",), VMEM_LIMIT), name=name)(*rows, *consts)


def _rms_parts(x):
    r = lax.rsqrt(jnp.mean(x * x, axis=-1, keepdims=True) + EPS)
    return r, x * r


def _rms_fwd(h, g, name, deps=()):
    def fn(x, gg, *_):
        _, xh = _rms_parts(x)
        return (xh * gg,)
    return _rowmap(fn, [h], [g, *deps], [(h.shape[1], BF16)], name=name)[0]


def _rms_bwd(h, dhn, dh, g, name, deps=()):
    def fn(x, dy, dres, gg, *_):
        r, xh = _rms_parts(x)
        dxh = dy * gg
        dx = dres + r * (dxh - xh * jnp.mean(dxh * xh, axis=-1, keepdims=True))
        return dx, dx, jnp.sum(dy * xh, axis=0, keepdims=True)
    D = h.shape[1]
    return _rowmap(fn, [h, dhn, dh], [g, *deps], [(D, F32), (D, BF16)], [D], name=name)


def _loss_head(h, tgt, g, name):
    D = h.shape[1]

    def fn(x, t, gg):
        r, xh = _rms_parts(x)
        diff = xh * gg - t
        dy = diff * (1.0 / D)
        dxh = dy * gg
        dx = r * (dxh - xh * jnp.mean(dxh * xh, axis=-1, keepdims=True))
        return dx, dx, jnp.sum(diff * diff, axis=0, keepdims=True), jnp.sum(dy * xh, axis=0, keepdims=True)
    return _rowmap(fn, [h, tgt], [g], [(D, F32), (D, BF16)], [D, D], name=name)


def _tri(n, strict):
    r = lax.broadcasted_iota(jnp.int32, (n, n), 0)
    c = lax.broadcasted_iota(jnp.int32, (n, n), 1)
    return jnp.where((c < r) if strict else (c <= r), 1.0, 0.0).astype(F32)


def _gla_gate(g_ref, wgu_ref, bg_ref):
    pre = _bdot(g_ref[...], wgu_ref[...], "nn") + bg_ref[...]
    la = (jnp.minimum(pre, 0.0) - jnp.log(1.0 + jnp.exp(-jnp.abs(pre)))) * (1.0 / GLA_TEMP)
    cum = _dot(_tri(CHUNK, False), la, "nn", lax.Precision.HIGHEST)
    return pre, cum, cum[CHUNK - 1:CHUNK, :]


def _gla_specs(H, DK, DV, cmap):
    KW, VW = H * DK, H * DV
    assert (2 * KW) % VW == 0 and (2 * KW + 2 * VW) % LANE == 0
    vb, gb = (2 * KW) // VW, (2 * KW + 2 * VW) // LANE
    return [
        pl.BlockSpec((CHUNK, KW), lambda c: (cmap(c), 0)),
        pl.BlockSpec((CHUNK, KW), lambda c: (cmap(c), 1)),
        pl.BlockSpec((CHUNK, VW), lambda c: (cmap(c), vb)),
        pl.BlockSpec((CHUNK, VW), lambda c: (cmap(c), vb + 1)),
        pl.BlockSpec((CHUNK, LANE), lambda c: (cmap(c), gb)),
        pl.BlockSpec((LANE, KW), lambda c: (0, 0)),
        pl.BlockSpec((1, KW), lambda c: (0, 0)),
        pl.BlockSpec((1, DV), lambda c: (0, 0)),
    ]


def _gla_fwd(proj, wgu, bg, on, *, H, DK, DV, name, deps=()):
    L = proj.shape[0]
    nc = L // CHUNK
    scale = DK ** -0.5
    n_d = len(deps)

    def body(q_ref, k_ref, v_ref, r_ref, g_ref, wgu_ref, bg_ref, on_ref, *rest):
        y_ref, st_ref, S = rest[n_d:]

        @pl.when(pl.program_id(0) == 0)
        def _():
            S[...] = jnp.zeros_like(S)
        _, cum, total = _gla_gate(g_ref, wgu_ref, bg_ref)
        kd = k_ref[...] * jnp.exp(total - cum)
        dec = jnp.exp(total)
        for h in range(H):
            ks, vs = slice(h * DK, (h + 1) * DK), slice(h * DV, (h + 1) * DV)
            St = S[h] * dec[:, ks] + _bdot(v_ref[:, vs], kd[:, ks], "tn")
            S[h] = St
            st_ref[h] = St
            o = _bdot(q_ref[:, ks] * scale, St, "nt")
            _, oh = _rms_parts(o)
            y_ref[:, vs] = (oh * on_ref[...] * jax.nn.silu(r_ref[:, vs])).astype(y_ref.dtype)

    return pl.pallas_call(
        body, grid=(nc,), in_specs=_gla_specs(H, DK, DV, lambda c: c) + [pl.BlockSpec(memory_space=pl.ANY)] * n_d,
        out_specs=[pl.BlockSpec((CHUNK, H * DV), lambda c: (c, 0)),
                   pl.BlockSpec((H, None, DV, DK), lambda c: (0, c, 0, 0))],
        out_shape=[jax.ShapeDtypeStruct((L, H * DV), BF16), jax.ShapeDtypeStruct((H, nc, DV, DK), F32)],
        scratch_shapes=[pltpu.VMEM((H, DV, DK), F32)],
        compiler_params=_params(("arbitrary",), VMEM_LIMIT), name=name,
    )(proj, proj, proj, proj, proj, wgu, bg, on, *deps)


def _gla_bwd(proj, wgu, bg, on, st, dy, *, H, DK, DV, name):
    L = proj.shape[0]
    nc = L // CHUNK
    KW, VW = H * DK, H * DV
    scale = DK ** -0.5
    rev = lambda c: nc - 1 - c

    def body(q_ref, k_ref, v_ref, r_ref, g_ref, wgu_ref, bg_ref, on_ref, sc_ref, sp_ref, dy_ref,
             dp_ref, don_ref, dbg_ref, dwgu_ref, G, decn):
        c = pl.program_id(0)

        @pl.when(c == 0)
        def _():
            for ref in (G, decn, don_ref, dbg_ref, dwgu_ref):
                ref[...] = jnp.zeros_like(ref)

        pre, cum, total = _gla_gate(g_ref, wgu_ref, bg_ref)
        ex = jnp.exp(total - cum)
        kd = k_ref[...] * ex
        dec = jnp.exp(total)
        gn = on_ref[...]
        alive = jnp.where(c == nc - 1, 0.0, 1.0)
        don = jnp.zeros_like(gn)
        ddec, dkd = [], []
        for h in range(H):
            ks, vs = slice(h * DK, (h + 1) * DK), slice(h * DV, (h + 1) * DV)
            qs = q_ref[:, ks] * scale
            Sc = sc_ref[h]
            o = _bdot(qs, Sc, "nt")
            rinv, oh = _rms_parts(o)
            r = r_ref[:, vs]
            sg = jax.nn.sigmoid(r)
            dyv = dy_ref[:, vs]
            d_on = dyv * (r * sg)
            dp_ref[:, 2 * KW + VW + h * DV:2 * KW + VW + (h + 1) * DV] = (
                dyv * (oh * gn) * (sg * (1.0 + r * (1.0 - sg)))).astype(dp_ref.dtype)
            don = don + jnp.sum(d_on * oh, axis=0, keepdims=True)
            dxh = d_on * gn
            do = rinv * (dxh - oh * jnp.mean(dxh * oh, axis=-1, keepdims=True))
            dp_ref[:, ks] = (_bdot(do, Sc, "nn") * scale).astype(dp_ref.dtype)
            Gt = G[h] * decn[:, ks] + _bdot(do, qs, "tn")
            G[h] = Gt
            dkd.append(_bdot(v_ref[:, vs], Gt, "nn"))
            dp_ref[:, 2 * KW + h * DV:2 * KW + (h + 1) * DV] = _bdot(kd[:, ks], Gt, "nt").astype(dp_ref.dtype)
            ddec.append(jnp.sum(Gt * (sp_ref[h] * alive), axis=0, keepdims=True))
        don_ref[...] += don
        decn[...] = dec
        dkd, ddec = jnp.concatenate(dkd, axis=1), jnp.concatenate(ddec, axis=1)
        dp_ref[:, KW:2 * KW] = (dkd * ex).astype(dp_ref.dtype)
        dla = ddec * dec + _dot(_tri(CHUNK, True), dkd * kd, "nn", lax.Precision.HIGHEST)
        dpre = dla * (1.0 / GLA_TEMP) * jax.nn.sigmoid(-pre)
        dbg_ref[...] += jnp.sum(dpre, axis=0, keepdims=True)
        dwgu_ref[...] += _bdot(g_ref[...], dpre, "tn")
        dp_ref[:, 2 * KW + 2 * VW:] = _bdot(dpre, wgu_ref[...], "nt").astype(dp_ref.dtype)

    in_specs = _gla_specs(H, DK, DV, rev) + [
        pl.BlockSpec((H, None, DV, DK), lambda c: (0, rev(c), 0, 0)),
        pl.BlockSpec((H, None, DV, DK), lambda c: (0, jnp.maximum(rev(c) - 1, 0), 0, 0)),
        pl.BlockSpec((CHUNK, VW), lambda c: (rev(c), 0)),
    ]
    W = 2 * KW + 2 * VW + LANE
    whole = lambda shp: pl.BlockSpec(shp, lambda c: (0, 0))
    sds = jax.ShapeDtypeStruct
    return pl.pallas_call(
        body, grid=(nc,), in_specs=in_specs,
        out_specs=[pl.BlockSpec((CHUNK, W), lambda c: (rev(c), 0)), whole((1, DV)), whole((1, KW)), whole((LANE, KW))],
        out_shape=[sds((L, W), BF16), sds((1, DV), F32), sds((1, KW), F32), sds((LANE, KW), F32)],
        scratch_shapes=[pltpu.VMEM((H, DV, DK), F32), pltpu.VMEM((1, KW), F32)],
        compiler_params=_params(("arbitrary",), VMEM_LIMIT), name=name,
    )(proj, proj, proj, proj, proj, wgu, bg, on, st, st, dy)


def _s5_param_fn(lam_re, lam_im, log_dt, brt, bit):
    lr = jnp.minimum(lam_re, S5_EIG_CLIP)
    li = lam_im
    dt = jnp.exp(log_dt)
    mag = jnp.exp(lr * dt)
    ang = li * dt
    ab_re = mag * jnp.cos(ang)
    ab_im = mag * jnp.sin(ang)
    den = lr * lr + li * li
    nr = ab_re - 1.0
    f_re = (nr * lr + ab_im * li) / den
    f_im = (ab_im * lr - nr * li) / den
    return ab_re, ab_im, f_re * brt - f_im * bit, f_re * bit + f_im * brt


def _s5_params(lam_re, lam_im, log_dt, brt, bit, name):
    G, _, N = lam_re.shape

    def body(lr_ref, li_ref, dt_ref, br_ref, bi_ref, ar_ref, ai_ref, bbr_ref, bbi_ref, pr_ref, pi_ref, a124_ref,
             qr_ref, qi_ref, c124_ref):
        lre, lim, ldt = lr_ref[...], li_ref[...], dt_ref[...]
        ar, ai, bbr, bbi = _s5_param_fn(lre, lim, ldt, br_ref[...], bi_ref[...])
        ar_ref[...], ai_ref[...], bbr_ref[...], bbi_ref[...] = ar, ai, bbr, bbi
        r = lax.broadcasted_iota(jnp.int32, (1, SUBLANE, 1), 1)
        dt = jnp.exp(ldt)
        rate, turn = jnp.minimum(lre, S5_EIG_CLIP) * dt, lim * dt

        def power(k):
            mag, ang = jnp.exp(k * rate), k * turn
            return mag * jnp.cos(ang), mag * jnp.sin(ang)
        pr_ref[...], pi_ref[...] = power((r + 1).astype(F32))
        qr, qi = power((SUBLANE - r).astype(F32))
        qr_ref[...], qi_ref[...] = qr, -qi
        re, im = power(jnp.where(r < 2, 1.0, jnp.where(r < 4, 2.0, jnp.where(r < 6, 4.0, 0.0))))
        a124_ref[...] = jnp.where(r % 2 == 0, re, im)
        c124_ref[...] = jnp.where(r % 2 == 0, re, -im)

    sds = jax.ShapeDtypeStruct
    return pl.pallas_call(
        body, out_shape=[sds((G, 1, N), F32), sds((G, 1, N), F32), sds(brt.shape, F32), sds(brt.shape, F32)]
        + [sds((G, SUBLANE, N), F32)] * 6, name=name,
    )(lam_re, lam_im, log_dt, brt, bit)


def _s5_params_bwd(lam_re, lam_im, log_dt, brt, bit, dar, dai, dbbr, dbbi, name):
    def body(lr_ref, li_ref, dt_ref, br_ref, bi_ref, dar_ref, dai_ref, dbbr_ref, dbbi_ref, *outs):
        _, vjp = jax.vjp(_s5_param_fn, lr_ref[...], li_ref[...], dt_ref[...], br_ref[...], bi_ref[...])
        for o_ref, val in zip(outs, vjp((dar_ref[...], dai_ref[...], dbbr_ref[...], dbbi_ref[...]))):
            o_ref[...] = val

    ins = (lam_re, lam_im, log_dt, brt, bit)
    return pl.pallas_call(body, out_shape=[jax.ShapeDtypeStruct(a.shape, F32) for a in ins], name=name)(
        *ins, dar, dai, dbbr, dbbi)


def _s5_scan(src, w_re, w_im, a124, pr, pi, c_re, c_im, res, *, reverse, name, fwd_state=None, W=512):
    L, S = src.shape
    n = pr.shape[1]
    W = _tile(n, W)
    nblk, nj = L // SUBLANE, n // W
    SB = S // nj
    hp = w_re.shape[1] // SB
    assert SB % LANE == 0 and w_re.shape[1] % SB == 0 and w_re.shape[2] == hp * W
    assert fwd_state is None or reverse
    n_x = (0 if res is None else 1) + (0 if fwd_state is None else 2)

    def body(src_ref, wr_ref, wi_ref, a_ref, pr_ref, pi_ref, cr_ref, ci_ref, *rest):
        xr_ref, xi_ref, y_ref = rest[n_x:n_x + 3]
        br_ref, bi_ref = xr_ref, xi_ref
        xr_ref[...] = _bdot(src_ref[...], wr_ref[...], "nn")
        xi_ref[...] = _bdot(src_ref[...], wi_ref[...], "nn")
        A = a_ref[...]
        PR, PI = pr_ref[...], pi_ref[...]
        row = lax.broadcasted_iota(jnp.int32, (SUBLANE, W), 0)
        last = 0 if reverse else SUBLANE - 1
        mult = []
        for j, k in enumerate((1, 2, 4)):
            keep = (row < SUBLANE - k) if reverse else (row >= k)
            mult.append((jnp.where(keep, A[2 * j:2 * j + 1, :], 0.0), jnp.where(keep, A[2 * j + 1:2 * j + 2, :], 0.0)))

        def step(i, carry):
            cr, ci = carry[:2]
            off = pl.multiple_of(((nblk - 1 - i) if reverse else i) * SUBLANE, SUBLANE)
            xr, xi = br_ref[pl.ds(off, SUBLANE), :], bi_ref[pl.ds(off, SUBLANE), :]
            for k, (ar, ai) in zip((1, 2, 4), mult):
                shift = (SUBLANE - k) if reverse else k
                sr, si = pltpu.roll(xr, shift, 0), pltpu.roll(xi, shift, 0)
                xr, xi = xr + ar * sr - ai * si, xi + ar * si + ai * sr
            xr, xi = xr + PR * cr - PI * ci, xi + PR * ci + PI * cr
            xr_ref[pl.ds(off, SUBLANE), :] = xr
            xi_ref[pl.ds(off, SUBLANE), :] = xi
            out = (xr[last:last + 1, :], xi[last:last + 1, :])
            if fwd_state is not None:
                sr_ref, si_ref = rest[n_x - 2:n_x]
                before = pl.multiple_of(jnp.maximum(off - SUBLANE, 0), SUBLANE)
                shifted = []
                for s_ref in (sr_ref, si_ref):
                    head = jnp.where(off > 0, s_ref[pl.ds(before, SUBLANE), :][SUBLANE - 1:, :], 0.0)
                    shifted.append(jnp.where(row == 0, head, pltpu.roll(s_ref[pl.ds(off, SUBLANE), :], 1, 0)))
                pr_, pi_ = shifted
                out += (carry[2] + xr * pr_ + xi * pi_, carry[3] + xi * pr_ - xr * pi_)
            return out

        z = jnp.zeros((1, W), F32)
        init = (z, z) if fwd_state is None else (z, z, jnp.zeros((SUBLANE, W), F32), jnp.zeros((SUBLANE, W), F32))
        final = lax.fori_loop(0, nblk, step, init)
        y = _bdot(xr_ref[...], cr_ref[...], "nn") + _bdot(xi_ref[...], ci_ref[...], "nn")
        y_ref[...] = y if res is None else y + rest[0][...]
        if fwd_state is not None:
            rest[n_x + 3][...] = jnp.sum(final[2], axis=0, keepdims=True)
            rest[n_x + 4][...] = jnp.sum(final[3], axis=0, keepdims=True)

    col = pl.BlockSpec((L, W), lambda j: (0, j))
    chan = pl.BlockSpec((L, SB), lambda j: (0, j))
    par = pl.BlockSpec((SUBLANE, W), lambda j: (0, j))
    one = pl.BlockSpec((1, W), lambda j: (0, j))
    w_in = pl.BlockSpec((None, SB, W), lambda j: (j // hp, j % hp, j % hp))
    w_out = pl.BlockSpec((None, W, SB), lambda j: (j // hp, j % hp, j % hp))
    extra = ([] if res is None else [res]) + list(fwd_state or ())
    sds = jax.ShapeDtypeStruct
    return pl.pallas_call(
        body, grid=(nj,),
        in_specs=[chan, w_in, w_in, par, par, par, w_out, w_out] + ([] if res is None else [chan]) + [col] * (len(extra) - (res is not None)),
        out_specs=[col, col, chan] + ([] if fwd_state is None else [one, one]),
        out_shape=[sds((L, n), F32)] * 2 + [sds((L, S), F32)] + ([] if fwd_state is None else [sds((1, n), F32)] * 2),
        compiler_params=_params(("parallel",), VMEM_LIMIT), name=name,
    )(src, w_re, w_im, a124, pr, pi, c_re, c_im, *extra)


def _bd_all(mats, name, deps=()):
    n = len(mats)

    def body(*refs):
        for w_ref, o_ref in zip(refs[:n], refs[n + len(deps):]):
            G, A, B = w_ref.shape
            rows, cols = S5_GB * A, S5_GB * B
            tile = jnp.where(lax.broadcasted_iota(jnp.int32, (B, cols), 1) % B == lax.broadcasted_iota(jnp.int32, (B, cols), 0),
                             1.0, 0.0).astype(BF16)
            keep = (lax.broadcasted_iota(jnp.int32, (rows, cols), 0) // A) == (lax.broadcasted_iota(jnp.int32, (rows, cols), 1) // B)
            for kb in range(G // S5_GB):
                w2 = w_ref[kb * S5_GB:(kb + 1) * S5_GB].reshape(rows, B)
                o_ref[kb] = jnp.where(keep, _bdot(w2, tile, "nn"), 0.0).astype(o_ref.dtype)

    return pl.pallas_call(
        body, out_shape=[jax.ShapeDtypeStruct((m.shape[0] // S5_GB, S5_GB * m.shape[1], S5_GB * m.shape[2]), BF16) for m in mats],
        compiler_params=pltpu.CompilerParams(vmem_limit_bytes=VMEM_LIMIT), name=name,
    )(*mats, *deps)


def _place():
    return lax.axis_index("x"), lax.axis_index("y"), lax.axis_index("c")


def _slot(p):
    return 4 * p[0] + 2 * p[1] + p[2]


N_CHIP = N_DEV // 2
_HBM = pl.BlockSpec(memory_space=pltpu.HBM)
_SEM = pl.BlockSpec(memory_space=pltpu.SEMAPHORE)
_ANY = pl.BlockSpec(memory_space=pl.ANY)
_EFFECT = pltpu.SideEffectType.DATAFLOW_SIDE_EFFECTING


def _gather_routes(x, y, c):
    me = _slot((x, y, c))
    peers = [(x, y, 1 - c)] + [(px, py, c) for px, py in ((1 - x, y), (x, 1 - y), (1 - x, 1 - y))]
    return [(p, me, me, _slot(p)) for p in peers]


def _chip_routes(x, y, c):
    myq = 2 * x + y
    return [((px, py, c), 2 * px + py, myq, 2 * px + py) for px, py in ((1 - x, y), (x, 1 - y), (1 - x, 1 - y))]


def _pass_routes(x, y, c):
    chips = ((1 - x, y), (x, 1 - y), (1 - x, 1 - y))
    return [((x, y, 1 - c), _slot((px, py, c)), _slot((px, py, c)), _slot((px, py, 1 - c))) for px, py in chips]


def _split_copies(s_refs, l_refs, send_sems, recv_sems, routes, arrival):
    out = []
    rts = routes(*_place())
    for a in range(len(l_refs)):
        for k, (peer, src_slot, dst_slot, arr_slot) in enumerate(rts):
            s_ref = l_refs[a] if s_refs is None else s_refs[a]
            src = s_ref if src_slot is None else s_ref.at[src_slot]
            sem = a * len(rts) + k
            out.append(pltpu.make_async_remote_copy(
                src_ref=src, dst_ref=l_refs[a].at[arr_slot if arrival else dst_slot], send_sem=send_sems.at[sem],
                recv_sem=recv_sems.at[sem], device_id=peer, device_id_type=MESH))
    return out


def _split_start(srcs, lands, routes, n_routes, dep, name):
    n, ns = len(lands), 0 if srcs is None else len(srcs)
    bufs = [*(srcs or ()), *lands]

    def body(*refs):
        s_refs = None if srcs is None else refs[:ns]
        for cp in _split_copies(s_refs, refs[ns:ns + n], refs[ns + n + 1], refs[ns + n + 2], routes, False):
            cp.start()
        refs[-1][...] = jnp.zeros_like(refs[-1])

    sems = pltpu.SemaphoreType.DMA((n * n_routes,))
    res = pl.pallas_call(
        body, name=name,
        out_shape=(sems, sems, *[pltpu.HBM(a.shape, a.dtype) for a in bufs], jax.ShapeDtypeStruct((SUBLANE, LANE), F32)),
        in_specs=[_HBM] * (ns + n) + [_ANY], out_specs=(_SEM, _SEM, *[_HBM] * (ns + n), pl.BlockSpec(memory_space=pltpu.VMEM)),
        input_output_aliases={i: 2 + i for i in range(ns + n)},
        compiler_params=pltpu.CompilerParams(has_side_effects=_EFFECT),
    )(*[pltpu.with_memory_space_constraint(a, pltpu.HBM) for a in bufs], dep)
    return res[0], res[1], (None if srcs is None else list(res[2:2 + ns])), list(res[2 + ns:2 + ns + n]), res[-1]


def _split_wait(send_sems, recv_sems, srcs, lands, routes, after, name):
    n, ns = len(lands), 0 if srcs is None else len(srcs)
    bufs = [*(srcs or ()), *lands]
    after = list(after) if isinstance(after, (list, tuple)) else [after]

    def body(*refs):
        s_refs = None if srcs is None else refs[:ns]
        for cp in _split_copies(s_refs, refs[ns:ns + n], refs[ns + n], refs[ns + n + 1], routes, True):
            cp.wait_send()
            cp.wait_recv()

    res = pl.pallas_call(
        body, name=name, out_shape=[pltpu.HBM(a.shape, a.dtype) for a in bufs],
        in_specs=[_HBM] * (ns + n) + [_SEM, _SEM] + [_ANY] * len(after), out_specs=[_HBM] * (ns + n),
        input_output_aliases={i: i for i in range(ns + n)},
        compiler_params=pltpu.CompilerParams(has_side_effects=_EFFECT),
    )(*bufs, send_sems, recv_sems, *after)
    return (None if srcs is None else list(res[:ns])), list(res[ns:])


def _to_slot(w, layer, place, out_dtype, name):
    _, R, C = w.shape
    unit = SUBLANE * (4 // jnp.dtype(out_dtype).itemsize)
    tr = _tile(R, max(unit, (512 * 1024 // C) // unit * unit), unit)

    def body(p_ref, w_ref, o_ref):
        o_ref[...] = w_ref[...].astype(o_ref.dtype)

    return pl.pallas_call(
        body,
        grid_spec=pltpu.PrefetchScalarGridSpec(
            num_scalar_prefetch=1, grid=(R // tr,),
            in_specs=[pl.BlockSpec((None, tr, C), lambda i, p: (layer, i, 0))],
            out_specs=pl.BlockSpec((None, tr, C), lambda i, p: (2 * p[1] + p[0], i, 0))),
        out_shape=jax.ShapeDtypeStruct((N_DEV, R, C), out_dtype),
        compiler_params=_params(("parallel",), VMEM_LIMIT), name=name,
    )(place, w)


def _pair_routes(x, y, c):
    return [((x, y, 1 - c), 2 * q + (1 - c), q, q) for q in range(N_CHIP)]


def _pair_sum(x, got, place, name):
    _, R, C = x.shape
    unit = SUBLANE * (4 // x.dtype.itemsize)
    tr = _tile(R, max(unit, (512 * 1024 // C) // unit * unit), unit)

    def body(p_ref, x_ref, g_ref, o_ref, land_ref):
        s = (x_ref[...].astype(F32) + g_ref[...].astype(F32)).astype(o_ref.dtype)
        o_ref[...] = s

        @pl.when(pl.program_id(1) == p_ref[1])
        def _():
            land_ref[...] = s

    blk = lambda f: pl.BlockSpec((None, tr, C), f)
    return pl.pallas_call(
        body,
        grid_spec=pltpu.PrefetchScalarGridSpec(
            num_scalar_prefetch=1, grid=(R // tr, N_CHIP),
            in_specs=[blk(lambda i, q, p: (2 * q + p[0], i, 0)), blk(lambda i, q, p: (q, i, 0))],
            out_specs=[blk(lambda i, q, p: (q, i, 0)), blk(lambda i, q, p: (p[1], i, 0))]),
        out_shape=[jax.ShapeDtypeStruct(got.shape, x.dtype)] * 2,
        compiler_params=_params(("parallel", "arbitrary"), VMEM_LIMIT), name=name,
    )(place, x, got)


def _adamw_math(w, g, m, v):
    m = ADAM_B1 * m + (1.0 - ADAM_B1) * g
    v = ADAM_B2 * v + (1.0 - ADAM_B2) * (g * g)
    m_hat = m / (1.0 - ADAM_B1 ** ADAM_STEP)
    v_hat = v / (1.0 - ADAM_B2 ** ADAM_STEP)
    return -ADAM_LR * (m_hat / (jnp.sqrt(v_hat) + ADAM_EPS) + ADAM_WD * w), m, v


def _adamw(w, m, v, parts, layer, prev, name):
    nl, R, C = w.shape
    P = parts.shape[0]
    unit = SUBLANE * (4 // parts.dtype.itemsize)
    tr = _tile(R, max(unit, (128 * 1024 // C) // unit * unit), unit)

    def body(w_ref, m_ref, v_ref, p_ref, *rest):
        g_ref, d_ref, nm_ref, nv_ref = rest[-4:]
        g = p_ref[0].astype(F32)
        for p in range(1, P):
            g = g + p_ref[p].astype(F32)
        d, nm, nv = _adamw_math(w_ref[...], g, m_ref[...], v_ref[...])
        g_ref[...], d_ref[...], nm_ref[...], nv_ref[...] = g, d, nm, nv

    lay = pl.BlockSpec((None, tr, C), lambda i: (layer, i, 0))
    in_specs = [lay, lay, lay, pl.BlockSpec((P, tr, C), lambda i: (0, i, 0))]
    ins = [w, m, v, parts]
    aliases = {}
    if prev is not None:
        in_specs += [pl.BlockSpec(memory_space=pl.ANY)] * 4
        ins += list(prev)
        aliases = {4 + k: k for k in range(4)}
    return pl.pallas_call(
        body, grid=(R // tr,), in_specs=in_specs, out_specs=[lay] * 4,
        out_shape=[jax.ShapeDtypeStruct(w.shape, F32)] * 4, input_output_aliases=aliases,
        compiler_params=_params(("parallel",), VMEM_LIMIT), name=name,
    )(*ins)


def _gelu_tanh(x):
    return jax.nn.gelu(x, approximate=True)


def _pack(arrs):
    tile = SUBLANE * LANE
    out = []
    for a in arrs:
        f = a.reshape(-1).astype(F32)
        out.append(jnp.pad(f, (0, (-f.shape[0]) % tile)))
    return jnp.concatenate(out)


def _unpack(flat, shapes):
    tile = SUBLANE * LANE
    out, off = [], 0
    for s in shapes:
        n = math.prod(s)
        out.append(flat[off:off + n].reshape(s))
        off += n + (-n) % tile
    return out


def kernel(x, gla_norm, gla_w_in, gla_w_gate_up, gla_b_gate, gla_o_norm, gla_w_out, s5_norm, s5_w_in, s5_lam_re, s5_lam_im, s5_log_dt, s5_b_re, s5_b_im, s5_c_re, s5_c_im, s5_d, s5_w_out, mlp_norm, mlp_w_up, mlp_w_down, final_norm, loss_target, m_gla_norm, m_gla_w_in, m_gla_w_gate_up, m_gla_b_gate, m_gla_o_norm, m_gla_w_out, m_s5_norm, m_s5_w_in, m_s5_lam_re, m_s5_lam_im, m_s5_log_dt, m_s5_b_re, m_s5_b_im, m_s5_c_re, m_s5_c_im, m_s5_d, m_s5_w_out, m_mlp_norm, m_mlp_w_up, m_mlp_w_down, m_final_norm, v_gla_norm, v_gla_w_in, v_gla_w_gate_up, v_gla_b_gate, v_gla_o_norm, v_gla_w_out, v_s5_norm, v_s5_w_in, v_s5_lam_re, v_s5_lam_im, v_s5_log_dt, v_s5_b_re, v_s5_b_im, v_s5_c_re, v_s5_c_im, v_s5_d, v_s5_w_out, v_mlp_norm, v_mlp_w_up, v_mlp_w_down, v_final_norm):
    W = dict(gla_norm=gla_norm, gla_w_in=gla_w_in, gla_w_gate_up=gla_w_gate_up, gla_b_gate=gla_b_gate, gla_o_norm=gla_o_norm, gla_w_out=gla_w_out, s5_norm=s5_norm, s5_w_in=s5_w_in, s5_lam_re=s5_lam_re, s5_lam_im=s5_lam_im, s5_log_dt=s5_log_dt, s5_b_re=s5_b_re, s5_b_im=s5_b_im, s5_c_re=s5_c_re, s5_c_im=s5_c_im, s5_d=s5_d, s5_w_out=s5_w_out, mlp_norm=mlp_norm, mlp_w_up=mlp_w_up, mlp_w_down=mlp_w_down, final_norm=final_norm)
    M = dict(gla_norm=m_gla_norm, gla_w_in=m_gla_w_in, gla_w_gate_up=m_gla_w_gate_up, gla_b_gate=m_gla_b_gate, gla_o_norm=m_gla_o_norm, gla_w_out=m_gla_w_out, s5_norm=m_s5_norm, s5_w_in=m_s5_w_in, s5_lam_re=m_s5_lam_re, s5_lam_im=m_s5_lam_im, s5_log_dt=m_s5_log_dt, s5_b_re=m_s5_b_re, s5_b_im=m_s5_b_im, s5_c_re=m_s5_c_re, s5_c_im=m_s5_c_im, s5_d=m_s5_d, s5_w_out=m_s5_w_out, mlp_norm=m_mlp_norm, mlp_w_up=m_mlp_w_up, mlp_w_down=m_mlp_w_down, final_norm=m_final_norm)
    V = dict(gla_norm=v_gla_norm, gla_w_in=v_gla_w_in, gla_w_gate_up=v_gla_w_gate_up, gla_b_gate=v_gla_b_gate, gla_o_norm=v_gla_o_norm, gla_w_out=v_gla_w_out, s5_norm=v_s5_norm, s5_w_in=v_s5_w_in, s5_lam_re=v_s5_lam_re, s5_lam_im=v_s5_lam_im, s5_log_dt=v_s5_log_dt, s5_b_re=v_s5_b_re, s5_b_im=v_s5_b_im, s5_c_re=v_s5_c_re, s5_c_im=v_s5_c_im, s5_d=v_s5_d, s5_w_out=v_s5_w_out, mlp_norm=v_mlp_norm, mlp_w_up=v_mlp_w_up, mlp_w_down=v_mlp_w_down, final_norm=v_final_norm)
    names = list(W)
    big = ["gla_w_in", "gla_w_out", "s5_w_in", "s5_w_out", "mlp_w_up", "mlp_w_down"]
    small_sharded = {"gla_w_gate_up": 2, "s5_norm": 1, "s5_d": 1}
    small = [n for n in names if n not in big]

    _, L, D = x.shape
    n_gla, n_s5, depth = gla_norm.shape[0], s5_lam_re.shape[0], mlp_norm.shape[0]
    H = GLA_HEADS
    KW, VW = D // 2, D
    DK, DV = KW // H, VW // H
    IN = 2 * KW + 2 * VW + GLA_RANK
    INP = 2 * KW + 2 * VW + LANE
    SW = s5_lam_re.shape[1] * S5_GROUP
    G, N = s5_lam_re.shape[1], s5_lam_re.shape[2]
    nb = G // S5_GB
    dev = _slot(_place())

    place = jnp.stack([lax.axis_index("c"), 2 * lax.axis_index("x") + lax.axis_index("y")]).astype(jnp.int32)
    sm_sh = _pack([W[n] for n in small_sharded]).reshape(1, -1, LANE)
    groups = []
    for i in range(depth):
        j = i // 2
        groups.append([("gla_w_in", j), ("gla_w_out", j)] if i % 2 == 0 else [("s5_w_in", j), ("s5_w_out", j)])
        groups.append([("mlp_w_up", i), ("mlp_w_down", i)])
    groups.append([("small", 0), groups[0].pop()])
    HEAD = len(groups) - 1
    tok = jnp.zeros((SUBLANE, LANE), F32)
    level1, level2, full = {}, {}, {}
    for gi in [0, HEAD] + list(range(1, HEAD)):
        lands = [_to_slot(sm_sh, 0, place, F32, "own_small") if n == "small" else _to_slot(W[n], j, place, BF16, f"own_{n}_{j}")
                 for n, j in groups[gi]]
        send_sems, recv_sems, _, lands, tok = _split_start(None, lands, _gather_routes, 4, tok, f"gather_start_{gi}")
        level1[gi] = (send_sems, recv_sems, lands)

    def arrive(gi, after):
        send_sems, recv_sems, lands = level1[gi]
        _, lands = _split_wait(send_sems, recv_sems, None, lands, _gather_routes, after, f"gather_wait_{gi}")
        send_sems, recv_sems, _, lands, token = _split_start(None, lands, _pass_routes, 3, after, f"gather_pass_{gi}")
        level2[gi] = (send_sems, recv_sems, lands)
        return token

    def fetch(gi, after):
        send_sems, recv_sems, lands = level2[gi]
        _, lands = _split_wait(send_sems, recv_sems, None, lands, _pass_routes, after, f"gather_done_{gi}")
        full.update(zip(groups[gi], lands))

    s5p = []
    for j in range(n_s5):
        lre, lim = s5_lam_re[j][:, None, :], s5_lam_im[j][:, None, :]
        ldt = s5_log_dt[j][:, None, None]
        brt, bit = jnp.swapaxes(s5_b_re[j], 1, 2), jnp.swapaxes(s5_b_im[j], 1, 2)
        ar, ai, bbr, bbi, *mult = _s5_params(lre, lim, ldt, brt, bit, f"s5{j}_params")
        pr, pi, a124, qr, qi, c124 = [jnp.swapaxes(t, 0, 1).reshape(SUBLANE, G * N) for t in mult]
        sw = lambda t: jnp.swapaxes(t, 1, 2)
        bd = _bd_all([bbr, bbi, sw(s5_c_re[j]), -sw(s5_c_im[j]), s5_c_re[j], -s5_c_im[j], sw(bbr), sw(bbi)], f"s5{j}_blockdiag",
                     (tok,) if j == n_s5 - 1 else ())
        s5p.append(dict(prm=(lre, lim, ldt, brt, bit), pr=pr, pi=pi, a124=a124, qr=qr, qi=qi, c124=c124, bd=dict(zip(
            ("bu_re", "bu_im", "cx_re", "cx_im", "w_re", "w_im", "du_re", "du_im"), bd))))

    tok = arrive(0, s5p[-1]["bd"]["bu_re"] if s5p else tok)
    fetch(0, tok)
    sm = {}

    def unpack_small():
        sm_all = full["small", 0]
        sm_parts = [_unpack(sm_all[d].reshape(-1), [W[n].shape for n in small_sharded]) for d in range(N_DEV)]
        sm["wgu"] = jnp.concatenate([p[0] for p in sm_parts], axis=2)
        sm["s5n"] = jnp.concatenate([p[1] for p in sm_parts], axis=1)
        sm["s5d"] = jnp.concatenate([p[2] for p in sm_parts], axis=1)

    def gla_w_in_full(j):
        w_in = full["gla_w_in", j]
        w_in = jnp.transpose(w_in, (1, 0, 2)).reshape(D, IN)
        return jnp.pad(w_in, ((0, 0), (0, INP - IN)))

    def gla_rest(j):
        w_out = full["gla_w_out", j].reshape(VW, D)
        wgu = jnp.pad(sm["wgu"][j], ((0, LANE - GLA_RANK), (0, 0))).astype(BF16)
        return w_out, wgu

    h = x[0]
    saved = []

    for i in range(depth):
        j = i // 2
        if i > 0:
            fetch(2 * i, h)
        if i % 2 == 0:
            w_in = gla_w_in_full(j)
            gn = gla_norm[j][None]
            hn = _rms_fwd(h, gn, f"gla{j}_norm")
            proj = _mm(hn, w_in, "nn", name=f"gla{j}_proj", tn=896)
            if i == 0:
                fetch(HEAD, arrive(HEAD, proj))
                unpack_small()
            w_out, wgu = gla_rest(j)
            ahead = arrive(2 * i + 1, proj)
            bg, on = gla_b_gate[j][None], gla_o_norm[j][None]
            y, st = _gla_fwd(proj, wgu, bg, on, H=H, DK=DK, DV=DV, name=f"gla{j}_mix", deps=(ahead,))
            h_new = _mm(y, w_out, "nn", name=f"gla{j}_out", res=h)
            saved.append(("gla", dict(h=h, hn=hn, proj=proj, y=y, st=st, w_in=w_in, w_out=w_out, wgu=wgu, gn=gn, bg=bg, on=on)))
        else:
            w_in = full["s5_w_in", j].reshape(D, SW)
            w_out3 = full["s5_w_out", j]
            sn = sm["s5n"][j][None]
            hn = _rms_fwd(h, sn, f"s5{j}_norm")
            u = _mm(hn, w_in, "nn", name=f"s5{j}_in")
            sp = s5p[j]
            bd = sp["bd"]
            xr, xi, cx = _s5_scan(u, bd["bu_re"], bd["bu_im"], sp["a124"], sp["pr"], sp["pi"], bd["cx_re"], bd["cx_im"], None,
                                  reverse=False, name=f"s5{j}_scan")
            ahead = arrive(2 * i + 1, xr)
            dsk = sm["s5d"][j][None]

            def act(cxv, uv, dv, *_):
                ypre = cxv + dv * uv
                return ypre, _gelu_tanh(ypre)
            ypre, yg = _rowmap(act, [cx, u], [dsk, ahead], [(SW, F32), (SW, BF16)], name=f"s5{j}_act")
            z = _mm_nn_cb(yg, w_out3, name=f"s5{j}_out")

            def glu(zv, hv):
                return (hv + zv[:, :D] * jax.nn.sigmoid(zv[:, D:]),)
            h_new = _rowmap(glu, [z, h], [], [(D, F32)], name=f"s5{j}_glu")[0]
            saved.append(("s5", dict(h=h, hn=hn, u=u, xr=xr, xi=xi, ypre=ypre, yg=yg, z=z, w_in=w_in, w_out3=w_out3, sn=sn,
                                     dsk=dsk, **sp)))
        h = h_new
        fetch(2 * i + 1, h)
        w_up3 = full["mlp_w_up", i]
        w_down = full["mlp_w_down", i].reshape(4 * D, D)
        mn = mlp_norm[i][None]
        hn = _rms_fwd(h, mn, f"mlp{i}_norm")
        def sq_relu(zv):
            a = jnp.maximum(zv, 0.0)
            return zv, a * a
        z, s = _mm_nn_cb(hn, w_up3, name=f"mlp{i}_up", out_dtype=[F32, BF16], epi=sq_relu)
        ahead = (arrive(2 * i + 2, s),) if i + 1 < depth else ()
        h_new = _mm(s, w_down, "nn", name=f"mlp{i}_down", res=h, deps=ahead)
        saved.append(("mlp", dict(h=h, hn=hn, z=z, s=s, w_up3=w_up3, w_down=w_down, mn=mn)))
        h = h_new

    dh, dhb, sq, dfin = _loss_head(h, loss_target[0], final_norm[None], "loss_head")
    loss = lax.psum(0.5 * jnp.sum(sq) / D, ("x", "y", "c"))
    small_grads = {"final_norm": dfin[0]}
    big_parts = {n: {} for n in big}
    stacks = {n: {} for n in small if n != "final_norm"}
    flat3 = lambda a: a.reshape(a.shape[0], math.prod(a.shape[1:-1]), a.shape[-1])
    exchanges = []

    def pair_grads(keys):
        tag = "_".join(f"{n}{j}" for n, j in keys)
        parts8 = [flat3(big_parts[n][j]) for n, j in keys]
        lands = [lax.empty((N_CHIP, *p.shape[1:]), p.dtype) for p in parts8]
        send_sems, recv_sems, parts8, lands, token = _split_start(parts8, lands, _pair_routes, N_CHIP, place, f"grads_pair_{tag}")
        return (keys, tag, send_sems, recv_sems, parts8, lands), token

    def send_grads(pending, after):
        keys, tag, send_sems, recv_sems, parts8, lands = pending
        parts8, from_sib = _split_wait(send_sems, recv_sems, parts8, lands, _pair_routes, after, f"grads_paired_{tag}")
        sums, lands = zip(*[_pair_sum(p, g, place, f"grads_pair_sum_{n}{j}") for (n, j), p, g in zip(keys, parts8, from_sib)])
        send_sems, recv_sems, srcs, lands, token = _split_start(list(sums), list(lands), _chip_routes, 3, place, f"grads_start_{tag}")
        exchanges.append((keys, tag, send_sems, recv_sems, srcs, lands))
        return token

    carry = ()

    outs = {}
    s5_small = [n for n in small if n.startswith("s5_") and n not in small_sharded]

    def small_start(subset, tag):
        for n in subset:
            if n in stacks:
                small_grads[n] = jnp.stack([stacks[n][k] for k in range(len(stacks[n]))])
        part = _pack([small_grads[n] for n in subset]).reshape(1, -1, LANE)
        lands = [_to_slot(part, 0, place, F32, f"own_small_grads_{tag}")]
        send_sems, recv_sems, _, lands, _ = _split_start(None, lands, _gather_routes, 4, place, f"small_grads_start_{tag}")
        return subset, tag, send_sems, recv_sems, lands

    def small_finish(handle, after):
        subset, tag, send_sems, recv_sems, lands = handle
        _, lands = _split_wait(send_sems, recv_sems, None, lands, _gather_routes, after, f"small_grads_wait_{tag}")
        send_sems, recv_sems, _, lands, t = _split_start(None, lands, _pass_routes, 3, after, f"small_grads_pass_{tag}")
        _, lands = _split_wait(send_sems, recv_sems, None, lands, _pass_routes, t, f"small_grads_done_{tag}")
        parts_all = lands[0]
        rows = parts_all.shape[1]
        shapes = [small_grads[n].shape for n in subset]

        def full_layout(T):
            arrs = []
            for n, shp in zip(subset, shapes):
                if n in small_sharded:
                    ax = small_sharded[n]
                    arrs.append(lax.dynamic_update_slice_in_dim(jnp.zeros(shp, F32), T[n], dev * T[n].shape[ax], axis=ax))
                else:
                    arrs.append(T[n])
            return _pack(arrs).reshape(1, rows, LANE)

        res = _adamw(full_layout(W), full_layout(M), full_layout(V), parts_all, 0, None, f"adamw_small_{tag}")
        for key, flat in zip(("grad", "delta", "new_m", "new_v"), res):
            for n, a in zip(subset, _unpack(flat.reshape(-1), shapes)):
                if n in small_sharded:
                    ax = small_sharded[n]
                    a = lax.dynamic_slice_in_dim(a, dev * W[n].shape[ax], W[n].shape[ax], axis=ax)
                outs[key, n] = a
        return res[0]

    for idx in range(len(saved) - 1, -1, -1):
        kind, s = saved[idx]
        li = sum(1 for k, _ in saved[:idx] if k == kind)
        if kind == "mlp":
            def sq_relu_bwd(dsv, zv):
                return (dsv * 2.0 * jnp.maximum(zv, 0.0),)
            dz = _mm(dhb, s["w_down"], "nt", name=f"mlp{li}_dz", out_dtype=BF16, epi=sq_relu_bwd, epi_ins=(s["z"],), deps=carry)
            big_parts["mlp_w_down"][li] = _mm(s["s"], dhb, "tn", name=f"mlp{li}_dw_down", out_dtype=BF16).reshape(N_DEV, 4 * D // N_DEV, D)
            big_parts["mlp_w_up"][li] = _mm_tn_cbout(s["hn"], dz, N_DEV, name=f"mlp{li}_dw_up")
            pending, token = pair_grads([("mlp_w_down", li), ("mlp_w_up", li)])
            dhn = _mm_nt_cb(dz, s["w_up3"], name=f"mlp{li}_dhn")
            dh, dhb, dg = _rms_bwd(s["h"], dhn, dh, s["mn"], f"mlp{li}_dnorm", deps=(token,))
            stacks["mlp_norm"][li] = dg[0]
        elif kind == "gla":
            dyv = _mm(dhb, s["w_out"], "nt", name=f"gla{li}_dy", deps=carry)
            big_parts["gla_w_out"][li] = _mm(s["y"], dhb, "tn", name=f"gla{li}_dw_out", out_dtype=BF16).reshape(N_DEV, VW // N_DEV, D)
            dproj, don, dbg, dwgu = _gla_bwd(s["proj"], s["wgu"], s["bg"], s["on"], s["st"], dyv, H=H, DK=DK, DV=DV, name=f"gla{li}_dmix")
            dwgu = dwgu[:GLA_RANK]
            dw_in = _mm(s["hn"], dproj, "tn", name=f"gla{li}_dw_in", out_dtype=BF16, tn=896)[:, :IN]
            big_parts["gla_w_in"][li] = jnp.transpose(dw_in.reshape(D, N_DEV, IN // N_DEV), (1, 0, 2))
            pending, token = pair_grads([("gla_w_out", li), ("gla_w_in", li)])
            dhn = _mm(dproj, s["w_in"], "nt", name=f"gla{li}_dhn", tk=896)
            dh, dhb, dg = _rms_bwd(s["h"], dhn, dh, s["gn"], f"gla{li}_dnorm", deps=(token,))
            stacks["gla_norm"][li], stacks["gla_b_gate"][li], stacks["gla_o_norm"][li] = dg[0], dbg[0], don[0]
            stacks["gla_w_gate_up"][li] = dwgu
        else:
            def glu_bwd(zv, dhv, *_):
                a, sg = zv[:, :D], jax.nn.sigmoid(zv[:, D:])
                return (jnp.concatenate([dhv * sg, dhv * a * sg * (1.0 - sg)], axis=1),)
            dz = _rowmap(glu_bwd, [s["z"], dh], list(carry), [(2 * D, BF16)], name=f"s5{li}_dglu")[0]
            big_parts["s5_w_out"][li] = _mm_tn_cbout(s["yg"], dz, N_DEV, name=f"s5{li}_dw_out")
            dyg = _mm_nt_cb(dz, s["w_out3"], name=f"s5{li}_dyg")

            def act_bwd(dygv, ypv, uv, dv):
                _, vjp = jax.vjp(_gelu_tanh, ypv)
                dyp = vjp(dygv)[0]
                return dyp, dyp * dv, jnp.sum(dyp * uv, axis=0, keepdims=True)
            dyp, du, dd = _rowmap(act_bwd, [dyg, s["ypre"], s["u"]], [s["dsk"]], [(SW, F32), (SW, F32)], [SW], name=f"s5{li}_dact")
            bd = s["bd"]
            lr_, li_, du, dar, dai = _s5_scan(dyp, bd["w_re"], bd["w_im"], s["c124"], s["qr"], s["qi"], bd["du_re"], bd["du_im"], du,
                                              reverse=True, fwd_state=(s["xr"], s["xi"]), name=f"s5{li}_dscan")
            dbbr = _mm_tn_bd(s["u"], lr_, nb, S5_GROUP, N, name=f"s5{li}_dbb_re")
            dbbi = _mm_tn_bd(s["u"], li_, nb, S5_GROUP, N, name=f"s5{li}_dbb_im")
            dcr = _mm_tn_bd(dyp, s["xr"], nb, S5_GROUP, N, name=f"s5{li}_dc_re")
            dci = -_mm_tn_bd(dyp, s["xi"], nb, S5_GROUP, N, name=f"s5{li}_dc_im")
            dlre, dlim, dldt, dbrt, dbit = _s5_params_bwd(*s["prm"], dar.reshape(G, 1, N), dai.reshape(G, 1, N), dbbr, dbbi,
                                                         f"s5{li}_dparams")
            big_parts["s5_w_in"][li] = _mm(s["hn"], du, "tn", name=f"s5{li}_dw_in", out_dtype=BF16).reshape(N_DEV, D // N_DEV, SW)
            pending, token = pair_grads([("s5_w_out", li), ("s5_w_in", li)])
            dhn = _mm(du, s["w_in"], "nt", name=f"s5{li}_dhn")
            dh, dhb, dg = _rms_bwd(s["h"], dhn, dh, s["sn"], f"s5{li}_dnorm", deps=(token,))
            stacks["s5_norm"][li], stacks["s5_d"][li] = dg[0], dd[0]
            stacks["s5_lam_re"][li], stacks["s5_lam_im"][li], stacks["s5_log_dt"][li] = dlre[:, 0], dlim[:, 0], dldt[:, 0, 0]
            stacks["s5_b_re"][li], stacks["s5_b_im"][li] = jnp.swapaxes(dbrt, 1, 2), jnp.swapaxes(dbit, 1, 2)
            stacks["s5_c_re"][li], stacks["s5_c_im"][li] = dcr, dci
            if li == 0:
                s5_handle = small_start(s5_small, "s5")
        carry = (send_grads(pending, dh),)
    grad_x = dh[None]

    rest = small_start([n for n in small if n not in s5_small], "rest")

    stacked = {n: None for n in big}
    as3 = lambda a: a.reshape(a.shape[0], math.prod(a.shape[1:-1]), a.shape[-1])

    def finish(exchange, after):
        keys, tag, e_send, e_recv, e_srcs, e_lands = exchange
        _, e_lands = _split_wait(e_send, e_recv, e_srcs, e_lands, _chip_routes, after, f"grads_wait_{tag}")
        for (n, j), parts in zip(keys, e_lands):
            stacked[n] = _adamw(as3(W[n]), as3(M[n]), as3(V[n]), parts, j, stacked[n], f"adamw_{n}_{j}")
        return stacked[keys[0][0]][0]

    done = [small_finish(s5_handle, carry[0])]
    for exchange in exchanges[:-1]:
        finish(exchange, carry[0])
    done.append(small_finish(rest, carry[0]))
    last_keys = {n for n, _ in exchanges[-1][0]}
    finish(exchanges[-1], done + [stacked[n][0] for n in big if n not in last_keys])
    for n in big:
        for key, a in zip(("grad", "delta", "new_m", "new_v"), stacked[n]):
            outs[key, n] = a.reshape(W[n].shape)

    return (loss, grad_x, *[outs["grad", n] for n in names], *[outs["delta", n] for n in names],
            *[outs["new_m", n] for n in names], *[outs["new_v", n] for n in names])
```
